```python
import jax, jax.numpy as jnp
from jax import lax
import numpy as np

D_MODEL = 4096
BATCH = 8
SEQ = 4096
DEPTH = 1

CHUNK = 64
LEFT_CHUNKS = 8
BAND = LEFT_CHUNKS + 1
HEAD_DIM = 128
N_HEADS_A = 16
N_HEADS_B = 16
WIDTH_A = N_HEADS_A * HEAD_DIM
WIDTH_B = N_HEADS_B * HEAD_DIM
MAX_REL = 256
N_REL = 2 * MAX_REL + 1
Q_BLOCK = 128
N_BRANCHES = 2
D_FF = -(-(8 * D_MODEL) // (3 * 256)) * 256
D_IN = 3 * WIDTH_A + 3 * WIDTH_B + N_HEADS_B + N_BRANCHES * D_MODEL
RMS_EPS = 1e-6
NEG_INF = -1e30

kernel_name = "hybrid_chunked_relpos_fox_gated_block"


def rms_norm(x, g):
    xf = x.astype(jnp.float32)
    y = xf * lax.rsqrt(jnp.mean(xf * xf, axis=-1, keepdims=True) + RMS_EPS)
    return (y * g.astype(jnp.float32)).astype(x.dtype)


def chunked_relpos_attention(q, k, v, rel_bias):
    b, s, h, dh = q.shape
    n_chunks = s // CHUNK
    q = q.reshape(b, n_chunks, CHUNK, h, dh)
    pad = ((0, 0), (LEFT_CHUNKS, 0), (0, 0), (0, 0), (0, 0))
    kp = jnp.pad(k.reshape(b, n_chunks, CHUNK, h, dh), pad)
    vp = jnp.pad(v.reshape(b, n_chunks, CHUNK, h, dh), pad)
    band_idx = jnp.arange(n_chunks)[:, None] + jnp.arange(BAND)[None, :]
    k_band = kp[:, band_idx].reshape(b, n_chunks, BAND * CHUNK, h, dh)
    v_band = vp[:, band_idx].reshape(b, n_chunks, BAND * CHUNK, h, dh)
    valid = (band_idx - LEFT_CHUNKS) >= 0
    valid = jnp.repeat(valid, CHUNK, axis=1)
    a_pos = jnp.arange(CHUNK)[:, None, None]
    slot = jnp.arange(BAND)[None, :, None]
    b_pos = jnp.arange(CHUNK)[None, None, :]
    dist = ((LEFT_CHUNKS - slot) * CHUNK + a_pos - b_pos).reshape(CHUNK, BAND * CHUNK)
    rel_idx = jnp.clip(dist, -MAX_REL, MAX_REL) + MAX_REL
    bias = rel_bias.astype(jnp.float32)[:, rel_idx]
    scale = HEAD_DIM ** -0.5
    logits = jnp.einsum('bcqhd,bckhd->bhcqk', q, k_band,
                        preferred_element_type=jnp.float32) * scale
    logits = logits + bias[None, :, None, :, :]
    logits = jnp.where(valid[None, None, :, None, :], logits, NEG_INF)
    p = jax.nn.softmax(logits, axis=-1).astype(v.dtype)
    out = jnp.einsum('bhcqk,bckhd->bcqhd', p, v_band)
    return out.reshape(b, s, h * dh)


def forgetting_attention(q, k, v, log_f):
    b, s, h, dh = q.shape
    n_blocks = s // Q_BLOCK
    cum = jnp.cumsum(log_f, axis=1)
    cum_k = cum.transpose(0, 2, 1)
    q_blocks = q.reshape(b, n_blocks, Q_BLOCK, h, dh).transpose(1, 0, 2, 3, 4)
    c_blocks = cum.reshape(b, n_blocks, Q_BLOCK, h).transpose(1, 0, 3, 2)
    starts = jnp.arange(n_blocks) * Q_BLOCK
    k_pos = jnp.arange(s)
    scale = HEAD_DIM ** -0.5

    def one_block(args):
        q_i, c_i, start = args
        logits = jnp.einsum('bqhd,bkhd->bhqk', q_i, k,
                            preferred_element_type=jnp.float32) * scale
        logits = logits + c_i[..., :, None] - cum_k[:, :, None, :]
        q_pos = start + jnp.arange(Q_BLOCK)
        causal = k_pos[None, :] <= q_pos[:, None]
        logits = jnp.where(causal[None, None], logits, NEG_INF)
        p = jax.nn.softmax(logits, axis=-1).astype(v.dtype)
        return jnp.einsum('bhqk,bkhd->bqhd', p, v)

    out = lax.map(one_block, (q_blocks, c_blocks, starts))
    return out.transpose(1, 0, 2, 3, 4).reshape(b, s, h * dh)


def _fwd_setup_inputs(seed: int = 0) -> dict:
    key = jax.random.key(seed)
    ks = jax.random.split(key, 16)
    L = DEPTH

    def dense(k, shape, fan_in):
        return jax.random.normal(k, shape, jnp.float32) * fan_in ** -0.5

    def normal(k, shape):
        return jax.random.normal(k, shape, jnp.float32)

    return {
        "x": normal(ks[0], (BATCH, SEQ, D_MODEL)),
        "g_mix": 1.0 + 0.02 * normal(ks[1], (L, D_MODEL)),
        "w_in": dense(ks[2], (L, D_MODEL, D_IN), D_MODEL),
        "b_f": 2.0 + 0.5 * normal(ks[3], (L, N_HEADS_B)),
        "b_gate": 0.01 * normal(ks[4], (L, N_BRANCHES * D_MODEL)),
        "rel_bias": 0.1 * normal(ks[5], (L, N_HEADS_A, N_REL)),
        "w_branch_a": dense(ks[6], (L, WIDTH_A, D_MODEL), WIDTH_A),
        "w_branch_b": dense(ks[7], (L, WIDTH_B, D_MODEL), WIDTH_B),
        "w_out": dense(ks[8], (L, D_MODEL, D_MODEL), D_MODEL),
        "g_ffn": 1.0 + 0.02 * normal(ks[9], (L, D_MODEL)),
        "w_gate_ffn": dense(ks[10], (L, D_MODEL, D_FF), D_MODEL),
        "w_up_ffn": dense(ks[11], (L, D_MODEL, D_FF), D_MODEL),
        "w_down_ffn": dense(ks[12], (L, D_FF, D_MODEL), D_FF),
        "g_final": 1.0 + 0.02 * normal(ks[13], (D_MODEL,)),
    }


def _fwd_reference(x, g_mix, w_in, b_f, b_gate, rel_bias, w_branch_a, w_branch_b, w_out,
              g_ffn, w_gate_ffn, w_up_ffn, w_down_ffn, g_final):
    b, s, _ = x.shape
    sizes = [WIDTH_A, WIDTH_A, WIDTH_A, WIDTH_B, WIDTH_B, WIDTH_B, N_HEADS_B, D_MODEL, D_MODEL]
    split_at = [int(v) for v in np.cumsum(sizes)[:-1]]
    for l in range(DEPTH):
        h = rms_norm(x, g_mix[l])
        proj = jnp.einsum('bsd,de->bse', h, w_in[l])
        qa, ka, va, qb, kb, vb, f_logit, gate_a, gate_b = jnp.split(proj, split_at, axis=-1)
        heads_a = lambda t: t.reshape(b, s, N_HEADS_A, HEAD_DIM)
        heads_b = lambda t: t.reshape(b, s, N_HEADS_B, HEAD_DIM)
        o_a = chunked_relpos_attention(heads_a(qa), heads_a(ka), heads_a(va), rel_bias[l])
        log_f = jax.nn.log_sigmoid((f_logit + b_f[l]).astype(jnp.float32))
        o_b = forgetting_attention(heads_b(qb), heads_b(kb), heads_b(vb), log_f)
        u_a = jnp.einsum('bse,ed->bsd', o_a, w_branch_a[l])
        u_b = jnp.einsum('bse,ed->bsd', o_b, w_branch_b[l])
        merged = (jax.nn.sigmoid(gate_a + b_gate[l, :D_MODEL]) * u_a
                  + jax.nn.sigmoid(gate_b + b_gate[l, D_MODEL:]) * u_b)
        x = x + jnp.einsum('bsd,de->bse', merged, w_out[l])
        h2 = rms_norm(x, g_ffn[l])
        hidden = jax.nn.silu(jnp.einsum('bsd,df->bsf', h2, w_gate_ffn[l])) * \
            jnp.einsum('bsd,df->bsf', h2, w_up_ffn[l])
        x = x + jnp.einsum('bsf,fd->bsd', hidden, w_down_ffn[l])
    return rms_norm(x, g_final)


import jax as _jax
import jax.numpy as _jnp

TWIN_FORMAT = 'train_step'
FWD_PARAMS = ['x', 'g_mix', 'w_in', 'b_f', 'b_gate', 'rel_bias', 'w_branch_a', 'w_branch_b', 'w_out', 'g_ffn', 'w_gate_ffn', 'w_up_ffn', 'w_down_ffn', 'g_final']
TWIN_WEIGHTS = ['g_mix', 'w_in', 'b_f', 'b_gate', 'rel_bias', 'w_branch_a', 'w_branch_b', 'w_out', 'g_ffn', 'w_gate_ffn', 'w_up_ffn', 'w_down_ffn', 'g_final']
TWIN_DIFF_INPUT = 'x'
TWIN_INPUTS = ['x', 'g_mix', 'w_in', 'b_f', 'b_gate', 'rel_bias', 'w_branch_a', 'w_branch_b', 'w_out', 'g_ffn', 'w_gate_ffn', 'w_up_ffn', 'w_down_ffn', 'g_final', 'loss_target', 'm_g_mix', 'm_w_in', 'm_b_f', 'm_b_gate', 'm_rel_bias', 'm_w_branch_a', 'm_w_branch_b', 'm_w_out', 'm_g_ffn', 'm_w_gate_ffn', 'm_w_up_ffn', 'm_w_down_ffn', 'm_g_final', 'v_g_mix', 'v_w_in', 'v_b_f', 'v_b_gate', 'v_rel_bias', 'v_w_branch_a', 'v_w_branch_b', 'v_w_out', 'v_g_ffn', 'v_w_gate_ffn', 'v_w_up_ffn', 'v_w_down_ffn', 'v_g_final']
TWIN_OUTPUTS = ['loss', 'grad_x', 'grad_g_mix', 'grad_w_in', 'grad_b_f', 'grad_b_gate', 'grad_rel_bias', 'grad_w_branch_a', 'grad_w_branch_b', 'grad_w_out', 'grad_g_ffn', 'grad_w_gate_ffn', 'grad_w_up_ffn', 'grad_w_down_ffn', 'grad_g_final', 'delta_g_mix', 'delta_w_in', 'delta_b_f', 'delta_b_gate', 'delta_rel_bias', 'delta_w_branch_a', 'delta_w_branch_b', 'delta_w_out', 'delta_g_ffn', 'delta_w_gate_ffn', 'delta_w_up_ffn', 'delta_w_down_ffn', 'delta_g_final', 'new_m_g_mix', 'new_m_w_in', 'new_m_b_f', 'new_m_b_gate', 'new_m_rel_bias', 'new_m_w_branch_a', 'new_m_w_branch_b', 'new_m_w_out', 'new_m_g_ffn', 'new_m_w_gate_ffn', 'new_m_w_up_ffn', 'new_m_w_down_ffn', 'new_m_g_final', 'new_v_g_mix', 'new_v_w_in', 'new_v_b_f', 'new_v_b_gate', 'new_v_rel_bias', 'new_v_w_branch_a', 'new_v_w_branch_b', 'new_v_w_out', 'new_v_g_ffn', 'new_v_w_gate_ffn', 'new_v_w_up_ffn', 'new_v_w_down_ffn', 'new_v_g_final']
TWIN_LEAF_KINDS = {'loss': 'loss', 'grad_x': 'grad_x', 'grad_g_mix': 'grad_w', 'grad_w_in': 'grad_w', 'grad_b_f': 'grad_w', 'grad_b_gate': 'grad_w', 'grad_rel_bias': 'grad_w', 'grad_w_branch_a': 'grad_w', 'grad_w_branch_b': 'grad_w', 'grad_w_out': 'grad_w', 'grad_g_ffn': 'grad_w', 'grad_w_gate_ffn': 'grad_w', 'grad_w_up_ffn': 'grad_w', 'grad_w_down_ffn': 'grad_w', 'grad_g_final': 'grad_w', 'delta_g_mix': 'delta_w', 'delta_w_in': 'delta_w', 'delta_b_f': 'delta_w', 'delta_b_gate': 'delta_w', 'delta_rel_bias': 'delta_w', 'delta_w_branch_a': 'delta_w', 'delta_w_branch_b': 'delta_w', 'delta_w_out': 'delta_w', 'delta_g_ffn': 'delta_w', 'delta_w_gate_ffn': 'delta_w', 'delta_w_up_ffn': 'delta_w', 'delta_w_down_ffn': 'delta_w', 'delta_g_final': 'delta_w', 'new_m_g_mix': 'new_m', 'new_m_w_in': 'new_m', 'new_m_b_f': 'new_m', 'new_m_b_gate': 'new_m', 'new_m_rel_bias': 'new_m', 'new_m_w_branch_a': 'new_m', 'new_m_w_branch_b': 'new_m', 'new_m_w_out': 'new_m', 'new_m_g_ffn': 'new_m', 'new_m_w_gate_ffn': 'new_m', 'new_m_w_up_ffn': 'new_m', 'new_m_w_down_ffn': 'new_m', 'new_m_g_final': 'new_m', 'new_v_g_mix': 'new_v', 'new_v_w_in': 'new_v', 'new_v_b_f': 'new_v', 'new_v_b_gate': 'new_v', 'new_v_rel_bias': 'new_v', 'new_v_w_branch_a': 'new_v', 'new_v_w_branch_b': 'new_v', 'new_v_w_out': 'new_v', 'new_v_g_ffn': 'new_v', 'new_v_w_gate_ffn': 'new_v', 'new_v_w_up_ffn': 'new_v', 'new_v_w_down_ffn': 'new_v', 'new_v_g_final': 'new_v'}


def _forward(args):
    return _fwd_reference(*[args[k] for k in FWD_PARAMS])


def _output_shape():
    out = _jax.eval_shape(lambda: _forward(_fwd_setup_inputs(0)))
    return out.shape, out.dtype

N_MICROBATCH = 1
ADAM_LR = 0.001
ADAM_B1 = 0.9
ADAM_B2 = 0.999
ADAM_EPS = 1e-08
ADAM_WD = 0.01
ADAM_STEP = 10
PER_EXAMPLE_BATCH_AXIS = {'x': 0, 'loss_target': 0}
SHARED_INPUTS = []
_WEIGHT_DTYPES = {'g_mix': _jnp.float32, 'w_in': _jnp.float32, 'b_f': _jnp.float32, 'b_gate': _jnp.float32, 'rel_bias': _jnp.float32, 'w_branch_a': _jnp.float32, 'w_branch_b': _jnp.float32, 'w_out': _jnp.float32, 'g_ffn': _jnp.float32, 'w_gate_ffn': _jnp.float32, 'w_up_ffn': _jnp.float32, 'w_down_ffn': _jnp.float32, 'g_final': _jnp.float32}
MOMENT_SCALE = {'g_mix': 1.796372e-02, 'w_in': 8.030130e-03, 'b_f': 9.551153e-02, 'b_gate': 3.132667e-03, 'rel_bias': 2.540606e-03, 'w_branch_a': 3.822850e-03, 'w_branch_b': 1.108564e-02, 'w_out': 1.172271e-02, 'g_ffn': 3.254216e-02, 'w_gate_ffn': 1.407063e-02, 'w_up_ffn': 1.361213e-02, 'w_down_ffn': 2.235117e-02, 'g_final': 7.987207e+00}


def _to_microbatches(a, axis):
    t = _jnp.moveaxis(a, axis, 0)
    t = t.reshape((N_MICROBATCH, t.shape[0] // N_MICROBATCH) + t.shape[1:])
    return _jnp.moveaxis(t, 1, axis + 1)


def setup_inputs(seed: int = 0) -> dict:
    inp = _fwd_setup_inputs(seed)
    key = _jax.random.fold_in(_jax.random.key(seed), 7919)
    shape, _ = _output_shape()
    out = dict(inp)
    out["loss_target"] = _jax.random.normal(_jax.random.fold_in(key, 0), shape, _jnp.float32)
    for i, name in enumerate(TWIN_WEIGHTS):
        w = inp[name].astype(_jnp.float32)
        if MOMENT_SCALE is None:
            s = _jnp.sqrt(_jnp.mean(_jnp.square(w)) + 1e-30)
        else:
            s = MOMENT_SCALE[name]
        km, kv = _jax.random.split(_jax.random.fold_in(key, i + 1))
        out[name] = w
        out["m_" + name] = s * _jax.random.normal(km, w.shape, _jnp.float32)
        out["v_" + name] = (s * s) * _jax.random.uniform(kv, w.shape, _jnp.float32, 0.5, 1.5)
    if N_MICROBATCH > 1:
        for name, axis in PER_EXAMPLE_BATCH_AXIS.items():
            out[name] = _to_microbatches(out[name], axis)
    return {'x': out['x'], 'g_mix': out['g_mix'], 'w_in': out['w_in'], 'b_f': out['b_f'], 'b_gate': out['b_gate'], 'rel_bias': out['rel_bias'], 'w_branch_a': out['w_branch_a'], 'w_branch_b': out['w_branch_b'], 'w_out': out['w_out'], 'g_ffn': out['g_ffn'], 'w_gate_ffn': out['w_gate_ffn'], 'w_up_ffn': out['w_up_ffn'], 'w_down_ffn': out['w_down_ffn'], 'g_final': out['g_final'], 'loss_target': out['loss_target'], 'm_g_mix': out['m_g_mix'], 'm_w_in': out['m_w_in'], 'm_b_f': out['m_b_f'], 'm_b_gate': out['m_b_gate'], 'm_rel_bias': out['m_rel_bias'], 'm_w_branch_a': out['m_w_branch_a'], 'm_w_branch_b': out['m_w_branch_b'], 'm_w_out': out['m_w_out'], 'm_g_ffn': out['m_g_ffn'], 'm_w_gate_ffn': out['m_w_gate_ffn'], 'm_w_up_ffn': out['m_w_up_ffn'], 'm_w_down_ffn': out['m_w_down_ffn'], 'm_g_final': out['m_g_final'], 'v_g_mix': out['v_g_mix'], 'v_w_in': out['v_w_in'], 'v_b_f': out['v_b_f'], 'v_b_gate': out['v_b_gate'], 'v_rel_bias': out['v_rel_bias'], 'v_w_branch_a': out['v_w_branch_a'], 'v_w_branch_b': out['v_w_branch_b'], 'v_w_out': out['v_w_out'], 'v_g_ffn': out['v_g_ffn'], 'v_w_gate_ffn': out['v_w_gate_ffn'], 'v_w_up_ffn': out['v_w_up_ffn'], 'v_w_down_ffn': out['v_w_down_ffn'], 'v_g_final': out['v_g_final']}


def _loss(weights, diff, rest, loss_target):
    with _jax.named_scope("forward"):
        args = {**rest, TWIN_DIFF_INPUT: diff, **{k: w.astype(_WEIGHT_DTYPES[k]) for k, w in weights.items()}}
        y = _forward(args)
    with _jax.named_scope("loss_head"):
        err = _jnp.square(y.astype(_jnp.float32) - loss_target)
        return 0.5 * _jnp.sum(_jnp.mean(err, axis=-1)) if err.ndim else 0.5 * err


def _adamw(w, g, m, v):
    m = ADAM_B1 * m + (1.0 - ADAM_B1) * g
    v = ADAM_B2 * v + (1.0 - ADAM_B2) * _jnp.square(g)
    m_hat = m / (1.0 - ADAM_B1 ** ADAM_STEP)
    v_hat = v / (1.0 - ADAM_B2 ** ADAM_STEP)
    delta = -ADAM_LR * (m_hat / (_jnp.sqrt(v_hat) + ADAM_EPS) + ADAM_WD * w)
    return delta, m, v


def reference(x, g_mix, w_in, b_f, b_gate, rel_bias, w_branch_a, w_branch_b, w_out, g_ffn, w_gate_ffn, w_up_ffn, w_down_ffn, g_final, loss_target, m_g_mix, m_w_in, m_b_f, m_b_gate, m_rel_bias, m_w_branch_a, m_w_branch_b, m_w_out, m_g_ffn, m_w_gate_ffn, m_w_up_ffn, m_w_down_ffn, m_g_final, v_g_mix, v_w_in, v_b_f, v_b_gate, v_rel_bias, v_w_branch_a, v_w_branch_b, v_w_out, v_g_ffn, v_w_gate_ffn, v_w_up_ffn, v_w_down_ffn, v_g_final):
    given = dict(x=x, g_mix=g_mix, w_in=w_in, b_f=b_f, b_gate=b_gate, rel_bias=rel_bias, w_branch_a=w_branch_a, w_branch_b=w_branch_b, w_out=w_out, g_ffn=g_ffn, w_gate_ffn=w_gate_ffn, w_up_ffn=w_up_ffn, w_down_ffn=w_down_ffn, g_final=g_final, loss_target=loss_target, m_g_mix=m_g_mix, m_w_in=m_w_in, m_b_f=m_b_f, m_b_gate=m_b_gate, m_rel_bias=m_rel_bias, m_w_branch_a=m_w_branch_a, m_w_branch_b=m_w_branch_b, m_w_out=m_w_out, m_g_ffn=m_g_ffn, m_w_gate_ffn=m_w_gate_ffn, m_w_up_ffn=m_w_up_ffn, m_w_down_ffn=m_w_down_ffn, m_g_final=m_g_final, v_g_mix=v_g_mix, v_w_in=v_w_in, v_b_f=v_b_f, v_b_gate=v_b_gate, v_rel_bias=v_rel_bias, v_w_branch_a=v_w_branch_a, v_w_branch_b=v_w_branch_b, v_w_out=v_w_out, v_g_ffn=v_g_ffn, v_w_gate_ffn=v_w_gate_ffn, v_w_up_ffn=v_w_up_ffn, v_w_down_ffn=v_w_down_ffn, v_g_final=v_g_final)
    weights = {n: given[n] for n in TWIN_WEIGHTS}
    shared = {n: given[n] for n in SHARED_INPUTS}
    per_example = {n: given[n] for n in ['x']}
    grad_fn = _jax.value_and_grad(_loss, argnums=(0, 1))

    def one_microbatch(ex, loss_target):
        ex = dict(ex)
        diff = ex.pop(TWIN_DIFF_INPUT)
        return grad_fn(weights, diff, {**shared, **ex}, loss_target)

    if N_MICROBATCH == 1:
        loss, (grad_w, grad_x) = one_microbatch(per_example, given["loss_target"])
    else:
        def body(carry, xs):
            loss_sum, grad_sum = carry
            l_k, (gw_k, gx_k) = one_microbatch(xs[0], xs[1])
            with _jax.named_scope("update"):
                return (loss_sum + l_k, _jax.tree.map(_jnp.add, grad_sum, gw_k)), gx_k

        init = (_jnp.zeros((), _jnp.float32), _jax.tree.map(_jnp.zeros_like, weights))
        (loss, grad_w), grad_x = _jax.lax.scan(body, init, (per_example, given["loss_target"]))
    with _jax.named_scope("update"):
        delta_w, new_m, new_v = {}, {}, {}
        for n in TWIN_WEIGHTS:
            delta_w[n], new_m[n], new_v[n] = _adamw(weights[n], grad_w[n], given["m_" + n], given["v_" + n])
    return (loss, grad_x, *[grad_w[n] for n in TWIN_WEIGHTS], *[delta_w[n] for n in TWIN_WEIGHTS],
            *[new_m[n] for n in TWIN_WEIGHTS], *[new_v[n] for n in TWIN_WEIGHTS])
```

```python
import functools

import numpy as np
import jax
import jax.numpy as jnp
from jax import lax
from jax.experimental import pallas as pl
from jax.experimental.pallas import tpu as pltpu

F32 = jnp.float32
BF16 = jnp.bfloat16
LANE = 128
HEAD_DIM = 128
CHUNK = 64
LEFT_CHUNKS = 8
GROUP = 128
WIN_BLOCKS = 5
WIN = WIN_BLOCKS * GROUP
BAND = (LEFT_CHUNKS + 1) * CHUNK
RMS_EPS = 1e-6
NEG_INF = -1e30
ADAM_LR = 0.001
ADAM_B1 = 0.9
ADAM_B2 = 0.999
ADAM_EPS = 1e-08
ADAM_WD = 0.01
ADAM_STEP = 10
N_CHIPS = 4
MESH = pl.DeviceIdType.MESH
VMEM_LIMIT = 52 * 1024 * 1024
ANY = pl.BlockSpec(memory_space=pl.ANY)

NN = (((1,), (0,)), ((), ()))
NT = (((1,), (1,)), ((), ()))
TN = (((0,), (0,)), ((), ()))


def _cp(sem):
    return pltpu.CompilerParams(dimension_semantics=sem, vmem_limit_bytes=VMEM_LIMIT)


def _sds(shape, dtype):
    return jax.ShapeDtypeStruct(shape, dtype)


def _pick(n, prefs):
    for p in prefs:
        if n % p == 0:
            return p
    return n


def _sigmoid(v):
    return 1.0 / (1.0 + jnp.exp(-v))


def _split3(v):
    hi = v.astype(BF16)
    r1 = v - hi.astype(F32)
    mid = r1.astype(BF16)
    lo = (r1 - mid.astype(F32)).astype(BF16)
    return hi, mid, lo


def _col_of(blk, h):
    lane = lax.broadcasted_iota(jnp.int32, blk.shape, 1)
    return jnp.sum(jnp.where(lane == h, blk, 0.0), axis=1, keepdims=True)


def _put_col(ref, h, col):
    lane = lax.broadcasted_iota(jnp.int32, ref.shape, 1)
    ref[...] = jnp.where(lane == h, col, ref[...])


def _mm(name, mode, a_list, a_specs, b_list, b_specs, pairs, n_acc, grid, tm, tn,
        out_shapes, out_specs, epilogue, extra=(), extra_specs=()):
    n_a, n_b, n_e, n_o = len(a_list), len(b_list), len(extra), len(out_shapes)
    nk = grid[2]
    dn = {"nn": NN, "nt": NT, "tn": TN}[mode]

    def body(*refs):
        a_refs = refs[:n_a]
        b_refs = refs[n_a:n_a + n_b]
        e_refs = refs[n_a + n_b:n_a + n_b + n_e]
        o_refs = refs[n_a + n_b + n_e:n_a + n_b + n_e + n_o]
        acc_refs = refs[n_a + n_b + n_e + n_o:]
        k = pl.program_id(2)

        @pl.when(k == 0)
        def _():
            for acc in acc_refs:
                acc[...] = jnp.zeros_like(acc)

        for ai, bi, ci in pairs:
            acc_refs[ci][...] += lax.dot_general(a_refs[ai][...], b_refs[bi][...], dn,
                                                 preferred_element_type=F32)

        @pl.when(k == nk - 1)
        def _():
            epilogue([acc[...] for acc in acc_refs], e_refs, o_refs)

    return pl.pallas_call(
        body, name=name, grid=grid,
        in_specs=list(a_specs) + list(b_specs) + list(extra_specs),
        out_specs=list(out_specs), out_shape=list(out_shapes),
        scratch_shapes=[pltpu.VMEM((tm, tn), F32) for _ in range(n_acc)],
        compiler_params=_cp(("parallel", "parallel", "arbitrary")),
    )(*a_list, *b_list, *extra)


def _store(dtype):
    def ep(accs, e_refs, o_refs):
        o_refs[0][...] = accs[0].astype(dtype)
    return ep


def _mm_nn(name, a, b, out_dtype, *, b_col0=0, n=None, slot_w=None, tm=512, tn=None, tk=None,
           residual=None):
    M, K = a.shape
    if b.ndim == 3:
        Ns = b.shape[2]
        n = b.shape[0] * Ns
        tn = tn or _pick(Ns, (1408, 1024, 512, 256, 128))
        nps = Ns // tn
        b_spec = pl.BlockSpec((None, tk or _pick(K, (1024, 512, 256, 128)), tn),
                              lambda i, j, k: (j // nps, k, j % nps))
    else:
        n = n or b.shape[1]
        tn = tn or _pick(math_gcd(n, b_col0) if b_col0 else n, (2048, 1024, 512, 256, 128))
        assert b_col0 % tn == 0 and n % tn == 0
        c0 = b_col0 // tn
        b_spec = pl.BlockSpec((tk or _pick(K, (1024, 512, 256, 128)), tn), lambda i, j, k: (k, c0 + j))
    tk = tk or _pick(K, (1024, 512, 256, 128))
    tm = _pick(M, (tm, 256, 128))
    grid = (M // tm, n // tn, K // tk)
    a_spec = pl.BlockSpec((tm, tk), lambda i, j, k: (i, k))
    o_spec = pl.BlockSpec((tm, tn), lambda i, j, k: (i, j))
    if residual is None:
        return _mm(name, "nn", [a], [a_spec], [b], [b_spec], [(0, 0, 0)], 1, grid, tm, tn,
                   [_sds((M, n), out_dtype)], [o_spec], _store(out_dtype))[0]

    def ep(accs, e_refs, o_refs):
        o_refs[0][...] = (e_refs[0][...] + accs[0]).astype(out_dtype)
    return _mm(name, "nn", [a], [a_spec], [b], [b_spec], [(0, 0, 0)], 1, grid, tm, tn,
               [_sds((M, n), out_dtype)], [o_spec], ep, extra=[residual], extra_specs=[o_spec])[0]


def _mm_nt(name, a_list, b_list, out_dtype, *, k0_list=None, tm=512, tn=None, tk=None):
    M, K = a_list[0].shape
    b0 = b_list[0]
    N = b0.shape[1] if b0.ndim == 3 else b0.shape[0]
    tm = _pick(M, (tm, 256, 128))
    tn = tn or _pick(N, (1024, 512, 256, 128))
    if b0.ndim == 3:
        Ks = b0.shape[2]
        tk = tk or _pick(Ks, (1408, 1024, 512, 256, 128))
        kps = Ks // tk
        b_specs = [pl.BlockSpec((None, tn, tk), lambda i, j, k: (k // kps, j, k % kps)) for _ in b_list]
    else:
        tk = tk or _pick(K, (1024, 896, 512, 256, 128))
        k0_list = k0_list or [0] * len(b_list)
        b_specs = []
        for k0 in k0_list:
            assert k0 % tk == 0
            b_specs.append(pl.BlockSpec((tn, tk), functools.partial(lambda i, j, k, c: (j, c + k), c=k0 // tk)))
    grid = (M // tm, N // tn, K // tk)
    a_specs = [pl.BlockSpec((tm, tk), lambda i, j, k: (i, k)) for _ in a_list]
    o_spec = pl.BlockSpec((tm, tn), lambda i, j, k: (i, j))
    pairs = [(p, p, 0) for p in range(len(a_list))]
    return _mm(name, "nt", a_list, a_specs, b_list, b_specs, pairs, 1, grid, tm, tn,
               [_sds((M, N), out_dtype)], [o_spec], _store(out_dtype))[0]


def _mm_tn(name, a, b, out_dtype, *, slots=None, tm=None, tn=None, tk=512):
    Kc, Mo = a.shape
    No = b.shape[1]
    tm = tm or _pick(Mo, (1024, 704, 512, 256, 128))
    tk = _pick(Kc, (tk, 256, 128))
    if slots:
        Ns = No // slots
        tn = tn or _pick(Ns, (1408, 1024, 512, 256, 128))
        nps = Ns // tn
        o_spec = pl.BlockSpec((None, tm, tn), lambda i, j, k: (j // nps, i, j % nps))
        o_shape = _sds((slots, Mo, Ns), out_dtype)
    else:
        tn = tn or _pick(No, (1024, 512, 256, 128))
        o_spec = pl.BlockSpec((tm, tn), lambda i, j, k: (i, j))
        o_shape = _sds((Mo, No), out_dtype)
    grid = (Mo // tm, No // tn, Kc // tk)
    a_spec = pl.BlockSpec((tk, tm), lambda i, j, k: (k, i))
    b_spec = pl.BlockSpec((tk, tn), lambda i, j, k: (k, j))
    return _mm(name, "tn", [a], [a_spec], [b], [b_spec], [(0, 0, 0)], 1, grid, tm, tn,
               [o_shape], [o_spec], _store(out_dtype))[0]


def _cast_bf16(name, w):
    R, C = w.shape
    tr = _pick(R, (256, 128, 64, 32, 16))

    def body(w_ref, o_ref):
        o_ref[...] = w_ref[...].astype(BF16)

    return pl.pallas_call(
        body, name=name, grid=(R // tr,),
        in_specs=[pl.BlockSpec((tr, C), lambda i: (i, 0))],
        out_specs=pl.BlockSpec((tr, C), lambda i: (i, 0)),
        out_shape=_sds((R, C), BF16), compiler_params=_cp(("parallel",)),
    )(w)


def _rms_fwd(name, x, g):
    T, D = x.shape
    tr = _pick(T, (256, 128))

    def body(x_ref, g_ref, h_ref, r_ref):
        xv = x_ref[...]
        r = lax.rsqrt(jnp.mean(xv * xv, axis=1, keepdims=True) + RMS_EPS)
        h_ref[...] = (xv * r * g_ref[...]).astype(BF16)
        r_ref[...] = r

    row = pl.BlockSpec((tr, D), lambda i: (i, 0))
    return pl.pallas_call(
        body, name=name, grid=(T // tr,),
        in_specs=[row, pl.BlockSpec((1, D), lambda i: (0, 0))],
        out_specs=[row, pl.BlockSpec((tr, 1), lambda i: (i, 0))],
        out_shape=[_sds((T, D), BF16), _sds((T, 1), F32)], compiler_params=_cp(("parallel",)),
    )(x, g)


def _final_loss_bwd(name, x2, tgt, g):
    T, D = x2.shape
    tr = _pick(T, (256, 128))

    def body(x_ref, t_ref, g_ref, dx_ref, dxb_ref, loss_ref, gg_ref):
        @pl.when(pl.program_id(0) == 0)
        def _():
            loss_ref[...] = jnp.zeros_like(loss_ref)
            gg_ref[...] = jnp.zeros_like(gg_ref)

        xv = x_ref[...]
        gv = g_ref[...]
        r = lax.rsqrt(jnp.mean(xv * xv, axis=1, keepdims=True) + RMS_EPS)
        n = xv * r
        e = n * gv - t_ref[...]
        loss_ref[...] += 0.5 * jnp.sum(jnp.mean(e * e, axis=1, keepdims=True), axis=0, keepdims=True)
        dy = e * (1.0 / D)
        gg_ref[...] += jnp.sum(dy * n, axis=0, keepdims=True)
        gy = dy * gv
        dx = r * (gy - n * jnp.mean(gy * n, axis=1, keepdims=True))
        dx_ref[...] = dx
        dxb_ref[...] = dx.astype(BF16)

    row = pl.BlockSpec((tr, D), lambda i: (i, 0))
    vec = pl.BlockSpec((1, D), lambda i: (0, 0))
    return pl.pallas_call(
        body, name=name, grid=(T // tr,),
        in_specs=[row, row, vec],
        out_specs=[row, row, pl.BlockSpec((1, LANE), lambda i: (0, 0)), vec],
        out_shape=[_sds((T, D), F32), _sds((T, D), BF16), _sds((1, LANE), F32), _sds((1, D), F32)],
        compiler_params=_cp(("arbitrary",)),
    )(x2, tgt, g)


def _rms_bwd(name, dh_list, x, r, g, dres, want_bf16):
    T, D = x.shape
    tr = _pick(T, (128,))
    n_dh = len(dh_list)

    def body(*refs):
        dh_refs = refs[:n_dh]
        x_ref, r_ref, g_ref, dres_ref = refs[n_dh:n_dh + 4]
        outs = refs[n_dh + 4:]
        gg_ref = outs[-1]

        @pl.when(pl.program_id(0) == 0)
        def _():
            gg_ref[...] = jnp.zeros_like(gg_ref)

        dh = dh_refs[0][...]
        for ref in dh_refs[1:]:
            dh = dh + ref[...]
        rv = r_ref[...]
        n = x_ref[...] * rv
        gg_ref[...] += jnp.sum(dh * n, axis=0, keepdims=True)
        gy = dh * g_ref[...]
        dx = dres_ref[...] + rv * (gy - n * jnp.mean(gy * n, axis=1, keepdims=True))
        outs[0][...] = dx
        if want_bf16:
            outs[1][...] = dx.astype(BF16)

    row = pl.BlockSpec((tr, D), lambda i: (i, 0))
    vec = pl.BlockSpec((1, D), lambda i: (0, 0))
    out_specs = [row] + ([row] if want_bf16 else []) + [vec]
    out_shape = [_sds((T, D), F32)] + ([_sds((T, D), BF16)] if want_bf16 else []) + [_sds((1, D), F32)]
    return pl.pallas_call(
        body, name=name, grid=(T // tr,),
        in_specs=[row] * n_dh + [row, pl.BlockSpec((tr, 1), lambda i: (i, 0)), vec, row],
        out_specs=out_specs, out_shape=out_shape, compiler_params=_cp(("arbitrary",)),
    )(*dh_list, x, r, g, dres)


def _merge_fwd(name, gates, u_a, u_b, b_gate):
    T, D = u_a.shape
    tr = _pick(T, (256, 128))

    def body(ga_ref, gb_ref, ua_ref, ub_ref, ba_ref, bb_ref, o_ref):
        sa = _sigmoid(ga_ref[...] + ba_ref[...])
        sb = _sigmoid(gb_ref[...] + bb_ref[...])
        o_ref[...] = (sa * ua_ref[...] + sb * ub_ref[...]).astype(BF16)

    row = pl.BlockSpec((tr, D), lambda i: (i, 0))
    row1 = pl.BlockSpec((tr, D), lambda i: (i, 1))
    v0 = pl.BlockSpec((1, D), lambda i: (0, 0))
    v1 = pl.BlockSpec((1, D), lambda i: (0, 1))
    return pl.pallas_call(
        body, name=name, grid=(T // tr,),
        in_specs=[row, row1, row, row, v0, v1], out_specs=row,
        out_shape=_sds((T, D), BF16), compiler_params=_cp(("parallel",)),
    )(gates, gates, u_a, u_b, b_gate, b_gate)


def _merge_bwd(name, dm, gates, u_a, u_b, b_gate):
    T, D = u_a.shape
    tr = _pick(T, (128,))

    def body(dm_ref, ga_ref, gb_ref, ua_ref, ub_ref, ba_ref, bb_ref, dua_ref, dub_ref, dga_ref, dgb_ref,
             gba_ref, gbb_ref):
        @pl.when(pl.program_id(0) == 0)
        def _():
            gba_ref[...] = jnp.zeros_like(gba_ref)
            gbb_ref[...] = jnp.zeros_like(gbb_ref)

        d = dm_ref[...]
        sa = _sigmoid(ga_ref[...] + ba_ref[...])
        sb = _sigmoid(gb_ref[...] + bb_ref[...])
        dua_ref[...] = (d * sa).astype(BF16)
        dub_ref[...] = (d * sb).astype(BF16)
        dga = d * ua_ref[...] * sa * (1.0 - sa)
        dgb = d * ub_ref[...] * sb * (1.0 - sb)
        dga_ref[...] = dga.astype(BF16)
        dgb_ref[...] = dgb.astype(BF16)
        gba_ref[...] += jnp.sum(dga, axis=0, keepdims=True)
        gbb_ref[...] += jnp.sum(dgb, axis=0, keepdims=True)

    row = pl.BlockSpec((tr, D), lambda i: (i, 0))
    row1 = pl.BlockSpec((tr, D), lambda i: (i, 1))
    v0 = pl.BlockSpec((1, D), lambda i: (0, 0))
    v1 = pl.BlockSpec((1, D), lambda i: (0, 1))
    outs = pl.pallas_call(
        body, name=name, grid=(T // tr,),
        in_specs=[row, row, row1, row, row, v0, v1],
        out_specs=[row, row, row, row, v0, v0],
        out_shape=[_sds((T, D), BF16), _sds((T, D), BF16), _sds((T, D), BF16), _sds((T, D), BF16),
                   _sds((1, D), F32), _sds((1, D), F32)],
        compiler_params=_cp(("arbitrary",)),
    )(dm, gates, gates, u_a, u_b, b_gate, b_gate)
    return outs


def _adamw(name, w, g, m, v):
    R, C = w.shape
    tr = _pick(R, (64, 32, 16, 8))
    c1 = 1.0 - ADAM_B1 ** ADAM_STEP
    c2 = 1.0 - ADAM_B2 ** ADAM_STEP

    def body(w_ref, g_ref, m_ref, v_ref, d_ref, mo_ref, vo_ref):
        gv = g_ref[...]
        mn = ADAM_B1 * m_ref[...] + (1.0 - ADAM_B1) * gv
        vn = ADAM_B2 * v_ref[...] + (1.0 - ADAM_B2) * (gv * gv)
        d_ref[...] = -ADAM_LR * ((mn / c1) / (jnp.sqrt(vn / c2) + ADAM_EPS) + ADAM_WD * w_ref[...])
        mo_ref[...] = mn
        vo_ref[...] = vn

    blk = pl.BlockSpec((tr, C), lambda i: (i, 0))
    return pl.pallas_call(
        body, name=name, grid=(R // tr,),
        in_specs=[blk] * 4, out_specs=[blk] * 3,
        out_shape=[_sds((R, C), F32)] * 3, compiler_params=_cp(("parallel",)),
    )(w, g, m, v)


def _add_bf16(name, a, a_row0, b):
    S, h, C = b.shape
    tr = _pick(h, (256, 128, 64, 32, 16))
    nb = h // tr

    def body(off_ref, a_ref, b_ref, o_ref):
        o_ref[...] = (a_ref[...].astype(F32) + b_ref[...].astype(F32)).astype(BF16)

    gs = pltpu.PrefetchScalarGridSpec(
        num_scalar_prefetch=1, grid=(S, nb),
        in_specs=[pl.BlockSpec((None, tr, C), lambda s, i, off: (s, off[0] * nb + i, 0)),
                  pl.BlockSpec((None, tr, C), lambda s, i, off: (s, i, 0))],
        out_specs=pl.BlockSpec((None, tr, C), lambda s, i, off: (s, i, 0)))
    return pl.pallas_call(body, name=name, grid_spec=gs, out_shape=_sds((S, h, C), BF16),
                          compiler_params=_cp(("parallel", "parallel")))(
        jnp.reshape(a_row0, (1,)).astype(jnp.int32), a, b)


def _sum4(name, parts):
    S, h, C = parts.shape
    tr = _pick(h, (256, 128, 64, 32, 16))

    def body(p_ref, o_ref):
        acc = p_ref[0].astype(F32)
        for s in range(1, S):
            acc = acc + p_ref[s].astype(F32)
        o_ref[...] = acc

    return pl.pallas_call(
        body, name=name, grid=(h // tr,),
        in_specs=[pl.BlockSpec((S, tr, C), lambda i: (0, i, 0))],
        out_specs=pl.BlockSpec((tr, C), lambda i: (i, 0)),
        out_shape=_sds((h, C), F32), compiler_params=_cp(("parallel",)),
    )(parts)


def _place():
    x, y, c = lax.axis_index("x"), lax.axis_index("y"), lax.axis_index("c")
    chips = [(1 - x, y), (x, 1 - y), (1 - x, 1 - y)]
    return x, y, c, chips


def _allgather(name, shards):
    n = len(shards)

    def body(*refs):
        src_refs = refs[:n]
        out_refs = refs[n:2 * n]
        send_sems, recv_sems, local_sems = refs[2 * n:]
        x, y, c, chips = _place()
        k = 2 * x + y
        sibling = (x, y, 1 - c)
        locals_, firsts, passed = [], [], []
        for a in range(n):
            R = src_refs[a].shape[0]
            h = R // 2
            out = out_refs[a]
            mine = pltpu.make_async_copy(src_refs[a], out.at[k], local_sems.at[a])
            mine.start()
            locals_.append(mine)
            for j, (cx, cy) in enumerate(chips):
                cp = pltpu.make_async_remote_copy(
                    src_ref=src_refs[a].at[pl.ds(c * h, h), :], dst_ref=out.at[k, pl.ds(c * h, h), :],
                    send_sem=send_sems.at[6 * a + j], recv_sem=recv_sems.at[6 * a + j],
                    device_id=(cx, cy, c), device_id_type=MESH)
                cp.start()
                firsts.append(cp)
        for a in range(n):
            h = src_refs[a].shape[0] // 2
            out = out_refs[a]
            for j, (cx, cy) in enumerate(chips):
                kj = 2 * cx + cy
                rows = out.at[kj, pl.ds(c * h, h), :]
                pltpu.make_async_remote_copy(
                    src_ref=rows, dst_ref=rows, send_sem=send_sems.at[6 * a + j], recv_sem=recv_sems.at[6 * a + j],
                    device_id=(cx, cy, c), device_id_type=MESH).wait_recv()
                fw = pltpu.make_async_remote_copy(
                    src_ref=rows, dst_ref=rows, send_sem=send_sems.at[6 * a + 3 + j],
                    recv_sem=recv_sems.at[6 * a + 3 + j], device_id=sibling, device_id_type=MESH)
                fw.start()
                passed.append(fw)
        for a in range(n):
            h = src_refs[a].shape[0] // 2
            out = out_refs[a]
            for j, (cx, cy) in enumerate(chips):
                kj = 2 * cx + cy
                rows = out.at[kj, pl.ds((1 - c) * h, h), :]
                pltpu.make_async_remote_copy(
                    src_ref=rows, dst_ref=rows, send_sem=send_sems.at[6 * a + 3 + j],
                    recv_sem=recv_sems.at[6 * a + 3 + j], device_id=sibling, device_id_type=MESH).wait_recv()
        for cp in firsts + passed:
            cp.wait_send()
        for cp in locals_:
            cp.wait()

    return pl.pallas_call(
        body, name=name,
        in_specs=[ANY] * n, out_specs=[ANY] * n,
        out_shape=[_sds((N_CHIPS,) + s.shape, s.dtype) for s in shards],
        scratch_shapes=[pltpu.SemaphoreType.DMA((6 * n,)), pltpu.SemaphoreType.DMA((6 * n,)),
                        pltpu.SemaphoreType.DMA((n,))],
    )(*shards)


def _sibling_halves(name, grads):
    n = len(grads)

    def body(*refs):
        g_refs = refs[:n]
        out_refs = refs[n:2 * n]
        send_sems, recv_sems = refs[2 * n:]
        x, y, c, _ = _place()
        sibling = (x, y, 1 - c)
        cps = []
        for a in range(n):
            h = g_refs[a].shape[1] // 2
            cp = pltpu.make_async_remote_copy(
                src_ref=g_refs[a].at[:, pl.ds((1 - c) * h, h), :], dst_ref=out_refs[a],
                send_sem=send_sems.at[a], recv_sem=recv_sems.at[a], device_id=sibling, device_id_type=MESH)
            cp.start()
            cps.append(cp)
        for cp in cps:
            cp.wait()

    return pl.pallas_call(
        body, name=name, in_specs=[ANY] * n, out_specs=[ANY] * n,
        out_shape=[_sds((g.shape[0], g.shape[1] // 2, g.shape[2]), g.dtype) for g in grads],
        scratch_shapes=[pltpu.SemaphoreType.DMA((n,)), pltpu.SemaphoreType.DMA((n,))],
    )(*grads)


def _scatter_chips(name, parts):
    n = len(parts)

    def body(*refs):
        p_refs = refs[:n]
        out_refs = refs[n:2 * n]
        send_sems, recv_sems, local_sems = refs[2 * n:]
        x, y, c, chips = _place()
        k = 2 * x + y
        cps, locals_ = [], []
        for a in range(n):
            mine = pltpu.make_async_copy(p_refs[a].at[k], out_refs[a].at[k], local_sems.at[a])
            mine.start()
            locals_.append(mine)
            for j, (cx, cy) in enumerate(chips):
                kj = 2 * cx + cy
                cp = pltpu.make_async_remote_copy(
                    src_ref=p_refs[a].at[kj], dst_ref=out_refs[a].at[k],
                    send_sem=send_sems.at[3 * a + j], recv_sem=recv_sems.at[3 * a + j],
                    device_id=(cx, cy, c), device_id_type=MESH)
                cp.start()
                cps.append(cp)
        for a in range(n):
            for j, (cx, cy) in enumerate(chips):
                kj = 2 * cx + cy
                pltpu.make_async_remote_copy(
                    src_ref=p_refs[a].at[kj], dst_ref=out_refs[a].at[kj],
                    send_sem=send_sems.at[3 * a + j], recv_sem=recv_sems.at[3 * a + j],
                    device_id=(cx, cy, c), device_id_type=MESH).wait_recv()
        for cp in cps:
            cp.wait_send()
        for cp in locals_:
            cp.wait()

    return pl.pallas_call(
        body, name=name, in_specs=[ANY] * n, out_specs=[ANY] * n,
        out_shape=[_sds(p.shape, p.dtype) for p in parts],
        scratch_shapes=[pltpu.SemaphoreType.DMA((3 * n,)), pltpu.SemaphoreType.DMA((3 * n,)),
                        pltpu.SemaphoreType.DMA((n,))],
    )(*parts)


def _swap_halves(name, halves):
    n = len(halves)

    def body(*refs):
        s_refs = refs[:n]
        out_refs = refs[n:2 * n]
        send_sems, recv_sems, local_sems = refs[2 * n:]
        x, y, c, _ = _place()
        sibling = (x, y, 1 - c)
        cps = []
        for a in range(n):
            h = s_refs[a].shape[0]
            mine = pltpu.make_async_copy(s_refs[a], out_refs[a].at[pl.ds(c * h, h), :], local_sems.at[a])
            mine.start()
            cps.append(mine)
            cp = pltpu.make_async_remote_copy(
                src_ref=s_refs[a], dst_ref=out_refs[a].at[pl.ds(c * h, h), :],
                send_sem=send_sems.at[a], recv_sem=recv_sems.at[a], device_id=sibling, device_id_type=MESH)
            cp.start()
            cps.append(cp)
        for a in range(n):
            h = s_refs[a].shape[0]
            pltpu.make_async_remote_copy(
                src_ref=s_refs[a], dst_ref=out_refs[a].at[pl.ds((1 - c) * h, h), :],
                send_sem=send_sems.at[a], recv_sem=recv_sems.at[a], device_id=sibling, device_id_type=MESH).wait_recv()
        for a in range(n):
            cps[2 * a + 1].wait_send()
            cps[2 * a].wait()

    return pl.pallas_call(
        body, name=name, in_specs=[ANY] * n, out_specs=[ANY] * n,
        out_shape=[_sds((2 * s.shape[0], s.shape[1]), s.dtype) for s in halves],
        scratch_shapes=[pltpu.SemaphoreType.DMA((n,)), pltpu.SemaphoreType.DMA((n,)),
                        pltpu.SemaphoreType.DMA((n,))],
    )(*halves)


def _small_allreduce(name, v):
    m_per, n = v.shape

    def body(x_ref, sum_ref, all_ref, send_sems, recv_sems, local_sem):
        x, y, c, chips = _place()
        me, sibling = (x, y, c), (x, y, 1 - c)

        def rows(px, py, pc):
            return all_ref.at[pl.ds((4 * px + 2 * py + pc) * m_per, m_per), :]

        def copy(kk, block, to, src=None):
            return pltpu.make_async_remote_copy(
                src_ref=rows(*block) if src is None else src, dst_ref=rows(*block),
                send_sem=send_sems.at[kk], recv_sem=recv_sems.at[kk], device_id=to, device_id_type=MESH)

        mine = pltpu.make_async_copy(x_ref, rows(*me), local_sem)
        mine.start()
        first = [copy(0, me, sibling, src=x_ref)]
        first += [copy(1 + j, me, (*chip, c), src=x_ref) for j, chip in enumerate(chips)]
        for cp in first:
            cp.start()
        passed = [copy(4 + j, (*chip, c), sibling) for j, chip in enumerate(chips)]
        for j, chip in enumerate(chips):
            copy(1 + j, (*chip, c), me).wait_recv()
            passed[j].start()
        copy(0, sibling, me).wait_recv()
        for j, chip in enumerate(chips):
            copy(4 + j, (*chip, 1 - c), me).wait_recv()
        for cp in first + passed:
            cp.wait_send()
        mine.wait()
        acc = all_ref[pl.ds(0, m_per), :]
        for d in range(1, 8):
            acc = acc + all_ref[pl.ds(d * m_per, m_per), :]
        sum_ref[...] = acc

    vm = pl.BlockSpec(memory_space=pltpu.VMEM)
    return pl.pallas_call(
        body, name=name, in_specs=[vm], out_specs=[vm, vm],
        out_shape=[_sds((m_per, n), F32), _sds((8 * m_per, n), F32)],
        scratch_shapes=[pltpu.SemaphoreType.DMA((7,)), pltpu.SemaphoreType.DMA((7,)), pltpu.SemaphoreType.DMA],
    )(v)[0]


def _in_layout(D, W6, nhb, Ls):
    nmain = W6 + 2 * D
    lay = []
    for k in range(N_CHIPS):
        g0, g1 = k * Ls, (k + 1) * Ls
        pieces = []
        a, b = max(g0, 0), min(g1, W6)
        if a < b:
            pieces.append((a - g0, b - g0, a))
        a, b = max(g0, W6 + nhb), min(g1, W6 + nhb + 2 * D)
        if a < b:
            pieces.append((a - g0, b - g0, a - nhb))
        a, b = max(g0, W6), min(g1, W6 + nhb)
        fpiece = (a - g0, b - g0, a - W6) if a < b else None
        assert fpiece is None or (b - a) == nhb
        main0 = min(p[2] for p in pieces)
        main1 = max(p[2] + p[1] - p[0] for p in pieces)
        lay.append(dict(pieces=pieces, f=fpiece, s=main0 // LANE, e=-(-main1 // LANE), main1=main1))
    assert sum(1 for l in lay if l["f"] is not None) == 1
    nbw = max(l["e"] - l["s"] + (1 if l["f"] else 0) for l in lay)
    for k in range(1, N_CHIPS):
        assert lay[k]["s"] >= lay[k - 1]["e"] - 1 and lay[k]["s"] > lay[k - 1]["s"]
    return lay, nbw, nmain


def _to_window(w, lay_k, nbw):
    D = w.shape[0]
    items = [(c0 - lay_k["s"] * LANE, l0, l1) for (l0, l1, c0) in lay_k["pieces"]]
    if lay_k["f"]:
        l0, l1, off = lay_k["f"]
        items.append(((lay_k["e"] - lay_k["s"]) * LANE + off, l0, l1))
    items.sort()
    cols, pos = [], 0
    for w0, l0, l1 in items:
        if w0 > pos:
            cols.append(jnp.zeros((D, w0 - pos), w.dtype))
        cols.append(w[:, l0:l1])
        pos = w0 + (l1 - l0)
    if pos < nbw * LANE:
        cols.append(jnp.zeros((D, nbw * LANE - pos), w.dtype))
    return jnp.concatenate(cols, axis=1)


def _from_window(win, lay_k):
    items = [(l0, c0 - lay_k["s"] * LANE, l1 - l0) for (l0, l1, c0) in lay_k["pieces"]]
    if lay_k["f"]:
        l0, l1, off = lay_k["f"]
        items.append((l0, (lay_k["e"] - lay_k["s"]) * LANE + off, l1 - l0))
    items.sort()
    return jnp.concatenate([win[:, w0:w0 + n] for (_, w0, n) in items], axis=1)


def _assemble_in(name, wins, lay, nbw, nmain):
    _, D, _ = wins.shape
    ncb = nmain // LANE + 1
    k1 = np.zeros(ncb, np.int32)
    i1 = np.zeros(ncb, np.int32)
    k2 = np.zeros(ncb, np.int32)
    i2 = np.zeros(ncb, np.int32)
    fl = np.zeros(ncb, np.int32)
    for b in range(ncb - 1):
        k = max(kk for kk in range(N_CHIPS) if lay[kk]["s"] <= b)
        k1[b], i1[b] = k, b - lay[k]["s"]
        if k >= 1 and b == lay[k]["s"] and lay[k - 1]["main1"] > b * LANE:
            k2[b], i2[b], fl[b] = k - 1, b - lay[k - 1]["s"], 1
    kf = [kk for kk in range(N_CHIPS) if lay[kk]["f"]][0]
    k1[ncb - 1], i1[ncb - 1] = kf, lay[kf]["e"] - lay[kf]["s"]

    def body(k1_ref, i1_ref, k2_ref, i2_ref, fl_ref, a_ref, b_ref, o_ref):
        b = pl.program_id(0)
        add = jnp.where(fl_ref[b] == 1, b_ref[...], jnp.zeros_like(b_ref))
        o_ref[...] = a_ref[...] + add

    gs = pltpu.PrefetchScalarGridSpec(
        num_scalar_prefetch=5, grid=(ncb,),
        in_specs=[pl.BlockSpec((None, D, LANE), lambda b, k1r, i1r, k2r, i2r, flr: (k1r[b], 0, i1r[b])),
                  pl.BlockSpec((None, D, LANE), lambda b, k1r, i1r, k2r, i2r, flr: (k2r[b], 0, i2r[b]))],
        out_specs=pl.BlockSpec((D, LANE), lambda b, k1r, i1r, k2r, i2r, flr: (0, b)))
    return pl.pallas_call(body, name=name, grid_spec=gs, out_shape=_sds((D, ncb * LANE), BF16),
                          compiler_params=_cp(("parallel",)))(
        jnp.asarray(k1), jnp.asarray(i1), jnp.asarray(k2), jnp.asarray(i2), jnp.asarray(fl), wins, wins)


def _a_specs_q(nh):
    q = pl.BlockSpec((GROUP, HEAD_DIM), lambda i, h: (i, h))
    ks = [pl.BlockSpec((GROUP, HEAD_DIM), functools.partial(
        lambda i, h, j: (jnp.maximum(i - (WIN_BLOCKS - 1) + j, 0), nh + h), j=j)) for j in range(WIN_BLOCKS)]
    vs = [pl.BlockSpec((GROUP, HEAD_DIM), functools.partial(
        lambda i, h, j: (jnp.maximum(i - (WIN_BLOCKS - 1) + j, 0), 2 * nh + h), j=j)) for j in range(WIN_BLOCKS)]
    return q, ks, vs


def _a_logits(q, k_refs, bias, i, scale):
    parts = [lax.dot_general(q, kr[...], NT, preferred_element_type=F32) for kr in k_refs]
    s = jnp.concatenate(parts, axis=1) * scale + bias
    col = lax.broadcasted_iota(jnp.int32, s.shape, 1)
    return jnp.where(col >= (WIN_BLOCKS - 1 - i) * GROUP, s, NEG_INF)


def _attn_a_fwd(name, qkv, bias2, nh):
    T = qkv.shape[0]
    ng = T // GROUP
    scale = HEAD_DIM ** -0.5

    def body(q_ref, *refs):
        k_refs = refs[:WIN_BLOCKS]
        v_refs = refs[WIN_BLOCKS:2 * WIN_BLOCKS]
        bias_ref, o_ref, lse_ref = refs[2 * WIN_BLOCKS:]
        i, h = pl.program_id(0), pl.program_id(1)

        @pl.when(h == 0)
        def _():
            lse_ref[...] = jnp.zeros_like(lse_ref)

        s = _a_logits(q_ref[...], k_refs, bias_ref[h], i, scale)
        m = jnp.max(s, axis=1, keepdims=True)
        p = jnp.exp(s - m)
        l = jnp.sum(p, axis=1, keepdims=True)
        pb = (p / l).astype(BF16)
        o = jnp.zeros((GROUP, HEAD_DIM), F32)
        for j in range(WIN_BLOCKS):
            o = o + jnp.dot(pb[:, j * GROUP:(j + 1) * GROUP], v_refs[j][...], preferred_element_type=F32)
        o_ref[...] = o.astype(BF16)
        _put_col(lse_ref, h, m + jnp.log(l))

    q_spec, k_specs, v_specs = _a_specs_q(nh)
    stat = pl.BlockSpec((GROUP, LANE), lambda i, h: (i, 0))
    return pl.pallas_call(
        body, name=name, grid=(ng, nh),
        in_specs=[q_spec] + k_specs + v_specs + [pl.BlockSpec((nh, GROUP, WIN), lambda i, h: (0, 0, 0))],
        out_specs=[pl.BlockSpec((GROUP, HEAD_DIM), lambda i, h: (i, h)), stat],
        out_shape=[_sds((T, nh * HEAD_DIM), BF16), _sds((T, LANE), F32)],
        compiler_params=_cp(("parallel", "arbitrary")),
    )(qkv, *([qkv] * (2 * WIN_BLOCKS)), bias2)


def _attn_a_dq(name, qkv, do, lse, bias2, nh):
    T = qkv.shape[0]
    ng = T // GROUP
    scale = HEAD_DIM ** -0.5

    def body(q_ref, *refs):
        k_refs = refs[:WIN_BLOCKS]
        v_refs = refs[WIN_BLOCKS:2 * WIN_BLOCKS]
        do_ref, lse_ref, bias_ref, dq_ref, delta_ref, db_ref = refs[2 * WIN_BLOCKS:]
        i, h = pl.program_id(0), pl.program_id(1)

        @pl.when(h == 0)
        def _():
            delta_ref[...] = jnp.zeros_like(delta_ref)

        @pl.when(i == 0)
        def _():
            db_ref[h] = jnp.zeros((GROUP, WIN), F32)

        s = _a_logits(q_ref[...], k_refs, bias_ref[h], i, scale)
        p = jnp.exp(s - _col_of(lse_ref[...], h))
        dov = do_ref[...]
        dp = jnp.concatenate([lax.dot_general(dov, vr[...], NT, preferred_element_type=F32) for vr in v_refs], axis=1)
        delta = jnp.sum(p * dp, axis=1, keepdims=True)
        ds = p * (dp - delta)
        db_ref[h] += ds
        dsb = ds.astype(BF16)
        dq = jnp.zeros((GROUP, HEAD_DIM), F32)
        for j in range(WIN_BLOCKS):
            dq = dq + jnp.dot(dsb[:, j * GROUP:(j + 1) * GROUP], k_refs[j][...], preferred_element_type=F32)
        dq_ref[...] = (dq * scale).astype(BF16)
        _put_col(delta_ref, h, delta)

    q_spec, k_specs, v_specs = _a_specs_q(nh)
    stat = pl.BlockSpec((GROUP, LANE), lambda i, h: (i, 0))
    full_b = pl.BlockSpec((nh, GROUP, WIN), lambda i, h: (0, 0, 0))
    return pl.pallas_call(
        body, name=name, grid=(ng, nh),
        in_specs=[q_spec] + k_specs + v_specs + [pl.BlockSpec((GROUP, HEAD_DIM), lambda i, h: (i, h)), stat, full_b],
        out_specs=[pl.BlockSpec((GROUP, HEAD_DIM), lambda i, h: (i, h)), stat, full_b],
        out_shape=[_sds((T, nh * HEAD_DIM), BF16), _sds((T, LANE), F32), _sds((nh, GROUP, WIN), F32)],
        compiler_params=_cp(("arbitrary", "arbitrary")),
    )(qkv, *([qkv] * (2 * WIN_BLOCKS)), do, lse, bias2)


def _attn_a_dkv(name, qkv, do, lse, delta, bias2, nh):
    T = qkv.shape[0]
    ng = T // GROUP
    scale = HEAD_DIM ** -0.5
    nj = WIN_BLOCKS

    def body(k_ref, v_ref, *refs):
        q_refs = refs[:nj]
        do_refs = refs[nj:2 * nj]
        lse_refs = refs[2 * nj:3 * nj]
        dl_refs = refs[3 * nj:4 * nj]
        bias_ref, dk_ref, dv_ref = refs[4 * nj:]
        r, h = pl.program_id(0), pl.program_id(1)
        kv, vv = k_ref[...], v_ref[...]
        bias = bias_ref[h]
        dk = jnp.zeros((GROUP, HEAD_DIM), F32)
        dv = jnp.zeros((GROUP, HEAD_DIM), F32)
        for j in range(nj):
            qv, dov = q_refs[j][...], do_refs[j][...]
            c0 = (nj - 1 - j) * GROUP
            s = lax.dot_general(qv, kv, NT, preferred_element_type=F32) * scale + bias[:, c0:c0 + GROUP]
            p = jnp.exp(s - _col_of(lse_refs[j][...], h))
            p = jnp.where(r + j <= ng - 1, p, 0.0)
            dp = lax.dot_general(dov, vv, NT, preferred_element_type=F32)
            ds = p * (dp - _col_of(dl_refs[j][...], h))
            dv = dv + lax.dot_general(p.astype(BF16), dov, TN, preferred_element_type=F32)
            dk = dk + lax.dot_general(ds.astype(BF16), qv, TN, preferred_element_type=F32)
        dk_ref[...] = (dk * scale).astype(BF16)
        dv_ref[...] = dv.astype(BF16)

    def qmap(j):
        return functools.partial(lambda r, h, j: (jnp.minimum(r + j, ng - 1), h), j=j)

    def smap(j):
        return functools.partial(lambda r, h, j: (jnp.minimum(r + j, ng - 1), 0), j=j)

    blk = (GROUP, HEAD_DIM)
    in_specs = ([pl.BlockSpec(blk, lambda r, h: (r, nh + h)), pl.BlockSpec(blk, lambda r, h: (r, 2 * nh + h))]
                + [pl.BlockSpec(blk, qmap(j)) for j in range(nj)]
                + [pl.BlockSpec(blk, qmap(j)) for j in range(nj)]
                + [pl.BlockSpec((GROUP, LANE), smap(j)) for j in range(nj)]
                + [pl.BlockSpec((GROUP, LANE), smap(j)) for j in range(nj)]
                + [pl.BlockSpec((nh, GROUP, WIN), lambda r, h: (0, 0, 0))])
    out = pl.BlockSpec(blk, lambda r, h: (r, h))
    return pl.pallas_call(
        body, name=name, grid=(ng, nh), in_specs=in_specs, out_specs=[out, out],
        out_shape=[_sds((T, nh * HEAD_DIM), BF16)] * 2,
        compiler_params=_cp(("parallel", "parallel")),
    )(qkv, qkv, *([qkv] * nj), *([do] * nj), *([lse] * nj), *([delta] * nj), bias2)


def _fox_prep(name, f, b_f):
    T = f.shape[0]
    tb = _pick(T, (256, 128))

    def body(f_ref, b_ref, cum_ref, cumt_ref, carry_ref):
        @pl.when(pl.program_id(0) == 0)
        def _():
            carry_ref[...] = jnp.zeros_like(carry_ref)

        z = f_ref[...] + b_ref[...]
        logf = jnp.minimum(z, 0.0) - jnp.log(1.0 + jnp.exp(-jnp.abs(z)))
        row = lax.broadcasted_iota(jnp.int32, (tb, tb), 0)
        col = lax.broadcasted_iota(jnp.int32, (tb, tb), 1)
        tri = (row >= col).astype(BF16)
        acc = jnp.zeros((tb, LANE), F32)
        for piece in _split3(logf):
            acc = acc + jnp.dot(tri, piece, preferred_element_type=F32)
        cum = acc + carry_ref[...]
        cum_ref[...] = cum
        cumt_ref[...] = cum.T
        carry_ref[...] = cum_ref[pl.ds(tb - 1, 1), :]

    return pl.pallas_call(
        body, name=name, grid=(T // tb,),
        in_specs=[pl.BlockSpec((tb, LANE), lambda i: (i, 0)), pl.BlockSpec((1, LANE), lambda i: (0, 0))],
        out_specs=[pl.BlockSpec((tb, LANE), lambda i: (i, 0)), pl.BlockSpec((LANE, tb), lambda i: (0, i))],
        out_shape=[_sds((T, LANE), F32), _sds((LANE, T), F32)],
        scratch_shapes=[pltpu.VMEM((1, LANE), F32)],
        compiler_params=_cp(("arbitrary",)),
    )(f, b_f)


def _fox_blk(T):
    return _pick(T, (256, 128))


def _fox_mask(s, i, j, tq, tk):
    qpos = i * tq + lax.broadcasted_iota(jnp.int32, s.shape, 0)
    kpos = j * tk + lax.broadcasted_iota(jnp.int32, s.shape, 1)
    return jnp.where(kpos <= qpos, s, NEG_INF)


def _fox_fwd(name, qkv, cum, cumt, nh):
    T = qkv.shape[0]
    tq = tk = _fox_blk(T)
    scale = HEAD_DIM ** -0.5

    def body(q_ref, k_ref, v_ref, cum_ref, cumt_ref, o_ref, lse_ref):
        i, h = pl.program_id(0), pl.program_id(1)

        @pl.when(h == 0)
        def _():
            lse_ref[...] = jnp.zeros_like(lse_ref)

        q = q_ref[...]
        cq = _col_of(cum_ref[...], h)

        def step(j, carry):
            m, l, acc = carry
            k0 = pl.multiple_of(j * tk, tk)
            kj = k_ref[pl.ds(k0, tk), :]
            vj = v_ref[pl.ds(k0, tk), :]
            ck = cumt_ref[pl.ds(h, 1), pl.ds(k0, tk)]
            s = lax.dot_general(q, kj, NT, preferred_element_type=F32) * scale + cq - ck
            s = _fox_mask(s, i, j, tq, tk)
            m_new = jnp.maximum(m, jnp.max(s, axis=1, keepdims=True))
            alpha = jnp.exp(m - m_new)
            p = jnp.exp(s - m_new)
            l = alpha * l + jnp.sum(p, axis=1, keepdims=True)
            acc = alpha * acc + jnp.dot(p.astype(BF16), vj, preferred_element_type=F32)
            return m_new, l, acc

        init = (jnp.full((tq, 1), NEG_INF, F32), jnp.zeros((tq, 1), F32), jnp.zeros((tq, HEAD_DIM), F32))
        m, l, acc = lax.fori_loop(0, i + 1, step, init)
        o_ref[...] = (acc / l).astype(BF16)
        _put_col(lse_ref, h, m + jnp.log(l))

    return pl.pallas_call(
        body, name=name, grid=(T // tq, nh),
        in_specs=[pl.BlockSpec((tq, HEAD_DIM), lambda i, h: (i, 3 * nh + h)),
                  pl.BlockSpec((T, HEAD_DIM), lambda i, h: (0, 4 * nh + h)),
                  pl.BlockSpec((T, HEAD_DIM), lambda i, h: (0, 5 * nh + h)),
                  pl.BlockSpec((tq, LANE), lambda i, h: (i, 0)),
                  pl.BlockSpec((LANE, T), lambda i, h: (0, 0))],
        out_specs=[pl.BlockSpec((tq, HEAD_DIM), lambda i, h: (i, h)), pl.BlockSpec((tq, LANE), lambda i, h: (i, 0))],
        out_shape=[_sds((T, nh * HEAD_DIM), BF16), _sds((T, LANE), F32)],
        compiler_params=_cp(("parallel", "arbitrary")),
    )(qkv, qkv, qkv, cum, cumt)


def _fox_dq(name, qkv, do, lse, cum, cumt, nh):
    T = qkv.shape[0]
    tq = tk = _fox_blk(T)
    scale = HEAD_DIM ** -0.5

    def body(q_ref, k_ref, v_ref, do_ref, lse_ref, cum_ref, cumt_ref, dq_ref, delta_ref):
        i, h = pl.program_id(0), pl.program_id(1)

        @pl.when(h == 0)
        def _():
            delta_ref[...] = jnp.zeros_like(delta_ref)

        q = q_ref[...]
        dov = do_ref[...]
        cq = _col_of(cum_ref[...], h)
        lse = _col_of(lse_ref[...], h)

        def p_dp(j):
            k0 = pl.multiple_of(j * tk, tk)
            kj = k_ref[pl.ds(k0, tk), :]
            vj = v_ref[pl.ds(k0, tk), :]
            ck = cumt_ref[pl.ds(h, 1), pl.ds(k0, tk)]
            s = lax.dot_general(q, kj, NT, preferred_element_type=F32) * scale + cq - ck
            p = jnp.exp(_fox_mask(s, i, j, tq, tk) - lse)
            return p, lax.dot_general(dov, vj, NT, preferred_element_type=F32), kj

        def sweep_delta(j, delta):
            p, dp, _ = p_dp(j)
            return delta + jnp.sum(p * dp, axis=1, keepdims=True)

        delta = lax.fori_loop(0, i + 1, sweep_delta, jnp.zeros((tq, 1), F32))

        def sweep_dq(j, dq):
            p, dp, kj = p_dp(j)
            ds = p * (dp - delta)
            return dq + jnp.dot(ds.astype(BF16), kj, preferred_element_type=F32)

        dq = lax.fori_loop(0, i + 1, sweep_dq, jnp.zeros((tq, HEAD_DIM), F32))
        dq_ref[...] = (dq * scale).astype(BF16)
        _put_col(delta_ref, h, delta)

    blk = pl.BlockSpec((tq, HEAD_DIM), lambda i, h: (i, h))
    stat = pl.BlockSpec((tq, LANE), lambda i, h: (i, 0))
    return pl.pallas_call(
        body, name=name, grid=(T // tq, nh),
        in_specs=[pl.BlockSpec((tq, HEAD_DIM), lambda i, h: (i, 3 * nh + h)),
                  pl.BlockSpec((T, HEAD_DIM), lambda i, h: (0, 4 * nh + h)),
                  pl.BlockSpec((T, HEAD_DIM), lambda i, h: (0, 5 * nh + h)),
                  blk, stat, stat, pl.BlockSpec((LANE, T), lambda i, h: (0, 0))],
        out_specs=[blk, stat],
        out_shape=[_sds((T, nh * HEAD_DIM), BF16), _sds((T, LANE), F32)],
        compiler_params=_cp(("parallel", "arbitrary")),
    )(qkv, qkv, qkv, do, lse, cum, cumt)


def _fox_dkv(name, qkv, do, lse, delta, cum, cumt, nh):
    T = qkv.shape[0]
    tq = tk = _fox_blk(T)
    nq = T // tq
    scale = HEAD_DIM ** -0.5

    def body(k_ref, v_ref, q_ref, do_ref, lse_ref, dl_ref, cum_ref, cumt_ref, dk_ref, dv_ref, dc_ref):
        j, h = pl.program_id(0), pl.program_id(1)

        @pl.when(h == 0)
        def _():
            dc_ref[...] = jnp.zeros_like(dc_ref)

        kj, vj = k_ref[...], v_ref[...]
        k0 = pl.multiple_of(j * tk, tk)
        ck = cumt_ref[pl.ds(h, 1), pl.ds(k0, tk)]

        def step(i, carry):
            dk, dv, dc = carry
            q0 = pl.multiple_of(i * tq, tq)
            qi = q_ref[pl.ds(q0, tq), :]
            doi = do_ref[pl.ds(q0, tq), :]
            cq = _col_of(cum_ref[pl.ds(q0, tq), :], h)
            lse = _col_of(lse_ref[pl.ds(q0, tq), :], h)
            dl = _col_of(dl_ref[pl.ds(q0, tq), :], h)
            s = lax.dot_general(qi, kj, NT, preferred_element_type=F32) * scale + cq - ck
            s = _fox_mask(s, i, j, tq, tk)
            p = jnp.exp(s - lse)
            dp = lax.dot_general(doi, vj, NT, preferred_element_type=F32)
            ds = p * (dp - dl)
            dv = dv + lax.dot_general(p.astype(BF16), doi, TN, preferred_element_type=F32)
            dk = dk + lax.dot_general(ds.astype(BF16), qi, TN, preferred_element_type=F32)
            dc = dc - jnp.sum(ds, axis=0, keepdims=True)
            return dk, dv, dc

        init = (jnp.zeros((tk, HEAD_DIM), F32), jnp.zeros((tk, HEAD_DIM), F32), jnp.zeros((1, tk), F32))
        dk, dv, dc = lax.fori_loop(j, nq, step, init)
        dk_ref[...] = (dk * scale).astype(BF16)
        dv_ref[...] = dv.astype(BF16)
        sub = lax.broadcasted_iota(jnp.int32, (LANE, tk), 0)
        dc_ref[...] = jnp.where(sub == h, dc, dc_ref[...])

    whole = lambda c: pl.BlockSpec((T, HEAD_DIM), c)
    stat = pl.BlockSpec((T, LANE), lambda j, h: (0, 0))
    out = pl.BlockSpec((tk, HEAD_DIM), lambda j, h: (j, h))
    return pl.pallas_call(
        body, name=name, grid=(T // tk, nh),
        in_specs=[pl.BlockSpec((tk, HEAD_DIM), lambda j, h: (j, 4 * nh + h)),
                  pl.BlockSpec((tk, HEAD_DIM), lambda j, h: (j, 5 * nh + h)),
                  whole(lambda j, h: (0, 3 * nh + h)), whole(lambda j, h: (0, h)),
                  stat, stat, stat, pl.BlockSpec((LANE, T), lambda j, h: (0, 0))],
        out_specs=[out, out, pl.BlockSpec((LANE, tk), lambda j, h: (0, j))],
        out_shape=[_sds((T, nh * HEAD_DIM), BF16)] * 2 + [_sds((LANE, T), F32)],
        compiler_params=_cp(("parallel", "arbitrary")),
    )(qkv, qkv, qkv, do, lse, delta, cum, cumt)


def _fox_post(name, dcumt, f, b_f):
    T = f.shape[0]
    tb = _pick(T, (256, 128))
    nb = T // tb

    def body(dc_ref, f_ref, b_ref, df_ref, gb_ref, carry_ref):
        @pl.when(pl.program_id(0) == 0)
        def _():
            carry_ref[...] = jnp.zeros_like(carry_ref)
            gb_ref[...] = jnp.zeros_like(gb_ref)

        dc = dc_ref[...]
        row = lax.broadcasted_iota(jnp.int32, (tb, tb), 0)
        col = lax.broadcasted_iota(jnp.int32, (tb, tb), 1)
        tri = (row >= col).astype(BF16)
        acc = jnp.zeros((LANE, tb), F32)
        for piece in _split3(dc):
            acc = acc + jnp.dot(piece, tri, preferred_element_type=F32)
        dlogf = (acc + carry_ref[...]).T
        carry_ref[...] += jnp.sum(dc, axis=1, keepdims=True)
        z = f_ref[...] + b_ref[...]
        df = dlogf * _sigmoid(-z)
        df_ref[...] = df.astype(BF16)
        gb_ref[...] += jnp.sum(df, axis=0, keepdims=True)

    return pl.pallas_call(
        body, name=name, grid=(nb,),
        in_specs=[pl.BlockSpec((LANE, tb), lambda g: (0, nb - 1 - g)),
                  pl.BlockSpec((tb, LANE), lambda g: (nb - 1 - g, 0)),
                  pl.BlockSpec((1, LANE), lambda g: (0, 0))],
        out_specs=[pl.BlockSpec((tb, LANE), lambda g: (nb - 1 - g, 0)), pl.BlockSpec((1, LANE), lambda g: (0, 0))],
        out_shape=[_sds((T, LANE), BF16), _sds((1, LANE), F32)],
        scratch_shapes=[pltpu.VMEM((LANE, 1), F32)],
        compiler_params=_cp(("arbitrary",)),
    )(dcumt, f, b_f)


def _rel_tables(n_rel):
    max_rel = (n_rel - 1) // 2
    nj = GROUP + WIN - 1
    onehot = np.zeros((n_rel, nj), np.float32)
    for j in range(nj):
        dist = (WIN - 1) - j
        onehot[int(np.clip(dist, -max_rel, max_rel)) + max_rel, j] = 1.0
    a = np.arange(GROUP)[:, None]
    kb = np.arange(WIN)[None, :]
    lo = CHUNK * (a // CHUNK)
    inband = (kb >= lo) & (kb < lo + BAND)
    return onehot, inband


def _bias2_of(rel_bias, onehot, inband):
    bv = jnp.dot(rel_bias, jnp.asarray(onehot), precision=lax.Precision.HIGHEST)
    rows = [bv[:, GROUP - 1 - a:GROUP - 1 - a + WIN] for a in range(GROUP)]
    toe = jnp.stack(rows, axis=1)
    return jnp.where(jnp.asarray(inband)[None], toe, NEG_INF)


def _rel_grad_of(dbias2, onehot):
    nj = GROUP + WIN - 1
    dbv = sum(jnp.pad(dbias2[:, a, :], ((0, 0), (GROUP - 1 - a, nj - WIN - (GROUP - 1 - a)))) for a in range(GROUP))
    return jnp.dot(dbv, jnp.asarray(onehot).T, precision=lax.Precision.HIGHEST)


def _reduce_grads(tag, grads):
    c = lax.axis_index("c")
    theirs = _sibling_halves(f"rs_sibling_{tag}", grads)
    parts = [_add_bf16(f"rs_add_{tag}_{a}", g, c, t) for a, (g, t) in enumerate(zip(grads, theirs))]
    got = _scatter_chips(f"rs_scatter_{tag}", parts)
    halves = [_sum4(f"rs_sum_{tag}_{a}", p) for a, p in enumerate(got)]
    return _swap_halves(f"rs_swap_{tag}", halves)


def kernel(x, g_mix, w_in, b_f, b_gate, rel_bias, w_branch_a, w_branch_b, w_out, g_ffn, w_gate_ffn, w_up_ffn, w_down_ffn, g_final, loss_target, m_g_mix, m_w_in, m_b_f, m_b_gate, m_rel_bias, m_w_branch_a, m_w_branch_b, m_w_out, m_g_ffn, m_w_gate_ffn, m_w_up_ffn, m_w_down_ffn, m_g_final, v_g_mix, v_w_in, v_b_f, v_b_gate, v_rel_bias, v_w_branch_a, v_w_branch_b, v_w_out, v_g_ffn, v_w_gate_ffn, v_w_up_ffn, v_w_down_ffn, v_g_final):
    T, D = x.shape[1], x.shape[2]
    Ls = w_in.shape[2]
    W = w_branch_a.shape[1]
    nh = W // HEAD_DIM
    nhb = b_f.shape[1]
    assert w_branch_b.shape[1] == W and nhb == nh and rel_bias.shape[1] == nh
    W6 = 6 * W
    Fl = w_gate_ffn.shape[2]
    Fp = -(-Fl // LANE) * LANE
    n_rel = rel_bias.shape[2]
    chip = 2 * lax.axis_index("x") + lax.axis_index("y")
    lay, nbw, nmain = _in_layout(D, W6, nhb, Ls)
    onehot, inband = _rel_tables(n_rel)

    xs, tgt = x[0], loss_target[0]

    win_f32 = lax.switch(chip, [functools.partial(_to_window, lay_k=lay[k], nbw=nbw) for k in range(N_CHIPS)], w_in[0])
    pad_c = lambda w: jnp.pad(w, ((0, 0), (0, Fp - Fl)))
    pad_r = lambda w: jnp.pad(w, ((0, Fp - Fl), (0, 0)))
    sh_in = _cast_bf16("cast_w_in", win_f32)
    sh_a = _cast_bf16("cast_w_a", w_branch_a[0])
    sh_b = _cast_bf16("cast_w_b", w_branch_b[0])
    sh_o = _cast_bf16("cast_w_out", w_out[0])
    sh_g = _cast_bf16("cast_w_gate", pad_c(w_gate_ffn[0]))
    sh_u = _cast_bf16("cast_w_up", pad_c(w_up_ffn[0]))
    sh_d = _cast_bf16("cast_w_down", pad_r(w_down_ffn[0]))
    (wins,) = _allgather("ag_w_in", [sh_in])
    wa_g, wb_g, wo_g = _allgather("ag_w_mix", [sh_a, sh_b, sh_o])
    wg_g, wu_g, wd_g = _allgather("ag_w_ffn", [sh_g, sh_u, sh_d])
    wc = _assemble_in("assemble_w_in", wins, lay, nbw, nmain)
    wo_full = wo_g.reshape(D, D)
    wd_full = wd_g.reshape(N_CHIPS * Fp, D)

    h1, r1 = _rms_fwd("rms1", xs, g_mix)
    qkv = _mm_nn("proj_qkv", h1, wc, BF16, b_col0=0, n=W6)
    gates = _mm_nn("proj_gates", h1, wc, F32, b_col0=W6, n=2 * D)
    fl = _mm_nn("proj_f", h1, wc, F32, b_col0=nmain, n=LANE, tn=LANE)
    bias2 = _bias2_of(rel_bias[0], onehot, inband)
    bf_pad = jnp.pad(b_f, ((0, 0), (0, LANE - nhb)))
    o_a, lse_a = _attn_a_fwd("attn_a_fwd", qkv, bias2, nh)
    cum, cumt = _fox_prep("fox_prep", fl, bf_pad)
    o_b, lse_b = _fox_fwd("fox_fwd", qkv, cum, cumt, nh)
    u_a = _mm_nn("branch_a", o_a, wa_g, F32)
    u_b = _mm_nn("branch_b", o_b, wb_g, F32)
    merged = _merge_fwd("merge", gates, u_a, u_b, b_gate)
    x1 = _mm_nn("out_proj", merged, wo_full, F32, residual=xs)
    h2, r2 = _rms_fwd("rms2", x1, g_ffn)

    tm_f = _pick(T, (512, 256, 128))
    tn_f = _pick(Fp, (1408, 1024, 512, 256, 128))
    tk_f = _pick(D, (1024, 512, 256, 128))
    nps_f = Fp // tn_f

    def swiglu_ep(accs, e_refs, o_refs):
        g, u = accs
        o_refs[0][...] = g.astype(BF16)
        o_refs[1][...] = u.astype(BF16)
        o_refs[2][...] = (g * _sigmoid(g) * u).astype(BF16)

    hid_spec = pl.BlockSpec((tm_f, tn_f), lambda i, j, k: (i, j))
    wcol_spec = pl.BlockSpec((None, tk_f, tn_f), lambda i, j, k: (j // nps_f, k, j % nps_f))
    gate, up, hidden = _mm(
        "ffn_up", "nn", [h2], [pl.BlockSpec((tm_f, tk_f), lambda i, j, k: (i, k))], [wg_g, wu_g], [wcol_spec, wcol_spec],
        [(0, 0, 0), (0, 1, 1)], 2, (T // tm_f, N_CHIPS * Fp // tn_f, D // tk_f), tm_f, tn_f,
        [_sds((T, N_CHIPS * Fp), BF16)] * 3, [hid_spec] * 3, swiglu_ep)
    x2 = _mm_nn("ffn_down", hidden, wd_full, F32, residual=x1, tk=_pick(N_CHIPS * Fp, (1408, 1024, 512, 256, 128)))

    dx2, dx2b, loss_part, gg_final = _final_loss_bwd("final_loss", x2, tgt, g_final.reshape(1, D))

    def swiglu_bwd_ep(accs, e_refs, o_refs):
        dh = accs[0]
        g = e_refs[0][...].astype(F32)
        u = e_refs[1][...].astype(F32)
        sg = _sigmoid(g)
        o_refs[0][...] = (dh * u * (sg * (1.0 + g * (1.0 - sg)))).astype(BF16)
        o_refs[1][...] = (dh * (g * sg)).astype(BF16)

    tk_b = _pick(D, (1024, 512, 256, 128))
    dgate, dup = _mm(
        "ffn_down_bwd", "nt", [dx2b], [pl.BlockSpec((tm_f, tk_b), lambda i, j, k: (i, k))],
        [wd_full], [pl.BlockSpec((tn_f, tk_b), lambda i, j, k: (j, k))], [(0, 0, 0)], 1,
        (T // tm_f, N_CHIPS * Fp // tn_f, D // tk_b), tm_f, tn_f,
        [_sds((T, N_CHIPS * Fp), BF16)] * 2, [hid_spec] * 2, swiglu_bwd_ep,
        extra=[gate, up], extra_specs=[hid_spec, hid_spec])
    dwd = _mm_tn("dw_down", hidden, dx2b, BF16, tm=_pick(N_CHIPS * Fp, (1408, 1024, 512, 256, 128)))
    dh2 = _mm_nt("ffn_up_bwd", [dgate, dup], [wg_g, wu_g], F32)
    dwg = _mm_tn("dw_gate", h2, dgate, BF16, slots=N_CHIPS)
    dwu = _mm_tn("dw_up", h2, dup, BF16, slots=N_CHIPS)
    dx1, dx1b, gg_ffn = _rms_bwd("rms2_bwd", [dh2], x1, r2, g_ffn, dx2, True)

    dmerged = _mm_nt("out_proj_bwd", [dx1b], [wo_full], F32)
    dwo = _mm_tn("dw_out", merged, dx1b, BF16)
    du_a, du_b, dga, dgb, gbg_a, gbg_b = _merge_bwd("merge_bwd", dmerged, gates, u_a, u_b, b_gate)
    do_a = _mm_nt("branch_a_bwd", [du_a], [wa_g], BF16)
    do_b = _mm_nt("branch_b_bwd", [du_b], [wb_g], BF16)
    dwa = _mm_tn("dw_a", o_a, du_a, BF16, slots=N_CHIPS)
    dwb = _mm_tn("dw_b", o_b, du_b, BF16, slots=N_CHIPS)

    dq_a, delta_a, dbias2 = _attn_a_dq("attn_a_dq", qkv, do_a, lse_a, bias2, nh)
    dk_a, dv_a = _attn_a_dkv("attn_a_dkv", qkv, do_a, lse_a, delta_a, bias2, nh)
    dq_b, delta_b = _fox_dq("fox_dq", qkv, do_b, lse_b, cum, cumt, nh)
    dk_b, dv_b, dcumt = _fox_dkv("fox_dkv", qkv, do_b, lse_b, delta_b, cum, cumt, nh)
    df, gbf = _fox_post("fox_post", dcumt, fl, bf_pad)

    dqkv = jnp.concatenate([dq_a, dk_a, dv_a, dq_b, dk_b, dv_b], axis=1)
    dgates = jnp.concatenate([dga, dgb], axis=1)
    dh_q = _mm_nt("proj_qkv_bwd", [dqkv], [wc], F32, k0_list=[0], tk=_pick(W6, (1024, 512, 256, 128)))
    dh_g = _mm_nt("proj_gates_bwd", [dgates], [wc], F32, k0_list=[W6], tk=_pick(math_gcd(W6, 2 * D), (1024, 512, 256, 128)))
    dh_f = _mm_nt("proj_f_bwd", [df], [wc], F32, k0_list=[nmain], tk=LANE)
    dwc_q = _mm_tn("dw_in_qkv", h1, dqkv, BF16)
    dwc_g = _mm_tn("dw_in_gates", h1, dgates, BF16)
    dwc_f = _mm_tn("dw_in_f", h1, df, BF16, tn=LANE)
    grad_x, gg_mix = _rms_bwd("rms1_bwd", [dh_q, dh_g, dh_f], xs, r1, g_mix, dx1, False)

    dwc = jnp.concatenate([dwc_q, dwc_g, dwc_f], axis=1)
    zeros_blk = jnp.zeros((D, LANE), BF16)
    win_parts = []
    for k in range(N_CHIPS):
        cols = [dwc[:, lay[k]["s"] * LANE:lay[k]["e"] * LANE]]
        nb = lay[k]["e"] - lay[k]["s"]
        if lay[k]["f"]:
            cols.append(dwc[:, nmain:nmain + LANE])
            nb += 1
        cols += [zeros_blk] * (nbw - nb)
        win_parts.append(jnp.concatenate(cols, axis=1) if len(cols) > 1 else cols[0])
    dwin = jnp.stack(win_parts, axis=0)
    (g_win,) = _reduce_grads("in", [dwin])
    g_a, g_b, g_o = _reduce_grads("mix", [dwa, dwb, dwo.reshape(N_CHIPS, D // N_CHIPS, D)])
    g_g, g_u, g_d = _reduce_grads("ffn", [dwg, dwu, dwd.reshape(N_CHIPS, Fp, D)])
    g_in = lax.switch(chip, [functools.partial(_from_window, lay_k=lay[k]) for k in range(N_CHIPS)], g_win)
    g_g, g_u, g_d = g_g[:, :Fl], g_u[:, :Fl], g_d[:Fl, :]

    big = {}
    for nm, w, g, m, v in (("w_in", w_in, g_in, m_w_in, v_w_in), ("w_branch_a", w_branch_a, g_a, m_w_branch_a, v_w_branch_a),
                           ("w_branch_b", w_branch_b, g_b, m_w_branch_b, v_w_branch_b), ("w_out", w_out, g_o, m_w_out, v_w_out),
                           ("w_gate_ffn", w_gate_ffn, g_g, m_w_gate_ffn, v_w_gate_ffn),
                           ("w_up_ffn", w_up_ffn, g_u, m_w_up_ffn, v_w_up_ffn),
                           ("w_down_ffn", w_down_ffn, g_d, m_w_down_ffn, v_w_down_ffn)):
        d, mn, vn = _adamw(f"adamw_{nm}", w[0], g, m[0], v[0])
        big[nm] = (g[None], d[None], mn[None], vn[None])

    g_rel = _rel_grad_of(dbias2, onehot)
    small = [("loss", loss_part[:, :1], None, None, None),
             ("g_mix", gg_mix, g_mix, m_g_mix, v_g_mix), ("b_f", gbf[:, :nhb], b_f, m_b_f, v_b_f),
             ("b_gate", jnp.concatenate([gbg_a, gbg_b], axis=1), b_gate, m_b_gate, v_b_gate),
             ("rel_bias", g_rel, rel_bias, m_rel_bias, v_rel_bias), ("g_ffn", gg_ffn, g_ffn, m_g_ffn, v_g_ffn),
             ("g_final", gg_final, g_final, m_g_final, v_g_final)]
    sizes = [int(np.prod(s[1].shape)) for s in small]
    total = sum(sizes)
    npad = -(-total // 1024) * 1024

    def pack(arrs):
        flat = jnp.concatenate([a.reshape(-1).astype(F32) for a in arrs])
        return jnp.pad(flat, (0, npad - total)).reshape(8, npad // 8)

    zero1 = jnp.zeros((1,), F32)
    g_all = _small_allreduce("small_allreduce", pack([s[1] for s in small]))
    w_s = pack([zero1 if s[2] is None else s[2] for s in small])
    m_s = pack([zero1 if s[3] is None else s[3] for s in small])
    v_s = pack([zero1 + 1.0 if s[4] is None else s[4] for s in small])
    d_s, mn_s, vn_s = _adamw("adamw_small", w_s, g_all, m_s, v_s)

    def unpack(packed):
        flat = packed.reshape(-1)
        out, pos = {}, 0
        for s, n in zip(small, sizes):
            if s[2] is not None:
                out[s[0]] = flat[pos:pos + n].reshape(s[2].shape)
            else:
                out[s[0]] = flat[pos:pos + n].reshape(())
            pos += n
        return out

    gs, ds, ms, vs = unpack(g_all), unpack(d_s), unpack(mn_s), unpack(vn_s)
    order = ["g_mix", "w_in", "b_f", "b_gate", "rel_bias", "w_branch_a", "w_branch_b", "w_out", "g_ffn",
             "w_gate_ffn", "w_up_ffn", "w_down_ffn", "g_final"]
    res = [[], [], [], []]
    for nm in order:
        four = big[nm] if nm in big else (gs[nm], ds[nm], ms[nm], vs[nm])
        for q in range(4):
            res[q].append(four[q])
    return (gs["loss"], grad_x[None], *res[0], *res[1], *res[2], *res[3])


def math_gcd(a, b):
    while b:
        a, b = b, a % b
    return a
```

```python
import functools

import numpy as np
import jax
import jax.numpy as jnp
from jax import lax
from jax.experimental import pallas as pl
from jax.experimental.pallas import tpu as pltpu

F32 = jnp.float32
BF16 = jnp.bfloat16
LANE = 128
HEAD_DIM = 128
CHUNK = 64
LEFT_CHUNKS = 8
GROUP = 128
WIN_BLOCKS = 5
WIN = WIN_BLOCKS * GROUP
BAND = (LEFT_CHUNKS + 1) * CHUNK
RMS_EPS = 1e-6
NEG_INF = -1e30
ADAM_LR = 0.001
ADAM_B1 = 0.9
ADAM_B2 = 0.999
ADAM_EPS = 1e-08
ADAM_WD = 0.01
ADAM_STEP = 10
N_CHIPS = 4
MESH = pl.DeviceIdType.MESH
VMEM_LIMIT = 52 * 1024 * 1024
ANY = pl.BlockSpec(memory_space=pl.ANY)

NN = (((1,), (0,)), ((), ()))
NT = (((1,), (1,)), ((), ()))
TN = (((0,), (0,)), ((), ()))


def _cp(sem):
    return pltpu.CompilerParams(dimension_semantics=sem, vmem_limit_bytes=VMEM_LIMIT)


def _sds(shape, dtype):
    return jax.ShapeDtypeStruct(shape, dtype)


def _pick(n, prefs):
    for p in prefs:
        if n % p == 0:
            return p
    return n


def _sigmoid(v):
    return 1.0 / (1.0 + jnp.exp(-v))


def _split3(v):
    hi = v.astype(BF16)
    r1 = v - hi.astype(F32)
    mid = r1.astype(BF16)
    lo = (r1 - mid.astype(F32)).astype(BF16)
    return hi, mid, lo


def _col_of(blk, h):
    lane = lax.broadcasted_iota(jnp.int32, blk.shape, 1)
    return jnp.sum(jnp.where(lane == h, blk, 0.0), axis=1, keepdims=True)


def _put_col(ref, h, col):
    lane = lax.broadcasted_iota(jnp.int32, ref.shape, 1)
    ref[...] = jnp.where(lane == h, col, ref[...])


def _mm(name, mode, a_list, a_specs, b_list, b_specs, pairs, n_acc, grid, tm, tn,
        out_shapes, out_specs, epilogue, extra=(), extra_specs=()):
    n_a, n_b, n_e, n_o = len(a_list), len(b_list), len(extra), len(out_shapes)
    nk = grid[2]
    dn = {"nn": NN, "nt": NT, "tn": TN}[mode]

    def body(*refs):
        a_refs = refs[:n_a]
        b_refs = refs[n_a:n_a + n_b]
        e_refs = refs[n_a + n_b:n_a + n_b + n_e]
        o_refs = refs[n_a + n_b + n_e:n_a + n_b + n_e + n_o]
        acc_refs = refs[n_a + n_b + n_e + n_o:]
        k = pl.program_id(2)

        @pl.when(k == 0)
        def _():
            for acc in acc_refs:
                acc[...] = jnp.zeros_like(acc)

        for ai, bi, ci in pairs:
            acc_refs[ci][...] += lax.dot_general(a_refs[ai][...], b_refs[bi][...], dn,
                                                 preferred_element_type=F32)

        @pl.when(k == nk - 1)
        def _():
            epilogue([acc[...] for acc in acc_refs], e_refs, o_refs)

    return pl.pallas_call(
        body, name=name, grid=grid,
        in_specs=list(a_specs) + list(b_specs) + list(extra_specs),
        out_specs=list(out_specs), out_shape=list(out_shapes),
        scratch_shapes=[pltpu.VMEM((tm, tn), F32) for _ in range(n_acc)],
        compiler_params=_cp(("parallel", "parallel", "arbitrary")),
    )(*a_list, *b_list, *extra)


def _store(dtype):
    def ep(accs, e_refs, o_refs):
        o_refs[0][...] = accs[0].astype(dtype)
    return ep


def _mm_nn(name, a, b, out_dtype, *, b_col0=0, n=None, slot_w=None, tm=512, tn=None, tk=None,
           residual=None):
    M, K = a.shape
    if b.ndim == 3:
        Ns = b.shape[2]
        n = b.shape[0] * Ns
        tn = tn or _pick(Ns, (1408, 1024, 512, 256, 128))
        nps = Ns // tn
        b_spec = pl.BlockSpec((None, tk or _pick(K, (1024, 512, 256, 128)), tn),
                              lambda i, j, k: (j // nps, k, j % nps))
    else:
        n = n or b.shape[1]
        tn = tn or _pick(math_gcd(n, b_col0) if b_col0 else n, (2048, 1024, 512, 256, 128))
        assert b_col0 % tn == 0 and n % tn == 0
        c0 = b_col0 // tn
        b_spec = pl.BlockSpec((tk or _pick(K, (1024, 512, 256, 128)), tn), lambda i, j, k: (k, c0 + j))
    tk = tk or _pick(K, (1024, 512, 256, 128))
    tm = _pick(M, (tm, 256, 128))
    grid = (M // tm, n // tn, K // tk)
    a_spec = pl.BlockSpec((tm, tk), lambda i, j, k: (i, k))
    o_spec = pl.BlockSpec((tm, tn), lambda i, j, k: (i, j))
    if residual is None:
        return _mm(name, "nn", [a], [a_spec], [b], [b_spec], [(0, 0, 0)], 1, grid, tm, tn,
                   [_sds((M, n), out_dtype)], [o_spec], _store(out_dtype))[0]

    def ep(accs, e_refs, o_refs):
        o_refs[0][...] = (e_refs[0][...] + accs[0]).astype(out_dtype)
    return _mm(name, "nn", [a], [a_spec], [b], [b_spec], [(0, 0, 0)], 1, grid, tm, tn,
               [_sds((M, n), out_dtype)], [o_spec], ep, extra=[residual], extra_specs=[o_spec])[0]


def _mm_nt(name, a_list, b_list, out_dtype, *, k0_list=None, tm=512, tn=None, tk=None):
    M, K = a_list[0].shape
    b0 = b_list[0]
    N = b0.shape[1] if b0.ndim == 3 else b0.shape[0]
    tm = _pick(M, (tm, 256, 128))
    tn = tn or _pick(N, (1024, 512, 256, 128))
    if b0.ndim == 3:
        Ks = b0.shape[2]
        tk = tk or _pick(Ks, (1408, 1024, 512, 256, 128))
        kps = Ks // tk
        b_specs = [pl.BlockSpec((None, tn, tk), lambda i, j, k: (k // kps, j, k % kps)) for _ in b_list]
    else:
        tk = tk or _pick(K, (1024, 896, 512, 256, 128))
        k0_list = k0_list or [0] * len(b_list)
        b_specs = []
        for k0 in k0_list:
            assert k0 % tk == 0
            b_specs.append(pl.BlockSpec((tn, tk), functools.partial(lambda i, j, k, c: (j, c + k), c=k0 // tk)))
    grid = (M // tm, N // tn, K // tk)
    a_specs = [pl.BlockSpec((tm, tk), lambda i, j, k: (i, k)) for _ in a_list]
    o_spec = pl.BlockSpec((tm, tn), lambda i, j, k: (i, j))
    pairs = [(p, p, 0) for p in range(len(a_list))]
    return _mm(name, "nt", a_list, a_specs, b_list, b_specs, pairs, 1, grid, tm, tn,
               [_sds((M, N), out_dtype)], [o_spec], _store(out_dtype))[0]


def _mm_tn(name, a, b, out_dtype, *, slots=None, tm=None, tn=None, tk=512):
    Kc, Mo = a.shape
    No = b.shape[1]
    tm = tm or _pick(Mo, (1024, 704, 512, 256, 128))
    tk = _pick(Kc, (tk, 256, 128))
    if slots:
        Ns = No // slots
        tn = tn or _pick(Ns, (1408, 1024, 512, 256, 128))
        nps = Ns // tn
        o_spec = pl.BlockSpec((None, tm, tn), lambda i, j, k: (j // nps, i, j % nps))
        o_shape = _sds((slots, Mo, Ns), out_dtype)
    else:
        tn = tn or _pick(No, (1024, 512, 256, 128))
        o_spec = pl.BlockSpec((tm, tn), lambda i, j, k: (i, j))
        o_shape = _sds((Mo, No), out_dtype)
    grid = (Mo // tm, No // tn, Kc // tk)
    a_spec = pl.BlockSpec((tk, tm), lambda i, j, k: (k, i))
    b_spec = pl.BlockSpec((tk, tn), lambda i, j, k: (k, j))
    return _mm(name, "tn", [a], [a_spec], [b], [b_spec], [(0, 0, 0)], 1, grid, tm, tn,
               [o_shape], [o_spec], _store(out_dtype))[0]


def _cast_bf16(name, w, chip):
    R, C = w.shape
    tr = _pick(R, (256, 128, 64, 32, 16))

    def body(k_ref, w_ref, o_ref):
        o_ref[...] = w_ref[...].astype(BF16)

    gs = pltpu.PrefetchScalarGridSpec(
        num_scalar_prefetch=1, grid=(R // tr,),
        in_specs=[pl.BlockSpec((tr, C), lambda i, k: (i, 0))],
        out_specs=pl.BlockSpec((None, tr, C), lambda i, k: (k[0], i, 0)))
    return pl.pallas_call(body, name=name, grid_spec=gs, out_shape=_sds((N_CHIPS, R, C), BF16),
                          compiler_params=_cp(("parallel",)))(jnp.reshape(chip, (1,)).astype(jnp.int32), w)


def _rms_fwd(name, x, g):
    T, D = x.shape
    tr = _pick(T, (256, 128))

    def body(x_ref, g_ref, h_ref, r_ref):
        xv = x_ref[...]
        r = lax.rsqrt(jnp.mean(xv * xv, axis=1, keepdims=True) + RMS_EPS)
        h_ref[...] = (xv * r * g_ref[...]).astype(BF16)
        r_ref[...] = r

    row = pl.BlockSpec((tr, D), lambda i: (i, 0))
    return pl.pallas_call(
        body, name=name, grid=(T // tr,),
        in_specs=[row, pl.BlockSpec((1, D), lambda i: (0, 0))],
        out_specs=[row, pl.BlockSpec((tr, 1), lambda i: (i, 0))],
        out_shape=[_sds((T, D), BF16), _sds((T, 1), F32)], compiler_params=_cp(("parallel",)),
    )(x, g)


def _final_loss_bwd(name, x2, tgt, g):
    T, D = x2.shape
    tr = _pick(T, (256, 128))

    def body(x_ref, t_ref, g_ref, dx_ref, dxb_ref, loss_ref, gg_ref):
        @pl.when(pl.program_id(0) == 0)
        def _():
            loss_ref[...] = jnp.zeros_like(loss_ref)
            gg_ref[...] = jnp.zeros_like(gg_ref)

        xv = x_ref[...]
        gv = g_ref[...]
        r = lax.rsqrt(jnp.mean(xv * xv, axis=1, keepdims=True) + RMS_EPS)
        n = xv * r
        e = n * gv - t_ref[...]
        loss_ref[...] += 0.5 * jnp.sum(jnp.mean(e * e, axis=1, keepdims=True), axis=0, keepdims=True)
        dy = e * (1.0 / D)
        gg_ref[...] += jnp.sum(dy * n, axis=0, keepdims=True)
        gy = dy * gv
        dx = r * (gy - n * jnp.mean(gy * n, axis=1, keepdims=True))
        dx_ref[...] = dx
        dxb_ref[...] = dx.astype(BF16)

    row = pl.BlockSpec((tr, D), lambda i: (i, 0))
    vec = pl.BlockSpec((1, D), lambda i: (0, 0))
    return pl.pallas_call(
        body, name=name, grid=(T // tr,),
        in_specs=[row, row, vec],
        out_specs=[row, row, pl.BlockSpec((1, LANE), lambda i: (0, 0)), vec],
        out_shape=[_sds((T, D), F32), _sds((T, D), BF16), _sds((1, LANE), F32), _sds((1, D), F32)],
        compiler_params=_cp(("arbitrary",)),
    )(x2, tgt, g)


def _rms_bwd(name, dh_list, x, r, g, dres, want_bf16):
    T, D = x.shape
    tr = _pick(T, (128,))
    n_dh = len(dh_list)

    def body(*refs):
        dh_refs = refs[:n_dh]
        x_ref, r_ref, g_ref, dres_ref = refs[n_dh:n_dh + 4]
        outs = refs[n_dh + 4:]
        gg_ref = outs[-1]

        @pl.when(pl.program_id(0) == 0)
        def _():
            gg_ref[...] = jnp.zeros_like(gg_ref)

        dh = dh_refs[0][...]
        for ref in dh_refs[1:]:
            dh = dh + ref[...]
        rv = r_ref[...]
        n = x_ref[...] * rv
        gg_ref[...] += jnp.sum(dh * n, axis=0, keepdims=True)
        gy = dh * g_ref[...]
        dx = dres_ref[...] + rv * (gy - n * jnp.mean(gy * n, axis=1, keepdims=True))
        outs[0][...] = dx
        if want_bf16:
            outs[1][...] = dx.astype(BF16)

    row = pl.BlockSpec((tr, D), lambda i: (i, 0))
    vec = pl.BlockSpec((1, D), lambda i: (0, 0))
    out_specs = [row] + ([row] if want_bf16 else []) + [vec]
    out_shape = [_sds((T, D), F32)] + ([_sds((T, D), BF16)] if want_bf16 else []) + [_sds((1, D), F32)]
    return pl.pallas_call(
        body, name=name, grid=(T // tr,),
        in_specs=[row] * n_dh + [row, pl.BlockSpec((tr, 1), lambda i: (i, 0)), vec, row],
        out_specs=out_specs, out_shape=out_shape, compiler_params=_cp(("arbitrary",)),
    )(*dh_list, x, r, g, dres)


def _merge_fwd(name, gates, u_a, u_b, b_gate):
    T, D = u_a.shape
    tr = _pick(T, (256, 128))

    def body(ga_ref, gb_ref, ua_ref, ub_ref, ba_ref, bb_ref, o_ref):
        sa = _sigmoid(ga_ref[...] + ba_ref[...])
        sb = _sigmoid(gb_ref[...] + bb_ref[...])
        o_ref[...] = (sa * ua_ref[...] + sb * ub_ref[...]).astype(BF16)

    row = pl.BlockSpec((tr, D), lambda i: (i, 0))
    row1 = pl.BlockSpec((tr, D), lambda i: (i, 1))
    v0 = pl.BlockSpec((1, D), lambda i: (0, 0))
    v1 = pl.BlockSpec((1, D), lambda i: (0, 1))
    return pl.pallas_call(
        body, name=name, grid=(T // tr,),
        in_specs=[row, row1, row, row, v0, v1], out_specs=row,
        out_shape=_sds((T, D), BF16), compiler_params=_cp(("parallel",)),
    )(gates, gates, u_a, u_b, b_gate, b_gate)


def _merge_bwd(name, dm, gates, u_a, u_b, b_gate):
    T, D = u_a.shape
    tr = _pick(T, (128,))

    def body(dm_ref, ga_ref, gb_ref, ua_ref, ub_ref, ba_ref, bb_ref, dua_ref, dub_ref, dga_ref, dgb_ref,
             gba_ref, gbb_ref):
        @pl.when(pl.program_id(0) == 0)
        def _():
            gba_ref[...] = jnp.zeros_like(gba_ref)
            gbb_ref[...] = jnp.zeros_like(gbb_ref)

        d = dm_ref[...]
        sa = _sigmoid(ga_ref[...] + ba_ref[...])
        sb = _sigmoid(gb_ref[...] + bb_ref[...])
        dua_ref[...] = (d * sa).astype(BF16)
        dub_ref[...] = (d * sb).astype(BF16)
        dga = d * ua_ref[...] * sa * (1.0 - sa)
        dgb = d * ub_ref[...] * sb * (1.0 - sb)
        dga_ref[...] = dga.astype(BF16)
        dgb_ref[...] = dgb.astype(BF16)
        gba_ref[...] += jnp.sum(dga, axis=0, keepdims=True)
        gbb_ref[...] += jnp.sum(dgb, axis=0, keepdims=True)

    row = pl.BlockSpec((tr, D), lambda i: (i, 0))
    row1 = pl.BlockSpec((tr, D), lambda i: (i, 1))
    v0 = pl.BlockSpec((1, D), lambda i: (0, 0))
    v1 = pl.BlockSpec((1, D), lambda i: (0, 1))
    outs = pl.pallas_call(
        body, name=name, grid=(T // tr,),
        in_specs=[row, row, row1, row, row, v0, v1],
        out_specs=[row, row, row, row, v0, v0],
        out_shape=[_sds((T, D), BF16), _sds((T, D), BF16), _sds((T, D), BF16), _sds((T, D), BF16),
                   _sds((1, D), F32), _sds((1, D), F32)],
        compiler_params=_cp(("arbitrary",)),
    )(dm, gates, gates, u_a, u_b, b_gate, b_gate)
    return outs


def _adamw(name, w, g, m, v):
    R, C = w.shape
    tr = _pick(R, (64, 32, 16, 8))
    c1 = 1.0 - ADAM_B1 ** ADAM_STEP
    c2 = 1.0 - ADAM_B2 ** ADAM_STEP

    def body(w_ref, g_ref, m_ref, v_ref, d_ref, mo_ref, vo_ref):
        gv = g_ref[...]
        mn = ADAM_B1 * m_ref[...] + (1.0 - ADAM_B1) * gv
        vn = ADAM_B2 * v_ref[...] + (1.0 - ADAM_B2) * (gv * gv)
        d_ref[...] = -ADAM_LR * ((mn / c1) / (jnp.sqrt(vn / c2) + ADAM_EPS) + ADAM_WD * w_ref[...])
        mo_ref[...] = mn
        vo_ref[...] = vn

    blk = pl.BlockSpec((tr, C), lambda i: (i, 0))
    return pl.pallas_call(
        body, name=name, grid=(R // tr,),
        in_specs=[blk] * 4, out_specs=[blk] * 3,
        out_shape=[_sds((R, C), F32)] * 3, compiler_params=_cp(("parallel",)),
    )(w, g, m, v)


def _add_bf16(name, a, a_row0, b):
    S, h, C = b.shape
    tr = _pick(h, (256, 128, 64, 32, 16))
    nb = h // tr

    def body(off_ref, a_ref, b_ref, o_ref):
        o_ref[...] = (a_ref[...].astype(F32) + b_ref[...].astype(F32)).astype(BF16)

    gs = pltpu.PrefetchScalarGridSpec(
        num_scalar_prefetch=1, grid=(S, nb),
        in_specs=[pl.BlockSpec((None, tr, C), lambda s, i, off: (s, off[0] * nb + i, 0)),
                  pl.BlockSpec((None, tr, C), lambda s, i, off: (s, i, 0))],
        out_specs=pl.BlockSpec((None, tr, C), lambda s, i, off: (s, i, 0)))
    return pl.pallas_call(body, name=name, grid_spec=gs, out_shape=_sds((S, h, C), BF16),
                          compiler_params=_cp(("parallel", "parallel")))(
        jnp.reshape(a_row0, (1,)).astype(jnp.int32), a, b)


def _sum4(name, got, mine, chip, core):
    S, h, C = got.shape
    tr = _pick(h, (256, 128, 64, 32, 16))
    nb = h // tr

    def body(chip_ref, core_ref, m_ref, g_ref, o_ref):
        acc = m_ref[...].astype(F32)
        for s in range(S):
            acc = acc + g_ref[s].astype(F32)
        o_ref[...] = acc

    gs = pltpu.PrefetchScalarGridSpec(
        num_scalar_prefetch=2, grid=(nb,),
        in_specs=[pl.BlockSpec((None, tr, C), lambda i, kc, cc: (kc[0], i, 0)),
                  pl.BlockSpec((S, tr, C), lambda i, kc, cc: (0, i, 0))],
        out_specs=pl.BlockSpec((tr, C), lambda i, kc, cc: (cc[0] * nb + i, 0)))
    return pl.pallas_call(body, name=name, grid_spec=gs, out_shape=_sds((2 * h, C), F32),
                          compiler_params=_cp(("parallel",)))(
        jnp.reshape(chip, (1,)).astype(jnp.int32), jnp.reshape(core, (1,)).astype(jnp.int32), mine, got)


def _place():
    x, y, c = lax.axis_index("x"), lax.axis_index("y"), lax.axis_index("c")
    chips = [(1 - x, y), (x, 1 - y), (1 - x, 1 - y)]
    return x, y, c, chips


def _allgather(name, shards):
    n = len(shards)

    def body(*refs):
        out_refs = refs[n:2 * n]
        send_sems, recv_sems = refs[2 * n:]
        x, y, c, chips = _place()
        k = 2 * x + y
        sibling = (x, y, 1 - c)
        firsts, passed = [], []
        for a in range(n):
            out = out_refs[a]
            h = out.shape[1] // 2
            for j, (cx, cy) in enumerate(chips):
                rows = out.at[k, pl.ds(c * h, h), :]
                cp = pltpu.make_async_remote_copy(
                    src_ref=rows, dst_ref=rows,
                    send_sem=send_sems.at[6 * a + j], recv_sem=recv_sems.at[6 * a + j],
                    device_id=(cx, cy, c), device_id_type=MESH)
                cp.start()
                firsts.append(cp)
        for a in range(n):
            out = out_refs[a]
            h = out.shape[1] // 2
            for j, (cx, cy) in enumerate(chips):
                kj = 2 * cx + cy
                rows = out.at[kj, pl.ds(c * h, h), :]
                pltpu.make_async_remote_copy(
                    src_ref=rows, dst_ref=rows, send_sem=send_sems.at[6 * a + j], recv_sem=recv_sems.at[6 * a + j],
                    device_id=(cx, cy, c), device_id_type=MESH).wait_recv()
                fw = pltpu.make_async_remote_copy(
                    src_ref=rows, dst_ref=rows, send_sem=send_sems.at[6 * a + 3 + j],
                    recv_sem=recv_sems.at[6 * a + 3 + j], device_id=sibling, device_id_type=MESH)
                fw.start()
                passed.append(fw)
        for a in range(n):
            out = out_refs[a]
            h = out.shape[1] // 2
            for j, (cx, cy) in enumerate(chips):
                kj = 2 * cx + cy
                rows = out.at[kj, pl.ds((1 - c) * h, h), :]
                pltpu.make_async_remote_copy(
                    src_ref=rows, dst_ref=rows, send_sem=send_sems.at[6 * a + 3 + j],
                    recv_sem=recv_sems.at[6 * a + 3 + j], device_id=sibling, device_id_type=MESH).wait_recv()
        for cp in firsts + passed:
            cp.wait_send()

    return pl.pallas_call(
        body, name=name,
        in_specs=[ANY] * n, out_specs=[ANY] * n,
        out_shape=[_sds(s.shape, s.dtype) for s in shards],
        input_output_aliases={a: a for a in range(n)},
        scratch_shapes=[pltpu.SemaphoreType.DMA((6 * n,)), pltpu.SemaphoreType.DMA((6 * n,))],
    )(*shards)


def _sibling_halves(name, grads):
    n = len(grads)

    def body(*refs):
        g_refs = refs[:n]
        out_refs = refs[n:2 * n]
        send_sems, recv_sems = refs[2 * n:]
        x, y, c, _ = _place()
        sibling = (x, y, 1 - c)
        cps = []
        for a in range(n):
            h = g_refs[a].shape[1] // 2
            cp = pltpu.make_async_remote_copy(
                src_ref=g_refs[a].at[:, pl.ds((1 - c) * h, h), :], dst_ref=out_refs[a],
                send_sem=send_sems.at[a], recv_sem=recv_sems.at[a], device_id=sibling, device_id_type=MESH)
            cp.start()
            cps.append(cp)
        for cp in cps:
            cp.wait()

    return pl.pallas_call(
        body, name=name, in_specs=[ANY] * n, out_specs=[ANY] * n,
        out_shape=[_sds((g.shape[0], g.shape[1] // 2, g.shape[2]), g.dtype) for g in grads],
        scratch_shapes=[pltpu.SemaphoreType.DMA((n,)), pltpu.SemaphoreType.DMA((n,))],
    )(*grads)


def _scatter_chips(name, parts):
    n = len(parts)

    def body(*refs):
        p_refs = refs[:n]
        out_refs = refs[n:2 * n]
        send_sems, recv_sems = refs[2 * n:]
        x, y, c, chips = _place()
        cps = []
        for a in range(n):
            for j, (cx, cy) in enumerate(chips):
                cp = pltpu.make_async_remote_copy(
                    src_ref=p_refs[a].at[2 * cx + cy], dst_ref=out_refs[a].at[j],
                    send_sem=send_sems.at[3 * a + j], recv_sem=recv_sems.at[3 * a + j],
                    device_id=(cx, cy, c), device_id_type=MESH)
                cp.start()
                cps.append(cp)
        for cp in cps:
            cp.wait_recv()
        for cp in cps:
            cp.wait_send()

    return pl.pallas_call(
        body, name=name, in_specs=[ANY] * n, out_specs=[ANY] * n,
        out_shape=[_sds((3,) + p.shape[1:], p.dtype) for p in parts],
        scratch_shapes=[pltpu.SemaphoreType.DMA((3 * n,)), pltpu.SemaphoreType.DMA((3 * n,))],
    )(*parts)


def _swap_halves(name, fulls):
    n = len(fulls)

    def body(*refs):
        out_refs = refs[n:2 * n]
        send_sems, recv_sems = refs[2 * n:]
        x, y, c, _ = _place()
        sibling = (x, y, 1 - c)
        cps = []
        for a in range(n):
            h = out_refs[a].shape[0] // 2
            rows = out_refs[a].at[pl.ds(c * h, h), :]
            cp = pltpu.make_async_remote_copy(
                src_ref=rows, dst_ref=rows,
                send_sem=send_sems.at[a], recv_sem=recv_sems.at[a], device_id=sibling, device_id_type=MESH)
            cp.start()
            cps.append(cp)
        for a in range(n):
            h = out_refs[a].shape[0] // 2
            rows = out_refs[a].at[pl.ds((1 - c) * h, h), :]
            pltpu.make_async_remote_copy(
                src_ref=rows, dst_ref=rows,
                send_sem=send_sems.at[a], recv_sem=recv_sems.at[a], device_id=sibling, device_id_type=MESH).wait_recv()
        for cp in cps:
            cp.wait_send()

    return pl.pallas_call(
        body, name=name, in_specs=[ANY] * n, out_specs=[ANY] * n,
        out_shape=[_sds(s.shape, s.dtype) for s in fulls],
        input_output_aliases={a: a for a in range(n)},
        scratch_shapes=[pltpu.SemaphoreType.DMA((n,)), pltpu.SemaphoreType.DMA((n,))],
    )(*fulls)


def _small_allreduce(name, v):
    m_per, n = v.shape

    def body(x_ref, sum_ref, all_ref, send_sems, recv_sems, local_sem):
        x, y, c, chips = _place()
        me, sibling = (x, y, c), (x, y, 1 - c)

        def rows(px, py, pc):
            return all_ref.at[pl.ds((4 * px + 2 * py + pc) * m_per, m_per), :]

        def copy(kk, block, to, src=None):
            return pltpu.make_async_remote_copy(
                src_ref=rows(*block) if src is None else src, dst_ref=rows(*block),
                send_sem=send_sems.at[kk], recv_sem=recv_sems.at[kk], device_id=to, device_id_type=MESH)

        mine = pltpu.make_async_copy(x_ref, rows(*me), local_sem)
        mine.start()
        first = [copy(0, me, sibling, src=x_ref)]
        first += [copy(1 + j, me, (*chip, c), src=x_ref) for j, chip in enumerate(chips)]
        for cp in first:
            cp.start()
        passed = [copy(4 + j, (*chip, c), sibling) for j, chip in enumerate(chips)]
        for j, chip in enumerate(chips):
            copy(1 + j, (*chip, c), me).wait_recv()
            passed[j].start()
        copy(0, sibling, me).wait_recv()
        for j, chip in enumerate(chips):
            copy(4 + j, (*chip, 1 - c), me).wait_recv()
        for cp in first + passed:
            cp.wait_send()
        mine.wait()
        acc = all_ref[pl.ds(0, m_per), :]
        for d in range(1, 8):
            acc = acc + all_ref[pl.ds(d * m_per, m_per), :]
        sum_ref[...] = acc

    vm = pl.BlockSpec(memory_space=pltpu.VMEM)
    return pl.pallas_call(
        body, name=name, in_specs=[vm], out_specs=[vm, vm],
        out_shape=[_sds((m_per, n), F32), _sds((8 * m_per, n), F32)],
        scratch_shapes=[pltpu.SemaphoreType.DMA((7,)), pltpu.SemaphoreType.DMA((7,)), pltpu.SemaphoreType.DMA],
    )(v)[0]


def _in_layout(D, W6, nhb, Ls):
    nmain = W6 + 2 * D
    lay = []
    for k in range(N_CHIPS):
        g0, g1 = k * Ls, (k + 1) * Ls
        pieces = []
        a, b = max(g0, 0), min(g1, W6)
        if a < b:
            pieces.append((a - g0, b - g0, a))
        a, b = max(g0, W6 + nhb), min(g1, W6 + nhb + 2 * D)
        if a < b:
            pieces.append((a - g0, b - g0, a - nhb))
        a, b = max(g0, W6), min(g1, W6 + nhb)
        fpiece = (a - g0, b - g0, a - W6) if a < b else None
        assert fpiece is None or (b - a) == nhb
        main0 = min(p[2] for p in pieces)
        main1 = max(p[2] + p[1] - p[0] for p in pieces)
        lay.append(dict(pieces=pieces, f=fpiece, s=main0 // LANE, e=-(-main1 // LANE), main1=main1))
    assert sum(1 for l in lay if l["f"] is not None) == 1
    nbw = max(l["e"] - l["s"] + (1 if l["f"] else 0) for l in lay)
    for k in range(1, N_CHIPS):
        assert lay[k]["s"] >= lay[k - 1]["e"] - 1 and lay[k]["s"] > lay[k - 1]["s"]
    return lay, nbw, nmain


def _to_window(w, lay_k, nbw):
    D = w.shape[0]
    items = [(c0 - lay_k["s"] * LANE, l0, l1) for (l0, l1, c0) in lay_k["pieces"]]
    if lay_k["f"]:
        l0, l1, off = lay_k["f"]
        items.append(((lay_k["e"] - lay_k["s"]) * LANE + off, l0, l1))
    items.sort()
    cols, pos = [], 0
    for w0, l0, l1 in items:
        if w0 > pos:
            cols.append(jnp.zeros((D, w0 - pos), w.dtype))
        cols.append(w[:, l0:l1])
        pos = w0 + (l1 - l0)
    if pos < nbw * LANE:
        cols.append(jnp.zeros((D, nbw * LANE - pos), w.dtype))
    return jnp.concatenate(cols, axis=1)


def _from_window(win, lay_k):
    items = [(l0, c0 - lay_k["s"] * LANE, l1 - l0) for (l0, l1, c0) in lay_k["pieces"]]
    if lay_k["f"]:
        l0, l1, off = lay_k["f"]
        items.append((l0, (lay_k["e"] - lay_k["s"]) * LANE + off, l1 - l0))
    items.sort()
    return jnp.concatenate([win[:, w0:w0 + n] for (_, w0, n) in items], axis=1)


def _assemble_in(name, wins, lay, nbw, nmain):
    _, D, _ = wins.shape
    ncb = nmain // LANE + 1
    k1 = np.zeros(ncb, np.int32)
    i1 = np.zeros(ncb, np.int32)
    k2 = np.zeros(ncb, np.int32)
    i2 = np.zeros(ncb, np.int32)
    fl = np.zeros(ncb, np.int32)
    for b in range(ncb - 1):
        k = max(kk for kk in range(N_CHIPS) if lay[kk]["s"] <= b)
        k1[b], i1[b] = k, b - lay[k]["s"]
        if k >= 1 and b == lay[k]["s"] and lay[k - 1]["main1"] > b * LANE:
            k2[b], i2[b], fl[b] = k - 1, b - lay[k - 1]["s"], 1
    kf = [kk for kk in range(N_CHIPS) if lay[kk]["f"]][0]
    k1[ncb - 1], i1[ncb - 1] = kf, lay[kf]["e"] - lay[kf]["s"]

    def body(k1_ref, i1_ref, k2_ref, i2_ref, fl_ref, a_ref, b_ref, o_ref):
        b = pl.program_id(0)
        add = jnp.where(fl_ref[b] == 1, b_ref[...], jnp.zeros_like(b_ref))
        o_ref[...] = a_ref[...] + add

    gs = pltpu.PrefetchScalarGridSpec(
        num_scalar_prefetch=5, grid=(ncb,),
        in_specs=[pl.BlockSpec((None, D, LANE), lambda b, k1r, i1r, k2r, i2r, flr: (k1r[b], 0, i1r[b])),
                  pl.BlockSpec((None, D, LANE), lambda b, k1r, i1r, k2r, i2r, flr: (k2r[b], 0, i2r[b]))],
        out_specs=pl.BlockSpec((D, LANE), lambda b, k1r, i1r, k2r, i2r, flr: (0, b)))
    return pl.pallas_call(body, name=name, grid_spec=gs, out_shape=_sds((D, ncb * LANE), BF16),
                          compiler_params=_cp(("parallel",)))(
        jnp.asarray(k1), jnp.asarray(i1), jnp.asarray(k2), jnp.asarray(i2), jnp.asarray(fl), wins, wins)


def _a_specs_q(nh):
    q = pl.BlockSpec((GROUP, HEAD_DIM), lambda i, h: (i, h))
    ks = [pl.BlockSpec((GROUP, HEAD_DIM), functools.partial(
        lambda i, h, j: (jnp.maximum(i - (WIN_BLOCKS - 1) + j, 0), nh + h), j=j)) for j in range(WIN_BLOCKS)]
    vs = [pl.BlockSpec((GROUP, HEAD_DIM), functools.partial(
        lambda i, h, j: (jnp.maximum(i - (WIN_BLOCKS - 1) + j, 0), 2 * nh + h), j=j)) for j in range(WIN_BLOCKS)]
    return q, ks, vs


def _a_logits(q, k_refs, bias, i, scale):
    parts = [lax.dot_general(q, kr[...], NT, preferred_element_type=F32) for kr in k_refs]
    s = jnp.concatenate(parts, axis=1) * scale + bias
    col = lax.broadcasted_iota(jnp.int32, s.shape, 1)
    return jnp.where(col >= (WIN_BLOCKS - 1 - i) * GROUP, s, NEG_INF)


def _attn_a_fwd(name, qkv, bias2, nh):
    T = qkv.shape[0]
    ng = T // GROUP
    scale = HEAD_DIM ** -0.5

    def body(q_ref, *refs):
        k_refs = refs[:WIN_BLOCKS]
        v_refs = refs[WIN_BLOCKS:2 * WIN_BLOCKS]
        bias_ref, o_ref, lse_ref = refs[2 * WIN_BLOCKS:]
        i, h = pl.program_id(0), pl.program_id(1)

        @pl.when(h == 0)
        def _():
            lse_ref[...] = jnp.zeros_like(lse_ref)

        s = _a_logits(q_ref[...], k_refs, bias_ref[h], i, scale)
        m = jnp.max(s, axis=1, keepdims=True)
        p = jnp.exp(s - m)
        l = jnp.sum(p, axis=1, keepdims=True)
        pb = (p / l).astype(BF16)
        o = jnp.zeros((GROUP, HEAD_DIM), F32)
        for j in range(WIN_BLOCKS):
            o = o + jnp.dot(pb[:, j * GROUP:(j + 1) * GROUP], v_refs[j][...], preferred_element_type=F32)
        o_ref[...] = o.astype(BF16)
        _put_col(lse_ref, h, m + jnp.log(l))

    q_spec, k_specs, v_specs = _a_specs_q(nh)
    stat = pl.BlockSpec((GROUP, LANE), lambda i, h: (i, 0))
    return pl.pallas_call(
        body, name=name, grid=(ng, nh),
        in_specs=[q_spec] + k_specs + v_specs + [pl.BlockSpec((nh, GROUP, WIN), lambda i, h: (0, 0, 0))],
        out_specs=[pl.BlockSpec((GROUP, HEAD_DIM), lambda i, h: (i, h)), stat],
        out_shape=[_sds((T, nh * HEAD_DIM), BF16), _sds((T, LANE), F32)],
        compiler_params=_cp(("parallel", "arbitrary")),
    )(qkv, *([qkv] * (2 * WIN_BLOCKS)), bias2)


def _attn_a_dq(name, qkv, do, lse, bias2, nh):
    T = qkv.shape[0]
    ng = T // GROUP
    scale = HEAD_DIM ** -0.5

    def body(q_ref, *refs):
        k_refs = refs[:WIN_BLOCKS]
        v_refs = refs[WIN_BLOCKS:2 * WIN_BLOCKS]
        do_ref, lse_ref, bias_ref, dq_ref, delta_ref, db_ref = refs[2 * WIN_BLOCKS:]
        i, h = pl.program_id(0), pl.program_id(1)

        @pl.when(h == 0)
        def _():
            delta_ref[...] = jnp.zeros_like(delta_ref)

        @pl.when(i == 0)
        def _():
            db_ref[h] = jnp.zeros((GROUP, WIN), F32)

        s = _a_logits(q_ref[...], k_refs, bias_ref[h], i, scale)
        p = jnp.exp(s - _col_of(lse_ref[...], h))
        dov = do_ref[...]
        dp = jnp.concatenate([lax.dot_general(dov, vr[...], NT, preferred_element_type=F32) for vr in v_refs], axis=1)
        delta = jnp.sum(p * dp, axis=1, keepdims=True)
        ds = p * (dp - delta)
        db_ref[h] += ds
        dsb = ds.astype(BF16)
        dq = jnp.zeros((GROUP, HEAD_DIM), F32)
        for j in range(WIN_BLOCKS):
            dq = dq + jnp.dot(dsb[:, j * GROUP:(j + 1) * GROUP], k_refs[j][...], preferred_element_type=F32)
        dq_ref[...] = (dq * scale).astype(BF16)
        _put_col(delta_ref, h, delta)

    q_spec, k_specs, v_specs = _a_specs_q(nh)
    stat = pl.BlockSpec((GROUP, LANE), lambda i, h: (i, 0))
    full_b = pl.BlockSpec((nh, GROUP, WIN), lambda i, h: (0, 0, 0))
    return pl.pallas_call(
        body, name=name, grid=(ng, nh),
        in_specs=[q_spec] + k_specs + v_specs + [pl.BlockSpec((GROUP, HEAD_DIM), lambda i, h: (i, h)), stat, full_b],
        out_specs=[pl.BlockSpec((GROUP, HEAD_DIM), lambda i, h: (i, h)), stat, full_b],
        out_shape=[_sds((T, nh * HEAD_DIM), BF16), _sds((T, LANE), F32), _sds((nh, GROUP, WIN), F32)],
        compiler_params=_cp(("arbitrary", "arbitrary")),
    )(qkv, *([qkv] * (2 * WIN_BLOCKS)), do, lse, bias2)


def _attn_a_dkv(name, qkv, do, lse, delta, bias2, nh):
    T = qkv.shape[0]
    ng = T // GROUP
    scale = HEAD_DIM ** -0.5
    nj = WIN_BLOCKS

    def body(k_ref, v_ref, *refs):
        q_refs = refs[:nj]
        do_refs = refs[nj:2 * nj]
        lse_refs = refs[2 * nj:3 * nj]
        dl_refs = refs[3 * nj:4 * nj]
        bias_ref, dk_ref, dv_ref = refs[4 * nj:]
        r, h = pl.program_id(0), pl.program_id(1)
        kv, vv = k_ref[...], v_ref[...]
        bias = bias_ref[h]
        dk = jnp.zeros((GROUP, HEAD_DIM), F32)
        dv = jnp.zeros((GROUP, HEAD_DIM), F32)
        for j in range(nj):
            qv, dov = q_refs[j][...], do_refs[j][...]
            c0 = (nj - 1 - j) * GROUP
            s = lax.dot_general(qv, kv, NT, preferred_element_type=F32) * scale + bias[:, c0:c0 + GROUP]
            p = jnp.exp(s - _col_of(lse_refs[j][...], h))
            p = jnp.where(r + j <= ng - 1, p, 0.0)
            dp = lax.dot_general(dov, vv, NT, preferred_element_type=F32)
            ds = p * (dp - _col_of(dl_refs[j][...], h))
            dv = dv + lax.dot_general(p.astype(BF16), dov, TN, preferred_element_type=F32)
            dk = dk + lax.dot_general(ds.astype(BF16), qv, TN, preferred_element_type=F32)
        dk_ref[...] = (dk * scale).astype(BF16)
        dv_ref[...] = dv.astype(BF16)

    def qmap(j):
        return functools.partial(lambda r, h, j: (jnp.minimum(r + j, ng - 1), h), j=j)

    def smap(j):
        return functools.partial(lambda r, h, j: (jnp.minimum(r + j, ng - 1), 0), j=j)

    blk = (GROUP, HEAD_DIM)
    in_specs = ([pl.BlockSpec(blk, lambda r, h: (r, nh + h)), pl.BlockSpec(blk, lambda r, h: (r, 2 * nh + h))]
                + [pl.BlockSpec(blk, qmap(j)) for j in range(nj)]
                + [pl.BlockSpec(blk, qmap(j)) for j in range(nj)]
                + [pl.BlockSpec((GROUP, LANE), smap(j)) for j in range(nj)]
                + [pl.BlockSpec((GROUP, LANE), smap(j)) for j in range(nj)]
                + [pl.BlockSpec((nh, GROUP, WIN), lambda r, h: (0, 0, 0))])
    out = pl.BlockSpec(blk, lambda r, h: (r, h))
    return pl.pallas_call(
        body, name=name, grid=(ng, nh), in_specs=in_specs, out_specs=[out, out],
        out_shape=[_sds((T, nh * HEAD_DIM), BF16)] * 2,
        compiler_params=_cp(("parallel", "parallel")),
    )(qkv, qkv, *([qkv] * nj), *([do] * nj), *([lse] * nj), *([delta] * nj), bias2)


def _fox_prep(name, f, b_f):
    T = f.shape[0]
    tb = _pick(T, (256, 128))

    def body(f_ref, b_ref, cum_ref, cumt_ref, carry_ref):
        @pl.when(pl.program_id(0) == 0)
        def _():
            carry_ref[...] = jnp.zeros_like(carry_ref)

        z = f_ref[...] + b_ref[...]
        logf = jnp.minimum(z, 0.0) - jnp.log(1.0 + jnp.exp(-jnp.abs(z)))
        row = lax.broadcasted_iota(jnp.int32, (tb, tb), 0)
        col = lax.broadcasted_iota(jnp.int32, (tb, tb), 1)
        tri = (row >= col).astype(BF16)
        acc = jnp.zeros((tb, LANE), F32)
        for piece in _split3(logf):
            acc = acc + jnp.dot(tri, piece, preferred_element_type=F32)
        cum = acc + carry_ref[...]
        cum_ref[...] = cum
        cumt_ref[...] = cum.T
        carry_ref[...] = cum_ref[pl.ds(tb - 1, 1), :]

    return pl.pallas_call(
        body, name=name, grid=(T // tb,),
        in_specs=[pl.BlockSpec((tb, LANE), lambda i: (i, 0)), pl.BlockSpec((1, LANE), lambda i: (0, 0))],
        out_specs=[pl.BlockSpec((tb, LANE), lambda i: (i, 0)), pl.BlockSpec((LANE, tb), lambda i: (0, i))],
        out_shape=[_sds((T, LANE), F32), _sds((LANE, T), F32)],
        scratch_shapes=[pltpu.VMEM((1, LANE), F32)],
        compiler_params=_cp(("arbitrary",)),
    )(f, b_f)


def _fox_blk(T):
    return _pick(T, (256, 128))


def _fox_mask(s, i, j, tq, tk):
    qpos = i * tq + lax.broadcasted_iota(jnp.int32, s.shape, 0)
    kpos = j * tk + lax.broadcasted_iota(jnp.int32, s.shape, 1)
    return jnp.where(kpos <= qpos, s, NEG_INF)


def _fox_fwd(name, qkv, cum, cumt, nh):
    T = qkv.shape[0]
    tq = tk = _fox_blk(T)
    scale = HEAD_DIM ** -0.5

    def body(q_ref, k_ref, v_ref, cum_ref, cumt_ref, o_ref, lse_ref):
        i, h = pl.program_id(0), pl.program_id(1)

        @pl.when(h == 0)
        def _():
            lse_ref[...] = jnp.zeros_like(lse_ref)

        q = q_ref[...]
        cq = _col_of(cum_ref[...], h)

        def step(j, carry):
            m, l, acc = carry
            k0 = pl.multiple_of(j * tk, tk)
            kj = k_ref[pl.ds(k0, tk), :]
            vj = v_ref[pl.ds(k0, tk), :]
            ck = cumt_ref[pl.ds(h, 1), pl.ds(k0, tk)]
            s = lax.dot_general(q, kj, NT, preferred_element_type=F32) * scale + cq - ck
            s = _fox_mask(s, i, j, tq, tk)
            m_new = jnp.maximum(m, jnp.max(s, axis=1, keepdims=True))
            alpha = jnp.exp(m - m_new)
            p = jnp.exp(s - m_new)
            l = alpha * l + jnp.sum(p, axis=1, keepdims=True)
            acc = alpha * acc + jnp.dot(p.astype(BF16), vj, preferred_element_type=F32)
            return m_new, l, acc

        init = (jnp.full((tq, 1), NEG_INF, F32), jnp.zeros((tq, 1), F32), jnp.zeros((tq, HEAD_DIM), F32))
        m, l, acc = lax.fori_loop(0, i + 1, step, init)
        o_ref[...] = (acc / l).astype(BF16)
        _put_col(lse_ref, h, m + jnp.log(l))

    return pl.pallas_call(
        body, name=name, grid=(T // tq, nh),
        in_specs=[pl.BlockSpec((tq, HEAD_DIM), lambda i, h: (i, 3 * nh + h)),
                  pl.BlockSpec((T, HEAD_DIM), lambda i, h: (0, 4 * nh + h)),
                  pl.BlockSpec((T, HEAD_DIM), lambda i, h: (0, 5 * nh + h)),
                  pl.BlockSpec((tq, LANE), lambda i, h: (i, 0)),
                  pl.BlockSpec((LANE, T), lambda i, h: (0, 0))],
        out_specs=[pl.BlockSpec((tq, HEAD_DIM), lambda i, h: (i, h)), pl.BlockSpec((tq, LANE), lambda i, h: (i, 0))],
        out_shape=[_sds((T, nh * HEAD_DIM), BF16), _sds((T, LANE), F32)],
        compiler_params=_cp(("parallel", "arbitrary")),
    )(qkv, qkv, qkv, cum, cumt)


def _fox_dq(name, qkv, do, lse, cum, cumt, nh):
    T = qkv.shape[0]
    tq = tk = _fox_blk(T)
    scale = HEAD_DIM ** -0.5

    def body(q_ref, k_ref, v_ref, do_ref, lse_ref, cum_ref, cumt_ref, dq_ref, delta_ref):
        i, h = pl.program_id(0), pl.program_id(1)

        @pl.when(h == 0)
        def _():
            delta_ref[...] = jnp.zeros_like(delta_ref)

        q = q_ref[...]
        dov = do_ref[...]
        cq = _col_of(cum_ref[...], h)
        lse = _col_of(lse_ref[...], h)

        def p_dp(j):
            k0 = pl.multiple_of(j * tk, tk)
            kj = k_ref[pl.ds(k0, tk), :]
            vj = v_ref[pl.ds(k0, tk), :]
            ck = cumt_ref[pl.ds(h, 1), pl.ds(k0, tk)]
            s = lax.dot_general(q, kj, NT, preferred_element_type=F32) * scale + cq - ck
            p = jnp.exp(_fox_mask(s, i, j, tq, tk) - lse)
            return p, lax.dot_general(dov, vj, NT, preferred_element_type=F32), kj

        def sweep_delta(j, delta):
            p, dp, _ = p_dp(j)
            return delta + jnp.sum(p * dp, axis=1, keepdims=True)

        delta = lax.fori_loop(0, i + 1, sweep_delta, jnp.zeros((tq, 1), F32))

        def sweep_dq(j, dq):
            p, dp, kj = p_dp(j)
            ds = p * (dp - delta)
            return dq + jnp.dot(ds.astype(BF16), kj, preferred_element_type=F32)

        dq = lax.fori_loop(0, i + 1, sweep_dq, jnp.zeros((tq, HEAD_DIM), F32))
        dq_ref[...] = (dq * scale).astype(BF16)
        _put_col(delta_ref, h, delta)

    blk = pl.BlockSpec((tq, HEAD_DIM), lambda i, h: (i, h))
    stat = pl.BlockSpec((tq, LANE), lambda i, h: (i, 0))
    return pl.pallas_call(
        body, name=name, grid=(T // tq, nh),
        in_specs=[pl.BlockSpec((tq, HEAD_DIM), lambda i, h: (i, 3 * nh + h)),
                  pl.BlockSpec((T, HEAD_DIM), lambda i, h: (0, 4 * nh + h)),
                  pl.BlockSpec((T, HEAD_DIM), lambda i, h: (0, 5 * nh + h)),
                  blk, stat, stat, pl.BlockSpec((LANE, T), lambda i, h: (0, 0))],
        out_specs=[blk, stat],
        out_shape=[_sds((T, nh * HEAD_DIM), BF16), _sds((T, LANE), F32)],
        compiler_params=_cp(("parallel", "arbitrary")),
    )(qkv, qkv, qkv, do, lse, cum, cumt)


def _fox_dkv(name, qkv, do, lse, delta, cum, cumt, nh):
    T = qkv.shape[0]
    tq = tk = _fox_blk(T)
    nq = T // tq
    scale = HEAD_DIM ** -0.5

    def body(k_ref, v_ref, q_ref, do_ref, lse_ref, dl_ref, cum_ref, cumt_ref, dk_ref, dv_ref, dc_ref):
        j, h = pl.program_id(0), pl.program_id(1)

        @pl.when(h == 0)
        def _():
            dc_ref[...] = jnp.zeros_like(dc_ref)

        kj, vj = k_ref[...], v_ref[...]
        k0 = pl.multiple_of(j * tk, tk)
        ck = cumt_ref[pl.ds(h, 1), pl.ds(k0, tk)]

        def step(i, carry):
            dk, dv, dc = carry
            q0 = pl.multiple_of(i * tq, tq)
            qi = q_ref[pl.ds(q0, tq), :]
            doi = do_ref[pl.ds(q0, tq), :]
            cq = _col_of(cum_ref[pl.ds(q0, tq), :], h)
            lse = _col_of(lse_ref[pl.ds(q0, tq), :], h)
            dl = _col_of(dl_ref[pl.ds(q0, tq), :], h)
            s = lax.dot_general(qi, kj, NT, preferred_element_type=F32) * scale + cq - ck
            s = _fox_mask(s, i, j, tq, tk)
            p = jnp.exp(s - lse)
            dp = lax.dot_general(doi, vj, NT, preferred_element_type=F32)
            ds = p * (dp - dl)
            dv = dv + lax.dot_general(p.astype(BF16), doi, TN, preferred_element_type=F32)
            dk = dk + lax.dot_general(ds.astype(BF16), qi, TN, preferred_element_type=F32)
            dc = dc - jnp.sum(ds, axis=0, keepdims=True)
            return dk, dv, dc

        init = (jnp.zeros((tk, HEAD_DIM), F32), jnp.zeros((tk, HEAD_DIM), F32), jnp.zeros((1, tk), F32))
        dk, dv, dc = lax.fori_loop(j, nq, step, init)
        dk_ref[...] = (dk * scale).astype(BF16)
        dv_ref[...] = dv.astype(BF16)
        sub = lax.broadcasted_iota(jnp.int32, (LANE, tk), 0)
        dc_ref[...] = jnp.where(sub == h, dc, dc_ref[...])

    whole = lambda c: pl.BlockSpec((T, HEAD_DIM), c)
    stat = pl.BlockSpec((T, LANE), lambda j, h: (0, 0))
    out = pl.BlockSpec((tk, HEAD_DIM), lambda j, h: (j, h))
    return pl.pallas_call(
        body, name=name, grid=(T // tk, nh),
        in_specs=[pl.BlockSpec((tk, HEAD_DIM), lambda j, h: (j, 4 * nh + h)),
                  pl.BlockSpec((tk, HEAD_DIM), lambda j, h: (j, 5 * nh + h)),
                  whole(lambda j, h: (0, 3 * nh + h)), whole(lambda j, h: (0, h)),
                  stat, stat, stat, pl.BlockSpec((LANE, T), lambda j, h: (0, 0))],
        out_specs=[out, out, pl.BlockSpec((LANE, tk), lambda j, h: (0, j))],
        out_shape=[_sds((T, nh * HEAD_DIM), BF16)] * 2 + [_sds((LANE, T), F32)],
        compiler_params=_cp(("parallel", "arbitrary")),
    )(qkv, qkv, qkv, do, lse, delta, cum, cumt)


def _fox_post(name, dcumt, f, b_f):
    T = f.shape[0]
    tb = _pick(T, (256, 128))
    nb = T // tb

    def body(dc_ref, f_ref, b_ref, df_ref, gb_ref, carry_ref):
        @pl.when(pl.program_id(0) == 0)
        def _():
            carry_ref[...] = jnp.zeros_like(carry_ref)
            gb_ref[...] = jnp.zeros_like(gb_ref)

        dc = dc_ref[...]
        row = lax.broadcasted_iota(jnp.int32, (tb, tb), 0)
        col = lax.broadcasted_iota(jnp.int32, (tb, tb), 1)
        tri = (row >= col).astype(BF16)
        acc = jnp.zeros((LANE, tb), F32)
        for piece in _split3(dc):
            acc = acc + jnp.dot(piece, tri, preferred_element_type=F32)
        dlogf = (acc + carry_ref[...]).T
        carry_ref[...] += jnp.sum(dc, axis=1, keepdims=True)
        z = f_ref[...] + b_ref[...]
        df = dlogf * _sigmoid(-z)
        df_ref[...] = df.astype(BF16)
        gb_ref[...] += jnp.sum(df, axis=0, keepdims=True)

    return pl.pallas_call(
        body, name=name, grid=(nb,),
        in_specs=[pl.BlockSpec((LANE, tb), lambda g: (0, nb - 1 - g)),
                  pl.BlockSpec((tb, LANE), lambda g: (nb - 1 - g, 0)),
                  pl.BlockSpec((1, LANE), lambda g: (0, 0))],
        out_specs=[pl.BlockSpec((tb, LANE), lambda g: (nb - 1 - g, 0)), pl.BlockSpec((1, LANE), lambda g: (0, 0))],
        out_shape=[_sds((T, LANE), BF16), _sds((1, LANE), F32)],
        scratch_shapes=[pltpu.VMEM((LANE, 1), F32)],
        compiler_params=_cp(("arbitrary",)),
    )(dcumt, f, b_f)


def _rel_tables(n_rel):
    max_rel = (n_rel - 1) // 2
    nj = GROUP + WIN - 1
    onehot = np.zeros((n_rel, nj), np.float32)
    for j in range(nj):
        dist = (WIN - 1) - j
        onehot[int(np.clip(dist, -max_rel, max_rel)) + max_rel, j] = 1.0
    a = np.arange(GROUP)[:, None]
    kb = np.arange(WIN)[None, :]
    lo = CHUNK * (a // CHUNK)
    inband = (kb >= lo) & (kb < lo + BAND)
    return onehot, inband


def _bias2_of(rel_bias, onehot, inband):
    bv = jnp.dot(rel_bias, jnp.asarray(onehot), precision=lax.Precision.HIGHEST)
    rows = [bv[:, GROUP - 1 - a:GROUP - 1 - a + WIN] for a in range(GROUP)]
    toe = jnp.stack(rows, axis=1)
    return jnp.where(jnp.asarray(inband)[None], toe, NEG_INF)


def _rel_grad_of(dbias2, onehot):
    nj = GROUP + WIN - 1
    dbv = sum(jnp.pad(dbias2[:, a, :], ((0, 0), (GROUP - 1 - a, nj - WIN - (GROUP - 1 - a)))) for a in range(GROUP))
    return jnp.dot(dbv, jnp.asarray(onehot).T, precision=lax.Precision.HIGHEST)


def _reduce_grads(tag, grads):
    c = lax.axis_index("c")
    chip = 2 * lax.axis_index("x") + lax.axis_index("y")
    theirs = _sibling_halves(f"rs_sibling_{tag}", grads)
    parts = [_add_bf16(f"rs_add_{tag}_{a}", g, c, t) for a, (g, t) in enumerate(zip(grads, theirs))]
    got = _scatter_chips(f"rs_scatter_{tag}", parts)
    fulls = [_sum4(f"rs_sum_{tag}_{a}", gt, p, chip, c) for a, (gt, p) in enumerate(zip(got, parts))]
    return _swap_halves(f"rs_swap_{tag}", fulls)


def kernel(x, g_mix, w_in, b_f, b_gate, rel_bias, w_branch_a, w_branch_b, w_out, g_ffn, w_gate_ffn, w_up_ffn, w_down_ffn, g_final, loss_target, m_g_mix, m_w_in, m_b_f, m_b_gate, m_rel_bias, m_w_branch_a, m_w_branch_b, m_w_out, m_g_ffn, m_w_gate_ffn, m_w_up_ffn, m_w_down_ffn, m_g_final, v_g_mix, v_w_in, v_b_f, v_b_gate, v_rel_bias, v_w_branch_a, v_w_branch_b, v_w_out, v_g_ffn, v_w_gate_ffn, v_w_up_ffn, v_w_down_ffn, v_g_final):
    T, D = x.shape[1], x.shape[2]
    Ls = w_in.shape[2]
    W = w_branch_a.shape[1]
    nh = W // HEAD_DIM
    nhb = b_f.shape[1]
    assert w_branch_b.shape[1] == W and nhb == nh and rel_bias.shape[1] == nh
    W6 = 6 * W
    Fl = w_gate_ffn.shape[2]
    Fp = -(-Fl // LANE) * LANE
    n_rel = rel_bias.shape[2]
    chip = 2 * lax.axis_index("x") + lax.axis_index("y")
    lay, nbw, nmain = _in_layout(D, W6, nhb, Ls)
    onehot, inband = _rel_tables(n_rel)

    xs, tgt = x[0], loss_target[0]

    win_f32 = lax.switch(chip, [functools.partial(_to_window, lay_k=lay[k], nbw=nbw) for k in range(N_CHIPS)], w_in[0])
    pad_c = lambda w: jnp.pad(w, ((0, 0), (0, Fp - Fl)))
    pad_r = lambda w: jnp.pad(w, ((0, Fp - Fl), (0, 0)))
    sh_in = _cast_bf16("cast_w_in", win_f32, chip)
    sh_a = _cast_bf16("cast_w_a", w_branch_a[0], chip)
    sh_b = _cast_bf16("cast_w_b", w_branch_b[0], chip)
    sh_o = _cast_bf16("cast_w_out", w_out[0], chip)
    sh_g = _cast_bf16("cast_w_gate", pad_c(w_gate_ffn[0]), chip)
    sh_u = _cast_bf16("cast_w_up", pad_c(w_up_ffn[0]), chip)
    sh_d = _cast_bf16("cast_w_down", pad_r(w_down_ffn[0]), chip)
    (wins,) = _allgather("ag_w_in", [sh_in])
    wa_g, wb_g, wo_g = _allgather("ag_w_mix", [sh_a, sh_b, sh_o])
    wg_g, wu_g, wd_g = _allgather("ag_w_ffn", [sh_g, sh_u, sh_d])
    wc = _assemble_in("assemble_w_in", wins, lay, nbw, nmain)
    wo_full = wo_g.reshape(D, D)
    wd_full = wd_g.reshape(N_CHIPS * Fp, D)

    h1, r1 = _rms_fwd("rms1", xs, g_mix)
    qkv = _mm_nn("proj_qkv", h1, wc, BF16, b_col0=0, n=W6)
    gates = _mm_nn("proj_gates", h1, wc, F32, b_col0=W6, n=2 * D)
    fl = _mm_nn("proj_f", h1, wc, F32, b_col0=nmain, n=LANE, tn=LANE)
    bias2 = _bias2_of(rel_bias[0], onehot, inband)
    bf_pad = jnp.pad(b_f, ((0, 0), (0, LANE - nhb)))
    o_a, lse_a = _attn_a_fwd("attn_a_fwd", qkv, bias2, nh)
    cum, cumt = _fox_prep("fox_prep", fl, bf_pad)
    o_b, lse_b = _fox_fwd("fox_fwd", qkv, cum, cumt, nh)
    u_a = _mm_nn("branch_a", o_a, wa_g, F32)
    u_b = _mm_nn("branch_b", o_b, wb_g, F32)
    merged = _merge_fwd("merge", gates, u_a, u_b, b_gate)
    x1 = _mm_nn("out_proj", merged, wo_full, F32, residual=xs)
    h2, r2 = _rms_fwd("rms2", x1, g_ffn)

    tm_f = _pick(T, (512, 256, 128))
    tn_f = _pick(Fp, (1408, 1024, 512, 256, 128))
    tk_f = _pick(D, (1024, 512, 256, 128))
    nps_f = Fp // tn_f

    def swiglu_ep(accs, e_refs, o_refs):
        g, u = accs
        o_refs[0][...] = g.astype(BF16)
        o_refs[1][...] = u.astype(BF16)
        o_refs[2][...] = (g * _sigmoid(g) * u).astype(BF16)

    hid_spec = pl.BlockSpec((tm_f, tn_f), lambda i, j, k: (i, j))
    wcol_spec = pl.BlockSpec((None, tk_f, tn_f), lambda i, j, k: (j // nps_f, k, j % nps_f))
    gate, up, hidden = _mm(
        "ffn_up", "nn", [h2], [pl.BlockSpec((tm_f, tk_f), lambda i, j, k: (i, k))], [wg_g, wu_g], [wcol_spec, wcol_spec],
        [(0, 0, 0), (0, 1, 1)], 2, (T // tm_f, N_CHIPS * Fp // tn_f, D // tk_f), tm_f, tn_f,
        [_sds((T, N_CHIPS * Fp), BF16)] * 3, [hid_spec] * 3, swiglu_ep)
    x2 = _mm_nn("ffn_down", hidden, wd_full, F32, residual=x1, tk=_pick(N_CHIPS * Fp, (1408, 1024, 512, 256, 128)))

    dx2, dx2b, loss_part, gg_final = _final_loss_bwd("final_loss", x2, tgt, g_final.reshape(1, D))

    def swiglu_bwd_ep(accs, e_refs, o_refs):
        dh = accs[0]
        g = e_refs[0][...].astype(F32)
        u = e_refs[1][...].astype(F32)
        sg = _sigmoid(g)
        o_refs[0][...] = (dh * u * (sg * (1.0 + g * (1.0 - sg)))).astype(BF16)
        o_refs[1][...] = (dh * (g * sg)).astype(BF16)

    tk_b = _pick(D, (1024, 512, 256, 128))
    dgate, dup = _mm(
        "ffn_down_bwd", "nt", [dx2b], [pl.BlockSpec((tm_f, tk_b), lambda i, j, k: (i, k))],
        [wd_full], [pl.BlockSpec((tn_f, tk_b), lambda i, j, k: (j, k))], [(0, 0, 0)], 1,
        (T // tm_f, N_CHIPS * Fp // tn_f, D // tk_b), tm_f, tn_f,
        [_sds((T, N_CHIPS * Fp), BF16)] * 2, [hid_spec] * 2, swiglu_bwd_ep,
        extra=[gate, up], extra_specs=[hid_spec, hid_spec])
    dwd = _mm_tn("dw_down", hidden, dx2b, BF16, tm=_pick(N_CHIPS * Fp, (1408, 1024, 512, 256, 128)))
    dh2 = _mm_nt("ffn_up_bwd", [dgate, dup], [wg_g, wu_g], F32)
    dwg = _mm_tn("dw_gate", h2, dgate, BF16, slots=N_CHIPS)
    dwu = _mm_tn("dw_up", h2, dup, BF16, slots=N_CHIPS)
    dx1, dx1b, gg_ffn = _rms_bwd("rms2_bwd", [dh2], x1, r2, g_ffn, dx2, True)

    dmerged = _mm_nt("out_proj_bwd", [dx1b], [wo_full], F32)
    dwo = _mm_tn("dw_out", merged, dx1b, BF16)
    du_a, du_b, dga, dgb, gbg_a, gbg_b = _merge_bwd("merge_bwd", dmerged, gates, u_a, u_b, b_gate)
    do_a = _mm_nt("branch_a_bwd", [du_a], [wa_g], BF16)
    do_b = _mm_nt("branch_b_bwd", [du_b], [wb_g], BF16)
    dwa = _mm_tn("dw_a", o_a, du_a, BF16, slots=N_CHIPS)
    dwb = _mm_tn("dw_b", o_b, du_b, BF16, slots=N_CHIPS)

    dq_a, delta_a, dbias2 = _attn_a_dq("attn_a_dq", qkv, do_a, lse_a, bias2, nh)
    dk_a, dv_a = _attn_a_dkv("attn_a_dkv", qkv, do_a, lse_a, delta_a, bias2, nh)
    dq_b, delta_b = _fox_dq("fox_dq", qkv, do_b, lse_b, cum, cumt, nh)
    dk_b, dv_b, dcumt = _fox_dkv("fox_dkv", qkv, do_b, lse_b, delta_b, cum, cumt, nh)
    df, gbf = _fox_post("fox_post", dcumt, fl, bf_pad)

    dqkv = jnp.concatenate([dq_a, dk_a, dv_a, dq_b, dk_b, dv_b], axis=1)
    dgates = jnp.concatenate([dga, dgb], axis=1)
    dh_q = _mm_nt("proj_qkv_bwd", [dqkv], [wc], F32, k0_list=[0], tk=_pick(W6, (1024, 512, 256, 128)))
    dh_g = _mm_nt("proj_gates_bwd", [dgates], [wc], F32, k0_list=[W6], tk=_pick(math_gcd(W6, 2 * D), (1024, 512, 256, 128)))
    dh_f = _mm_nt("proj_f_bwd", [df], [wc], F32, k0_list=[nmain], tk=LANE)
    dwc_q = _mm_tn("dw_in_qkv", h1, dqkv, BF16)
    dwc_g = _mm_tn("dw_in_gates", h1, dgates, BF16)
    dwc_f = _mm_tn("dw_in_f", h1, df, BF16, tn=LANE)
    grad_x, gg_mix = _rms_bwd("rms1_bwd", [dh_q, dh_g, dh_f], xs, r1, g_mix, dx1, False)

    dwc = jnp.concatenate([dwc_q, dwc_g, dwc_f], axis=1)
    zeros_blk = jnp.zeros((D, LANE), BF16)
    win_parts = []
    for k in range(N_CHIPS):
        cols = [dwc[:, lay[k]["s"] * LANE:lay[k]["e"] * LANE]]
        nb = lay[k]["e"] - lay[k]["s"]
        if lay[k]["f"]:
            cols.append(dwc[:, nmain:nmain + LANE])
            nb += 1
        cols += [zeros_blk] * (nbw - nb)
        win_parts.append(jnp.concatenate(cols, axis=1) if len(cols) > 1 else cols[0])
    dwin = jnp.stack(win_parts, axis=0)
    (g_win,) = _reduce_grads("in", [dwin])
    g_a, g_b, g_o = _reduce_grads("mix", [dwa, dwb, dwo.reshape(N_CHIPS, D // N_CHIPS, D)])
    g_g, g_u, g_d = _reduce_grads("ffn", [dwg, dwu, dwd.reshape(N_CHIPS, Fp, D)])
    g_in = lax.switch(chip, [functools.partial(_from_window, lay_k=lay[k]) for k in range(N_CHIPS)], g_win)
    g_g, g_u, g_d = g_g[:, :Fl], g_u[:, :Fl], g_d[:Fl, :]

    big = {}
    for nm, w, g, m, v in (("w_in", w_in, g_in, m_w_in, v_w_in), ("w_branch_a", w_branch_a, g_a, m_w_branch_a, v_w_branch_a),
                           ("w_branch_b", w_branch_b, g_b, m_w_branch_b, v_w_branch_b), ("w_out", w_out, g_o, m_w_out, v_w_out),
                           ("w_gate_ffn", w_gate_ffn, g_g, m_w_gate_ffn, v_w_gate_ffn),
                           ("w_up_ffn", w_up_ffn, g_u, m_w_up_ffn, v_w_up_ffn),
                           ("w_down_ffn", w_down_ffn, g_d, m_w_down_ffn, v_w_down_ffn)):
        d, mn, vn = _adamw(f"adamw_{nm}", w[0], g, m[0], v[0])
        big[nm] = (g[None], d[None], mn[None], vn[None])

    g_rel = _rel_grad_of(dbias2, onehot)
    small = [("loss", loss_part[:, :1], None, None, None),
             ("g_mix", gg_mix, g_mix, m_g_mix, v_g_mix), ("b_f", gbf[:, :nhb], b_f, m_b_f, v_b_f),
             ("b_gate", jnp.concatenate([gbg_a, gbg_b], axis=1), b_gate, m_b_gate, v_b_gate),
             ("rel_bias", g_rel, rel_bias, m_rel_bias, v_rel_bias), ("g_ffn", gg_ffn, g_ffn, m_g_ffn, v_g_ffn),
             ("g_final", gg_final, g_final, m_g_final, v_g_final)]
    sizes = [int(np.prod(s[1].shape)) for s in small]
    total = sum(sizes)
    npad = -(-total // 1024) * 1024

    def pack(arrs):
        flat = jnp.concatenate([a.reshape(-1).astype(F32) for a in arrs])
        return jnp.pad(flat, (0, npad - total)).reshape(8, npad // 8)

    zero1 = jnp.zeros((1,), F32)
    g_all = _small_allreduce("small_allreduce", pack([s[1] for s in small]))
    w_s = pack([zero1 if s[2] is None else s[2] for s in small])
    m_s = pack([zero1 if s[3] is None else s[3] for s in small])
    v_s = pack([zero1 + 1.0 if s[4] is None else s[4] for s in small])
    d_s, mn_s, vn_s = _adamw("adamw_small", w_s, g_all, m_s, v_s)

    def unpack(packed):
        flat = packed.reshape(-1)
        out, pos = {}, 0
        for s, n in zip(small, sizes):
            if s[2] is not None:
                out[s[0]] = flat[pos:pos + n].reshape(s[2].shape)
            else:
                out[s[0]] = flat[pos:pos + n].reshape(())
            pos += n
        return out

    gs, ds, ms, vs = unpack(g_all), unpack(d_s), unpack(mn_s), unpack(vn_s)
    order = ["g_mix", "w_in", "b_f", "b_gate", "rel_bias", "w_branch_a", "w_branch_b", "w_out", "g_ffn",
             "w_gate_ffn", "w_up_ffn", "w_down_ffn", "g_final"]
    res = [[], [], [], []]
    for nm in order:
        four = big[nm] if nm in big else (gs[nm], ds[nm], ms[nm], vs[nm])
        for q in range(4):
            res[q].append(four[q])
    return (gs["loss"], grad_x[None], *res[0], *res[1], *res[2], *res[3])


def math_gcd(a, b):
    while b:
        a, b = b, a % b
    return a
```

```python
import functools

import numpy as np
import jax
import jax.numpy as jnp
from jax import lax
from jax.experimental import pallas as pl
from jax.experimental.pallas import tpu as pltpu

F32 = jnp.float32
BF16 = jnp.bfloat16
LANE = 128
HEAD_DIM = 128
CHUNK = 64
LEFT_CHUNKS = 8
GROUP = 128
WIN_BLOCKS = 5
WIN = WIN_BLOCKS * GROUP
BAND = (LEFT_CHUNKS + 1) * CHUNK
RMS_EPS = 1e-6
NEG_INF = -1e30
ADAM_LR = 0.001
ADAM_B1 = 0.9
ADAM_B2 = 0.999
ADAM_EPS = 1e-08
ADAM_WD = 0.01
ADAM_STEP = 10
N_CHIPS = 4
MESH = pl.DeviceIdType.MESH
VMEM_LIMIT = 52 * 1024 * 1024
ANY = pl.BlockSpec(memory_space=pl.ANY)

NN = (((1,), (0,)), ((), ()))
NT = (((1,), (1,)), ((), ()))
TN = (((0,), (0,)), ((), ()))


def _cp(sem):
    return pltpu.CompilerParams(dimension_semantics=sem, vmem_limit_bytes=VMEM_LIMIT)


def _sds(shape, dtype):
    return jax.ShapeDtypeStruct(shape, dtype)


def _pick(n, prefs):
    for p in prefs:
        if n % p == 0:
            return p
    return n


def _sigmoid(v):
    return 1.0 / (1.0 + jnp.exp(-v))


def _split3(v):
    hi = v.astype(BF16)
    r1 = v - hi.astype(F32)
    mid = r1.astype(BF16)
    lo = (r1 - mid.astype(F32)).astype(BF16)
    return hi, mid, lo


def _col_of(blk, h):
    lane = lax.broadcasted_iota(jnp.int32, blk.shape, 1)
    return jnp.sum(jnp.where(lane == h, blk, 0.0), axis=1, keepdims=True)


def _put_col(ref, h, col):
    lane = lax.broadcasted_iota(jnp.int32, ref.shape, 1)
    ref[...] = jnp.where(lane == h, col, ref[...])


def _mm(name, mode, a_list, a_specs, b_list, b_specs, pairs, n_acc, grid, tm, tn,
        out_shapes, out_specs, epilogue, extra=(), extra_specs=(), jobs=()):
    n_a, n_b, n_e, n_o = len(a_list), len(b_list), len(extra), len(out_shapes)
    nk = grid[2]
    dn = {"nn": NN, "nt": NT, "tn": TN}[mode]

    def body(*refs):
        a_refs = refs[:n_a]
        b_refs = refs[n_a:n_a + n_b]
        e_refs = refs[n_a + n_b:n_a + n_b + n_e]
        o_refs = refs[n_a + n_b + n_e:n_a + n_b + n_e + n_o]
        acc_refs = refs[n_a + n_b + n_e + n_o:]
        k = pl.program_id(2)

        @pl.when(k == 0)
        def _():
            for acc in acc_refs:
                acc[...] = jnp.zeros_like(acc)

        for ai, bi, ci in pairs:
            acc_refs[ci][...] += lax.dot_general(a_refs[ai][...], b_refs[bi][...], dn,
                                                 preferred_element_type=F32)

        @pl.when(k == nk - 1)
        def _():
            epilogue([acc[...] for acc in acc_refs], e_refs, o_refs)

    return _pcall(
        body, name=name, grid=grid,
        in_specs=list(a_specs) + list(b_specs) + list(extra_specs),
        out_specs=list(out_specs), out_shape=list(out_shapes),
        scratch_shapes=[pltpu.VMEM((tm, tn), F32) for _ in range(n_acc)],
        sem=("parallel", "parallel", "arbitrary"), jobs=jobs,
    )(*a_list, *b_list, *extra)


def _one(res, jobs):
    outs, jouts = res
    return (outs[0], jouts) if jobs else outs[0]


def _store(dtype):
    def ep(accs, e_refs, o_refs):
        o_refs[0][...] = accs[0].astype(dtype)
    return ep


def _mm_nn(name, a, b, out_dtype, *, b_col0=0, n=None, tm=512, tn=None, tk=None, residual=None, jobs=()):
    M, K = a.shape
    if b.ndim == 3:
        Ns = b.shape[2]
        n = b.shape[0] * Ns
        tn = tn or _pick(Ns, (1408, 1024, 512, 256, 128))
        nps = Ns // tn
        b_spec = pl.BlockSpec((None, tk or _pick(K, (1024, 512, 256, 128)), tn),
                              lambda i, j, k: (j // nps, k, j % nps))
    else:
        n = n or b.shape[1]
        tn = tn or _pick(math_gcd(n, b_col0) if b_col0 else n, (2048, 1024, 512, 256, 128))
        assert b_col0 % tn == 0 and n % tn == 0
        c0 = b_col0 // tn
        b_spec = pl.BlockSpec((tk or _pick(K, (1024, 512, 256, 128)), tn), lambda i, j, k: (k, c0 + j))
    tk = tk or _pick(K, (1024, 512, 256, 128))
    tm = _pick(M, (tm, 256, 128))
    grid = (M // tm, n // tn, K // tk)
    a_spec = pl.BlockSpec((tm, tk), lambda i, j, k: (i, k))
    o_spec = pl.BlockSpec((tm, tn), lambda i, j, k: (i, j))
    if residual is None:
        return _one(_mm(name, "nn", [a], [a_spec], [b], [b_spec], [(0, 0, 0)], 1, grid, tm, tn,
                        [_sds((M, n), out_dtype)], [o_spec], _store(out_dtype), jobs=jobs), jobs)

    def ep(accs, e_refs, o_refs):
        o_refs[0][...] = (e_refs[0][...] + accs[0]).astype(out_dtype)
    return _one(_mm(name, "nn", [a], [a_spec], [b], [b_spec], [(0, 0, 0)], 1, grid, tm, tn,
                    [_sds((M, n), out_dtype)], [o_spec], ep, extra=[residual], extra_specs=[o_spec], jobs=jobs), jobs)


def _mm_nt(name, a_list, b_list, out_dtype, *, k0_list=None, tm=512, tn=None, tk=None, jobs=()):
    M, K = a_list[0].shape
    b0 = b_list[0]
    N = b0.shape[1] if b0.ndim == 3 else b0.shape[0]
    tm = _pick(M, (tm, 256, 128))
    tn = tn or _pick(N, (1024, 512, 256, 128))
    if b0.ndim == 3:
        Ks = b0.shape[2]
        tk = tk or _pick(Ks, (1408, 1024, 512, 256, 128))
        kps = Ks // tk
        b_specs = [pl.BlockSpec((None, tn, tk), lambda i, j, k: (k // kps, j, k % kps)) for _ in b_list]
    else:
        tk = tk or _pick(K, (1024, 896, 512, 256, 128))
        k0_list = k0_list or [0] * len(b_list)
        b_specs = []
        for k0 in k0_list:
            assert k0 % tk == 0
            b_specs.append(pl.BlockSpec((tn, tk), functools.partial(lambda i, j, k, c: (j, c + k), c=k0 // tk)))
    grid = (M // tm, N // tn, K // tk)
    a_specs = [pl.BlockSpec((tm, tk), lambda i, j, k: (i, k)) for _ in a_list]
    o_spec = pl.BlockSpec((tm, tn), lambda i, j, k: (i, j))
    pairs = [(p, p, 0) for p in range(len(a_list))]
    return _one(_mm(name, "nt", a_list, a_specs, b_list, b_specs, pairs, 1, grid, tm, tn,
                    [_sds((M, N), out_dtype)], [o_spec], _store(out_dtype), jobs=jobs), jobs)


def _mm_tn(name, a, b, out_dtype, *, slots=None, tm=None, tn=None, tk=512, jobs=()):
    Kc, Mo = a.shape
    No = b.shape[1]
    tm = tm or _pick(Mo, (1024, 704, 512, 256, 128))
    tk = _pick(Kc, (tk, 256, 128))
    if slots:
        Ns = No // slots
        tn = tn or _pick(Ns, (1408, 1024, 512, 256, 128))
        nps = Ns // tn
        o_spec = pl.BlockSpec((None, tm, tn), lambda i, j, k: (j // nps, i, j % nps))
        o_shape = _sds((slots, Mo, Ns), out_dtype)
    else:
        tn = tn or _pick(No, (1024, 512, 256, 128))
        o_spec = pl.BlockSpec((tm, tn), lambda i, j, k: (i, j))
        o_shape = _sds((Mo, No), out_dtype)
    grid = (Mo // tm, No // tn, Kc // tk)
    a_spec = pl.BlockSpec((tk, tm), lambda i, j, k: (k, i))
    b_spec = pl.BlockSpec((tk, tn), lambda i, j, k: (k, j))
    return _one(_mm(name, "tn", [a], [a_spec], [b], [b_spec], [(0, 0, 0)], 1, grid, tm, tn,
                    [o_shape], [o_spec], _store(out_dtype), jobs=jobs), jobs)


def _cast_bf16(name, w, chip):
    R, C = w.shape
    tr = _pick(R, (256, 128, 64, 32, 16))

    def body(k_ref, w_ref, o_ref):
        o_ref[...] = w_ref[...].astype(BF16)

    gs = pltpu.PrefetchScalarGridSpec(
        num_scalar_prefetch=1, grid=(R // tr,),
        in_specs=[pl.BlockSpec((tr, C), lambda i, k: (i, 0))],
        out_specs=pl.BlockSpec((None, tr, C), lambda i, k: (k[0], i, 0)))
    return pl.pallas_call(body, name=name, grid_spec=gs, out_shape=_sds((N_CHIPS, R, C), BF16),
                          compiler_params=_cp(("parallel",)))(jnp.reshape(chip, (1,)).astype(jnp.int32), w)


def _rms_fwd(name, x, g):
    T, D = x.shape
    tr = _pick(T, (256, 128))

    def body(x_ref, g_ref, h_ref, r_ref):
        xv = x_ref[...]
        r = lax.rsqrt(jnp.mean(xv * xv, axis=1, keepdims=True) + RMS_EPS)
        h_ref[...] = (xv * r * g_ref[...]).astype(BF16)
        r_ref[...] = r

    row = pl.BlockSpec((tr, D), lambda i: (i, 0))
    return pl.pallas_call(
        body, name=name, grid=(T // tr,),
        in_specs=[row, pl.BlockSpec((1, D), lambda i: (0, 0))],
        out_specs=[row, pl.BlockSpec((tr, 1), lambda i: (i, 0))],
        out_shape=[_sds((T, D), BF16), _sds((T, 1), F32)], compiler_params=_cp(("parallel",)),
    )(x, g)


def _final_loss_bwd(name, x2, tgt, g):
    T, D = x2.shape
    tr = _pick(T, (256, 128))

    def body(x_ref, t_ref, g_ref, dx_ref, dxb_ref, loss_ref, gg_ref):
        @pl.when(pl.program_id(0) == 0)
        def _():
            loss_ref[...] = jnp.zeros_like(loss_ref)
            gg_ref[...] = jnp.zeros_like(gg_ref)

        xv = x_ref[...]
        gv = g_ref[...]
        r = lax.rsqrt(jnp.mean(xv * xv, axis=1, keepdims=True) + RMS_EPS)
        n = xv * r
        e = n * gv - t_ref[...]
        loss_ref[...] += 0.5 * jnp.sum(jnp.mean(e * e, axis=1, keepdims=True), axis=0, keepdims=True)
        dy = e * (1.0 / D)
        gg_ref[...] += jnp.sum(dy * n, axis=0, keepdims=True)
        gy = dy * gv
        dx = r * (gy - n * jnp.mean(gy * n, axis=1, keepdims=True))
        dx_ref[...] = dx
        dxb_ref[...] = dx.astype(BF16)

    row = pl.BlockSpec((tr, D), lambda i: (i, 0))
    vec = pl.BlockSpec((1, D), lambda i: (0, 0))
    return pl.pallas_call(
        body, name=name, grid=(T // tr,),
        in_specs=[row, row, vec],
        out_specs=[row, row, pl.BlockSpec((1, LANE), lambda i: (0, 0)), vec],
        out_shape=[_sds((T, D), F32), _sds((T, D), BF16), _sds((1, LANE), F32), _sds((1, D), F32)],
        compiler_params=_cp(("arbitrary",)),
    )(x2, tgt, g)


def _rms_bwd(name, dh_list, x, r, g, dres, want_bf16):
    T, D = x.shape
    tr = _pick(T, (128,))
    n_dh = len(dh_list)

    def body(*refs):
        dh_refs = refs[:n_dh]
        x_ref, r_ref, g_ref, dres_ref = refs[n_dh:n_dh + 4]
        outs = refs[n_dh + 4:]
        gg_ref = outs[-1]

        @pl.when(pl.program_id(0) == 0)
        def _():
            gg_ref[...] = jnp.zeros_like(gg_ref)

        dh = dh_refs[0][...]
        for ref in dh_refs[1:]:
            dh = dh + ref[...]
        rv = r_ref[...]
        n = x_ref[...] * rv
        gg_ref[...] += jnp.sum(dh * n, axis=0, keepdims=True)
        gy = dh * g_ref[...]
        dx = dres_ref[...] + rv * (gy - n * jnp.mean(gy * n, axis=1, keepdims=True))
        outs[0][...] = dx
        if want_bf16:
            outs[1][...] = dx.astype(BF16)

    row = pl.BlockSpec((tr, D), lambda i: (i, 0))
    vec = pl.BlockSpec((1, D), lambda i: (0, 0))
    out_specs = [row] + ([row] if want_bf16 else []) + [vec]
    out_shape = [_sds((T, D), F32)] + ([_sds((T, D), BF16)] if want_bf16 else []) + [_sds((1, D), F32)]
    return pl.pallas_call(
        body, name=name, grid=(T // tr,),
        in_specs=[row] * n_dh + [row, pl.BlockSpec((tr, 1), lambda i: (i, 0)), vec, row],
        out_specs=out_specs, out_shape=out_shape, compiler_params=_cp(("arbitrary",)),
    )(*dh_list, x, r, g, dres)


def _merge_fwd(name, gates, u_a, u_b, b_gate):
    T, D = u_a.shape
    tr = _pick(T, (256, 128))

    def body(ga_ref, gb_ref, ua_ref, ub_ref, ba_ref, bb_ref, o_ref):
        sa = _sigmoid(ga_ref[...] + ba_ref[...])
        sb = _sigmoid(gb_ref[...] + bb_ref[...])
        o_ref[...] = (sa * ua_ref[...] + sb * ub_ref[...]).astype(BF16)

    row = pl.BlockSpec((tr, D), lambda i: (i, 0))
    row1 = pl.BlockSpec((tr, D), lambda i: (i, 1))
    v0 = pl.BlockSpec((1, D), lambda i: (0, 0))
    v1 = pl.BlockSpec((1, D), lambda i: (0, 1))
    return pl.pallas_call(
        body, name=name, grid=(T // tr,),
        in_specs=[row, row1, row, row, v0, v1], out_specs=row,
        out_shape=_sds((T, D), BF16), compiler_params=_cp(("parallel",)),
    )(gates, gates, u_a, u_b, b_gate, b_gate)


def _merge_bwd(name, dm, gates, u_a, u_b, b_gate):
    T, D = u_a.shape
    tr = _pick(T, (128,))

    def body(dm_ref, ga_ref, gb_ref, ua_ref, ub_ref, ba_ref, bb_ref, dua_ref, dub_ref, dga_ref, dgb_ref,
             gba_ref, gbb_ref):
        @pl.when(pl.program_id(0) == 0)
        def _():
            gba_ref[...] = jnp.zeros_like(gba_ref)
            gbb_ref[...] = jnp.zeros_like(gbb_ref)

        d = dm_ref[...]
        sa = _sigmoid(ga_ref[...] + ba_ref[...])
        sb = _sigmoid(gb_ref[...] + bb_ref[...])
        dua_ref[...] = (d * sa).astype(BF16)
        dub_ref[...] = (d * sb).astype(BF16)
        dga = d * ua_ref[...] * sa * (1.0 - sa)
        dgb = d * ub_ref[...] * sb * (1.0 - sb)
        dga_ref[...] = dga.astype(BF16)
        dgb_ref[...] = dgb.astype(BF16)
        gba_ref[...] += jnp.sum(dga, axis=0, keepdims=True)
        gbb_ref[...] += jnp.sum(dgb, axis=0, keepdims=True)

    row = pl.BlockSpec((tr, D), lambda i: (i, 0))
    row1 = pl.BlockSpec((tr, D), lambda i: (i, 1))
    v0 = pl.BlockSpec((1, D), lambda i: (0, 0))
    v1 = pl.BlockSpec((1, D), lambda i: (0, 1))
    outs = pl.pallas_call(
        body, name=name, grid=(T // tr,),
        in_specs=[row, row, row1, row, row, v0, v1],
        out_specs=[row, row, row, row, v0, v0],
        out_shape=[_sds((T, D), BF16), _sds((T, D), BF16), _sds((T, D), BF16), _sds((T, D), BF16),
                   _sds((1, D), F32), _sds((1, D), F32)],
        compiler_params=_cp(("arbitrary",)),
    )(dm, gates, gates, u_a, u_b, b_gate, b_gate)
    return outs


def _adamw(name, w, g, m, v):
    R, C = w.shape
    tr = _pick(R, (64, 32, 16, 8))
    c1 = 1.0 - ADAM_B1 ** ADAM_STEP
    c2 = 1.0 - ADAM_B2 ** ADAM_STEP

    def body(w_ref, g_ref, m_ref, v_ref, d_ref, mo_ref, vo_ref):
        gv = g_ref[...]
        mn = ADAM_B1 * m_ref[...] + (1.0 - ADAM_B1) * gv
        vn = ADAM_B2 * v_ref[...] + (1.0 - ADAM_B2) * (gv * gv)
        d_ref[...] = -ADAM_LR * ((mn / c1) / (jnp.sqrt(vn / c2) + ADAM_EPS) + ADAM_WD * w_ref[...])
        mo_ref[...] = mn
        vo_ref[...] = vn

    blk = pl.BlockSpec((tr, C), lambda i: (i, 0))
    return pl.pallas_call(
        body, name=name, grid=(R // tr,),
        in_specs=[blk] * 4, out_specs=[blk] * 3,
        out_shape=[_sds((R, C), F32)] * 3, compiler_params=_cp(("parallel",)),
    )(w, g, m, v)


def _add_bf16(name, a, a_row0, b):
    S, h, C = b.shape
    tr = _pick(h, (256, 128, 64, 32, 16))
    nb = h // tr

    def body(off_ref, a_ref, b_ref, o_ref):
        o_ref[...] = (a_ref[...].astype(F32) + b_ref[...].astype(F32)).astype(BF16)

    gs = pltpu.PrefetchScalarGridSpec(
        num_scalar_prefetch=1, grid=(S, nb),
        in_specs=[pl.BlockSpec((None, tr, C), lambda s, i, off: (s, off[0] * nb + i, 0)),
                  pl.BlockSpec((None, tr, C), lambda s, i, off: (s, i, 0))],
        out_specs=pl.BlockSpec((None, tr, C), lambda s, i, off: (s, i, 0)))
    return pl.pallas_call(body, name=name, grid_spec=gs, out_shape=_sds((S, h, C), BF16),
                          compiler_params=_cp(("parallel", "parallel")))(
        jnp.reshape(a_row0, (1,)).astype(jnp.int32), a, b)


def _sum4(name, got, mine, chip, core):
    S, h, C = got.shape
    tr = _pick(h, (256, 128, 64, 32, 16))
    nb = h // tr

    def body(chip_ref, core_ref, m_ref, g_ref, o_ref):
        acc = m_ref[...].astype(F32)
        for s in range(S):
            acc = acc + g_ref[s].astype(F32)
        o_ref[...] = acc

    gs = pltpu.PrefetchScalarGridSpec(
        num_scalar_prefetch=2, grid=(nb,),
        in_specs=[pl.BlockSpec((None, tr, C), lambda i, kc, cc: (kc[0], i, 0)),
                  pl.BlockSpec((S, tr, C), lambda i, kc, cc: (0, i, 0))],
        out_specs=pl.BlockSpec((tr, C), lambda i, kc, cc: (cc[0] * nb + i, 0)))
    return pl.pallas_call(body, name=name, grid_spec=gs, out_shape=_sds((2 * h, C), F32),
                          compiler_params=_cp(("parallel",)))(
        jnp.reshape(chip, (1,)).astype(jnp.int32), jnp.reshape(core, (1,)).astype(jnp.int32), mine, got)


def _place():
    x, y, c = lax.axis_index("x"), lax.axis_index("y"), lax.axis_index("c")
    chips = [(1 - x, y), (x, 1 - y), (1 - x, 1 - y)]
    return x, y, c, chips


def _allgather(name, shards):
    n = len(shards)

    def body(*refs):
        out_refs = refs[n:2 * n]
        send_sems, recv_sems = refs[2 * n:]
        x, y, c, chips = _place()
        k = 2 * x + y
        sibling = (x, y, 1 - c)
        firsts, passed = [], []
        for a in range(n):
            out = out_refs[a]
            h = out.shape[1] // 2
            for j, (cx, cy) in enumerate(chips):
                rows = out.at[k, pl.ds(c * h, h), :]
                cp = pltpu.make_async_remote_copy(
                    src_ref=rows, dst_ref=rows,
                    send_sem=send_sems.at[6 * a + j], recv_sem=recv_sems.at[6 * a + j],
                    device_id=(cx, cy, c), device_id_type=MESH)
                cp.start()
                firsts.append(cp)
        for a in range(n):
            out = out_refs[a]
            h = out.shape[1] // 2
            for j, (cx, cy) in enumerate(chips):
                kj = 2 * cx + cy
                rows = out.at[kj, pl.ds(c * h, h), :]
                pltpu.make_async_remote_copy(
                    src_ref=rows, dst_ref=rows, send_sem=send_sems.at[6 * a + j], recv_sem=recv_sems.at[6 * a + j],
                    device_id=(cx, cy, c), device_id_type=MESH).wait_recv()
                fw = pltpu.make_async_remote_copy(
                    src_ref=rows, dst_ref=rows, send_sem=send_sems.at[6 * a + 3 + j],
                    recv_sem=recv_sems.at[6 * a + 3 + j], device_id=sibling, device_id_type=MESH)
                fw.start()
                passed.append(fw)
        for a in range(n):
            out = out_refs[a]
            h = out.shape[1] // 2
            for j, (cx, cy) in enumerate(chips):
                kj = 2 * cx + cy
                rows = out.at[kj, pl.ds((1 - c) * h, h), :]
                pltpu.make_async_remote_copy(
                    src_ref=rows, dst_ref=rows, send_sem=send_sems.at[6 * a + 3 + j],
                    recv_sem=recv_sems.at[6 * a + 3 + j], device_id=sibling, device_id_type=MESH).wait_recv()
        for cp in firsts + passed:
            cp.wait_send()

    return pl.pallas_call(
        body, name=name,
        in_specs=[ANY] * n, out_specs=[ANY] * n,
        out_shape=[_sds(s.shape, s.dtype) for s in shards],
        input_output_aliases={a: a for a in range(n)},
        scratch_shapes=[pltpu.SemaphoreType.DMA((6 * n,)), pltpu.SemaphoreType.DMA((6 * n,))],
    )(*shards)


class _Job:
    def __init__(self, ins, out_shapes, aliases, n_sems, start, finish):
        self.ins, self.out_shapes, self.aliases, self.n_sems = list(ins), list(out_shapes), dict(aliases), n_sems
        self.start, self.finish = start, finish


def _rdma(src, dst, ss, rs, idx, to):
    return pltpu.make_async_remote_copy(src_ref=src, dst_ref=dst, send_sem=ss.at[idx], recv_sem=rs.at[idx],
                                        device_id=to, device_id_type=MESH)


def _job_gather_ici(bufs):
    def descs(outs, ss, rs, incoming):
        x, y, c, chips = _place()
        res = []
        for a, out in enumerate(outs):
            h = out.shape[1] // 2
            for j, (cx, cy) in enumerate(chips):
                rows = out.at[(2 * cx + cy) if incoming else (2 * x + y), pl.ds(c * h, h), :]
                res.append(_rdma(rows, rows, ss, rs, 3 * a + j, (cx, cy, c)))
        return res

    def start(ins, outs, ss, rs):
        for d in descs(outs, ss, rs, False):
            d.start()

    def finish(ins, outs, ss, rs):
        for d in descs(outs, ss, rs, True):
            d.wait_recv()
        for d in descs(outs, ss, rs, False):
            d.wait_send()

    return _Job(bufs, [_sds(b.shape, b.dtype) for b in bufs], {a: a for a in range(len(bufs))}, 3 * len(bufs),
                start, finish)


def _job_gather_d2d(bufs):
    def descs(outs, ss, rs, incoming):
        x, y, c, chips = _place()
        res = []
        for a, out in enumerate(outs):
            h = out.shape[1] // 2
            for j, (cx, cy) in enumerate(chips):
                rows = out.at[2 * cx + cy, pl.ds(((1 - c) if incoming else c) * h, h), :]
                res.append(_rdma(rows, rows, ss, rs, 3 * a + j, (x, y, 1 - c)))
        return res

    def start(ins, outs, ss, rs):
        for d in descs(outs, ss, rs, False):
            d.start()

    def finish(ins, outs, ss, rs):
        for d in descs(outs, ss, rs, True):
            d.wait_recv()
        for d in descs(outs, ss, rs, False):
            d.wait_send()

    return _Job(bufs, [_sds(b.shape, b.dtype) for b in bufs], {a: a for a in range(len(bufs))}, 3 * len(bufs),
                start, finish)


def _job_sibling(grads):
    def descs(ins, outs, ss, rs):
        x, y, c, _ = _place()
        res = []
        for a, (g, out) in enumerate(zip(ins, outs)):
            h = g.shape[1] // 2
            res.append(_rdma(g.at[:, pl.ds((1 - c) * h, h), :], out, ss, rs, a, (x, y, 1 - c)))
        return res

    def start(ins, outs, ss, rs):
        for d in descs(ins, outs, ss, rs):
            d.start()

    def finish(ins, outs, ss, rs):
        for d in descs(ins, outs, ss, rs):
            d.wait()

    return _Job(grads, [_sds((g.shape[0], g.shape[1] // 2, g.shape[2]), g.dtype) for g in grads], {}, len(grads),
                start, finish)


def _job_scatter(parts):
    def descs(ins, outs, ss, rs):
        x, y, c, chips = _place()
        res = []
        for a, (p, out) in enumerate(zip(ins, outs)):
            for j, (cx, cy) in enumerate(chips):
                res.append(_rdma(p.at[2 * cx + cy], out.at[j], ss, rs, 3 * a + j, (cx, cy, c)))
        return res

    def start(ins, outs, ss, rs):
        for d in descs(ins, outs, ss, rs):
            d.start()

    def finish(ins, outs, ss, rs):
        for d in descs(ins, outs, ss, rs):
            d.wait()

    return _Job(parts, [_sds((3,) + p.shape[1:], p.dtype) for p in parts], {}, 3 * len(parts), start, finish)


def _job_swap(fulls):
    def descs(outs, ss, rs, incoming):
        x, y, c, _ = _place()
        res = []
        for a, out in enumerate(outs):
            h = out.shape[0] // 2
            rows = out.at[pl.ds(((1 - c) if incoming else c) * h, h), :]
            res.append(_rdma(rows, rows, ss, rs, a, (x, y, 1 - c)))
        return res

    def start(ins, outs, ss, rs):
        for d in descs(outs, ss, rs, False):
            d.start()

    def finish(ins, outs, ss, rs):
        for d in descs(outs, ss, rs, True):
            d.wait_recv()
        for d in descs(outs, ss, rs, False):
            d.wait_send()

    return _Job(fulls, [_sds(f.shape, f.dtype) for f in fulls], {a: a for a in range(len(fulls))}, len(fulls),
                start, finish)


def _pcall(body, *, name, grid, in_specs, out_specs, out_shape, scratch_shapes=(), sem, jobs=()):
    in_specs, out_specs, out_shape = list(in_specs), list(out_specs), list(out_shape)
    scratch = list(scratch_shapes)
    n_in, n_out, n_scr = len(in_specs), len(out_shape), len(scratch)
    if not jobs:
        call = pl.pallas_call(body, name=name, grid=grid, in_specs=in_specs, out_specs=out_specs, out_shape=out_shape,
                              scratch_shapes=scratch, compiler_params=_cp(sem))
        return lambda *args: (call(*args), [])
    jin = sum(len(j.ins) for j in jobs)
    jout = sum(len(j.out_shapes) for j in jobs)
    aliases, pi, po = {}, n_in, n_out
    for j in jobs:
        for ia, oa in j.aliases.items():
            aliases[pi + ia] = po + oa
        pi, po = pi + len(j.ins), po + len(j.out_shapes)

    def wrapped(*refs):
        ins = refs[:n_in]
        jins = refs[n_in:n_in + jin]
        outs = refs[n_in + jin:n_in + jin + n_out]
        jouts = refs[n_in + jin + n_out:n_in + jin + n_out + jout]
        scr = refs[n_in + jin + n_out + jout:n_in + jin + n_out + jout + n_scr]
        sems = refs[n_in + jin + n_out + jout + n_scr:]
        first, last = None, None
        for d, g in enumerate(grid):
            f, l = pl.program_id(d) == 0, pl.program_id(d) == g - 1
            first = f if first is None else jnp.logical_and(first, f)
            last = l if last is None else jnp.logical_and(last, l)

        def each(what):
            pi, po = 0, 0
            for q, j in enumerate(jobs):
                getattr(j, what)(jins[pi:pi + len(j.ins)], jouts[po:po + len(j.out_shapes)], sems[2 * q], sems[2 * q + 1])
                pi, po = pi + len(j.ins), po + len(j.out_shapes)

        @pl.when(first)
        def _():
            each("start")

        body(*ins, *outs, *scr)

        @pl.when(last)
        def _():
            each("finish")

    call = pl.pallas_call(
        wrapped, name=name, grid=grid,
        in_specs=in_specs + [ANY] * jin, out_specs=out_specs + [ANY] * jout,
        out_shape=out_shape + [s for j in jobs for s in j.out_shapes],
        input_output_aliases=aliases,
        scratch_shapes=scratch + [pltpu.SemaphoreType.DMA((j.n_sems,)) for j in jobs for _ in range(2)],
        compiler_params=_cp(("arbitrary",) * len(grid)))

    def run(*args):
        res = call(*args, *[a for j in jobs for a in j.ins])
        main, rest, per_job = list(res[:n_out]), list(res[n_out:]), []
        for j in jobs:
            per_job.append(rest[:len(j.out_shapes)])
            rest = rest[len(j.out_shapes):]
        return main, per_job
    return run


def _comm_only(name, jobs):
    jin = sum(len(j.ins) for j in jobs)
    jout = sum(len(j.out_shapes) for j in jobs)
    aliases, pi, po = {}, 0, 0
    for j in jobs:
        for ia, oa in j.aliases.items():
            aliases[pi + ia] = po + oa
        pi, po = pi + len(j.ins), po + len(j.out_shapes)

    def body(*refs):
        jins, jouts, sems = refs[:jin], refs[jin:jin + jout], refs[jin + jout:]
        for what in ("start", "finish"):
            pi, po = 0, 0
            for q, j in enumerate(jobs):
                getattr(j, what)(jins[pi:pi + len(j.ins)], jouts[po:po + len(j.out_shapes)], sems[2 * q], sems[2 * q + 1])
                pi, po = pi + len(j.ins), po + len(j.out_shapes)

    res = pl.pallas_call(
        body, name=name, in_specs=[ANY] * jin, out_specs=[ANY] * jout,
        out_shape=[s for j in jobs for s in j.out_shapes], input_output_aliases=aliases,
        scratch_shapes=[pltpu.SemaphoreType.DMA((j.n_sems,)) for j in jobs for _ in range(2)],
    )(*[a for j in jobs for a in j.ins])
    rest, per_job = list(res), []
    for j in jobs:
        per_job.append(rest[:len(j.out_shapes)])
        rest = rest[len(j.out_shapes):]
    return per_job


def _small_allreduce(name, v):
    m_per, n = v.shape

    def body(x_ref, sum_ref, all_ref, send_sems, recv_sems, local_sem):
        x, y, c, chips = _place()
        me, sibling = (x, y, c), (x, y, 1 - c)

        def rows(px, py, pc):
            return all_ref.at[pl.ds((4 * px + 2 * py + pc) * m_per, m_per), :]

        def copy(kk, block, to, src=None):
            return pltpu.make_async_remote_copy(
                src_ref=rows(*block) if src is None else src, dst_ref=rows(*block),
                send_sem=send_sems.at[kk], recv_sem=recv_sems.at[kk], device_id=to, device_id_type=MESH)

        mine = pltpu.make_async_copy(x_ref, rows(*me), local_sem)
        mine.start()
        first = [copy(0, me, sibling, src=x_ref)]
        first += [copy(1 + j, me, (*chip, c), src=x_ref) for j, chip in enumerate(chips)]
        for cp in first:
            cp.start()
        passed = [copy(4 + j, (*chip, c), sibling) for j, chip in enumerate(chips)]
        for j, chip in enumerate(chips):
            copy(1 + j, (*chip, c), me).wait_recv()
            passed[j].start()
        copy(0, sibling, me).wait_recv()
        for j, chip in enumerate(chips):
            copy(4 + j, (*chip, 1 - c), me).wait_recv()
        for cp in first + passed:
            cp.wait_send()
        mine.wait()
        acc = all_ref[pl.ds(0, m_per), :]
        for d in range(1, 8):
            acc = acc + all_ref[pl.ds(d * m_per, m_per), :]
        sum_ref[...] = acc

    vm = pl.BlockSpec(memory_space=pltpu.VMEM)
    return pl.pallas_call(
        body, name=name, in_specs=[vm], out_specs=[vm, vm],
        out_shape=[_sds((m_per, n), F32), _sds((8 * m_per, n), F32)],
        scratch_shapes=[pltpu.SemaphoreType.DMA((7,)), pltpu.SemaphoreType.DMA((7,)), pltpu.SemaphoreType.DMA],
    )(v)[0]


def _in_layout(D, W6, nhb, Ls):
    nmain = W6 + 2 * D
    lay = []
    for k in range(N_CHIPS):
        g0, g1 = k * Ls, (k + 1) * Ls
        pieces = []
        a, b = max(g0, 0), min(g1, W6)
        if a < b:
            pieces.append((a - g0, b - g0, a))
        a, b = max(g0, W6 + nhb), min(g1, W6 + nhb + 2 * D)
        if a < b:
            pieces.append((a - g0, b - g0, a - nhb))
        a, b = max(g0, W6), min(g1, W6 + nhb)
        fpiece = (a - g0, b - g0, a - W6) if a < b else None
        assert fpiece is None or (b - a) == nhb
        main0 = min(p[2] for p in pieces)
        main1 = max(p[2] + p[1] - p[0] for p in pieces)
        lay.append(dict(pieces=pieces, f=fpiece, s=main0 // LANE, e=-(-main1 // LANE), main1=main1))
    assert sum(1 for l in lay if l["f"] is not None) == 1
    nbw = max(l["e"] - l["s"] + (1 if l["f"] else 0) for l in lay)
    for k in range(1, N_CHIPS):
        assert lay[k]["s"] >= lay[k - 1]["e"] - 1 and lay[k]["s"] > lay[k - 1]["s"]
    return lay, nbw, nmain


def _to_window(w, lay_k, nbw):
    D = w.shape[0]
    items = [(c0 - lay_k["s"] * LANE, l0, l1) for (l0, l1, c0) in lay_k["pieces"]]
    if lay_k["f"]:
        l0, l1, off = lay_k["f"]
        items.append(((lay_k["e"] - lay_k["s"]) * LANE + off, l0, l1))
    items.sort()
    cols, pos = [], 0
    for w0, l0, l1 in items:
        if w0 > pos:
            cols.append(jnp.zeros((D, w0 - pos), w.dtype))
        cols.append(w[:, l0:l1])
        pos = w0 + (l1 - l0)
    if pos < nbw * LANE:
        cols.append(jnp.zeros((D, nbw * LANE - pos), w.dtype))
    return jnp.concatenate(cols, axis=1)


def _from_window(win, lay_k):
    items = [(l0, c0 - lay_k["s"] * LANE, l1 - l0) for (l0, l1, c0) in lay_k["pieces"]]
    if lay_k["f"]:
        l0, l1, off = lay_k["f"]
        items.append((l0, (lay_k["e"] - lay_k["s"]) * LANE + off, l1 - l0))
    items.sort()
    return jnp.concatenate([win[:, w0:w0 + n] for (_, w0, n) in items], axis=1)


def _assemble_in(name, wins, lay, nbw, nmain):
    _, D, _ = wins.shape
    ncb = nmain // LANE + 1
    k1 = np.zeros(ncb, np.int32)
    i1 = np.zeros(ncb, np.int32)
    k2 = np.zeros(ncb, np.int32)
    i2 = np.zeros(ncb, np.int32)
    fl = np.zeros(ncb, np.int32)
    for b in range(ncb - 1):
        k = max(kk for kk in range(N_CHIPS) if lay[kk]["s"] <= b)
        k1[b], i1[b] = k, b - lay[k]["s"]
        if k >= 1 and b == lay[k]["s"] and lay[k - 1]["main1"] > b * LANE:
            k2[b], i2[b], fl[b] = k - 1, b - lay[k - 1]["s"], 1
    kf = [kk for kk in range(N_CHIPS) if lay[kk]["f"]][0]
    k1[ncb - 1], i1[ncb - 1] = kf, lay[kf]["e"] - lay[kf]["s"]

    def body(k1_ref, i1_ref, k2_ref, i2_ref, fl_ref, a_ref, b_ref, o_ref):
        b = pl.program_id(0)
        add = jnp.where(fl_ref[b] == 1, b_ref[...], jnp.zeros_like(b_ref))
        o_ref[...] = a_ref[...] + add

    gs = pltpu.PrefetchScalarGridSpec(
        num_scalar_prefetch=5, grid=(ncb,),
        in_specs=[pl.BlockSpec((None, D, LANE), lambda b, k1r, i1r, k2r, i2r, flr: (k1r[b], 0, i1r[b])),
                  pl.BlockSpec((None, D, LANE), lambda b, k1r, i1r, k2r, i2r, flr: (k2r[b], 0, i2r[b]))],
        out_specs=pl.BlockSpec((D, LANE), lambda b, k1r, i1r, k2r, i2r, flr: (0, b)))
    return pl.pallas_call(body, name=name, grid_spec=gs, out_shape=_sds((D, ncb * LANE), BF16),
                          compiler_params=_cp(("parallel",)))(
        jnp.asarray(k1), jnp.asarray(i1), jnp.asarray(k2), jnp.asarray(i2), jnp.asarray(fl), wins, wins)


def _a_specs_q(nh):
    q = pl.BlockSpec((GROUP, HEAD_DIM), lambda i, h: (i, h))
    ks = [pl.BlockSpec((GROUP, HEAD_DIM), functools.partial(
        lambda i, h, j: (jnp.maximum(i - (WIN_BLOCKS - 1) + j, 0), nh + h), j=j)) for j in range(WIN_BLOCKS)]
    vs = [pl.BlockSpec((GROUP, HEAD_DIM), functools.partial(
        lambda i, h, j: (jnp.maximum(i - (WIN_BLOCKS - 1) + j, 0), 2 * nh + h), j=j)) for j in range(WIN_BLOCKS)]
    return q, ks, vs


def _a_logits(q, k_refs, bias, i, scale):
    parts = [lax.dot_general(q, kr[...], NT, preferred_element_type=F32) for kr in k_refs]
    s = jnp.concatenate(parts, axis=1) * scale + bias
    col = lax.broadcasted_iota(jnp.int32, s.shape, 1)
    return jnp.where(col >= (WIN_BLOCKS - 1 - i) * GROUP, s, NEG_INF)


def _attn_a_fwd(name, qkv, bias2, nh, jobs=()):
    T = qkv.shape[0]
    ng = T // GROUP
    scale = HEAD_DIM ** -0.5

    def body(q_ref, *refs):
        k_refs = refs[:WIN_BLOCKS]
        v_refs = refs[WIN_BLOCKS:2 * WIN_BLOCKS]
        bias_ref, o_ref, lse_ref = refs[2 * WIN_BLOCKS:]
        i, h = pl.program_id(0), pl.program_id(1)

        @pl.when(h == 0)
        def _():
            lse_ref[...] = jnp.zeros_like(lse_ref)

        s = _a_logits(q_ref[...], k_refs, bias_ref[h], i, scale)
        m = jnp.max(s, axis=1, keepdims=True)
        p = jnp.exp(s - m)
        l = jnp.sum(p, axis=1, keepdims=True)
        pb = (p / l).astype(BF16)
        o = jnp.zeros((GROUP, HEAD_DIM), F32)
        for j in range(WIN_BLOCKS):
            o = o + jnp.dot(pb[:, j * GROUP:(j + 1) * GROUP], v_refs[j][...], preferred_element_type=F32)
        o_ref[...] = o.astype(BF16)
        _put_col(lse_ref, h, m + jnp.log(l))

    q_spec, k_specs, v_specs = _a_specs_q(nh)
    stat = pl.BlockSpec((GROUP, LANE), lambda i, h: (i, 0))
    return _pcall(
        body, name=name, grid=(ng, nh),
        in_specs=[q_spec] + k_specs + v_specs + [pl.BlockSpec((nh, GROUP, WIN), lambda i, h: (0, 0, 0))],
        out_specs=[pl.BlockSpec((GROUP, HEAD_DIM), lambda i, h: (i, h)), stat],
        out_shape=[_sds((T, nh * HEAD_DIM), BF16), _sds((T, LANE), F32)],
        sem=("parallel", "arbitrary"), jobs=jobs,
    )(qkv, *([qkv] * (2 * WIN_BLOCKS)), bias2)


def _attn_a_dq(name, qkv, do, lse, bias2, nh, jobs=()):
    T = qkv.shape[0]
    ng = T // GROUP
    scale = HEAD_DIM ** -0.5

    def body(q_ref, *refs):
        k_refs = refs[:WIN_BLOCKS]
        v_refs = refs[WIN_BLOCKS:2 * WIN_BLOCKS]
        do_ref, lse_ref, bias_ref, dq_ref, delta_ref, db_ref = refs[2 * WIN_BLOCKS:]
        i, h = pl.program_id(0), pl.program_id(1)

        @pl.when(h == 0)
        def _():
            delta_ref[...] = jnp.zeros_like(delta_ref)

        @pl.when(i == 0)
        def _():
            db_ref[h] = jnp.zeros((GROUP, WIN), F32)

        s = _a_logits(q_ref[...], k_refs, bias_ref[h], i, scale)
        p = jnp.exp(s - _col_of(lse_ref[...], h))
        dov = do_ref[...]
        dp = jnp.concatenate([lax.dot_general(dov, vr[...], NT, preferred_element_type=F32) for vr in v_refs], axis=1)
        delta = jnp.sum(p * dp, axis=1, keepdims=True)
        ds = p * (dp - delta)
        db_ref[h] += ds
        dsb = ds.astype(BF16)
        dq = jnp.zeros((GROUP, HEAD_DIM), F32)
        for j in range(WIN_BLOCKS):
            dq = dq + jnp.dot(dsb[:, j * GROUP:(j + 1) * GROUP], k_refs[j][...], preferred_element_type=F32)
        dq_ref[...] = (dq * scale).astype(BF16)
        _put_col(delta_ref, h, delta)

    q_spec, k_specs, v_specs = _a_specs_q(nh)
    stat = pl.BlockSpec((GROUP, LANE), lambda i, h: (i, 0))
    full_b = pl.BlockSpec((nh, GROUP, WIN), lambda i, h: (0, 0, 0))
    return _pcall(
        body, name=name, grid=(ng, nh),
        in_specs=[q_spec] + k_specs + v_specs + [pl.BlockSpec((GROUP, HEAD_DIM), lambda i, h: (i, h)), stat, full_b],
        out_specs=[pl.BlockSpec((GROUP, HEAD_DIM), lambda i, h: (i, h)), stat, full_b],
        out_shape=[_sds((T, nh * HEAD_DIM), BF16), _sds((T, LANE), F32), _sds((nh, GROUP, WIN), F32)],
        sem=("arbitrary", "arbitrary"), jobs=jobs,
    )(qkv, *([qkv] * (2 * WIN_BLOCKS)), do, lse, bias2)


def _attn_a_dkv(name, qkv, do, lse, delta, bias2, nh, jobs=()):
    T = qkv.shape[0]
    ng = T // GROUP
    scale = HEAD_DIM ** -0.5
    nj = WIN_BLOCKS

    def body(k_ref, v_ref, *refs):
        q_refs = refs[:nj]
        do_refs = refs[nj:2 * nj]
        lse_refs = refs[2 * nj:3 * nj]
        dl_refs = refs[3 * nj:4 * nj]
        bias_ref, dk_ref, dv_ref = refs[4 * nj:]
        r, h = pl.program_id(0), pl.program_id(1)
        kv, vv = k_ref[...], v_ref[...]
        bias = bias_ref[h]
        dk = jnp.zeros((GROUP, HEAD_DIM), F32)
        dv = jnp.zeros((GROUP, HEAD_DIM), F32)
        for j in range(nj):
            qv, dov = q_refs[j][...], do_refs[j][...]
            c0 = (nj - 1 - j) * GROUP
            s = lax.dot_general(qv, kv, NT, preferred_element_type=F32) * scale + bias[:, c0:c0 + GROUP]
            p = jnp.exp(s - _col_of(lse_refs[j][...], h))
            p = jnp.where(r + j <= ng - 1, p, 0.0)
            dp = lax.dot_general(dov, vv, NT, preferred_element_type=F32)
            ds = p * (dp - _col_of(dl_refs[j][...], h))
            dv = dv + lax.dot_general(p.astype(BF16), dov, TN, preferred_element_type=F32)
            dk = dk + lax.dot_general(ds.astype(BF16), qv, TN, preferred_element_type=F32)
        dk_ref[...] = (dk * scale).astype(BF16)
        dv_ref[...] = dv.astype(BF16)

    def qmap(j):
        return functools.partial(lambda r, h, j: (jnp.minimum(r + j, ng - 1), h), j=j)

    def smap(j):
        return functools.partial(lambda r, h, j: (jnp.minimum(r + j, ng - 1), 0), j=j)

    blk = (GROUP, HEAD_DIM)
    in_specs = ([pl.BlockSpec(blk, lambda r, h: (r, nh + h)), pl.BlockSpec(blk, lambda r, h: (r, 2 * nh + h))]
                + [pl.BlockSpec(blk, qmap(j)) for j in range(nj)]
                + [pl.BlockSpec(blk, qmap(j)) for j in range(nj)]
                + [pl.BlockSpec((GROUP, LANE), smap(j)) for j in range(nj)]
                + [pl.BlockSpec((GROUP, LANE), smap(j)) for j in range(nj)]
                + [pl.BlockSpec((nh, GROUP, WIN), lambda r, h: (0, 0, 0))])
    out = pl.BlockSpec(blk, lambda r, h: (r, h))
    return _pcall(
        body, name=name, grid=(ng, nh), in_specs=in_specs, out_specs=[out, out],
        out_shape=[_sds((T, nh * HEAD_DIM), BF16)] * 2,
        sem=("parallel", "parallel"), jobs=jobs,
    )(qkv, qkv, *([qkv] * nj), *([do] * nj), *([lse] * nj), *([delta] * nj), bias2)


def _fox_prep(name, f, b_f):
    T = f.shape[0]
    tb = _pick(T, (256, 128))

    def body(f_ref, b_ref, cum_ref, cumt_ref, carry_ref):
        @pl.when(pl.program_id(0) == 0)
        def _():
            carry_ref[...] = jnp.zeros_like(carry_ref)

        z = f_ref[...] + b_ref[...]
        logf = jnp.minimum(z, 0.0) - jnp.log(1.0 + jnp.exp(-jnp.abs(z)))
        row = lax.broadcasted_iota(jnp.int32, (tb, tb), 0)
        col = lax.broadcasted_iota(jnp.int32, (tb, tb), 1)
        tri = (row >= col).astype(BF16)
        acc = jnp.zeros((tb, LANE), F32)
        for piece in _split3(logf):
            acc = acc + jnp.dot(tri, piece, preferred_element_type=F32)
        cum = acc + carry_ref[...]
        cum_ref[...] = cum
        cumt_ref[...] = cum.T
        carry_ref[...] = cum_ref[pl.ds(tb - 1, 1), :]

    return pl.pallas_call(
        body, name=name, grid=(T // tb,),
        in_specs=[pl.BlockSpec((tb, LANE), lambda i: (i, 0)), pl.BlockSpec((1, LANE), lambda i: (0, 0))],
        out_specs=[pl.BlockSpec((tb, LANE), lambda i: (i, 0)), pl.BlockSpec((LANE, tb), lambda i: (0, i))],
        out_shape=[_sds((T, LANE), F32), _sds((LANE, T), F32)],
        scratch_shapes=[pltpu.VMEM((1, LANE), F32)],
        compiler_params=_cp(("arbitrary",)),
    )(f, b_f)


def _fox_blk(T):
    return _pick(T, (256, 128))


def _fox_mask(s, i, j, tq, tk):
    qpos = i * tq + lax.broadcasted_iota(jnp.int32, s.shape, 0)
    kpos = j * tk + lax.broadcasted_iota(jnp.int32, s.shape, 1)
    return jnp.where(kpos <= qpos, s, NEG_INF)


def _fox_fwd(name, qkv, cum, cumt, nh, jobs=()):
    T = qkv.shape[0]
    tq = tk = _fox_blk(T)
    scale = HEAD_DIM ** -0.5

    def body(q_ref, k_ref, v_ref, cum_ref, cumt_ref, o_ref, lse_ref):
        i, h = pl.program_id(0), pl.program_id(1)

        @pl.when(h == 0)
        def _():
            lse_ref[...] = jnp.zeros_like(lse_ref)

        q = q_ref[...]
        cq = _col_of(cum_ref[...], h)

        def step(j, carry):
            m, l, acc = carry
            k0 = pl.multiple_of(j * tk, tk)
            kj = k_ref[pl.ds(k0, tk), :]
            vj = v_ref[pl.ds(k0, tk), :]
            ck = cumt_ref[pl.ds(h, 1), pl.ds(k0, tk)]
            s = lax.dot_general(q, kj, NT, preferred_element_type=F32) * scale + cq - ck
            s = _fox_mask(s, i, j, tq, tk)
            m_new = jnp.maximum(m, jnp.max(s, axis=1, keepdims=True))
            alpha = jnp.exp(m - m_new)
            p = jnp.exp(s - m_new)
            l = alpha * l + jnp.sum(p, axis=1, keepdims=True)
            acc = alpha * acc + jnp.dot(p.astype(BF16), vj, preferred_element_type=F32)
            return m_new, l, acc

        init = (jnp.full((tq, 1), NEG_INF, F32), jnp.zeros((tq, 1), F32), jnp.zeros((tq, HEAD_DIM), F32))
        m, l, acc = lax.fori_loop(0, i + 1, step, init)
        o_ref[...] = (acc / l).astype(BF16)
        _put_col(lse_ref, h, m + jnp.log(l))

    return _pcall(
        body, name=name, grid=(T // tq, nh),
        in_specs=[pl.BlockSpec((tq, HEAD_DIM), lambda i, h: (i, 3 * nh + h)),
                  pl.BlockSpec((T, HEAD_DIM), lambda i, h: (0, 4 * nh + h)),
                  pl.BlockSpec((T, HEAD_DIM), lambda i, h: (0, 5 * nh + h)),
                  pl.BlockSpec((tq, LANE), lambda i, h: (i, 0)),
                  pl.BlockSpec((LANE, T), lambda i, h: (0, 0))],
        out_specs=[pl.BlockSpec((tq, HEAD_DIM), lambda i, h: (i, h)), pl.BlockSpec((tq, LANE), lambda i, h: (i, 0))],
        out_shape=[_sds((T, nh * HEAD_DIM), BF16), _sds((T, LANE), F32)],
        sem=("parallel", "arbitrary"), jobs=jobs,
    )(qkv, qkv, qkv, cum, cumt)


def _fox_dq(name, qkv, do, lse, cum, cumt, nh, jobs=()):
    T = qkv.shape[0]
    tq = tk = _fox_blk(T)
    scale = HEAD_DIM ** -0.5

    def body(q_ref, k_ref, v_ref, do_ref, lse_ref, cum_ref, cumt_ref, dq_ref, delta_ref):
        i, h = pl.program_id(0), pl.program_id(1)

        @pl.when(h == 0)
        def _():
            delta_ref[...] = jnp.zeros_like(delta_ref)

        q = q_ref[...]
        dov = do_ref[...]
        cq = _col_of(cum_ref[...], h)
        lse = _col_of(lse_ref[...], h)

        def p_dp(j):
            k0 = pl.multiple_of(j * tk, tk)
            kj = k_ref[pl.ds(k0, tk), :]
            vj = v_ref[pl.ds(k0, tk), :]
            ck = cumt_ref[pl.ds(h, 1), pl.ds(k0, tk)]
            s = lax.dot_general(q, kj, NT, preferred_element_type=F32) * scale + cq - ck
            p = jnp.exp(_fox_mask(s, i, j, tq, tk) - lse)
            return p, lax.dot_general(dov, vj, NT, preferred_element_type=F32), kj

        def sweep_delta(j, delta):
            p, dp, _ = p_dp(j)
            return delta + jnp.sum(p * dp, axis=1, keepdims=True)

        delta = lax.fori_loop(0, i + 1, sweep_delta, jnp.zeros((tq, 1), F32))

        def sweep_dq(j, dq):
            p, dp, kj = p_dp(j)
            ds = p * (dp - delta)
            return dq + jnp.dot(ds.astype(BF16), kj, preferred_element_type=F32)

        dq = lax.fori_loop(0, i + 1, sweep_dq, jnp.zeros((tq, HEAD_DIM), F32))
        dq_ref[...] = (dq * scale).astype(BF16)
        _put_col(delta_ref, h, delta)

    blk = pl.BlockSpec((tq, HEAD_DIM), lambda i, h: (i, h))
    stat = pl.BlockSpec((tq, LANE), lambda i, h: (i, 0))
    return _pcall(
        body, name=name, grid=(T // tq, nh),
        in_specs=[pl.BlockSpec((tq, HEAD_DIM), lambda i, h: (i, 3 * nh + h)),
                  pl.BlockSpec((T, HEAD_DIM), lambda i, h: (0, 4 * nh + h)),
                  pl.BlockSpec((T, HEAD_DIM), lambda i, h: (0, 5 * nh + h)),
                  blk, stat, stat, pl.BlockSpec((LANE, T), lambda i, h: (0, 0))],
        out_specs=[blk, stat],
        out_shape=[_sds((T, nh * HEAD_DIM), BF16), _sds((T, LANE), F32)],
        sem=("parallel", "arbitrary"), jobs=jobs,
    )(qkv, qkv, qkv, do, lse, cum, cumt)


def _fox_dkv(name, qkv, do, lse, delta, cum, cumt, nh, jobs=()):
    T = qkv.shape[0]
    tq = tk = _fox_blk(T)
    nq = T // tq
    scale = HEAD_DIM ** -0.5

    def body(k_ref, v_ref, q_ref, do_ref, lse_ref, dl_ref, cum_ref, cumt_ref, dk_ref, dv_ref, dc_ref):
        j, h = pl.program_id(0), pl.program_id(1)

        @pl.when(h == 0)
        def _():
            dc_ref[...] = jnp.zeros_like(dc_ref)

        kj, vj = k_ref[...], v_ref[...]
        k0 = pl.multiple_of(j * tk, tk)
        ck = cumt_ref[pl.ds(h, 1), pl.ds(k0, tk)]

        def step(i, carry):
            dk, dv, dc = carry
            q0 = pl.multiple_of(i * tq, tq)
            qi = q_ref[pl.ds(q0, tq), :]
            doi = do_ref[pl.ds(q0, tq), :]
            cq = _col_of(cum_ref[pl.ds(q0, tq), :], h)
            lse = _col_of(lse_ref[pl.ds(q0, tq), :], h)
            dl = _col_of(dl_ref[pl.ds(q0, tq), :], h)
            s = lax.dot_general(qi, kj, NT, preferred_element_type=F32) * scale + cq - ck
            s = _fox_mask(s, i, j, tq, tk)
            p = jnp.exp(s - lse)
            dp = lax.dot_general(doi, vj, NT, preferred_element_type=F32)
            ds = p * (dp - dl)
            dv = dv + lax.dot_general(p.astype(BF16), doi, TN, preferred_element_type=F32)
            dk = dk + lax.dot_general(ds.astype(BF16), qi, TN, preferred_element_type=F32)
            dc = dc - jnp.sum(ds, axis=0, keepdims=True)
            return dk, dv, dc

        init = (jnp.zeros((tk, HEAD_DIM), F32), jnp.zeros((tk, HEAD_DIM), F32), jnp.zeros((1, tk), F32))
        dk, dv, dc = lax.fori_loop(j, nq, step, init)
        dk_ref[...] = (dk * scale).astype(BF16)
        dv_ref[...] = dv.astype(BF16)
        sub = lax.broadcasted_iota(jnp.int32, (LANE, tk), 0)
        dc_ref[...] = jnp.where(sub == h, dc, dc_ref[...])

    whole = lambda c: pl.BlockSpec((T, HEAD_DIM), c)
    stat = pl.BlockSpec((T, LANE), lambda j, h: (0, 0))
    out = pl.BlockSpec((tk, HEAD_DIM), lambda j, h: (j, h))
    return _pcall(
        body, name=name, grid=(T // tk, nh),
        in_specs=[pl.BlockSpec((tk, HEAD_DIM), lambda j, h: (j, 4 * nh + h)),
                  pl.BlockSpec((tk, HEAD_DIM), lambda j, h: (j, 5 * nh + h)),
                  whole(lambda j, h: (0, 3 * nh + h)), whole(lambda j, h: (0, h)),
                  stat, stat, stat, pl.BlockSpec((LANE, T), lambda j, h: (0, 0))],
        out_specs=[out, out, pl.BlockSpec((LANE, tk), lambda j, h: (0, j))],
        out_shape=[_sds((T, nh * HEAD_DIM), BF16)] * 2 + [_sds((LANE, T), F32)],
        sem=("parallel", "arbitrary"), jobs=jobs,
    )(qkv, qkv, qkv, do, lse, delta, cum, cumt)


def _fox_post(name, dcumt, f, b_f):
    T = f.shape[0]
    tb = _pick(T, (256, 128))
    nb = T // tb

    def body(dc_ref, f_ref, b_ref, df_ref, gb_ref, carry_ref):
        @pl.when(pl.program_id(0) == 0)
        def _():
            carry_ref[...] = jnp.zeros_like(carry_ref)
            gb_ref[...] = jnp.zeros_like(gb_ref)

        dc = dc_ref[...]
        row = lax.broadcasted_iota(jnp.int32, (tb, tb), 0)
        col = lax.broadcasted_iota(jnp.int32, (tb, tb), 1)
        tri = (row >= col).astype(BF16)
        acc = jnp.zeros((LANE, tb), F32)
        for piece in _split3(dc):
            acc = acc + jnp.dot(piece, tri, preferred_element_type=F32)
        dlogf = (acc + carry_ref[...]).T
        carry_ref[...] += jnp.sum(dc, axis=1, keepdims=True)
        z = f_ref[...] + b_ref[...]
        df = dlogf * _sigmoid(-z)
        df_ref[...] = df.astype(BF16)
        gb_ref[...] += jnp.sum(df, axis=0, keepdims=True)

    return pl.pallas_call(
        body, name=name, grid=(nb,),
        in_specs=[pl.BlockSpec((LANE, tb), lambda g: (0, nb - 1 - g)),
                  pl.BlockSpec((tb, LANE), lambda g: (nb - 1 - g, 0)),
                  pl.BlockSpec((1, LANE), lambda g: (0, 0))],
        out_specs=[pl.BlockSpec((tb, LANE), lambda g: (nb - 1 - g, 0)), pl.BlockSpec((1, LANE), lambda g: (0, 0))],
        out_shape=[_sds((T, LANE), BF16), _sds((1, LANE), F32)],
        scratch_shapes=[pltpu.VMEM((LANE, 1), F32)],
        compiler_params=_cp(("arbitrary",)),
    )(dcumt, f, b_f)


def _rel_tables(n_rel):
    max_rel = (n_rel - 1) // 2
    nj = GROUP + WIN - 1
    onehot = np.zeros((n_rel, nj), np.float32)
    for j in range(nj):
        dist = (WIN - 1) - j
        onehot[int(np.clip(dist, -max_rel, max_rel)) + max_rel, j] = 1.0
    a = np.arange(GROUP)[:, None]
    kb = np.arange(WIN)[None, :]
    lo = CHUNK * (a // CHUNK)
    inband = (kb >= lo) & (kb < lo + BAND)
    return onehot, inband


def _bias2_of(rel_bias, onehot, inband):
    bv = jnp.dot(rel_bias, jnp.asarray(onehot), precision=lax.Precision.HIGHEST)
    rows = [bv[:, GROUP - 1 - a:GROUP - 1 - a + WIN] for a in range(GROUP)]
    toe = jnp.stack(rows, axis=1)
    return jnp.where(jnp.asarray(inband)[None], toe, NEG_INF)


def _rel_grad_of(dbias2, onehot):
    nj = GROUP + WIN - 1
    dbv = sum(jnp.pad(dbias2[:, a, :], ((0, 0), (GROUP - 1 - a, nj - WIN - (GROUP - 1 - a)))) for a in range(GROUP))
    return jnp.dot(dbv, jnp.asarray(onehot).T, precision=lax.Precision.HIGHEST)


def kernel(x, g_mix, w_in, b_f, b_gate, rel_bias, w_branch_a, w_branch_b, w_out, g_ffn, w_gate_ffn, w_up_ffn, w_down_ffn, g_final, loss_target, m_g_mix, m_w_in, m_b_f, m_b_gate, m_rel_bias, m_w_branch_a, m_w_branch_b, m_w_out, m_g_ffn, m_w_gate_ffn, m_w_up_ffn, m_w_down_ffn, m_g_final, v_g_mix, v_w_in, v_b_f, v_b_gate, v_rel_bias, v_w_branch_a, v_w_branch_b, v_w_out, v_g_ffn, v_w_gate_ffn, v_w_up_ffn, v_w_down_ffn, v_g_final):
    T, D = x.shape[1], x.shape[2]
    Ls = w_in.shape[2]
    W = w_branch_a.shape[1]
    nh = W // HEAD_DIM
    nhb = b_f.shape[1]
    assert w_branch_b.shape[1] == W and nhb == nh and rel_bias.shape[1] == nh
    W6 = 6 * W
    Fl = w_gate_ffn.shape[2]
    Fp = -(-Fl // LANE) * LANE
    n_rel = rel_bias.shape[2]
    chip = 2 * lax.axis_index("x") + lax.axis_index("y")
    lay, nbw, nmain = _in_layout(D, W6, nhb, Ls)
    onehot, inband = _rel_tables(n_rel)

    xs, tgt = x[0], loss_target[0]

    win_f32 = lax.switch(chip, [functools.partial(_to_window, lay_k=lay[k], nbw=nbw) for k in range(N_CHIPS)], w_in[0])
    pad_c = lambda w: jnp.pad(w, ((0, 0), (0, Fp - Fl)))
    pad_r = lambda w: jnp.pad(w, ((0, Fp - Fl), (0, 0)))
    sh_in = _cast_bf16("cast_w_in", win_f32, chip)
    sh_a = _cast_bf16("cast_w_a", w_branch_a[0], chip)
    sh_b = _cast_bf16("cast_w_b", w_branch_b[0], chip)
    sh_o = _cast_bf16("cast_w_out", w_out[0], chip)
    sh_g = _cast_bf16("cast_w_gate", pad_c(w_gate_ffn[0]), chip)
    sh_u = _cast_bf16("cast_w_up", pad_c(w_up_ffn[0]), chip)
    sh_d = _cast_bf16("cast_w_down", pad_r(w_down_ffn[0]), chip)
    (wins,) = _allgather("ag_w_in", [sh_in])
    wc = _assemble_in("assemble_w_in", wins, lay, nbw, nmain)

    h1, r1 = _rms_fwd("rms1", xs, g_mix)
    qkv, ((wa_g, wb_g, wo_g),) = _mm_nn("proj_qkv", h1, wc, BF16, b_col0=0, n=W6,
                                        jobs=[_job_gather_ici([sh_a, sh_b, sh_o])])
    gates, ((wa_g, wb_g, wo_g), (wg_g,)) = _mm_nn("proj_gates", h1, wc, F32, b_col0=W6, n=2 * D,
                                                  jobs=[_job_gather_d2d([wa_g, wb_g, wo_g]), _job_gather_ici([sh_g])])
    fl = _mm_nn("proj_f", h1, wc, F32, b_col0=nmain, n=LANE, tn=LANE)
    bias2 = _bias2_of(rel_bias[0], onehot, inband)
    bf_pad = jnp.pad(b_f, ((0, 0), (0, LANE - nhb)))
    (o_a, lse_a), ((wu_g,),) = _attn_a_fwd("attn_a_fwd", qkv, bias2, nh, jobs=[_job_gather_ici([sh_u])])
    cum, cumt = _fox_prep("fox_prep", fl, bf_pad)
    (o_b, lse_b), ((wd_g,), (wg_g, wu_g)) = _fox_fwd("fox_fwd", qkv, cum, cumt, nh,
                                                     jobs=[_job_gather_ici([sh_d]), _job_gather_d2d([wg_g, wu_g])])
    u_a = _mm_nn("branch_a", o_a, wa_g, F32)
    u_b = _mm_nn("branch_b", o_b, wb_g, F32)
    merged = _merge_fwd("merge", gates, u_a, u_b, b_gate)
    wo_full = wo_g.reshape(D, D)
    x1, ((wd_g,),) = _mm_nn("out_proj", merged, wo_full, F32, residual=xs, jobs=[_job_gather_d2d([wd_g])])
    wd_full = wd_g.reshape(N_CHIPS * Fp, D)
    h2, r2 = _rms_fwd("rms2", x1, g_ffn)

    tm_f = _pick(T, (512, 256, 128))
    tn_f = _pick(Fp, (1408, 1024, 512, 256, 128))
    tk_f = _pick(D, (1024, 512, 256, 128))
    nps_f = Fp // tn_f

    def swiglu_ep(accs, e_refs, o_refs):
        g, u = accs
        o_refs[0][...] = g.astype(BF16)
        o_refs[1][...] = u.astype(BF16)
        o_refs[2][...] = (g * _sigmoid(g) * u).astype(BF16)

    hid_spec = pl.BlockSpec((tm_f, tn_f), lambda i, j, k: (i, j))
    wcol_spec = pl.BlockSpec((None, tk_f, tn_f), lambda i, j, k: (j // nps_f, k, j % nps_f))
    (gate, up, hidden), _ = _mm(
        "ffn_up", "nn", [h2], [pl.BlockSpec((tm_f, tk_f), lambda i, j, k: (i, k))], [wg_g, wu_g], [wcol_spec, wcol_spec],
        [(0, 0, 0), (0, 1, 1)], 2, (T // tm_f, N_CHIPS * Fp // tn_f, D // tk_f), tm_f, tn_f,
        [_sds((T, N_CHIPS * Fp), BF16)] * 3, [hid_spec] * 3, swiglu_ep)
    x2 = _mm_nn("ffn_down", hidden, wd_full, F32, residual=x1, tk=_pick(N_CHIPS * Fp, (1408, 1024, 512, 256, 128)))

    dx2, dx2b, loss_part, gg_final = _final_loss_bwd("final_loss", x2, tgt, g_final.reshape(1, D))

    def swiglu_bwd_ep(accs, e_refs, o_refs):
        dh = accs[0]
        g = e_refs[0][...].astype(F32)
        u = e_refs[1][...].astype(F32)
        sg = _sigmoid(g)
        o_refs[0][...] = (dh * u * (sg * (1.0 + g * (1.0 - sg)))).astype(BF16)
        o_refs[1][...] = (dh * (g * sg)).astype(BF16)

    tk_b = _pick(D, (1024, 512, 256, 128))
    core = lax.axis_index("c")
    (dgate, dup), _ = _mm(
        "ffn_down_bwd", "nt", [dx2b], [pl.BlockSpec((tm_f, tk_b), lambda i, j, k: (i, k))],
        [wd_full], [pl.BlockSpec((tn_f, tk_b), lambda i, j, k: (j, k))], [(0, 0, 0)], 1,
        (T // tm_f, N_CHIPS * Fp // tn_f, D // tk_b), tm_f, tn_f,
        [_sds((T, N_CHIPS * Fp), BF16)] * 2, [hid_spec] * 2, swiglu_bwd_ep,
        extra=[gate, up], extra_specs=[hid_spec, hid_spec])
    dwd = _mm_tn("dw_down", hidden, dx2b, BF16, tm=_pick(N_CHIPS * Fp, (1408, 1024, 512, 256, 128)))
    dwd = dwd.reshape(N_CHIPS, Fp, D)
    dh2, ((sib_d,),) = _mm_nt("ffn_up_bwd", [dgate, dup], [wg_g, wu_g], F32, jobs=[_job_sibling([dwd])])
    dwg = _mm_tn("dw_gate", h2, dgate, BF16, slots=N_CHIPS)
    dwu = _mm_tn("dw_up", h2, dup, BF16, slots=N_CHIPS)
    dx1, dx1b, gg_ffn = _rms_bwd("rms2_bwd", [dh2], x1, r2, g_ffn, dx2, True)
    part_d = _add_bf16("rs_add_down", dwd, core, sib_d)

    dmerged, ((sib_g, sib_u),) = _mm_nt("out_proj_bwd", [dx1b], [wo_full], F32, jobs=[_job_sibling([dwg, dwu])])
    dwo = _mm_tn("dw_out", merged, dx1b, BF16).reshape(N_CHIPS, D // N_CHIPS, D)
    du_a, du_b, dga, dgb, gbg_a, gbg_b = _merge_bwd("merge_bwd", dmerged, gates, u_a, u_b, b_gate)
    part_g = _add_bf16("rs_add_gate", dwg, core, sib_g)
    part_u = _add_bf16("rs_add_up", dwu, core, sib_u)
    do_a = _mm_nt("branch_a_bwd", [du_a], [wa_g], BF16)
    do_b = _mm_nt("branch_b_bwd", [du_b], [wb_g], BF16)
    dwa = _mm_tn("dw_a", o_a, du_a, BF16, slots=N_CHIPS)
    dwb = _mm_tn("dw_b", o_b, du_b, BF16, slots=N_CHIPS)

    (dq_a, delta_a, dbias2), ((got_d,), (sib_a, sib_b, sib_o)) = _attn_a_dq(
        "attn_a_dq", qkv, do_a, lse_a, bias2, nh, jobs=[_job_scatter([part_d]), _job_sibling([dwa, dwb, dwo])])
    part_a = _add_bf16("rs_add_a", dwa, core, sib_a)
    part_b = _add_bf16("rs_add_b", dwb, core, sib_b)
    part_o = _add_bf16("rs_add_out", dwo, core, sib_o)
    full_d = _sum4("rs_sum_down", got_d, part_d, chip, core)
    (dk_a, dv_a), ((got_g,),) = _attn_a_dkv("attn_a_dkv", qkv, do_a, lse_a, delta_a, bias2, nh,
                                            jobs=[_job_scatter([part_g])])
    full_g = _sum4("rs_sum_gate", got_g, part_g, chip, core)
    (dq_b, delta_b), ((got_u,), (got_a, got_b, got_o)) = _fox_dq(
        "fox_dq", qkv, do_b, lse_b, cum, cumt, nh, jobs=[_job_scatter([part_u]), _job_scatter([part_a, part_b, part_o])])
    full_u = _sum4("rs_sum_up", got_u, part_u, chip, core)
    full_a = _sum4("rs_sum_a", got_a, part_a, chip, core)
    full_b = _sum4("rs_sum_b", got_b, part_b, chip, core)
    full_o = _sum4("rs_sum_out", got_o, part_o, chip, core)
    (dk_b, dv_b, dcumt), ((g_d, g_g, g_u, g_a, g_b, g_o),) = _fox_dkv(
        "fox_dkv", qkv, do_b, lse_b, delta_b, cum, cumt, nh,
        jobs=[_job_swap([full_d, full_g, full_u, full_a, full_b, full_o])])
    df, gbf = _fox_post("fox_post", dcumt, fl, bf_pad)

    dqkv = jnp.concatenate([dq_a, dk_a, dv_a, dq_b, dk_b, dv_b], axis=1)
    dgates = jnp.concatenate([dga, dgb], axis=1)
    dwc_q = _mm_tn("dw_in_qkv", h1, dqkv, BF16)
    dwc_g = _mm_tn("dw_in_gates", h1, dgates, BF16)
    dwc_f = _mm_tn("dw_in_f", h1, df, BF16, tn=LANE)
    dwc = jnp.concatenate([dwc_q, dwc_g, dwc_f], axis=1)
    zeros_blk = jnp.zeros((D, LANE), BF16)
    win_parts = []
    for k in range(N_CHIPS):
        cols = [dwc[:, lay[k]["s"] * LANE:lay[k]["e"] * LANE]]
        nb = lay[k]["e"] - lay[k]["s"]
        if lay[k]["f"]:
            cols.append(dwc[:, nmain:nmain + LANE])
            nb += 1
        cols += [zeros_blk] * (nbw - nb)
        win_parts.append(jnp.concatenate(cols, axis=1) if len(cols) > 1 else cols[0])
    dwin = jnp.stack(win_parts, axis=0)
    ((sib_in,),) = _comm_only("rs_sibling_in", [_job_sibling([dwin])])
    part_in = _add_bf16("rs_add_in", dwin, core, sib_in)
    dh_q, ((got_in,),) = _mm_nt("proj_qkv_bwd", [dqkv], [wc], F32, k0_list=[0], tk=_pick(W6, (1024, 512, 256, 128)),
                                jobs=[_job_scatter([part_in])])
    full_in = _sum4("rs_sum_in", got_in, part_in, chip, core)
    dh_g, ((g_win,),) = _mm_nt("proj_gates_bwd", [dgates], [wc], F32, k0_list=[W6],
                               tk=_pick(math_gcd(W6, 2 * D), (1024, 512, 256, 128)), jobs=[_job_swap([full_in])])
    dh_f = _mm_nt("proj_f_bwd", [df], [wc], F32, k0_list=[nmain], tk=LANE)
    grad_x, gg_mix = _rms_bwd("rms1_bwd", [dh_q, dh_g, dh_f], xs, r1, g_mix, dx1, False)
    g_in = lax.switch(chip, [functools.partial(_from_window, lay_k=lay[k]) for k in range(N_CHIPS)], g_win)
    g_g, g_u, g_d = g_g[:, :Fl], g_u[:, :Fl], g_d[:Fl, :]

    big = {}
    for nm, w, g, m, v in (("w_in", w_in, g_in, m_w_in, v_w_in), ("w_branch_a", w_branch_a, g_a, m_w_branch_a, v_w_branch_a),
                           ("w_branch_b", w_branch_b, g_b, m_w_branch_b, v_w_branch_b), ("w_out", w_out, g_o, m_w_out, v_w_out),
                           ("w_gate_ffn", w_gate_ffn, g_g, m_w_gate_ffn, v_w_gate_ffn),
                           ("w_up_ffn", w_up_ffn, g_u, m_w_up_ffn, v_w_up_ffn),
                           ("w_down_ffn", w_down_ffn, g_d, m_w_down_ffn, v_w_down_ffn)):
        d, mn, vn = _adamw(f"adamw_{nm}", w[0], g, m[0], v[0])
        big[nm] = (g[None], d[None], mn[None], vn[None])

    g_rel = _rel_grad_of(dbias2, onehot)
    small = [("loss", loss_part[:, :1], None, None, None),
             ("g_mix", gg_mix, g_mix, m_g_mix, v_g_mix), ("b_f", gbf[:, :nhb], b_f, m_b_f, v_b_f),
             ("b_gate", jnp.concatenate([gbg_a, gbg_b], axis=1), b_gate, m_b_gate, v_b_gate),
             ("rel_bias", g_rel, rel_bias, m_rel_bias, v_rel_bias), ("g_ffn", gg_ffn, g_ffn, m_g_ffn, v_g_ffn),
             ("g_final", gg_final, g_final, m_g_final, v_g_final)]
    sizes = [int(np.prod(s[1].shape)) for s in small]
    total = sum(sizes)
    npad = -(-total // 1024) * 1024

    def pack(arrs):
        flat = jnp.concatenate([a.reshape(-1).astype(F32) for a in arrs])
        return jnp.pad(flat, (0, npad - total)).reshape(8, npad // 8)

    zero1 = jnp.zeros((1,), F32)
    g_all = _small_allreduce("small_allreduce", pack([s[1] for s in small]))
    w_s = pack([zero1 if s[2] is None else s[2] for s in small])
    m_s = pack([zero1 if s[3] is None else s[3] for s in small])
    v_s = pack([zero1 + 1.0 if s[4] is None else s[4] for s in small])
    d_s, mn_s, vn_s = _adamw("adamw_small", w_s, g_all, m_s, v_s)

    def unpack(packed):
        flat = packed.reshape(-1)
        out, pos = {}, 0
        for s, n in zip(small, sizes):
            if s[2] is not None:
                out[s[0]] = flat[pos:pos + n].reshape(s[2].shape)
            else:
                out[s[0]] = flat[pos:pos + n].reshape(())
            pos += n
        return out

    gs, ds, ms, vs = unpack(g_all), unpack(d_s), unpack(mn_s), unpack(vn_s)
    order = ["g_mix", "w_in", "b_f", "b_gate", "rel_bias", "w_branch_a", "w_branch_b", "w_out", "g_ffn",
             "w_gate_ffn", "w_up_ffn", "w_down_ffn", "g_final"]
    res = [[], [], [], []]
    for nm in order:
        four = big[nm] if nm in big else (gs[nm], ds[nm], ms[nm], vs[nm])
        for q in range(4):
            res[q].append(four[q])
    return (gs["loss"], grad_x[None], *res[0], *res[1], *res[2], *res[3])


def math_gcd(a, b):
    while b:
        a, b = b, a % b
    return a
```

```python
import functools

import numpy as np
import jax
import jax.numpy as jnp
from jax import lax
from jax.experimental import pallas as pl
from jax.experimental.pallas import tpu as pltpu

F32 = jnp.float32
BF16 = jnp.bfloat16
LANE = 128
HEAD_DIM = 128
CHUNK = 64
LEFT_CHUNKS = 8
GROUP = 128
WIN_BLOCKS = 5
WIN = WIN_BLOCKS * GROUP
BAND = (LEFT_CHUNKS + 1) * CHUNK
RMS_EPS = 1e-6
NEG_INF = -1e30
ADAM_LR = 0.001
ADAM_B1 = 0.9
ADAM_B2 = 0.999
ADAM_EPS = 1e-08
ADAM_WD = 0.01
ADAM_STEP = 10
N_CHIPS = 4
MESH = pl.DeviceIdType.MESH
VMEM_LIMIT = 52 * 1024 * 1024
ANY = pl.BlockSpec(memory_space=pl.ANY)

NN = (((1,), (0,)), ((), ()))
NT = (((1,), (1,)), ((), ()))
TN = (((0,), (0,)), ((), ()))


def _cp(sem):
    return pltpu.CompilerParams(dimension_semantics=sem, vmem_limit_bytes=VMEM_LIMIT)


def _sds(shape, dtype):
    return jax.ShapeDtypeStruct(shape, dtype)


def _pick(n, prefs):
    for p in prefs:
        if n % p == 0:
            return p
    return n


def _sigmoid(v):
    return 1.0 / (1.0 + jnp.exp(-v))


def _split3(v):
    hi = v.astype(BF16)
    r1 = v - hi.astype(F32)
    mid = r1.astype(BF16)
    lo = (r1 - mid.astype(F32)).astype(BF16)
    return hi, mid, lo


def _col_of(blk, h):
    lane = lax.broadcasted_iota(jnp.int32, blk.shape, 1)
    return jnp.sum(jnp.where(lane == h, blk, 0.0), axis=1, keepdims=True)


def _put_col(ref, h, col):
    lane = lax.broadcasted_iota(jnp.int32, ref.shape, 1)
    ref[...] = jnp.where(lane == h, col, ref[...])


def _mm(name, mode, a_list, a_specs, b_list, b_specs, pairs, n_acc, grid, tm, tn,
        out_shapes, out_specs, epilogue, extra=(), extra_specs=(), jobs=()):
    n_a, n_b, n_e, n_o = len(a_list), len(b_list), len(extra), len(out_shapes)
    nk = grid[2]
    dn = {"nn": NN, "nt": NT, "tn": TN}[mode]

    def body(*refs):
        a_refs = refs[:n_a]
        b_refs = refs[n_a:n_a + n_b]
        e_refs = refs[n_a + n_b:n_a + n_b + n_e]
        o_refs = refs[n_a + n_b + n_e:n_a + n_b + n_e + n_o]
        acc_refs = refs[n_a + n_b + n_e + n_o:]
        k = pl.program_id(2)

        @pl.when(k == 0)
        def _():
            for acc in acc_refs:
                acc[...] = jnp.zeros_like(acc)

        for ai, bi, ci in pairs:
            acc_refs[ci][...] += lax.dot_general(a_refs[ai][...], b_refs[bi][...], dn,
                                                 preferred_element_type=F32)

        @pl.when(k == nk - 1)
        def _():
            epilogue([acc[...] for acc in acc_refs], e_refs, o_refs)

    return _pcall(
        body, name=name, grid=grid,
        in_specs=list(a_specs) + list(b_specs) + list(extra_specs),
        out_specs=list(out_specs), out_shape=list(out_shapes),
        scratch_shapes=[pltpu.VMEM((tm, tn), F32) for _ in range(n_acc)],
        sem=("parallel", "parallel", "arbitrary"), jobs=jobs,
    )(*a_list, *b_list, *extra)


def _one(res, jobs):
    outs, jouts = res
    return (outs[0], jouts) if jobs else outs[0]


def _store(dtype):
    def ep(accs, e_refs, o_refs):
        o_refs[0][...] = accs[0].astype(dtype)
    return ep


def _mm_nn(name, a, b, out_dtype, *, b_col0=0, n=None, tm=512, tn=None, tk=None, residual=None, jobs=()):
    M, K = a.shape
    if b.ndim == 3:
        Ns = b.shape[2]
        n = b.shape[0] * Ns
        tn = tn or _pick(Ns, (1408, 1024, 512, 256, 128))
        nps = Ns // tn
        b_spec = pl.BlockSpec((None, tk or _pick(K, (1024, 512, 256, 128)), tn),
                              lambda i, j, k: (j // nps, k, j % nps))
    else:
        n = n or b.shape[1]
        tn = tn or _pick(math_gcd(n, b_col0) if b_col0 else n, (2048, 1024, 512, 256, 128))
        assert b_col0 % tn == 0 and n % tn == 0
        c0 = b_col0 // tn
        b_spec = pl.BlockSpec((tk or _pick(K, (1024, 512, 256, 128)), tn), lambda i, j, k: (k, c0 + j))
    tk = tk or _pick(K, (1024, 512, 256, 128))
    tm = _pick(M, (tm, 256, 128))
    grid = (M // tm, n // tn, K // tk)
    a_spec = pl.BlockSpec((tm, tk), lambda i, j, k: (i, k))
    o_spec = pl.BlockSpec((tm, tn), lambda i, j, k: (i, j))
    if residual is None:
        return _one(_mm(name, "nn", [a], [a_spec], [b], [b_spec], [(0, 0, 0)], 1, grid, tm, tn,
                        [_sds((M, n), out_dtype)], [o_spec], _store(out_dtype), jobs=jobs), jobs)

    def ep(accs, e_refs, o_refs):
        o_refs[0][...] = (e_refs[0][...] + accs[0]).astype(out_dtype)
    return _one(_mm(name, "nn", [a], [a_spec], [b], [b_spec], [(0, 0, 0)], 1, grid, tm, tn,
                    [_sds((M, n), out_dtype)], [o_spec], ep, extra=[residual], extra_specs=[o_spec], jobs=jobs), jobs)


def _mm_nt(name, a_list, b_list, out_dtype, *, k0_list=None, tm=512, tn=None, tk=None, jobs=()):
    M, K = a_list[0].shape
    b0 = b_list[0]
    N = b0.shape[1] if b0.ndim == 3 else b0.shape[0]
    tm = _pick(M, (tm, 256, 128))
    tn = tn or _pick(N, (1024, 512, 256, 128))
    if b0.ndim == 3:
        Ks = b0.shape[2]
        tk = tk or _pick(Ks, (1408, 1024, 512, 256, 128))
        kps = Ks // tk
        b_specs = [pl.BlockSpec((None, tn, tk), lambda i, j, k: (k // kps, j, k % kps)) for _ in b_list]
    else:
        tk = tk or _pick(K, (1024, 896, 512, 256, 128))
        k0_list = k0_list or [0] * len(b_list)
        b_specs = []
        for k0 in k0_list:
            assert k0 % tk == 0
            b_specs.append(pl.BlockSpec((tn, tk), functools.partial(lambda i, j, k, c: (j, c + k), c=k0 // tk)))
    grid = (M // tm, N // tn, K // tk)
    a_specs = [pl.BlockSpec((tm, tk), lambda i, j, k: (i, k)) for _ in a_list]
    o_spec = pl.BlockSpec((tm, tn), lambda i, j, k: (i, j))
    pairs = [(p, p, 0) for p in range(len(a_list))]
    return _one(_mm(name, "nt", a_list, a_specs, b_list, b_specs, pairs, 1, grid, tm, tn,
                    [_sds((M, N), out_dtype)], [o_spec], _store(out_dtype), jobs=jobs), jobs)


def _mm_tn(name, a, b, out_dtype, *, slots=None, tm=None, tn=None, tk=512, jobs=()):
    Kc, Mo = a.shape
    No = b.shape[1]
    tm = tm or _pick(Mo, (1024, 704, 512, 256, 128))
    tk = _pick(Kc, (tk, 256, 128))
    if slots:
        Ns = No // slots
        tn = tn or _pick(Ns, (1408, 1024, 512, 256, 128))
        nps = Ns // tn
        o_spec = pl.BlockSpec((None, tm, tn), lambda i, j, k: (j // nps, i, j % nps))
        o_shape = _sds((slots, Mo, Ns), out_dtype)
    else:
        tn = tn or _pick(No, (1024, 512, 256, 128))
        o_spec = pl.BlockSpec((tm, tn), lambda i, j, k: (i, j))
        o_shape = _sds((Mo, No), out_dtype)
    grid = (Mo // tm, No // tn, Kc // tk)
    a_spec = pl.BlockSpec((tk, tm), lambda i, j, k: (k, i))
    b_spec = pl.BlockSpec((tk, tn), lambda i, j, k: (k, j))
    return _one(_mm(name, "tn", [a], [a_spec], [b], [b_spec], [(0, 0, 0)], 1, grid, tm, tn,
                    [o_shape], [o_spec], _store(out_dtype), jobs=jobs), jobs)


def _cast_bf16(name, w, chip):
    R, C = w.shape
    tr = _pick(R, (256, 128, 64, 32, 16))

    def body(k_ref, w_ref, o_ref):
        o_ref[...] = w_ref[...].astype(BF16)

    gs = pltpu.PrefetchScalarGridSpec(
        num_scalar_prefetch=1, grid=(R // tr,),
        in_specs=[pl.BlockSpec((tr, C), lambda i, k: (i, 0))],
        out_specs=pl.BlockSpec((None, tr, C), lambda i, k: (k[0], i, 0)))
    return pl.pallas_call(body, name=name, grid_spec=gs, out_shape=_sds((N_CHIPS, R, C), BF16),
                          compiler_params=_cp(("parallel",)))(jnp.reshape(chip, (1,)).astype(jnp.int32), w)


def _rms_fwd(name, x, g):
    T, D = x.shape
    tr = _pick(T, (256, 128))

    def body(x_ref, g_ref, h_ref, r_ref):
        xv = x_ref[...]
        r = lax.rsqrt(jnp.mean(xv * xv, axis=1, keepdims=True) + RMS_EPS)
        h_ref[...] = (xv * r * g_ref[...]).astype(BF16)
        r_ref[...] = r

    row = pl.BlockSpec((tr, D), lambda i: (i, 0))
    return pl.pallas_call(
        body, name=name, grid=(T // tr,),
        in_specs=[row, pl.BlockSpec((1, D), lambda i: (0, 0))],
        out_specs=[row, pl.BlockSpec((tr, 1), lambda i: (i, 0))],
        out_shape=[_sds((T, D), BF16), _sds((T, 1), F32)], compiler_params=_cp(("parallel",)),
    )(x, g)


def _final_loss_bwd(name, x2, tgt, g):
    T, D = x2.shape
    tr = _pick(T, (256, 128))

    def body(x_ref, t_ref, g_ref, dx_ref, dxb_ref, loss_ref, gg_ref):
        @pl.when(pl.program_id(0) == 0)
        def _():
            loss_ref[...] = jnp.zeros_like(loss_ref)
            gg_ref[...] = jnp.zeros_like(gg_ref)

        xv = x_ref[...]
        gv = g_ref[...]
        r = lax.rsqrt(jnp.mean(xv * xv, axis=1, keepdims=True) + RMS_EPS)
        n = xv * r
        e = n * gv - t_ref[...]
        loss_ref[...] += 0.5 * jnp.sum(jnp.mean(e * e, axis=1, keepdims=True), axis=0, keepdims=True)
        dy = e * (1.0 / D)
        gg_ref[...] += jnp.sum(dy * n, axis=0, keepdims=True)
        gy = dy * gv
        dx = r * (gy - n * jnp.mean(gy * n, axis=1, keepdims=True))
        dx_ref[...] = dx
        dxb_ref[...] = dx.astype(BF16)

    row = pl.BlockSpec((tr, D), lambda i: (i, 0))
    vec = pl.BlockSpec((1, D), lambda i: (0, 0))
    return pl.pallas_call(
        body, name=name, grid=(T // tr,),
        in_specs=[row, row, vec],
        out_specs=[row, row, pl.BlockSpec((1, LANE), lambda i: (0, 0)), vec],
        out_shape=[_sds((T, D), F32), _sds((T, D), BF16), _sds((1, LANE), F32), _sds((1, D), F32)],
        compiler_params=_cp(("arbitrary",)),
    )(x2, tgt, g)


def _rms_bwd(name, dh_list, x, r, g, dres, want_bf16):
    T, D = x.shape
    tr = _pick(T, (128,))
    n_dh = len(dh_list)

    def body(*refs):
        dh_refs = refs[:n_dh]
        x_ref, r_ref, g_ref, dres_ref = refs[n_dh:n_dh + 4]
        outs = refs[n_dh + 4:]
        gg_ref = outs[-1]

        @pl.when(pl.program_id(0) == 0)
        def _():
            gg_ref[...] = jnp.zeros_like(gg_ref)

        dh = dh_refs[0][...]
        for ref in dh_refs[1:]:
            dh = dh + ref[...]
        rv = r_ref[...]
        n = x_ref[...] * rv
        gg_ref[...] += jnp.sum(dh * n, axis=0, keepdims=True)
        gy = dh * g_ref[...]
        dx = dres_ref[...] + rv * (gy - n * jnp.mean(gy * n, axis=1, keepdims=True))
        outs[0][...] = dx
        if want_bf16:
            outs[1][...] = dx.astype(BF16)

    row = pl.BlockSpec((tr, D), lambda i: (i, 0))
    vec = pl.BlockSpec((1, D), lambda i: (0, 0))
    out_specs = [row] + ([row] if want_bf16 else []) + [vec]
    out_shape = [_sds((T, D), F32)] + ([_sds((T, D), BF16)] if want_bf16 else []) + [_sds((1, D), F32)]
    return pl.pallas_call(
        body, name=name, grid=(T // tr,),
        in_specs=[row] * n_dh + [row, pl.BlockSpec((tr, 1), lambda i: (i, 0)), vec, row],
        out_specs=out_specs, out_shape=out_shape, compiler_params=_cp(("arbitrary",)),
    )(*dh_list, x, r, g, dres)


def _merge_fwd(name, gates, u_a, u_b, b_gate):
    T, D = u_a.shape
    tr = _pick(T, (256, 128))

    def body(ga_ref, gb_ref, ua_ref, ub_ref, ba_ref, bb_ref, o_ref):
        sa = _sigmoid(ga_ref[...] + ba_ref[...])
        sb = _sigmoid(gb_ref[...] + bb_ref[...])
        o_ref[...] = (sa * ua_ref[...] + sb * ub_ref[...]).astype(BF16)

    row = pl.BlockSpec((tr, D), lambda i: (i, 0))
    row1 = pl.BlockSpec((tr, D), lambda i: (i, 1))
    v0 = pl.BlockSpec((1, D), lambda i: (0, 0))
    v1 = pl.BlockSpec((1, D), lambda i: (0, 1))
    return pl.pallas_call(
        body, name=name, grid=(T // tr,),
        in_specs=[row, row1, row, row, v0, v1], out_specs=row,
        out_shape=_sds((T, D), BF16), compiler_params=_cp(("parallel",)),
    )(gates, gates, u_a, u_b, b_gate, b_gate)


def _merge_bwd(name, dm, gates, u_a, u_b, b_gate):
    T, D = u_a.shape
    tr = _pick(T, (128,))

    def body(dm_ref, ga_ref, gb_ref, ua_ref, ub_ref, ba_ref, bb_ref, dua_ref, dub_ref, dga_ref, dgb_ref,
             gba_ref, gbb_ref):
        @pl.when(pl.program_id(0) == 0)
        def _():
            gba_ref[...] = jnp.zeros_like(gba_ref)
            gbb_ref[...] = jnp.zeros_like(gbb_ref)

        d = dm_ref[...]
        sa = _sigmoid(ga_ref[...] + ba_ref[...])
        sb = _sigmoid(gb_ref[...] + bb_ref[...])
        dua_ref[...] = (d * sa).astype(BF16)
        dub_ref[...] = (d * sb).astype(BF16)
        dga = d * ua_ref[...] * sa * (1.0 - sa)
        dgb = d * ub_ref[...] * sb * (1.0 - sb)
        dga_ref[...] = dga.astype(BF16)
        dgb_ref[...] = dgb.astype(BF16)
        gba_ref[...] += jnp.sum(dga, axis=0, keepdims=True)
        gbb_ref[...] += jnp.sum(dgb, axis=0, keepdims=True)

    row = pl.BlockSpec((tr, D), lambda i: (i, 0))
    row1 = pl.BlockSpec((tr, D), lambda i: (i, 1))
    v0 = pl.BlockSpec((1, D), lambda i: (0, 0))
    v1 = pl.BlockSpec((1, D), lambda i: (0, 1))
    outs = pl.pallas_call(
        body, name=name, grid=(T // tr,),
        in_specs=[row, row, row1, row, row, v0, v1],
        out_specs=[row, row, row, row, v0, v0],
        out_shape=[_sds((T, D), BF16), _sds((T, D), BF16), _sds((T, D), BF16), _sds((T, D), BF16),
                   _sds((1, D), F32), _sds((1, D), F32)],
        compiler_params=_cp(("arbitrary",)),
    )(dm, gates, gates, u_a, u_b, b_gate, b_gate)
    return outs


def _adamw(name, w, g, m, v):
    R, C = w.shape
    tr = _pick(R, (64, 32, 16, 8))
    c1 = 1.0 - ADAM_B1 ** ADAM_STEP
    c2 = 1.0 - ADAM_B2 ** ADAM_STEP

    def body(w_ref, g_ref, m_ref, v_ref, d_ref, mo_ref, vo_ref):
        gv = g_ref[...]
        mn = ADAM_B1 * m_ref[...] + (1.0 - ADAM_B1) * gv
        vn = ADAM_B2 * v_ref[...] + (1.0 - ADAM_B2) * (gv * gv)
        d_ref[...] = -ADAM_LR * ((mn / c1) / (jnp.sqrt(vn / c2) + ADAM_EPS) + ADAM_WD * w_ref[...])
        mo_ref[...] = mn
        vo_ref[...] = vn

    blk = pl.BlockSpec((tr, C), lambda i: (i, 0))
    return pl.pallas_call(
        body, name=name, grid=(R // tr,),
        in_specs=[blk] * 4, out_specs=[blk] * 3,
        out_shape=[_sds((R, C), F32)] * 3, compiler_params=_cp(("parallel",)),
    )(w, g, m, v)


def _add_bf16(name, a, a_row0, b):
    S, h, C = b.shape
    tr = _pick(h, (256, 128, 64, 32, 16))
    nb = h // tr

    def body(off_ref, a_ref, b_ref, o_ref):
        o_ref[...] = (a_ref[...].astype(F32) + b_ref[...].astype(F32)).astype(BF16)

    gs = pltpu.PrefetchScalarGridSpec(
        num_scalar_prefetch=1, grid=(S, nb),
        in_specs=[pl.BlockSpec((None, tr, C), lambda s, i, off: (s, off[0] * nb + i, 0)),
                  pl.BlockSpec((None, tr, C), lambda s, i, off: (s, i, 0))],
        out_specs=pl.BlockSpec((None, tr, C), lambda s, i, off: (s, i, 0)))
    return pl.pallas_call(body, name=name, grid_spec=gs, out_shape=_sds((S, h, C), BF16),
                          compiler_params=_cp(("parallel", "parallel")))(
        jnp.reshape(a_row0, (1,)).astype(jnp.int32), a, b)


def _sum4(name, got, mine, chip, core):
    S, h, C = got.shape
    tr = _pick(h, (256, 128, 64, 32, 16))
    nb = h // tr

    def body(chip_ref, core_ref, m_ref, g_ref, o_ref):
        acc = m_ref[...].astype(F32)
        for s in range(S):
            acc = acc + g_ref[s].astype(F32)
        o_ref[...] = acc

    gs = pltpu.PrefetchScalarGridSpec(
        num_scalar_prefetch=2, grid=(nb,),
        in_specs=[pl.BlockSpec((None, tr, C), lambda i, kc, cc: (kc[0], i, 0)),
                  pl.BlockSpec((S, tr, C), lambda i, kc, cc: (0, i, 0))],
        out_specs=pl.BlockSpec((tr, C), lambda i, kc, cc: (cc[0] * nb + i, 0)))
    return pl.pallas_call(body, name=name, grid_spec=gs, out_shape=_sds((2 * h, C), F32),
                          compiler_params=_cp(("parallel",)))(
        jnp.reshape(chip, (1,)).astype(jnp.int32), jnp.reshape(core, (1,)).astype(jnp.int32), mine, got)


def _place():
    x, y, c = lax.axis_index("x"), lax.axis_index("y"), lax.axis_index("c")
    chips = [(1 - x, y), (x, 1 - y), (1 - x, 1 - y)]
    return x, y, c, chips


def _allgather(name, shards):
    n = len(shards)

    def body(*refs):
        out_refs = refs[n:2 * n]
        send_sems, recv_sems = refs[2 * n:]
        x, y, c, chips = _place()
        k = 2 * x + y
        sibling = (x, y, 1 - c)
        firsts, passed = [], []
        for a in range(n):
            out = out_refs[a]
            h = out.shape[1] // 2
            for j, (cx, cy) in enumerate(chips):
                rows = out.at[k, pl.ds(c * h, h), :]
                cp = pltpu.make_async_remote_copy(
                    src_ref=rows, dst_ref=rows,
                    send_sem=send_sems.at[6 * a + j], recv_sem=recv_sems.at[6 * a + j],
                    device_id=(cx, cy, c), device_id_type=MESH)
                cp.start()
                firsts.append(cp)
        for a in range(n):
            out = out_refs[a]
            h = out.shape[1] // 2
            for j, (cx, cy) in enumerate(chips):
                kj = 2 * cx + cy
                rows = out.at[kj, pl.ds(c * h, h), :]
                pltpu.make_async_remote_copy(
                    src_ref=rows, dst_ref=rows, send_sem=send_sems.at[6 * a + j], recv_sem=recv_sems.at[6 * a + j],
                    device_id=(cx, cy, c), device_id_type=MESH).wait_recv()
                fw = pltpu.make_async_remote_copy(
                    src_ref=rows, dst_ref=rows, send_sem=send_sems.at[6 * a + 3 + j],
                    recv_sem=recv_sems.at[6 * a + 3 + j], device_id=sibling, device_id_type=MESH)
                fw.start()
                passed.append(fw)
        for a in range(n):
            out = out_refs[a]
            h = out.shape[1] // 2
            for j, (cx, cy) in enumerate(chips):
                kj = 2 * cx + cy
                rows = out.at[kj, pl.ds((1 - c) * h, h), :]
                pltpu.make_async_remote_copy(
                    src_ref=rows, dst_ref=rows, send_sem=send_sems.at[6 * a + 3 + j],
                    recv_sem=recv_sems.at[6 * a + 3 + j], device_id=sibling, device_id_type=MESH).wait_recv()
        for cp in firsts + passed:
            cp.wait_send()

    return pl.pallas_call(
        body, name=name,
        in_specs=[ANY] * n, out_specs=[ANY] * n,
        out_shape=[_sds(s.shape, s.dtype) for s in shards],
        input_output_aliases={a: a for a in range(n)},
        scratch_shapes=[pltpu.SemaphoreType.DMA((6 * n,)), pltpu.SemaphoreType.DMA((6 * n,))],
    )(*shards)


class _Job:
    def __init__(self, ins, out_shapes, aliases, n_sems, start, finish):
        self.ins, self.out_shapes, self.aliases, self.n_sems = list(ins), list(out_shapes), dict(aliases), n_sems
        self.start, self.finish = start, finish


def _rdma(src, dst, ss, rs, idx, to):
    return pltpu.make_async_remote_copy(src_ref=src, dst_ref=dst, send_sem=ss.at[idx], recv_sem=rs.at[idx],
                                        device_id=to, device_id_type=MESH)


def _job_gather_ici(bufs):
    def descs(outs, ss, rs, incoming):
        x, y, c, chips = _place()
        res = []
        for a, out in enumerate(outs):
            h = out.shape[1] // 2
            for j, (cx, cy) in enumerate(chips):
                rows = out.at[(2 * cx + cy) if incoming else (2 * x + y), pl.ds(c * h, h), :]
                res.append(_rdma(rows, rows, ss, rs, 3 * a + j, (cx, cy, c)))
        return res

    def start(ins, outs, ss, rs):
        for d in descs(outs, ss, rs, False):
            d.start()

    def finish(ins, outs, ss, rs):
        for d in descs(outs, ss, rs, True):
            d.wait_recv()
        for d in descs(outs, ss, rs, False):
            d.wait_send()

    return _Job(bufs, [_sds(b.shape, b.dtype) for b in bufs], {a: a for a in range(len(bufs))}, 3 * len(bufs),
                start, finish)


def _job_gather_d2d(bufs):
    def descs(outs, ss, rs, incoming):
        x, y, c, chips = _place()
        res = []
        for a, out in enumerate(outs):
            h = out.shape[1] // 2
            for j, (cx, cy) in enumerate(chips):
                rows = out.at[2 * cx + cy, pl.ds(((1 - c) if incoming else c) * h, h), :]
                res.append(_rdma(rows, rows, ss, rs, 3 * a + j, (x, y, 1 - c)))
        return res

    def start(ins, outs, ss, rs):
        for d in descs(outs, ss, rs, False):
            d.start()

    def finish(ins, outs, ss, rs):
        for d in descs(outs, ss, rs, True):
            d.wait_recv()
        for d in descs(outs, ss, rs, False):
            d.wait_send()

    return _Job(bufs, [_sds(b.shape, b.dtype) for b in bufs], {a: a for a in range(len(bufs))}, 3 * len(bufs),
                start, finish)


def _job_sibling(grads):
    def descs(ins, outs, ss, rs):
        x, y, c, _ = _place()
        res = []
        for a, (g, out) in enumerate(zip(ins, outs)):
            h = g.shape[1] // 2
            res.append(_rdma(g.at[:, pl.ds((1 - c) * h, h), :], out, ss, rs, a, (x, y, 1 - c)))
        return res

    def start(ins, outs, ss, rs):
        for d in descs(ins, outs, ss, rs):
            d.start()

    def finish(ins, outs, ss, rs):
        for d in descs(ins, outs, ss, rs):
            d.wait()

    return _Job(grads, [_sds((g.shape[0], g.shape[1] // 2, g.shape[2]), g.dtype) for g in grads], {}, len(grads),
                start, finish)


def _job_scatter(parts):
    def descs(ins, outs, ss, rs):
        x, y, c, chips = _place()
        res = []
        for a, (p, out) in enumerate(zip(ins, outs)):
            for j, (cx, cy) in enumerate(chips):
                res.append(_rdma(p.at[2 * cx + cy], out.at[j], ss, rs, 3 * a + j, (cx, cy, c)))
        return res

    def start(ins, outs, ss, rs):
        for d in descs(ins, outs, ss, rs):
            d.start()

    def finish(ins, outs, ss, rs):
        for d in descs(ins, outs, ss, rs):
            d.wait()

    return _Job(parts, [_sds((3,) + p.shape[1:], p.dtype) for p in parts], {}, 3 * len(parts), start, finish)


def _job_swap(fulls):
    def descs(outs, ss, rs, incoming):
        x, y, c, _ = _place()
        res = []
        for a, out in enumerate(outs):
            h = out.shape[0] // 2
            rows = out.at[pl.ds(((1 - c) if incoming else c) * h, h), :]
            res.append(_rdma(rows, rows, ss, rs, a, (x, y, 1 - c)))
        return res

    def start(ins, outs, ss, rs):
        for d in descs(outs, ss, rs, False):
            d.start()

    def finish(ins, outs, ss, rs):
        for d in descs(outs, ss, rs, True):
            d.wait_recv()
        for d in descs(outs, ss, rs, False):
            d.wait_send()

    return _Job(fulls, [_sds(f.shape, f.dtype) for f in fulls], {a: a for a in range(len(fulls))}, len(fulls),
                start, finish)


def _pcall(body, *, name, grid, in_specs, out_specs, out_shape, scratch_shapes=(), sem, jobs=()):
    in_specs, out_specs, out_shape = list(in_specs), list(out_specs), list(out_shape)
    scratch = list(scratch_shapes)
    n_in, n_out, n_scr = len(in_specs), len(out_shape), len(scratch)
    if not jobs:
        call = pl.pallas_call(body, name=name, grid=grid, in_specs=in_specs, out_specs=out_specs, out_shape=out_shape,
                              scratch_shapes=scratch, compiler_params=_cp(sem))
        return lambda *args: (call(*args), [])
    jin = sum(len(j.ins) for j in jobs)
    jout = sum(len(j.out_shapes) for j in jobs)
    aliases, pi, po = {}, n_in, n_out
    for j in jobs:
        for ia, oa in j.aliases.items():
            aliases[pi + ia] = po + oa
        pi, po = pi + len(j.ins), po + len(j.out_shapes)

    def wrapped(*refs):
        ins = refs[:n_in]
        jins = refs[n_in:n_in + jin]
        outs = refs[n_in + jin:n_in + jin + n_out]
        jouts = refs[n_in + jin + n_out:n_in + jin + n_out + jout]
        scr = refs[n_in + jin + n_out + jout:n_in + jin + n_out + jout + n_scr]
        sems = refs[n_in + jin + n_out + jout + n_scr:]
        first, last = None, None
        for d, g in enumerate(grid):
            f, l = pl.program_id(d) == 0, pl.program_id(d) == g - 1
            first = f if first is None else jnp.logical_and(first, f)
            last = l if last is None else jnp.logical_and(last, l)

        def each(what):
            pi, po = 0, 0
            for q, j in enumerate(jobs):
                getattr(j, what)(jins[pi:pi + len(j.ins)], jouts[po:po + len(j.out_shapes)], sems[2 * q], sems[2 * q + 1])
                pi, po = pi + len(j.ins), po + len(j.out_shapes)

        @pl.when(first)
        def _():
            each("start")

        body(*ins, *outs, *scr)

        @pl.when(last)
        def _():
            each("finish")

    call = pl.pallas_call(
        wrapped, name=name, grid=grid,
        in_specs=in_specs + [ANY] * jin, out_specs=out_specs + [ANY] * jout,
        out_shape=out_shape + [s for j in jobs for s in j.out_shapes],
        input_output_aliases=aliases,
        scratch_shapes=scratch + [pltpu.SemaphoreType.DMA((j.n_sems,)) for j in jobs for _ in range(2)],
        compiler_params=_cp(("arbitrary",) * len(grid)))

    def run(*args):
        res = call(*args, *[a for j in jobs for a in j.ins])
        main, rest, per_job = list(res[:n_out]), list(res[n_out:]), []
        for j in jobs:
            per_job.append(rest[:len(j.out_shapes)])
            rest = rest[len(j.out_shapes):]
        return main, per_job
    return run


def _comm_only(name, jobs):
    jin = sum(len(j.ins) for j in jobs)
    jout = sum(len(j.out_shapes) for j in jobs)
    aliases, pi, po = {}, 0, 0
    for j in jobs:
        for ia, oa in j.aliases.items():
            aliases[pi + ia] = po + oa
        pi, po = pi + len(j.ins), po + len(j.out_shapes)

    def body(*refs):
        jins, jouts, sems = refs[:jin], refs[jin:jin + jout], refs[jin + jout:]
        for what in ("start", "finish"):
            pi, po = 0, 0
            for q, j in enumerate(jobs):
                getattr(j, what)(jins[pi:pi + len(j.ins)], jouts[po:po + len(j.out_shapes)], sems[2 * q], sems[2 * q + 1])
                pi, po = pi + len(j.ins), po + len(j.out_shapes)

    res = pl.pallas_call(
        body, name=name, in_specs=[ANY] * jin, out_specs=[ANY] * jout,
        out_shape=[s for j in jobs for s in j.out_shapes], input_output_aliases=aliases,
        scratch_shapes=[pltpu.SemaphoreType.DMA((j.n_sems,)) for j in jobs for _ in range(2)],
    )(*[a for j in jobs for a in j.ins])
    rest, per_job = list(res), []
    for j in jobs:
        per_job.append(rest[:len(j.out_shapes)])
        rest = rest[len(j.out_shapes):]
    return per_job


def _small_allreduce(name, v):
    m_per, n = v.shape

    def body(x_ref, sum_ref, all_ref, send_sems, recv_sems, local_sem):
        x, y, c, chips = _place()
        me, sibling = (x, y, c), (x, y, 1 - c)

        def rows(px, py, pc):
            return all_ref.at[pl.ds((4 * px + 2 * py + pc) * m_per, m_per), :]

        def copy(kk, block, to, src=None):
            return pltpu.make_async_remote_copy(
                src_ref=rows(*block) if src is None else src, dst_ref=rows(*block),
                send_sem=send_sems.at[kk], recv_sem=recv_sems.at[kk], device_id=to, device_id_type=MESH)

        mine = pltpu.make_async_copy(x_ref, rows(*me), local_sem)
        mine.start()
        first = [copy(0, me, sibling, src=x_ref)]
        first += [copy(1 + j, me, (*chip, c), src=x_ref) for j, chip in enumerate(chips)]
        for cp in first:
            cp.start()
        passed = [copy(4 + j, (*chip, c), sibling) for j, chip in enumerate(chips)]
        for j, chip in enumerate(chips):
            copy(1 + j, (*chip, c), me).wait_recv()
            passed[j].start()
        copy(0, sibling, me).wait_recv()
        for j, chip in enumerate(chips):
            copy(4 + j, (*chip, 1 - c), me).wait_recv()
        for cp in first + passed:
            cp.wait_send()
        mine.wait()
        acc = all_ref[pl.ds(0, m_per), :]
        for d in range(1, 8):
            acc = acc + all_ref[pl.ds(d * m_per, m_per), :]
        sum_ref[...] = acc

    vm = pl.BlockSpec(memory_space=pltpu.VMEM)
    return pl.pallas_call(
        body, name=name, in_specs=[vm], out_specs=[vm, vm],
        out_shape=[_sds((m_per, n), F32), _sds((8 * m_per, n), F32)],
        scratch_shapes=[pltpu.SemaphoreType.DMA((7,)), pltpu.SemaphoreType.DMA((7,)), pltpu.SemaphoreType.DMA],
    )(v)[0]


def _in_layout(D, W6, nhb, Ls):
    nmain = W6 + 2 * D
    lay = []
    for k in range(N_CHIPS):
        g0, g1 = k * Ls, (k + 1) * Ls
        pieces = []
        a, b = max(g0, 0), min(g1, W6)
        if a < b:
            pieces.append((a - g0, b - g0, a))
        a, b = max(g0, W6 + nhb), min(g1, W6 + nhb + 2 * D)
        if a < b:
            pieces.append((a - g0, b - g0, a - nhb))
        a, b = max(g0, W6), min(g1, W6 + nhb)
        fpiece = (a - g0, b - g0, a - W6) if a < b else None
        assert fpiece is None or (b - a) == nhb
        main0 = min(p[2] for p in pieces)
        main1 = max(p[2] + p[1] - p[0] for p in pieces)
        lay.append(dict(pieces=pieces, f=fpiece, s=main0 // LANE, e=-(-main1 // LANE), main1=main1))
    assert sum(1 for l in lay if l["f"] is not None) == 1
    nbw = max(l["e"] - l["s"] + (1 if l["f"] else 0) for l in lay)
    for k in range(1, N_CHIPS):
        assert lay[k]["s"] >= lay[k - 1]["e"] - 1 and lay[k]["s"] > lay[k - 1]["s"]
    return lay, nbw, nmain


def _to_window(w, lay_k, nbw):
    D = w.shape[0]
    items = [(c0 - lay_k["s"] * LANE, l0, l1) for (l0, l1, c0) in lay_k["pieces"]]
    if lay_k["f"]:
        l0, l1, off = lay_k["f"]
        items.append(((lay_k["e"] - lay_k["s"]) * LANE + off, l0, l1))
    items.sort()
    cols, pos = [], 0
    for w0, l0, l1 in items:
        if w0 > pos:
            cols.append(jnp.zeros((D, w0 - pos), w.dtype))
        cols.append(w[:, l0:l1])
        pos = w0 + (l1 - l0)
    if pos < nbw * LANE:
        cols.append(jnp.zeros((D, nbw * LANE - pos), w.dtype))
    return jnp.concatenate(cols, axis=1)


def _from_window(win, lay_k):
    items = [(l0, c0 - lay_k["s"] * LANE, l1 - l0) for (l0, l1, c0) in lay_k["pieces"]]
    if lay_k["f"]:
        l0, l1, off = lay_k["f"]
        items.append((l0, (lay_k["e"] - lay_k["s"]) * LANE + off, l1 - l0))
    items.sort()
    return jnp.concatenate([win[:, w0:w0 + n] for (_, w0, n) in items], axis=1)


def _assemble_in(name, wins, lay, nbw, nmain):
    _, D, _ = wins.shape
    ncb = nmain // LANE + 1
    k1 = np.zeros(ncb, np.int32)
    i1 = np.zeros(ncb, np.int32)
    k2 = np.zeros(ncb, np.int32)
    i2 = np.zeros(ncb, np.int32)
    fl = np.zeros(ncb, np.int32)
    for b in range(ncb - 1):
        k = max(kk for kk in range(N_CHIPS) if lay[kk]["s"] <= b)
        k1[b], i1[b] = k, b - lay[k]["s"]
        if k >= 1 and b == lay[k]["s"] and lay[k - 1]["main1"] > b * LANE:
            k2[b], i2[b], fl[b] = k - 1, b - lay[k - 1]["s"], 1
    kf = [kk for kk in range(N_CHIPS) if lay[kk]["f"]][0]
    k1[ncb - 1], i1[ncb - 1] = kf, lay[kf]["e"] - lay[kf]["s"]

    def body(k1_ref, i1_ref, k2_ref, i2_ref, fl_ref, a_ref, b_ref, o_ref):
        b = pl.program_id(0)
        add = jnp.where(fl_ref[b] == 1, b_ref[...], jnp.zeros_like(b_ref))
        o_ref[...] = a_ref[...] + add

    gs = pltpu.PrefetchScalarGridSpec(
        num_scalar_prefetch=5, grid=(ncb,),
        in_specs=[pl.BlockSpec((None, D, LANE), lambda b, k1r, i1r, k2r, i2r, flr: (k1r[b], 0, i1r[b])),
                  pl.BlockSpec((None, D, LANE), lambda b, k1r, i1r, k2r, i2r, flr: (k2r[b], 0, i2r[b]))],
        out_specs=pl.BlockSpec((D, LANE), lambda b, k1r, i1r, k2r, i2r, flr: (0, b)))
    return pl.pallas_call(body, name=name, grid_spec=gs, out_shape=_sds((D, ncb * LANE), BF16),
                          compiler_params=_cp(("parallel",)))(
        jnp.asarray(k1), jnp.asarray(i1), jnp.asarray(k2), jnp.asarray(i2), jnp.asarray(fl), wins, wins)


def _hgroup(nh):
    return _pick(nh, (4, 2, 1))


def _a_specs_q(nh, G):
    ngrp = nh // G
    blk = (GROUP, G * HEAD_DIM)
    q = pl.BlockSpec(blk, lambda i, hg: (i, hg))
    ks = [pl.BlockSpec(blk, functools.partial(
        lambda i, hg, j: (jnp.maximum(i - (WIN_BLOCKS - 1) + j, 0), ngrp + hg), j=j)) for j in range(WIN_BLOCKS)]
    vs = [pl.BlockSpec(blk, functools.partial(
        lambda i, hg, j: (jnp.maximum(i - (WIN_BLOCKS - 1) + j, 0), 2 * ngrp + hg), j=j)) for j in range(WIN_BLOCKS)]
    return q, ks, vs


def _a_logits(q, ks, bias, i, scale):
    parts = [lax.dot_general(q, k, NT, preferred_element_type=F32) for k in ks]
    s = jnp.concatenate(parts, axis=1) * scale + bias
    col = lax.broadcasted_iota(jnp.int32, s.shape, 1)
    return jnp.where(col >= (WIN_BLOCKS - 1 - i) * GROUP, s, NEG_INF)


def _attn_a_fwd(name, qkv, bias2, nh, jobs=()):
    T = qkv.shape[0]
    ng = T // GROUP
    G = _hgroup(nh)
    scale = HEAD_DIM ** -0.5

    def body(q_ref, *refs):
        k_refs = refs[:WIN_BLOCKS]
        v_refs = refs[WIN_BLOCKS:2 * WIN_BLOCKS]
        bias_ref, o_ref, lse_ref = refs[2 * WIN_BLOCKS:]
        i, hg = pl.program_id(0), pl.program_id(1)

        @pl.when(hg == 0)
        def _():
            lse_ref[...] = jnp.zeros_like(lse_ref)

        for g in range(G):
            h = hg * G + g
            sl = slice(g * HEAD_DIM, (g + 1) * HEAD_DIM)
            s = _a_logits(q_ref[:, sl], [kr[:, sl] for kr in k_refs], bias_ref[h], i, scale)
            m = jnp.max(s, axis=1, keepdims=True)
            p = jnp.exp(s - m)
            l = jnp.sum(p, axis=1, keepdims=True)
            pb = (p / l).astype(BF16)
            o = jnp.zeros((GROUP, HEAD_DIM), F32)
            for j in range(WIN_BLOCKS):
                o = o + jnp.dot(pb[:, j * GROUP:(j + 1) * GROUP], v_refs[j][:, sl], preferred_element_type=F32)
            o_ref[:, sl] = o.astype(BF16)
            _put_col(lse_ref, h, m + jnp.log(l))

    q_spec, k_specs, v_specs = _a_specs_q(nh, G)
    stat = pl.BlockSpec((GROUP, LANE), lambda i, hg: (i, 0))
    return _pcall(
        body, name=name, grid=(ng, nh // G),
        in_specs=[q_spec] + k_specs + v_specs + [pl.BlockSpec((nh, GROUP, WIN), lambda i, hg: (0, 0, 0))],
        out_specs=[pl.BlockSpec((GROUP, G * HEAD_DIM), lambda i, hg: (i, hg)), stat],
        out_shape=[_sds((T, nh * HEAD_DIM), BF16), _sds((T, LANE), F32)],
        sem=("parallel", "arbitrary"), jobs=jobs,
    )(qkv, *([qkv] * (2 * WIN_BLOCKS)), bias2)


def _attn_a_dq(name, qkv, do, lse, bias2, nh, jobs=()):
    T = qkv.shape[0]
    ng = T // GROUP
    G = _hgroup(nh)
    scale = HEAD_DIM ** -0.5

    def body(q_ref, *refs):
        k_refs = refs[:WIN_BLOCKS]
        v_refs = refs[WIN_BLOCKS:2 * WIN_BLOCKS]
        do_ref, lse_ref, bias_ref, dq_ref, delta_ref, db_ref = refs[2 * WIN_BLOCKS:]
        i, hg = pl.program_id(0), pl.program_id(1)

        @pl.when(hg == 0)
        def _():
            delta_ref[...] = jnp.zeros_like(delta_ref)

        @pl.when(i == 0)
        def _():
            for g in range(G):
                db_ref[hg * G + g] = jnp.zeros((GROUP, WIN), F32)

        for g in range(G):
            h = hg * G + g
            sl = slice(g * HEAD_DIM, (g + 1) * HEAD_DIM)
            ks = [kr[:, sl] for kr in k_refs]
            s = _a_logits(q_ref[:, sl], ks, bias_ref[h], i, scale)
            p = jnp.exp(s - _col_of(lse_ref[...], h))
            dov = do_ref[:, sl]
            dp = jnp.concatenate([lax.dot_general(dov, vr[:, sl], NT, preferred_element_type=F32) for vr in v_refs],
                                 axis=1)
            delta = jnp.sum(p * dp, axis=1, keepdims=True)
            ds = p * (dp - delta)
            db_ref[h] += ds
            dsb = ds.astype(BF16)
            dq = jnp.zeros((GROUP, HEAD_DIM), F32)
            for j in range(WIN_BLOCKS):
                dq = dq + jnp.dot(dsb[:, j * GROUP:(j + 1) * GROUP], ks[j], preferred_element_type=F32)
            dq_ref[:, sl] = (dq * scale).astype(BF16)
            _put_col(delta_ref, h, delta)

    q_spec, k_specs, v_specs = _a_specs_q(nh, G)
    blk = pl.BlockSpec((GROUP, G * HEAD_DIM), lambda i, hg: (i, hg))
    stat = pl.BlockSpec((GROUP, LANE), lambda i, hg: (i, 0))
    full_b = pl.BlockSpec((nh, GROUP, WIN), lambda i, hg: (0, 0, 0))
    return _pcall(
        body, name=name, grid=(ng, nh // G),
        in_specs=[q_spec] + k_specs + v_specs + [blk, stat, full_b],
        out_specs=[blk, stat, full_b],
        out_shape=[_sds((T, nh * HEAD_DIM), BF16), _sds((T, LANE), F32), _sds((nh, GROUP, WIN), F32)],
        sem=("arbitrary", "arbitrary"), jobs=jobs,
    )(qkv, *([qkv] * (2 * WIN_BLOCKS)), do, lse, bias2)


def _attn_a_dkv(name, qkv, do, lse, delta, bias2, nh, jobs=()):
    T = qkv.shape[0]
    ng = T // GROUP
    G = _hgroup(nh)
    ngrp = nh // G
    scale = HEAD_DIM ** -0.5
    nj = WIN_BLOCKS

    def body(k_ref, v_ref, *refs):
        q_refs = refs[:nj]
        do_refs = refs[nj:2 * nj]
        lse_refs = refs[2 * nj:3 * nj]
        dl_refs = refs[3 * nj:4 * nj]
        bias_ref, dk_ref, dv_ref = refs[4 * nj:]
        r, hg = pl.program_id(0), pl.program_id(1)
        for g in range(G):
            h = hg * G + g
            sl = slice(g * HEAD_DIM, (g + 1) * HEAD_DIM)
            kv, vv = k_ref[:, sl], v_ref[:, sl]
            bias = bias_ref[h]
            dk = jnp.zeros((GROUP, HEAD_DIM), F32)
            dv = jnp.zeros((GROUP, HEAD_DIM), F32)
            for j in range(nj):
                qv, dov = q_refs[j][:, sl], do_refs[j][:, sl]
                c0 = (nj - 1 - j) * GROUP
                s = lax.dot_general(qv, kv, NT, preferred_element_type=F32) * scale + bias[:, c0:c0 + GROUP]
                p = jnp.exp(s - _col_of(lse_refs[j][...], h))
                p = jnp.where(r + j <= ng - 1, p, 0.0)
                dp = lax.dot_general(dov, vv, NT, preferred_element_type=F32)
                ds = p * (dp - _col_of(dl_refs[j][...], h))
                dv = dv + lax.dot_general(p.astype(BF16), dov, TN, preferred_element_type=F32)
                dk = dk + lax.dot_general(ds.astype(BF16), qv, TN, preferred_element_type=F32)
            dk_ref[:, sl] = (dk * scale).astype(BF16)
            dv_ref[:, sl] = dv.astype(BF16)

    def qmap(j):
        return functools.partial(lambda r, hg, j: (jnp.minimum(r + j, ng - 1), hg), j=j)

    def smap(j):
        return functools.partial(lambda r, hg, j: (jnp.minimum(r + j, ng - 1), 0), j=j)

    blk = (GROUP, G * HEAD_DIM)
    in_specs = ([pl.BlockSpec(blk, lambda r, hg: (r, ngrp + hg)), pl.BlockSpec(blk, lambda r, hg: (r, 2 * ngrp + hg))]
                + [pl.BlockSpec(blk, qmap(j)) for j in range(nj)]
                + [pl.BlockSpec(blk, qmap(j)) for j in range(nj)]
                + [pl.BlockSpec((GROUP, LANE), smap(j)) for j in range(nj)]
                + [pl.BlockSpec((GROUP, LANE), smap(j)) for j in range(nj)]
                + [pl.BlockSpec((nh, GROUP, WIN), lambda r, hg: (0, 0, 0))])
    out = pl.BlockSpec(blk, lambda r, hg: (r, hg))
    return _pcall(
        body, name=name, grid=(ng, ngrp), in_specs=in_specs, out_specs=[out, out],
        out_shape=[_sds((T, nh * HEAD_DIM), BF16)] * 2,
        sem=("parallel", "parallel"), jobs=jobs,
    )(qkv, qkv, *([qkv] * nj), *([do] * nj), *([lse] * nj), *([delta] * nj), bias2)


def _fox_prep(name, f, b_f):
    T = f.shape[0]
    tb = _pick(T, (256, 128))

    def body(f_ref, b_ref, cum_ref, cumt_ref, carry_ref):
        @pl.when(pl.program_id(0) == 0)
        def _():
            carry_ref[...] = jnp.zeros_like(carry_ref)

        z = f_ref[...] + b_ref[...]
        logf = jnp.minimum(z, 0.0) - jnp.log(1.0 + jnp.exp(-jnp.abs(z)))
        row = lax.broadcasted_iota(jnp.int32, (tb, tb), 0)
        col = lax.broadcasted_iota(jnp.int32, (tb, tb), 1)
        tri = (row >= col).astype(BF16)
        acc = jnp.zeros((tb, LANE), F32)
        for piece in _split3(logf):
            acc = acc + jnp.dot(tri, piece, preferred_element_type=F32)
        cum = acc + carry_ref[...]
        cum_ref[...] = cum
        cumt_ref[...] = cum.T
        carry_ref[...] = cum_ref[pl.ds(tb - 1, 1), :]

    return pl.pallas_call(
        body, name=name, grid=(T // tb,),
        in_specs=[pl.BlockSpec((tb, LANE), lambda i: (i, 0)), pl.BlockSpec((1, LANE), lambda i: (0, 0))],
        out_specs=[pl.BlockSpec((tb, LANE), lambda i: (i, 0)), pl.BlockSpec((LANE, tb), lambda i: (0, i))],
        out_shape=[_sds((T, LANE), F32), _sds((LANE, T), F32)],
        scratch_shapes=[pltpu.VMEM((1, LANE), F32)],
        compiler_params=_cp(("arbitrary",)),
    )(f, b_f)


def _fox_blk(T):
    return _pick(T, (256, 128))


def _fox_allowed(i, j, tq, tk):
    diff = lax.broadcasted_iota(jnp.int32, (tq, tk), 1) - lax.broadcasted_iota(jnp.int32, (tq, tk), 0)
    return diff <= (i - j) * tq


def _fox_fwd(name, qkv, cum, cumt, nh, jobs=()):
    T = qkv.shape[0]
    tq = tk = _fox_blk(T)
    G = _hgroup(nh)
    ngrp = nh // G
    scale = HEAD_DIM ** -0.5

    def body(q_ref, k_ref, v_ref, cum_ref, cumt_ref, o_ref, lse_ref):
        i, hg = pl.program_id(0), pl.program_id(1)

        @pl.when(hg == 0)
        def _():
            lse_ref[...] = jnp.zeros_like(lse_ref)

        sls = [slice(g * HEAD_DIM, (g + 1) * HEAD_DIM) for g in range(G)]
        qs = [q_ref[:, sl] for sl in sls]
        cqs = [_col_of(cum_ref[...], hg * G + g) for g in range(G)]

        def step(j, carry):
            k0 = pl.multiple_of(j * tk, tk)
            ok = _fox_allowed(i, j, tq, tk)
            out = []
            for g in range(G):
                m, l, acc = carry[g]
                kj = k_ref[pl.ds(k0, tk), sls[g]]
                vj = v_ref[pl.ds(k0, tk), sls[g]]
                ck = cumt_ref[pl.ds(hg * G + g, 1), pl.ds(k0, tk)]
                s = lax.dot_general(qs[g], kj, NT, preferred_element_type=F32) * scale + (cqs[g] - ck)
                s = jnp.where(ok, s, NEG_INF)
                m_new = jnp.maximum(m, jnp.max(s, axis=1, keepdims=True))
                alpha = jnp.exp(m - m_new)
                p = jnp.exp(s - m_new)
                l = alpha * l + jnp.sum(p, axis=1, keepdims=True)
                acc = alpha * acc + jnp.dot(p.astype(BF16), vj, preferred_element_type=F32)
                out.append((m_new, l, acc))
            return tuple(out)

        one = (jnp.full((tq, 1), NEG_INF, F32), jnp.zeros((tq, 1), F32), jnp.zeros((tq, HEAD_DIM), F32))
        res = lax.fori_loop(0, i + 1, step, tuple(one for _ in range(G)))
        for g in range(G):
            m, l, acc = res[g]
            o_ref[:, sls[g]] = (acc / l).astype(BF16)
            _put_col(lse_ref, hg * G + g, m + jnp.log(l))

    GW = G * HEAD_DIM
    return _pcall(
        body, name=name, grid=(T // tq, ngrp),
        in_specs=[pl.BlockSpec((tq, GW), lambda i, hg: (i, 3 * ngrp + hg)),
                  pl.BlockSpec((T, GW), lambda i, hg: (0, 4 * ngrp + hg)),
                  pl.BlockSpec((T, GW), lambda i, hg: (0, 5 * ngrp + hg)),
                  pl.BlockSpec((tq, LANE), lambda i, hg: (i, 0)),
                  pl.BlockSpec((LANE, T), lambda i, hg: (0, 0))],
        out_specs=[pl.BlockSpec((tq, GW), lambda i, hg: (i, hg)), pl.BlockSpec((tq, LANE), lambda i, hg: (i, 0))],
        out_shape=[_sds((T, nh * HEAD_DIM), BF16), _sds((T, LANE), F32)],
        sem=("parallel", "arbitrary"), jobs=jobs,
    )(qkv, qkv, qkv, cum, cumt)


def _fox_dq(name, qkv, do, lse, cum, cumt, nh, jobs=()):
    T = qkv.shape[0]
    tq = tk = _fox_blk(T)
    G = _hgroup(nh)
    ngrp = nh // G
    GW = G * HEAD_DIM
    scale = HEAD_DIM ** -0.5

    def body(q_ref, k_ref, v_ref, do_ref, lse_ref, cum_ref, cumt_ref, dq_ref, delta_ref):
        i, hg = pl.program_id(0), pl.program_id(1)

        @pl.when(hg == 0)
        def _():
            delta_ref[...] = jnp.zeros_like(delta_ref)

        sls = [slice(g * HEAD_DIM, (g + 1) * HEAD_DIM) for g in range(G)]
        qs = [q_ref[:, sl] for sl in sls]
        dos = [do_ref[:, sl] for sl in sls]
        cqs = [_col_of(cum_ref[...], hg * G + g) for g in range(G)]
        lses = [_col_of(lse_ref[...], hg * G + g) for g in range(G)]

        def p_dp(j, g, ok):
            k0 = pl.multiple_of(j * tk, tk)
            kj = k_ref[pl.ds(k0, tk), sls[g]]
            vj = v_ref[pl.ds(k0, tk), sls[g]]
            ck = cumt_ref[pl.ds(hg * G + g, 1), pl.ds(k0, tk)]
            s = lax.dot_general(qs[g], kj, NT, preferred_element_type=F32) * scale + (cqs[g] - ck)
            p = jnp.exp(jnp.where(ok, s, NEG_INF) - lses[g])
            return p, lax.dot_general(dos[g], vj, NT, preferred_element_type=F32), kj

        def sweep_delta(j, deltas):
            ok = _fox_allowed(i, j, tq, tk)
            out = []
            for g in range(G):
                p, dp, _ = p_dp(j, g, ok)
                out.append(deltas[g] + jnp.sum(p * dp, axis=1, keepdims=True))
            return tuple(out)

        deltas = lax.fori_loop(0, i + 1, sweep_delta, tuple(jnp.zeros((tq, 1), F32) for _ in range(G)))

        def sweep_dq(j, dqs):
            ok = _fox_allowed(i, j, tq, tk)
            out = []
            for g in range(G):
                p, dp, kj = p_dp(j, g, ok)
                ds = p * (dp - deltas[g])
                out.append(dqs[g] + jnp.dot(ds.astype(BF16), kj, preferred_element_type=F32))
            return tuple(out)

        dqs = lax.fori_loop(0, i + 1, sweep_dq, tuple(jnp.zeros((tq, HEAD_DIM), F32) for _ in range(G)))
        for g in range(G):
            dq_ref[:, sls[g]] = (dqs[g] * scale).astype(BF16)
            _put_col(delta_ref, hg * G + g, deltas[g])

    blk = pl.BlockSpec((tq, GW), lambda i, hg: (i, hg))
    stat = pl.BlockSpec((tq, LANE), lambda i, hg: (i, 0))
    return _pcall(
        body, name=name, grid=(T // tq, ngrp),
        in_specs=[pl.BlockSpec((tq, GW), lambda i, hg: (i, 3 * ngrp + hg)),
                  pl.BlockSpec((T, GW), lambda i, hg: (0, 4 * ngrp + hg)),
                  pl.BlockSpec((T, GW), lambda i, hg: (0, 5 * ngrp + hg)),
                  blk, stat, stat, pl.BlockSpec((LANE, T), lambda i, hg: (0, 0))],
        out_specs=[blk, stat],
        out_shape=[_sds((T, nh * HEAD_DIM), BF16), _sds((T, LANE), F32)],
        sem=("parallel", "arbitrary"), jobs=jobs,
    )(qkv, qkv, qkv, do, lse, cum, cumt)


def _fox_dkv(name, qkv, do, lse, delta, cum, cumt, nh, jobs=()):
    T = qkv.shape[0]
    tq = tk = _fox_blk(T)
    nq = T // tq
    G = _hgroup(nh)
    ngrp = nh // G
    GW = G * HEAD_DIM
    scale = HEAD_DIM ** -0.5

    def body(k_ref, v_ref, q_ref, do_ref, lse_ref, dl_ref, cum_ref, cumt_ref, dk_ref, dv_ref, dc_ref):
        j, hg = pl.program_id(0), pl.program_id(1)

        @pl.when(hg == 0)
        def _():
            dc_ref[...] = jnp.zeros_like(dc_ref)

        sls = [slice(g * HEAD_DIM, (g + 1) * HEAD_DIM) for g in range(G)]
        kjs = [k_ref[:, sl] for sl in sls]
        vjs = [v_ref[:, sl] for sl in sls]
        k0 = pl.multiple_of(j * tk, tk)
        cks = [cumt_ref[pl.ds(hg * G + g, 1), pl.ds(k0, tk)] for g in range(G)]

        def step(i, carry):
            q0 = pl.multiple_of(i * tq, tq)
            ok = _fox_allowed(i, j, tq, tk)
            cum_i, lse_i, dl_i = cum_ref[pl.ds(q0, tq), :], lse_ref[pl.ds(q0, tq), :], dl_ref[pl.ds(q0, tq), :]
            out = []
            for g in range(G):
                dk, dv, dc = carry[g]
                h = hg * G + g
                qi = q_ref[pl.ds(q0, tq), sls[g]]
                doi = do_ref[pl.ds(q0, tq), sls[g]]
                s = lax.dot_general(qi, kjs[g], NT, preferred_element_type=F32) * scale + (_col_of(cum_i, h) - cks[g])
                p = jnp.exp(jnp.where(ok, s, NEG_INF) - _col_of(lse_i, h))
                dp = lax.dot_general(doi, vjs[g], NT, preferred_element_type=F32)
                ds = p * (dp - _col_of(dl_i, h))
                dv = dv + lax.dot_general(p.astype(BF16), doi, TN, preferred_element_type=F32)
                dk = dk + lax.dot_general(ds.astype(BF16), qi, TN, preferred_element_type=F32)
                dc = dc - jnp.sum(ds, axis=0, keepdims=True)
                out.append((dk, dv, dc))
            return tuple(out)

        one = (jnp.zeros((tk, HEAD_DIM), F32), jnp.zeros((tk, HEAD_DIM), F32), jnp.zeros((1, tk), F32))
        res = lax.fori_loop(j, nq, step, tuple(one for _ in range(G)))
        sub = lax.broadcasted_iota(jnp.int32, (LANE, tk), 0)
        dc_all = dc_ref[...]
        for g in range(G):
            dk, dv, dc = res[g]
            dk_ref[:, sls[g]] = (dk * scale).astype(BF16)
            dv_ref[:, sls[g]] = dv.astype(BF16)
            dc_all = jnp.where(sub == hg * G + g, dc, dc_all)
        dc_ref[...] = dc_all

    whole = lambda c: pl.BlockSpec((T, GW), c)
    stat = pl.BlockSpec((T, LANE), lambda j, hg: (0, 0))
    out = pl.BlockSpec((tk, GW), lambda j, hg: (j, hg))
    return _pcall(
        body, name=name, grid=(T // tk, ngrp),
        in_specs=[pl.BlockSpec((tk, GW), lambda j, hg: (j, 4 * ngrp + hg)),
                  pl.BlockSpec((tk, GW), lambda j, hg: (j, 5 * ngrp + hg)),
                  whole(lambda j, hg: (0, 3 * ngrp + hg)), whole(lambda j, hg: (0, hg)),
                  stat, stat, stat, pl.BlockSpec((LANE, T), lambda j, hg: (0, 0))],
        out_specs=[out, out, pl.BlockSpec((LANE, tk), lambda j, hg: (0, j))],
        out_shape=[_sds((T, nh * HEAD_DIM), BF16)] * 2 + [_sds((LANE, T), F32)],
        sem=("parallel", "arbitrary"), jobs=jobs,
    )(qkv, qkv, qkv, do, lse, delta, cum, cumt)


def _fox_post(name, dcumt, f, b_f):
    T = f.shape[0]
    tb = _pick(T, (256, 128))
    nb = T // tb

    def body(dc_ref, f_ref, b_ref, df_ref, gb_ref, carry_ref):
        @pl.when(pl.program_id(0) == 0)
        def _():
            carry_ref[...] = jnp.zeros_like(carry_ref)
            gb_ref[...] = jnp.zeros_like(gb_ref)

        dc = dc_ref[...]
        row = lax.broadcasted_iota(jnp.int32, (tb, tb), 0)
        col = lax.broadcasted_iota(jnp.int32, (tb, tb), 1)
        tri = (row >= col).astype(BF16)
        acc = jnp.zeros((LANE, tb), F32)
        for piece in _split3(dc):
            acc = acc + jnp.dot(piece, tri, preferred_element_type=F32)
        dlogf = (acc + carry_ref[...]).T
        carry_ref[...] += jnp.sum(dc, axis=1, keepdims=True)
        z = f_ref[...] + b_ref[...]
        df = dlogf * _sigmoid(-z)
        df_ref[...] = df.astype(BF16)
        gb_ref[...] += jnp.sum(df, axis=0, keepdims=True)

    return pl.pallas_call(
        body, name=name, grid=(nb,),
        in_specs=[pl.BlockSpec((LANE, tb), lambda g: (0, nb - 1 - g)),
                  pl.BlockSpec((tb, LANE), lambda g: (nb - 1 - g, 0)),
                  pl.BlockSpec((1, LANE), lambda g: (0, 0))],
        out_specs=[pl.BlockSpec((tb, LANE), lambda g: (nb - 1 - g, 0)), pl.BlockSpec((1, LANE), lambda g: (0, 0))],
        out_shape=[_sds((T, LANE), BF16), _sds((1, LANE), F32)],
        scratch_shapes=[pltpu.VMEM((LANE, 1), F32)],
        compiler_params=_cp(("arbitrary",)),
    )(dcumt, f, b_f)


def _rel_tables(n_rel):
    max_rel = (n_rel - 1) // 2
    nj = GROUP + WIN - 1
    onehot = np.zeros((n_rel, nj), np.float32)
    for j in range(nj):
        dist = (WIN - 1) - j
        onehot[int(np.clip(dist, -max_rel, max_rel)) + max_rel, j] = 1.0
    a = np.arange(GROUP)[:, None]
    kb = np.arange(WIN)[None, :]
    lo = CHUNK * (a // CHUNK)
    inband = (kb >= lo) & (kb < lo + BAND)
    return onehot, inband


def _bias2_of(rel_bias, onehot, inband):
    bv = jnp.dot(rel_bias, jnp.asarray(onehot), precision=lax.Precision.HIGHEST)
    rows = [bv[:, GROUP - 1 - a:GROUP - 1 - a + WIN] for a in range(GROUP)]
    toe = jnp.stack(rows, axis=1)
    return jnp.where(jnp.asarray(inband)[None], toe, NEG_INF)


def _rel_grad_of(dbias2, onehot):
    nj = GROUP + WIN - 1
    dbv = sum(jnp.pad(dbias2[:, a, :], ((0, 0), (GROUP - 1 - a, nj - WIN - (GROUP - 1 - a)))) for a in range(GROUP))
    return jnp.dot(dbv, jnp.asarray(onehot).T, precision=lax.Precision.HIGHEST)


def kernel(x, g_mix, w_in, b_f, b_gate, rel_bias, w_branch_a, w_branch_b, w_out, g_ffn, w_gate_ffn, w_up_ffn, w_down_ffn, g_final, loss_target, m_g_mix, m_w_in, m_b_f, m_b_gate, m_rel_bias, m_w_branch_a, m_w_branch_b, m_w_out, m_g_ffn, m_w_gate_ffn, m_w_up_ffn, m_w_down_ffn, m_g_final, v_g_mix, v_w_in, v_b_f, v_b_gate, v_rel_bias, v_w_branch_a, v_w_branch_b, v_w_out, v_g_ffn, v_w_gate_ffn, v_w_up_ffn, v_w_down_ffn, v_g_final):
    T, D = x.shape[1], x.shape[2]
    Ls = w_in.shape[2]
    W = w_branch_a.shape[1]
    nh = W // HEAD_DIM
    nhb = b_f.shape[1]
    assert w_branch_b.shape[1] == W and nhb == nh and rel_bias.shape[1] == nh
    W6 = 6 * W
    Fl = w_gate_ffn.shape[2]
    Fp = -(-Fl // LANE) * LANE
    n_rel = rel_bias.shape[2]
    chip = 2 * lax.axis_index("x") + lax.axis_index("y")
    lay, nbw, nmain = _in_layout(D, W6, nhb, Ls)
    onehot, inband = _rel_tables(n_rel)

    xs, tgt = x[0], loss_target[0]

    win_f32 = lax.switch(chip, [functools.partial(_to_window, lay_k=lay[k], nbw=nbw) for k in range(N_CHIPS)], w_in[0])
    pad_c = lambda w: jnp.pad(w, ((0, 0), (0, Fp - Fl)))
    pad_r = lambda w: jnp.pad(w, ((0, Fp - Fl), (0, 0)))
    sh_in = _cast_bf16("cast_w_in", win_f32, chip)
    sh_a = _cast_bf16("cast_w_a", w_branch_a[0], chip)
    sh_b = _cast_bf16("cast_w_b", w_branch_b[0], chip)
    sh_o = _cast_bf16("cast_w_out", w_out[0], chip)
    sh_g = _cast_bf16("cast_w_gate", pad_c(w_gate_ffn[0]), chip)
    sh_u = _cast_bf16("cast_w_up", pad_c(w_up_ffn[0]), chip)
    sh_d = _cast_bf16("cast_w_down", pad_r(w_down_ffn[0]), chip)
    (wins,) = _allgather("ag_w_in", [sh_in])
    wc = _assemble_in("assemble_w_in", wins, lay, nbw, nmain)

    h1, r1 = _rms_fwd("rms1", xs, g_mix)
    qkv, ((wa_g, wb_g, wo_g),) = _mm_nn("proj_qkv", h1, wc, BF16, b_col0=0, n=W6, tm=1024,
                                        jobs=[_job_gather_ici([sh_a, sh_b, sh_o])])
    gates, ((wa_g, wb_g, wo_g), (wg_g,)) = _mm_nn("proj_gates", h1, wc, F32, b_col0=W6, n=2 * D, tm=1024,
                                                  jobs=[_job_gather_d2d([wa_g, wb_g, wo_g]), _job_gather_ici([sh_g])])
    fl = _mm_nn("proj_f", h1, wc, F32, b_col0=nmain, n=LANE, tn=LANE)
    bias2 = _bias2_of(rel_bias[0], onehot, inband)
    bf_pad = jnp.pad(b_f, ((0, 0), (0, LANE - nhb)))
    (o_a, lse_a), ((wu_g,),) = _attn_a_fwd("attn_a_fwd", qkv, bias2, nh, jobs=[_job_gather_ici([sh_u])])
    cum, cumt = _fox_prep("fox_prep", fl, bf_pad)
    (o_b, lse_b), ((wd_g,), (wg_g, wu_g)) = _fox_fwd("fox_fwd", qkv, cum, cumt, nh,
                                                     jobs=[_job_gather_ici([sh_d]), _job_gather_d2d([wg_g, wu_g])])
    u_a = _mm_nn("branch_a", o_a, wa_g, F32)
    u_b = _mm_nn("branch_b", o_b, wb_g, F32)
    merged = _merge_fwd("merge", gates, u_a, u_b, b_gate)
    wo_full = wo_g.reshape(D, D)
    x1, ((wd_g,),) = _mm_nn("out_proj", merged, wo_full, F32, residual=xs, jobs=[_job_gather_d2d([wd_g])])
    wd_full = wd_g.reshape(N_CHIPS * Fp, D)
    h2, r2 = _rms_fwd("rms2", x1, g_ffn)

    tm_f = _pick(T, (512, 256, 128))
    tn_f = _pick(Fp, (1408, 1024, 512, 256, 128))
    tk_f = _pick(D, (1024, 512, 256, 128))
    nps_f = Fp // tn_f

    def swiglu_ep(accs, e_refs, o_refs):
        g, u = accs
        o_refs[0][...] = g.astype(BF16)
        o_refs[1][...] = u.astype(BF16)
        o_refs[2][...] = (g * _sigmoid(g) * u).astype(BF16)

    hid_spec = pl.BlockSpec((tm_f, tn_f), lambda i, j, k: (i, j))
    wcol_spec = pl.BlockSpec((None, tk_f, tn_f), lambda i, j, k: (j // nps_f, k, j % nps_f))
    (gate, up, hidden), _ = _mm(
        "ffn_up", "nn", [h2], [pl.BlockSpec((tm_f, tk_f), lambda i, j, k: (i, k))], [wg_g, wu_g], [wcol_spec, wcol_spec],
        [(0, 0, 0), (0, 1, 1)], 2, (T // tm_f, N_CHIPS * Fp // tn_f, D // tk_f), tm_f, tn_f,
        [_sds((T, N_CHIPS * Fp), BF16)] * 3, [hid_spec] * 3, swiglu_ep)
    x2 = _mm_nn("ffn_down", hidden, wd_full, F32, residual=x1, tk=_pick(N_CHIPS * Fp, (1408, 1024, 512, 256, 128)))

    dx2, dx2b, loss_part, gg_final = _final_loss_bwd("final_loss", x2, tgt, g_final.reshape(1, D))

    def swiglu_bwd_ep(accs, e_refs, o_refs):
        dh = accs[0]
        g = e_refs[0][...].astype(F32)
        u = e_refs[1][...].astype(F32)
        sg = _sigmoid(g)
        o_refs[0][...] = (dh * u * (sg * (1.0 + g * (1.0 - sg)))).astype(BF16)
        o_refs[1][...] = (dh * (g * sg)).astype(BF16)

    tk_b = _pick(D, (1024, 512, 256, 128))
    core = lax.axis_index("c")
    (dgate, dup), _ = _mm(
        "ffn_down_bwd", "nt", [dx2b], [pl.BlockSpec((tm_f, tk_b), lambda i, j, k: (i, k))],
        [wd_full], [pl.BlockSpec((tn_f, tk_b), lambda i, j, k: (j, k))], [(0, 0, 0)], 1,
        (T // tm_f, N_CHIPS * Fp // tn_f, D // tk_b), tm_f, tn_f,
        [_sds((T, N_CHIPS * Fp), BF16)] * 2, [hid_spec] * 2, swiglu_bwd_ep,
        extra=[gate, up], extra_specs=[hid_spec, hid_spec])
    dwd = _mm_tn("dw_down", hidden, dx2b, BF16, tm=_pick(N_CHIPS * Fp, (1408, 1024, 512, 256, 128)))
    dwd = dwd.reshape(N_CHIPS, Fp, D)
    dh2, ((sib_d,),) = _mm_nt("ffn_up_bwd", [dgate, dup], [wg_g, wu_g], F32, tm=1024, jobs=[_job_sibling([dwd])])
    dwg = _mm_tn("dw_gate", h2, dgate, BF16, slots=N_CHIPS)
    dwu = _mm_tn("dw_up", h2, dup, BF16, slots=N_CHIPS)
    dx1, dx1b, gg_ffn = _rms_bwd("rms2_bwd", [dh2], x1, r2, g_ffn, dx2, True)
    part_d = _add_bf16("rs_add_down", dwd, core, sib_d)

    dmerged, ((sib_g, sib_u),) = _mm_nt("out_proj_bwd", [dx1b], [wo_full], F32, jobs=[_job_sibling([dwg, dwu])])
    dwo = _mm_tn("dw_out", merged, dx1b, BF16).reshape(N_CHIPS, D // N_CHIPS, D)
    du_a, du_b, dga, dgb, gbg_a, gbg_b = _merge_bwd("merge_bwd", dmerged, gates, u_a, u_b, b_gate)
    part_g = _add_bf16("rs_add_gate", dwg, core, sib_g)
    part_u = _add_bf16("rs_add_up", dwu, core, sib_u)
    do_a = _mm_nt("branch_a_bwd", [du_a], [wa_g], BF16)
    do_b = _mm_nt("branch_b_bwd", [du_b], [wb_g], BF16)
    dwa = _mm_tn("dw_a", o_a, du_a, BF16, slots=N_CHIPS)
    dwb = _mm_tn("dw_b", o_b, du_b, BF16, slots=N_CHIPS)

    (dq_a, delta_a, dbias2), ((got_d,), (sib_a, sib_b, sib_o)) = _attn_a_dq(
        "attn_a_dq", qkv, do_a, lse_a, bias2, nh, jobs=[_job_scatter([part_d]), _job_sibling([dwa, dwb, dwo])])
    part_a = _add_bf16("rs_add_a", dwa, core, sib_a)
    part_b = _add_bf16("rs_add_b", dwb, core, sib_b)
    part_o = _add_bf16("rs_add_out", dwo, core, sib_o)
    full_d = _sum4("rs_sum_down", got_d, part_d, chip, core)
    (dk_a, dv_a), ((got_g,),) = _attn_a_dkv("attn_a_dkv", qkv, do_a, lse_a, delta_a, bias2, nh,
                                            jobs=[_job_scatter([part_g])])
    full_g = _sum4("rs_sum_gate", got_g, part_g, chip, core)
    (dq_b, delta_b), ((got_u,), (got_a, got_b, got_o)) = _fox_dq(
        "fox_dq", qkv, do_b, lse_b, cum, cumt, nh, jobs=[_job_scatter([part_u]), _job_scatter([part_a, part_b, part_o])])
    full_u = _sum4("rs_sum_up", got_u, part_u, chip, core)
    full_a = _sum4("rs_sum_a", got_a, part_a, chip, core)
    full_b = _sum4("rs_sum_b", got_b, part_b, chip, core)
    full_o = _sum4("rs_sum_out", got_o, part_o, chip, core)
    (dk_b, dv_b, dcumt), ((g_d, g_g, g_u, g_a, g_b, g_o),) = _fox_dkv(
        "fox_dkv", qkv, do_b, lse_b, delta_b, cum, cumt, nh,
        jobs=[_job_swap([full_d, full_g, full_u, full_a, full_b, full_o])])
    df, gbf = _fox_post("fox_post", dcumt, fl, bf_pad)

    dqkv = jnp.concatenate([dq_a, dk_a, dv_a, dq_b, dk_b, dv_b], axis=1)
    dgates = jnp.concatenate([dga, dgb], axis=1)
    dwc_q = _mm_tn("dw_in_qkv", h1, dqkv, BF16)
    dwc_g = _mm_tn("dw_in_gates", h1, dgates, BF16)
    dwc_f = _mm_tn("dw_in_f", h1, df, BF16, tn=LANE)
    dwc = jnp.concatenate([dwc_q, dwc_g, dwc_f], axis=1)
    zeros_blk = jnp.zeros((D, LANE), BF16)
    win_parts = []
    for k in range(N_CHIPS):
        cols = [dwc[:, lay[k]["s"] * LANE:lay[k]["e"] * LANE]]
        nb = lay[k]["e"] - lay[k]["s"]
        if lay[k]["f"]:
            cols.append(dwc[:, nmain:nmain + LANE])
            nb += 1
        cols += [zeros_blk] * (nbw - nb)
        win_parts.append(jnp.concatenate(cols, axis=1) if len(cols) > 1 else cols[0])
    dwin = jnp.stack(win_parts, axis=0)
    ((sib_in,),) = _comm_only("rs_sibling_in", [_job_sibling([dwin])])
    part_in = _add_bf16("rs_add_in", dwin, core, sib_in)
    dh_q, ((got_in,),) = _mm_nt("proj_qkv_bwd", [dqkv], [wc], F32, k0_list=[0], tk=_pick(W6, (1024, 512, 256, 128)), tm=1024,
                                jobs=[_job_scatter([part_in])])
    full_in = _sum4("rs_sum_in", got_in, part_in, chip, core)
    dh_g, ((g_win,),) = _mm_nt("proj_gates_bwd", [dgates], [wc], F32, k0_list=[W6], tm=1024,
                               tk=_pick(math_gcd(W6, 2 * D), (1024, 512, 256, 128)), jobs=[_job_swap([full_in])])
    dh_f = _mm_nt("proj_f_bwd", [df], [wc], F32, k0_list=[nmain], tk=LANE)
    grad_x, gg_mix = _rms_bwd("rms1_bwd", [dh_q, dh_g, dh_f], xs, r1, g_mix, dx1, False)
    g_in = lax.switch(chip, [functools.partial(_from_window, lay_k=lay[k]) for k in range(N_CHIPS)], g_win)
    g_g, g_u, g_d = g_g[:, :Fl], g_u[:, :Fl], g_d[:Fl, :]

    big = {}
    for nm, w, g, m, v in (("w_in", w_in, g_in, m_w_in, v_w_in), ("w_branch_a", w_branch_a, g_a, m_w_branch_a, v_w_branch_a),
                           ("w_branch_b", w_branch_b, g_b, m_w_branch_b, v_w_branch_b), ("w_out", w_out, g_o, m_w_out, v_w_out),
                           ("w_gate_ffn", w_gate_ffn, g_g, m_w_gate_ffn, v_w_gate_ffn),
                           ("w_up_ffn", w_up_ffn, g_u, m_w_up_ffn, v_w_up_ffn),
                           ("w_down_ffn", w_down_ffn, g_d, m_w_down_ffn, v_w_down_ffn)):
        d, mn, vn = _adamw(f"adamw_{nm}", w[0], g, m[0], v[0])
        big[nm] = (g[None], d[None], mn[None], vn[None])

    g_rel = _rel_grad_of(dbias2, onehot)
    small = [("loss", loss_part[:, :1], None, None, None),
             ("g_mix", gg_mix, g_mix, m_g_mix, v_g_mix), ("b_f", gbf[:, :nhb], b_f, m_b_f, v_b_f),
             ("b_gate", jnp.concatenate([gbg_a, gbg_b], axis=1), b_gate, m_b_gate, v_b_gate),
             ("rel_bias", g_rel, rel_bias, m_rel_bias, v_rel_bias), ("g_ffn", gg_ffn, g_ffn, m_g_ffn, v_g_ffn),
             ("g_final", gg_final, g_final, m_g_final, v_g_final)]
    sizes = [int(np.prod(s[1].shape)) for s in small]
    total = sum(sizes)
    npad = -(-total // 1024) * 1024

    def pack(arrs):
        flat = jnp.concatenate([a.reshape(-1).astype(F32) for a in arrs])
        return jnp.pad(flat, (0, npad - total)).reshape(8, npad // 8)

    zero1 = jnp.zeros((1,), F32)
    g_all = _small_allreduce("small_allreduce", pack([s[1] for s in small]))
    w_s = pack([zero1 if s[2] is None else s[2] for s in small])
    m_s = pack([zero1 if s[3] is None else s[3] for s in small])
    v_s = pack([zero1 + 1.0 if s[4] is None else s[4] for s in small])
    d_s, mn_s, vn_s = _adamw("adamw_small", w_s, g_all, m_s, v_s)

    def unpack(packed):
        flat = packed.reshape(-1)
        out, pos = {}, 0
        for s, n in zip(small, sizes):
            if s[2] is not None:
                out[s[0]] = flat[pos:pos + n].reshape(s[2].shape)
            else:
                out[s[0]] = flat[pos:pos + n].reshape(())
            pos += n
        return out

    gs, ds, ms, vs = unpack(g_all), unpack(d_s), unpack(mn_s), unpack(vn_s)
    order = ["g_mix", "w_in", "b_f", "b_gate", "rel_bias", "w_branch_a", "w_branch_b", "w_out", "g_ffn",
             "w_gate_ffn", "w_up_ffn", "w_down_ffn", "g_final"]
    res = [[], [], [], []]
    for nm in order:
        four = big[nm] if nm in big else (gs[nm], ds[nm], ms[nm], vs[nm])
        for q in range(4):
            res[q].append(four[q])
    return (gs["loss"], grad_x[None], *res[0], *res[1], *res[2], *res[3])


def math_gcd(a, b):
    while b:
        a, b = b, a % b
    return a
```

```python
import functools

import numpy as np
import jax
import jax.numpy as jnp
from jax import lax
from jax.experimental import pallas as pl
from jax.experimental.pallas import tpu as pltpu

F32 = jnp.float32
BF16 = jnp.bfloat16
LANE = 128
HEAD_DIM = 128
CHUNK = 64
LEFT_CHUNKS = 8
GROUP = 128
WIN_BLOCKS = 5
WIN = WIN_BLOCKS * GROUP
BAND = (LEFT_CHUNKS + 1) * CHUNK
RMS_EPS = 1e-6
NEG_INF = -1e30
ADAM_LR = 0.001
ADAM_B1 = 0.9
ADAM_B2 = 0.999
ADAM_EPS = 1e-08
ADAM_WD = 0.01
ADAM_STEP = 10
N_CHIPS = 4
MESH = pl.DeviceIdType.MESH
VMEM_LIMIT = 52 * 1024 * 1024
ANY = pl.BlockSpec(memory_space=pl.ANY)

NN = (((1,), (0,)), ((), ()))
NT = (((1,), (1,)), ((), ()))
TN = (((0,), (0,)), ((), ()))


def _cp(sem):
    return pltpu.CompilerParams(dimension_semantics=sem, vmem_limit_bytes=VMEM_LIMIT)


def _sds(shape, dtype):
    return jax.ShapeDtypeStruct(shape, dtype)


def _pick(n, prefs):
    for p in prefs:
        if n % p == 0:
            return p
    return n


def _sigmoid(v):
    return 1.0 / (1.0 + jnp.exp(-v))


def _split3(v):
    hi = v.astype(BF16)
    r1 = v - hi.astype(F32)
    mid = r1.astype(BF16)
    lo = (r1 - mid.astype(F32)).astype(BF16)
    return hi, mid, lo


def _col_of(blk, h):
    lane = lax.broadcasted_iota(jnp.int32, blk.shape, 1)
    return jnp.sum(jnp.where(lane == h, blk, 0.0), axis=1, keepdims=True)


def _put_col(ref, h, col):
    lane = lax.broadcasted_iota(jnp.int32, ref.shape, 1)
    ref[...] = jnp.where(lane == h, col, ref[...])


def _mm(name, mode, a_list, a_specs, b_list, b_specs, pairs, n_acc, grid, tm, tn,
        out_shapes, out_specs, epilogue, extra=(), extra_specs=(), jobs=()):
    n_a, n_b, n_e, n_o = len(a_list), len(b_list), len(extra), len(out_shapes)
    nk = grid[2]
    dn = {"nn": NN, "nt": NT, "tn": TN}[mode]

    def body(*refs):
        a_refs = refs[:n_a]
        b_refs = refs[n_a:n_a + n_b]
        e_refs = refs[n_a + n_b:n_a + n_b + n_e]
        o_refs = refs[n_a + n_b + n_e:n_a + n_b + n_e + n_o]
        acc_refs = refs[n_a + n_b + n_e + n_o:]
        k = pl.program_id(2)

        @pl.when(k == 0)
        def _():
            for acc in acc_refs:
                acc[...] = jnp.zeros_like(acc)

        for ai, bi, ci in pairs:
            acc_refs[ci][...] += lax.dot_general(a_refs[ai][...], b_refs[bi][...], dn,
                                                 preferred_element_type=F32)

        @pl.when(k == nk - 1)
        def _():
            epilogue([acc[...] for acc in acc_refs], e_refs, o_refs)

    return _pcall(
        body, name=name, grid=grid,
        in_specs=list(a_specs) + list(b_specs) + list(extra_specs),
        out_specs=list(out_specs), out_shape=list(out_shapes),
        scratch_shapes=[pltpu.VMEM((tm, tn), F32) for _ in range(n_acc)],
        sem=("parallel", "parallel", "arbitrary"), jobs=jobs,
    )(*a_list, *b_list, *extra)


def _one(res, jobs):
    outs, jouts = res
    return (outs[0], jouts) if jobs else outs[0]


def _store(dtype):
    def ep(accs, e_refs, o_refs):
        o_refs[0][...] = accs[0].astype(dtype)
    return ep


def _mm_nn(name, a, b, out_dtype, *, b_col0=0, n=None, tm=512, tn=None, tk=None, residual=None, jobs=()):
    M, K = a.shape
    if b.ndim == 3:
        Ns = b.shape[2]
        n = b.shape[0] * Ns
        tn = tn or _pick(Ns, (1408, 1024, 512, 256, 128))
        nps = Ns // tn
        b_spec = pl.BlockSpec((None, tk or _pick(K, (1024, 512, 256, 128)), tn),
                              lambda i, j, k: (j // nps, k, j % nps))
    else:
        n = n or b.shape[1]
        tn = tn or _pick(math_gcd(n, b_col0) if b_col0 else n, (2048, 1024, 512, 256, 128))
        assert b_col0 % tn == 0 and n % tn == 0
        c0 = b_col0 // tn
        b_spec = pl.BlockSpec((tk or _pick(K, (1024, 512, 256, 128)), tn), lambda i, j, k: (k, c0 + j))
    tk = tk or _pick(K, (1024, 512, 256, 128))
    tm = _pick(M, (tm, 256, 128))
    grid = (M // tm, n // tn, K // tk)
    a_spec = pl.BlockSpec((tm, tk), lambda i, j, k: (i, k))
    o_spec = pl.BlockSpec((tm, tn), lambda i, j, k: (i, j))
    if residual is None:
        return _one(_mm(name, "nn", [a], [a_spec], [b], [b_spec], [(0, 0, 0)], 1, grid, tm, tn,
                        [_sds((M, n), out_dtype)], [o_spec], _store(out_dtype), jobs=jobs), jobs)

    def ep(accs, e_refs, o_refs):
        o_refs[0][...] = (e_refs[0][...] + accs[0]).astype(out_dtype)
    return _one(_mm(name, "nn", [a], [a_spec], [b], [b_spec], [(0, 0, 0)], 1, grid, tm, tn,
                    [_sds((M, n), out_dtype)], [o_spec], ep, extra=[residual], extra_specs=[o_spec], jobs=jobs), jobs)


def _mm_nt(name, a_list, b_list, out_dtype, *, k0_list=None, tm=512, tn=None, tk=None, jobs=()):
    M, K = a_list[0].shape
    b0 = b_list[0]
    N = b0.shape[1] if b0.ndim == 3 else b0.shape[0]
    tm = _pick(M, (tm, 256, 128))
    tn = tn or _pick(N, (1024, 512, 256, 128))
    if b0.ndim == 3:
        Ks = b0.shape[2]
        tk = tk or _pick(Ks, (1408, 1024, 512, 256, 128))
        kps = Ks // tk
        b_specs = [pl.BlockSpec((None, tn, tk), lambda i, j, k: (k // kps, j, k % kps)) for _ in b_list]
    else:
        tk = tk or _pick(K, (1024, 896, 512, 256, 128))
        k0_list = k0_list or [0] * len(b_list)
        b_specs = []
        for k0 in k0_list:
            assert k0 % tk == 0
            b_specs.append(pl.BlockSpec((tn, tk), functools.partial(lambda i, j, k, c: (j, c + k), c=k0 // tk)))
    grid = (M // tm, N // tn, K // tk)
    a_specs = [pl.BlockSpec((tm, tk), lambda i, j, k: (i, k)) for _ in a_list]
    o_spec = pl.BlockSpec((tm, tn), lambda i, j, k: (i, j))
    pairs = [(p, p, 0) for p in range(len(a_list))]
    return _one(_mm(name, "nt", a_list, a_specs, b_list, b_specs, pairs, 1, grid, tm, tn,
                    [_sds((M, N), out_dtype)], [o_spec], _store(out_dtype), jobs=jobs), jobs)


def _mm_tn(name, a, b, out_dtype, *, slots=None, tm=None, tn=None, tk=1024, jobs=()):
    Kc, Mo = a.shape
    No = b.shape[1]
    tm = tm or _pick(Mo, (1024, 704, 512, 256, 128))
    tk = _pick(Kc, (tk, 256, 128))
    if slots:
        Ns = No // slots
        tn = tn or _pick(Ns, (1408, 1024, 512, 256, 128))
        nps = Ns // tn
        o_spec = pl.BlockSpec((None, tm, tn), lambda i, j, k: (j // nps, i, j % nps))
        o_shape = _sds((slots, Mo, Ns), out_dtype)
    else:
        tn = tn or _pick(No, (1024, 512, 256, 128))
        o_spec = pl.BlockSpec((tm, tn), lambda i, j, k: (i, j))
        o_shape = _sds((Mo, No), out_dtype)
    grid = (Mo // tm, No // tn, Kc // tk)
    a_spec = pl.BlockSpec((tk, tm), lambda i, j, k: (k, i))
    b_spec = pl.BlockSpec((tk, tn), lambda i, j, k: (k, j))
    return _one(_mm(name, "tn", [a], [a_spec], [b], [b_spec], [(0, 0, 0)], 1, grid, tm, tn,
                    [o_shape], [o_spec], _store(out_dtype), jobs=jobs), jobs)


def _cast_bf16(name, w, chip):
    R, C = w.shape
    tr = _pick(R, (256, 128, 64, 32, 16))

    def body(k_ref, w_ref, o_ref):
        o_ref[...] = w_ref[...].astype(BF16)

    gs = pltpu.PrefetchScalarGridSpec(
        num_scalar_prefetch=1, grid=(R // tr,),
        in_specs=[pl.BlockSpec((tr, C), lambda i, k: (i, 0))],
        out_specs=pl.BlockSpec((None, tr, C), lambda i, k: (k[0], i, 0)))
    return pl.pallas_call(body, name=name, grid_spec=gs, out_shape=_sds((N_CHIPS, R, C), BF16),
                          compiler_params=_cp(("parallel",)))(jnp.reshape(chip, (1,)).astype(jnp.int32), w)


def _rms_fwd(name, x, g):
    T, D = x.shape
    tr = _pick(T, (256, 128))

    def body(x_ref, g_ref, h_ref, r_ref):
        xv = x_ref[...]
        r = lax.rsqrt(jnp.mean(xv * xv, axis=1, keepdims=True) + RMS_EPS)
        h_ref[...] = (xv * r * g_ref[...]).astype(BF16)
        r_ref[...] = r

    row = pl.BlockSpec((tr, D), lambda i: (i, 0))
    return pl.pallas_call(
        body, name=name, grid=(T // tr,),
        in_specs=[row, pl.BlockSpec((1, D), lambda i: (0, 0))],
        out_specs=[row, pl.BlockSpec((tr, 1), lambda i: (i, 0))],
        out_shape=[_sds((T, D), BF16), _sds((T, 1), F32)], compiler_params=_cp(("parallel",)),
    )(x, g)


def _final_loss_bwd(name, x2, tgt, g):
    T, D = x2.shape
    tr = _pick(T, (256, 128))

    def body(x_ref, t_ref, g_ref, dx_ref, dxb_ref, loss_ref, gg_ref):
        @pl.when(pl.program_id(0) == 0)
        def _():
            loss_ref[...] = jnp.zeros_like(loss_ref)
            gg_ref[...] = jnp.zeros_like(gg_ref)

        xv = x_ref[...]
        gv = g_ref[...]
        r = lax.rsqrt(jnp.mean(xv * xv, axis=1, keepdims=True) + RMS_EPS)
        n = xv * r
        e = n * gv - t_ref[...]
        loss_ref[...] += 0.5 * jnp.sum(jnp.mean(e * e, axis=1, keepdims=True), axis=0, keepdims=True)
        dy = e * (1.0 / D)
        gg_ref[...] += jnp.sum(dy * n, axis=0, keepdims=True)
        gy = dy * gv
        dx = r * (gy - n * jnp.mean(gy * n, axis=1, keepdims=True))
        dx_ref[...] = dx
        dxb_ref[...] = dx.astype(BF16)

    row = pl.BlockSpec((tr, D), lambda i: (i, 0))
    vec = pl.BlockSpec((1, D), lambda i: (0, 0))
    return pl.pallas_call(
        body, name=name, grid=(T // tr,),
        in_specs=[row, row, vec],
        out_specs=[row, row, pl.BlockSpec((1, LANE), lambda i: (0, 0)), vec],
        out_shape=[_sds((T, D), F32), _sds((T, D), BF16), _sds((1, LANE), F32), _sds((1, D), F32)],
        compiler_params=_cp(("arbitrary",)),
    )(x2, tgt, g)


def _rms_bwd(name, dh_list, x, r, g, dres, want_bf16):
    T, D = x.shape
    tr = _pick(T, (128,))
    n_dh = len(dh_list)

    def body(*refs):
        dh_refs = refs[:n_dh]
        x_ref, r_ref, g_ref, dres_ref = refs[n_dh:n_dh + 4]
        outs = refs[n_dh + 4:]
        gg_ref = outs[-1]

        @pl.when(pl.program_id(0) == 0)
        def _():
            gg_ref[...] = jnp.zeros_like(gg_ref)

        dh = dh_refs[0][...]
        for ref in dh_refs[1:]:
            dh = dh + ref[...]
        rv = r_ref[...]
        n = x_ref[...] * rv
        gg_ref[...] += jnp.sum(dh * n, axis=0, keepdims=True)
        gy = dh * g_ref[...]
        dx = dres_ref[...] + rv * (gy - n * jnp.mean(gy * n, axis=1, keepdims=True))
        outs[0][...] = dx
        if want_bf16:
            outs[1][...] = dx.astype(BF16)

    row = pl.BlockSpec((tr, D), lambda i: (i, 0))
    vec = pl.BlockSpec((1, D), lambda i: (0, 0))
    out_specs = [row] + ([row] if want_bf16 else []) + [vec]
    out_shape = [_sds((T, D), F32)] + ([_sds((T, D), BF16)] if want_bf16 else []) + [_sds((1, D), F32)]
    return pl.pallas_call(
        body, name=name, grid=(T // tr,),
        in_specs=[row] * n_dh + [row, pl.BlockSpec((tr, 1), lambda i: (i, 0)), vec, row],
        out_specs=out_specs, out_shape=out_shape, compiler_params=_cp(("arbitrary",)),
    )(*dh_list, x, r, g, dres)


def _merge_fwd(name, gates, u_a, u_b, b_gate):
    T, D = u_a.shape
    tr = _pick(T, (256, 128))

    def body(ga_ref, gb_ref, ua_ref, ub_ref, ba_ref, bb_ref, o_ref):
        sa = _sigmoid(ga_ref[...] + ba_ref[...])
        sb = _sigmoid(gb_ref[...] + bb_ref[...])
        o_ref[...] = (sa * ua_ref[...] + sb * ub_ref[...]).astype(BF16)

    row = pl.BlockSpec((tr, D), lambda i: (i, 0))
    row1 = pl.BlockSpec((tr, D), lambda i: (i, 1))
    v0 = pl.BlockSpec((1, D), lambda i: (0, 0))
    v1 = pl.BlockSpec((1, D), lambda i: (0, 1))
    return pl.pallas_call(
        body, name=name, grid=(T // tr,),
        in_specs=[row, row1, row, row, v0, v1], out_specs=row,
        out_shape=_sds((T, D), BF16), compiler_params=_cp(("parallel",)),
    )(gates, gates, u_a, u_b, b_gate, b_gate)


def _merge_bwd(name, dm, gates, u_a, u_b, b_gate):
    T, D = u_a.shape
    tr = _pick(T, (128,))

    def body(dm_ref, ga_ref, gb_ref, ua_ref, ub_ref, ba_ref, bb_ref, dua_ref, dub_ref, dga_ref, dgb_ref,
             gba_ref, gbb_ref):
        @pl.when(pl.program_id(0) == 0)
        def _():
            gba_ref[...] = jnp.zeros_like(gba_ref)
            gbb_ref[...] = jnp.zeros_like(gbb_ref)

        d = dm_ref[...]
        sa = _sigmoid(ga_ref[...] + ba_ref[...])
        sb = _sigmoid(gb_ref[...] + bb_ref[...])
        dua_ref[...] = (d * sa).astype(BF16)
        dub_ref[...] = (d * sb).astype(BF16)
        dga = d * ua_ref[...] * sa * (1.0 - sa)
        dgb = d * ub_ref[...] * sb * (1.0 - sb)
        dga_ref[...] = dga.astype(BF16)
        dgb_ref[...] = dgb.astype(BF16)
        gba_ref[...] += jnp.sum(dga, axis=0, keepdims=True)
        gbb_ref[...] += jnp.sum(dgb, axis=0, keepdims=True)

    row = pl.BlockSpec((tr, D), lambda i: (i, 0))
    row1 = pl.BlockSpec((tr, D), lambda i: (i, 1))
    v0 = pl.BlockSpec((1, D), lambda i: (0, 0))
    v1 = pl.BlockSpec((1, D), lambda i: (0, 1))
    outs = pl.pallas_call(
        body, name=name, grid=(T // tr,),
        in_specs=[row, row, row1, row, row, v0, v1],
        out_specs=[row, row, row, row, v0, v0],
        out_shape=[_sds((T, D), BF16), _sds((T, D), BF16), _sds((T, D), BF16), _sds((T, D), BF16),
                   _sds((1, D), F32), _sds((1, D), F32)],
        compiler_params=_cp(("arbitrary",)),
    )(dm, gates, gates, u_a, u_b, b_gate, b_gate)
    return outs


def _adamw(name, w, g, m, v):
    R, C = w.shape
    tr = _pick(R, (64, 32, 16, 8))
    c1 = 1.0 - ADAM_B1 ** ADAM_STEP
    c2 = 1.0 - ADAM_B2 ** ADAM_STEP

    def body(w_ref, g_ref, m_ref, v_ref, d_ref, mo_ref, vo_ref):
        gv = g_ref[...]
        mn = ADAM_B1 * m_ref[...] + (1.0 - ADAM_B1) * gv
        vn = ADAM_B2 * v_ref[...] + (1.0 - ADAM_B2) * (gv * gv)
        d_ref[...] = -ADAM_LR * ((mn / c1) / (jnp.sqrt(vn / c2) + ADAM_EPS) + ADAM_WD * w_ref[...])
        mo_ref[...] = mn
        vo_ref[...] = vn

    blk = pl.BlockSpec((tr, C), lambda i: (i, 0))
    return pl.pallas_call(
        body, name=name, grid=(R // tr,),
        in_specs=[blk] * 4, out_specs=[blk] * 3,
        out_shape=[_sds((R, C), F32)] * 3, compiler_params=_cp(("parallel",)),
    )(w, g, m, v)


def _add_bf16(name, a, a_row0, b):
    S, h, C = b.shape
    tr = _pick(h, (256, 128, 64, 32, 16))
    nb = h // tr

    def body(off_ref, a_ref, b_ref, o_ref):
        o_ref[...] = (a_ref[...].astype(F32) + b_ref[...].astype(F32)).astype(BF16)

    gs = pltpu.PrefetchScalarGridSpec(
        num_scalar_prefetch=1, grid=(S, nb),
        in_specs=[pl.BlockSpec((None, tr, C), lambda s, i, off: (s, off[0] * nb + i, 0)),
                  pl.BlockSpec((None, tr, C), lambda s, i, off: (s, i, 0))],
        out_specs=pl.BlockSpec((None, tr, C), lambda s, i, off: (s, i, 0)))
    return pl.pallas_call(body, name=name, grid_spec=gs, out_shape=_sds((S, h, C), BF16),
                          compiler_params=_cp(("parallel", "parallel")))(
        jnp.reshape(a_row0, (1,)).astype(jnp.int32), a, b)


def _sum4(name, got, mine, chip, core):
    S, h, C = got.shape
    tr = _pick(h, (256, 128, 64, 32, 16))
    nb = h // tr

    def body(chip_ref, core_ref, m_ref, g_ref, o_ref):
        acc = m_ref[...].astype(F32)
        for s in range(S):
            acc = acc + g_ref[s].astype(F32)
        o_ref[...] = acc

    gs = pltpu.PrefetchScalarGridSpec(
        num_scalar_prefetch=2, grid=(nb,),
        in_specs=[pl.BlockSpec((None, tr, C), lambda i, kc, cc: (kc[0], i, 0)),
                  pl.BlockSpec((S, tr, C), lambda i, kc, cc: (0, i, 0))],
        out_specs=pl.BlockSpec((tr, C), lambda i, kc, cc: (cc[0] * nb + i, 0)))
    return pl.pallas_call(body, name=name, grid_spec=gs, out_shape=_sds((2 * h, C), F32),
                          compiler_params=_cp(("parallel",)))(
        jnp.reshape(chip, (1,)).astype(jnp.int32), jnp.reshape(core, (1,)).astype(jnp.int32), mine, got)


def _place():
    x, y, c = lax.axis_index("x"), lax.axis_index("y"), lax.axis_index("c")
    chips = [(1 - x, y), (x, 1 - y), (1 - x, 1 - y)]
    return x, y, c, chips


def _allgather(name, shards):
    n = len(shards)

    def body(*refs):
        out_refs = refs[n:2 * n]
        send_sems, recv_sems = refs[2 * n:]
        x, y, c, chips = _place()
        k = 2 * x + y
        sibling = (x, y, 1 - c)
        firsts, passed = [], []
        for a in range(n):
            out = out_refs[a]
            h = out.shape[1] // 2
            for j, (cx, cy) in enumerate(chips):
                rows = out.at[k, pl.ds(c * h, h), :]
                cp = pltpu.make_async_remote_copy(
                    src_ref=rows, dst_ref=rows,
                    send_sem=send_sems.at[6 * a + j], recv_sem=recv_sems.at[6 * a + j],
                    device_id=(cx, cy, c), device_id_type=MESH)
                cp.start()
                firsts.append(cp)
        for a in range(n):
            out = out_refs[a]
            h = out.shape[1] // 2
            for j, (cx, cy) in enumerate(chips):
                kj = 2 * cx + cy
                rows = out.at[kj, pl.ds(c * h, h), :]
                pltpu.make_async_remote_copy(
                    src_ref=rows, dst_ref=rows, send_sem=send_sems.at[6 * a + j], recv_sem=recv_sems.at[6 * a + j],
                    device_id=(cx, cy, c), device_id_type=MESH).wait_recv()
                fw = pltpu.make_async_remote_copy(
                    src_ref=rows, dst_ref=rows, send_sem=send_sems.at[6 * a + 3 + j],
                    recv_sem=recv_sems.at[6 * a + 3 + j], device_id=sibling, device_id_type=MESH)
                fw.start()
                passed.append(fw)
        for a in range(n):
            out = out_refs[a]
            h = out.shape[1] // 2
            for j, (cx, cy) in enumerate(chips):
                kj = 2 * cx + cy
                rows = out.at[kj, pl.ds((1 - c) * h, h), :]
                pltpu.make_async_remote_copy(
                    src_ref=rows, dst_ref=rows, send_sem=send_sems.at[6 * a + 3 + j],
                    recv_sem=recv_sems.at[6 * a + 3 + j], device_id=sibling, device_id_type=MESH).wait_recv()
        for cp in firsts + passed:
            cp.wait_send()

    return pl.pallas_call(
        body, name=name,
        in_specs=[ANY] * n, out_specs=[ANY] * n,
        out_shape=[_sds(s.shape, s.dtype) for s in shards],
        input_output_aliases={a: a for a in range(n)},
        scratch_shapes=[pltpu.SemaphoreType.DMA((6 * n,)), pltpu.SemaphoreType.DMA((6 * n,))],
    )(*shards)


class _Job:
    def __init__(self, ins, out_shapes, aliases, n_sems, start, finish):
        self.ins, self.out_shapes, self.aliases, self.n_sems = list(ins), list(out_shapes), dict(aliases), n_sems
        self.start, self.finish = start, finish


def _rdma(src, dst, ss, rs, idx, to):
    return pltpu.make_async_remote_copy(src_ref=src, dst_ref=dst, send_sem=ss.at[idx], recv_sem=rs.at[idx],
                                        device_id=to, device_id_type=MESH)


def _job_gather_ici(bufs, part=(0, 1)):
    pi, pn = part

    def descs(outs, ss, rs, incoming):
        x, y, c, chips = _place()
        res = []
        for a, out in enumerate(outs):
            h = out.shape[1] // 2
            hp = h // pn
            for j, (cx, cy) in enumerate(chips):
                rows = out.at[(2 * cx + cy) if incoming else (2 * x + y), pl.ds(c * h + pi * hp, hp), :]
                res.append(_rdma(rows, rows, ss, rs, 3 * a + j, (cx, cy, c)))
        return res

    def start(ins, outs, ss, rs):
        for d in descs(outs, ss, rs, False):
            d.start()

    def finish(ins, outs, ss, rs):
        for d in descs(outs, ss, rs, True):
            d.wait_recv()
        for d in descs(outs, ss, rs, False):
            d.wait_send()

    return _Job(bufs, [_sds(b.shape, b.dtype) for b in bufs], {a: a for a in range(len(bufs))}, 3 * len(bufs),
                start, finish)


def _job_gather_d2d(bufs):
    def descs(outs, ss, rs, incoming):
        x, y, c, chips = _place()
        res = []
        for a, out in enumerate(outs):
            h = out.shape[1] // 2
            for j, (cx, cy) in enumerate(chips):
                rows = out.at[2 * cx + cy, pl.ds(((1 - c) if incoming else c) * h, h), :]
                res.append(_rdma(rows, rows, ss, rs, 3 * a + j, (x, y, 1 - c)))
        return res

    def start(ins, outs, ss, rs):
        for d in descs(outs, ss, rs, False):
            d.start()

    def finish(ins, outs, ss, rs):
        for d in descs(outs, ss, rs, True):
            d.wait_recv()
        for d in descs(outs, ss, rs, False):
            d.wait_send()

    return _Job(bufs, [_sds(b.shape, b.dtype) for b in bufs], {a: a for a in range(len(bufs))}, 3 * len(bufs),
                start, finish)


def _job_sibling(grads):
    def descs(ins, outs, ss, rs):
        x, y, c, _ = _place()
        res = []
        for a, (g, out) in enumerate(zip(ins, outs)):
            h = g.shape[1] // 2
            res.append(_rdma(g.at[:, pl.ds((1 - c) * h, h), :], out, ss, rs, a, (x, y, 1 - c)))
        return res

    def start(ins, outs, ss, rs):
        for d in descs(ins, outs, ss, rs):
            d.start()

    def finish(ins, outs, ss, rs):
        for d in descs(ins, outs, ss, rs):
            d.wait()

    return _Job(grads, [_sds((g.shape[0], g.shape[1] // 2, g.shape[2]), g.dtype) for g in grads], {}, len(grads),
                start, finish)


def _job_scatter(parts, part=(0, 1), into=None):
    pi, pn = part
    n = len(parts)

    def descs(ins, outs, ss, rs):
        x, y, c, chips = _place()
        res = []
        for a, (p, out) in enumerate(zip(ins[:n], outs)):
            hp = p.shape[1] // pn
            for j, (cx, cy) in enumerate(chips):
                res.append(_rdma(p.at[2 * cx + cy, pl.ds(pi * hp, hp), :], out.at[j, pl.ds(pi * hp, hp), :],
                                 ss, rs, 3 * a + j, (cx, cy, c)))
        return res

    def start(ins, outs, ss, rs):
        for d in descs(ins, outs, ss, rs):
            d.start()

    def finish(ins, outs, ss, rs):
        for d in descs(ins, outs, ss, rs):
            d.wait()

    shapes = [_sds((3,) + p.shape[1:], p.dtype) for p in parts]
    if into is None:
        return _Job(parts, shapes, {}, 3 * n, start, finish)
    return _Job(list(parts) + list(into), shapes, {n + a: a for a in range(n)}, 3 * n, start, finish)


def _job_swap(fulls):
    def descs(outs, ss, rs, incoming):
        x, y, c, _ = _place()
        res = []
        for a, out in enumerate(outs):
            h = out.shape[0] // 2
            rows = out.at[pl.ds(((1 - c) if incoming else c) * h, h), :]
            res.append(_rdma(rows, rows, ss, rs, a, (x, y, 1 - c)))
        return res

    def start(ins, outs, ss, rs):
        for d in descs(outs, ss, rs, False):
            d.start()

    def finish(ins, outs, ss, rs):
        for d in descs(outs, ss, rs, True):
            d.wait_recv()
        for d in descs(outs, ss, rs, False):
            d.wait_send()

    return _Job(fulls, [_sds(f.shape, f.dtype) for f in fulls], {a: a for a in range(len(fulls))}, len(fulls),
                start, finish)


def _pcall(body, *, name, grid, in_specs, out_specs, out_shape, scratch_shapes=(), sem, jobs=()):
    in_specs, out_specs, out_shape = list(in_specs), list(out_specs), list(out_shape)
    scratch = list(scratch_shapes)
    n_in, n_out, n_scr = len(in_specs), len(out_shape), len(scratch)
    if not jobs:
        call = pl.pallas_call(body, name=name, grid=grid, in_specs=in_specs, out_specs=out_specs, out_shape=out_shape,
                              scratch_shapes=scratch, compiler_params=_cp(sem))
        return lambda *args: (call(*args), [])
    jin = sum(len(j.ins) for j in jobs)
    jout = sum(len(j.out_shapes) for j in jobs)
    aliases, pi, po = {}, n_in, n_out
    for j in jobs:
        for ia, oa in j.aliases.items():
            aliases[pi + ia] = po + oa
        pi, po = pi + len(j.ins), po + len(j.out_shapes)

    def wrapped(*refs):
        ins = refs[:n_in]
        jins = refs[n_in:n_in + jin]
        outs = refs[n_in + jin:n_in + jin + n_out]
        jouts = refs[n_in + jin + n_out:n_in + jin + n_out + jout]
        scr = refs[n_in + jin + n_out + jout:n_in + jin + n_out + jout + n_scr]
        sems = refs[n_in + jin + n_out + jout + n_scr:]
        first, last = None, None
        for d, g in enumerate(grid):
            f, l = pl.program_id(d) == 0, pl.program_id(d) == g - 1
            first = f if first is None else jnp.logical_and(first, f)
            last = l if last is None else jnp.logical_and(last, l)

        def each(what):
            pi, po = 0, 0
            for q, j in enumerate(jobs):
                getattr(j, what)(jins[pi:pi + len(j.ins)], jouts[po:po + len(j.out_shapes)], sems[2 * q], sems[2 * q + 1])
                pi, po = pi + len(j.ins), po + len(j.out_shapes)

        @pl.when(first)
        def _():
            each("start")

        body(*ins, *outs, *scr)

        @pl.when(last)
        def _():
            each("finish")

    call = pl.pallas_call(
        wrapped, name=name, grid=grid,
        in_specs=in_specs + [ANY] * jin, out_specs=out_specs + [ANY] * jout,
        out_shape=out_shape + [s for j in jobs for s in j.out_shapes],
        input_output_aliases=aliases,
        scratch_shapes=scratch + [pltpu.SemaphoreType.DMA((j.n_sems,)) for j in jobs for _ in range(2)],
        compiler_params=_cp(("arbitrary",) * len(grid)))

    def run(*args):
        res = call(*args, *[a for j in jobs for a in j.ins])
        main, rest, per_job = list(res[:n_out]), list(res[n_out:]), []
        for j in jobs:
            per_job.append(rest[:len(j.out_shapes)])
            rest = rest[len(j.out_shapes):]
        return main, per_job
    return run


def _comm_only(name, jobs):
    jin = sum(len(j.ins) for j in jobs)
    jout = sum(len(j.out_shapes) for j in jobs)
    aliases, pi, po = {}, 0, 0
    for j in jobs:
        for ia, oa in j.aliases.items():
            aliases[pi + ia] = po + oa
        pi, po = pi + len(j.ins), po + len(j.out_shapes)

    def body(*refs):
        jins, jouts, sems = refs[:jin], refs[jin:jin + jout], refs[jin + jout:]
        for what in ("start", "finish"):
            pi, po = 0, 0
            for q, j in enumerate(jobs):
                getattr(j, what)(jins[pi:pi + len(j.ins)], jouts[po:po + len(j.out_shapes)], sems[2 * q], sems[2 * q + 1])
                pi, po = pi + len(j.ins), po + len(j.out_shapes)

    res = pl.pallas_call(
        body, name=name, in_specs=[ANY] * jin, out_specs=[ANY] * jout,
        out_shape=[s for j in jobs for s in j.out_shapes], input_output_aliases=aliases,
        scratch_shapes=[pltpu.SemaphoreType.DMA((j.n_sems,)) for j in jobs for _ in range(2)],
    )(*[a for j in jobs for a in j.ins])
    rest, per_job = list(res), []
    for j in jobs:
        per_job.append(rest[:len(j.out_shapes)])
        rest = rest[len(j.out_shapes):]
    return per_job


def _small_allreduce(name, v):
    m_per, n = v.shape

    def body(x_ref, sum_ref, all_ref, send_sems, recv_sems, local_sem):
        x, y, c, chips = _place()
        me, sibling = (x, y, c), (x, y, 1 - c)

        def rows(px, py, pc):
            return all_ref.at[pl.ds((4 * px + 2 * py + pc) * m_per, m_per), :]

        def copy(kk, block, to, src=None):
            return pltpu.make_async_remote_copy(
                src_ref=rows(*block) if src is None else src, dst_ref=rows(*block),
                send_sem=send_sems.at[kk], recv_sem=recv_sems.at[kk], device_id=to, device_id_type=MESH)

        mine = pltpu.make_async_copy(x_ref, rows(*me), local_sem)
        mine.start()
        first = [copy(0, me, sibling, src=x_ref)]
        first += [copy(1 + j, me, (*chip, c), src=x_ref) for j, chip in enumerate(chips)]
        for cp in first:
            cp.start()
        passed = [copy(4 + j, (*chip, c), sibling) for j, chip in enumerate(chips)]
        for j, chip in enumerate(chips):
            copy(1 + j, (*chip, c), me).wait_recv()
            passed[j].start()
        copy(0, sibling, me).wait_recv()
        for j, chip in enumerate(chips):
            copy(4 + j, (*chip, 1 - c), me).wait_recv()
        for cp in first + passed:
            cp.wait_send()
        mine.wait()
        acc = all_ref[pl.ds(0, m_per), :]
        for d in range(1, 8):
            acc = acc + all_ref[pl.ds(d * m_per, m_per), :]
        sum_ref[...] = acc

    vm = pl.BlockSpec(memory_space=pltpu.VMEM)
    return pl.pallas_call(
        body, name=name, in_specs=[vm], out_specs=[vm, vm],
        out_shape=[_sds((m_per, n), F32), _sds((8 * m_per, n), F32)],
        scratch_shapes=[pltpu.SemaphoreType.DMA((7,)), pltpu.SemaphoreType.DMA((7,)), pltpu.SemaphoreType.DMA],
    )(v)[0]


def _in_layout(D, W6, nhb, Ls):
    nmain = W6 + 2 * D
    lay = []
    for k in range(N_CHIPS):
        g0, g1 = k * Ls, (k + 1) * Ls
        pieces = []
        a, b = max(g0, 0), min(g1, W6)
        if a < b:
            pieces.append((a - g0, b - g0, a))
        a, b = max(g0, W6 + nhb), min(g1, W6 + nhb + 2 * D)
        if a < b:
            pieces.append((a - g0, b - g0, a - nhb))
        a, b = max(g0, W6), min(g1, W6 + nhb)
        fpiece = (a - g0, b - g0, a - W6) if a < b else None
        assert fpiece is None or (b - a) == nhb
        main0 = min(p[2] for p in pieces)
        main1 = max(p[2] + p[1] - p[0] for p in pieces)
        lay.append(dict(pieces=pieces, f=fpiece, s=main0 // LANE, e=-(-main1 // LANE), main1=main1))
    assert sum(1 for l in lay if l["f"] is not None) == 1
    nbw = max(l["e"] - l["s"] + (1 if l["f"] else 0) for l in lay)
    for k in range(1, N_CHIPS):
        assert lay[k]["s"] >= lay[k - 1]["e"] - 1 and lay[k]["s"] > lay[k - 1]["s"]
    return lay, nbw, nmain


def _to_window(w, lay_k, nbw):
    D = w.shape[0]
    items = [(c0 - lay_k["s"] * LANE, l0, l1) for (l0, l1, c0) in lay_k["pieces"]]
    if lay_k["f"]:
        l0, l1, off = lay_k["f"]
        items.append(((lay_k["e"] - lay_k["s"]) * LANE + off, l0, l1))
    items.sort()
    cols, pos = [], 0
    for w0, l0, l1 in items:
        if w0 > pos:
            cols.append(jnp.zeros((D, w0 - pos), w.dtype))
        cols.append(w[:, l0:l1])
        pos = w0 + (l1 - l0)
    if pos < nbw * LANE:
        cols.append(jnp.zeros((D, nbw * LANE - pos), w.dtype))
    return jnp.concatenate(cols, axis=1)


def _from_window(win, lay_k):
    items = [(l0, c0 - lay_k["s"] * LANE, l1 - l0) for (l0, l1, c0) in lay_k["pieces"]]
    if lay_k["f"]:
        l0, l1, off = lay_k["f"]
        items.append((l0, (lay_k["e"] - lay_k["s"]) * LANE + off, l1 - l0))
    items.sort()
    return jnp.concatenate([win[:, w0:w0 + n] for (_, w0, n) in items], axis=1)


def _assemble_in(name, wins, lay, nbw, nmain):
    _, D, _ = wins.shape
    ncb = nmain // LANE + 1
    k1 = np.zeros(ncb, np.int32)
    i1 = np.zeros(ncb, np.int32)
    k2 = np.zeros(ncb, np.int32)
    i2 = np.zeros(ncb, np.int32)
    fl = np.zeros(ncb, np.int32)
    for b in range(ncb - 1):
        k = max(kk for kk in range(N_CHIPS) if lay[kk]["s"] <= b)
        k1[b], i1[b] = k, b - lay[k]["s"]
        if k >= 1 and b == lay[k]["s"] and lay[k - 1]["main1"] > b * LANE:
            k2[b], i2[b], fl[b] = k - 1, b - lay[k - 1]["s"], 1
    kf = [kk for kk in range(N_CHIPS) if lay[kk]["f"]][0]
    k1[ncb - 1], i1[ncb - 1] = kf, lay[kf]["e"] - lay[kf]["s"]

    def body(k1_ref, i1_ref, k2_ref, i2_ref, fl_ref, a_ref, b_ref, o_ref):
        b = pl.program_id(0)
        add = jnp.where(fl_ref[b] == 1, b_ref[...], jnp.zeros_like(b_ref))
        o_ref[...] = a_ref[...] + add

    gs = pltpu.PrefetchScalarGridSpec(
        num_scalar_prefetch=5, grid=(ncb,),
        in_specs=[pl.BlockSpec((None, D, LANE), lambda b, k1r, i1r, k2r, i2r, flr: (k1r[b], 0, i1r[b])),
                  pl.BlockSpec((None, D, LANE), lambda b, k1r, i1r, k2r, i2r, flr: (k2r[b], 0, i2r[b]))],
        out_specs=pl.BlockSpec((D, LANE), lambda b, k1r, i1r, k2r, i2r, flr: (0, b)))
    return pl.pallas_call(body, name=name, grid_spec=gs, out_shape=_sds((D, ncb * LANE), BF16),
                          compiler_params=_cp(("parallel",)))(
        jnp.asarray(k1), jnp.asarray(i1), jnp.asarray(k2), jnp.asarray(i2), jnp.asarray(fl), wins, wins)


def _hgroup(nh):
    return _pick(nh, (4, 2, 1))


def _a_specs_q(nh, G):
    ngrp = nh // G
    blk = (GROUP, G * HEAD_DIM)
    q = pl.BlockSpec(blk, lambda i, hg: (i, hg))
    ks = [pl.BlockSpec(blk, functools.partial(
        lambda i, hg, j: (jnp.maximum(i - (WIN_BLOCKS - 1) + j, 0), ngrp + hg), j=j)) for j in range(WIN_BLOCKS)]
    vs = [pl.BlockSpec(blk, functools.partial(
        lambda i, hg, j: (jnp.maximum(i - (WIN_BLOCKS - 1) + j, 0), 2 * ngrp + hg), j=j)) for j in range(WIN_BLOCKS)]
    return q, ks, vs


def _a_logits(q, ks, bias, i, scale):
    parts = [lax.dot_general(q, k, NT, preferred_element_type=F32) for k in ks]
    s = jnp.concatenate(parts, axis=1) * scale + bias
    col = lax.broadcasted_iota(jnp.int32, s.shape, 1)
    return jnp.where(col >= (WIN_BLOCKS - 1 - i) * GROUP, s, NEG_INF)


def _attn_a_fwd(name, qkv, bias2, nh, jobs=()):
    T = qkv.shape[0]
    ng = T // GROUP
    G = _hgroup(nh)
    scale = HEAD_DIM ** -0.5

    def body(q_ref, *refs):
        k_refs = refs[:WIN_BLOCKS]
        v_refs = refs[WIN_BLOCKS:2 * WIN_BLOCKS]
        bias_ref, o_ref, lse_ref = refs[2 * WIN_BLOCKS:]
        i, hg = pl.program_id(0), pl.program_id(1)

        @pl.when(hg == 0)
        def _():
            lse_ref[...] = jnp.zeros_like(lse_ref)

        for g in range(G):
            h = hg * G + g
            sl = slice(g * HEAD_DIM, (g + 1) * HEAD_DIM)
            s = _a_logits(q_ref[:, sl], [kr[:, sl] for kr in k_refs], bias_ref[h], i, scale)
            m = jnp.max(s, axis=1, keepdims=True)
            p = jnp.exp(s - m)
            l = jnp.sum(p, axis=1, keepdims=True)
            pb = (p / l).astype(BF16)
            o = jnp.zeros((GROUP, HEAD_DIM), F32)
            for j in range(WIN_BLOCKS):
                o = o + jnp.dot(pb[:, j * GROUP:(j + 1) * GROUP], v_refs[j][:, sl], preferred_element_type=F32)
            o_ref[:, sl] = o.astype(BF16)
            _put_col(lse_ref, h, m + jnp.log(l))

    q_spec, k_specs, v_specs = _a_specs_q(nh, G)
    stat = pl.BlockSpec((GROUP, LANE), lambda i, hg: (i, 0))
    return _pcall(
        body, name=name, grid=(ng, nh // G),
        in_specs=[q_spec] + k_specs + v_specs + [pl.BlockSpec((nh, GROUP, WIN), lambda i, hg: (0, 0, 0))],
        out_specs=[pl.BlockSpec((GROUP, G * HEAD_DIM), lambda i, hg: (i, hg)), stat],
        out_shape=[_sds((T, nh * HEAD_DIM), BF16), _sds((T, LANE), F32)],
        sem=("parallel", "arbitrary"), jobs=jobs,
    )(qkv, *([qkv] * (2 * WIN_BLOCKS)), bias2)


def _attn_a_dq(name, qkv, do, lse, bias2, nh, jobs=()):
    T = qkv.shape[0]
    ng = T // GROUP
    G = _hgroup(nh)
    scale = HEAD_DIM ** -0.5

    def body(q_ref, *refs):
        k_refs = refs[:WIN_BLOCKS]
        v_refs = refs[WIN_BLOCKS:2 * WIN_BLOCKS]
        do_ref, lse_ref, bias_ref, dq_ref, delta_ref, db_ref = refs[2 * WIN_BLOCKS:]
        i, hg = pl.program_id(0), pl.program_id(1)

        @pl.when(hg == 0)
        def _():
            delta_ref[...] = jnp.zeros_like(delta_ref)

        @pl.when(i == 0)
        def _():
            for g in range(G):
                db_ref[hg * G + g] = jnp.zeros((GROUP, WIN), F32)

        for g in range(G):
            h = hg * G + g
            sl = slice(g * HEAD_DIM, (g + 1) * HEAD_DIM)
            ks = [kr[:, sl] for kr in k_refs]
            s = _a_logits(q_ref[:, sl], ks, bias_ref[h], i, scale)
            p = jnp.exp(s - _col_of(lse_ref[...], h))
            dov = do_ref[:, sl]
            dp = jnp.concatenate([lax.dot_general(dov, vr[:, sl], NT, preferred_element_type=F32) for vr in v_refs],
                                 axis=1)
            delta = jnp.sum(p * dp, axis=1, keepdims=True)
            ds = p * (dp - delta)
            db_ref[h] += ds
            dsb = ds.astype(BF16)
            dq = jnp.zeros((GROUP, HEAD_DIM), F32)
            for j in range(WIN_BLOCKS):
                dq = dq + jnp.dot(dsb[:, j * GROUP:(j + 1) * GROUP], ks[j], preferred_element_type=F32)
            dq_ref[:, sl] = (dq * scale).astype(BF16)
            _put_col(delta_ref, h, delta)

    q_spec, k_specs, v_specs = _a_specs_q(nh, G)
    blk = pl.BlockSpec((GROUP, G * HEAD_DIM), lambda i, hg: (i, hg))
    stat = pl.BlockSpec((GROUP, LANE), lambda i, hg: (i, 0))
    full_b = pl.BlockSpec((nh, GROUP, WIN), lambda i, hg: (0, 0, 0))
    return _pcall(
        body, name=name, grid=(ng, nh // G),
        in_specs=[q_spec] + k_specs + v_specs + [blk, stat, full_b],
        out_specs=[blk, stat, full_b],
        out_shape=[_sds((T, nh * HEAD_DIM), BF16), _sds((T, LANE), F32), _sds((nh, GROUP, WIN), F32)],
        sem=("arbitrary", "arbitrary"), jobs=jobs,
    )(qkv, *([qkv] * (2 * WIN_BLOCKS)), do, lse, bias2)


def _attn_a_dkv(name, qkv, do, lse, delta, bias2, nh, jobs=()):
    T = qkv.shape[0]
    ng = T // GROUP
    G = _hgroup(nh)
    ngrp = nh // G
    scale = HEAD_DIM ** -0.5
    nj = WIN_BLOCKS

    def body(k_ref, v_ref, *refs):
        q_refs = refs[:nj]
        do_refs = refs[nj:2 * nj]
        lse_refs = refs[2 * nj:3 * nj]
        dl_refs = refs[3 * nj:4 * nj]
        bias_ref, dk_ref, dv_ref = refs[4 * nj:]
        r, hg = pl.program_id(0), pl.program_id(1)
        for g in range(G):
            h = hg * G + g
            sl = slice(g * HEAD_DIM, (g + 1) * HEAD_DIM)
            kv, vv = k_ref[:, sl], v_ref[:, sl]
            bias = bias_ref[h]
            dk = jnp.zeros((GROUP, HEAD_DIM), F32)
            dv = jnp.zeros((GROUP, HEAD_DIM), F32)
            for j in range(nj):
                qv, dov = q_refs[j][:, sl], do_refs[j][:, sl]
                c0 = (nj - 1 - j) * GROUP
                s = lax.dot_general(qv, kv, NT, preferred_element_type=F32) * scale + bias[:, c0:c0 + GROUP]
                p = jnp.exp(s - _col_of(lse_refs[j][...], h))
                p = jnp.where(r + j <= ng - 1, p, 0.0)
                dp = lax.dot_general(dov, vv, NT, preferred_element_type=F32)
                ds = p * (dp - _col_of(dl_refs[j][...], h))
                dv = dv + lax.dot_general(p.astype(BF16), dov, TN, preferred_element_type=F32)
                dk = dk + lax.dot_general(ds.astype(BF16), qv, TN, preferred_element_type=F32)
            dk_ref[:, sl] = (dk * scale).astype(BF16)
            dv_ref[:, sl] = dv.astype(BF16)

    def qmap(j):
        return functools.partial(lambda r, hg, j: (jnp.minimum(r + j, ng - 1), hg), j=j)

    def smap(j):
        return functools.partial(lambda r, hg, j: (jnp.minimum(r + j, ng - 1), 0), j=j)

    blk = (GROUP, G * HEAD_DIM)
    in_specs = ([pl.BlockSpec(blk, lambda r, hg: (r, ngrp + hg)), pl.BlockSpec(blk, lambda r, hg: (r, 2 * ngrp + hg))]
                + [pl.BlockSpec(blk, qmap(j)) for j in range(nj)]
                + [pl.BlockSpec(blk, qmap(j)) for j in range(nj)]
                + [pl.BlockSpec((GROUP, LANE), smap(j)) for j in range(nj)]
                + [pl.BlockSpec((GROUP, LANE), smap(j)) for j in range(nj)]
                + [pl.BlockSpec((nh, GROUP, WIN), lambda r, hg: (0, 0, 0))])
    out = pl.BlockSpec(blk, lambda r, hg: (r, hg))
    return _pcall(
        body, name=name, grid=(ng, ngrp), in_specs=in_specs, out_specs=[out, out],
        out_shape=[_sds((T, nh * HEAD_DIM), BF16)] * 2,
        sem=("parallel", "parallel"), jobs=jobs,
    )(qkv, qkv, *([qkv] * nj), *([do] * nj), *([lse] * nj), *([delta] * nj), bias2)


def _fox_prep(name, f, b_f):
    T = f.shape[0]
    tb = _pick(T, (256, 128))

    def body(f_ref, b_ref, cum_ref, cumt_ref, carry_ref):
        @pl.when(pl.program_id(0) == 0)
        def _():
            carry_ref[...] = jnp.zeros_like(carry_ref)

        z = f_ref[...] + b_ref[...]
        logf = jnp.minimum(z, 0.0) - jnp.log(1.0 + jnp.exp(-jnp.abs(z)))
        row = lax.broadcasted_iota(jnp.int32, (tb, tb), 0)
        col = lax.broadcasted_iota(jnp.int32, (tb, tb), 1)
        tri = (row >= col).astype(BF16)
        acc = jnp.zeros((tb, LANE), F32)
        for piece in _split3(logf):
            acc = acc + jnp.dot(tri, piece, preferred_element_type=F32)
        cum = acc + carry_ref[...]
        cum_ref[...] = cum
        cumt_ref[...] = cum.T
        carry_ref[...] = cum_ref[pl.ds(tb - 1, 1), :]

    return pl.pallas_call(
        body, name=name, grid=(T // tb,),
        in_specs=[pl.BlockSpec((tb, LANE), lambda i: (i, 0)), pl.BlockSpec((1, LANE), lambda i: (0, 0))],
        out_specs=[pl.BlockSpec((tb, LANE), lambda i: (i, 0)), pl.BlockSpec((LANE, tb), lambda i: (0, i))],
        out_shape=[_sds((T, LANE), F32), _sds((LANE, T), F32)],
        scratch_shapes=[pltpu.VMEM((1, LANE), F32)],
        compiler_params=_cp(("arbitrary",)),
    )(f, b_f)


def _fox_blk(T):
    return _pick(T, (256, 128))


def _fox_allowed(i, j, tq, tk):
    diff = lax.broadcasted_iota(jnp.int32, (tq, tk), 1) - lax.broadcasted_iota(jnp.int32, (tq, tk), 0)
    return diff <= (i - j) * tq


def _fox_fwd(name, qkv, cum, cumt, nh, jobs=()):
    T = qkv.shape[0]
    tq = tk = _fox_blk(T)
    G = _hgroup(nh)
    ngrp = nh // G
    scale = HEAD_DIM ** -0.5

    def body(q_ref, k_ref, v_ref, cum_ref, cumt_ref, o_ref, lse_ref):
        i, hg = pl.program_id(0), pl.program_id(1)

        @pl.when(hg == 0)
        def _():
            lse_ref[...] = jnp.zeros_like(lse_ref)

        sls = [slice(g * HEAD_DIM, (g + 1) * HEAD_DIM) for g in range(G)]
        qs = [q_ref[:, sl] for sl in sls]
        cqs = [_col_of(cum_ref[...], hg * G + g) for g in range(G)]

        def step(j, carry):
            k0 = pl.multiple_of(j * tk, tk)
            ok = _fox_allowed(i, j, tq, tk)
            out = []
            for g in range(G):
                m, l, acc = carry[g]
                kj = k_ref[pl.ds(k0, tk), sls[g]]
                vj = v_ref[pl.ds(k0, tk), sls[g]]
                ck = cumt_ref[pl.ds(hg * G + g, 1), pl.ds(k0, tk)]
                s = lax.dot_general(qs[g], kj, NT, preferred_element_type=F32) * scale + (cqs[g] - ck)
                s = jnp.where(ok, s, NEG_INF)
                m_new = jnp.maximum(m, jnp.max(s, axis=1, keepdims=True))
                alpha = jnp.exp(m - m_new)
                p = jnp.exp(s - m_new)
                l = alpha * l + jnp.sum(p, axis=1, keepdims=True)
                acc = alpha * acc + jnp.dot(p.astype(BF16), vj, preferred_element_type=F32)
                out.append((m_new, l, acc))
            return tuple(out)

        one = (jnp.full((tq, 1), NEG_INF, F32), jnp.zeros((tq, 1), F32), jnp.zeros((tq, HEAD_DIM), F32))
        res = lax.fori_loop(0, i + 1, step, tuple(one for _ in range(G)))
        for g in range(G):
            m, l, acc = res[g]
            o_ref[:, sls[g]] = (acc / l).astype(BF16)
            _put_col(lse_ref, hg * G + g, m + jnp.log(l))

    GW = G * HEAD_DIM
    return _pcall(
        body, name=name, grid=(T // tq, ngrp),
        in_specs=[pl.BlockSpec((tq, GW), lambda i, hg: (i, 3 * ngrp + hg)),
                  pl.BlockSpec((T, GW), lambda i, hg: (0, 4 * ngrp + hg)),
                  pl.BlockSpec((T, GW), lambda i, hg: (0, 5 * ngrp + hg)),
                  pl.BlockSpec((tq, LANE), lambda i, hg: (i, 0)),
                  pl.BlockSpec((LANE, T), lambda i, hg: (0, 0))],
        out_specs=[pl.BlockSpec((tq, GW), lambda i, hg: (i, hg)), pl.BlockSpec((tq, LANE), lambda i, hg: (i, 0))],
        out_shape=[_sds((T, nh * HEAD_DIM), BF16), _sds((T, LANE), F32)],
        sem=("parallel", "arbitrary"), jobs=jobs,
    )(qkv, qkv, qkv, cum, cumt)


def _fox_dq(name, qkv, do, lse, cum, cumt, nh, jobs=()):
    T = qkv.shape[0]
    tq = tk = _fox_blk(T)
    G = _hgroup(nh)
    ngrp = nh // G
    GW = G * HEAD_DIM
    scale = HEAD_DIM ** -0.5

    def body(q_ref, k_ref, v_ref, do_ref, lse_ref, cum_ref, cumt_ref, dq_ref, delta_ref):
        i, hg = pl.program_id(0), pl.program_id(1)

        @pl.when(hg == 0)
        def _():
            delta_ref[...] = jnp.zeros_like(delta_ref)

        sls = [slice(g * HEAD_DIM, (g + 1) * HEAD_DIM) for g in range(G)]
        qs = [q_ref[:, sl] for sl in sls]
        dos = [do_ref[:, sl] for sl in sls]
        cqs = [_col_of(cum_ref[...], hg * G + g) for g in range(G)]
        lses = [_col_of(lse_ref[...], hg * G + g) for g in range(G)]

        def p_dp(j, g, ok):
            k0 = pl.multiple_of(j * tk, tk)
            kj = k_ref[pl.ds(k0, tk), sls[g]]
            vj = v_ref[pl.ds(k0, tk), sls[g]]
            ck = cumt_ref[pl.ds(hg * G + g, 1), pl.ds(k0, tk)]
            s = lax.dot_general(qs[g], kj, NT, preferred_element_type=F32) * scale + (cqs[g] - ck)
            p = jnp.exp(jnp.where(ok, s, NEG_INF) - lses[g])
            return p, lax.dot_general(dos[g], vj, NT, preferred_element_type=F32), kj

        def sweep_delta(j, deltas):
            ok = _fox_allowed(i, j, tq, tk)
            out = []
            for g in range(G):
                p, dp, _ = p_dp(j, g, ok)
                out.append(deltas[g] + jnp.sum(p * dp, axis=1, keepdims=True))
            return tuple(out)

        deltas = lax.fori_loop(0, i + 1, sweep_delta, tuple(jnp.zeros((tq, 1), F32) for _ in range(G)))

        def sweep_dq(j, dqs):
            ok = _fox_allowed(i, j, tq, tk)
            out = []
            for g in range(G):
                p, dp, kj = p_dp(j, g, ok)
                ds = p * (dp - deltas[g])
                out.append(dqs[g] + jnp.dot(ds.astype(BF16), kj, preferred_element_type=F32))
            return tuple(out)

        dqs = lax.fori_loop(0, i + 1, sweep_dq, tuple(jnp.zeros((tq, HEAD_DIM), F32) for _ in range(G)))
        for g in range(G):
            dq_ref[:, sls[g]] = (dqs[g] * scale).astype(BF16)
            _put_col(delta_ref, hg * G + g, deltas[g])

    blk = pl.BlockSpec((tq, GW), lambda i, hg: (i, hg))
    stat = pl.BlockSpec((tq, LANE), lambda i, hg: (i, 0))
    return _pcall(
        body, name=name, grid=(T // tq, ngrp),
        in_specs=[pl.BlockSpec((tq, GW), lambda i, hg: (i, 3 * ngrp + hg)),
                  pl.BlockSpec((T, GW), lambda i, hg: (0, 4 * ngrp + hg)),
                  pl.BlockSpec((T, GW), lambda i, hg: (0, 5 * ngrp + hg)),
                  blk, stat, stat, pl.BlockSpec((LANE, T), lambda i, hg: (0, 0))],
        out_specs=[blk, stat],
        out_shape=[_sds((T, nh * HEAD_DIM), BF16), _sds((T, LANE), F32)],
        sem=("parallel", "arbitrary"), jobs=jobs,
    )(qkv, qkv, qkv, do, lse, cum, cumt)


def _fox_dkv(name, qkv, do, lse, delta, cum, cumt, nh, jobs=()):
    T = qkv.shape[0]
    tq = tk = _fox_blk(T)
    nq = T // tq
    G = _hgroup(nh)
    ngrp = nh // G
    GW = G * HEAD_DIM
    scale = HEAD_DIM ** -0.5

    def body(k_ref, v_ref, q_ref, do_ref, lse_ref, dl_ref, cum_ref, cumt_ref, dk_ref, dv_ref, dc_ref):
        j, hg = pl.program_id(0), pl.program_id(1)

        @pl.when(hg == 0)
        def _():
            dc_ref[...] = jnp.zeros_like(dc_ref)

        sls = [slice(g * HEAD_DIM, (g + 1) * HEAD_DIM) for g in range(G)]
        kjs = [k_ref[:, sl] for sl in sls]
        vjs = [v_ref[:, sl] for sl in sls]
        k0 = pl.multiple_of(j * tk, tk)
        cks = [cumt_ref[pl.ds(hg * G + g, 1), pl.ds(k0, tk)] for g in range(G)]

        def step(i, carry):
            q0 = pl.multiple_of(i * tq, tq)
            ok = _fox_allowed(i, j, tq, tk)
            cum_i, lse_i, dl_i = cum_ref[pl.ds(q0, tq), :], lse_ref[pl.ds(q0, tq), :], dl_ref[pl.ds(q0, tq), :]
            out = []
            for g in range(G):
                dk, dv, dc = carry[g]
                h = hg * G + g
                qi = q_ref[pl.ds(q0, tq), sls[g]]
                doi = do_ref[pl.ds(q0, tq), sls[g]]
                s = lax.dot_general(qi, kjs[g], NT, preferred_element_type=F32) * scale + (_col_of(cum_i, h) - cks[g])
                p = jnp.exp(jnp.where(ok, s, NEG_INF) - _col_of(lse_i, h))
                dp = lax.dot_general(doi, vjs[g], NT, preferred_element_type=F32)
                ds = p * (dp - _col_of(dl_i, h))
                dv = dv + lax.dot_general(p.astype(BF16), doi, TN, preferred_element_type=F32)
                dk = dk + lax.dot_general(ds.astype(BF16), qi, TN, preferred_element_type=F32)
                dc = dc - jnp.sum(ds, axis=0, keepdims=True)
                out.append((dk, dv, dc))
            return tuple(out)

        one = (jnp.zeros((tk, HEAD_DIM), F32), jnp.zeros((tk, HEAD_DIM), F32), jnp.zeros((1, tk), F32))
        res = lax.fori_loop(j, nq, step, tuple(one for _ in range(G)))
        sub = lax.broadcasted_iota(jnp.int32, (LANE, tk), 0)
        dc_all = dc_ref[...]
        for g in range(G):
            dk, dv, dc = res[g]
            dk_ref[:, sls[g]] = (dk * scale).astype(BF16)
            dv_ref[:, sls[g]] = dv.astype(BF16)
            dc_all = jnp.where(sub == hg * G + g, dc, dc_all)
        dc_ref[...] = dc_all

    whole = lambda c: pl.BlockSpec((T, GW), c)
    stat = pl.BlockSpec((T, LANE), lambda j, hg: (0, 0))
    out = pl.BlockSpec((tk, GW), lambda j, hg: (j, hg))
    return _pcall(
        body, name=name, grid=(T // tk, ngrp),
        in_specs=[pl.BlockSpec((tk, GW), lambda j, hg: (j, 4 * ngrp + hg)),
                  pl.BlockSpec((tk, GW), lambda j, hg: (j, 5 * ngrp + hg)),
                  whole(lambda j, hg: (0, 3 * ngrp + hg)), whole(lambda j, hg: (0, hg)),
                  stat, stat, stat, pl.BlockSpec((LANE, T), lambda j, hg: (0, 0))],
        out_specs=[out, out, pl.BlockSpec((LANE, tk), lambda j, hg: (0, j))],
        out_shape=[_sds((T, nh * HEAD_DIM), BF16)] * 2 + [_sds((LANE, T), F32)],
        sem=("parallel", "arbitrary"), jobs=jobs,
    )(qkv, qkv, qkv, do, lse, delta, cum, cumt)


def _fox_post(name, dcumt, f, b_f):
    T = f.shape[0]
    tb = _pick(T, (256, 128))
    nb = T // tb

    def body(dc_ref, f_ref, b_ref, df_ref, gb_ref, carry_ref):
        @pl.when(pl.program_id(0) == 0)
        def _():
            carry_ref[...] = jnp.zeros_like(carry_ref)
            gb_ref[...] = jnp.zeros_like(gb_ref)

        dc = dc_ref[...]
        row = lax.broadcasted_iota(jnp.int32, (tb, tb), 0)
        col = lax.broadcasted_iota(jnp.int32, (tb, tb), 1)
        tri = (row >= col).astype(BF16)
        acc = jnp.zeros((LANE, tb), F32)
        for piece in _split3(dc):
            acc = acc + jnp.dot(piece, tri, preferred_element_type=F32)
        dlogf = (acc + carry_ref[...]).T
        carry_ref[...] += jnp.sum(dc, axis=1, keepdims=True)
        z = f_ref[...] + b_ref[...]
        df = dlogf * _sigmoid(-z)
        df_ref[...] = df.astype(BF16)
        gb_ref[...] += jnp.sum(df, axis=0, keepdims=True)

    return pl.pallas_call(
        body, name=name, grid=(nb,),
        in_specs=[pl.BlockSpec((LANE, tb), lambda g: (0, nb - 1 - g)),
                  pl.BlockSpec((tb, LANE), lambda g: (nb - 1 - g, 0)),
                  pl.BlockSpec((1, LANE), lambda g: (0, 0))],
        out_specs=[pl.BlockSpec((tb, LANE), lambda g: (nb - 1 - g, 0)), pl.BlockSpec((1, LANE), lambda g: (0, 0))],
        out_shape=[_sds((T, LANE), BF16), _sds((1, LANE), F32)],
        scratch_shapes=[pltpu.VMEM((LANE, 1), F32)],
        compiler_params=_cp(("arbitrary",)),
    )(dcumt, f, b_f)


def _rel_tables(n_rel):
    max_rel = (n_rel - 1) // 2
    nj = GROUP + WIN - 1
    onehot = np.zeros((n_rel, nj), np.float32)
    for j in range(nj):
        dist = (WIN - 1) - j
        onehot[int(np.clip(dist, -max_rel, max_rel)) + max_rel, j] = 1.0
    a = np.arange(GROUP)[:, None]
    kb = np.arange(WIN)[None, :]
    lo = CHUNK * (a // CHUNK)
    inband = (kb >= lo) & (kb < lo + BAND)
    return onehot, inband


def _bias2_of(rel_bias, onehot, inband):
    bv = jnp.dot(rel_bias, jnp.asarray(onehot), precision=lax.Precision.HIGHEST)
    rows = [bv[:, GROUP - 1 - a:GROUP - 1 - a + WIN] for a in range(GROUP)]
    toe = jnp.stack(rows, axis=1)
    return jnp.where(jnp.asarray(inband)[None], toe, NEG_INF)


def _rel_grad_of(dbias2, onehot):
    nj = GROUP + WIN - 1
    dbv = sum(jnp.pad(dbias2[:, a, :], ((0, 0), (GROUP - 1 - a, nj - WIN - (GROUP - 1 - a)))) for a in range(GROUP))
    return jnp.dot(dbv, jnp.asarray(onehot).T, precision=lax.Precision.HIGHEST)


def kernel(x, g_mix, w_in, b_f, b_gate, rel_bias, w_branch_a, w_branch_b, w_out, g_ffn, w_gate_ffn, w_up_ffn, w_down_ffn, g_final, loss_target, m_g_mix, m_w_in, m_b_f, m_b_gate, m_rel_bias, m_w_branch_a, m_w_branch_b, m_w_out, m_g_ffn, m_w_gate_ffn, m_w_up_ffn, m_w_down_ffn, m_g_final, v_g_mix, v_w_in, v_b_f, v_b_gate, v_rel_bias, v_w_branch_a, v_w_branch_b, v_w_out, v_g_ffn, v_w_gate_ffn, v_w_up_ffn, v_w_down_ffn, v_g_final):
    T, D = x.shape[1], x.shape[2]
    Ls = w_in.shape[2]
    W = w_branch_a.shape[1]
    nh = W // HEAD_DIM
    nhb = b_f.shape[1]
    assert w_branch_b.shape[1] == W and nhb == nh and rel_bias.shape[1] == nh
    W6 = 6 * W
    Fl = w_gate_ffn.shape[2]
    Fp = -(-Fl // LANE) * LANE
    n_rel = rel_bias.shape[2]
    chip = 2 * lax.axis_index("x") + lax.axis_index("y")
    lay, nbw, nmain = _in_layout(D, W6, nhb, Ls)
    onehot, inband = _rel_tables(n_rel)

    xs, tgt = x[0], loss_target[0]

    win_f32 = lax.switch(chip, [functools.partial(_to_window, lay_k=lay[k], nbw=nbw) for k in range(N_CHIPS)], w_in[0])
    pad_c = lambda w: jnp.pad(w, ((0, 0), (0, Fp - Fl)))
    pad_r = lambda w: jnp.pad(w, ((0, Fp - Fl), (0, 0)))
    sh_in = _cast_bf16("cast_w_in", win_f32, chip)
    sh_a = _cast_bf16("cast_w_a", w_branch_a[0], chip)
    sh_b = _cast_bf16("cast_w_b", w_branch_b[0], chip)
    sh_o = _cast_bf16("cast_w_out", w_out[0], chip)
    sh_g = _cast_bf16("cast_w_gate", pad_c(w_gate_ffn[0]), chip)
    sh_u = _cast_bf16("cast_w_up", pad_c(w_up_ffn[0]), chip)
    sh_d = _cast_bf16("cast_w_down", pad_r(w_down_ffn[0]), chip)
    (wins,) = _allgather("ag_w_in", [sh_in])
    wc = _assemble_in("assemble_w_in", wins, lay, nbw, nmain)

    h1, r1 = _rms_fwd("rms1", xs, g_mix)
    qkv, ((wa_g, wb_g, wo_g),) = _mm_nn("proj_qkv", h1, wc, BF16, b_col0=0, n=W6, tm=1024,
                                        jobs=[_job_gather_ici([sh_a, sh_b, sh_o])])
    gates, ((wa_g, wb_g, wo_g), (wg_g,)) = _mm_nn(
        "proj_gates", h1, wc, F32, b_col0=W6, n=2 * D, tm=1024,
        jobs=[_job_gather_d2d([wa_g, wb_g, wo_g]), _job_gather_ici([sh_g], part=(0, 2))])
    fl = _mm_nn("proj_f", h1, wc, F32, b_col0=nmain, n=LANE, tn=LANE)
    bias2 = _bias2_of(rel_bias[0], onehot, inband)
    bf_pad = jnp.pad(b_f, ((0, 0), (0, LANE - nhb)))
    (o_a, lse_a), ((wg_g,),) = _attn_a_fwd("attn_a_fwd", qkv, bias2, nh, jobs=[_job_gather_ici([wg_g], part=(1, 2))])
    cum, cumt = _fox_prep("fox_prep", fl, bf_pad)
    (o_b, lse_b), ((wu_g,), (wg_g,)) = _fox_fwd("fox_fwd", qkv, cum, cumt, nh,
                                                jobs=[_job_gather_ici([sh_u]), _job_gather_d2d([wg_g])])
    u_a = _mm_nn("branch_a", o_a, wa_g, F32)
    u_b = _mm_nn("branch_b", o_b, wb_g, F32)
    merged = _merge_fwd("merge", gates, u_a, u_b, b_gate)
    wo_full = wo_g.reshape(D, D)
    x1, ((wu_g,),) = _mm_nn("out_proj", merged, wo_full, F32, residual=xs, jobs=[_job_gather_d2d([wu_g])])
    h2, r2 = _rms_fwd("rms2", x1, g_ffn)

    tm_f = _pick(T, (1024, 512, 256, 128))
    tn_f = _pick(Fp, (1408, 1024, 512, 256, 128))
    tk_f = _pick(D, (1024, 512, 256, 128))
    nps_f = Fp // tn_f

    def swiglu_ep(accs, e_refs, o_refs):
        g, u = accs
        o_refs[0][...] = g.astype(BF16)
        o_refs[1][...] = u.astype(BF16)
        o_refs[2][...] = (g * _sigmoid(g) * u).astype(BF16)

    hid_spec = pl.BlockSpec((tm_f, tn_f), lambda i, j, k: (i, j))
    wcol_spec = pl.BlockSpec((None, tk_f, tn_f), lambda i, j, k: (j // nps_f, k, j % nps_f))
    (gate, up, hidden), ((wd_g,),) = _mm(
        "ffn_up", "nn", [h2], [pl.BlockSpec((tm_f, tk_f), lambda i, j, k: (i, k))], [wg_g, wu_g], [wcol_spec, wcol_spec],
        [(0, 0, 0), (0, 1, 1)], 2, (T // tm_f, N_CHIPS * Fp // tn_f, D // tk_f), tm_f, tn_f,
        [_sds((T, N_CHIPS * Fp), BF16)] * 3, [hid_spec] * 3, swiglu_ep, jobs=[_job_gather_ici([sh_d])])
    ((wd_g,),) = _comm_only("ag_w_down_d2d", [_job_gather_d2d([wd_g])])
    wd_full = wd_g.reshape(N_CHIPS * Fp, D)
    x2 = _mm_nn("ffn_down", hidden, wd_full, F32, residual=x1, tm=1024, tn=_pick(D, (1024, 512, 256, 128)),
                tk=_pick(N_CHIPS * Fp, (1408, 1024, 512, 256, 128)))

    dx2, dx2b, loss_part, gg_final = _final_loss_bwd("final_loss", x2, tgt, g_final.reshape(1, D))

    def swiglu_bwd_ep(accs, e_refs, o_refs):
        dh = accs[0]
        g = e_refs[0][...].astype(F32)
        u = e_refs[1][...].astype(F32)
        sg = _sigmoid(g)
        o_refs[0][...] = (dh * u * (sg * (1.0 + g * (1.0 - sg)))).astype(BF16)
        o_refs[1][...] = (dh * (g * sg)).astype(BF16)

    tk_b = _pick(D, (1024, 512, 256, 128))
    core = lax.axis_index("c")
    (dgate, dup), _ = _mm(
        "ffn_down_bwd", "nt", [dx2b], [pl.BlockSpec((tm_f, tk_b), lambda i, j, k: (i, k))],
        [wd_full], [pl.BlockSpec((tn_f, tk_b), lambda i, j, k: (j, k))], [(0, 0, 0)], 1,
        (T // tm_f, N_CHIPS * Fp // tn_f, D // tk_b), tm_f, tn_f,
        [_sds((T, N_CHIPS * Fp), BF16)] * 2, [hid_spec] * 2, swiglu_bwd_ep,
        extra=[gate, up], extra_specs=[hid_spec, hid_spec])
    dwd = _mm_tn("dw_down", hidden, dx2b, BF16, tm=_pick(N_CHIPS * Fp, (1408, 1024, 512, 256, 128)))
    dwd = dwd.reshape(N_CHIPS, Fp, D)
    dh2, ((sib_d,),) = _mm_nt("ffn_up_bwd", [dgate, dup], [wg_g, wu_g], F32, tm=1024, jobs=[_job_sibling([dwd])])
    dwg = _mm_tn("dw_gate", h2, dgate, BF16, slots=N_CHIPS)
    dwu = _mm_tn("dw_up", h2, dup, BF16, slots=N_CHIPS)
    dx1, dx1b, gg_ffn = _rms_bwd("rms2_bwd", [dh2], x1, r2, g_ffn, dx2, True)
    part_d = _add_bf16("rs_add_down", dwd, core, sib_d)

    dmerged, ((sib_g, sib_u), (got_d,)) = _mm_nt(
        "out_proj_bwd", [dx1b], [wo_full], F32, jobs=[_job_sibling([dwg, dwu]), _job_scatter([part_d], part=(0, 2))])
    dwo, ((got_d,),) = _mm_tn("dw_out", merged, dx1b, BF16, jobs=[_job_scatter([part_d], part=(1, 2), into=[got_d])])
    dwo = dwo.reshape(N_CHIPS, D // N_CHIPS, D)
    du_a, du_b, dga, dgb, gbg_a, gbg_b = _merge_bwd("merge_bwd", dmerged, gates, u_a, u_b, b_gate)
    part_g = _add_bf16("rs_add_gate", dwg, core, sib_g)
    part_u = _add_bf16("rs_add_up", dwu, core, sib_u)
    do_a = _mm_nt("branch_a_bwd", [du_a], [wa_g], BF16)
    do_b = _mm_nt("branch_b_bwd", [du_b], [wb_g], BF16)
    dwa = _mm_tn("dw_a", o_a, du_a, BF16, slots=N_CHIPS)
    dwb = _mm_tn("dw_b", o_b, du_b, BF16, slots=N_CHIPS)

    (dq_a, delta_a, dbias2), ((got_g,), (sib_a, sib_b, sib_o)) = _attn_a_dq(
        "attn_a_dq", qkv, do_a, lse_a, bias2, nh,
        jobs=[_job_scatter([part_g], part=(0, 2)), _job_sibling([dwa, dwb, dwo])])
    part_a = _add_bf16("rs_add_a", dwa, core, sib_a)
    part_b = _add_bf16("rs_add_b", dwb, core, sib_b)
    part_o = _add_bf16("rs_add_out", dwo, core, sib_o)
    full_d = _sum4("rs_sum_down", got_d, part_d, chip, core)
    (dk_a, dv_a), ((got_g,), (got_u,)) = _attn_a_dkv(
        "attn_a_dkv", qkv, do_a, lse_a, delta_a, bias2, nh,
        jobs=[_job_scatter([part_g], part=(1, 2), into=[got_g]), _job_scatter([part_u], part=(0, 2))])
    full_g = _sum4("rs_sum_gate", got_g, part_g, chip, core)
    (dq_b, delta_b), ((got_u,), (got_a, got_b, got_o)) = _fox_dq(
        "fox_dq", qkv, do_b, lse_b, cum, cumt, nh,
        jobs=[_job_scatter([part_u], part=(1, 2), into=[got_u]), _job_scatter([part_a, part_b, part_o])])
    full_u = _sum4("rs_sum_up", got_u, part_u, chip, core)
    full_a = _sum4("rs_sum_a", got_a, part_a, chip, core)
    full_b = _sum4("rs_sum_b", got_b, part_b, chip, core)
    full_o = _sum4("rs_sum_out", got_o, part_o, chip, core)
    (dk_b, dv_b, dcumt), ((g_d, g_g, g_u, g_a, g_b, g_o),) = _fox_dkv(
        "fox_dkv", qkv, do_b, lse_b, delta_b, cum, cumt, nh,
        jobs=[_job_swap([full_d, full_g, full_u, full_a, full_b, full_o])])
    df, gbf = _fox_post("fox_post", dcumt, fl, bf_pad)

    dqkv = jnp.concatenate([dq_a, dk_a, dv_a, dq_b, dk_b, dv_b], axis=1)
    dgates = jnp.concatenate([dga, dgb], axis=1)
    dwc_q = _mm_tn("dw_in_qkv", h1, dqkv, BF16)
    dwc_g = _mm_tn("dw_in_gates", h1, dgates, BF16)
    dwc_f = _mm_tn("dw_in_f", h1, df, BF16, tn=LANE)
    dwc = jnp.concatenate([dwc_q, dwc_g, dwc_f], axis=1)
    zeros_blk = jnp.zeros((D, LANE), BF16)
    win_parts = []
    for k in range(N_CHIPS):
        cols = [dwc[:, lay[k]["s"] * LANE:lay[k]["e"] * LANE]]
        nb = lay[k]["e"] - lay[k]["s"]
        if lay[k]["f"]:
            cols.append(dwc[:, nmain:nmain + LANE])
            nb += 1
        cols += [zeros_blk] * (nbw - nb)
        win_parts.append(jnp.concatenate(cols, axis=1) if len(cols) > 1 else cols[0])
    dwin = jnp.stack(win_parts, axis=0)
    ((sib_in,),) = _comm_only("rs_sibling_in", [_job_sibling([dwin])])
    part_in = _add_bf16("rs_add_in", dwin, core, sib_in)
    dh_q, ((got_in,),) = _mm_nt("proj_qkv_bwd", [dqkv], [wc], F32, k0_list=[0], tk=_pick(W6, (1024, 512, 256, 128)), tm=1024,
                                jobs=[_job_scatter([part_in], part=(0, 2))])
    dh_g, ((got_in,),) = _mm_nt("proj_gates_bwd", [dgates], [wc], F32, k0_list=[W6], tm=1024,
                                tk=_pick(math_gcd(W6, 2 * D), (1024, 512, 256, 128)),
                                jobs=[_job_scatter([part_in], part=(1, 2), into=[got_in])])
    full_in = _sum4("rs_sum_in", got_in, part_in, chip, core)
    dh_f, ((g_win,),) = _mm_nt("proj_f_bwd", [df], [wc], F32, k0_list=[nmain], tk=LANE, jobs=[_job_swap([full_in])])
    grad_x, gg_mix = _rms_bwd("rms1_bwd", [dh_q, dh_g, dh_f], xs, r1, g_mix, dx1, False)
    g_in = lax.switch(chip, [functools.partial(_from_window, lay_k=lay[k]) for k in range(N_CHIPS)], g_win)
    g_g, g_u, g_d = g_g[:, :Fl], g_u[:, :Fl], g_d[:Fl, :]

    big = {}
    for nm, w, g, m, v in (("w_in", w_in, g_in, m_w_in, v_w_in), ("w_branch_a", w_branch_a, g_a, m_w_branch_a, v_w_branch_a),
                           ("w_branch_b", w_branch_b, g_b, m_w_branch_b, v_w_branch_b), ("w_out", w_out, g_o, m_w_out, v_w_out),
                           ("w_gate_ffn", w_gate_ffn, g_g, m_w_gate_ffn, v_w_gate_ffn),
                           ("w_up_ffn", w_up_ffn, g_u, m_w_up_ffn, v_w_up_ffn),
                           ("w_down_ffn", w_down_ffn, g_d, m_w_down_ffn, v_w_down_ffn)):
        d, mn, vn = _adamw(f"adamw_{nm}", w[0], g, m[0], v[0])
        big[nm] = (g[None], d[None], mn[None], vn[None])

    g_rel = _rel_grad_of(dbias2, onehot)
    small = [("loss", loss_part[:, :1], None, None, None),
             ("g_mix", gg_mix, g_mix, m_g_mix, v_g_mix), ("b_f", gbf[:, :nhb], b_f, m_b_f, v_b_f),
             ("b_gate", jnp.concatenate([gbg_a, gbg_b], axis=1), b_gate, m_b_gate, v_b_gate),
             ("rel_bias", g_rel, rel_bias, m_rel_bias, v_rel_bias), ("g_ffn", gg_ffn, g_ffn, m_g_ffn, v_g_ffn),
             ("g_final", gg_final, g_final, m_g_final, v_g_final)]
    sizes = [int(np.prod(s[1].shape)) for s in small]
    total = sum(sizes)
    npad = -(-total // 1024) * 1024

    def pack(arrs):
        flat = jnp.concatenate([a.reshape(-1).astype(F32) for a in arrs])
        return jnp.pad(flat, (0, npad - total)).reshape(8, npad // 8)

    zero1 = jnp.zeros((1,), F32)
    g_all = _small_allreduce("small_allreduce", pack([s[1] for s in small]))
    w_s = pack([zero1 if s[2] is None else s[2] for s in small])
    m_s = pack([zero1 if s[3] is None else s[3] for s in small])
    v_s = pack([zero1 + 1.0 if s[4] is None else s[4] for s in small])
    d_s, mn_s, vn_s = _adamw("adamw_small", w_s, g_all, m_s, v_s)

    def unpack(packed):
        flat = packed.reshape(-1)
        out, pos = {}, 0
        for s, n in zip(small, sizes):
            if s[2] is not None:
                out[s[0]] = flat[pos:pos + n].reshape(s[2].shape)
            else:
                out[s[0]] = flat[pos:pos + n].reshape(())
            pos += n
        return out

    gs, ds, ms, vs = unpack(g_all), unpack(d_s), unpack(mn_s), unpack(vn_s)
    order = ["g_mix", "w_in", "b_f", "b_gate", "rel_bias", "w_branch_a", "w_branch_b", "w_out", "g_ffn",
             "w_gate_ffn", "w_up_ffn", "w_down_ffn", "g_final"]
    res = [[], [], [], []]
    for nm in order:
        four = big[nm] if nm in big else (gs[nm], ds[nm], ms[nm], vs[nm])
        for q in range(4):
            res[q].append(four[q])
    return (gs["loss"], grad_x[None], *res[0], *res[1], *res[2], *res[3])


def math_gcd(a, b):
    while b:
        a, b = b, a % b
    return a
```

```python
import functools

import numpy as np
import jax
import jax.numpy as jnp
from jax import lax
from jax.experimental import pallas as pl
from jax.experimental.pallas import tpu as pltpu

F32 = jnp.float32
BF16 = jnp.bfloat16
LANE = 128
HEAD_DIM = 128
CHUNK = 64
LEFT_CHUNKS = 8
GROUP = 128
WIN_BLOCKS = 5
WIN = WIN_BLOCKS * GROUP
BAND = (LEFT_CHUNKS + 1) * CHUNK
RMS_EPS = 1e-6
NEG_INF = -1e30
ADAM_LR = 0.001
ADAM_B1 = 0.9
ADAM_B2 = 0.999
ADAM_EPS = 1e-08
ADAM_WD = 0.01
ADAM_STEP = 10
N_CHIPS = 4
MESH = pl.DeviceIdType.MESH
VMEM_LIMIT = 52 * 1024 * 1024
ANY = pl.BlockSpec(memory_space=pl.ANY)

NN = (((1,), (0,)), ((), ()))
NT = (((1,), (1,)), ((), ()))
TN = (((0,), (0,)), ((), ()))


def _cp(sem):
    return pltpu.CompilerParams(dimension_semantics=sem, vmem_limit_bytes=VMEM_LIMIT)


def _sds(shape, dtype):
    return jax.ShapeDtypeStruct(shape, dtype)


def _pick(n, prefs):
    for p in prefs:
        if n % p == 0:
            return p
    return n


def _sigmoid(v):
    return 1.0 / (1.0 + jnp.exp(-v))


def _split3(v):
    hi = v.astype(BF16)
    r1 = v - hi.astype(F32)
    mid = r1.astype(BF16)
    lo = (r1 - mid.astype(F32)).astype(BF16)
    return hi, mid, lo


def _col_of(blk, h):
    lane = lax.broadcasted_iota(jnp.int32, blk.shape, 1)
    return jnp.sum(jnp.where(lane == h, blk, 0.0), axis=1, keepdims=True)


def _put_col(ref, h, col):
    lane = lax.broadcasted_iota(jnp.int32, ref.shape, 1)
    ref[...] = jnp.where(lane == h, col, ref[...])


def _mm(name, mode, a_list, a_specs, b_list, b_specs, pairs, n_acc, grid, tm, tn,
        out_shapes, out_specs, epilogue, extra=(), extra_specs=(), jobs=()):
    n_a, n_b, n_e, n_o = len(a_list), len(b_list), len(extra), len(out_shapes)
    nk = grid[2]
    dn = {"nn": NN, "nt": NT, "tn": TN}[mode]

    def body(*refs):
        a_refs = refs[:n_a]
        b_refs = refs[n_a:n_a + n_b]
        e_refs = refs[n_a + n_b:n_a + n_b + n_e]
        o_refs = refs[n_a + n_b + n_e:n_a + n_b + n_e + n_o]
        acc_refs = refs[n_a + n_b + n_e + n_o:]
        k = pl.program_id(2)

        @pl.when(k == 0)
        def _():
            for acc in acc_refs:
                acc[...] = jnp.zeros_like(acc)

        for ai, bi, ci in pairs:
            acc_refs[ci][...] += lax.dot_general(a_refs[ai][...], b_refs[bi][...], dn,
                                                 preferred_element_type=F32)

        @pl.when(k == nk - 1)
        def _():
            epilogue([acc[...] for acc in acc_refs], e_refs, o_refs)

    return _pcall(
        body, name=name, grid=grid,
        in_specs=list(a_specs) + list(b_specs) + list(extra_specs),
        out_specs=list(out_specs), out_shape=list(out_shapes),
        scratch_shapes=[pltpu.VMEM((tm, tn), F32) for _ in range(n_acc)],
        sem=("parallel", "parallel", "arbitrary"), jobs=jobs,
    )(*a_list, *b_list, *extra)


def _one(res, jobs):
    outs, jouts = res
    return (outs[0], jouts) if jobs else outs[0]


def _store(dtype):
    def ep(accs, e_refs, o_refs):
        o_refs[0][...] = accs[0].astype(dtype)
    return ep


def _mm_nn(name, a, b, out_dtype, *, b_col0=0, n=None, tm=512, tn=None, tk=None, residual=None, jobs=()):
    M, K = a.shape
    if b.ndim == 3:
        Ns = b.shape[2]
        n = b.shape[0] * Ns
        tn = tn or _pick(Ns, (1408, 1024, 512, 256, 128))
        nps = Ns // tn
        b_spec = pl.BlockSpec((None, tk or _pick(K, (1024, 512, 256, 128)), tn),
                              lambda i, j, k: (j // nps, k, j % nps))
    else:
        n = n or b.shape[1]
        tn = tn or _pick(math_gcd(n, b_col0) if b_col0 else n, (2048, 1024, 512, 256, 128))
        assert b_col0 % tn == 0 and n % tn == 0
        c0 = b_col0 // tn
        b_spec = pl.BlockSpec((tk or _pick(K, (1024, 512, 256, 128)), tn), lambda i, j, k: (k, c0 + j))
    tk = tk or _pick(K, (1024, 512, 256, 128))
    tm = _pick(M, (tm, 256, 128))
    grid = (M // tm, n // tn, K // tk)
    a_spec = pl.BlockSpec((tm, tk), lambda i, j, k: (i, k))
    o_spec = pl.BlockSpec((tm, tn), lambda i, j, k: (i, j))
    if residual is None:
        return _one(_mm(name, "nn", [a], [a_spec], [b], [b_spec], [(0, 0, 0)], 1, grid, tm, tn,
                        [_sds((M, n), out_dtype)], [o_spec], _store(out_dtype), jobs=jobs), jobs)

    def ep(accs, e_refs, o_refs):
        o_refs[0][...] = (e_refs[0][...] + accs[0]).astype(out_dtype)
    return _one(_mm(name, "nn", [a], [a_spec], [b], [b_spec], [(0, 0, 0)], 1, grid, tm, tn,
                    [_sds((M, n), out_dtype)], [o_spec], ep, extra=[residual], extra_specs=[o_spec], jobs=jobs), jobs)


def _mm_nt(name, a_list, b_list, out_dtype, *, k0_list=None, tm=512, tn=None, tk=None, residual=None, jobs=()):
    M, K = a_list[0].shape
    b0 = b_list[0]
    N = b0.shape[1] if b0.ndim == 3 else b0.shape[0]
    tm = _pick(M, (tm, 256, 128))
    tn = tn or _pick(N, (1024, 512, 256, 128))
    if b0.ndim == 3:
        Ks = b0.shape[2]
        tk = tk or _pick(Ks, (1408, 1024, 512, 256, 128))
        kps = Ks // tk
        b_specs = [pl.BlockSpec((None, tn, tk), lambda i, j, k: (k // kps, j, k % kps)) for _ in b_list]
    else:
        tk = tk or _pick(K, (1024, 896, 512, 256, 128))
        k0_list = k0_list or [0] * len(b_list)
        b_specs = []
        for k0 in k0_list:
            assert k0 % tk == 0
            b_specs.append(pl.BlockSpec((tn, tk), functools.partial(lambda i, j, k, c: (j, c + k), c=k0 // tk)))
    grid = (M // tm, N // tn, K // tk)
    a_specs = [pl.BlockSpec((tm, tk), lambda i, j, k: (i, k)) for _ in a_list]
    o_spec = pl.BlockSpec((tm, tn), lambda i, j, k: (i, j))
    pairs = [(p, p, 0) for p in range(len(a_list))]
    if residual is None:
        return _one(_mm(name, "nt", a_list, a_specs, b_list, b_specs, pairs, 1, grid, tm, tn,
                        [_sds((M, N), out_dtype)], [o_spec], _store(out_dtype), jobs=jobs), jobs)

    def ep(accs, e_refs, o_refs):
        o_refs[0][...] = (e_refs[0][...] + accs[0]).astype(out_dtype)
    return _one(_mm(name, "nt", a_list, a_specs, b_list, b_specs, pairs, 1, grid, tm, tn,
                    [_sds((M, N), out_dtype)], [o_spec], ep, extra=[residual], extra_specs=[o_spec], jobs=jobs), jobs)


def _mm_tn(name, a, b, out_dtype, *, slots=None, tm=None, tn=None, tk=1024, jobs=()):
    Kc, Mo = a.shape
    No = b.shape[1]
    tm = tm or _pick(Mo, (1024, 704, 512, 256, 128))
    tk = _pick(Kc, (tk, 256, 128))
    if slots:
        Ns = No // slots
        tn = tn or _pick(Ns, (1408, 1024, 512, 256, 128))
        nps = Ns // tn
        o_spec = pl.BlockSpec((None, tm, tn), lambda i, j, k: (j // nps, i, j % nps))
        o_shape = _sds((slots, Mo, Ns), out_dtype)
    else:
        tn = tn or _pick(No, (1024, 512, 256, 128))
        o_spec = pl.BlockSpec((tm, tn), lambda i, j, k: (i, j))
        o_shape = _sds((Mo, No), out_dtype)
    grid = (Mo // tm, No // tn, Kc // tk)
    a_spec = pl.BlockSpec((tk, tm), lambda i, j, k: (k, i))
    b_spec = pl.BlockSpec((tk, tn), lambda i, j, k: (k, j))
    return _one(_mm(name, "tn", [a], [a_spec], [b], [b_spec], [(0, 0, 0)], 1, grid, tm, tn,
                    [o_shape], [o_spec], _store(out_dtype), jobs=jobs), jobs)


def _cast_bf16(name, w, chip):
    R, C = w.shape
    tr = _pick(R, (256, 128, 64, 32, 16))

    def body(k_ref, w_ref, o_ref):
        o_ref[...] = w_ref[...].astype(BF16)

    gs = pltpu.PrefetchScalarGridSpec(
        num_scalar_prefetch=1, grid=(R // tr,),
        in_specs=[pl.BlockSpec((tr, C), lambda i, k: (i, 0))],
        out_specs=pl.BlockSpec((None, tr, C), lambda i, k: (k[0], i, 0)))
    return pl.pallas_call(body, name=name, grid_spec=gs, out_shape=_sds((N_CHIPS, R, C), BF16),
                          compiler_params=_cp(("parallel",)))(jnp.reshape(chip, (1,)).astype(jnp.int32), w)


def _rms_fwd(name, x, g):
    T, D = x.shape
    tr = _pick(T, (256, 128))

    def body(x_ref, g_ref, h_ref, r_ref):
        xv = x_ref[...]
        r = lax.rsqrt(jnp.mean(xv * xv, axis=1, keepdims=True) + RMS_EPS)
        h_ref[...] = (xv * r * g_ref[...]).astype(BF16)
        r_ref[...] = r

    row = pl.BlockSpec((tr, D), lambda i: (i, 0))
    return pl.pallas_call(
        body, name=name, grid=(T // tr,),
        in_specs=[row, pl.BlockSpec((1, D), lambda i: (0, 0))],
        out_specs=[row, pl.BlockSpec((tr, 1), lambda i: (i, 0))],
        out_shape=[_sds((T, D), BF16), _sds((T, 1), F32)], compiler_params=_cp(("parallel",)),
    )(x, g)


def _final_loss_bwd(name, x2, tgt, g):
    T, D = x2.shape
    tr = _pick(T, (256, 128))

    def body(x_ref, t_ref, g_ref, dx_ref, dxb_ref, loss_ref, gg_ref):
        @pl.when(pl.program_id(0) == 0)
        def _():
            loss_ref[...] = jnp.zeros_like(loss_ref)
            gg_ref[...] = jnp.zeros_like(gg_ref)

        xv = x_ref[...]
        gv = g_ref[...]
        r = lax.rsqrt(jnp.mean(xv * xv, axis=1, keepdims=True) + RMS_EPS)
        n = xv * r
        e = n * gv - t_ref[...]
        loss_ref[...] += 0.5 * jnp.sum(jnp.mean(e * e, axis=1, keepdims=True), axis=0, keepdims=True)
        dy = e * (1.0 / D)
        gg_ref[...] += jnp.sum(dy * n, axis=0, keepdims=True)
        gy = dy * gv
        dx = r * (gy - n * jnp.mean(gy * n, axis=1, keepdims=True))
        dx_ref[...] = dx
        dxb_ref[...] = dx.astype(BF16)

    row = pl.BlockSpec((tr, D), lambda i: (i, 0))
    vec = pl.BlockSpec((1, D), lambda i: (0, 0))
    return pl.pallas_call(
        body, name=name, grid=(T // tr,),
        in_specs=[row, row, vec],
        out_specs=[row, row, pl.BlockSpec((1, LANE), lambda i: (0, 0)), vec],
        out_shape=[_sds((T, D), F32), _sds((T, D), BF16), _sds((1, LANE), F32), _sds((1, D), F32)],
        compiler_params=_cp(("arbitrary",)),
    )(x2, tgt, g)


def _rms_bwd(name, dh_list, x, r, g, dres, want_bf16):
    T, D = x.shape
    tr = _pick(T, (128,))
    n_dh = len(dh_list)

    def body(*refs):
        dh_refs = refs[:n_dh]
        x_ref, r_ref, g_ref, dres_ref = refs[n_dh:n_dh + 4]
        outs = refs[n_dh + 4:]
        gg_ref = outs[-1]

        @pl.when(pl.program_id(0) == 0)
        def _():
            gg_ref[...] = jnp.zeros_like(gg_ref)

        dh = dh_refs[0][...]
        for ref in dh_refs[1:]:
            dh = dh + ref[...]
        rv = r_ref[...]
        n = x_ref[...] * rv
        gg_ref[...] += jnp.sum(dh * n, axis=0, keepdims=True)
        gy = dh * g_ref[...]
        dx = dres_ref[...] + rv * (gy - n * jnp.mean(gy * n, axis=1, keepdims=True))
        outs[0][...] = dx
        if want_bf16:
            outs[1][...] = dx.astype(BF16)

    row = pl.BlockSpec((tr, D), lambda i: (i, 0))
    vec = pl.BlockSpec((1, D), lambda i: (0, 0))
    out_specs = [row] + ([row] if want_bf16 else []) + [vec]
    out_shape = [_sds((T, D), F32)] + ([_sds((T, D), BF16)] if want_bf16 else []) + [_sds((1, D), F32)]
    return pl.pallas_call(
        body, name=name, grid=(T // tr,),
        in_specs=[row] * n_dh + [row, pl.BlockSpec((tr, 1), lambda i: (i, 0)), vec, row],
        out_specs=out_specs, out_shape=out_shape, compiler_params=_cp(("arbitrary",)),
    )(*dh_list, x, r, g, dres)


def _merge_fwd(name, gates, u_a, u_b, b_gate):
    T, D = u_a.shape
    tr = _pick(T, (256, 128))

    def body(ga_ref, gb_ref, ua_ref, ub_ref, ba_ref, bb_ref, o_ref):
        sa = _sigmoid(ga_ref[...] + ba_ref[...])
        sb = _sigmoid(gb_ref[...] + bb_ref[...])
        o_ref[...] = (sa * ua_ref[...] + sb * ub_ref[...]).astype(BF16)

    row = pl.BlockSpec((tr, D), lambda i: (i, 0))
    row1 = pl.BlockSpec((tr, D), lambda i: (i, 1))
    v0 = pl.BlockSpec((1, D), lambda i: (0, 0))
    v1 = pl.BlockSpec((1, D), lambda i: (0, 1))
    return pl.pallas_call(
        body, name=name, grid=(T // tr,),
        in_specs=[row, row1, row, row, v0, v1], out_specs=row,
        out_shape=_sds((T, D), BF16), compiler_params=_cp(("parallel",)),
    )(gates, gates, u_a, u_b, b_gate, b_gate)


def _merge_bwd(name, dm, gates, u_a, u_b, b_gate):
    T, D = u_a.shape
    tr = _pick(T, (128,))

    def body(dm_ref, ga_ref, gb_ref, ua_ref, ub_ref, ba_ref, bb_ref, dua_ref, dub_ref, dga_ref, dgb_ref,
             gba_ref, gbb_ref):
        @pl.when(pl.program_id(0) == 0)
        def _():
            gba_ref[...] = jnp.zeros_like(gba_ref)
            gbb_ref[...] = jnp.zeros_like(gbb_ref)

        d = dm_ref[...]
        sa = _sigmoid(ga_ref[...] + ba_ref[...])
        sb = _sigmoid(gb_ref[...] + bb_ref[...])
        dua_ref[...] = (d * sa).astype(BF16)
        dub_ref[...] = (d * sb).astype(BF16)
        dga = d * ua_ref[...] * sa * (1.0 - sa)
        dgb = d * ub_ref[...] * sb * (1.0 - sb)
        dga_ref[...] = dga.astype(BF16)
        dgb_ref[...] = dgb.astype(BF16)
        gba_ref[...] += jnp.sum(dga, axis=0, keepdims=True)
        gbb_ref[...] += jnp.sum(dgb, axis=0, keepdims=True)

    row = pl.BlockSpec((tr, D), lambda i: (i, 0))
    row1 = pl.BlockSpec((tr, D), lambda i: (i, 1))
    v0 = pl.BlockSpec((1, D), lambda i: (0, 0))
    v1 = pl.BlockSpec((1, D), lambda i: (0, 1))
    outs = pl.pallas_call(
        body, name=name, grid=(T // tr,),
        in_specs=[row, row, row1, row, row, v0, v1],
        out_specs=[row, row, row, row, v0, v0],
        out_shape=[_sds((T, D), BF16), _sds((T, D), BF16), _sds((T, D), BF16), _sds((T, D), BF16),
                   _sds((1, D), F32), _sds((1, D), F32)],
        compiler_params=_cp(("arbitrary",)),
    )(dm, gates, gates, u_a, u_b, b_gate, b_gate)
    return outs


def _adamw(name, w, g, m, v, jobs=()):
    R, C = w.shape
    tr = _pick(R, (64, 32, 16, 8))
    c1 = 1.0 - ADAM_B1 ** ADAM_STEP
    c2 = 1.0 - ADAM_B2 ** ADAM_STEP

    def body(w_ref, g_ref, m_ref, v_ref, d_ref, mo_ref, vo_ref):
        gv = g_ref[...]
        mn = ADAM_B1 * m_ref[...] + (1.0 - ADAM_B1) * gv
        vn = ADAM_B2 * v_ref[...] + (1.0 - ADAM_B2) * (gv * gv)
        d_ref[...] = -ADAM_LR * ((mn / c1) / (jnp.sqrt(vn / c2) + ADAM_EPS) + ADAM_WD * w_ref[...])
        mo_ref[...] = mn
        vo_ref[...] = vn

    blk = pl.BlockSpec((tr, C), lambda i: (i, 0))
    return _pcall(
        body, name=name, grid=(R // tr,),
        in_specs=[blk] * 4, out_specs=[blk] * 3,
        out_shape=[_sds((R, C), F32)] * 3, sem=("parallel",), jobs=jobs,
    )(w, g, m, v)


def _add_bf16(name, a, a_row0, b):
    S, h, C = b.shape
    tr = _pick(h, (256, 128, 64, 32, 16))
    nb = h // tr

    def body(off_ref, a_ref, b_ref, o_ref):
        o_ref[...] = (a_ref[...].astype(F32) + b_ref[...].astype(F32)).astype(BF16)

    gs = pltpu.PrefetchScalarGridSpec(
        num_scalar_prefetch=1, grid=(S, nb),
        in_specs=[pl.BlockSpec((None, tr, C), lambda s, i, off: (s, off[0] * nb + i, 0)),
                  pl.BlockSpec((None, tr, C), lambda s, i, off: (s, i, 0))],
        out_specs=pl.BlockSpec((None, tr, C), lambda s, i, off: (s, i, 0)))
    return pl.pallas_call(body, name=name, grid_spec=gs, out_shape=_sds((S, h, C), BF16),
                          compiler_params=_cp(("parallel", "parallel")))(
        jnp.reshape(a_row0, (1,)).astype(jnp.int32), a, b)


def _sum4(name, got, mine, chip, core):
    S, h, C = got.shape
    tr = _pick(h, (256, 128, 64, 32, 16))
    nb = h // tr

    def body(chip_ref, core_ref, m_ref, g_ref, o_ref):
        acc = m_ref[...].astype(F32)
        for s in range(S):
            acc = acc + g_ref[s].astype(F32)
        o_ref[...] = acc

    gs = pltpu.PrefetchScalarGridSpec(
        num_scalar_prefetch=2, grid=(nb,),
        in_specs=[pl.BlockSpec((None, tr, C), lambda i, kc, cc: (kc[0], i, 0)),
                  pl.BlockSpec((S, tr, C), lambda i, kc, cc: (0, i, 0))],
        out_specs=pl.BlockSpec((tr, C), lambda i, kc, cc: (cc[0] * nb + i, 0)))
    return pl.pallas_call(body, name=name, grid_spec=gs, out_shape=_sds((2 * h, C), F32),
                          compiler_params=_cp(("parallel",)))(
        jnp.reshape(chip, (1,)).astype(jnp.int32), jnp.reshape(core, (1,)).astype(jnp.int32), mine, got)


def _place():
    x, y, c = lax.axis_index("x"), lax.axis_index("y"), lax.axis_index("c")
    chips = [(1 - x, y), (x, 1 - y), (1 - x, 1 - y)]
    return x, y, c, chips


def _allgather(name, shards):
    n = len(shards)
    NS = 7

    def body(*refs):
        out_refs = refs[n:2 * n]
        ss, rs = refs[2 * n:]
        x, y, c, _ = _place()
        k, kx, ky, kd = 2 * x + y, 2 * (1 - x) + y, 2 * x + (1 - y), 2 * (1 - x) + (1 - y)
        across_x, across_y, sibling = (1 - x, y, c), (x, 1 - y, c), (x, y, 1 - c)
        sends = []

        def go(cp):
            cp.start()
            sends.append(cp)

        for a, out in enumerate(out_refs):
            h = out.shape[1] // 2
            q = h // 2
            half = lambda slot, cc=c: out.at[slot, pl.ds(cc * h, h), :]
            part0 = lambda slot: out.at[slot, pl.ds(c * h, q), :]
            part1 = lambda slot: out.at[slot, pl.ds(c * h + q, q), :]
            b = NS * a
            go(_rdma(half(k), half(k), ss, rs, b + 0, across_x))
            go(_rdma(half(k), half(k), ss, rs, b + 1, across_y))
            _rdma(half(kx), half(kx), ss, rs, b + 0, across_x).wait_recv()
            go(_rdma(part0(kx), part0(kx), ss, rs, b + 2, across_y))
            go(_rdma(half(kx), half(kx), ss, rs, b + 4, sibling))
            _rdma(half(ky), half(ky), ss, rs, b + 1, across_y).wait_recv()
            go(_rdma(part1(ky), part1(ky), ss, rs, b + 3, across_x))
            go(_rdma(half(ky), half(ky), ss, rs, b + 5, sibling))
            _rdma(part0(kd), part0(kd), ss, rs, b + 2, across_y).wait_recv()
            _rdma(part1(kd), part1(kd), ss, rs, b + 3, across_x).wait_recv()
            go(_rdma(half(kd), half(kd), ss, rs, b + 6, sibling))
        for a, out in enumerate(out_refs):
            h = out.shape[1] // 2
            for j, slot in enumerate((kx, ky, kd)):
                rows = out.at[slot, pl.ds((1 - c) * h, h), :]
                _rdma(rows, rows, ss, rs, NS * a + 4 + j, sibling).wait_recv()
        for cp in sends:
            cp.wait_send()

    return pl.pallas_call(
        body, name=name,
        in_specs=[ANY] * n, out_specs=[ANY] * n,
        out_shape=[_sds(s.shape, s.dtype) for s in shards],
        input_output_aliases={a: a for a in range(n)},
        scratch_shapes=[pltpu.SemaphoreType.DMA((NS * n,)), pltpu.SemaphoreType.DMA((NS * n,))],
    )(*shards)


class _Job:
    def __init__(self, ins, out_shapes, aliases, n_sems, start, finish):
        self.ins, self.out_shapes, self.aliases, self.n_sems = list(ins), list(out_shapes), dict(aliases), n_sems
        self.start, self.finish = start, finish


def _rdma(src, dst, ss, rs, idx, to):
    return pltpu.make_async_remote_copy(src_ref=src, dst_ref=dst, send_sem=ss.at[idx], recv_sem=rs.at[idx],
                                        device_id=to, device_id_type=MESH)


def _job_gather_ici(bufs, part=(0, 1)):
    pi, pn = part

    def descs(outs, ss, rs, incoming):
        x, y, c, chips = _place()
        res = []
        for a, out in enumerate(outs):
            h = out.shape[1] // 2
            hp = h // pn
            for j, (cx, cy) in enumerate(chips):
                rows = out.at[(2 * cx + cy) if incoming else (2 * x + y), pl.ds(c * h + pi * hp, hp), :]
                res.append(_rdma(rows, rows, ss, rs, 3 * a + j, (cx, cy, c)))
        return res

    def start(ins, outs, ss, rs):
        for d in descs(outs, ss, rs, False):
            d.start()

    def finish(ins, outs, ss, rs):
        for d in descs(outs, ss, rs, True):
            d.wait_recv()
        for d in descs(outs, ss, rs, False):
            d.wait_send()

    return _Job(bufs, [_sds(b.shape, b.dtype) for b in bufs], {a: a for a in range(len(bufs))}, 3 * len(bufs),
                start, finish)


def _job_gather_d2d(bufs):
    def descs(outs, ss, rs, incoming):
        x, y, c, chips = _place()
        res = []
        for a, out in enumerate(outs):
            h = out.shape[1] // 2
            for j, (cx, cy) in enumerate(chips):
                rows = out.at[2 * cx + cy, pl.ds(((1 - c) if incoming else c) * h, h), :]
                res.append(_rdma(rows, rows, ss, rs, 3 * a + j, (x, y, 1 - c)))
        return res

    def start(ins, outs, ss, rs):
        for d in descs(outs, ss, rs, False):
            d.start()

    def finish(ins, outs, ss, rs):
        for d in descs(outs, ss, rs, True):
            d.wait_recv()
        for d in descs(outs, ss, rs, False):
            d.wait_send()

    return _Job(bufs, [_sds(b.shape, b.dtype) for b in bufs], {a: a for a in range(len(bufs))}, 3 * len(bufs),
                start, finish)


def _job_sibling(grads):
    def descs(ins, outs, ss, rs):
        x, y, c, _ = _place()
        res = []
        for a, (g, out) in enumerate(zip(ins, outs)):
            h = g.shape[1] // 2
            res.append(_rdma(g.at[:, pl.ds((1 - c) * h, h), :], out, ss, rs, a, (x, y, 1 - c)))
        return res

    def start(ins, outs, ss, rs):
        for d in descs(ins, outs, ss, rs):
            d.start()

    def finish(ins, outs, ss, rs):
        for d in descs(ins, outs, ss, rs):
            d.wait()

    return _Job(grads, [_sds((g.shape[0], g.shape[1] // 2, g.shape[2]), g.dtype) for g in grads], {}, len(grads),
                start, finish)


def _job_scatter(parts, part=(0, 1), into=None):
    pi, pn = part
    n = len(parts)

    def descs(ins, outs, ss, rs):
        x, y, c, chips = _place()
        res = []
        for a, (p, out) in enumerate(zip(ins[:n], outs)):
            hp = p.shape[1] // pn
            for j, (cx, cy) in enumerate(chips):
                res.append(_rdma(p.at[2 * cx + cy, pl.ds(pi * hp, hp), :], out.at[j, pl.ds(pi * hp, hp), :],
                                 ss, rs, 3 * a + j, (cx, cy, c)))
        return res

    def start(ins, outs, ss, rs):
        for d in descs(ins, outs, ss, rs):
            d.start()

    def finish(ins, outs, ss, rs):
        for d in descs(ins, outs, ss, rs):
            d.wait()

    shapes = [_sds((3,) + p.shape[1:], p.dtype) for p in parts]
    if into is None:
        return _Job(parts, shapes, {}, 3 * n, start, finish)
    return _Job(list(parts) + list(into), shapes, {n + a: a for a in range(n)}, 3 * n, start, finish)


def _job_swap(fulls):
    def descs(outs, ss, rs, incoming):
        x, y, c, _ = _place()
        res = []
        for a, out in enumerate(outs):
            h = out.shape[0] // 2
            rows = out.at[pl.ds(((1 - c) if incoming else c) * h, h), :]
            res.append(_rdma(rows, rows, ss, rs, a, (x, y, 1 - c)))
        return res

    def start(ins, outs, ss, rs):
        for d in descs(outs, ss, rs, False):
            d.start()

    def finish(ins, outs, ss, rs):
        for d in descs(outs, ss, rs, True):
            d.wait_recv()
        for d in descs(outs, ss, rs, False):
            d.wait_send()

    return _Job(fulls, [_sds(f.shape, f.dtype) for f in fulls], {a: a for a in range(len(fulls))}, len(fulls),
                start, finish)


def _pcall(body, *, name, grid, in_specs, out_specs, out_shape, scratch_shapes=(), sem, jobs=()):
    in_specs, out_specs, out_shape = list(in_specs), list(out_specs), list(out_shape)
    scratch = list(scratch_shapes)
    n_in, n_out, n_scr = len(in_specs), len(out_shape), len(scratch)
    if not jobs:
        call = pl.pallas_call(body, name=name, grid=grid, in_specs=in_specs, out_specs=out_specs, out_shape=out_shape,
                              scratch_shapes=scratch, compiler_params=_cp(sem))
        return lambda *args: (call(*args), [])
    jin = sum(len(j.ins) for j in jobs)
    jout = sum(len(j.out_shapes) for j in jobs)
    aliases, pi, po = {}, n_in, n_out
    for j in jobs:
        for ia, oa in j.aliases.items():
            aliases[pi + ia] = po + oa
        pi, po = pi + len(j.ins), po + len(j.out_shapes)

    def wrapped(*refs):
        ins = refs[:n_in]
        jins = refs[n_in:n_in + jin]
        outs = refs[n_in + jin:n_in + jin + n_out]
        jouts = refs[n_in + jin + n_out:n_in + jin + n_out + jout]
        scr = refs[n_in + jin + n_out + jout:n_in + jin + n_out + jout + n_scr]
        sems = refs[n_in + jin + n_out + jout + n_scr:]
        first, last = None, None
        for d, g in enumerate(grid):
            f, l = pl.program_id(d) == 0, pl.program_id(d) == g - 1
            first = f if first is None else jnp.logical_and(first, f)
            last = l if last is None else jnp.logical_and(last, l)

        def each(what):
            pi, po = 0, 0
            for q, j in enumerate(jobs):
                getattr(j, what)(jins[pi:pi + len(j.ins)], jouts[po:po + len(j.out_shapes)], sems[2 * q], sems[2 * q + 1])
                pi, po = pi + len(j.ins), po + len(j.out_shapes)

        @pl.when(first)
        def _():
            each("start")

        body(*ins, *outs, *scr)

        @pl.when(last)
        def _():
            each("finish")

    call = pl.pallas_call(
        wrapped, name=name, grid=grid,
        in_specs=in_specs + [ANY] * jin, out_specs=out_specs + [ANY] * jout,
        out_shape=out_shape + [s for j in jobs for s in j.out_shapes],
        input_output_aliases=aliases,
        scratch_shapes=scratch + [pltpu.SemaphoreType.DMA((j.n_sems,)) for j in jobs for _ in range(2)],
        compiler_params=_cp(("arbitrary",) * len(grid)))

    def run(*args):
        res = call(*args, *[a for j in jobs for a in j.ins])
        main, rest, per_job = list(res[:n_out]), list(res[n_out:]), []
        for j in jobs:
            per_job.append(rest[:len(j.out_shapes)])
            rest = rest[len(j.out_shapes):]
        return main, per_job
    return run


def _comm_only(name, jobs):
    jin = sum(len(j.ins) for j in jobs)
    jout = sum(len(j.out_shapes) for j in jobs)
    aliases, pi, po = {}, 0, 0
    for j in jobs:
        for ia, oa in j.aliases.items():
            aliases[pi + ia] = po + oa
        pi, po = pi + len(j.ins), po + len(j.out_shapes)

    def body(*refs):
        jins, jouts, sems = refs[:jin], refs[jin:jin + jout], refs[jin + jout:]
        for what in ("start", "finish"):
            pi, po = 0, 0
            for q, j in enumerate(jobs):
                getattr(j, what)(jins[pi:pi + len(j.ins)], jouts[po:po + len(j.out_shapes)], sems[2 * q], sems[2 * q + 1])
                pi, po = pi + len(j.ins), po + len(j.out_shapes)

    res = pl.pallas_call(
        body, name=name, in_specs=[ANY] * jin, out_specs=[ANY] * jout,
        out_shape=[s for j in jobs for s in j.out_shapes], input_output_aliases=aliases,
        scratch_shapes=[pltpu.SemaphoreType.DMA((j.n_sems,)) for j in jobs for _ in range(2)],
    )(*[a for j in jobs for a in j.ins])
    rest, per_job = list(res), []
    for j in jobs:
        per_job.append(rest[:len(j.out_shapes)])
        rest = rest[len(j.out_shapes):]
    return per_job


def _small_allreduce(name, v):
    m_per, n = v.shape

    def body(x_ref, sum_ref, all_ref, send_sems, recv_sems, local_sem):
        x, y, c, chips = _place()
        me, sibling = (x, y, c), (x, y, 1 - c)

        def rows(px, py, pc):
            return all_ref.at[pl.ds((4 * px + 2 * py + pc) * m_per, m_per), :]

        def copy(kk, block, to, src=None):
            return pltpu.make_async_remote_copy(
                src_ref=rows(*block) if src is None else src, dst_ref=rows(*block),
                send_sem=send_sems.at[kk], recv_sem=recv_sems.at[kk], device_id=to, device_id_type=MESH)

        mine = pltpu.make_async_copy(x_ref, rows(*me), local_sem)
        mine.start()
        first = [copy(0, me, sibling, src=x_ref)]
        first += [copy(1 + j, me, (*chip, c), src=x_ref) for j, chip in enumerate(chips)]
        for cp in first:
            cp.start()
        passed = [copy(4 + j, (*chip, c), sibling) for j, chip in enumerate(chips)]
        for j, chip in enumerate(chips):
            copy(1 + j, (*chip, c), me).wait_recv()
            passed[j].start()
        copy(0, sibling, me).wait_recv()
        for j, chip in enumerate(chips):
            copy(4 + j, (*chip, 1 - c), me).wait_recv()
        for cp in first + passed:
            cp.wait_send()
        mine.wait()
        acc = all_ref[pl.ds(0, m_per), :]
        for d in range(1, 8):
            acc = acc + all_ref[pl.ds(d * m_per, m_per), :]
        sum_ref[...] = acc

    vm = pl.BlockSpec(memory_space=pltpu.VMEM)
    return pl.pallas_call(
        body, name=name, in_specs=[vm], out_specs=[vm, vm],
        out_shape=[_sds((m_per, n), F32), _sds((8 * m_per, n), F32)],
        scratch_shapes=[pltpu.SemaphoreType.DMA((7,)), pltpu.SemaphoreType.DMA((7,)), pltpu.SemaphoreType.DMA],
    )(v)[0]


def _in_layout(D, W6, nhb, Ls):
    nmain = W6 + 2 * D
    lay = []
    for k in range(N_CHIPS):
        g0, g1 = k * Ls, (k + 1) * Ls
        pieces = []
        a, b = max(g0, 0), min(g1, W6)
        if a < b:
            pieces.append((a - g0, b - g0, a))
        a, b = max(g0, W6 + nhb), min(g1, W6 + nhb + 2 * D)
        if a < b:
            pieces.append((a - g0, b - g0, a - nhb))
        a, b = max(g0, W6), min(g1, W6 + nhb)
        fpiece = (a - g0, b - g0, a - W6) if a < b else None
        assert fpiece is None or (b - a) == nhb
        main0 = min(p[2] for p in pieces)
        main1 = max(p[2] + p[1] - p[0] for p in pieces)
        lay.append(dict(pieces=pieces, f=fpiece, s=main0 // LANE, e=-(-main1 // LANE), main1=main1))
    assert sum(1 for l in lay if l["f"] is not None) == 1
    nbw = max(l["e"] - l["s"] + (1 if l["f"] else 0) for l in lay)
    for k in range(1, N_CHIPS):
        assert lay[k]["s"] >= lay[k - 1]["e"] - 1 and lay[k]["s"] > lay[k - 1]["s"]
    return lay, nbw, nmain


def _to_window(w, lay_k, nbw):
    D = w.shape[0]
    items = [(c0 - lay_k["s"] * LANE, l0, l1) for (l0, l1, c0) in lay_k["pieces"]]
    if lay_k["f"]:
        l0, l1, off = lay_k["f"]
        items.append(((lay_k["e"] - lay_k["s"]) * LANE + off, l0, l1))
    items.sort()
    cols, pos = [], 0
    for w0, l0, l1 in items:
        if w0 > pos:
            cols.append(jnp.zeros((D, w0 - pos), w.dtype))
        cols.append(w[:, l0:l1])
        pos = w0 + (l1 - l0)
    if pos < nbw * LANE:
        cols.append(jnp.zeros((D, nbw * LANE - pos), w.dtype))
    return jnp.concatenate(cols, axis=1)


def _from_window(win, lay_k):
    items = [(l0, c0 - lay_k["s"] * LANE, l1 - l0) for (l0, l1, c0) in lay_k["pieces"]]
    if lay_k["f"]:
        l0, l1, off = lay_k["f"]
        items.append((l0, (lay_k["e"] - lay_k["s"]) * LANE + off, l1 - l0))
    items.sort()
    return jnp.concatenate([win[:, w0:w0 + n] for (_, w0, n) in items], axis=1)


def _assemble_in(name, wins, lay, nbw, nmain):
    _, D, _ = wins.shape
    ncb = nmain // LANE + 1
    k1 = np.zeros(ncb, np.int32)
    i1 = np.zeros(ncb, np.int32)
    k2 = np.zeros(ncb, np.int32)
    i2 = np.zeros(ncb, np.int32)
    fl = np.zeros(ncb, np.int32)
    for b in range(ncb - 1):
        k = max(kk for kk in range(N_CHIPS) if lay[kk]["s"] <= b)
        k1[b], i1[b] = k, b - lay[k]["s"]
        if k >= 1 and b == lay[k]["s"] and lay[k - 1]["main1"] > b * LANE:
            k2[b], i2[b], fl[b] = k - 1, b - lay[k - 1]["s"], 1
    kf = [kk for kk in range(N_CHIPS) if lay[kk]["f"]][0]
    k1[ncb - 1], i1[ncb - 1] = kf, lay[kf]["e"] - lay[kf]["s"]

    def body(k1_ref, i1_ref, k2_ref, i2_ref, fl_ref, a_ref, b_ref, o_ref):
        b = pl.program_id(0)
        add = jnp.where(fl_ref[b] == 1, b_ref[...], jnp.zeros_like(b_ref))
        o_ref[...] = a_ref[...] + add

    gs = pltpu.PrefetchScalarGridSpec(
        num_scalar_prefetch=5, grid=(ncb,),
        in_specs=[pl.BlockSpec((None, D, LANE), lambda b, k1r, i1r, k2r, i2r, flr: (k1r[b], 0, i1r[b])),
                  pl.BlockSpec((None, D, LANE), lambda b, k1r, i1r, k2r, i2r, flr: (k2r[b], 0, i2r[b]))],
        out_specs=pl.BlockSpec((D, LANE), lambda b, k1r, i1r, k2r, i2r, flr: (0, b)))
    return pl.pallas_call(body, name=name, grid_spec=gs, out_shape=_sds((D, ncb * LANE), BF16),
                          compiler_params=_cp(("parallel",)))(
        jnp.asarray(k1), jnp.asarray(i1), jnp.asarray(k2), jnp.asarray(i2), jnp.asarray(fl), wins, wins)


def _hgroup(nh):
    return _pick(nh, (4, 2, 1))


def _a_specs_q(nh, G):
    ngrp = nh // G
    blk = (GROUP, G * HEAD_DIM)
    q = pl.BlockSpec(blk, lambda i, hg: (i, hg))
    ks = [pl.BlockSpec(blk, functools.partial(
        lambda i, hg, j: (jnp.maximum(i - (WIN_BLOCKS - 1) + j, 0), ngrp + hg), j=j)) for j in range(WIN_BLOCKS)]
    vs = [pl.BlockSpec(blk, functools.partial(
        lambda i, hg, j: (jnp.maximum(i - (WIN_BLOCKS - 1) + j, 0), 2 * ngrp + hg), j=j)) for j in range(WIN_BLOCKS)]
    return q, ks, vs


def _a_logits(q, ks, bias, i, scale):
    parts = [lax.dot_general(q, k, NT, preferred_element_type=F32) for k in ks]
    s = jnp.concatenate(parts, axis=1) * scale + bias
    col = lax.broadcasted_iota(jnp.int32, s.shape, 1)
    return jnp.where(col >= (WIN_BLOCKS - 1 - i) * GROUP, s, NEG_INF)


def _attn_a_fwd(name, qkv, bias2, nh, jobs=()):
    T = qkv.shape[0]
    ng = T // GROUP
    G = _hgroup(nh)
    scale = HEAD_DIM ** -0.5

    def body(q_ref, *refs):
        k_refs = refs[:WIN_BLOCKS]
        v_refs = refs[WIN_BLOCKS:2 * WIN_BLOCKS]
        bias_ref, o_ref, lse_ref = refs[2 * WIN_BLOCKS:]
        i, hg = pl.program_id(0), pl.program_id(1)

        @pl.when(hg == 0)
        def _():
            lse_ref[...] = jnp.zeros_like(lse_ref)

        for g in range(G):
            h = hg * G + g
            sl = slice(g * HEAD_DIM, (g + 1) * HEAD_DIM)
            s = _a_logits(q_ref[:, sl], [kr[:, sl] for kr in k_refs], bias_ref[h], i, scale)
            m = jnp.max(s, axis=1, keepdims=True)
            p = jnp.exp(s - m)
            l = jnp.sum(p, axis=1, keepdims=True)
            pb = (p / l).astype(BF16)
            o = jnp.zeros((GROUP, HEAD_DIM), F32)
            for j in range(WIN_BLOCKS):
                o = o + jnp.dot(pb[:, j * GROUP:(j + 1) * GROUP], v_refs[j][:, sl], preferred_element_type=F32)
            o_ref[:, sl] = o.astype(BF16)
            _put_col(lse_ref, h, m + jnp.log(l))

    q_spec, k_specs, v_specs = _a_specs_q(nh, G)
    stat = pl.BlockSpec((GROUP, LANE), lambda i, hg: (i, 0))
    return _pcall(
        body, name=name, grid=(ng, nh // G),
        in_specs=[q_spec] + k_specs + v_specs + [pl.BlockSpec((nh, GROUP, WIN), lambda i, hg: (0, 0, 0))],
        out_specs=[pl.BlockSpec((GROUP, G * HEAD_DIM), lambda i, hg: (i, hg)), stat],
        out_shape=[_sds((T, nh * HEAD_DIM), BF16), _sds((T, LANE), F32)],
        sem=("parallel", "arbitrary"), jobs=jobs,
    )(qkv, *([qkv] * (2 * WIN_BLOCKS)), bias2)


def _attn_a_dq(name, qkv, do, lse, bias2, nh, jobs=()):
    T = qkv.shape[0]
    ng = T // GROUP
    G = _hgroup(nh)
    scale = HEAD_DIM ** -0.5

    def body(q_ref, *refs):
        k_refs = refs[:WIN_BLOCKS]
        v_refs = refs[WIN_BLOCKS:2 * WIN_BLOCKS]
        do_ref, lse_ref, bias_ref, dq_ref, delta_ref, db_ref = refs[2 * WIN_BLOCKS:]
        i, hg = pl.program_id(0), pl.program_id(1)

        @pl.when(hg == 0)
        def _():
            delta_ref[...] = jnp.zeros_like(delta_ref)

        @pl.when(i == 0)
        def _():
            for g in range(G):
                db_ref[hg * G + g] = jnp.zeros((GROUP, WIN), F32)

        for g in range(G):
            h = hg * G + g
            sl = slice(g * HEAD_DIM, (g + 1) * HEAD_DIM)
            ks = [kr[:, sl] for kr in k_refs]
            s = _a_logits(q_ref[:, sl], ks, bias_ref[h], i, scale)
            p = jnp.exp(s - _col_of(lse_ref[...], h))
            dov = do_ref[:, sl]
            dp = jnp.concatenate([lax.dot_general(dov, vr[:, sl], NT, preferred_element_type=F32) for vr in v_refs],
                                 axis=1)
            delta = jnp.sum(p * dp, axis=1, keepdims=True)
            ds = p * (dp - delta)
            db_ref[h] += ds
            dsb = ds.astype(BF16)
            dq = jnp.zeros((GROUP, HEAD_DIM), F32)
            for j in range(WIN_BLOCKS):
                dq = dq + jnp.dot(dsb[:, j * GROUP:(j + 1) * GROUP], ks[j], preferred_element_type=F32)
            dq_ref[:, sl] = (dq * scale).astype(BF16)
            _put_col(delta_ref, h, delta)

    q_spec, k_specs, v_specs = _a_specs_q(nh, G)
    blk = pl.BlockSpec((GROUP, G * HEAD_DIM), lambda i, hg: (i, hg))
    stat = pl.BlockSpec((GROUP, LANE), lambda i, hg: (i, 0))
    full_b = pl.BlockSpec((nh, GROUP, WIN), lambda i, hg: (0, 0, 0))
    return _pcall(
        body, name=name, grid=(ng, nh // G),
        in_specs=[q_spec] + k_specs + v_specs + [blk, stat, full_b],
        out_specs=[blk, stat, full_b],
        out_shape=[_sds((T, nh * HEAD_DIM), BF16), _sds((T, LANE), F32), _sds((nh, GROUP, WIN), F32)],
        sem=("arbitrary", "arbitrary"), jobs=jobs,
    )(qkv, *([qkv] * (2 * WIN_BLOCKS)), do, lse, bias2)


def _attn_a_dkv(name, qkv, do, lse, delta, bias2, nh, jobs=()):
    T = qkv.shape[0]
    ng = T // GROUP
    G = _hgroup(nh)
    ngrp = nh // G
    scale = HEAD_DIM ** -0.5
    nj = WIN_BLOCKS

    def body(k_ref, v_ref, *refs):
        q_refs = refs[:nj]
        do_refs = refs[nj:2 * nj]
        lse_refs = refs[2 * nj:3 * nj]
        dl_refs = refs[3 * nj:4 * nj]
        bias_ref, dk_ref, dv_ref = refs[4 * nj:]
        r, hg = pl.program_id(0), pl.program_id(1)
        for g in range(G):
            h = hg * G + g
            sl = slice(g * HEAD_DIM, (g + 1) * HEAD_DIM)
            kv, vv = k_ref[:, sl], v_ref[:, sl]
            bias = bias_ref[h]
            dk = jnp.zeros((GROUP, HEAD_DIM), F32)
            dv = jnp.zeros((GROUP, HEAD_DIM), F32)
            for j in range(nj):
                qv, dov = q_refs[j][:, sl], do_refs[j][:, sl]
                c0 = (nj - 1 - j) * GROUP
                s = lax.dot_general(qv, kv, NT, preferred_element_type=F32) * scale + bias[:, c0:c0 + GROUP]
                p = jnp.exp(s - _col_of(lse_refs[j][...], h))
                p = jnp.where(r + j <= ng - 1, p, 0.0)
                dp = lax.dot_general(dov, vv, NT, preferred_element_type=F32)
                ds = p * (dp - _col_of(dl_refs[j][...], h))
                dv = dv + lax.dot_general(p.astype(BF16), dov, TN, preferred_element_type=F32)
                dk = dk + lax.dot_general(ds.astype(BF16), qv, TN, preferred_element_type=F32)
            dk_ref[:, sl] = (dk * scale).astype(BF16)
            dv_ref[:, sl] = dv.astype(BF16)

    def qmap(j):
        return functools.partial(lambda r, hg, j: (jnp.minimum(r + j, ng - 1), hg), j=j)

    def smap(j):
        return functools.partial(lambda r, hg, j: (jnp.minimum(r + j, ng - 1), 0), j=j)

    blk = (GROUP, G * HEAD_DIM)
    in_specs = ([pl.BlockSpec(blk, lambda r, hg: (r, ngrp + hg)), pl.BlockSpec(blk, lambda r, hg: (r, 2 * ngrp + hg))]
                + [pl.BlockSpec(blk, qmap(j)) for j in range(nj)]
                + [pl.BlockSpec(blk, qmap(j)) for j in range(nj)]
                + [pl.BlockSpec((GROUP, LANE), smap(j)) for j in range(nj)]
                + [pl.BlockSpec((GROUP, LANE), smap(j)) for j in range(nj)]
                + [pl.BlockSpec((nh, GROUP, WIN), lambda r, hg: (0, 0, 0))])
    out = pl.BlockSpec(blk, lambda r, hg: (r, hg))
    return _pcall(
        body, name=name, grid=(ng, ngrp), in_specs=in_specs, out_specs=[out, out],
        out_shape=[_sds((T, nh * HEAD_DIM), BF16)] * 2,
        sem=("parallel", "parallel"), jobs=jobs,
    )(qkv, qkv, *([qkv] * nj), *([do] * nj), *([lse] * nj), *([delta] * nj), bias2)


def _fox_prep(name, f, b_f):
    T = f.shape[0]
    tb = _pick(T, (256, 128))

    def body(f_ref, b_ref, cum_ref, cumt_ref, carry_ref):
        @pl.when(pl.program_id(0) == 0)
        def _():
            carry_ref[...] = jnp.zeros_like(carry_ref)

        z = f_ref[...] + b_ref[...]
        logf = jnp.minimum(z, 0.0) - jnp.log(1.0 + jnp.exp(-jnp.abs(z)))
        row = lax.broadcasted_iota(jnp.int32, (tb, tb), 0)
        col = lax.broadcasted_iota(jnp.int32, (tb, tb), 1)
        tri = (row >= col).astype(BF16)
        acc = jnp.zeros((tb, LANE), F32)
        for piece in _split3(logf):
            acc = acc + jnp.dot(tri, piece, preferred_element_type=F32)
        cum = acc + carry_ref[...]
        cum_ref[...] = cum
        cumt_ref[...] = cum.T
        carry_ref[...] = cum_ref[pl.ds(tb - 1, 1), :]

    return pl.pallas_call(
        body, name=name, grid=(T // tb,),
        in_specs=[pl.BlockSpec((tb, LANE), lambda i: (i, 0)), pl.BlockSpec((1, LANE), lambda i: (0, 0))],
        out_specs=[pl.BlockSpec((tb, LANE), lambda i: (i, 0)), pl.BlockSpec((LANE, tb), lambda i: (0, i))],
        out_shape=[_sds((T, LANE), F32), _sds((LANE, T), F32)],
        scratch_shapes=[pltpu.VMEM((1, LANE), F32)],
        compiler_params=_cp(("arbitrary",)),
    )(f, b_f)


def _fox_blk(T):
    return _pick(T, (256, 128))


def _fox_group(nh):
    return _pick(nh, (4, 2, 1))


def _fox_allowed(i, j, tq, tk):
    diff = lax.broadcasted_iota(jnp.int32, (tq, tk), 1) - lax.broadcasted_iota(jnp.int32, (tq, tk), 0)
    return diff <= (i - j) * tq


def _fox_fwd(name, qkv, cum, cumt, nh, jobs=()):
    T = qkv.shape[0]
    tq = tk = _fox_blk(T)
    G = _fox_group(nh)
    ngrp = nh // G
    scale = HEAD_DIM ** -0.5

    def body(q_ref, k_ref, v_ref, cum_ref, cumt_ref, o_ref, lse_ref):
        i, hg = pl.program_id(0), pl.program_id(1)

        @pl.when(hg == 0)
        def _():
            lse_ref[...] = jnp.zeros_like(lse_ref)

        sls = [slice(g * HEAD_DIM, (g + 1) * HEAD_DIM) for g in range(G)]
        qs = [q_ref[:, sl] for sl in sls]
        cqs = [_col_of(cum_ref[...], hg * G + g) for g in range(G)]

        def step(j, carry):
            k0 = pl.multiple_of(j * tk, tk)
            ok = _fox_allowed(i, j, tq, tk)
            out = []
            for g in range(G):
                m, l, acc = carry[g]
                kj = k_ref[pl.ds(k0, tk), sls[g]]
                vj = v_ref[pl.ds(k0, tk), sls[g]]
                ck = cumt_ref[pl.ds(hg * G + g, 1), pl.ds(k0, tk)]
                s = lax.dot_general(qs[g], kj, NT, preferred_element_type=F32) * scale + (cqs[g] - ck)
                s = jnp.where(ok, s, NEG_INF)
                m_new = jnp.maximum(m, jnp.max(s, axis=1, keepdims=True))
                alpha = jnp.exp(m - m_new)
                p = jnp.exp(s - m_new)
                l = alpha * l + jnp.sum(p, axis=1, keepdims=True)
                acc = alpha * acc + jnp.dot(p.astype(BF16), vj, preferred_element_type=F32)
                out.append((m_new, l, acc))
            return tuple(out)

        one = (jnp.full((tq, 1), NEG_INF, F32), jnp.zeros((tq, 1), F32), jnp.zeros((tq, HEAD_DIM), F32))
        res = lax.fori_loop(0, i + 1, step, tuple(one for _ in range(G)))
        for g in range(G):
            m, l, acc = res[g]
            o_ref[:, sls[g]] = (acc / l).astype(BF16)
            _put_col(lse_ref, hg * G + g, m + jnp.log(l))

    GW = G * HEAD_DIM
    return _pcall(
        body, name=name, grid=(T // tq, ngrp),
        in_specs=[pl.BlockSpec((tq, GW), lambda i, hg: (i, 3 * ngrp + hg)),
                  pl.BlockSpec((T, GW), lambda i, hg: (0, 4 * ngrp + hg)),
                  pl.BlockSpec((T, GW), lambda i, hg: (0, 5 * ngrp + hg)),
                  pl.BlockSpec((tq, LANE), lambda i, hg: (i, 0)),
                  pl.BlockSpec((LANE, T), lambda i, hg: (0, 0))],
        out_specs=[pl.BlockSpec((tq, GW), lambda i, hg: (i, hg)), pl.BlockSpec((tq, LANE), lambda i, hg: (i, 0))],
        out_shape=[_sds((T, nh * HEAD_DIM), BF16), _sds((T, LANE), F32)],
        sem=("parallel", "arbitrary"), jobs=jobs,
    )(qkv, qkv, qkv, cum, cumt)


def _fox_dq(name, qkv, do, lse, cum, cumt, nh, jobs=()):
    T = qkv.shape[0]
    tq = tk = _fox_blk(T)
    G = _fox_group(nh)
    ngrp = nh // G
    GW = G * HEAD_DIM
    scale = HEAD_DIM ** -0.5

    def body(q_ref, k_ref, v_ref, do_ref, lse_ref, cum_ref, cumt_ref, dq_ref, delta_ref):
        i, hg = pl.program_id(0), pl.program_id(1)

        @pl.when(hg == 0)
        def _():
            delta_ref[...] = jnp.zeros_like(delta_ref)

        sls = [slice(g * HEAD_DIM, (g + 1) * HEAD_DIM) for g in range(G)]
        qs = [q_ref[:, sl] for sl in sls]
        dos = [do_ref[:, sl] for sl in sls]
        cqs = [_col_of(cum_ref[...], hg * G + g) for g in range(G)]
        lses = [_col_of(lse_ref[...], hg * G + g) for g in range(G)]

        def p_dp(j, g, ok):
            k0 = pl.multiple_of(j * tk, tk)
            kj = k_ref[pl.ds(k0, tk), sls[g]]
            vj = v_ref[pl.ds(k0, tk), sls[g]]
            ck = cumt_ref[pl.ds(hg * G + g, 1), pl.ds(k0, tk)]
            s = lax.dot_general(qs[g], kj, NT, preferred_element_type=F32) * scale + (cqs[g] - ck)
            p = jnp.exp(jnp.where(ok, s, NEG_INF) - lses[g])
            return p, lax.dot_general(dos[g], vj, NT, preferred_element_type=F32), kj

        def sweep_delta(j, deltas):
            ok = _fox_allowed(i, j, tq, tk)
            out = []
            for g in range(G):
                p, dp, _ = p_dp(j, g, ok)
                out.append(deltas[g] + jnp.sum(p * dp, axis=1, keepdims=True))
            return tuple(out)

        deltas = lax.fori_loop(0, i + 1, sweep_delta, tuple(jnp.zeros((tq, 1), F32) for _ in range(G)))

        def sweep_dq(j, dqs):
            ok = _fox_allowed(i, j, tq, tk)
            out = []
            for g in range(G):
                p, dp, kj = p_dp(j, g, ok)
                ds = p * (dp - deltas[g])
                out.append(dqs[g] + jnp.dot(ds.astype(BF16), kj, preferred_element_type=F32))
            return tuple(out)

        dqs = lax.fori_loop(0, i + 1, sweep_dq, tuple(jnp.zeros((tq, HEAD_DIM), F32) for _ in range(G)))
        for g in range(G):
            dq_ref[:, sls[g]] = (dqs[g] * scale).astype(BF16)
            _put_col(delta_ref, hg * G + g, deltas[g])

    blk = pl.BlockSpec((tq, GW), lambda i, hg: (i, hg))
    stat = pl.BlockSpec((tq, LANE), lambda i, hg: (i, 0))
    return _pcall(
        body, name=name, grid=(T // tq, ngrp),
        in_specs=[pl.BlockSpec((tq, GW), lambda i, hg: (i, 3 * ngrp + hg)),
                  pl.BlockSpec((T, GW), lambda i, hg: (0, 4 * ngrp + hg)),
                  pl.BlockSpec((T, GW), lambda i, hg: (0, 5 * ngrp + hg)),
                  blk, stat, stat, pl.BlockSpec((LANE, T), lambda i, hg: (0, 0))],
        out_specs=[blk, stat],
        out_shape=[_sds((T, nh * HEAD_DIM), BF16), _sds((T, LANE), F32)],
        sem=("parallel", "arbitrary"), jobs=jobs,
    )(qkv, qkv, qkv, do, lse, cum, cumt)


def _fox_dkv(name, qkv, do, lse, delta, cum, cumt, nh, jobs=()):
    T = qkv.shape[0]
    tq = tk = _fox_blk(T)
    nq = T // tq
    G = _fox_group(nh)
    ngrp = nh // G
    GW = G * HEAD_DIM
    scale = HEAD_DIM ** -0.5

    def body(k_ref, v_ref, q_ref, do_ref, lse_ref, dl_ref, cum_ref, cumt_ref, dk_ref, dv_ref, dc_ref):
        j, hg = pl.program_id(0), pl.program_id(1)

        @pl.when(hg == 0)
        def _():
            dc_ref[...] = jnp.zeros_like(dc_ref)

        sls = [slice(g * HEAD_DIM, (g + 1) * HEAD_DIM) for g in range(G)]
        kjs = [k_ref[:, sl] for sl in sls]
        vjs = [v_ref[:, sl] for sl in sls]
        k0 = pl.multiple_of(j * tk, tk)
        cks = [cumt_ref[pl.ds(hg * G + g, 1), pl.ds(k0, tk)] for g in range(G)]

        def step(i, carry):
            q0 = pl.multiple_of(i * tq, tq)
            ok = _fox_allowed(i, j, tq, tk)
            cum_i, lse_i, dl_i = cum_ref[pl.ds(q0, tq), :], lse_ref[pl.ds(q0, tq), :], dl_ref[pl.ds(q0, tq), :]
            out = []
            for g in range(G):
                dk, dv, dc = carry[g]
                h = hg * G + g
                qi = q_ref[pl.ds(q0, tq), sls[g]]
                doi = do_ref[pl.ds(q0, tq), sls[g]]
                s = lax.dot_general(qi, kjs[g], NT, preferred_element_type=F32) * scale + (_col_of(cum_i, h) - cks[g])
                p = jnp.exp(jnp.where(ok, s, NEG_INF) - _col_of(lse_i, h))
                dp = lax.dot_general(doi, vjs[g], NT, preferred_element_type=F32)
                ds = p * (dp - _col_of(dl_i, h))
                dv = dv + lax.dot_general(p.astype(BF16), doi, TN, preferred_element_type=F32)
                dk = dk + lax.dot_general(ds.astype(BF16), qi, TN, preferred_element_type=F32)
                dc = dc - jnp.sum(ds, axis=0, keepdims=True)
                out.append((dk, dv, dc))
            return tuple(out)

        one = (jnp.zeros((tk, HEAD_DIM), F32), jnp.zeros((tk, HEAD_DIM), F32), jnp.zeros((1, tk), F32))
        res = lax.fori_loop(j, nq, step, tuple(one for _ in range(G)))
        sub = lax.broadcasted_iota(jnp.int32, (LANE, tk), 0)
        dc_all = dc_ref[...]
        for g in range(G):
            dk, dv, dc = res[g]
            dk_ref[:, sls[g]] = (dk * scale).astype(BF16)
            dv_ref[:, sls[g]] = dv.astype(BF16)
            dc_all = jnp.where(sub == hg * G + g, dc, dc_all)
        dc_ref[...] = dc_all

    whole = lambda c: pl.BlockSpec((T, GW), c)
    stat = pl.BlockSpec((T, LANE), lambda j, hg: (0, 0))
    out = pl.BlockSpec((tk, GW), lambda j, hg: (j, hg))
    return _pcall(
        body, name=name, grid=(T // tk, ngrp),
        in_specs=[pl.BlockSpec((tk, GW), lambda j, hg: (j, 4 * ngrp + hg)),
                  pl.BlockSpec((tk, GW), lambda j, hg: (j, 5 * ngrp + hg)),
                  whole(lambda j, hg: (0, 3 * ngrp + hg)), whole(lambda j, hg: (0, hg)),
                  stat, stat, stat, pl.BlockSpec((LANE, T), lambda j, hg: (0, 0))],
        out_specs=[out, out, pl.BlockSpec((LANE, tk), lambda j, hg: (0, j))],
        out_shape=[_sds((T, nh * HEAD_DIM), BF16)] * 2 + [_sds((LANE, T), F32)],
        sem=("parallel", "arbitrary"), jobs=jobs,
    )(qkv, qkv, qkv, do, lse, delta, cum, cumt)


def _fox_post(name, dcumt, f, b_f):
    T = f.shape[0]
    tb = _pick(T, (256, 128))
    nb = T // tb

    def body(dc_ref, f_ref, b_ref, df_ref, gb_ref, carry_ref):
        @pl.when(pl.program_id(0) == 0)
        def _():
            carry_ref[...] = jnp.zeros_like(carry_ref)
            gb_ref[...] = jnp.zeros_like(gb_ref)

        dc = dc_ref[...]
        row = lax.broadcasted_iota(jnp.int32, (tb, tb), 0)
        col = lax.broadcasted_iota(jnp.int32, (tb, tb), 1)
        tri = (row >= col).astype(BF16)
        acc = jnp.zeros((LANE, tb), F32)
        for piece in _split3(dc):
            acc = acc + jnp.dot(piece, tri, preferred_element_type=F32)
        dlogf = (acc + carry_ref[...]).T
        carry_ref[...] += jnp.sum(dc, axis=1, keepdims=True)
        z = f_ref[...] + b_ref[...]
        df = dlogf * _sigmoid(-z)
        df_ref[...] = df.astype(BF16)
        gb_ref[...] += jnp.sum(df, axis=0, keepdims=True)

    return pl.pallas_call(
        body, name=name, grid=(nb,),
        in_specs=[pl.BlockSpec((LANE, tb), lambda g: (0, nb - 1 - g)),
                  pl.BlockSpec((tb, LANE), lambda g: (nb - 1 - g, 0)),
                  pl.BlockSpec((1, LANE), lambda g: (0, 0))],
        out_specs=[pl.BlockSpec((tb, LANE), lambda g: (nb - 1 - g, 0)), pl.BlockSpec((1, LANE), lambda g: (0, 0))],
        out_shape=[_sds((T, LANE), BF16), _sds((1, LANE), F32)],
        scratch_shapes=[pltpu.VMEM((LANE, 1), F32)],
        compiler_params=_cp(("arbitrary",)),
    )(dcumt, f, b_f)


def _rel_tables(n_rel):
    max_rel = (n_rel - 1) // 2
    nj = GROUP + WIN - 1
    onehot = np.zeros((n_rel, nj), np.float32)
    for j in range(nj):
        dist = (WIN - 1) - j
        onehot[int(np.clip(dist, -max_rel, max_rel)) + max_rel, j] = 1.0
    a = np.arange(GROUP)[:, None]
    kb = np.arange(WIN)[None, :]
    lo = CHUNK * (a // CHUNK)
    inband = (kb >= lo) & (kb < lo + BAND)
    return onehot, inband


def _bias2_of(rel_bias, onehot, inband):
    bv = jnp.dot(rel_bias, jnp.asarray(onehot), precision=lax.Precision.HIGHEST)
    rows = [bv[:, GROUP - 1 - a:GROUP - 1 - a + WIN] for a in range(GROUP)]
    toe = jnp.stack(rows, axis=1)
    return jnp.where(jnp.asarray(inband)[None], toe, NEG_INF)


def _rel_grad_of(dbias2, onehot):
    nj = GROUP + WIN - 1
    dbv = sum(jnp.pad(dbias2[:, a, :], ((0, 0), (GROUP - 1 - a, nj - WIN - (GROUP - 1 - a)))) for a in range(GROUP))
    return jnp.dot(dbv, jnp.asarray(onehot).T, precision=lax.Precision.HIGHEST)


def kernel(x, g_mix, w_in, b_f, b_gate, rel_bias, w_branch_a, w_branch_b, w_out, g_ffn, w_gate_ffn, w_up_ffn, w_down_ffn, g_final, loss_target, m_g_mix, m_w_in, m_b_f, m_b_gate, m_rel_bias, m_w_branch_a, m_w_branch_b, m_w_out, m_g_ffn, m_w_gate_ffn, m_w_up_ffn, m_w_down_ffn, m_g_final, v_g_mix, v_w_in, v_b_f, v_b_gate, v_rel_bias, v_w_branch_a, v_w_branch_b, v_w_out, v_g_ffn, v_w_gate_ffn, v_w_up_ffn, v_w_down_ffn, v_g_final):
    T, D = x.shape[1], x.shape[2]
    Ls = w_in.shape[2]
    W = w_branch_a.shape[1]
    nh = W // HEAD_DIM
    nhb = b_f.shape[1]
    assert w_branch_b.shape[1] == W and nhb == nh and rel_bias.shape[1] == nh
    W6 = 6 * W
    Fl = w_gate_ffn.shape[2]
    Fp = -(-Fl // LANE) * LANE
    n_rel = rel_bias.shape[2]
    chip = 2 * lax.axis_index("x") + lax.axis_index("y")
    lay, nbw, nmain = _in_layout(D, W6, nhb, Ls)
    onehot, inband = _rel_tables(n_rel)

    xs, tgt = x[0], loss_target[0]

    win_f32 = lax.switch(chip, [functools.partial(_to_window, lay_k=lay[k], nbw=nbw) for k in range(N_CHIPS)], w_in[0])
    pad_c = lambda w: jnp.pad(w, ((0, 0), (0, Fp - Fl)))
    pad_r = lambda w: jnp.pad(w, ((0, Fp - Fl), (0, 0)))
    sh_in = _cast_bf16("cast_w_in", win_f32, chip)
    sh_a = _cast_bf16("cast_w_a", w_branch_a[0], chip)
    sh_b = _cast_bf16("cast_w_b", w_branch_b[0], chip)
    sh_o = _cast_bf16("cast_w_out", w_out[0], chip)
    sh_g = _cast_bf16("cast_w_gate", pad_c(w_gate_ffn[0]), chip)
    sh_u = _cast_bf16("cast_w_up", pad_c(w_up_ffn[0]), chip)
    sh_d = _cast_bf16("cast_w_down", pad_r(w_down_ffn[0]), chip)
    (wins,) = _allgather("ag_w_in", [sh_in])
    wc = _assemble_in("assemble_w_in", wins, lay, nbw, nmain)

    h1, r1 = _rms_fwd("rms1", xs, g_mix)
    qkv, ((wa_g, wb_g, wo_g),) = _mm_nn("proj_qkv", h1, wc, BF16, b_col0=0, n=W6, tm=1024,
                                        jobs=[_job_gather_ici([sh_a, sh_b, sh_o])])
    gates, ((wa_g, wb_g, wo_g), (wg_g,)) = _mm_nn(
        "proj_gates", h1, wc, F32, b_col0=W6, n=2 * D, tm=1024,
        jobs=[_job_gather_d2d([wa_g, wb_g, wo_g]), _job_gather_ici([sh_g], part=(0, 2))])
    fl = _mm_nn("proj_f", h1, wc, F32, b_col0=nmain, n=LANE, tn=LANE)
    bias2 = _bias2_of(rel_bias[0], onehot, inband)
    bf_pad = jnp.pad(b_f, ((0, 0), (0, LANE - nhb)))
    (o_a, lse_a), ((wg_g,),) = _attn_a_fwd("attn_a_fwd", qkv, bias2, nh, jobs=[_job_gather_ici([wg_g], part=(1, 2))])
    cum, cumt = _fox_prep("fox_prep", fl, bf_pad)
    (o_b, lse_b), ((wu_g,), (wg_g,)) = _fox_fwd("fox_fwd", qkv, cum, cumt, nh,
                                                jobs=[_job_gather_ici([sh_u]), _job_gather_d2d([wg_g])])
    u_a = _mm_nn("branch_a", o_a, wa_g, F32)
    u_b = _mm_nn("branch_b", o_b, wb_g, F32)
    merged = _merge_fwd("merge", gates, u_a, u_b, b_gate)
    wo_full = wo_g.reshape(D, D)
    x1, ((wu_g,),) = _mm_nn("out_proj", merged, wo_full, F32, residual=xs, jobs=[_job_gather_d2d([wu_g])])
    h2, r2 = _rms_fwd("rms2", x1, g_ffn)

    tm_f = _pick(T, (1024, 512, 256, 128))
    tn_f = _pick(Fp, (1408, 1024, 512, 256, 128))
    tk_f = _pick(D, (1024, 512, 256, 128))
    nps_f = Fp // tn_f

    def swiglu_ep(accs, e_refs, o_refs):
        g, u = accs
        o_refs[0][...] = g.astype(BF16)
        o_refs[1][...] = u.astype(BF16)
        o_refs[2][...] = (g * _sigmoid(g) * u).astype(BF16)

    hid_spec = pl.BlockSpec((tm_f, tn_f), lambda i, j, k: (i, j))
    wcol_spec = pl.BlockSpec((None, tk_f, tn_f), lambda i, j, k: (j // nps_f, k, j % nps_f))
    (gate, up, hidden), ((wd_g,),) = _mm(
        "ffn_up", "nn", [h2], [pl.BlockSpec((tm_f, tk_f), lambda i, j, k: (i, k))], [wg_g, wu_g], [wcol_spec, wcol_spec],
        [(0, 0, 0), (0, 1, 1)], 2, (T // tm_f, N_CHIPS * Fp // tn_f, D // tk_f), tm_f, tn_f,
        [_sds((T, N_CHIPS * Fp), BF16)] * 3, [hid_spec] * 3, swiglu_ep, jobs=[_job_gather_ici([sh_d])])
    ((wd_g,),) = _comm_only("ag_w_down_d2d", [_job_gather_d2d([wd_g])])
    wd_full = wd_g.reshape(N_CHIPS * Fp, D)
    x2 = _mm_nn("ffn_down", hidden, wd_full, F32, residual=x1, tm=1024, tn=_pick(D, (1024, 512, 256, 128)),
                tk=_pick(N_CHIPS * Fp, (1408, 1024, 512, 256, 128)))

    dx2, dx2b, loss_part, gg_final = _final_loss_bwd("final_loss", x2, tgt, g_final.reshape(1, D))

    def swiglu_bwd_ep(accs, e_refs, o_refs):
        dh = accs[0]
        g = e_refs[0][...].astype(F32)
        u = e_refs[1][...].astype(F32)
        sg = _sigmoid(g)
        o_refs[0][...] = (dh * u * (sg * (1.0 + g * (1.0 - sg)))).astype(BF16)
        o_refs[1][...] = (dh * (g * sg)).astype(BF16)

    tk_b = _pick(D, (1024, 512, 256, 128))
    core = lax.axis_index("c")
    (dgate, dup), _ = _mm(
        "ffn_down_bwd", "nt", [dx2b], [pl.BlockSpec((tm_f, tk_b), lambda i, j, k: (i, k))],
        [wd_full], [pl.BlockSpec((tn_f, tk_b), lambda i, j, k: (j, k))], [(0, 0, 0)], 1,
        (T // tm_f, N_CHIPS * Fp // tn_f, D // tk_b), tm_f, tn_f,
        [_sds((T, N_CHIPS * Fp), BF16)] * 2, [hid_spec] * 2, swiglu_bwd_ep,
        extra=[gate, up], extra_specs=[hid_spec, hid_spec])
    dwd = _mm_tn("dw_down", hidden, dx2b, BF16, tm=_pick(N_CHIPS * Fp, (1408, 1024, 512, 256, 128)))
    dwd = dwd.reshape(N_CHIPS, Fp, D)
    dh2, ((sib_d,),) = _mm_nt("ffn_up_bwd", [dgate, dup], [wg_g, wu_g], F32, tm=1024, jobs=[_job_sibling([dwd])])
    dwg = _mm_tn("dw_gate", h2, dgate, BF16, slots=N_CHIPS)
    dwu = _mm_tn("dw_up", h2, dup, BF16, slots=N_CHIPS)
    dx1, dx1b, gg_ffn = _rms_bwd("rms2_bwd", [dh2], x1, r2, g_ffn, dx2, True)
    part_d = _add_bf16("rs_add_down", dwd, core, sib_d)

    dmerged, ((sib_g, sib_u),) = _mm_nt("out_proj_bwd", [dx1b], [wo_full], F32, jobs=[_job_sibling([dwg, dwu])])
    dwo = _mm_tn("dw_out", merged, dx1b, BF16).reshape(N_CHIPS, D // N_CHIPS, D)
    du_a, du_b, dga, dgb, gbg_a, gbg_b = _merge_bwd("merge_bwd", dmerged, gates, u_a, u_b, b_gate)
    part_g = _add_bf16("rs_add_gate", dwg, core, sib_g)
    part_u = _add_bf16("rs_add_up", dwu, core, sib_u)
    do_a = _mm_nt("branch_a_bwd", [du_a], [wa_g], BF16)
    do_b = _mm_nt("branch_b_bwd", [du_b], [wb_g], BF16)
    dwa = _mm_tn("dw_a", o_a, du_a, BF16, slots=N_CHIPS)
    dwb = _mm_tn("dw_b", o_b, du_b, BF16, slots=N_CHIPS)

    (dq_a, delta_a, dbias2), ((got_d,), (sib_a, sib_b, sib_o)) = _attn_a_dq(
        "attn_a_dq", qkv, do_a, lse_a, bias2, nh,
        jobs=[_job_scatter([part_d], part=(0, 2)), _job_sibling([dwa, dwb, dwo])])
    part_a = _add_bf16("rs_add_a", dwa, core, sib_a)
    part_b = _add_bf16("rs_add_b", dwb, core, sib_b)
    part_o = _add_bf16("rs_add_out", dwo, core, sib_o)
    (dk_a, dv_a), ((got_d,), (got_g,)) = _attn_a_dkv(
        "attn_a_dkv", qkv, do_a, lse_a, delta_a, bias2, nh,
        jobs=[_job_scatter([part_d], part=(1, 2), into=[got_d]), _job_scatter([part_g], part=(0, 2))])
    full_d = _sum4("rs_sum_down", got_d, part_d, chip, core)
    (dq_b, delta_b), ((got_g,), (got_u,)) = _fox_dq(
        "fox_dq", qkv, do_b, lse_b, cum, cumt, nh,
        jobs=[_job_scatter([part_g], part=(1, 2), into=[got_g]), _job_scatter([part_u])])
    full_g = _sum4("rs_sum_gate", got_g, part_g, chip, core)
    full_u = _sum4("rs_sum_up", got_u, part_u, chip, core)
    (dk_b, dv_b, dcumt), ((got_a, got_b, got_o),) = _fox_dkv(
        "fox_dkv", qkv, do_b, lse_b, delta_b, cum, cumt, nh, jobs=[_job_scatter([part_a, part_b, part_o])])
    full_a = _sum4("rs_sum_a", got_a, part_a, chip, core)
    full_b = _sum4("rs_sum_b", got_b, part_b, chip, core)
    full_o = _sum4("rs_sum_out", got_o, part_o, chip, core)
    df, gbf = _fox_post("fox_post", dcumt, fl, bf_pad)

    dqkv = jnp.concatenate([dq_a, dk_a, dv_a, dq_b, dk_b, dv_b], axis=1)
    dgates = jnp.concatenate([dga, dgb], axis=1)
    dwc_q, ((g_d, g_g, g_u, g_a, g_b, g_o),) = _mm_tn(
        "dw_in_qkv", h1, dqkv, BF16, jobs=[_job_swap([full_d, full_g, full_u, full_a, full_b, full_o])])
    dwc_g = _mm_tn("dw_in_gates", h1, dgates, BF16)
    dwc_f = _mm_tn("dw_in_f", h1, df, BF16, tn=LANE)
    dwc = jnp.concatenate([dwc_q, dwc_g, dwc_f], axis=1)
    zeros_blk = jnp.zeros((D, LANE), BF16)
    win_parts = []
    for k in range(N_CHIPS):
        cols = [dwc[:, lay[k]["s"] * LANE:lay[k]["e"] * LANE]]
        nb = lay[k]["e"] - lay[k]["s"]
        if lay[k]["f"]:
            cols.append(dwc[:, nmain:nmain + LANE])
            nb += 1
        cols += [zeros_blk] * (nbw - nb)
        win_parts.append(jnp.concatenate(cols, axis=1) if len(cols) > 1 else cols[0])
    dwin = jnp.stack(win_parts, axis=0)
    g_g, g_u, g_d = g_g[:, :Fl], g_u[:, :Fl], g_d[:Fl, :]
    big = {}

    def adamw(nm, w, g, m, v, jobs=()):
        (d, mn, vn), jouts = _adamw(f"adamw_{nm}", w[0], g, m[0], v[0], jobs=jobs)
        big[nm] = (g[None], d[None], mn[None], vn[None])
        return jouts

    ((sib_in,),) = adamw("w_gate_ffn", w_gate_ffn, g_g, m_w_gate_ffn, v_w_gate_ffn, jobs=[_job_sibling([dwin])])
    part_in = _add_bf16("rs_add_in", dwin, core, sib_in)
    dh, ((got_in,),) = _mm_nt("proj_qkv_bwd", [dqkv], [wc], F32, k0_list=[0], tk=_pick(W6, (1024, 512, 256, 128)), tm=1024,
                              jobs=[_job_scatter([part_in], part=(0, 2))])
    dh, ((got_in,),) = _mm_nt("proj_gates_bwd", [dgates], [wc], F32, k0_list=[W6], tm=1024,
                              tk=_pick(math_gcd(W6, 2 * D), (1024, 512, 256, 128)), residual=dh,
                              jobs=[_job_scatter([part_in], part=(1, 2), into=[got_in])])
    full_in = _sum4("rs_sum_in", got_in, part_in, chip, core)
    dh, ((g_win,),) = _mm_nt("proj_f_bwd", [df], [wc], F32, k0_list=[nmain], tk=LANE, residual=dh,
                             jobs=[_job_swap([full_in])])
    grad_x, gg_mix = _rms_bwd("rms1_bwd", [dh], xs, r1, g_mix, dx1, False)
    g_in = lax.switch(chip, [functools.partial(_from_window, lay_k=lay[k]) for k in range(N_CHIPS)], g_win)

    for nm, w, g, m, v in (("w_in", w_in, g_in, m_w_in, v_w_in), ("w_branch_a", w_branch_a, g_a, m_w_branch_a, v_w_branch_a),
                           ("w_branch_b", w_branch_b, g_b, m_w_branch_b, v_w_branch_b), ("w_out", w_out, g_o, m_w_out, v_w_out),
                           ("w_up_ffn", w_up_ffn, g_u, m_w_up_ffn, v_w_up_ffn),
                           ("w_down_ffn", w_down_ffn, g_d, m_w_down_ffn, v_w_down_ffn)):
        adamw(nm, w, g, m, v)

    g_rel = _rel_grad_of(dbias2, onehot)
    small = [("loss", loss_part[:, :1], None, None, None),
             ("g_mix", gg_mix, g_mix, m_g_mix, v_g_mix), ("b_f", gbf[:, :nhb], b_f, m_b_f, v_b_f),
             ("b_gate", jnp.concatenate([gbg_a, gbg_b], axis=1), b_gate, m_b_gate, v_b_gate),
             ("rel_bias", g_rel, rel_bias, m_rel_bias, v_rel_bias), ("g_ffn", gg_ffn, g_ffn, m_g_ffn, v_g_ffn),
             ("g_final", gg_final, g_final, m_g_final, v_g_final)]
    sizes = [int(np.prod(s[1].shape)) for s in small]
    total = sum(sizes)
    npad = -(-total // 1024) * 1024

    def pack(arrs):
        flat = jnp.concatenate([a.reshape(-1).astype(F32) for a in arrs])
        return jnp.pad(flat, (0, npad - total)).reshape(8, npad // 8)

    zero1 = jnp.zeros((1,), F32)
    g_all = _small_allreduce("small_allreduce", pack([s[1] for s in small]))
    w_s = pack([zero1 if s[2] is None else s[2] for s in small])
    m_s = pack([zero1 if s[3] is None else s[3] for s in small])
    v_s = pack([zero1 + 1.0 if s[4] is None else s[4] for s in small])
    (d_s, mn_s, vn_s), _ = _adamw("adamw_small", w_s, g_all, m_s, v_s)

    def unpack(packed):
        flat = packed.reshape(-1)
        out, pos = {}, 0
        for s, n in zip(small, sizes):
            if s[2] is not None:
                out[s[0]] = flat[pos:pos + n].reshape(s[2].shape)
            else:
                out[s[0]] = flat[pos:pos + n].reshape(())
            pos += n
        return out

    gs, ds, ms, vs = unpack(g_all), unpack(d_s), unpack(mn_s), unpack(vn_s)
    order = ["g_mix", "w_in", "b_f", "b_gate", "rel_bias", "w_branch_a", "w_branch_b", "w_out", "g_ffn",
             "w_gate_ffn", "w_up_ffn", "w_down_ffn", "g_final"]
    res = [[], [], [], []]
    for nm in order:
        four = big[nm] if nm in big else (gs[nm], ds[nm], ms[nm], vs[nm])
        for q in range(4):
            res[q].append(four[q])
    return (gs["loss"], grad_x[None], *res[0], *res[1], *res[2], *res[3])


def math_gcd(a, b):
    while b:
        a, b = b, a % b
    return a
```

```python
import functools

import numpy as np
import jax
import jax.numpy as jnp
from jax import lax
from jax.experimental import pallas as pl
from jax.experimental.pallas import tpu as pltpu

F32 = jnp.float32
BF16 = jnp.bfloat16
LANE = 128
HEAD_DIM = 128
CHUNK = 64
LEFT_CHUNKS = 8
GROUP = 128
WIN_BLOCKS = 5
WIN = WIN_BLOCKS * GROUP
BAND = (LEFT_CHUNKS + 1) * CHUNK
RMS_EPS = 1e-6
NEG_INF = -1e30
ADAM_LR = 0.001
ADAM_B1 = 0.9
ADAM_B2 = 0.999
ADAM_EPS = 1e-08
ADAM_WD = 0.01
ADAM_STEP = 10
N_CHIPS = 4
MESH = pl.DeviceIdType.MESH
VMEM_LIMIT = 52 * 1024 * 1024
ANY = pl.BlockSpec(memory_space=pl.ANY)

NN = (((1,), (0,)), ((), ()))
NT = (((1,), (1,)), ((), ()))
TN = (((0,), (0,)), ((), ()))


def _cp(sem):
    return pltpu.CompilerParams(dimension_semantics=sem, vmem_limit_bytes=VMEM_LIMIT)


def _sds(shape, dtype):
    return jax.ShapeDtypeStruct(shape, dtype)


def _pick(n, prefs):
    for p in prefs:
        if n % p == 0:
            return p
    return n


def _sigmoid(v):
    return 1.0 / (1.0 + jnp.exp(-v))


def _split3(v):
    hi = v.astype(BF16)
    r1 = v - hi.astype(F32)
    mid = r1.astype(BF16)
    lo = (r1 - mid.astype(F32)).astype(BF16)
    return hi, mid, lo


def _col_of(blk, h):
    lane = lax.broadcasted_iota(jnp.int32, blk.shape, 1)
    return jnp.sum(jnp.where(lane == h, blk, 0.0), axis=1, keepdims=True)


def _put_col(ref, h, col):
    lane = lax.broadcasted_iota(jnp.int32, ref.shape, 1)
    ref[...] = jnp.where(lane == h, col, ref[...])


def _mm(name, mode, a_list, a_specs, b_list, b_specs, pairs, n_acc, grid, tm, tn,
        out_shapes, out_specs, epilogue, extra=(), extra_specs=(), jobs=()):
    n_a, n_b, n_e, n_o = len(a_list), len(b_list), len(extra), len(out_shapes)
    nk = grid[2]
    dn = {"nn": NN, "nt": NT, "tn": TN}[mode]

    def body(*refs):
        a_refs = refs[:n_a]
        b_refs = refs[n_a:n_a + n_b]
        e_refs = refs[n_a + n_b:n_a + n_b + n_e]
        o_refs = refs[n_a + n_b + n_e:n_a + n_b + n_e + n_o]
        acc_refs = refs[n_a + n_b + n_e + n_o:]
        k = pl.program_id(2)

        @pl.when(k == 0)
        def _():
            for acc in acc_refs:
                acc[...] = jnp.zeros_like(acc)

        for ai, bi, ci in pairs:
            acc_refs[ci][...] += lax.dot_general(a_refs[ai][...], b_refs[bi][...], dn,
                                                 preferred_element_type=F32)

        @pl.when(k == nk - 1)
        def _():
            epilogue([acc[...] for acc in acc_refs], e_refs, o_refs)

    return _pcall(
        body, name=name, grid=grid,
        in_specs=list(a_specs) + list(b_specs) + list(extra_specs),
        out_specs=list(out_specs), out_shape=list(out_shapes),
        scratch_shapes=[pltpu.VMEM((tm, tn), F32) for _ in range(n_acc)],
        sem=("parallel", "parallel", "arbitrary"), jobs=jobs,
    )(*a_list, *b_list, *extra)


def _one(res, jobs):
    outs, jouts = res
    return (outs[0], jouts) if jobs else outs[0]


def _store(dtype):
    def ep(accs, e_refs, o_refs):
        o_refs[0][...] = accs[0].astype(dtype)
    return ep


def _mm_nn(name, a, b, out_dtype, *, b_col0=0, n=None, tm=512, tn=None, tk=None, residual=None, jobs=()):
    M, K = a.shape
    if b.ndim == 3:
        Ns = b.shape[2]
        n = b.shape[0] * Ns
        tn = tn or _pick(Ns, (1408, 1024, 512, 256, 128))
        nps = Ns // tn
        b_spec = pl.BlockSpec((None, tk or _pick(K, (1024, 512, 256, 128)), tn),
                              lambda i, j, k: (j // nps, k, j % nps))
    else:
        n = n or b.shape[1]
        tn = tn or _pick(math_gcd(n, b_col0) if b_col0 else n, (2048, 1024, 512, 256, 128))
        assert b_col0 % tn == 0 and n % tn == 0
        c0 = b_col0 // tn
        b_spec = pl.BlockSpec((tk or _pick(K, (1024, 512, 256, 128)), tn), lambda i, j, k: (k, c0 + j))
    tk = tk or _pick(K, (1024, 512, 256, 128))
    tm = _pick(M, (tm, 256, 128))
    grid = (M // tm, n // tn, K // tk)
    a_spec = pl.BlockSpec((tm, tk), lambda i, j, k: (i, k))
    o_spec = pl.BlockSpec((tm, tn), lambda i, j, k: (i, j))
    if residual is None:
        return _one(_mm(name, "nn", [a], [a_spec], [b], [b_spec], [(0, 0, 0)], 1, grid, tm, tn,
                        [_sds((M, n), out_dtype)], [o_spec], _store(out_dtype), jobs=jobs), jobs)

    def ep(accs, e_refs, o_refs):
        o_refs[0][...] = (e_refs[0][...] + accs[0]).astype(out_dtype)
    return _one(_mm(name, "nn", [a], [a_spec], [b], [b_spec], [(0, 0, 0)], 1, grid, tm, tn,
                    [_sds((M, n), out_dtype)], [o_spec], ep, extra=[residual], extra_specs=[o_spec], jobs=jobs), jobs)


def _mm_nt(name, a_list, b_list, out_dtype, *, k0_list=None, tm=512, tn=None, tk=None, residual=None, jobs=()):
    M, K = a_list[0].shape
    b0 = b_list[0]
    N = b0.shape[1] if b0.ndim == 3 else b0.shape[0]
    tm = _pick(M, (tm, 256, 128))
    tn = tn or _pick(N, (1024, 512, 256, 128))
    if b0.ndim == 3:
        Ks = b0.shape[2]
        tk = tk or _pick(Ks, (1408, 1024, 512, 256, 128))
        kps = Ks // tk
        b_specs = [pl.BlockSpec((None, tn, tk), lambda i, j, k: (k // kps, j, k % kps)) for _ in b_list]
    else:
        tk = tk or _pick(K, (1024, 896, 512, 256, 128))
        k0_list = k0_list or [0] * len(b_list)
        b_specs = []
        for k0 in k0_list:
            assert k0 % tk == 0
            b_specs.append(pl.BlockSpec((tn, tk), functools.partial(lambda i, j, k, c: (j, c + k), c=k0 // tk)))
    grid = (M // tm, N // tn, K // tk)
    a_specs = [pl.BlockSpec((tm, tk), lambda i, j, k: (i, k)) for _ in a_list]
    o_spec = pl.BlockSpec((tm, tn), lambda i, j, k: (i, j))
    pairs = [(p, p, 0) for p in range(len(a_list))]
    if residual is None:
        return _one(_mm(name, "nt", a_list, a_specs, b_list, b_specs, pairs, 1, grid, tm, tn,
                        [_sds((M, N), out_dtype)], [o_spec], _store(out_dtype), jobs=jobs), jobs)

    def ep(accs, e_refs, o_refs):
        o_refs[0][...] = (e_refs[0][...] + accs[0]).astype(out_dtype)
    return _one(_mm(name, "nt", a_list, a_specs, b_list, b_specs, pairs, 1, grid, tm, tn,
                    [_sds((M, N), out_dtype)], [o_spec], ep, extra=[residual], extra_specs=[o_spec], jobs=jobs), jobs)


def _mm_tn(name, a, b, out_dtype, *, slots=None, tm=None, tn=None, tk=1024, jobs=()):
    Kc, Mo = a.shape
    No = b.shape[1]
    tm = tm or _pick(Mo, (1024, 704, 512, 256, 128))
    tk = _pick(Kc, (tk, 256, 128))
    if slots:
        Ns = No // slots
        tn = tn or _pick(Ns, (1408, 1024, 512, 256, 128))
        nps = Ns // tn
        o_spec = pl.BlockSpec((None, tm, tn), lambda i, j, k: (j // nps, i, j % nps))
        o_shape = _sds((slots, Mo, Ns), out_dtype)
    else:
        tn = tn or _pick(No, (1024, 512, 256, 128))
        o_spec = pl.BlockSpec((tm, tn), lambda i, j, k: (i, j))
        o_shape = _sds((Mo, No), out_dtype)
    grid = (Mo // tm, No // tn, Kc // tk)
    a_spec = pl.BlockSpec((tk, tm), lambda i, j, k: (k, i))
    b_spec = pl.BlockSpec((tk, tn), lambda i, j, k: (k, j))
    return _one(_mm(name, "tn", [a], [a_spec], [b], [b_spec], [(0, 0, 0)], 1, grid, tm, tn,
                    [o_shape], [o_spec], _store(out_dtype), jobs=jobs), jobs)


def _cast_bf16(name, w, chip):
    R, C = w.shape
    tr = _pick(R, (256, 128, 64, 32, 16))

    def body(k_ref, w_ref, o_ref):
        o_ref[...] = w_ref[...].astype(BF16)

    gs = pltpu.PrefetchScalarGridSpec(
        num_scalar_prefetch=1, grid=(R // tr,),
        in_specs=[pl.BlockSpec((tr, C), lambda i, k: (i, 0))],
        out_specs=pl.BlockSpec((None, tr, C), lambda i, k: (k[0], i, 0)))
    return pl.pallas_call(body, name=name, grid_spec=gs, out_shape=_sds((N_CHIPS, R, C), BF16),
                          compiler_params=_cp(("parallel",)))(jnp.reshape(chip, (1,)).astype(jnp.int32), w)


def _rms_fwd(name, x, g):
    T, D = x.shape
    tr = _pick(T, (256, 128))

    def body(x_ref, g_ref, h_ref, r_ref):
        xv = x_ref[...]
        r = lax.rsqrt(jnp.mean(xv * xv, axis=1, keepdims=True) + RMS_EPS)
        h_ref[...] = (xv * r * g_ref[...]).astype(BF16)
        r_ref[...] = r

    row = pl.BlockSpec((tr, D), lambda i: (i, 0))
    return pl.pallas_call(
        body, name=name, grid=(T // tr,),
        in_specs=[row, pl.BlockSpec((1, D), lambda i: (0, 0))],
        out_specs=[row, pl.BlockSpec((tr, 1), lambda i: (i, 0))],
        out_shape=[_sds((T, D), BF16), _sds((T, 1), F32)], compiler_params=_cp(("parallel",)),
    )(x, g)


def _final_loss_bwd(name, x2, tgt, g):
    T, D = x2.shape
    tr = _pick(T, (256, 128))

    def body(x_ref, t_ref, g_ref, dx_ref, dxb_ref, loss_ref, gg_ref):
        @pl.when(pl.program_id(0) == 0)
        def _():
            loss_ref[...] = jnp.zeros_like(loss_ref)
            gg_ref[...] = jnp.zeros_like(gg_ref)

        xv = x_ref[...]
        gv = g_ref[...]
        r = lax.rsqrt(jnp.mean(xv * xv, axis=1, keepdims=True) + RMS_EPS)
        n = xv * r
        e = n * gv - t_ref[...]
        loss_ref[...] += 0.5 * jnp.sum(jnp.mean(e * e, axis=1, keepdims=True), axis=0, keepdims=True)
        dy = e * (1.0 / D)
        gg_ref[...] += jnp.sum(dy * n, axis=0, keepdims=True)
        gy = dy * gv
        dx = r * (gy - n * jnp.mean(gy * n, axis=1, keepdims=True))
        dx_ref[...] = dx
        dxb_ref[...] = dx.astype(BF16)

    row = pl.BlockSpec((tr, D), lambda i: (i, 0))
    vec = pl.BlockSpec((1, D), lambda i: (0, 0))
    return pl.pallas_call(
        body, name=name, grid=(T // tr,),
        in_specs=[row, row, vec],
        out_specs=[row, row, pl.BlockSpec((1, LANE), lambda i: (0, 0)), vec],
        out_shape=[_sds((T, D), F32), _sds((T, D), BF16), _sds((1, LANE), F32), _sds((1, D), F32)],
        compiler_params=_cp(("arbitrary",)),
    )(x2, tgt, g)


def _rms_bwd(name, dh_list, x, r, g, dres, want_bf16):
    T, D = x.shape
    tr = _pick(T, (128,))
    n_dh = len(dh_list)

    def body(*refs):
        dh_refs = refs[:n_dh]
        x_ref, r_ref, g_ref, dres_ref = refs[n_dh:n_dh + 4]
        outs = refs[n_dh + 4:]
        gg_ref = outs[-1]

        @pl.when(pl.program_id(0) == 0)
        def _():
            gg_ref[...] = jnp.zeros_like(gg_ref)

        dh = dh_refs[0][...]
        for ref in dh_refs[1:]:
            dh = dh + ref[...]
        rv = r_ref[...]
        n = x_ref[...] * rv
        gg_ref[...] += jnp.sum(dh * n, axis=0, keepdims=True)
        gy = dh * g_ref[...]
        dx = dres_ref[...] + rv * (gy - n * jnp.mean(gy * n, axis=1, keepdims=True))
        outs[0][...] = dx
        if want_bf16:
            outs[1][...] = dx.astype(BF16)

    row = pl.BlockSpec((tr, D), lambda i: (i, 0))
    vec = pl.BlockSpec((1, D), lambda i: (0, 0))
    out_specs = [row] + ([row] if want_bf16 else []) + [vec]
    out_shape = [_sds((T, D), F32)] + ([_sds((T, D), BF16)] if want_bf16 else []) + [_sds((1, D), F32)]
    return pl.pallas_call(
        body, name=name, grid=(T // tr,),
        in_specs=[row] * n_dh + [row, pl.BlockSpec((tr, 1), lambda i: (i, 0)), vec, row],
        out_specs=out_specs, out_shape=out_shape, compiler_params=_cp(("arbitrary",)),
    )(*dh_list, x, r, g, dres)


def _merge_fwd(name, gates, u_a, u_b, b_gate):
    T, D = u_a.shape
    tr = _pick(T, (256, 128))

    def body(ga_ref, gb_ref, ua_ref, ub_ref, ba_ref, bb_ref, o_ref):
        sa = _sigmoid(ga_ref[...].astype(F32) + ba_ref[...])
        sb = _sigmoid(gb_ref[...].astype(F32) + bb_ref[...])
        o_ref[...] = (sa * ua_ref[...].astype(F32) + sb * ub_ref[...].astype(F32)).astype(BF16)

    row = pl.BlockSpec((tr, D), lambda i: (i, 0))
    row1 = pl.BlockSpec((tr, D), lambda i: (i, 1))
    v0 = pl.BlockSpec((1, D), lambda i: (0, 0))
    v1 = pl.BlockSpec((1, D), lambda i: (0, 1))
    return pl.pallas_call(
        body, name=name, grid=(T // tr,),
        in_specs=[row, row1, row, row, v0, v1], out_specs=row,
        out_shape=_sds((T, D), BF16), compiler_params=_cp(("parallel",)),
    )(gates, gates, u_a, u_b, b_gate, b_gate)


def _merge_bwd(name, dm, gates, u_a, u_b, b_gate):
    T, D = u_a.shape
    tr = _pick(T, (128,))

    def body(dm_ref, ga_ref, gb_ref, ua_ref, ub_ref, ba_ref, bb_ref, dua_ref, dub_ref, dga_ref, dgb_ref,
             gba_ref, gbb_ref):
        @pl.when(pl.program_id(0) == 0)
        def _():
            gba_ref[...] = jnp.zeros_like(gba_ref)
            gbb_ref[...] = jnp.zeros_like(gbb_ref)

        d = dm_ref[...].astype(F32)
        sa = _sigmoid(ga_ref[...].astype(F32) + ba_ref[...])
        sb = _sigmoid(gb_ref[...].astype(F32) + bb_ref[...])
        dua_ref[...] = (d * sa).astype(BF16)
        dub_ref[...] = (d * sb).astype(BF16)
        dga = d * ua_ref[...].astype(F32) * sa * (1.0 - sa)
        dgb = d * ub_ref[...].astype(F32) * sb * (1.0 - sb)
        dga_ref[...] = dga.astype(BF16)
        dgb_ref[...] = dgb.astype(BF16)
        gba_ref[...] += jnp.sum(dga, axis=0, keepdims=True)
        gbb_ref[...] += jnp.sum(dgb, axis=0, keepdims=True)

    row = pl.BlockSpec((tr, D), lambda i: (i, 0))
    row1 = pl.BlockSpec((tr, D), lambda i: (i, 1))
    v0 = pl.BlockSpec((1, D), lambda i: (0, 0))
    v1 = pl.BlockSpec((1, D), lambda i: (0, 1))
    outs = pl.pallas_call(
        body, name=name, grid=(T // tr,),
        in_specs=[row, row, row1, row, row, v0, v1],
        out_specs=[row, row, row, row, v0, v0],
        out_shape=[_sds((T, D), BF16), _sds((T, D), BF16), _sds((T, D), BF16), _sds((T, D), BF16),
                   _sds((1, D), F32), _sds((1, D), F32)],
        compiler_params=_cp(("arbitrary",)),
    )(dm, gates, gates, u_a, u_b, b_gate, b_gate)
    return outs


def _adamw(name, w, g, m, v, jobs=()):
    R, C = w.shape
    Cg = g.shape[1]
    tr = _pick(R, (64, 32, 16, 8))
    c1 = 1.0 - ADAM_B1 ** ADAM_STEP
    c2 = 1.0 - ADAM_B2 ** ADAM_STEP

    def body(w_ref, g_ref, m_ref, v_ref, d_ref, mo_ref, vo_ref, go_ref):
        gv = g_ref[...] if Cg == C else g_ref[:, :C]
        mn = ADAM_B1 * m_ref[...] + (1.0 - ADAM_B1) * gv
        vn = ADAM_B2 * v_ref[...] + (1.0 - ADAM_B2) * (gv * gv)
        d_ref[...] = -ADAM_LR * ((mn / c1) / (jnp.sqrt(vn / c2) + ADAM_EPS) + ADAM_WD * w_ref[...])
        mo_ref[...] = mn
        vo_ref[...] = vn
        go_ref[...] = gv

    blk = pl.BlockSpec((tr, C), lambda i: (i, 0))
    gblk = pl.BlockSpec((tr, Cg), lambda i: (i, 0))
    return _pcall(
        body, name=name, grid=(R // tr,),
        in_specs=[blk, gblk, blk, blk], out_specs=[blk] * 4,
        out_shape=[_sds((R, C), F32)] * 4, sem=("parallel",), jobs=jobs,
    )(w, g, m, v)


def _add_bf16(name, a, a_row0, b):
    S, h, C = b.shape
    tr = _pick(h, (256, 128, 64, 32, 16))
    nb = h // tr

    def body(off_ref, a_ref, b_ref, o_ref):
        o_ref[...] = (a_ref[...].astype(F32) + b_ref[...].astype(F32)).astype(BF16)

    gs = pltpu.PrefetchScalarGridSpec(
        num_scalar_prefetch=1, grid=(S, nb),
        in_specs=[pl.BlockSpec((None, tr, C), lambda s, i, off: (s, off[0] * nb + i, 0)),
                  pl.BlockSpec((None, tr, C), lambda s, i, off: (s, i, 0))],
        out_specs=pl.BlockSpec((None, tr, C), lambda s, i, off: (s, i, 0)))
    return pl.pallas_call(body, name=name, grid_spec=gs, out_shape=_sds((S, h, C), BF16),
                          compiler_params=_cp(("parallel", "parallel")))(
        jnp.reshape(a_row0, (1,)).astype(jnp.int32), a, b)


def _sum4(name, got, mine, chip, core):
    S, h, C = got.shape
    tr = _pick(h, (256, 128, 64, 32, 16))
    nb = h // tr

    def body(chip_ref, core_ref, m_ref, g_ref, o_ref):
        acc = m_ref[...].astype(F32)
        for s in range(S):
            acc = acc + g_ref[s].astype(F32)
        o_ref[...] = acc

    gs = pltpu.PrefetchScalarGridSpec(
        num_scalar_prefetch=2, grid=(nb,),
        in_specs=[pl.BlockSpec((None, tr, C), lambda i, kc, cc: (kc[0], i, 0)),
                  pl.BlockSpec((S, tr, C), lambda i, kc, cc: (0, i, 0))],
        out_specs=pl.BlockSpec((tr, C), lambda i, kc, cc: (cc[0] * nb + i, 0)))
    return pl.pallas_call(body, name=name, grid_spec=gs, out_shape=_sds((2 * h, C), F32),
                          compiler_params=_cp(("parallel",)))(
        jnp.reshape(chip, (1,)).astype(jnp.int32), jnp.reshape(core, (1,)).astype(jnp.int32), mine, got)


def _place():
    x, y, c = lax.axis_index("x"), lax.axis_index("y"), lax.axis_index("c")
    chips = [(1 - x, y), (x, 1 - y), (1 - x, 1 - y)]
    return x, y, c, chips


def _allgather(name, shards):
    n = len(shards)
    NS = 7

    def body(*refs):
        out_refs = refs[n:2 * n]
        ss, rs = refs[2 * n:]
        x, y, c, _ = _place()
        k, kx, ky, kd = 2 * x + y, 2 * (1 - x) + y, 2 * x + (1 - y), 2 * (1 - x) + (1 - y)
        across_x, across_y, sibling = (1 - x, y, c), (x, 1 - y, c), (x, y, 1 - c)
        sends = []

        def go(cp):
            cp.start()
            sends.append(cp)

        for a, out in enumerate(out_refs):
            h = out.shape[1] // 2
            q = h // 2
            half = lambda slot, cc=c: out.at[slot, pl.ds(cc * h, h), :]
            part0 = lambda slot: out.at[slot, pl.ds(c * h, q), :]
            part1 = lambda slot: out.at[slot, pl.ds(c * h + q, q), :]
            b = NS * a
            go(_rdma(half(k), half(k), ss, rs, b + 0, across_x))
            go(_rdma(half(k), half(k), ss, rs, b + 1, across_y))
            _rdma(half(kx), half(kx), ss, rs, b + 0, across_x).wait_recv()
            go(_rdma(part0(kx), part0(kx), ss, rs, b + 2, across_y))
            go(_rdma(half(kx), half(kx), ss, rs, b + 4, sibling))
            _rdma(half(ky), half(ky), ss, rs, b + 1, across_y).wait_recv()
            go(_rdma(part1(ky), part1(ky), ss, rs, b + 3, across_x))
            go(_rdma(half(ky), half(ky), ss, rs, b + 5, sibling))
            _rdma(part0(kd), part0(kd), ss, rs, b + 2, across_y).wait_recv()
            _rdma(part1(kd), part1(kd), ss, rs, b + 3, across_x).wait_recv()
            go(_rdma(half(kd), half(kd), ss, rs, b + 6, sibling))
        for a, out in enumerate(out_refs):
            h = out.shape[1] // 2
            for j, slot in enumerate((kx, ky, kd)):
                rows = out.at[slot, pl.ds((1 - c) * h, h), :]
                _rdma(rows, rows, ss, rs, NS * a + 4 + j, sibling).wait_recv()
        for cp in sends:
            cp.wait_send()

    return pl.pallas_call(
        body, name=name,
        in_specs=[ANY] * n, out_specs=[ANY] * n,
        out_shape=[_sds(s.shape, s.dtype) for s in shards],
        input_output_aliases={a: a for a in range(n)},
        scratch_shapes=[pltpu.SemaphoreType.DMA((NS * n,)), pltpu.SemaphoreType.DMA((NS * n,))],
    )(*shards)


class _Job:
    def __init__(self, ins, out_shapes, aliases, n_sems, start, finish):
        self.ins, self.out_shapes, self.aliases, self.n_sems = list(ins), list(out_shapes), dict(aliases), n_sems
        self.start, self.finish = start, finish


def _rdma(src, dst, ss, rs, idx, to):
    return pltpu.make_async_remote_copy(src_ref=src, dst_ref=dst, send_sem=ss.at[idx], recv_sem=rs.at[idx],
                                        device_id=to, device_id_type=MESH)


def _job_gather_ici(bufs, part=(0, 1)):
    pi, pn = part

    def descs(outs, ss, rs, incoming):
        x, y, c, chips = _place()
        res = []
        for a, out in enumerate(outs):
            h = out.shape[1] // 2
            hp = h // pn
            for j, (cx, cy) in enumerate(chips):
                rows = out.at[(2 * cx + cy) if incoming else (2 * x + y), pl.ds(c * h + pi * hp, hp), :]
                res.append(_rdma(rows, rows, ss, rs, 3 * a + j, (cx, cy, c)))
        return res

    def start(ins, outs, ss, rs):
        for d in descs(outs, ss, rs, False):
            d.start()

    def finish(ins, outs, ss, rs):
        for d in descs(outs, ss, rs, True):
            d.wait_recv()
        for d in descs(outs, ss, rs, False):
            d.wait_send()

    return _Job(bufs, [_sds(b.shape, b.dtype) for b in bufs], {a: a for a in range(len(bufs))}, 3 * len(bufs),
                start, finish)


def _job_gather_d2d(bufs):
    def descs(outs, ss, rs, incoming):
        x, y, c, chips = _place()
        res = []
        for a, out in enumerate(outs):
            h = out.shape[1] // 2
            for j, (cx, cy) in enumerate(chips):
                rows = out.at[2 * cx + cy, pl.ds(((1 - c) if incoming else c) * h, h), :]
                res.append(_rdma(rows, rows, ss, rs, 3 * a + j, (x, y, 1 - c)))
        return res

    def start(ins, outs, ss, rs):
        for d in descs(outs, ss, rs, False):
            d.start()

    def finish(ins, outs, ss, rs):
        for d in descs(outs, ss, rs, True):
            d.wait_recv()
        for d in descs(outs, ss, rs, False):
            d.wait_send()

    return _Job(bufs, [_sds(b.shape, b.dtype) for b in bufs], {a: a for a in range(len(bufs))}, 3 * len(bufs),
                start, finish)


def _job_sibling(grads):
    def descs(ins, outs, ss, rs):
        x, y, c, _ = _place()
        res = []
        for a, (g, out) in enumerate(zip(ins, outs)):
            h = g.shape[1] // 2
            res.append(_rdma(g.at[:, pl.ds((1 - c) * h, h), :], out, ss, rs, a, (x, y, 1 - c)))
        return res

    def start(ins, outs, ss, rs):
        for d in descs(ins, outs, ss, rs):
            d.start()

    def finish(ins, outs, ss, rs):
        for d in descs(ins, outs, ss, rs):
            d.wait()

    return _Job(grads, [_sds((g.shape[0], g.shape[1] // 2, g.shape[2]), g.dtype) for g in grads], {}, len(grads),
                start, finish)


def _job_scatter(parts, part=(0, 1), into=None):
    pi, pn = part
    n = len(parts)

    def descs(ins, outs, ss, rs):
        x, y, c, chips = _place()
        res = []
        for a, (p, out) in enumerate(zip(ins[:n], outs)):
            hp = p.shape[1] // pn
            for j, (cx, cy) in enumerate(chips):
                res.append(_rdma(p.at[2 * cx + cy, pl.ds(pi * hp, hp), :], out.at[j, pl.ds(pi * hp, hp), :],
                                 ss, rs, 3 * a + j, (cx, cy, c)))
        return res

    def start(ins, outs, ss, rs):
        for d in descs(ins, outs, ss, rs):
            d.start()

    def finish(ins, outs, ss, rs):
        for d in descs(ins, outs, ss, rs):
            d.wait()

    shapes = [_sds((3,) + p.shape[1:], p.dtype) for p in parts]
    if into is None:
        return _Job(parts, shapes, {}, 3 * n, start, finish)
    return _Job(list(parts) + list(into), shapes, {n + a: a for a in range(n)}, 3 * n, start, finish)


def _job_swap(fulls):
    def descs(outs, ss, rs, incoming):
        x, y, c, _ = _place()
        res = []
        for a, out in enumerate(outs):
            h = out.shape[0] // 2
            rows = out.at[pl.ds(((1 - c) if incoming else c) * h, h), :]
            res.append(_rdma(rows, rows, ss, rs, a, (x, y, 1 - c)))
        return res

    def start(ins, outs, ss, rs):
        for d in descs(outs, ss, rs, False):
            d.start()

    def finish(ins, outs, ss, rs):
        for d in descs(outs, ss, rs, True):
            d.wait_recv()
        for d in descs(outs, ss, rs, False):
            d.wait_send()

    return _Job(fulls, [_sds(f.shape, f.dtype) for f in fulls], {a: a for a in range(len(fulls))}, len(fulls),
                start, finish)


def _pcall(body, *, name, grid, in_specs, out_specs, out_shape, scratch_shapes=(), sem, jobs=()):
    in_specs, out_specs, out_shape = list(in_specs), list(out_specs), list(out_shape)
    scratch = list(scratch_shapes)
    n_in, n_out, n_scr = len(in_specs), len(out_shape), len(scratch)
    if not jobs:
        call = pl.pallas_call(body, name=name, grid=grid, in_specs=in_specs, out_specs=out_specs, out_shape=out_shape,
                              scratch_shapes=scratch, compiler_params=_cp(sem))
        return lambda *args: (call(*args), [])
    jin = sum(len(j.ins) for j in jobs)
    jout = sum(len(j.out_shapes) for j in jobs)
    aliases, pi, po = {}, n_in, n_out
    for j in jobs:
        for ia, oa in j.aliases.items():
            aliases[pi + ia] = po + oa
        pi, po = pi + len(j.ins), po + len(j.out_shapes)

    def wrapped(*refs):
        ins = refs[:n_in]
        jins = refs[n_in:n_in + jin]
        outs = refs[n_in + jin:n_in + jin + n_out]
        jouts = refs[n_in + jin + n_out:n_in + jin + n_out + jout]
        scr = refs[n_in + jin + n_out + jout:n_in + jin + n_out + jout + n_scr]
        sems = refs[n_in + jin + n_out + jout + n_scr:]
        first, last = None, None
        for d, g in enumerate(grid):
            f, l = pl.program_id(d) == 0, pl.program_id(d) == g - 1
            first = f if first is None else jnp.logical_and(first, f)
            last = l if last is None else jnp.logical_and(last, l)

        def each(what):
            pi, po = 0, 0
            for q, j in enumerate(jobs):
                getattr(j, what)(jins[pi:pi + len(j.ins)], jouts[po:po + len(j.out_shapes)], sems[2 * q], sems[2 * q + 1])
                pi, po = pi + len(j.ins), po + len(j.out_shapes)

        @pl.when(first)
        def _():
            each("start")

        body(*ins, *outs, *scr)

        @pl.when(last)
        def _():
            each("finish")

    call = pl.pallas_call(
        wrapped, name=name, grid=grid,
        in_specs=in_specs + [ANY] * jin, out_specs=out_specs + [ANY] * jout,
        out_shape=out_shape + [s for j in jobs for s in j.out_shapes],
        input_output_aliases=aliases,
        scratch_shapes=scratch + [pltpu.SemaphoreType.DMA((j.n_sems,)) for j in jobs for _ in range(2)],
        compiler_params=_cp(("arbitrary",) * len(grid)))

    def run(*args):
        res = call(*args, *[a for j in jobs for a in j.ins])
        main, rest, per_job = list(res[:n_out]), list(res[n_out:]), []
        for j in jobs:
            per_job.append(rest[:len(j.out_shapes)])
            rest = rest[len(j.out_shapes):]
        return main, per_job
    return run


def _comm_only(name, jobs):
    jin = sum(len(j.ins) for j in jobs)
    jout = sum(len(j.out_shapes) for j in jobs)
    aliases, pi, po = {}, 0, 0
    for j in jobs:
        for ia, oa in j.aliases.items():
            aliases[pi + ia] = po + oa
        pi, po = pi + len(j.ins), po + len(j.out_shapes)

    def body(*refs):
        jins, jouts, sems = refs[:jin], refs[jin:jin + jout], refs[jin + jout:]
        for what in ("start", "finish"):
            pi, po = 0, 0
            for q, j in enumerate(jobs):
                getattr(j, what)(jins[pi:pi + len(j.ins)], jouts[po:po + len(j.out_shapes)], sems[2 * q], sems[2 * q + 1])
                pi, po = pi + len(j.ins), po + len(j.out_shapes)

    res = pl.pallas_call(
        body, name=name, in_specs=[ANY] * jin, out_specs=[ANY] * jout,
        out_shape=[s for j in jobs for s in j.out_shapes], input_output_aliases=aliases,
        scratch_shapes=[pltpu.SemaphoreType.DMA((j.n_sems,)) for j in jobs for _ in range(2)],
    )(*[a for j in jobs for a in j.ins])
    rest, per_job = list(res), []
    for j in jobs:
        per_job.append(rest[:len(j.out_shapes)])
        rest = rest[len(j.out_shapes):]
    return per_job


def _small_allreduce(name, v):
    m_per, n = v.shape

    def body(x_ref, sum_ref, all_ref, send_sems, recv_sems, local_sem):
        x, y, c, chips = _place()
        me, sibling = (x, y, c), (x, y, 1 - c)

        def rows(px, py, pc):
            return all_ref.at[pl.ds((4 * px + 2 * py + pc) * m_per, m_per), :]

        def copy(kk, block, to, src=None):
            return pltpu.make_async_remote_copy(
                src_ref=rows(*block) if src is None else src, dst_ref=rows(*block),
                send_sem=send_sems.at[kk], recv_sem=recv_sems.at[kk], device_id=to, device_id_type=MESH)

        mine = pltpu.make_async_copy(x_ref, rows(*me), local_sem)
        mine.start()
        first = [copy(0, me, sibling, src=x_ref)]
        first += [copy(1 + j, me, (*chip, c), src=x_ref) for j, chip in enumerate(chips)]
        for cp in first:
            cp.start()
        passed = [copy(4 + j, (*chip, c), sibling) for j, chip in enumerate(chips)]
        for j, chip in enumerate(chips):
            copy(1 + j, (*chip, c), me).wait_recv()
            passed[j].start()
        copy(0, sibling, me).wait_recv()
        for j, chip in enumerate(chips):
            copy(4 + j, (*chip, 1 - c), me).wait_recv()
        for cp in first + passed:
            cp.wait_send()
        mine.wait()
        acc = all_ref[pl.ds(0, m_per), :]
        for d in range(1, 8):
            acc = acc + all_ref[pl.ds(d * m_per, m_per), :]
        sum_ref[...] = acc

    vm = pl.BlockSpec(memory_space=pltpu.VMEM)
    return pl.pallas_call(
        body, name=name, in_specs=[vm], out_specs=[vm, vm],
        out_shape=[_sds((m_per, n), F32), _sds((8 * m_per, n), F32)],
        scratch_shapes=[pltpu.SemaphoreType.DMA((7,)), pltpu.SemaphoreType.DMA((7,)), pltpu.SemaphoreType.DMA],
    )(v)[0]


def _in_layout(D, W6, nhb, Ls):
    nmain = W6 + 2 * D
    lay = []
    for k in range(N_CHIPS):
        g0, g1 = k * Ls, (k + 1) * Ls
        pieces = []
        a, b = max(g0, 0), min(g1, W6)
        if a < b:
            pieces.append((a - g0, b - g0, a))
        a, b = max(g0, W6 + nhb), min(g1, W6 + nhb + 2 * D)
        if a < b:
            pieces.append((a - g0, b - g0, a - nhb))
        a, b = max(g0, W6), min(g1, W6 + nhb)
        fpiece = (a - g0, b - g0, a - W6) if a < b else None
        assert fpiece is None or (b - a) == nhb
        main0 = min(p[2] for p in pieces)
        main1 = max(p[2] + p[1] - p[0] for p in pieces)
        lay.append(dict(pieces=pieces, f=fpiece, s=main0 // LANE, e=-(-main1 // LANE), main1=main1))
    assert sum(1 for l in lay if l["f"] is not None) == 1
    nbw = max(l["e"] - l["s"] + (1 if l["f"] else 0) for l in lay)
    for k in range(1, N_CHIPS):
        assert lay[k]["s"] >= lay[k - 1]["e"] - 1 and lay[k]["s"] > lay[k - 1]["s"]
    return lay, nbw, nmain


def _to_window(w, lay_k, nbw):
    D = w.shape[0]
    items = [(c0 - lay_k["s"] * LANE, l0, l1) for (l0, l1, c0) in lay_k["pieces"]]
    if lay_k["f"]:
        l0, l1, off = lay_k["f"]
        items.append(((lay_k["e"] - lay_k["s"]) * LANE + off, l0, l1))
    items.sort()
    cols, pos = [], 0
    for w0, l0, l1 in items:
        if w0 > pos:
            cols.append(jnp.zeros((D, w0 - pos), w.dtype))
        cols.append(w[:, l0:l1])
        pos = w0 + (l1 - l0)
    if pos < nbw * LANE:
        cols.append(jnp.zeros((D, nbw * LANE - pos), w.dtype))
    return jnp.concatenate(cols, axis=1)


def _from_window(win, lay_k):
    items = [(l0, c0 - lay_k["s"] * LANE, l1 - l0) for (l0, l1, c0) in lay_k["pieces"]]
    if lay_k["f"]:
        l0, l1, off = lay_k["f"]
        items.append((l0, (lay_k["e"] - lay_k["s"]) * LANE + off, l1 - l0))
    items.sort()
    return jnp.concatenate([win[:, w0:w0 + n] for (_, w0, n) in items], axis=1)


def _assemble_in(name, wins, lay, nbw, nmain):
    _, D, _ = wins.shape
    ncb = nmain // LANE + 1
    k1 = np.zeros(ncb, np.int32)
    i1 = np.zeros(ncb, np.int32)
    k2 = np.zeros(ncb, np.int32)
    i2 = np.zeros(ncb, np.int32)
    fl = np.zeros(ncb, np.int32)
    for b in range(ncb - 1):
        k = max(kk for kk in range(N_CHIPS) if lay[kk]["s"] <= b)
        k1[b], i1[b] = k, b - lay[k]["s"]
        if k >= 1 and b == lay[k]["s"] and lay[k - 1]["main1"] > b * LANE:
            k2[b], i2[b], fl[b] = k - 1, b - lay[k - 1]["s"], 1
    kf = [kk for kk in range(N_CHIPS) if lay[kk]["f"]][0]
    k1[ncb - 1], i1[ncb - 1] = kf, lay[kf]["e"] - lay[kf]["s"]

    def body(k1_ref, i1_ref, k2_ref, i2_ref, fl_ref, a_ref, b_ref, o_ref):
        b = pl.program_id(0)
        add = jnp.where(fl_ref[b] == 1, b_ref[...], jnp.zeros_like(b_ref))
        o_ref[...] = a_ref[...] + add

    gs = pltpu.PrefetchScalarGridSpec(
        num_scalar_prefetch=5, grid=(ncb,),
        in_specs=[pl.BlockSpec((None, D, LANE), lambda b, k1r, i1r, k2r, i2r, flr: (k1r[b], 0, i1r[b])),
                  pl.BlockSpec((None, D, LANE), lambda b, k1r, i1r, k2r, i2r, flr: (k2r[b], 0, i2r[b]))],
        out_specs=pl.BlockSpec((D, LANE), lambda b, k1r, i1r, k2r, i2r, flr: (0, b)))
    return pl.pallas_call(body, name=name, grid_spec=gs, out_shape=_sds((D, ncb * LANE), BF16),
                          compiler_params=_cp(("parallel",)))(
        jnp.asarray(k1), jnp.asarray(i1), jnp.asarray(k2), jnp.asarray(i2), jnp.asarray(fl), wins, wins)


def _hgroup(nh):
    return _pick(nh, (4, 2, 1))


def _a_specs_q(nh, G):
    ngrp = nh // G
    blk = (GROUP, G * HEAD_DIM)
    q = pl.BlockSpec(blk, lambda i, hg: (i, hg))
    ks = [pl.BlockSpec(blk, functools.partial(
        lambda i, hg, j: (jnp.maximum(i - (WIN_BLOCKS - 1) + j, 0), ngrp + hg), j=j)) for j in range(WIN_BLOCKS)]
    vs = [pl.BlockSpec(blk, functools.partial(
        lambda i, hg, j: (jnp.maximum(i - (WIN_BLOCKS - 1) + j, 0), 2 * ngrp + hg), j=j)) for j in range(WIN_BLOCKS)]
    return q, ks, vs


def _a_logits(q, ks, bias, i, scale):
    parts = [lax.dot_general(q, k, NT, preferred_element_type=F32) for k in ks]
    s = jnp.concatenate(parts, axis=1) * scale + bias
    col = lax.broadcasted_iota(jnp.int32, s.shape, 1)
    return jnp.where(col >= (WIN_BLOCKS - 1 - i) * GROUP, s, NEG_INF)


def _attn_a_fwd(name, qkv, bias2, nh, jobs=()):
    T = qkv.shape[0]
    ng = T // GROUP
    G = _hgroup(nh)
    scale = HEAD_DIM ** -0.5

    def body(q_ref, *refs):
        k_refs = refs[:WIN_BLOCKS]
        v_refs = refs[WIN_BLOCKS:2 * WIN_BLOCKS]
        bias_ref, o_ref, lse_ref = refs[2 * WIN_BLOCKS:]
        i, hg = pl.program_id(0), pl.program_id(1)

        @pl.when(hg == 0)
        def _():
            lse_ref[...] = jnp.zeros_like(lse_ref)

        for g in range(G):
            h = hg * G + g
            sl = slice(g * HEAD_DIM, (g + 1) * HEAD_DIM)
            s = _a_logits(q_ref[:, sl], [kr[:, sl] for kr in k_refs], bias_ref[h], i, scale)
            m = jnp.max(s, axis=1, keepdims=True)
            p = jnp.exp(s - m)
            l = jnp.sum(p, axis=1, keepdims=True)
            pb = (p / l).astype(BF16)
            o = jnp.zeros((GROUP, HEAD_DIM), F32)
            for j in range(WIN_BLOCKS):
                o = o + jnp.dot(pb[:, j * GROUP:(j + 1) * GROUP], v_refs[j][:, sl], preferred_element_type=F32)
            o_ref[:, sl] = o.astype(BF16)
            _put_col(lse_ref, h, m + jnp.log(l))

    q_spec, k_specs, v_specs = _a_specs_q(nh, G)
    stat = pl.BlockSpec((GROUP, LANE), lambda i, hg: (i, 0))
    return _pcall(
        body, name=name, grid=(ng, nh // G),
        in_specs=[q_spec] + k_specs + v_specs + [pl.BlockSpec((nh, GROUP, WIN), lambda i, hg: (0, 0, 0))],
        out_specs=[pl.BlockSpec((GROUP, G * HEAD_DIM), lambda i, hg: (i, hg)), stat],
        out_shape=[_sds((T, nh * HEAD_DIM), BF16), _sds((T, LANE), F32)],
        sem=("parallel", "arbitrary"), jobs=jobs,
    )(qkv, *([qkv] * (2 * WIN_BLOCKS)), bias2)


def _attn_a_dq(name, qkv, do, lse, bias2, nh, jobs=()):
    T = qkv.shape[0]
    ng = T // GROUP
    G = _hgroup(nh)
    scale = HEAD_DIM ** -0.5

    def body(q_ref, *refs):
        k_refs = refs[:WIN_BLOCKS]
        v_refs = refs[WIN_BLOCKS:2 * WIN_BLOCKS]
        do_ref, lse_ref, bias_ref, dq_ref, delta_ref, db_ref = refs[2 * WIN_BLOCKS:]
        i, hg = pl.program_id(0), pl.program_id(1)

        @pl.when(hg == 0)
        def _():
            delta_ref[...] = jnp.zeros_like(delta_ref)

        @pl.when(i == 0)
        def _():
            for g in range(G):
                db_ref[hg * G + g] = jnp.zeros((GROUP, WIN), F32)

        for g in range(G):
            h = hg * G + g
            sl = slice(g * HEAD_DIM, (g + 1) * HEAD_DIM)
            ks = [kr[:, sl] for kr in k_refs]
            s = _a_logits(q_ref[:, sl], ks, bias_ref[h], i, scale)
            p = jnp.exp(s - _col_of(lse_ref[...], h))
            dov = do_ref[:, sl]
            dp = jnp.concatenate([lax.dot_general(dov, vr[:, sl], NT, preferred_element_type=F32) for vr in v_refs],
                                 axis=1)
            delta = jnp.sum(p * dp, axis=1, keepdims=True)
            ds = p * (dp - delta)
            db_ref[h] += ds
            dsb = ds.astype(BF16)
            dq = jnp.zeros((GROUP, HEAD_DIM), F32)
            for j in range(WIN_BLOCKS):
                dq = dq + jnp.dot(dsb[:, j * GROUP:(j + 1) * GROUP], ks[j], preferred_element_type=F32)
            dq_ref[:, sl] = (dq * scale).astype(BF16)
            _put_col(delta_ref, h, delta)

    q_spec, k_specs, v_specs = _a_specs_q(nh, G)
    blk = pl.BlockSpec((GROUP, G * HEAD_DIM), lambda i, hg: (i, hg))
    stat = pl.BlockSpec((GROUP, LANE), lambda i, hg: (i, 0))
    full_b = pl.BlockSpec((nh, GROUP, WIN), lambda i, hg: (0, 0, 0))
    return _pcall(
        body, name=name, grid=(ng, nh // G),
        in_specs=[q_spec] + k_specs + v_specs + [blk, stat, full_b],
        out_specs=[blk, stat, full_b],
        out_shape=[_sds((T, nh * HEAD_DIM), BF16), _sds((T, LANE), F32), _sds((nh, GROUP, WIN), F32)],
        sem=("arbitrary", "arbitrary"), jobs=jobs,
    )(qkv, *([qkv] * (2 * WIN_BLOCKS)), do, lse, bias2)


def _attn_a_dkv(name, qkv, do, lse, delta, bias2, nh, jobs=()):
    T = qkv.shape[0]
    ng = T // GROUP
    G = _hgroup(nh)
    ngrp = nh // G
    scale = HEAD_DIM ** -0.5
    nj = WIN_BLOCKS

    def body(k_ref, v_ref, *refs):
        q_refs = refs[:nj]
        do_refs = refs[nj:2 * nj]
        lse_refs = refs[2 * nj:3 * nj]
        dl_refs = refs[3 * nj:4 * nj]
        bias_ref, dk_ref, dv_ref = refs[4 * nj:]
        r, hg = pl.program_id(0), pl.program_id(1)
        for g in range(G):
            h = hg * G + g
            sl = slice(g * HEAD_DIM, (g + 1) * HEAD_DIM)
            kv, vv = k_ref[:, sl], v_ref[:, sl]
            bias = bias_ref[h]
            dk = jnp.zeros((GROUP, HEAD_DIM), F32)
            dv = jnp.zeros((GROUP, HEAD_DIM), F32)
            for j in range(nj):
                qv, dov = q_refs[j][:, sl], do_refs[j][:, sl]
                c0 = (nj - 1 - j) * GROUP
                s = lax.dot_general(qv, kv, NT, preferred_element_type=F32) * scale + bias[:, c0:c0 + GROUP]
                p = jnp.exp(s - _col_of(lse_refs[j][...], h))
                p = jnp.where(r + j <= ng - 1, p, 0.0)
                dp = lax.dot_general(dov, vv, NT, preferred_element_type=F32)
                ds = p * (dp - _col_of(dl_refs[j][...], h))
                dv = dv + lax.dot_general(p.astype(BF16), dov, TN, preferred_element_type=F32)
                dk = dk + lax.dot_general(ds.astype(BF16), qv, TN, preferred_element_type=F32)
            dk_ref[:, sl] = (dk * scale).astype(BF16)
            dv_ref[:, sl] = dv.astype(BF16)

    def qmap(j):
        return functools.partial(lambda r, hg, j: (jnp.minimum(r + j, ng - 1), hg), j=j)

    def smap(j):
        return functools.partial(lambda r, hg, j: (jnp.minimum(r + j, ng - 1), 0), j=j)

    blk = (GROUP, G * HEAD_DIM)
    in_specs = ([pl.BlockSpec(blk, lambda r, hg: (r, ngrp + hg)), pl.BlockSpec(blk, lambda r, hg: (r, 2 * ngrp + hg))]
                + [pl.BlockSpec(blk, qmap(j)) for j in range(nj)]
                + [pl.BlockSpec(blk, qmap(j)) for j in range(nj)]
                + [pl.BlockSpec((GROUP, LANE), smap(j)) for j in range(nj)]
                + [pl.BlockSpec((GROUP, LANE), smap(j)) for j in range(nj)]
                + [pl.BlockSpec((nh, GROUP, WIN), lambda r, hg: (0, 0, 0))])
    out = pl.BlockSpec(blk, lambda r, hg: (r, hg))
    return _pcall(
        body, name=name, grid=(ng, ngrp), in_specs=in_specs, out_specs=[out, out],
        out_shape=[_sds((T, nh * HEAD_DIM), BF16)] * 2,
        sem=("parallel", "parallel"), jobs=jobs,
    )(qkv, qkv, *([qkv] * nj), *([do] * nj), *([lse] * nj), *([delta] * nj), bias2)


def _fox_prep(name, f, b_f):
    T = f.shape[0]
    tb = _pick(T, (256, 128))

    def body(f_ref, b_ref, cum_ref, cumt_ref, carry_ref):
        @pl.when(pl.program_id(0) == 0)
        def _():
            carry_ref[...] = jnp.zeros_like(carry_ref)

        z = f_ref[...] + b_ref[...]
        logf = jnp.minimum(z, 0.0) - jnp.log(1.0 + jnp.exp(-jnp.abs(z)))
        row = lax.broadcasted_iota(jnp.int32, (tb, tb), 0)
        col = lax.broadcasted_iota(jnp.int32, (tb, tb), 1)
        tri = (row >= col).astype(BF16)
        acc = jnp.zeros((tb, LANE), F32)
        for piece in _split3(logf):
            acc = acc + jnp.dot(tri, piece, preferred_element_type=F32)
        cum = acc + carry_ref[...]
        cum_ref[...] = cum
        cumt_ref[...] = cum.T
        carry_ref[...] = cum_ref[pl.ds(tb - 1, 1), :]

    return pl.pallas_call(
        body, name=name, grid=(T // tb,),
        in_specs=[pl.BlockSpec((tb, LANE), lambda i: (i, 0)), pl.BlockSpec((1, LANE), lambda i: (0, 0))],
        out_specs=[pl.BlockSpec((tb, LANE), lambda i: (i, 0)), pl.BlockSpec((LANE, tb), lambda i: (0, i))],
        out_shape=[_sds((T, LANE), F32), _sds((LANE, T), F32)],
        scratch_shapes=[pltpu.VMEM((1, LANE), F32)],
        compiler_params=_cp(("arbitrary",)),
    )(f, b_f)


def _fox_blk(T):
    return _pick(T, (256, 128))


def _fox_group(nh):
    return _pick(nh, (4, 2, 1))


def _fox_allowed(i, j, tq, tk):
    diff = lax.broadcasted_iota(jnp.int32, (tq, tk), 1) - lax.broadcasted_iota(jnp.int32, (tq, tk), 0)
    return diff <= (i - j) * tq


def _fox_fwd(name, qkv, cum, cumt, nh, jobs=()):
    T = qkv.shape[0]
    tq = tk = _fox_blk(T)
    G = _fox_group(nh)
    ngrp = nh // G
    scale = HEAD_DIM ** -0.5

    def body(q_ref, k_ref, v_ref, cum_ref, cumt_ref, o_ref, lse_ref):
        i, hg = pl.program_id(0), pl.program_id(1)

        @pl.when(hg == 0)
        def _():
            lse_ref[...] = jnp.zeros_like(lse_ref)

        sls = [slice(g * HEAD_DIM, (g + 1) * HEAD_DIM) for g in range(G)]
        qs = [q_ref[:, sl] for sl in sls]
        cqs = [_col_of(cum_ref[...], hg * G + g) for g in range(G)]

        def step(j, carry):
            k0 = pl.multiple_of(j * tk, tk)
            ok = _fox_allowed(i, j, tq, tk)
            out = []
            for g in range(G):
                m, l, acc = carry[g]
                kj = k_ref[pl.ds(k0, tk), sls[g]]
                vj = v_ref[pl.ds(k0, tk), sls[g]]
                ck = cumt_ref[pl.ds(hg * G + g, 1), pl.ds(k0, tk)]
                s = lax.dot_general(qs[g], kj, NT, preferred_element_type=F32) * scale + (cqs[g] - ck)
                s = jnp.where(ok, s, NEG_INF)
                m_new = jnp.maximum(m, jnp.max(s, axis=1, keepdims=True))
                alpha = jnp.exp(m - m_new)
                p = jnp.exp(s - m_new)
                l = alpha * l + jnp.sum(p, axis=1, keepdims=True)
                acc = alpha * acc + jnp.dot(p.astype(BF16), vj, preferred_element_type=F32)
                out.append((m_new, l, acc))
            return tuple(out)

        one = (jnp.full((tq, 1), NEG_INF, F32), jnp.zeros((tq, 1), F32), jnp.zeros((tq, HEAD_DIM), F32))
        res = lax.fori_loop(0, i + 1, step, tuple(one for _ in range(G)))
        for g in range(G):
            m, l, acc = res[g]
            o_ref[:, sls[g]] = (acc / l).astype(BF16)
            _put_col(lse_ref, hg * G + g, m + jnp.log(l))

    GW = G * HEAD_DIM
    return _pcall(
        body, name=name, grid=(T // tq, ngrp),
        in_specs=[pl.BlockSpec((tq, GW), lambda i, hg: (i, 3 * ngrp + hg)),
                  pl.BlockSpec((T, GW), lambda i, hg: (0, 4 * ngrp + hg)),
                  pl.BlockSpec((T, GW), lambda i, hg: (0, 5 * ngrp + hg)),
                  pl.BlockSpec((tq, LANE), lambda i, hg: (i, 0)),
                  pl.BlockSpec((LANE, T), lambda i, hg: (0, 0))],
        out_specs=[pl.BlockSpec((tq, GW), lambda i, hg: (i, hg)), pl.BlockSpec((tq, LANE), lambda i, hg: (i, 0))],
        out_shape=[_sds((T, nh * HEAD_DIM), BF16), _sds((T, LANE), F32)],
        sem=("parallel", "arbitrary"), jobs=jobs,
    )(qkv, qkv, qkv, cum, cumt)


def _fox_dq(name, qkv, do, lse, cum, cumt, nh, jobs=()):
    T = qkv.shape[0]
    tq = tk = _fox_blk(T)
    G = _fox_group(nh)
    ngrp = nh // G
    GW = G * HEAD_DIM
    scale = HEAD_DIM ** -0.5

    def body(q_ref, k_ref, v_ref, do_ref, lse_ref, cum_ref, cumt_ref, dq_ref, delta_ref):
        i, hg = pl.program_id(0), pl.program_id(1)

        @pl.when(hg == 0)
        def _():
            delta_ref[...] = jnp.zeros_like(delta_ref)

        sls = [slice(g * HEAD_DIM, (g + 1) * HEAD_DIM) for g in range(G)]
        qs = [q_ref[:, sl] for sl in sls]
        dos = [do_ref[:, sl] for sl in sls]
        cqs = [_col_of(cum_ref[...], hg * G + g) for g in range(G)]
        lses = [_col_of(lse_ref[...], hg * G + g) for g in range(G)]

        def p_dp(j, g, ok):
            k0 = pl.multiple_of(j * tk, tk)
            kj = k_ref[pl.ds(k0, tk), sls[g]]
            vj = v_ref[pl.ds(k0, tk), sls[g]]
            ck = cumt_ref[pl.ds(hg * G + g, 1), pl.ds(k0, tk)]
            s = lax.dot_general(qs[g], kj, NT, preferred_element_type=F32) * scale + (cqs[g] - ck)
            p = jnp.exp(jnp.where(ok, s, NEG_INF) - lses[g])
            return p, lax.dot_general(dos[g], vj, NT, preferred_element_type=F32), kj

        def sweep_delta(j, deltas):
            ok = _fox_allowed(i, j, tq, tk)
            out = []
            for g in range(G):
                p, dp, _ = p_dp(j, g, ok)
                out.append(deltas[g] + jnp.sum(p * dp, axis=1, keepdims=True))
            return tuple(out)

        deltas = lax.fori_loop(0, i + 1, sweep_delta, tuple(jnp.zeros((tq, 1), F32) for _ in range(G)))

        def sweep_dq(j, dqs):
            ok = _fox_allowed(i, j, tq, tk)
            out = []
            for g in range(G):
                p, dp, kj = p_dp(j, g, ok)
                ds = p * (dp - deltas[g])
                out.append(dqs[g] + jnp.dot(ds.astype(BF16), kj, preferred_element_type=F32))
            return tuple(out)

        dqs = lax.fori_loop(0, i + 1, sweep_dq, tuple(jnp.zeros((tq, HEAD_DIM), F32) for _ in range(G)))
        for g in range(G):
            dq_ref[:, sls[g]] = (dqs[g] * scale).astype(BF16)
            _put_col(delta_ref, hg * G + g, deltas[g])

    blk = pl.BlockSpec((tq, GW), lambda i, hg: (i, hg))
    stat = pl.BlockSpec((tq, LANE), lambda i, hg: (i, 0))
    return _pcall(
        body, name=name, grid=(T // tq, ngrp),
        in_specs=[pl.BlockSpec((tq, GW), lambda i, hg: (i, 3 * ngrp + hg)),
                  pl.BlockSpec((T, GW), lambda i, hg: (0, 4 * ngrp + hg)),
                  pl.BlockSpec((T, GW), lambda i, hg: (0, 5 * ngrp + hg)),
                  blk, stat, stat, pl.BlockSpec((LANE, T), lambda i, hg: (0, 0))],
        out_specs=[blk, stat],
        out_shape=[_sds((T, nh * HEAD_DIM), BF16), _sds((T, LANE), F32)],
        sem=("parallel", "arbitrary"), jobs=jobs,
    )(qkv, qkv, qkv, do, lse, cum, cumt)


def _fox_dkv(name, qkv, do, lse, delta, cum, cumt, nh, jobs=()):
    T = qkv.shape[0]
    tq = tk = _fox_blk(T)
    nq = T // tq
    G = _fox_group(nh)
    ngrp = nh // G
    GW = G * HEAD_DIM
    scale = HEAD_DIM ** -0.5

    def body(k_ref, v_ref, q_ref, do_ref, lse_ref, dl_ref, cum_ref, cumt_ref, dk_ref, dv_ref, dc_ref):
        j, hg = pl.program_id(0), pl.program_id(1)

        @pl.when(hg == 0)
        def _():
            dc_ref[...] = jnp.zeros_like(dc_ref)

        sls = [slice(g * HEAD_DIM, (g + 1) * HEAD_DIM) for g in range(G)]
        kjs = [k_ref[:, sl] for sl in sls]
        vjs = [v_ref[:, sl] for sl in sls]
        k0 = pl.multiple_of(j * tk, tk)
        cks = [cumt_ref[pl.ds(hg * G + g, 1), pl.ds(k0, tk)] for g in range(G)]

        def step(i, carry):
            q0 = pl.multiple_of(i * tq, tq)
            ok = _fox_allowed(i, j, tq, tk)
            cum_i, lse_i, dl_i = cum_ref[pl.ds(q0, tq), :], lse_ref[pl.ds(q0, tq), :], dl_ref[pl.ds(q0, tq), :]
            out = []
            for g in range(G):
                dk, dv, dc = carry[g]
                h = hg * G + g
                qi = q_ref[pl.ds(q0, tq), sls[g]]
                doi = do_ref[pl.ds(q0, tq), sls[g]]
                s = lax.dot_general(qi, kjs[g], NT, preferred_element_type=F32) * scale + (_col_of(cum_i, h) - cks[g])
                p = jnp.exp(jnp.where(ok, s, NEG_INF) - _col_of(lse_i, h))
                dp = lax.dot_general(doi, vjs[g], NT, preferred_element_type=F32)
                ds = p * (dp - _col_of(dl_i, h))
                dv = dv + lax.dot_general(p.astype(BF16), doi, TN, preferred_element_type=F32)
                dk = dk + lax.dot_general(ds.astype(BF16), qi, TN, preferred_element_type=F32)
                dc = dc - jnp.sum(ds, axis=0, keepdims=True)
                out.append((dk, dv, dc))
            return tuple(out)

        one = (jnp.zeros((tk, HEAD_DIM), F32), jnp.zeros((tk, HEAD_DIM), F32), jnp.zeros((1, tk), F32))
        res = lax.fori_loop(j, nq, step, tuple(one for _ in range(G)))
        sub = lax.broadcasted_iota(jnp.int32, (LANE, tk), 0)
        dc_all = dc_ref[...]
        for g in range(G):
            dk, dv, dc = res[g]
            dk_ref[:, sls[g]] = (dk * scale).astype(BF16)
            dv_ref[:, sls[g]] = dv.astype(BF16)
            dc_all = jnp.where(sub == hg * G + g, dc, dc_all)
        dc_ref[...] = dc_all

    whole = lambda c: pl.BlockSpec((T, GW), c)
    stat = pl.BlockSpec((T, LANE), lambda j, hg: (0, 0))
    out = pl.BlockSpec((tk, GW), lambda j, hg: (j, hg))
    return _pcall(
        body, name=name, grid=(T // tk, ngrp),
        in_specs=[pl.BlockSpec((tk, GW), lambda j, hg: (j, 4 * ngrp + hg)),
                  pl.BlockSpec((tk, GW), lambda j, hg: (j, 5 * ngrp + hg)),
                  whole(lambda j, hg: (0, 3 * ngrp + hg)), whole(lambda j, hg: (0, hg)),
                  stat, stat, stat, pl.BlockSpec((LANE, T), lambda j, hg: (0, 0))],
        out_specs=[out, out, pl.BlockSpec((LANE, tk), lambda j, hg: (0, j))],
        out_shape=[_sds((T, nh * HEAD_DIM), BF16)] * 2 + [_sds((LANE, T), F32)],
        sem=("parallel", "arbitrary"), jobs=jobs,
    )(qkv, qkv, qkv, do, lse, delta, cum, cumt)


def _fox_post(name, dcumt, f, b_f):
    T = f.shape[0]
    tb = _pick(T, (256, 128))
    nb = T // tb

    def body(dc_ref, f_ref, b_ref, df_ref, gb_ref, carry_ref):
        @pl.when(pl.program_id(0) == 0)
        def _():
            carry_ref[...] = jnp.zeros_like(carry_ref)
            gb_ref[...] = jnp.zeros_like(gb_ref)

        dc = dc_ref[...]
        row = lax.broadcasted_iota(jnp.int32, (tb, tb), 0)
        col = lax.broadcasted_iota(jnp.int32, (tb, tb), 1)
        tri = (row >= col).astype(BF16)
        acc = jnp.zeros((LANE, tb), F32)
        for piece in _split3(dc):
            acc = acc + jnp.dot(piece, tri, preferred_element_type=F32)
        dlogf = (acc + carry_ref[...]).T
        carry_ref[...] += jnp.sum(dc, axis=1, keepdims=True)
        z = f_ref[...] + b_ref[...]
        df = dlogf * _sigmoid(-z)
        df_ref[...] = df.astype(BF16)
        gb_ref[...] += jnp.sum(df, axis=0, keepdims=True)

    return pl.pallas_call(
        body, name=name, grid=(nb,),
        in_specs=[pl.BlockSpec((LANE, tb), lambda g: (0, nb - 1 - g)),
                  pl.BlockSpec((tb, LANE), lambda g: (nb - 1 - g, 0)),
                  pl.BlockSpec((1, LANE), lambda g: (0, 0))],
        out_specs=[pl.BlockSpec((tb, LANE), lambda g: (nb - 1 - g, 0)), pl.BlockSpec((1, LANE), lambda g: (0, 0))],
        out_shape=[_sds((T, LANE), BF16), _sds((1, LANE), F32)],
        scratch_shapes=[pltpu.VMEM((LANE, 1), F32)],
        compiler_params=_cp(("arbitrary",)),
    )(dcumt, f, b_f)


def _rel_tables(n_rel):
    max_rel = (n_rel - 1) // 2
    nj = GROUP + WIN - 1
    onehot = np.zeros((n_rel, nj), np.float32)
    for j in range(nj):
        dist = (WIN - 1) - j
        onehot[int(np.clip(dist, -max_rel, max_rel)) + max_rel, j] = 1.0
    a = np.arange(GROUP)[:, None]
    kb = np.arange(WIN)[None, :]
    lo = CHUNK * (a // CHUNK)
    inband = (kb >= lo) & (kb < lo + BAND)
    return onehot, inband


def _bias2_of(rel_bias, onehot, inband):
    bv = jnp.dot(rel_bias, jnp.asarray(onehot), precision=lax.Precision.HIGHEST)
    rows = [bv[:, GROUP - 1 - a:GROUP - 1 - a + WIN] for a in range(GROUP)]
    toe = jnp.stack(rows, axis=1)
    return jnp.where(jnp.asarray(inband)[None], toe, NEG_INF)


def _rel_grad_of(dbias2, onehot):
    nj = GROUP + WIN - 1
    dbv = sum(jnp.pad(dbias2[:, a, :], ((0, 0), (GROUP - 1 - a, nj - WIN - (GROUP - 1 - a)))) for a in range(GROUP))
    return jnp.dot(dbv, jnp.asarray(onehot).T, precision=lax.Precision.HIGHEST)


def kernel(x, g_mix, w_in, b_f, b_gate, rel_bias, w_branch_a, w_branch_b, w_out, g_ffn, w_gate_ffn, w_up_ffn, w_down_ffn, g_final, loss_target, m_g_mix, m_w_in, m_b_f, m_b_gate, m_rel_bias, m_w_branch_a, m_w_branch_b, m_w_out, m_g_ffn, m_w_gate_ffn, m_w_up_ffn, m_w_down_ffn, m_g_final, v_g_mix, v_w_in, v_b_f, v_b_gate, v_rel_bias, v_w_branch_a, v_w_branch_b, v_w_out, v_g_ffn, v_w_gate_ffn, v_w_up_ffn, v_w_down_ffn, v_g_final):
    T, D = x.shape[1], x.shape[2]
    Ls = w_in.shape[2]
    W = w_branch_a.shape[1]
    nh = W // HEAD_DIM
    nhb = b_f.shape[1]
    assert w_branch_b.shape[1] == W and nhb == nh and rel_bias.shape[1] == nh
    W6 = 6 * W
    Fl = w_gate_ffn.shape[2]
    Fp = -(-Fl // LANE) * LANE
    n_rel = rel_bias.shape[2]
    chip = 2 * lax.axis_index("x") + lax.axis_index("y")
    lay, nbw, nmain = _in_layout(D, W6, nhb, Ls)
    onehot, inband = _rel_tables(n_rel)

    xs, tgt = x[0], loss_target[0]

    win_f32 = lax.switch(chip, [functools.partial(_to_window, lay_k=lay[k], nbw=nbw) for k in range(N_CHIPS)], w_in[0])
    pad_c = lambda w: jnp.pad(w, ((0, 0), (0, Fp - Fl)))
    pad_r = lambda w: jnp.pad(w, ((0, Fp - Fl), (0, 0)))
    sh_in = _cast_bf16("cast_w_in", win_f32, chip)
    sh_a = _cast_bf16("cast_w_a", w_branch_a[0], chip)
    sh_b = _cast_bf16("cast_w_b", w_branch_b[0], chip)
    sh_o = _cast_bf16("cast_w_out", w_out[0], chip)
    sh_g = _cast_bf16("cast_w_gate", pad_c(w_gate_ffn[0]), chip)
    sh_u = _cast_bf16("cast_w_up", pad_c(w_up_ffn[0]), chip)
    sh_d = _cast_bf16("cast_w_down", pad_r(w_down_ffn[0]), chip)
    (wins,) = _allgather("ag_w_in", [sh_in])
    wc = _assemble_in("assemble_w_in", wins, lay, nbw, nmain)

    h1, r1 = _rms_fwd("rms1", xs, g_mix)
    qkv, ((wa_g, wb_g, wo_g),) = _mm_nn("proj_qkv", h1, wc, BF16, b_col0=0, n=W6, tm=1024,
                                        jobs=[_job_gather_ici([sh_a, sh_b, sh_o])])
    gates, ((wa_g, wb_g, wo_g), (wg_g,)) = _mm_nn(
        "proj_gates", h1, wc, BF16, b_col0=W6, n=2 * D, tm=1024,
        jobs=[_job_gather_d2d([wa_g, wb_g, wo_g]), _job_gather_ici([sh_g], part=(0, 2))])
    fl = _mm_nn("proj_f", h1, wc, F32, b_col0=nmain, n=LANE, tn=LANE)
    bias2 = _bias2_of(rel_bias[0], onehot, inband)
    bf_pad = jnp.pad(b_f, ((0, 0), (0, LANE - nhb)))
    (o_a, lse_a), ((wg_g,),) = _attn_a_fwd("attn_a_fwd", qkv, bias2, nh, jobs=[_job_gather_ici([wg_g], part=(1, 2))])
    cum, cumt = _fox_prep("fox_prep", fl, bf_pad)
    (o_b, lse_b), ((wu_g,), (wg_g,)) = _fox_fwd("fox_fwd", qkv, cum, cumt, nh,
                                                jobs=[_job_gather_ici([sh_u]), _job_gather_d2d([wg_g])])
    u_a = _mm_nn("branch_a", o_a, wa_g, BF16, tm=1024)
    u_b = _mm_nn("branch_b", o_b, wb_g, BF16, tm=1024)
    merged = _merge_fwd("merge", gates, u_a, u_b, b_gate)
    wo_full = wo_g.reshape(D, D)
    x1, ((wu_g,),) = _mm_nn("out_proj", merged, wo_full, F32, residual=xs, tm=1024, tn=_pick(D, (1024, 512, 256, 128)),
                            jobs=[_job_gather_d2d([wu_g])])
    h2, r2 = _rms_fwd("rms2", x1, g_ffn)

    tm_f = _pick(T, (1024, 512, 256, 128))
    tn_f = _pick(Fp, (1408, 1024, 512, 256, 128))
    tk_f = _pick(D, (1024, 512, 256, 128))
    nps_f = Fp // tn_f

    def swiglu_ep(accs, e_refs, o_refs):
        g, u = accs
        o_refs[0][...] = g.astype(BF16)
        o_refs[1][...] = u.astype(BF16)
        o_refs[2][...] = (g * _sigmoid(g) * u).astype(BF16)

    hid_spec = pl.BlockSpec((tm_f, tn_f), lambda i, j, k: (i, j))
    wcol_spec = pl.BlockSpec((None, tk_f, tn_f), lambda i, j, k: (j // nps_f, k, j % nps_f))
    (gate, up, hidden), ((wd_g,),) = _mm(
        "ffn_up", "nn", [h2], [pl.BlockSpec((tm_f, tk_f), lambda i, j, k: (i, k))], [wg_g, wu_g], [wcol_spec, wcol_spec],
        [(0, 0, 0), (0, 1, 1)], 2, (T // tm_f, N_CHIPS * Fp // tn_f, D // tk_f), tm_f, tn_f,
        [_sds((T, N_CHIPS * Fp), BF16)] * 3, [hid_spec] * 3, swiglu_ep, jobs=[_job_gather_ici([sh_d])])
    ((wd_g,),) = _comm_only("ag_w_down_d2d", [_job_gather_d2d([wd_g])])
    wd_full = wd_g.reshape(N_CHIPS * Fp, D)
    x2 = _mm_nn("ffn_down", hidden, wd_full, F32, residual=x1, tm=1024, tn=_pick(D, (1024, 512, 256, 128)),
                tk=_pick(N_CHIPS * Fp, (1408, 1024, 512, 256, 128)))

    dx2, dx2b, loss_part, gg_final = _final_loss_bwd("final_loss", x2, tgt, g_final.reshape(1, D))

    def swiglu_bwd_ep(accs, e_refs, o_refs):
        dh = accs[0]
        g = e_refs[0][...].astype(F32)
        u = e_refs[1][...].astype(F32)
        sg = _sigmoid(g)
        o_refs[0][...] = (dh * u * (sg * (1.0 + g * (1.0 - sg)))).astype(BF16)
        o_refs[1][...] = (dh * (g * sg)).astype(BF16)

    tk_b = _pick(D, (1024, 512, 256, 128))
    core = lax.axis_index("c")
    (dgate, dup), _ = _mm(
        "ffn_down_bwd", "nt", [dx2b], [pl.BlockSpec((tm_f, tk_b), lambda i, j, k: (i, k))],
        [wd_full], [pl.BlockSpec((tn_f, tk_b), lambda i, j, k: (j, k))], [(0, 0, 0)], 1,
        (T // tm_f, N_CHIPS * Fp // tn_f, D // tk_b), tm_f, tn_f,
        [_sds((T, N_CHIPS * Fp), BF16)] * 2, [hid_spec] * 2, swiglu_bwd_ep,
        extra=[gate, up], extra_specs=[hid_spec, hid_spec])
    dwd = _mm_tn("dw_down", hidden, dx2b, BF16, tm=_pick(N_CHIPS * Fp, (1408, 1024, 512, 256, 128)))
    dwd = dwd.reshape(N_CHIPS, Fp, D)
    dh2, ((sib_d,),) = _mm_nt("ffn_up_bwd", [dgate, dup], [wg_g, wu_g], F32, tm=1024, jobs=[_job_sibling([dwd])])
    dwg = _mm_tn("dw_gate", h2, dgate, BF16, slots=N_CHIPS)
    dwu = _mm_tn("dw_up", h2, dup, BF16, slots=N_CHIPS)
    dx1, dx1b, gg_ffn = _rms_bwd("rms2_bwd", [dh2], x1, r2, g_ffn, dx2, True)
    part_d = _add_bf16("rs_add_down", dwd, core, sib_d)

    dmerged, ((sib_g, sib_u),) = _mm_nt("out_proj_bwd", [dx1b], [wo_full], BF16, tm=1024,
                                        jobs=[_job_sibling([dwg, dwu])])
    dwo = _mm_tn("dw_out", merged, dx1b, BF16).reshape(N_CHIPS, D // N_CHIPS, D)
    du_a, du_b, dga, dgb, gbg_a, gbg_b = _merge_bwd("merge_bwd", dmerged, gates, u_a, u_b, b_gate)
    part_g = _add_bf16("rs_add_gate", dwg, core, sib_g)
    part_u = _add_bf16("rs_add_up", dwu, core, sib_u)
    do_a = _mm_nt("branch_a_bwd", [du_a], [wa_g], BF16, tm=1024)
    do_b = _mm_nt("branch_b_bwd", [du_b], [wb_g], BF16, tm=1024)
    dwa = _mm_tn("dw_a", o_a, du_a, BF16, slots=N_CHIPS)
    dwb = _mm_tn("dw_b", o_b, du_b, BF16, slots=N_CHIPS)

    (dq_a, delta_a, dbias2), ((got_d,), (sib_a, sib_b, sib_o)) = _attn_a_dq(
        "attn_a_dq", qkv, do_a, lse_a, bias2, nh,
        jobs=[_job_scatter([part_d], part=(0, 2)), _job_sibling([dwa, dwb, dwo])])
    part_a = _add_bf16("rs_add_a", dwa, core, sib_a)
    part_b = _add_bf16("rs_add_b", dwb, core, sib_b)
    part_o = _add_bf16("rs_add_out", dwo, core, sib_o)
    (dk_a, dv_a), ((got_d,), (got_g,)) = _attn_a_dkv(
        "attn_a_dkv", qkv, do_a, lse_a, delta_a, bias2, nh,
        jobs=[_job_scatter([part_d], part=(1, 2), into=[got_d]), _job_scatter([part_g], part=(0, 2))])
    full_d = _sum4("rs_sum_down", got_d, part_d, chip, core)
    (dq_b, delta_b), ((got_g,), (got_u,)) = _fox_dq(
        "fox_dq", qkv, do_b, lse_b, cum, cumt, nh,
        jobs=[_job_scatter([part_g], part=(1, 2), into=[got_g]), _job_scatter([part_u])])
    full_g = _sum4("rs_sum_gate", got_g, part_g, chip, core)
    full_u = _sum4("rs_sum_up", got_u, part_u, chip, core)
    (dk_b, dv_b, dcumt), ((got_a, got_b, got_o),) = _fox_dkv(
        "fox_dkv", qkv, do_b, lse_b, delta_b, cum, cumt, nh, jobs=[_job_scatter([part_a, part_b, part_o])])
    full_a = _sum4("rs_sum_a", got_a, part_a, chip, core)
    full_b = _sum4("rs_sum_b", got_b, part_b, chip, core)
    full_o = _sum4("rs_sum_out", got_o, part_o, chip, core)
    df, gbf = _fox_post("fox_post", dcumt, fl, bf_pad)

    dqkv = jnp.concatenate([dq_a, dk_a, dv_a, dq_b, dk_b, dv_b], axis=1)
    dgates = jnp.concatenate([dga, dgb], axis=1)
    dwc_q, ((g_d, g_g, g_u, g_a, g_b, g_o),) = _mm_tn(
        "dw_in_qkv", h1, dqkv, BF16, jobs=[_job_swap([full_d, full_g, full_u, full_a, full_b, full_o])])
    dwc_g = _mm_tn("dw_in_gates", h1, dgates, BF16)
    dwc_f = _mm_tn("dw_in_f", h1, df, BF16, tn=LANE)
    dwc = jnp.concatenate([dwc_q, dwc_g, dwc_f], axis=1)
    zeros_blk = jnp.zeros((D, LANE), BF16)
    win_parts = []
    for k in range(N_CHIPS):
        cols = [dwc[:, lay[k]["s"] * LANE:lay[k]["e"] * LANE]]
        nb = lay[k]["e"] - lay[k]["s"]
        if lay[k]["f"]:
            cols.append(dwc[:, nmain:nmain + LANE])
            nb += 1
        cols += [zeros_blk] * (nbw - nb)
        win_parts.append(jnp.concatenate(cols, axis=1) if len(cols) > 1 else cols[0])
    dwin = jnp.stack(win_parts, axis=0)
    big = {}

    def adamw(nm, w, g, m, v, jobs=()):
        (d, mn, vn, go), jouts = _adamw(f"adamw_{nm}", w[0], g, m[0], v[0], jobs=jobs)
        big[nm] = (go[None], d[None], mn[None], vn[None])
        return jouts

    ((sib_in,),) = adamw("w_gate_ffn", w_gate_ffn, g_g, m_w_gate_ffn, v_w_gate_ffn, jobs=[_job_sibling([dwin])])
    part_in = _add_bf16("rs_add_in", dwin, core, sib_in)
    dh, ((got_in,),) = _mm_nt("proj_qkv_bwd", [dqkv], [wc], F32, k0_list=[0], tk=_pick(W6, (1024, 512, 256, 128)), tm=1024,
                              jobs=[_job_scatter([part_in], part=(0, 2))])
    dh, ((got_in,),) = _mm_nt("proj_gates_bwd", [dgates], [wc], F32, k0_list=[W6], tm=1024,
                              tk=_pick(math_gcd(W6, 2 * D), (1024, 512, 256, 128)), residual=dh,
                              jobs=[_job_scatter([part_in], part=(1, 2), into=[got_in])])
    full_in = _sum4("rs_sum_in", got_in, part_in, chip, core)
    dh, ((g_win,),) = _mm_nt("proj_f_bwd", [df], [wc], F32, k0_list=[nmain], tk=LANE, residual=dh,
                             jobs=[_job_swap([full_in])])
    grad_x, gg_mix = _rms_bwd("rms1_bwd", [dh], xs, r1, g_mix, dx1, False)
    g_in = lax.switch(chip, [functools.partial(_from_window, lay_k=lay[k]) for k in range(N_CHIPS)], g_win)

    for nm, w, g, m, v in (("w_in", w_in, g_in, m_w_in, v_w_in), ("w_branch_a", w_branch_a, g_a, m_w_branch_a, v_w_branch_a),
                           ("w_branch_b", w_branch_b, g_b, m_w_branch_b, v_w_branch_b), ("w_out", w_out, g_o, m_w_out, v_w_out),
                           ("w_up_ffn", w_up_ffn, g_u, m_w_up_ffn, v_w_up_ffn),
                           ("w_down_ffn", w_down_ffn, g_d, m_w_down_ffn, v_w_down_ffn)):
        adamw(nm, w, g, m, v)

    g_rel = _rel_grad_of(dbias2, onehot)
    small = [("loss", loss_part[:, :1], None, None, None),
             ("g_mix", gg_mix, g_mix, m_g_mix, v_g_mix), ("b_f", gbf[:, :nhb], b_f, m_b_f, v_b_f),
             ("b_gate", jnp.concatenate([gbg_a, gbg_b], axis=1), b_gate, m_b_gate, v_b_gate),
             ("rel_bias", g_rel, rel_bias, m_rel_bias, v_rel_bias), ("g_ffn", gg_ffn, g_ffn, m_g_ffn, v_g_ffn),
             ("g_final", gg_final, g_final, m_g_final, v_g_final)]
    sizes = [int(np.prod(s[1].shape)) for s in small]
    total = sum(sizes)
    npad = -(-total // 1024) * 1024

    def pack(arrs):
        flat = jnp.concatenate([a.reshape(-1).astype(F32) for a in arrs])
        return jnp.pad(flat, (0, npad - total)).reshape(8, npad // 8)

    zero1 = jnp.zeros((1,), F32)
    g_all = _small_allreduce("small_allreduce", pack([s[1] for s in small]))
    w_s = pack([zero1 if s[2] is None else s[2] for s in small])
    m_s = pack([zero1 if s[3] is None else s[3] for s in small])
    v_s = pack([zero1 + 1.0 if s[4] is None else s[4] for s in small])
    (d_s, mn_s, vn_s, _), _ = _adamw("adamw_small", w_s, g_all, m_s, v_s)

    def unpack(packed):
        flat = packed.reshape(-1)
        out, pos = {}, 0
        for s, n in zip(small, sizes):
            if s[2] is not None:
                out[s[0]] = flat[pos:pos + n].reshape(s[2].shape)
            else:
                out[s[0]] = flat[pos:pos + n].reshape(())
            pos += n
        return out

    gs, ds, ms, vs = unpack(g_all), unpack(d_s), unpack(mn_s), unpack(vn_s)
    order = ["g_mix", "w_in", "b_f", "b_gate", "rel_bias", "w_branch_a", "w_branch_b", "w_out", "g_ffn",
             "w_gate_ffn", "w_up_ffn", "w_down_ffn", "g_final"]
    res = [[], [], [], []]
    for nm in order:
        four = big[nm] if nm in big else (gs[nm], ds[nm], ms[nm], vs[nm])
        for q in range(4):
            res[q].append(four[q])
    return (gs["loss"], grad_x[None], *res[0], *res[1], *res[2], *res[3])


def math_gcd(a, b):
    while b:
        a, b = b, a % b
    return a
```

```python
import functools

import numpy as np
import jax
import jax.numpy as jnp
from jax import lax
from jax.experimental import pallas as pl
from jax.experimental.pallas import tpu as pltpu

F32 = jnp.float32
BF16 = jnp.bfloat16
LANE = 128
HEAD_DIM = 128
CHUNK = 64
LEFT_CHUNKS = 8
GROUP = 128
WIN_BLOCKS = 5
WIN = WIN_BLOCKS * GROUP
BAND = (LEFT_CHUNKS + 1) * CHUNK
RMS_EPS = 1e-6
NEG_INF = -1e30
ADAM_LR = 0.001
ADAM_B1 = 0.9
ADAM_B2 = 0.999
ADAM_EPS = 1e-08
ADAM_WD = 0.01
ADAM_STEP = 10
N_CHIPS = 4
MESH = pl.DeviceIdType.MESH
VMEM_LIMIT = 52 * 1024 * 1024
ANY = pl.BlockSpec(memory_space=pl.ANY)

NN = (((1,), (0,)), ((), ()))
NT = (((1,), (1,)), ((), ()))
TN = (((0,), (0,)), ((), ()))


def _cp(sem):
    return pltpu.CompilerParams(dimension_semantics=sem, vmem_limit_bytes=VMEM_LIMIT)


def _sds(shape, dtype):
    return jax.ShapeDtypeStruct(shape, dtype)


def _pick(n, prefs):
    for p in prefs:
        if n % p == 0:
            return p
    return n


def _sigmoid(v):
    return 1.0 / (1.0 + jnp.exp(-v))


def _split3(v):
    hi = v.astype(BF16)
    r1 = v - hi.astype(F32)
    mid = r1.astype(BF16)
    lo = (r1 - mid.astype(F32)).astype(BF16)
    return hi, mid, lo


def _col_of(blk, h):
    lane = lax.broadcasted_iota(jnp.int32, blk.shape, 1)
    return jnp.sum(jnp.where(lane == h, blk, 0.0), axis=1, keepdims=True)


def _put_col(ref, h, col):
    lane = lax.broadcasted_iota(jnp.int32, ref.shape, 1)
    ref[...] = jnp.where(lane == h, col, ref[...])


def _mm(name, mode, a_list, a_specs, b_list, b_specs, pairs, n_acc, grid, tm, tn,
        out_shapes, out_specs, epilogue, extra=(), extra_specs=(), jobs=()):
    n_a, n_b, n_e, n_o = len(a_list), len(b_list), len(extra), len(out_shapes)
    nk = grid[2]
    dn = {"nn": NN, "nt": NT, "tn": TN}[mode]

    def body(*refs):
        a_refs = refs[:n_a]
        b_refs = refs[n_a:n_a + n_b]
        e_refs = refs[n_a + n_b:n_a + n_b + n_e]
        o_refs = refs[n_a + n_b + n_e:n_a + n_b + n_e + n_o]
        acc_refs = refs[n_a + n_b + n_e + n_o:]
        k = pl.program_id(2)

        @pl.when(k == 0)
        def _():
            for acc in acc_refs:
                acc[...] = jnp.zeros_like(acc)

        for ai, bi, ci in pairs:
            acc_refs[ci][...] += lax.dot_general(a_refs[ai][...], b_refs[bi][...], dn,
                                                 preferred_element_type=F32)

        @pl.when(k == nk - 1)
        def _():
            epilogue([acc[...] for acc in acc_refs], e_refs, o_refs)

    return _pcall(
        body, name=name, grid=grid,
        in_specs=list(a_specs) + list(b_specs) + list(extra_specs),
        out_specs=list(out_specs), out_shape=list(out_shapes),
        scratch_shapes=[pltpu.VMEM((tm, tn), F32) for _ in range(n_acc)],
        sem=("parallel", "parallel", "arbitrary"), jobs=jobs,
    )(*a_list, *b_list, *extra)


def _one(res, jobs):
    outs, jouts = res
    return (outs[0], jouts) if jobs else outs[0]


def _store(dtype):
    def ep(accs, e_refs, o_refs):
        o_refs[0][...] = accs[0].astype(dtype)
    return ep


def _mm_nn(name, a, b, out_dtype, *, b_col0=0, n=None, tm=512, tn=None, tk=None, residual=None, jobs=()):
    M, K = a.shape
    if b.ndim == 3:
        Ns = b.shape[2]
        n = b.shape[0] * Ns
        tn = tn or _pick(Ns, (1408, 1024, 512, 256, 128))
        nps = Ns // tn
        b_spec = pl.BlockSpec((None, tk or _pick(K, (1024, 512, 256, 128)), tn),
                              lambda i, j, k: (j // nps, k, j % nps))
    else:
        n = n or b.shape[1]
        tn = tn or _pick(math_gcd(n, b_col0) if b_col0 else n, (2048, 1024, 512, 256, 128))
        assert b_col0 % tn == 0 and n % tn == 0
        c0 = b_col0 // tn
        b_spec = pl.BlockSpec((tk or _pick(K, (1024, 512, 256, 128)), tn), lambda i, j, k: (k, c0 + j))
    tk = tk or _pick(K, (1024, 512, 256, 128))
    tm = _pick(M, (tm, 256, 128))
    grid = (M // tm, n // tn, K // tk)
    a_spec = pl.BlockSpec((tm, tk), lambda i, j, k: (i, k))
    o_spec = pl.BlockSpec((tm, tn), lambda i, j, k: (i, j))
    if residual is None:
        return _one(_mm(name, "nn", [a], [a_spec], [b], [b_spec], [(0, 0, 0)], 1, grid, tm, tn,
                        [_sds((M, n), out_dtype)], [o_spec], _store(out_dtype), jobs=jobs), jobs)

    def ep(accs, e_refs, o_refs):
        o_refs[0][...] = (e_refs[0][...] + accs[0]).astype(out_dtype)
    return _one(_mm(name, "nn", [a], [a_spec], [b], [b_spec], [(0, 0, 0)], 1, grid, tm, tn,
                    [_sds((M, n), out_dtype)], [o_spec], ep, extra=[residual], extra_specs=[o_spec], jobs=jobs), jobs)


def _mm_nt(name, a_list, b_list, out_dtype, *, k0_list=None, tm=512, tn=None, tk=None, residual=None, jobs=()):
    M, K = a_list[0].shape
    b0 = b_list[0]
    N = b0.shape[1] if b0.ndim == 3 else b0.shape[0]
    tm = _pick(M, (tm, 256, 128))
    tn = tn or _pick(N, (1024, 512, 256, 128))
    if b0.ndim == 3:
        Ks = b0.shape[2]
        tk = tk or _pick(Ks, (1408, 1024, 512, 256, 128))
        kps = Ks // tk
        b_specs = [pl.BlockSpec((None, tn, tk), lambda i, j, k: (k // kps, j, k % kps)) for _ in b_list]
    else:
        tk = tk or _pick(K, (1024, 896, 512, 256, 128))
        k0_list = k0_list or [0] * len(b_list)
        b_specs = []
        for k0 in k0_list:
            assert k0 % tk == 0
            b_specs.append(pl.BlockSpec((tn, tk), functools.partial(lambda i, j, k, c: (j, c + k), c=k0 // tk)))
    grid = (M // tm, N // tn, K // tk)
    a_specs = [pl.BlockSpec((tm, tk), lambda i, j, k: (i, k)) for _ in a_list]
    o_spec = pl.BlockSpec((tm, tn), lambda i, j, k: (i, j))
    pairs = [(p, p, 0) for p in range(len(a_list))]
    if residual is None:
        return _one(_mm(name, "nt", a_list, a_specs, b_list, b_specs, pairs, 1, grid, tm, tn,
                        [_sds((M, N), out_dtype)], [o_spec], _store(out_dtype), jobs=jobs), jobs)

    def ep(accs, e_refs, o_refs):
        o_refs[0][...] = (e_refs[0][...] + accs[0]).astype(out_dtype)
    return _one(_mm(name, "nt", a_list, a_specs, b_list, b_specs, pairs, 1, grid, tm, tn,
                    [_sds((M, N), out_dtype)], [o_spec], ep, extra=[residual], extra_specs=[o_spec], jobs=jobs), jobs)


def _mm_tn(name, a, b, out_dtype, *, slots=None, tm=None, tn=None, tk=1024, jobs=()):
    Kc, Mo = a.shape
    No = b.shape[1]
    tm = tm or _pick(Mo, (1024, 704, 512, 256, 128))
    tk = _pick(Kc, (tk, 256, 128))
    if slots:
        Ns = No // slots
        tn = tn or _pick(Ns, (1408, 1024, 512, 256, 128))
        nps = Ns // tn
        o_spec = pl.BlockSpec((None, tm, tn), lambda i, j, k: (j // nps, i, j % nps))
        o_shape = _sds((slots, Mo, Ns), out_dtype)
    else:
        tn = tn or _pick(No, (1024, 512, 256, 128))
        o_spec = pl.BlockSpec((tm, tn), lambda i, j, k: (i, j))
        o_shape = _sds((Mo, No), out_dtype)
    grid = (Mo // tm, No // tn, Kc // tk)
    a_spec = pl.BlockSpec((tk, tm), lambda i, j, k: (k, i))
    b_spec = pl.BlockSpec((tk, tn), lambda i, j, k: (k, j))
    return _one(_mm(name, "tn", [a], [a_spec], [b], [b_spec], [(0, 0, 0)], 1, grid, tm, tn,
                    [o_shape], [o_spec], _store(out_dtype), jobs=jobs), jobs)


def _cast_bf16(name, w, chip):
    R, C = w.shape
    tr = _pick(R, (256, 128, 64, 32, 16))

    def body(k_ref, w_ref, o_ref):
        o_ref[...] = w_ref[...].astype(BF16)

    gs = pltpu.PrefetchScalarGridSpec(
        num_scalar_prefetch=1, grid=(R // tr,),
        in_specs=[pl.BlockSpec((tr, C), lambda i, k: (i, 0))],
        out_specs=pl.BlockSpec((None, tr, C), lambda i, k: (k[0], i, 0)))
    return pl.pallas_call(body, name=name, grid_spec=gs, out_shape=_sds((N_CHIPS, R, C), BF16),
                          compiler_params=_cp(("parallel",)))(jnp.reshape(chip, (1,)).astype(jnp.int32), w)


def _rms_fwd(name, x, g):
    T, D = x.shape
    tr = _pick(T, (256, 128))

    def body(x_ref, g_ref, h_ref, r_ref):
        xv = x_ref[...]
        r = lax.rsqrt(jnp.mean(xv * xv, axis=1, keepdims=True) + RMS_EPS)
        h_ref[...] = (xv * r * g_ref[...]).astype(BF16)
        r_ref[...] = r

    row = pl.BlockSpec((tr, D), lambda i: (i, 0))
    return pl.pallas_call(
        body, name=name, grid=(T // tr,),
        in_specs=[row, pl.BlockSpec((1, D), lambda i: (0, 0))],
        out_specs=[row, pl.BlockSpec((tr, 1), lambda i: (i, 0))],
        out_shape=[_sds((T, D), BF16), _sds((T, 1), F32)], compiler_params=_cp(("parallel",)),
    )(x, g)


def _final_loss_bwd(name, x2, tgt, g):
    T, D = x2.shape
    tr = _pick(T, (256, 128))

    def body(x_ref, t_ref, g_ref, dx_ref, dxb_ref, loss_ref, gg_ref):
        @pl.when(pl.program_id(0) == 0)
        def _():
            loss_ref[...] = jnp.zeros_like(loss_ref)
            gg_ref[...] = jnp.zeros_like(gg_ref)

        xv = x_ref[...]
        gv = g_ref[...]
        r = lax.rsqrt(jnp.mean(xv * xv, axis=1, keepdims=True) + RMS_EPS)
        n = xv * r
        e = n * gv - t_ref[...]
        loss_ref[...] += 0.5 * jnp.sum(jnp.mean(e * e, axis=1, keepdims=True), axis=0, keepdims=True)
        dy = e * (1.0 / D)
        gg_ref[...] += jnp.sum(dy * n, axis=0, keepdims=True)
        gy = dy * gv
        dx = r * (gy - n * jnp.mean(gy * n, axis=1, keepdims=True))
        dx_ref[...] = dx
        dxb_ref[...] = dx.astype(BF16)

    row = pl.BlockSpec((tr, D), lambda i: (i, 0))
    vec = pl.BlockSpec((1, D), lambda i: (0, 0))
    return pl.pallas_call(
        body, name=name, grid=(T // tr,),
        in_specs=[row, row, vec],
        out_specs=[row, row, pl.BlockSpec((1, LANE), lambda i: (0, 0)), vec],
        out_shape=[_sds((T, D), F32), _sds((T, D), BF16), _sds((1, LANE), F32), _sds((1, D), F32)],
        compiler_params=_cp(("arbitrary",)),
    )(x2, tgt, g)


def _rms_bwd(name, dh_list, x, r, g, dres, want_bf16):
    T, D = x.shape
    tr = _pick(T, (128,))
    n_dh = len(dh_list)

    def body(*refs):
        dh_refs = refs[:n_dh]
        x_ref, r_ref, g_ref, dres_ref = refs[n_dh:n_dh + 4]
        outs = refs[n_dh + 4:]
        gg_ref = outs[-1]

        @pl.when(pl.program_id(0) == 0)
        def _():
            gg_ref[...] = jnp.zeros_like(gg_ref)

        dh = dh_refs[0][...]
        for ref in dh_refs[1:]:
            dh = dh + ref[...]
        rv = r_ref[...]
        n = x_ref[...] * rv
        gg_ref[...] += jnp.sum(dh * n, axis=0, keepdims=True)
        gy = dh * g_ref[...]
        dx = dres_ref[...] + rv * (gy - n * jnp.mean(gy * n, axis=1, keepdims=True))
        outs[0][...] = dx
        if want_bf16:
            outs[1][...] = dx.astype(BF16)

    row = pl.BlockSpec((tr, D), lambda i: (i, 0))
    vec = pl.BlockSpec((1, D), lambda i: (0, 0))
    out_specs = [row] + ([row] if want_bf16 else []) + [vec]
    out_shape = [_sds((T, D), F32)] + ([_sds((T, D), BF16)] if want_bf16 else []) + [_sds((1, D), F32)]
    return pl.pallas_call(
        body, name=name, grid=(T // tr,),
        in_specs=[row] * n_dh + [row, pl.BlockSpec((tr, 1), lambda i: (i, 0)), vec, row],
        out_specs=out_specs, out_shape=out_shape, compiler_params=_cp(("arbitrary",)),
    )(*dh_list, x, r, g, dres)


def _merge_fwd(name, gates, u_a, u_b, b_gate):
    T, D = u_a.shape
    tr = _pick(T, (256, 128))

    def body(ga_ref, gb_ref, ua_ref, ub_ref, ba_ref, bb_ref, o_ref):
        sa = _sigmoid(ga_ref[...].astype(F32) + ba_ref[...])
        sb = _sigmoid(gb_ref[...].astype(F32) + bb_ref[...])
        o_ref[...] = (sa * ua_ref[...].astype(F32) + sb * ub_ref[...].astype(F32)).astype(BF16)

    row = pl.BlockSpec((tr, D), lambda i: (i, 0))
    row1 = pl.BlockSpec((tr, D), lambda i: (i, 1))
    v0 = pl.BlockSpec((1, D), lambda i: (0, 0))
    v1 = pl.BlockSpec((1, D), lambda i: (0, 1))
    return pl.pallas_call(
        body, name=name, grid=(T // tr,),
        in_specs=[row, row1, row, row, v0, v1], out_specs=row,
        out_shape=_sds((T, D), BF16), compiler_params=_cp(("parallel",)),
    )(gates, gates, u_a, u_b, b_gate, b_gate)


def _merge_bwd(name, dm, gates, u_a, u_b, b_gate):
    T, D = u_a.shape
    tr = _pick(T, (128,))

    def body(dm_ref, ga_ref, gb_ref, ua_ref, ub_ref, ba_ref, bb_ref, dua_ref, dub_ref, dga_ref, dgb_ref,
             gba_ref, gbb_ref):
        @pl.when(pl.program_id(0) == 0)
        def _():
            gba_ref[...] = jnp.zeros_like(gba_ref)
            gbb_ref[...] = jnp.zeros_like(gbb_ref)

        d = dm_ref[...].astype(F32)
        sa = _sigmoid(ga_ref[...].astype(F32) + ba_ref[...])
        sb = _sigmoid(gb_ref[...].astype(F32) + bb_ref[...])
        dua_ref[...] = (d * sa).astype(BF16)
        dub_ref[...] = (d * sb).astype(BF16)
        dga = d * ua_ref[...].astype(F32) * sa * (1.0 - sa)
        dgb = d * ub_ref[...].astype(F32) * sb * (1.0 - sb)
        dga_ref[...] = dga.astype(BF16)
        dgb_ref[...] = dgb.astype(BF16)
        gba_ref[...] += jnp.sum(dga, axis=0, keepdims=True)
        gbb_ref[...] += jnp.sum(dgb, axis=0, keepdims=True)

    row = pl.BlockSpec((tr, D), lambda i: (i, 0))
    row1 = pl.BlockSpec((tr, D), lambda i: (i, 1))
    v0 = pl.BlockSpec((1, D), lambda i: (0, 0))
    v1 = pl.BlockSpec((1, D), lambda i: (0, 1))
    outs = pl.pallas_call(
        body, name=name, grid=(T // tr,),
        in_specs=[row, row, row1, row, row, v0, v1],
        out_specs=[row, row, row, row, v0, v0],
        out_shape=[_sds((T, D), BF16), _sds((T, D), BF16), _sds((T, D), BF16), _sds((T, D), BF16),
                   _sds((1, D), F32), _sds((1, D), F32)],
        compiler_params=_cp(("arbitrary",)),
    )(dm, gates, gates, u_a, u_b, b_gate, b_gate)
    return outs


def _adamw(name, w, g, m, v, jobs=()):
    R, C = w.shape
    Cg = g.shape[1]
    tr = _pick(R, (64, 32, 16, 8))
    c1 = 1.0 - ADAM_B1 ** ADAM_STEP
    c2 = 1.0 - ADAM_B2 ** ADAM_STEP

    def body(w_ref, g_ref, m_ref, v_ref, d_ref, mo_ref, vo_ref, go_ref):
        gv = g_ref[...] if Cg == C else g_ref[:, :C]
        mn = ADAM_B1 * m_ref[...] + (1.0 - ADAM_B1) * gv
        vn = ADAM_B2 * v_ref[...] + (1.0 - ADAM_B2) * (gv * gv)
        d_ref[...] = -ADAM_LR * ((mn / c1) / (jnp.sqrt(vn / c2) + ADAM_EPS) + ADAM_WD * w_ref[...])
        mo_ref[...] = mn
        vo_ref[...] = vn
        go_ref[...] = gv

    blk = pl.BlockSpec((tr, C), lambda i: (i, 0))
    gblk = pl.BlockSpec((tr, Cg), lambda i: (i, 0))
    return _pcall(
        body, name=name, grid=(R // tr,),
        in_specs=[blk, gblk, blk, blk], out_specs=[blk] * 4,
        out_shape=[_sds((R, C), F32)] * 4, sem=("parallel",), jobs=jobs,
    )(w, g, m, v)


def _add_bf16(name, a, a_row0, b):
    S, h, C = b.shape
    tr = _pick(h, (256, 128, 64, 32, 16))
    nb = h // tr

    def body(off_ref, a_ref, b_ref, o_ref):
        o_ref[...] = (a_ref[...].astype(F32) + b_ref[...].astype(F32)).astype(BF16)

    gs = pltpu.PrefetchScalarGridSpec(
        num_scalar_prefetch=1, grid=(S, nb),
        in_specs=[pl.BlockSpec((None, tr, C), lambda s, i, off: (s, off[0] * nb + i, 0)),
                  pl.BlockSpec((None, tr, C), lambda s, i, off: (s, i, 0))],
        out_specs=pl.BlockSpec((None, tr, C), lambda s, i, off: (s, i, 0)))
    return pl.pallas_call(body, name=name, grid_spec=gs, out_shape=_sds((S, h, C), BF16),
                          compiler_params=_cp(("parallel", "parallel")))(
        jnp.reshape(a_row0, (1,)).astype(jnp.int32), a, b)


def _sum4(name, got, mine, chip, core):
    S, h, C = got.shape
    tr = _pick(h, (256, 128, 64, 32, 16))
    nb = h // tr

    def body(chip_ref, core_ref, m_ref, g_ref, o_ref):
        acc = m_ref[...].astype(F32)
        for s in range(S):
            acc = acc + g_ref[s].astype(F32)
        o_ref[...] = acc

    gs = pltpu.PrefetchScalarGridSpec(
        num_scalar_prefetch=2, grid=(nb,),
        in_specs=[pl.BlockSpec((None, tr, C), lambda i, kc, cc: (kc[0], i, 0)),
                  pl.BlockSpec((S, tr, C), lambda i, kc, cc: (0, i, 0))],
        out_specs=pl.BlockSpec((tr, C), lambda i, kc, cc: (cc[0] * nb + i, 0)))
    return pl.pallas_call(body, name=name, grid_spec=gs, out_shape=_sds((2 * h, C), F32),
                          compiler_params=_cp(("parallel",)))(
        jnp.reshape(chip, (1,)).astype(jnp.int32), jnp.reshape(core, (1,)).astype(jnp.int32), mine, got)


def _place():
    x, y, c = lax.axis_index("x"), lax.axis_index("y"), lax.axis_index("c")
    chips = [(1 - x, y), (x, 1 - y), (1 - x, 1 - y)]
    return x, y, c, chips


def _allgather(name, shards):
    n = len(shards)
    NS = 7

    def body(*refs):
        out_refs = refs[n:2 * n]
        ss, rs = refs[2 * n:]
        x, y, c, _ = _place()
        k, kx, ky, kd = 2 * x + y, 2 * (1 - x) + y, 2 * x + (1 - y), 2 * (1 - x) + (1 - y)
        across_x, across_y, sibling = (1 - x, y, c), (x, 1 - y, c), (x, y, 1 - c)
        sends = []

        def go(cp):
            cp.start()
            sends.append(cp)

        for a, out in enumerate(out_refs):
            h = out.shape[1] // 2
            q = h // 2
            half = lambda slot, cc=c: out.at[slot, pl.ds(cc * h, h), :]
            part0 = lambda slot: out.at[slot, pl.ds(c * h, q), :]
            part1 = lambda slot: out.at[slot, pl.ds(c * h + q, q), :]
            b = NS * a
            go(_rdma(half(k), half(k), ss, rs, b + 0, across_x))
            go(_rdma(half(k), half(k), ss, rs, b + 1, across_y))
            _rdma(half(kx), half(kx), ss, rs, b + 0, across_x).wait_recv()
            go(_rdma(part0(kx), part0(kx), ss, rs, b + 2, across_y))
            go(_rdma(half(kx), half(kx), ss, rs, b + 4, sibling))
            _rdma(half(ky), half(ky), ss, rs, b + 1, across_y).wait_recv()
            go(_rdma(part1(ky), part1(ky), ss, rs, b + 3, across_x))
            go(_rdma(half(ky), half(ky), ss, rs, b + 5, sibling))
            _rdma(part0(kd), part0(kd), ss, rs, b + 2, across_y).wait_recv()
            _rdma(part1(kd), part1(kd), ss, rs, b + 3, across_x).wait_recv()
            go(_rdma(half(kd), half(kd), ss, rs, b + 6, sibling))
        for a, out in enumerate(out_refs):
            h = out.shape[1] // 2
            for j, slot in enumerate((kx, ky, kd)):
                rows = out.at[slot, pl.ds((1 - c) * h, h), :]
                _rdma(rows, rows, ss, rs, NS * a + 4 + j, sibling).wait_recv()
        for cp in sends:
            cp.wait_send()

    return pl.pallas_call(
        body, name=name,
        in_specs=[ANY] * n, out_specs=[ANY] * n,
        out_shape=[_sds(s.shape, s.dtype) for s in shards],
        input_output_aliases={a: a for a in range(n)},
        scratch_shapes=[pltpu.SemaphoreType.DMA((NS * n,)), pltpu.SemaphoreType.DMA((NS * n,))],
    )(*shards)


class _Job:
    def __init__(self, ins, out_shapes, aliases, n_sems, start, finish):
        self.ins, self.out_shapes, self.aliases, self.n_sems = list(ins), list(out_shapes), dict(aliases), n_sems
        self.start, self.finish = start, finish


def _rdma(src, dst, ss, rs, idx, to):
    return pltpu.make_async_remote_copy(src_ref=src, dst_ref=dst, send_sem=ss.at[idx], recv_sem=rs.at[idx],
                                        device_id=to, device_id_type=MESH)


def _job_gather_ici(bufs, part=(0, 1)):
    pi, pn = part

    def descs(outs, ss, rs, incoming):
        x, y, c, chips = _place()
        res = []
        for a, out in enumerate(outs):
            h = out.shape[1] // 2
            hp = h // pn
            for j, (cx, cy) in enumerate(chips):
                rows = out.at[(2 * cx + cy) if incoming else (2 * x + y), pl.ds(c * h + pi * hp, hp), :]
                res.append(_rdma(rows, rows, ss, rs, 3 * a + j, (cx, cy, c)))
        return res

    def start(ins, outs, ss, rs):
        for d in descs(outs, ss, rs, False):
            d.start()

    def finish(ins, outs, ss, rs):
        for d in descs(outs, ss, rs, True):
            d.wait_recv()
        for d in descs(outs, ss, rs, False):
            d.wait_send()

    return _Job(bufs, [_sds(b.shape, b.dtype) for b in bufs], {a: a for a in range(len(bufs))}, 3 * len(bufs),
                start, finish)


def _job_gather_d2d(bufs):
    def descs(outs, ss, rs, incoming):
        x, y, c, chips = _place()
        res = []
        for a, out in enumerate(outs):
            h = out.shape[1] // 2
            for j, (cx, cy) in enumerate(chips):
                rows = out.at[2 * cx + cy, pl.ds(((1 - c) if incoming else c) * h, h), :]
                res.append(_rdma(rows, rows, ss, rs, 3 * a + j, (x, y, 1 - c)))
        return res

    def start(ins, outs, ss, rs):
        for d in descs(outs, ss, rs, False):
            d.start()

    def finish(ins, outs, ss, rs):
        for d in descs(outs, ss, rs, True):
            d.wait_recv()
        for d in descs(outs, ss, rs, False):
            d.wait_send()

    return _Job(bufs, [_sds(b.shape, b.dtype) for b in bufs], {a: a for a in range(len(bufs))}, 3 * len(bufs),
                start, finish)


def _job_sibling(grads):
    def descs(ins, outs, ss, rs):
        x, y, c, _ = _place()
        res = []
        for a, (g, out) in enumerate(zip(ins, outs)):
            h = g.shape[1] // 2
            res.append(_rdma(g.at[:, pl.ds((1 - c) * h, h), :], out, ss, rs, a, (x, y, 1 - c)))
        return res

    def start(ins, outs, ss, rs):
        for d in descs(ins, outs, ss, rs):
            d.start()

    def finish(ins, outs, ss, rs):
        for d in descs(ins, outs, ss, rs):
            d.wait()

    return _Job(grads, [_sds((g.shape[0], g.shape[1] // 2, g.shape[2]), g.dtype) for g in grads], {}, len(grads),
                start, finish)


def _job_scatter(parts, part=(0, 1), into=None):
    p0, p1, pn = part if len(part) == 3 else (part[0], part[0] + 1, part[1])
    n = len(parts)

    def descs(ins, outs, ss, rs):
        x, y, c, chips = _place()
        res = []
        for a, (p, out) in enumerate(zip(ins[:n], outs)):
            hp = p.shape[1] // pn
            rows = pl.ds(p0 * hp, (p1 - p0) * hp)
            for j, (cx, cy) in enumerate(chips):
                res.append(_rdma(p.at[2 * cx + cy, rows, :], out.at[j, rows, :], ss, rs, 3 * a + j, (cx, cy, c)))
        return res

    def start(ins, outs, ss, rs):
        for d in descs(ins, outs, ss, rs):
            d.start()

    def finish(ins, outs, ss, rs):
        for d in descs(ins, outs, ss, rs):
            d.wait()

    shapes = [_sds((3,) + p.shape[1:], p.dtype) for p in parts]
    if into is None:
        return _Job(parts, shapes, {}, 3 * n, start, finish)
    return _Job(list(parts) + list(into), shapes, {n + a: a for a in range(n)}, 3 * n, start, finish)


def _job_swap(fulls):
    def descs(outs, ss, rs, incoming):
        x, y, c, _ = _place()
        res = []
        for a, out in enumerate(outs):
            h = out.shape[0] // 2
            rows = out.at[pl.ds(((1 - c) if incoming else c) * h, h), :]
            res.append(_rdma(rows, rows, ss, rs, a, (x, y, 1 - c)))
        return res

    def start(ins, outs, ss, rs):
        for d in descs(outs, ss, rs, False):
            d.start()

    def finish(ins, outs, ss, rs):
        for d in descs(outs, ss, rs, True):
            d.wait_recv()
        for d in descs(outs, ss, rs, False):
            d.wait_send()

    return _Job(fulls, [_sds(f.shape, f.dtype) for f in fulls], {a: a for a in range(len(fulls))}, len(fulls),
                start, finish)


def _pcall(body, *, name, grid, in_specs, out_specs, out_shape, scratch_shapes=(), sem, jobs=()):
    in_specs, out_specs, out_shape = list(in_specs), list(out_specs), list(out_shape)
    scratch = list(scratch_shapes)
    n_in, n_out, n_scr = len(in_specs), len(out_shape), len(scratch)
    if not jobs:
        call = pl.pallas_call(body, name=name, grid=grid, in_specs=in_specs, out_specs=out_specs, out_shape=out_shape,
                              scratch_shapes=scratch, compiler_params=_cp(sem))
        return lambda *args: (call(*args), [])
    jin = sum(len(j.ins) for j in jobs)
    jout = sum(len(j.out_shapes) for j in jobs)
    aliases, pi, po = {}, n_in, n_out
    for j in jobs:
        for ia, oa in j.aliases.items():
            aliases[pi + ia] = po + oa
        pi, po = pi + len(j.ins), po + len(j.out_shapes)

    def wrapped(*refs):
        ins = refs[:n_in]
        jins = refs[n_in:n_in + jin]
        outs = refs[n_in + jin:n_in + jin + n_out]
        jouts = refs[n_in + jin + n_out:n_in + jin + n_out + jout]
        scr = refs[n_in + jin + n_out + jout:n_in + jin + n_out + jout + n_scr]
        sems = refs[n_in + jin + n_out + jout + n_scr:]
        first, last = None, None
        for d, g in enumerate(grid):
            f, l = pl.program_id(d) == 0, pl.program_id(d) == g - 1
            first = f if first is None else jnp.logical_and(first, f)
            last = l if last is None else jnp.logical_and(last, l)

        def each(what):
            pi, po = 0, 0
            for q, j in enumerate(jobs):
                getattr(j, what)(jins[pi:pi + len(j.ins)], jouts[po:po + len(j.out_shapes)], sems[2 * q], sems[2 * q + 1])
                pi, po = pi + len(j.ins), po + len(j.out_shapes)

        @pl.when(first)
        def _():
            each("start")

        body(*ins, *outs, *scr)

        @pl.when(last)
        def _():
            each("finish")

    call = pl.pallas_call(
        wrapped, name=name, grid=grid,
        in_specs=in_specs + [ANY] * jin, out_specs=out_specs + [ANY] * jout,
        out_shape=out_shape + [s for j in jobs for s in j.out_shapes],
        input_output_aliases=aliases,
        scratch_shapes=scratch + [pltpu.SemaphoreType.DMA((j.n_sems,)) for j in jobs for _ in range(2)],
        compiler_params=_cp(("arbitrary",) * len(grid)))

    def run(*args):
        res = call(*args, *[a for j in jobs for a in j.ins])
        main, rest, per_job = list(res[:n_out]), list(res[n_out:]), []
        for j in jobs:
            per_job.append(rest[:len(j.out_shapes)])
            rest = rest[len(j.out_shapes):]
        return main, per_job
    return run


def _comm_only(name, jobs):
    jin = sum(len(j.ins) for j in jobs)
    jout = sum(len(j.out_shapes) for j in jobs)
    aliases, pi, po = {}, 0, 0
    for j in jobs:
        for ia, oa in j.aliases.items():
            aliases[pi + ia] = po + oa
        pi, po = pi + len(j.ins), po + len(j.out_shapes)

    def body(*refs):
        jins, jouts, sems = refs[:jin], refs[jin:jin + jout], refs[jin + jout:]
        for what in ("start", "finish"):
            pi, po = 0, 0
            for q, j in enumerate(jobs):
                getattr(j, what)(jins[pi:pi + len(j.ins)], jouts[po:po + len(j.out_shapes)], sems[2 * q], sems[2 * q + 1])
                pi, po = pi + len(j.ins), po + len(j.out_shapes)

    res = pl.pallas_call(
        body, name=name, in_specs=[ANY] * jin, out_specs=[ANY] * jout,
        out_shape=[s for j in jobs for s in j.out_shapes], input_output_aliases=aliases,
        scratch_shapes=[pltpu.SemaphoreType.DMA((j.n_sems,)) for j in jobs for _ in range(2)],
    )(*[a for j in jobs for a in j.ins])
    rest, per_job = list(res), []
    for j in jobs:
        per_job.append(rest[:len(j.out_shapes)])
        rest = rest[len(j.out_shapes):]
    return per_job


def _small_allreduce(name, v):
    m_per, n = v.shape

    def body(x_ref, sum_ref, all_ref, send_sems, recv_sems, local_sem):
        x, y, c, chips = _place()
        me, sibling = (x, y, c), (x, y, 1 - c)

        def rows(px, py, pc):
            return all_ref.at[pl.ds((4 * px + 2 * py + pc) * m_per, m_per), :]

        def copy(kk, block, to, src=None):
            return pltpu.make_async_remote_copy(
                src_ref=rows(*block) if src is None else src, dst_ref=rows(*block),
                send_sem=send_sems.at[kk], recv_sem=recv_sems.at[kk], device_id=to, device_id_type=MESH)

        mine = pltpu.make_async_copy(x_ref, rows(*me), local_sem)
        mine.start()
        first = [copy(0, me, sibling, src=x_ref)]
        first += [copy(1 + j, me, (*chip, c), src=x_ref) for j, chip in enumerate(chips)]
        for cp in first:
            cp.start()
        passed = [copy(4 + j, (*chip, c), sibling) for j, chip in enumerate(chips)]
        for j, chip in enumerate(chips):
            copy(1 + j, (*chip, c), me).wait_recv()
            passed[j].start()
        copy(0, sibling, me).wait_recv()
        for j, chip in enumerate(chips):
            copy(4 + j, (*chip, 1 - c), me).wait_recv()
        for cp in first + passed:
            cp.wait_send()
        mine.wait()
        acc = all_ref[pl.ds(0, m_per), :]
        for d in range(1, 8):
            acc = acc + all_ref[pl.ds(d * m_per, m_per), :]
        sum_ref[...] = acc

    vm = pl.BlockSpec(memory_space=pltpu.VMEM)
    return pl.pallas_call(
        body, name=name, in_specs=[vm], out_specs=[vm, vm],
        out_shape=[_sds((m_per, n), F32), _sds((8 * m_per, n), F32)],
        scratch_shapes=[pltpu.SemaphoreType.DMA((7,)), pltpu.SemaphoreType.DMA((7,)), pltpu.SemaphoreType.DMA],
    )(v)[0]


def _in_layout(D, W6, nhb, Ls):
    nmain = W6 + 2 * D
    lay = []
    for k in range(N_CHIPS):
        g0, g1 = k * Ls, (k + 1) * Ls
        pieces = []
        a, b = max(g0, 0), min(g1, W6)
        if a < b:
            pieces.append((a - g0, b - g0, a))
        a, b = max(g0, W6 + nhb), min(g1, W6 + nhb + 2 * D)
        if a < b:
            pieces.append((a - g0, b - g0, a - nhb))
        a, b = max(g0, W6), min(g1, W6 + nhb)
        fpiece = (a - g0, b - g0, a - W6) if a < b else None
        assert fpiece is None or (b - a) == nhb
        main0 = min(p[2] for p in pieces)
        main1 = max(p[2] + p[1] - p[0] for p in pieces)
        lay.append(dict(pieces=pieces, f=fpiece, s=main0 // LANE, e=-(-main1 // LANE), main1=main1))
    assert sum(1 for l in lay if l["f"] is not None) == 1
    nbw = max(l["e"] - l["s"] + (1 if l["f"] else 0) for l in lay)
    for k in range(1, N_CHIPS):
        assert lay[k]["s"] >= lay[k - 1]["e"] - 1 and lay[k]["s"] > lay[k - 1]["s"]
    return lay, nbw, nmain


def _to_window(w, lay_k, nbw):
    D = w.shape[0]
    items = [(c0 - lay_k["s"] * LANE, l0, l1) for (l0, l1, c0) in lay_k["pieces"]]
    if lay_k["f"]:
        l0, l1, off = lay_k["f"]
        items.append(((lay_k["e"] - lay_k["s"]) * LANE + off, l0, l1))
    items.sort()
    cols, pos = [], 0
    for w0, l0, l1 in items:
        if w0 > pos:
            cols.append(jnp.zeros((D, w0 - pos), w.dtype))
        cols.append(w[:, l0:l1])
        pos = w0 + (l1 - l0)
    if pos < nbw * LANE:
        cols.append(jnp.zeros((D, nbw * LANE - pos), w.dtype))
    return jnp.concatenate(cols, axis=1)


def _from_window(win, lay_k):
    items = [(l0, c0 - lay_k["s"] * LANE, l1 - l0) for (l0, l1, c0) in lay_k["pieces"]]
    if lay_k["f"]:
        l0, l1, off = lay_k["f"]
        items.append((l0, (lay_k["e"] - lay_k["s"]) * LANE + off, l1 - l0))
    items.sort()
    return jnp.concatenate([win[:, w0:w0 + n] for (_, w0, n) in items], axis=1)


def _assemble_in(name, wins, lay, nbw, nmain):
    _, D, _ = wins.shape
    ncb = nmain // LANE + 1
    k1 = np.zeros(ncb, np.int32)
    i1 = np.zeros(ncb, np.int32)
    k2 = np.zeros(ncb, np.int32)
    i2 = np.zeros(ncb, np.int32)
    fl = np.zeros(ncb, np.int32)
    for b in range(ncb - 1):
        k = max(kk for kk in range(N_CHIPS) if lay[kk]["s"] <= b)
        k1[b], i1[b] = k, b - lay[k]["s"]
        if k >= 1 and b == lay[k]["s"] and lay[k - 1]["main1"] > b * LANE:
            k2[b], i2[b], fl[b] = k - 1, b - lay[k - 1]["s"], 1
    kf = [kk for kk in range(N_CHIPS) if lay[kk]["f"]][0]
    k1[ncb - 1], i1[ncb - 1] = kf, lay[kf]["e"] - lay[kf]["s"]

    def body(k1_ref, i1_ref, k2_ref, i2_ref, fl_ref, a_ref, b_ref, o_ref):
        b = pl.program_id(0)
        add = jnp.where(fl_ref[b] == 1, b_ref[...], jnp.zeros_like(b_ref))
        o_ref[...] = a_ref[...] + add

    gs = pltpu.PrefetchScalarGridSpec(
        num_scalar_prefetch=5, grid=(ncb,),
        in_specs=[pl.BlockSpec((None, D, LANE), lambda b, k1r, i1r, k2r, i2r, flr: (k1r[b], 0, i1r[b])),
                  pl.BlockSpec((None, D, LANE), lambda b, k1r, i1r, k2r, i2r, flr: (k2r[b], 0, i2r[b]))],
        out_specs=pl.BlockSpec((D, LANE), lambda b, k1r, i1r, k2r, i2r, flr: (0, b)))
    return pl.pallas_call(body, name=name, grid_spec=gs, out_shape=_sds((D, ncb * LANE), BF16),
                          compiler_params=_cp(("parallel",)))(
        jnp.asarray(k1), jnp.asarray(i1), jnp.asarray(k2), jnp.asarray(i2), jnp.asarray(fl), wins, wins)


def _hgroup(nh):
    return _pick(nh, (4, 2, 1))


def _a_specs_q(nh, G):
    ngrp = nh // G
    blk = (GROUP, G * HEAD_DIM)
    q = pl.BlockSpec(blk, lambda i, hg: (i, hg))
    ks = [pl.BlockSpec(blk, functools.partial(
        lambda i, hg, j: (jnp.maximum(i - (WIN_BLOCKS - 1) + j, 0), ngrp + hg), j=j)) for j in range(WIN_BLOCKS)]
    vs = [pl.BlockSpec(blk, functools.partial(
        lambda i, hg, j: (jnp.maximum(i - (WIN_BLOCKS - 1) + j, 0), 2 * ngrp + hg), j=j)) for j in range(WIN_BLOCKS)]
    return q, ks, vs


def _a_logits(q, ks, bias, i, scale):
    parts = [lax.dot_general(q, k, NT, preferred_element_type=F32) for k in ks]
    s = jnp.concatenate(parts, axis=1) * scale + bias
    col = lax.broadcasted_iota(jnp.int32, s.shape, 1)
    return jnp.where(col >= (WIN_BLOCKS - 1 - i) * GROUP, s, NEG_INF)


def _attn_a_fwd(name, qkv, bias2, nh, jobs=()):
    T = qkv.shape[0]
    ng = T // GROUP
    G = _hgroup(nh)
    scale = HEAD_DIM ** -0.5

    def body(q_ref, *refs):
        k_refs = refs[:WIN_BLOCKS]
        v_refs = refs[WIN_BLOCKS:2 * WIN_BLOCKS]
        bias_ref, o_ref, lse_ref = refs[2 * WIN_BLOCKS:]
        i, hg = pl.program_id(0), pl.program_id(1)

        @pl.when(hg == 0)
        def _():
            lse_ref[...] = jnp.zeros_like(lse_ref)

        for g in range(G):
            h = hg * G + g
            sl = slice(g * HEAD_DIM, (g + 1) * HEAD_DIM)
            s = _a_logits(q_ref[:, sl], [kr[:, sl] for kr in k_refs], bias_ref[h], i, scale)
            m = jnp.max(s, axis=1, keepdims=True)
            p = jnp.exp(s - m)
            l = jnp.sum(p, axis=1, keepdims=True)
            pb = (p / l).astype(BF16)
            o = jnp.zeros((GROUP, HEAD_DIM), F32)
            for j in range(WIN_BLOCKS):
                o = o + jnp.dot(pb[:, j * GROUP:(j + 1) * GROUP], v_refs[j][:, sl], preferred_element_type=F32)
            o_ref[:, sl] = o.astype(BF16)
            _put_col(lse_ref, h, m + jnp.log(l))

    q_spec, k_specs, v_specs = _a_specs_q(nh, G)
    stat = pl.BlockSpec((GROUP, LANE), lambda i, hg: (i, 0))
    return _pcall(
        body, name=name, grid=(ng, nh // G),
        in_specs=[q_spec] + k_specs + v_specs + [pl.BlockSpec((nh, GROUP, WIN), lambda i, hg: (0, 0, 0))],
        out_specs=[pl.BlockSpec((GROUP, G * HEAD_DIM), lambda i, hg: (i, hg)), stat],
        out_shape=[_sds((T, nh * HEAD_DIM), BF16), _sds((T, LANE), F32)],
        sem=("parallel", "arbitrary"), jobs=jobs,
    )(qkv, *([qkv] * (2 * WIN_BLOCKS)), bias2)


def _attn_a_dq(name, qkv, do, lse, bias2, nh, jobs=()):
    T = qkv.shape[0]
    ng = T // GROUP
    G = _hgroup(nh)
    scale = HEAD_DIM ** -0.5

    def body(q_ref, *refs):
        k_refs = refs[:WIN_BLOCKS]
        v_refs = refs[WIN_BLOCKS:2 * WIN_BLOCKS]
        do_ref, lse_ref, bias_ref, dq_ref, delta_ref, db_ref = refs[2 * WIN_BLOCKS:]
        i, hg = pl.program_id(0), pl.program_id(1)

        @pl.when(hg == 0)
        def _():
            delta_ref[...] = jnp.zeros_like(delta_ref)

        @pl.when(i == 0)
        def _():
            for g in range(G):
                db_ref[hg * G + g] = jnp.zeros((GROUP, WIN), F32)

        for g in range(G):
            h = hg * G + g
            sl = slice(g * HEAD_DIM, (g + 1) * HEAD_DIM)
            ks = [kr[:, sl] for kr in k_refs]
            s = _a_logits(q_ref[:, sl], ks, bias_ref[h], i, scale)
            p = jnp.exp(s - _col_of(lse_ref[...], h))
            dov = do_ref[:, sl]
            dp = jnp.concatenate([lax.dot_general(dov, vr[:, sl], NT, preferred_element_type=F32) for vr in v_refs],
                                 axis=1)
            delta = jnp.sum(p * dp, axis=1, keepdims=True)
            ds = p * (dp - delta)
            db_ref[h] += ds
            dsb = ds.astype(BF16)
            dq = jnp.zeros((GROUP, HEAD_DIM), F32)
            for j in range(WIN_BLOCKS):
                dq = dq + jnp.dot(dsb[:, j * GROUP:(j + 1) * GROUP], ks[j], preferred_element_type=F32)
            dq_ref[:, sl] = (dq * scale).astype(BF16)
            _put_col(delta_ref, h, delta)

    q_spec, k_specs, v_specs = _a_specs_q(nh, G)
    blk = pl.BlockSpec((GROUP, G * HEAD_DIM), lambda i, hg: (i, hg))
    stat = pl.BlockSpec((GROUP, LANE), lambda i, hg: (i, 0))
    full_b = pl.BlockSpec((nh, GROUP, WIN), lambda i, hg: (0, 0, 0))
    return _pcall(
        body, name=name, grid=(ng, nh // G),
        in_specs=[q_spec] + k_specs + v_specs + [blk, stat, full_b],
        out_specs=[blk, stat, full_b],
        out_shape=[_sds((T, nh * HEAD_DIM), BF16), _sds((T, LANE), F32), _sds((nh, GROUP, WIN), F32)],
        sem=("arbitrary", "arbitrary"), jobs=jobs,
    )(qkv, *([qkv] * (2 * WIN_BLOCKS)), do, lse, bias2)


def _attn_a_dkv(name, qkv, do, lse, delta, bias2, nh, jobs=()):
    T = qkv.shape[0]
    ng = T // GROUP
    G = _hgroup(nh)
    ngrp = nh // G
    scale = HEAD_DIM ** -0.5
    nj = WIN_BLOCKS

    def body(k_ref, v_ref, *refs):
        q_refs = refs[:nj]
        do_refs = refs[nj:2 * nj]
        lse_refs = refs[2 * nj:3 * nj]
        dl_refs = refs[3 * nj:4 * nj]
        bias_ref, dk_ref, dv_ref = refs[4 * nj:]
        r, hg = pl.program_id(0), pl.program_id(1)
        for g in range(G):
            h = hg * G + g
            sl = slice(g * HEAD_DIM, (g + 1) * HEAD_DIM)
            kv, vv = k_ref[:, sl], v_ref[:, sl]
            bias = bias_ref[h]
            dk = jnp.zeros((GROUP, HEAD_DIM), F32)
            dv = jnp.zeros((GROUP, HEAD_DIM), F32)
            for j in range(nj):
                qv, dov = q_refs[j][:, sl], do_refs[j][:, sl]
                c0 = (nj - 1 - j) * GROUP
                s = lax.dot_general(qv, kv, NT, preferred_element_type=F32) * scale + bias[:, c0:c0 + GROUP]
                p = jnp.exp(s - _col_of(lse_refs[j][...], h))
                p = jnp.where(r + j <= ng - 1, p, 0.0)
                dp = lax.dot_general(dov, vv, NT, preferred_element_type=F32)
                ds = p * (dp - _col_of(dl_refs[j][...], h))
                dv = dv + lax.dot_general(p.astype(BF16), dov, TN, preferred_element_type=F32)
                dk = dk + lax.dot_general(ds.astype(BF16), qv, TN, preferred_element_type=F32)
            dk_ref[:, sl] = (dk * scale).astype(BF16)
            dv_ref[:, sl] = dv.astype(BF16)

    def qmap(j):
        return functools.partial(lambda r, hg, j: (jnp.minimum(r + j, ng - 1), hg), j=j)

    def smap(j):
        return functools.partial(lambda r, hg, j: (jnp.minimum(r + j, ng - 1), 0), j=j)

    blk = (GROUP, G * HEAD_DIM)
    in_specs = ([pl.BlockSpec(blk, lambda r, hg: (r, ngrp + hg)), pl.BlockSpec(blk, lambda r, hg: (r, 2 * ngrp + hg))]
                + [pl.BlockSpec(blk, qmap(j)) for j in range(nj)]
                + [pl.BlockSpec(blk, qmap(j)) for j in range(nj)]
                + [pl.BlockSpec((GROUP, LANE), smap(j)) for j in range(nj)]
                + [pl.BlockSpec((GROUP, LANE), smap(j)) for j in range(nj)]
                + [pl.BlockSpec((nh, GROUP, WIN), lambda r, hg: (0, 0, 0))])
    out = pl.BlockSpec(blk, lambda r, hg: (r, hg))
    return _pcall(
        body, name=name, grid=(ng, ngrp), in_specs=in_specs, out_specs=[out, out],
        out_shape=[_sds((T, nh * HEAD_DIM), BF16)] * 2,
        sem=("parallel", "parallel"), jobs=jobs,
    )(qkv, qkv, *([qkv] * nj), *([do] * nj), *([lse] * nj), *([delta] * nj), bias2)


def _fox_prep(name, f, b_f):
    T = f.shape[0]
    tb = _pick(T, (256, 128))

    def body(f_ref, b_ref, cum_ref, cumt_ref, carry_ref):
        @pl.when(pl.program_id(0) == 0)
        def _():
            carry_ref[...] = jnp.zeros_like(carry_ref)

        z = f_ref[...] + b_ref[...]
        logf = jnp.minimum(z, 0.0) - jnp.log(1.0 + jnp.exp(-jnp.abs(z)))
        row = lax.broadcasted_iota(jnp.int32, (tb, tb), 0)
        col = lax.broadcasted_iota(jnp.int32, (tb, tb), 1)
        tri = (row >= col).astype(BF16)
        acc = jnp.zeros((tb, LANE), F32)
        for piece in _split3(logf):
            acc = acc + jnp.dot(tri, piece, preferred_element_type=F32)
        cum = acc + carry_ref[...]
        cum_ref[...] = cum
        cumt_ref[...] = cum.T
        carry_ref[...] = cum_ref[pl.ds(tb - 1, 1), :]

    return pl.pallas_call(
        body, name=name, grid=(T // tb,),
        in_specs=[pl.BlockSpec((tb, LANE), lambda i: (i, 0)), pl.BlockSpec((1, LANE), lambda i: (0, 0))],
        out_specs=[pl.BlockSpec((tb, LANE), lambda i: (i, 0)), pl.BlockSpec((LANE, tb), lambda i: (0, i))],
        out_shape=[_sds((T, LANE), F32), _sds((LANE, T), F32)],
        scratch_shapes=[pltpu.VMEM((1, LANE), F32)],
        compiler_params=_cp(("arbitrary",)),
    )(f, b_f)


def _fox_blk(T):
    return _pick(T, (256, 128))


def _fox_group(nh):
    return _pick(nh, (4, 2, 1))


def _fox_allowed(i, j, tq, tk):
    diff = lax.broadcasted_iota(jnp.int32, (tq, tk), 1) - lax.broadcasted_iota(jnp.int32, (tq, tk), 0)
    return diff <= (i - j) * tq


def _fox_fwd(name, qkv, cum, cumt, nh, jobs=()):
    T = qkv.shape[0]
    tq = tk = _fox_blk(T)
    G = _fox_group(nh)
    ngrp = nh // G
    scale = HEAD_DIM ** -0.5

    def body(q_ref, k_ref, v_ref, cum_ref, cumt_ref, o_ref, lse_ref):
        i, hg = pl.program_id(0), pl.program_id(1)

        @pl.when(hg == 0)
        def _():
            lse_ref[...] = jnp.zeros_like(lse_ref)

        sls = [slice(g * HEAD_DIM, (g + 1) * HEAD_DIM) for g in range(G)]
        qs = [q_ref[:, sl] for sl in sls]
        cqs = [_col_of(cum_ref[...], hg * G + g) for g in range(G)]

        def step(j, carry, diagonal=False):
            k0 = pl.multiple_of(j * tk, tk)
            out = []
            for g in range(G):
                m, l, acc = carry[g]
                kj = k_ref[pl.ds(k0, tk), sls[g]]
                vj = v_ref[pl.ds(k0, tk), sls[g]]
                ck = cumt_ref[pl.ds(hg * G + g, 1), pl.ds(k0, tk)]
                s = lax.dot_general(qs[g], kj, NT, preferred_element_type=F32) * scale + (cqs[g] - ck)
                if diagonal:
                    s = jnp.where(_fox_allowed(i, j, tq, tk), s, NEG_INF)
                m_new = jnp.maximum(m, jnp.max(s, axis=1, keepdims=True))
                alpha = jnp.exp(m - m_new)
                p = jnp.exp(s - m_new)
                l = alpha * l + jnp.sum(p, axis=1, keepdims=True)
                acc = alpha * acc + jnp.dot(p.astype(BF16), vj, preferred_element_type=F32)
                out.append((m_new, l, acc))
            return tuple(out)

        one = (jnp.full((tq, 1), NEG_INF, F32), jnp.zeros((tq, 1), F32), jnp.zeros((tq, HEAD_DIM), F32))
        res = step(i, lax.fori_loop(0, i, step, tuple(one for _ in range(G))), diagonal=True)
        for g in range(G):
            m, l, acc = res[g]
            o_ref[:, sls[g]] = (acc / l).astype(BF16)
            _put_col(lse_ref, hg * G + g, m + jnp.log(l))

    GW = G * HEAD_DIM
    return _pcall(
        body, name=name, grid=(T // tq, ngrp),
        in_specs=[pl.BlockSpec((tq, GW), lambda i, hg: (i, 3 * ngrp + hg)),
                  pl.BlockSpec((T, GW), lambda i, hg: (0, 4 * ngrp + hg)),
                  pl.BlockSpec((T, GW), lambda i, hg: (0, 5 * ngrp + hg)),
                  pl.BlockSpec((tq, LANE), lambda i, hg: (i, 0)),
                  pl.BlockSpec((LANE, T), lambda i, hg: (0, 0))],
        out_specs=[pl.BlockSpec((tq, GW), lambda i, hg: (i, hg)), pl.BlockSpec((tq, LANE), lambda i, hg: (i, 0))],
        out_shape=[_sds((T, nh * HEAD_DIM), BF16), _sds((T, LANE), F32)],
        sem=("parallel", "arbitrary"), jobs=jobs,
    )(qkv, qkv, qkv, cum, cumt)


def _fox_dq(name, qkv, do, lse, cum, cumt, nh, jobs=()):
    T = qkv.shape[0]
    tq = tk = _fox_blk(T)
    G = _fox_group(nh)
    ngrp = nh // G
    GW = G * HEAD_DIM
    scale = HEAD_DIM ** -0.5

    def body(q_ref, k_ref, v_ref, do_ref, lse_ref, cum_ref, cumt_ref, dq_ref, delta_ref):
        i, hg = pl.program_id(0), pl.program_id(1)

        @pl.when(hg == 0)
        def _():
            delta_ref[...] = jnp.zeros_like(delta_ref)

        sls = [slice(g * HEAD_DIM, (g + 1) * HEAD_DIM) for g in range(G)]
        qs = [q_ref[:, sl] for sl in sls]
        dos = [do_ref[:, sl] for sl in sls]
        cqs = [_col_of(cum_ref[...], hg * G + g) for g in range(G)]
        lses = [_col_of(lse_ref[...], hg * G + g) for g in range(G)]

        def p_dp(j, g, ok):
            k0 = pl.multiple_of(j * tk, tk)
            kj = k_ref[pl.ds(k0, tk), sls[g]]
            vj = v_ref[pl.ds(k0, tk), sls[g]]
            ck = cumt_ref[pl.ds(hg * G + g, 1), pl.ds(k0, tk)]
            s = lax.dot_general(qs[g], kj, NT, preferred_element_type=F32) * scale + (cqs[g] - ck)
            if ok is not None:
                s = jnp.where(ok, s, NEG_INF)
            p = jnp.exp(s - lses[g])
            return p, lax.dot_general(dos[g], vj, NT, preferred_element_type=F32), kj

        def sweep_delta(j, deltas, diagonal=False):
            ok = _fox_allowed(i, j, tq, tk) if diagonal else None
            out = []
            for g in range(G):
                p, dp, _ = p_dp(j, g, ok)
                out.append(deltas[g] + jnp.sum(p * dp, axis=1, keepdims=True))
            return tuple(out)

        deltas = lax.fori_loop(0, i, sweep_delta, tuple(jnp.zeros((tq, 1), F32) for _ in range(G)))
        deltas = sweep_delta(i, deltas, diagonal=True)

        def sweep_dq(j, dqs, diagonal=False):
            ok = _fox_allowed(i, j, tq, tk) if diagonal else None
            out = []
            for g in range(G):
                p, dp, kj = p_dp(j, g, ok)
                ds = p * (dp - deltas[g])
                out.append(dqs[g] + jnp.dot(ds.astype(BF16), kj, preferred_element_type=F32))
            return tuple(out)

        dqs = lax.fori_loop(0, i, sweep_dq, tuple(jnp.zeros((tq, HEAD_DIM), F32) for _ in range(G)))
        dqs = sweep_dq(i, dqs, diagonal=True)
        for g in range(G):
            dq_ref[:, sls[g]] = (dqs[g] * scale).astype(BF16)
            _put_col(delta_ref, hg * G + g, deltas[g])

    blk = pl.BlockSpec((tq, GW), lambda i, hg: (i, hg))
    stat = pl.BlockSpec((tq, LANE), lambda i, hg: (i, 0))
    return _pcall(
        body, name=name, grid=(T // tq, ngrp),
        in_specs=[pl.BlockSpec((tq, GW), lambda i, hg: (i, 3 * ngrp + hg)),
                  pl.BlockSpec((T, GW), lambda i, hg: (0, 4 * ngrp + hg)),
                  pl.BlockSpec((T, GW), lambda i, hg: (0, 5 * ngrp + hg)),
                  blk, stat, stat, pl.BlockSpec((LANE, T), lambda i, hg: (0, 0))],
        out_specs=[blk, stat],
        out_shape=[_sds((T, nh * HEAD_DIM), BF16), _sds((T, LANE), F32)],
        sem=("parallel", "arbitrary"), jobs=jobs,
    )(qkv, qkv, qkv, do, lse, cum, cumt)


def _fox_dkv(name, qkv, do, lse, delta, cum, cumt, nh, jobs=()):
    T = qkv.shape[0]
    tq = tk = _fox_blk(T)
    nq = T // tq
    G = _fox_group(nh)
    ngrp = nh // G
    GW = G * HEAD_DIM
    scale = HEAD_DIM ** -0.5

    def body(k_ref, v_ref, q_ref, do_ref, lse_ref, dl_ref, cum_ref, cumt_ref, dk_ref, dv_ref, dc_ref):
        j, hg = pl.program_id(0), pl.program_id(1)

        @pl.when(hg == 0)
        def _():
            dc_ref[...] = jnp.zeros_like(dc_ref)

        sls = [slice(g * HEAD_DIM, (g + 1) * HEAD_DIM) for g in range(G)]
        kjs = [k_ref[:, sl] for sl in sls]
        vjs = [v_ref[:, sl] for sl in sls]
        k0 = pl.multiple_of(j * tk, tk)
        cks = [cumt_ref[pl.ds(hg * G + g, 1), pl.ds(k0, tk)] for g in range(G)]

        def step(i, carry, diagonal=False):
            q0 = pl.multiple_of(i * tq, tq)
            cum_i, lse_i, dl_i = cum_ref[pl.ds(q0, tq), :], lse_ref[pl.ds(q0, tq), :], dl_ref[pl.ds(q0, tq), :]
            out = []
            for g in range(G):
                dk, dv, dc = carry[g]
                h = hg * G + g
                qi = q_ref[pl.ds(q0, tq), sls[g]]
                doi = do_ref[pl.ds(q0, tq), sls[g]]
                s = lax.dot_general(qi, kjs[g], NT, preferred_element_type=F32) * scale + (_col_of(cum_i, h) - cks[g])
                if diagonal:
                    s = jnp.where(_fox_allowed(i, j, tq, tk), s, NEG_INF)
                p = jnp.exp(s - _col_of(lse_i, h))
                dp = lax.dot_general(doi, vjs[g], NT, preferred_element_type=F32)
                ds = p * (dp - _col_of(dl_i, h))
                dv = dv + lax.dot_general(p.astype(BF16), doi, TN, preferred_element_type=F32)
                dk = dk + lax.dot_general(ds.astype(BF16), qi, TN, preferred_element_type=F32)
                dc = dc - jnp.sum(ds, axis=0, keepdims=True)
                out.append((dk, dv, dc))
            return tuple(out)

        one = (jnp.zeros((tk, HEAD_DIM), F32), jnp.zeros((tk, HEAD_DIM), F32), jnp.zeros((1, tk), F32))
        res = lax.fori_loop(j + 1, nq, step, step(j, tuple(one for _ in range(G)), diagonal=True))
        sub = lax.broadcasted_iota(jnp.int32, (LANE, tk), 0)
        dc_all = dc_ref[...]
        for g in range(G):
            dk, dv, dc = res[g]
            dk_ref[:, sls[g]] = (dk * scale).astype(BF16)
            dv_ref[:, sls[g]] = dv.astype(BF16)
            dc_all = jnp.where(sub == hg * G + g, dc, dc_all)
        dc_ref[...] = dc_all

    whole = lambda c: pl.BlockSpec((T, GW), c)
    stat = pl.BlockSpec((T, LANE), lambda j, hg: (0, 0))
    out = pl.BlockSpec((tk, GW), lambda j, hg: (j, hg))
    return _pcall(
        body, name=name, grid=(T // tk, ngrp),
        in_specs=[pl.BlockSpec((tk, GW), lambda j, hg: (j, 4 * ngrp + hg)),
                  pl.BlockSpec((tk, GW), lambda j, hg: (j, 5 * ngrp + hg)),
                  whole(lambda j, hg: (0, 3 * ngrp + hg)), whole(lambda j, hg: (0, hg)),
                  stat, stat, stat, pl.BlockSpec((LANE, T), lambda j, hg: (0, 0))],
        out_specs=[out, out, pl.BlockSpec((LANE, tk), lambda j, hg: (0, j))],
        out_shape=[_sds((T, nh * HEAD_DIM), BF16)] * 2 + [_sds((LANE, T), F32)],
        sem=("parallel", "arbitrary"), jobs=jobs,
    )(qkv, qkv, qkv, do, lse, delta, cum, cumt)


def _fox_post(name, dcumt, f, b_f):
    T = f.shape[0]
    tb = _pick(T, (256, 128))
    nb = T // tb

    def body(dc_ref, f_ref, b_ref, df_ref, gb_ref, carry_ref):
        @pl.when(pl.program_id(0) == 0)
        def _():
            carry_ref[...] = jnp.zeros_like(carry_ref)
            gb_ref[...] = jnp.zeros_like(gb_ref)

        dc = dc_ref[...]
        row = lax.broadcasted_iota(jnp.int32, (tb, tb), 0)
        col = lax.broadcasted_iota(jnp.int32, (tb, tb), 1)
        tri = (row >= col).astype(BF16)
        acc = jnp.zeros((LANE, tb), F32)
        for piece in _split3(dc):
            acc = acc + jnp.dot(piece, tri, preferred_element_type=F32)
        dlogf = (acc + carry_ref[...]).T
        carry_ref[...] += jnp.sum(dc, axis=1, keepdims=True)
        z = f_ref[...] + b_ref[...]
        df = dlogf * _sigmoid(-z)
        df_ref[...] = df.astype(BF16)
        gb_ref[...] += jnp.sum(df, axis=0, keepdims=True)

    return pl.pallas_call(
        body, name=name, grid=(nb,),
        in_specs=[pl.BlockSpec((LANE, tb), lambda g: (0, nb - 1 - g)),
                  pl.BlockSpec((tb, LANE), lambda g: (nb - 1 - g, 0)),
                  pl.BlockSpec((1, LANE), lambda g: (0, 0))],
        out_specs=[pl.BlockSpec((tb, LANE), lambda g: (nb - 1 - g, 0)), pl.BlockSpec((1, LANE), lambda g: (0, 0))],
        out_shape=[_sds((T, LANE), BF16), _sds((1, LANE), F32)],
        scratch_shapes=[pltpu.VMEM((LANE, 1), F32)],
        compiler_params=_cp(("arbitrary",)),
    )(dcumt, f, b_f)


def _rel_tables(n_rel):
    max_rel = (n_rel - 1) // 2
    nj = GROUP + WIN - 1
    onehot = np.zeros((n_rel, nj), np.float32)
    for j in range(nj):
        dist = (WIN - 1) - j
        onehot[int(np.clip(dist, -max_rel, max_rel)) + max_rel, j] = 1.0
    a = np.arange(GROUP)[:, None]
    kb = np.arange(WIN)[None, :]
    lo = CHUNK * (a // CHUNK)
    inband = (kb >= lo) & (kb < lo + BAND)
    return onehot, inband


def _bias2_of(rel_bias, onehot, inband):
    bv = jnp.dot(rel_bias, jnp.asarray(onehot), precision=lax.Precision.HIGHEST)
    rows = [bv[:, GROUP - 1 - a:GROUP - 1 - a + WIN] for a in range(GROUP)]
    toe = jnp.stack(rows, axis=1)
    return jnp.where(jnp.asarray(inband)[None], toe, NEG_INF)


def _rel_grad_of(dbias2, onehot):
    nj = GROUP + WIN - 1
    dbv = sum(jnp.pad(dbias2[:, a, :], ((0, 0), (GROUP - 1 - a, nj - WIN - (GROUP - 1 - a)))) for a in range(GROUP))
    return jnp.dot(dbv, jnp.asarray(onehot).T, precision=lax.Precision.HIGHEST)


def kernel(x, g_mix, w_in, b_f, b_gate, rel_bias, w_branch_a, w_branch_b, w_out, g_ffn, w_gate_ffn, w_up_ffn, w_down_ffn, g_final, loss_target, m_g_mix, m_w_in, m_b_f, m_b_gate, m_rel_bias, m_w_branch_a, m_w_branch_b, m_w_out, m_g_ffn, m_w_gate_ffn, m_w_up_ffn, m_w_down_ffn, m_g_final, v_g_mix, v_w_in, v_b_f, v_b_gate, v_rel_bias, v_w_branch_a, v_w_branch_b, v_w_out, v_g_ffn, v_w_gate_ffn, v_w_up_ffn, v_w_down_ffn, v_g_final):
    T, D = x.shape[1], x.shape[2]
    Ls = w_in.shape[2]
    W = w_branch_a.shape[1]
    nh = W // HEAD_DIM
    nhb = b_f.shape[1]
    assert w_branch_b.shape[1] == W and nhb == nh and rel_bias.shape[1] == nh
    W6 = 6 * W
    Fl = w_gate_ffn.shape[2]
    Fp = -(-Fl // LANE) * LANE
    n_rel = rel_bias.shape[2]
    chip = 2 * lax.axis_index("x") + lax.axis_index("y")
    lay, nbw, nmain = _in_layout(D, W6, nhb, Ls)
    onehot, inband = _rel_tables(n_rel)

    xs, tgt = x[0], loss_target[0]

    win_f32 = lax.switch(chip, [functools.partial(_to_window, lay_k=lay[k], nbw=nbw) for k in range(N_CHIPS)], w_in[0])
    pad_c = lambda w: jnp.pad(w, ((0, 0), (0, Fp - Fl)))
    pad_r = lambda w: jnp.pad(w, ((0, Fp - Fl), (0, 0)))
    sh_in = _cast_bf16("cast_w_in", win_f32, chip)
    sh_a = _cast_bf16("cast_w_a", w_branch_a[0], chip)
    sh_b = _cast_bf16("cast_w_b", w_branch_b[0], chip)
    sh_o = _cast_bf16("cast_w_out", w_out[0], chip)
    sh_g = _cast_bf16("cast_w_gate", pad_c(w_gate_ffn[0]), chip)
    sh_u = _cast_bf16("cast_w_up", pad_c(w_up_ffn[0]), chip)
    sh_d = _cast_bf16("cast_w_down", pad_r(w_down_ffn[0]), chip)
    (wins,) = _allgather("ag_w_in", [sh_in])
    wc = _assemble_in("assemble_w_in", wins, lay, nbw, nmain)

    h1, r1 = _rms_fwd("rms1", xs, g_mix)
    qkv, ((wa_g, wb_g, wo_g),) = _mm_nn("proj_qkv", h1, wc, BF16, b_col0=0, n=W6, tm=1024,
                                        jobs=[_job_gather_ici([sh_a, sh_b, sh_o])])
    gates, ((wa_g, wb_g, wo_g), (wg_g,)) = _mm_nn(
        "proj_gates", h1, wc, BF16, b_col0=W6, n=2 * D, tm=1024,
        jobs=[_job_gather_d2d([wa_g, wb_g, wo_g]), _job_gather_ici([sh_g], part=(0, 2))])
    fl = _mm_nn("proj_f", h1, wc, F32, b_col0=nmain, n=LANE, tn=LANE)
    bias2 = _bias2_of(rel_bias[0], onehot, inband)
    bf_pad = jnp.pad(b_f, ((0, 0), (0, LANE - nhb)))
    (o_a, lse_a), ((wg_g,),) = _attn_a_fwd("attn_a_fwd", qkv, bias2, nh, jobs=[_job_gather_ici([wg_g], part=(1, 2))])
    cum, cumt = _fox_prep("fox_prep", fl, bf_pad)
    (o_b, lse_b), ((wu_g,), (wg_g,)) = _fox_fwd("fox_fwd", qkv, cum, cumt, nh,
                                                jobs=[_job_gather_ici([sh_u]), _job_gather_d2d([wg_g])])
    u_a = _mm_nn("branch_a", o_a, wa_g, BF16, tm=1024)
    u_b = _mm_nn("branch_b", o_b, wb_g, BF16, tm=1024)
    merged = _merge_fwd("merge", gates, u_a, u_b, b_gate)
    wo_full = wo_g.reshape(D, D)
    x1, ((wu_g,),) = _mm_nn("out_proj", merged, wo_full, F32, residual=xs, tm=1024, tn=_pick(D, (1024, 512, 256, 128)),
                            jobs=[_job_gather_d2d([wu_g])])
    h2, r2 = _rms_fwd("rms2", x1, g_ffn)

    tm_f = _pick(T, (1024, 512, 256, 128))
    tn_f = _pick(Fp, (1408, 1024, 512, 256, 128))
    tk_f = _pick(D, (1024, 512, 256, 128))
    nps_f = Fp // tn_f

    def swiglu_ep(accs, e_refs, o_refs):
        g, u = accs
        o_refs[0][...] = g.astype(BF16)
        o_refs[1][...] = u.astype(BF16)
        o_refs[2][...] = (g * _sigmoid(g) * u).astype(BF16)

    hid_spec = pl.BlockSpec((tm_f, tn_f), lambda i, j, k: (i, j))
    wcol_spec = pl.BlockSpec((None, tk_f, tn_f), lambda i, j, k: (j // nps_f, k, j % nps_f))
    (gate, up, hidden), ((wd_g,),) = _mm(
        "ffn_up", "nn", [h2], [pl.BlockSpec((tm_f, tk_f), lambda i, j, k: (i, k))], [wg_g, wu_g], [wcol_spec, wcol_spec],
        [(0, 0, 0), (0, 1, 1)], 2, (T // tm_f, N_CHIPS * Fp // tn_f, D // tk_f), tm_f, tn_f,
        [_sds((T, N_CHIPS * Fp), BF16)] * 3, [hid_spec] * 3, swiglu_ep, jobs=[_job_gather_ici([sh_d])])
    ((wd_g,),) = _comm_only("ag_w_down_d2d", [_job_gather_d2d([wd_g])])
    wd_full = wd_g.reshape(N_CHIPS * Fp, D)
    x2 = _mm_nn("ffn_down", hidden, wd_full, F32, residual=x1, tm=1024, tn=_pick(D, (1024, 512, 256, 128)),
                tk=_pick(N_CHIPS * Fp, (1408, 1024, 512, 256, 128)))

    dx2, dx2b, loss_part, gg_final = _final_loss_bwd("final_loss", x2, tgt, g_final.reshape(1, D))

    def swiglu_bwd_ep(accs, e_refs, o_refs):
        dh = accs[0]
        g = e_refs[0][...].astype(F32)
        u = e_refs[1][...].astype(F32)
        sg = _sigmoid(g)
        o_refs[0][...] = (dh * u * (sg * (1.0 + g * (1.0 - sg)))).astype(BF16)
        o_refs[1][...] = (dh * (g * sg)).astype(BF16)

    tk_b = _pick(D, (1024, 512, 256, 128))
    core = lax.axis_index("c")
    (dgate, dup), _ = _mm(
        "ffn_down_bwd", "nt", [dx2b], [pl.BlockSpec((tm_f, tk_b), lambda i, j, k: (i, k))],
        [wd_full], [pl.BlockSpec((tn_f, tk_b), lambda i, j, k: (j, k))], [(0, 0, 0)], 1,
        (T // tm_f, N_CHIPS * Fp // tn_f, D // tk_b), tm_f, tn_f,
        [_sds((T, N_CHIPS * Fp), BF16)] * 2, [hid_spec] * 2, swiglu_bwd_ep,
        extra=[gate, up], extra_specs=[hid_spec, hid_spec])
    dwd = _mm_tn("dw_down", hidden, dx2b, BF16, tm=_pick(N_CHIPS * Fp, (1408, 1024, 512, 256, 128)))
    dwd = dwd.reshape(N_CHIPS, Fp, D)
    dh2, ((sib_d,),) = _mm_nt("ffn_up_bwd", [dgate, dup], [wg_g, wu_g], F32, tm=1024, jobs=[_job_sibling([dwd])])
    dwg = _mm_tn("dw_gate", h2, dgate, BF16, slots=N_CHIPS)
    dwu = _mm_tn("dw_up", h2, dup, BF16, slots=N_CHIPS)
    dx1, dx1b, gg_ffn = _rms_bwd("rms2_bwd", [dh2], x1, r2, g_ffn, dx2, True)
    part_d = _add_bf16("rs_add_down", dwd, core, sib_d)

    dmerged, ((sib_g, sib_u),) = _mm_nt("out_proj_bwd", [dx1b], [wo_full], BF16, tm=1024,
                                        jobs=[_job_sibling([dwg, dwu])])
    dwo = _mm_tn("dw_out", merged, dx1b, BF16).reshape(N_CHIPS, D // N_CHIPS, D)
    du_a, du_b, dga, dgb, gbg_a, gbg_b = _merge_bwd("merge_bwd", dmerged, gates, u_a, u_b, b_gate)
    part_g = _add_bf16("rs_add_gate", dwg, core, sib_g)
    part_u = _add_bf16("rs_add_up", dwu, core, sib_u)
    do_a = _mm_nt("branch_a_bwd", [du_a], [wa_g], BF16, tm=1024)
    do_b = _mm_nt("branch_b_bwd", [du_b], [wb_g], BF16, tm=1024)
    dwa = _mm_tn("dw_a", o_a, du_a, BF16, slots=N_CHIPS)
    dwb = _mm_tn("dw_b", o_b, du_b, BF16, slots=N_CHIPS)

    (dq_a, delta_a, dbias2), ((got_d,), (sib_a, sib_b, sib_o)) = _attn_a_dq(
        "attn_a_dq", qkv, do_a, lse_a, bias2, nh,
        jobs=[_job_scatter([part_d], part=(0, 2)), _job_sibling([dwa, dwb, dwo])])
    part_a = _add_bf16("rs_add_a", dwa, core, sib_a)
    part_b = _add_bf16("rs_add_b", dwb, core, sib_b)
    part_o = _add_bf16("rs_add_out", dwo, core, sib_o)
    (dk_a, dv_a), ((got_d,), (got_g,)) = _attn_a_dkv(
        "attn_a_dkv", qkv, do_a, lse_a, delta_a, bias2, nh,
        jobs=[_job_scatter([part_d], part=(1, 2), into=[got_d]), _job_scatter([part_g], part=(0, 2))])
    full_d = _sum4("rs_sum_down", got_d, part_d, chip, core)
    (dq_b, delta_b), ((got_g,), (got_u,)) = _fox_dq(
        "fox_dq", qkv, do_b, lse_b, cum, cumt, nh,
        jobs=[_job_scatter([part_g], part=(1, 2), into=[got_g]), _job_scatter([part_u])])
    full_g = _sum4("rs_sum_gate", got_g, part_g, chip, core)
    full_u = _sum4("rs_sum_up", got_u, part_u, chip, core)
    (dk_b, dv_b, dcumt), ((got_a, got_b, got_o),) = _fox_dkv(
        "fox_dkv", qkv, do_b, lse_b, delta_b, cum, cumt, nh, jobs=[_job_scatter([part_a, part_b, part_o])])
    full_a = _sum4("rs_sum_a", got_a, part_a, chip, core)
    full_b = _sum4("rs_sum_b", got_b, part_b, chip, core)
    full_o = _sum4("rs_sum_out", got_o, part_o, chip, core)
    df, gbf = _fox_post("fox_post", dcumt, fl, bf_pad)

    dqkv = jnp.concatenate([dq_a, dk_a, dv_a, dq_b, dk_b, dv_b], axis=1)
    dgates = jnp.concatenate([dga, dgb], axis=1)
    dwc_q, ((g_d, g_g, g_u, g_a, g_b, g_o),) = _mm_tn(
        "dw_in_qkv", h1, dqkv, BF16, jobs=[_job_swap([full_d, full_g, full_u, full_a, full_b, full_o])])
    dwc_g = _mm_tn("dw_in_gates", h1, dgates, BF16)
    dwc_f = _mm_tn("dw_in_f", h1, df, BF16, tn=LANE)
    dwc = jnp.concatenate([dwc_q, dwc_g, dwc_f], axis=1)
    zeros_blk = jnp.zeros((D, LANE), BF16)
    win_parts = []
    for k in range(N_CHIPS):
        cols = [dwc[:, lay[k]["s"] * LANE:lay[k]["e"] * LANE]]
        nb = lay[k]["e"] - lay[k]["s"]
        if lay[k]["f"]:
            cols.append(dwc[:, nmain:nmain + LANE])
            nb += 1
        cols += [zeros_blk] * (nbw - nb)
        win_parts.append(jnp.concatenate(cols, axis=1) if len(cols) > 1 else cols[0])
    dwin = jnp.stack(win_parts, axis=0)
    big = {}

    def adamw(nm, w, g, m, v, jobs=()):
        (d, mn, vn, go), jouts = _adamw(f"adamw_{nm}", w[0], g, m[0], v[0], jobs=jobs)
        big[nm] = (go[None], d[None], mn[None], vn[None])
        return jouts

    ((sib_in,),) = adamw("w_gate_ffn", w_gate_ffn, g_g, m_w_gate_ffn, v_w_gate_ffn, jobs=[_job_sibling([dwin])])
    part_in = _add_bf16("rs_add_in", dwin, core, sib_in)
    dh, ((got_in,),) = _mm_nt("proj_qkv_bwd", [dqkv], [wc], F32, k0_list=[0], tk=_pick(W6, (1024, 512, 256, 128)), tm=1024,
                              jobs=[_job_scatter([part_in], part=(0, 5, 8))])
    dh, ((got_in,),) = _mm_nt("proj_gates_bwd", [dgates], [wc], F32, k0_list=[W6], tm=1024,
                              tk=_pick(math_gcd(W6, 2 * D), (1024, 512, 256, 128)), residual=dh,
                              jobs=[_job_scatter([part_in], part=(5, 8, 8), into=[got_in])])
    full_in = _sum4("rs_sum_in", got_in, part_in, chip, core)
    dh, ((g_win,),) = _mm_nt("proj_f_bwd", [df], [wc], F32, k0_list=[nmain], tk=LANE, residual=dh,
                             jobs=[_job_swap([full_in])])
    grad_x, gg_mix = _rms_bwd("rms1_bwd", [dh], xs, r1, g_mix, dx1, False)
    g_in = lax.switch(chip, [functools.partial(_from_window, lay_k=lay[k]) for k in range(N_CHIPS)], g_win)

    for nm, w, g, m, v in (("w_in", w_in, g_in, m_w_in, v_w_in), ("w_branch_a", w_branch_a, g_a, m_w_branch_a, v_w_branch_a),
                           ("w_branch_b", w_branch_b, g_b, m_w_branch_b, v_w_branch_b), ("w_out", w_out, g_o, m_w_out, v_w_out),
                           ("w_up_ffn", w_up_ffn, g_u, m_w_up_ffn, v_w_up_ffn),
                           ("w_down_ffn", w_down_ffn, g_d, m_w_down_ffn, v_w_down_ffn)):
        adamw(nm, w, g, m, v)

    g_rel = _rel_grad_of(dbias2, onehot)
    small = [("loss", loss_part[:, :1], None, None, None),
             ("g_mix", gg_mix, g_mix, m_g_mix, v_g_mix), ("b_f", gbf[:, :nhb], b_f, m_b_f, v_b_f),
             ("b_gate", jnp.concatenate([gbg_a, gbg_b], axis=1), b_gate, m_b_gate, v_b_gate),
             ("rel_bias", g_rel, rel_bias, m_rel_bias, v_rel_bias), ("g_ffn", gg_ffn, g_ffn, m_g_ffn, v_g_ffn),
             ("g_final", gg_final, g_final, m_g_final, v_g_final)]
    sizes = [int(np.prod(s[1].shape)) for s in small]
    total = sum(sizes)
    npad = -(-total // 1024) * 1024

    def pack(arrs):
        flat = jnp.concatenate([a.reshape(-1).astype(F32) for a in arrs])
        return jnp.pad(flat, (0, npad - total)).reshape(8, npad // 8)

    zero1 = jnp.zeros((1,), F32)
    g_all = _small_allreduce("small_allreduce", pack([s[1] for s in small]))
    w_s = pack([zero1 if s[2] is None else s[2] for s in small])
    m_s = pack([zero1 if s[3] is None else s[3] for s in small])
    v_s = pack([zero1 + 1.0 if s[4] is None else s[4] for s in small])
    (d_s, mn_s, vn_s, _), _ = _adamw("adamw_small", w_s, g_all, m_s, v_s)

    def unpack(packed):
        flat = packed.reshape(-1)
        out, pos = {}, 0
        for s, n in zip(small, sizes):
            if s[2] is not None:
                out[s[0]] = flat[pos:pos + n].reshape(s[2].shape)
            else:
                out[s[0]] = flat[pos:pos + n].reshape(())
            pos += n
        return out

    gs, ds, ms, vs = unpack(g_all), unpack(d_s), unpack(mn_s), unpack(vn_s)
    order = ["g_mix", "w_in", "b_f", "b_gate", "rel_bias", "w_branch_a", "w_branch_b", "w_out", "g_ffn",
             "w_gate_ffn", "w_up_ffn", "w_down_ffn", "g_final"]
    res = [[], [], [], []]
    for nm in order:
        four = big[nm] if nm in big else (gs[nm], ds[nm], ms[nm], vs[nm])
        for q in range(4):
            res[q].append(four[q])
    return (gs["loss"], grad_x[None], *res[0], *res[1], *res[2], *res[3])


def math_gcd(a, b):
    while b:
        a, b = b, a % b
    return a
```

```python
import functools

import numpy as np
import jax
import jax.numpy as jnp
from jax import lax
from jax.experimental import pallas as pl
from jax.experimental.pallas import tpu as pltpu

F32 = jnp.float32
BF16 = jnp.bfloat16
LANE = 128
HEAD_DIM = 128
CHUNK = 64
LEFT_CHUNKS = 8
GROUP = 128
WIN_BLOCKS = 5
WIN = WIN_BLOCKS * GROUP
BAND = (LEFT_CHUNKS + 1) * CHUNK
RMS_EPS = 1e-6
NEG_INF = -1e30
ADAM_LR = 0.001
ADAM_B1 = 0.9
ADAM_B2 = 0.999
ADAM_EPS = 1e-08
ADAM_WD = 0.01
ADAM_STEP = 10
N_CHIPS = 4
MESH = pl.DeviceIdType.MESH
VMEM_LIMIT = 52 * 1024 * 1024
ANY = pl.BlockSpec(memory_space=pl.ANY)

NN = (((1,), (0,)), ((), ()))
NT = (((1,), (1,)), ((), ()))
TN = (((0,), (0,)), ((), ()))


def _cp(sem):
    return pltpu.CompilerParams(dimension_semantics=sem, vmem_limit_bytes=VMEM_LIMIT)


def _sds(shape, dtype):
    return jax.ShapeDtypeStruct(shape, dtype)


def _pick(n, prefs):
    for p in prefs:
        if n % p == 0:
            return p
    return n


def _sigmoid(v):
    return 1.0 / (1.0 + jnp.exp(-v))


def _split3(v):
    hi = v.astype(BF16)
    r1 = v - hi.astype(F32)
    mid = r1.astype(BF16)
    lo = (r1 - mid.astype(F32)).astype(BF16)
    return hi, mid, lo


def _col_of(blk, h):
    lane = lax.broadcasted_iota(jnp.int32, blk.shape, 1)
    return jnp.sum(jnp.where(lane == h, blk, 0.0), axis=1, keepdims=True)


def _put_col(ref, h, col):
    lane = lax.broadcasted_iota(jnp.int32, ref.shape, 1)
    ref[...] = jnp.where(lane == h, col, ref[...])


def _mm(name, mode, a_list, a_specs, b_list, b_specs, pairs, n_acc, grid, tm, tn,
        out_shapes, out_specs, epilogue, extra=(), extra_specs=(), jobs=()):
    n_a, n_b, n_e, n_o = len(a_list), len(b_list), len(extra), len(out_shapes)
    nk = grid[2]
    dn = {"nn": NN, "nt": NT, "tn": TN}[mode]

    def body(*refs):
        a_refs = refs[:n_a]
        b_refs = refs[n_a:n_a + n_b]
        e_refs = refs[n_a + n_b:n_a + n_b + n_e]
        o_refs = refs[n_a + n_b + n_e:n_a + n_b + n_e + n_o]
        acc_refs = refs[n_a + n_b + n_e + n_o:]
        k = pl.program_id(2)

        @pl.when(k == 0)
        def _():
            for acc in acc_refs:
                acc[...] = jnp.zeros_like(acc)

        for ai, bi, ci in pairs:
            acc_refs[ci][...] += lax.dot_general(a_refs[ai][...], b_refs[bi][...], dn,
                                                 preferred_element_type=F32)

        @pl.when(k == nk - 1)
        def _():
            epilogue([acc[...] for acc in acc_refs], e_refs, o_refs)

    return _pcall(
        body, name=name, grid=grid,
        in_specs=list(a_specs) + list(b_specs) + list(extra_specs),
        out_specs=list(out_specs), out_shape=list(out_shapes),
        scratch_shapes=[pltpu.VMEM((tm, tn), F32) for _ in range(n_acc)],
        sem=("parallel", "parallel", "arbitrary"), jobs=jobs,
    )(*a_list, *b_list, *extra)


def _one(res, jobs):
    outs, jouts = res
    return (outs[0], jouts) if jobs else outs[0]


def _store(dtype):
    def ep(accs, e_refs, o_refs):
        o_refs[0][...] = accs[0].astype(dtype)
    return ep


def _mm_nn(name, a, b, out_dtype, *, b_col0=0, n=None, tm=512, tn=None, tk=None, residual=None, jobs=()):
    M, K = a.shape
    if b.ndim == 3:
        Ns = b.shape[2]
        n = b.shape[0] * Ns
        tn = tn or _pick(Ns, (1408, 1024, 512, 256, 128))
        nps = Ns // tn
        b_spec = pl.BlockSpec((None, tk or _pick(K, (1024, 512, 256, 128)), tn),
                              lambda i, j, k: (j // nps, k, j % nps))
    else:
        n = n or b.shape[1]
        tn = tn or _pick(math_gcd(n, b_col0) if b_col0 else n, (2048, 1024, 512, 256, 128))
        assert b_col0 % tn == 0 and n % tn == 0
        c0 = b_col0 // tn
        b_spec = pl.BlockSpec((tk or _pick(K, (1024, 512, 256, 128)), tn), lambda i, j, k: (k, c0 + j))
    tk = tk or _pick(K, (1024, 512, 256, 128))
    tm = _pick(M, (tm, 256, 128))
    grid = (M // tm, n // tn, K // tk)
    a_spec = pl.BlockSpec((tm, tk), lambda i, j, k: (i, k))
    o_spec = pl.BlockSpec((tm, tn), lambda i, j, k: (i, j))
    if residual is None:
        return _one(_mm(name, "nn", [a], [a_spec], [b], [b_spec], [(0, 0, 0)], 1, grid, tm, tn,
                        [_sds((M, n), out_dtype)], [o_spec], _store(out_dtype), jobs=jobs), jobs)

    def ep(accs, e_refs, o_refs):
        o_refs[0][...] = (e_refs[0][...] + accs[0]).astype(out_dtype)
    return _one(_mm(name, "nn", [a], [a_spec], [b], [b_spec], [(0, 0, 0)], 1, grid, tm, tn,
                    [_sds((M, n), out_dtype)], [o_spec], ep, extra=[residual], extra_specs=[o_spec], jobs=jobs), jobs)


def _mm_nt(name, a_list, b_list, out_dtype, *, k0_list=None, tm=512, tn=None, tk=None, residual=None, jobs=()):
    M, K = a_list[0].shape
    b0 = b_list[0]
    N = b0.shape[1] if b0.ndim == 3 else b0.shape[0]
    tm = _pick(M, (tm, 256, 128))
    tn = tn or _pick(N, (1024, 512, 256, 128))
    if b0.ndim == 3:
        Ks = b0.shape[2]
        tk = tk or _pick(Ks, (1408, 1024, 512, 256, 128))
        kps = Ks // tk
        b_specs = [pl.BlockSpec((None, tn, tk), lambda i, j, k: (k // kps, j, k % kps)) for _ in b_list]
    else:
        tk = tk or _pick(K, (1024, 896, 512, 256, 128))
        k0_list = k0_list or [0] * len(b_list)
        b_specs = []
        for k0 in k0_list:
            assert k0 % tk == 0
            b_specs.append(pl.BlockSpec((tn, tk), functools.partial(lambda i, j, k, c: (j, c + k), c=k0 // tk)))
    grid = (M // tm, N // tn, K // tk)
    a_specs = [pl.BlockSpec((tm, tk), lambda i, j, k: (i, k)) for _ in a_list]
    o_spec = pl.BlockSpec((tm, tn), lambda i, j, k: (i, j))
    pairs = [(p, p, 0) for p in range(len(a_list))]
    if residual is None:
        return _one(_mm(name, "nt", a_list, a_specs, b_list, b_specs, pairs, 1, grid, tm, tn,
                        [_sds((M, N), out_dtype)], [o_spec], _store(out_dtype), jobs=jobs), jobs)

    def ep(accs, e_refs, o_refs):
        o_refs[0][...] = (e_refs[0][...] + accs[0]).astype(out_dtype)
    return _one(_mm(name, "nt", a_list, a_specs, b_list, b_specs, pairs, 1, grid, tm, tn,
                    [_sds((M, N), out_dtype)], [o_spec], ep, extra=[residual], extra_specs=[o_spec], jobs=jobs), jobs)


def _mm_tn(name, a, b, out_dtype, *, slots=None, tm=None, tn=None, tk=1024, jobs=()):
    Kc, Mo = a.shape
    No = b.shape[1]
    tm = tm or _pick(Mo, (1024, 704, 512, 256, 128))
    tk = _pick(Kc, (tk, 256, 128))
    if slots:
        Ns = No // slots
        tn = tn or _pick(Ns, (1408, 1024, 512, 256, 128))
        nps = Ns // tn
        o_spec = pl.BlockSpec((None, tm, tn), lambda i, j, k: (j // nps, i, j % nps))
        o_shape = _sds((slots, Mo, Ns), out_dtype)
    else:
        tn = tn or _pick(No, (1024, 512, 256, 128))
        o_spec = pl.BlockSpec((tm, tn), lambda i, j, k: (i, j))
        o_shape = _sds((Mo, No), out_dtype)
    grid = (Mo // tm, No // tn, Kc // tk)
    a_spec = pl.BlockSpec((tk, tm), lambda i, j, k: (k, i))
    b_spec = pl.BlockSpec((tk, tn), lambda i, j, k: (k, j))
    return _one(_mm(name, "tn", [a], [a_spec], [b], [b_spec], [(0, 0, 0)], 1, grid, tm, tn,
                    [o_shape], [o_spec], _store(out_dtype), jobs=jobs), jobs)


def _cast_bf16(name, w, chip):
    R, C = w.shape
    tr = _pick(R, (256, 128, 64, 32, 16))

    def body(k_ref, w_ref, o_ref):
        o_ref[...] = w_ref[...].astype(BF16)

    gs = pltpu.PrefetchScalarGridSpec(
        num_scalar_prefetch=1, grid=(R // tr,),
        in_specs=[pl.BlockSpec((tr, C), lambda i, k: (i, 0))],
        out_specs=pl.BlockSpec((None, tr, C), lambda i, k: (k[0], i, 0)))
    return pl.pallas_call(body, name=name, grid_spec=gs, out_shape=_sds((N_CHIPS, R, C), BF16),
                          compiler_params=_cp(("parallel",)))(jnp.reshape(chip, (1,)).astype(jnp.int32), w)


def _rms_fwd(name, x, g):
    T, D = x.shape
    tr = _pick(T, (256, 128))

    def body(x_ref, g_ref, h_ref, r_ref):
        xv = x_ref[...]
        r = lax.rsqrt(jnp.mean(xv * xv, axis=1, keepdims=True) + RMS_EPS)
        h_ref[...] = (xv * r * g_ref[...]).astype(BF16)
        r_ref[...] = r

    row = pl.BlockSpec((tr, D), lambda i: (i, 0))
    return pl.pallas_call(
        body, name=name, grid=(T // tr,),
        in_specs=[row, pl.BlockSpec((1, D), lambda i: (0, 0))],
        out_specs=[row, pl.BlockSpec((tr, 1), lambda i: (i, 0))],
        out_shape=[_sds((T, D), BF16), _sds((T, 1), F32)], compiler_params=_cp(("parallel",)),
    )(x, g)


def _final_loss_bwd(name, x2, tgt, g):
    T, D = x2.shape
    tr = _pick(T, (256, 128))

    def body(x_ref, t_ref, g_ref, dx_ref, dxb_ref, loss_ref, gg_ref):
        @pl.when(pl.program_id(0) == 0)
        def _():
            loss_ref[...] = jnp.zeros_like(loss_ref)
            gg_ref[...] = jnp.zeros_like(gg_ref)

        xv = x_ref[...]
        gv = g_ref[...]
        r = lax.rsqrt(jnp.mean(xv * xv, axis=1, keepdims=True) + RMS_EPS)
        n = xv * r
        e = n * gv - t_ref[...]
        loss_ref[...] += 0.5 * jnp.sum(jnp.mean(e * e, axis=1, keepdims=True), axis=0, keepdims=True)
        dy = e * (1.0 / D)
        gg_ref[...] += jnp.sum(dy * n, axis=0, keepdims=True)
        gy = dy * gv
        dx = r * (gy - n * jnp.mean(gy * n, axis=1, keepdims=True))
        dx_ref[...] = dx
        dxb_ref[...] = dx.astype(BF16)

    row = pl.BlockSpec((tr, D), lambda i: (i, 0))
    vec = pl.BlockSpec((1, D), lambda i: (0, 0))
    return pl.pallas_call(
        body, name=name, grid=(T // tr,),
        in_specs=[row, row, vec],
        out_specs=[row, row, pl.BlockSpec((1, LANE), lambda i: (0, 0)), vec],
        out_shape=[_sds((T, D), F32), _sds((T, D), BF16), _sds((1, LANE), F32), _sds((1, D), F32)],
        compiler_params=_cp(("arbitrary",)),
    )(x2, tgt, g)


def _rms_bwd(name, dh_list, x, r, g, dres, want_bf16):
    T, D = x.shape
    tr = _pick(T, (128,))
    n_dh = len(dh_list)

    def body(*refs):
        dh_refs = refs[:n_dh]
        x_ref, r_ref, g_ref, dres_ref = refs[n_dh:n_dh + 4]
        outs = refs[n_dh + 4:]
        gg_ref = outs[-1]

        @pl.when(pl.program_id(0) == 0)
        def _():
            gg_ref[...] = jnp.zeros_like(gg_ref)

        dh = dh_refs[0][...]
        for ref in dh_refs[1:]:
            dh = dh + ref[...]
        rv = r_ref[...]
        n = x_ref[...] * rv
        gg_ref[...] += jnp.sum(dh * n, axis=0, keepdims=True)
        gy = dh * g_ref[...]
        dx = dres_ref[...] + rv * (gy - n * jnp.mean(gy * n, axis=1, keepdims=True))
        outs[0][...] = dx
        if want_bf16:
            outs[1][...] = dx.astype(BF16)

    row = pl.BlockSpec((tr, D), lambda i: (i, 0))
    vec = pl.BlockSpec((1, D), lambda i: (0, 0))
    out_specs = [row] + ([row] if want_bf16 else []) + [vec]
    out_shape = [_sds((T, D), F32)] + ([_sds((T, D), BF16)] if want_bf16 else []) + [_sds((1, D), F32)]
    return pl.pallas_call(
        body, name=name, grid=(T // tr,),
        in_specs=[row] * n_dh + [row, pl.BlockSpec((tr, 1), lambda i: (i, 0)), vec, row],
        out_specs=out_specs, out_shape=out_shape, compiler_params=_cp(("arbitrary",)),
    )(*dh_list, x, r, g, dres)


def _merge_fwd(name, gates, u_a, u_b, b_gate):
    T, D = u_a.shape
    tr = _pick(T, (256, 128))

    def body(ga_ref, gb_ref, ua_ref, ub_ref, ba_ref, bb_ref, o_ref):
        sa = _sigmoid(ga_ref[...].astype(F32) + ba_ref[...])
        sb = _sigmoid(gb_ref[...].astype(F32) + bb_ref[...])
        o_ref[...] = (sa * ua_ref[...].astype(F32) + sb * ub_ref[...].astype(F32)).astype(BF16)

    row = pl.BlockSpec((tr, D), lambda i: (i, 0))
    row1 = pl.BlockSpec((tr, D), lambda i: (i, 1))
    v0 = pl.BlockSpec((1, D), lambda i: (0, 0))
    v1 = pl.BlockSpec((1, D), lambda i: (0, 1))
    return pl.pallas_call(
        body, name=name, grid=(T // tr,),
        in_specs=[row, row1, row, row, v0, v1], out_specs=row,
        out_shape=_sds((T, D), BF16), compiler_params=_cp(("parallel",)),
    )(gates, gates, u_a, u_b, b_gate, b_gate)


def _merge_bwd(name, dm, gates, u_a, u_b, b_gate):
    T, D = u_a.shape
    tr = _pick(T, (128,))

    def body(dm_ref, ga_ref, gb_ref, ua_ref, ub_ref, ba_ref, bb_ref, dua_ref, dub_ref, dga_ref, dgb_ref,
             gba_ref, gbb_ref):
        @pl.when(pl.program_id(0) == 0)
        def _():
            gba_ref[...] = jnp.zeros_like(gba_ref)
            gbb_ref[...] = jnp.zeros_like(gbb_ref)

        d = dm_ref[...].astype(F32)
        sa = _sigmoid(ga_ref[...].astype(F32) + ba_ref[...])
        sb = _sigmoid(gb_ref[...].astype(F32) + bb_ref[...])
        dua_ref[...] = (d * sa).astype(BF16)
        dub_ref[...] = (d * sb).astype(BF16)
        dga = d * ua_ref[...].astype(F32) * sa * (1.0 - sa)
        dgb = d * ub_ref[...].astype(F32) * sb * (1.0 - sb)
        dga_ref[...] = dga.astype(BF16)
        dgb_ref[...] = dgb.astype(BF16)
        gba_ref[...] += jnp.sum(dga, axis=0, keepdims=True)
        gbb_ref[...] += jnp.sum(dgb, axis=0, keepdims=True)

    row = pl.BlockSpec((tr, D), lambda i: (i, 0))
    row1 = pl.BlockSpec((tr, D), lambda i: (i, 1))
    v0 = pl.BlockSpec((1, D), lambda i: (0, 0))
    v1 = pl.BlockSpec((1, D), lambda i: (0, 1))
    outs = pl.pallas_call(
        body, name=name, grid=(T // tr,),
        in_specs=[row, row, row1, row, row, v0, v1],
        out_specs=[row, row, row, row, v0, v0],
        out_shape=[_sds((T, D), BF16), _sds((T, D), BF16), _sds((T, D), BF16), _sds((T, D), BF16),
                   _sds((1, D), F32), _sds((1, D), F32)],
        compiler_params=_cp(("arbitrary",)),
    )(dm, gates, gates, u_a, u_b, b_gate, b_gate)
    return outs


def _adamw(name, w, g, m, v, jobs=()):
    R, C = w.shape
    Cg = g.shape[1]
    tr = _pick(R, (64, 32, 16, 8))
    c1 = 1.0 - ADAM_B1 ** ADAM_STEP
    c2 = 1.0 - ADAM_B2 ** ADAM_STEP

    def body(w_ref, g_ref, m_ref, v_ref, d_ref, mo_ref, vo_ref, go_ref):
        gv = g_ref[...] if Cg == C else g_ref[:, :C]
        mn = ADAM_B1 * m_ref[...] + (1.0 - ADAM_B1) * gv
        vn = ADAM_B2 * v_ref[...] + (1.0 - ADAM_B2) * (gv * gv)
        d_ref[...] = -ADAM_LR * ((mn / c1) / (jnp.sqrt(vn / c2) + ADAM_EPS) + ADAM_WD * w_ref[...])
        mo_ref[...] = mn
        vo_ref[...] = vn
        go_ref[...] = gv

    blk = pl.BlockSpec((tr, C), lambda i: (i, 0))
    gblk = pl.BlockSpec((tr, Cg), lambda i: (i, 0))
    return _pcall(
        body, name=name, grid=(R // tr,),
        in_specs=[blk, gblk, blk, blk], out_specs=[blk] * 4,
        out_shape=[_sds((R, C), F32)] * 4, sem=("parallel",), jobs=jobs,
    )(w, g, m, v)


def _add_bf16(name, a, a_row0, b):
    S, h, C = b.shape
    tr = _pick(h, (256, 128, 64, 32, 16))
    nb = h // tr

    def body(off_ref, a_ref, b_ref, o_ref):
        o_ref[...] = (a_ref[...].astype(F32) + b_ref[...].astype(F32)).astype(BF16)

    gs = pltpu.PrefetchScalarGridSpec(
        num_scalar_prefetch=1, grid=(S, nb),
        in_specs=[pl.BlockSpec((None, tr, C), lambda s, i, off: (s, off[0] * nb + i, 0)),
                  pl.BlockSpec((None, tr, C), lambda s, i, off: (s, i, 0))],
        out_specs=pl.BlockSpec((None, tr, C), lambda s, i, off: (s, i, 0)))
    return pl.pallas_call(body, name=name, grid_spec=gs, out_shape=_sds((S, h, C), BF16),
                          compiler_params=_cp(("parallel", "parallel")))(
        jnp.reshape(a_row0, (1,)).astype(jnp.int32), a, b)


def _sum4(name, got, mine, chip, core):
    S, h, C = got.shape
    tr = _pick(h, (256, 128, 64, 32, 16))
    nb = h // tr

    def body(chip_ref, core_ref, m_ref, g_ref, o_ref):
        acc = m_ref[...].astype(F32)
        for s in range(S):
            acc = acc + g_ref[s].astype(F32)
        o_ref[...] = acc

    gs = pltpu.PrefetchScalarGridSpec(
        num_scalar_prefetch=2, grid=(nb,),
        in_specs=[pl.BlockSpec((None, tr, C), lambda i, kc, cc: (kc[0], i, 0)),
                  pl.BlockSpec((S, tr, C), lambda i, kc, cc: (0, i, 0))],
        out_specs=pl.BlockSpec((tr, C), lambda i, kc, cc: (cc[0] * nb + i, 0)))
    return pl.pallas_call(body, name=name, grid_spec=gs, out_shape=_sds((2 * h, C), F32),
                          compiler_params=_cp(("parallel",)))(
        jnp.reshape(chip, (1,)).astype(jnp.int32), jnp.reshape(core, (1,)).astype(jnp.int32), mine, got)


def _place():
    x, y, c = lax.axis_index("x"), lax.axis_index("y"), lax.axis_index("c")
    chips = [(1 - x, y), (x, 1 - y), (1 - x, 1 - y)]
    return x, y, c, chips


def _allgather(name, shards):
    n = len(shards)
    NS = 7

    def body(*refs):
        out_refs = refs[n:2 * n]
        ss, rs = refs[2 * n:]
        x, y, c, _ = _place()
        k, kx, ky, kd = 2 * x + y, 2 * (1 - x) + y, 2 * x + (1 - y), 2 * (1 - x) + (1 - y)
        across_x, across_y, sibling = (1 - x, y, c), (x, 1 - y, c), (x, y, 1 - c)
        sends = []

        def go(cp):
            cp.start()
            sends.append(cp)

        for a, out in enumerate(out_refs):
            h = out.shape[1] // 2
            q = h // 2
            half = lambda slot, cc=c: out.at[slot, pl.ds(cc * h, h), :]
            part0 = lambda slot: out.at[slot, pl.ds(c * h, q), :]
            part1 = lambda slot: out.at[slot, pl.ds(c * h + q, q), :]
            b = NS * a
            go(_rdma(half(k), half(k), ss, rs, b + 0, across_x))
            go(_rdma(half(k), half(k), ss, rs, b + 1, across_y))
            _rdma(half(kx), half(kx), ss, rs, b + 0, across_x).wait_recv()
            go(_rdma(part0(kx), part0(kx), ss, rs, b + 2, across_y))
            go(_rdma(half(kx), half(kx), ss, rs, b + 4, sibling))
            _rdma(half(ky), half(ky), ss, rs, b + 1, across_y).wait_recv()
            go(_rdma(part1(ky), part1(ky), ss, rs, b + 3, across_x))
            go(_rdma(half(ky), half(ky), ss, rs, b + 5, sibling))
            _rdma(part0(kd), part0(kd), ss, rs, b + 2, across_y).wait_recv()
            _rdma(part1(kd), part1(kd), ss, rs, b + 3, across_x).wait_recv()
            go(_rdma(half(kd), half(kd), ss, rs, b + 6, sibling))
        for a, out in enumerate(out_refs):
            h = out.shape[1] // 2
            for j, slot in enumerate((kx, ky, kd)):
                rows = out.at[slot, pl.ds((1 - c) * h, h), :]
                _rdma(rows, rows, ss, rs, NS * a + 4 + j, sibling).wait_recv()
        for cp in sends:
            cp.wait_send()

    return pl.pallas_call(
        body, name=name,
        in_specs=[ANY] * n, out_specs=[ANY] * n,
        out_shape=[_sds(s.shape, s.dtype) for s in shards],
        input_output_aliases={a: a for a in range(n)},
        scratch_shapes=[pltpu.SemaphoreType.DMA((NS * n,)), pltpu.SemaphoreType.DMA((NS * n,))],
    )(*shards)


class _Job:
    def __init__(self, ins, out_shapes, aliases, n_sems, start, finish):
        self.ins, self.out_shapes, self.aliases, self.n_sems = list(ins), list(out_shapes), dict(aliases), n_sems
        self.start, self.finish = start, finish


def _rdma(src, dst, ss, rs, idx, to):
    return pltpu.make_async_remote_copy(src_ref=src, dst_ref=dst, send_sem=ss.at[idx], recv_sem=rs.at[idx],
                                        device_id=to, device_id_type=MESH)


def _job_gather_ici(bufs, part=(0, 1)):
    pi, pn = part

    def descs(outs, ss, rs, incoming):
        x, y, c, chips = _place()
        res = []
        for a, out in enumerate(outs):
            h = out.shape[1] // 2
            hp = h // pn
            for j, (cx, cy) in enumerate(chips):
                rows = out.at[(2 * cx + cy) if incoming else (2 * x + y), pl.ds(c * h + pi * hp, hp), :]
                res.append(_rdma(rows, rows, ss, rs, 3 * a + j, (cx, cy, c)))
        return res

    def start(ins, outs, ss, rs):
        for d in descs(outs, ss, rs, False):
            d.start()

    def finish(ins, outs, ss, rs):
        for d in descs(outs, ss, rs, True):
            d.wait_recv()
        for d in descs(outs, ss, rs, False):
            d.wait_send()

    return _Job(bufs, [_sds(b.shape, b.dtype) for b in bufs], {a: a for a in range(len(bufs))}, 3 * len(bufs),
                start, finish)


def _job_gather_d2d(bufs):
    def descs(outs, ss, rs, incoming):
        x, y, c, chips = _place()
        res = []
        for a, out in enumerate(outs):
            h = out.shape[1] // 2
            for j, (cx, cy) in enumerate(chips):
                rows = out.at[2 * cx + cy, pl.ds(((1 - c) if incoming else c) * h, h), :]
                res.append(_rdma(rows, rows, ss, rs, 3 * a + j, (x, y, 1 - c)))
        return res

    def start(ins, outs, ss, rs):
        for d in descs(outs, ss, rs, False):
            d.start()

    def finish(ins, outs, ss, rs):
        for d in descs(outs, ss, rs, True):
            d.wait_recv()
        for d in descs(outs, ss, rs, False):
            d.wait_send()

    return _Job(bufs, [_sds(b.shape, b.dtype) for b in bufs], {a: a for a in range(len(bufs))}, 3 * len(bufs),
                start, finish)


def _job_sibling(grads):
    def descs(ins, outs, ss, rs):
        x, y, c, _ = _place()
        res = []
        for a, (g, out) in enumerate(zip(ins, outs)):
            h = g.shape[1] // 2
            res.append(_rdma(g.at[:, pl.ds((1 - c) * h, h), :], out, ss, rs, a, (x, y, 1 - c)))
        return res

    def start(ins, outs, ss, rs):
        for d in descs(ins, outs, ss, rs):
            d.start()

    def finish(ins, outs, ss, rs):
        for d in descs(ins, outs, ss, rs):
            d.wait()

    return _Job(grads, [_sds((g.shape[0], g.shape[1] // 2, g.shape[2]), g.dtype) for g in grads], {}, len(grads),
                start, finish)


def _job_scatter(parts, part=(0, 1), into=None):
    p0, p1, pn = part if len(part) == 3 else (part[0], part[0] + 1, part[1])
    n = len(parts)

    def descs(ins, outs, ss, rs):
        x, y, c, chips = _place()
        res = []
        for a, (p, out) in enumerate(zip(ins[:n], outs)):
            hp = p.shape[1] // pn
            rows = pl.ds(p0 * hp, (p1 - p0) * hp)
            for j, (cx, cy) in enumerate(chips):
                res.append(_rdma(p.at[2 * cx + cy, rows, :], out.at[j, rows, :], ss, rs, 3 * a + j, (cx, cy, c)))
        return res

    def start(ins, outs, ss, rs):
        for d in descs(ins, outs, ss, rs):
            d.start()

    def finish(ins, outs, ss, rs):
        for d in descs(ins, outs, ss, rs):
            d.wait()

    shapes = [_sds((3,) + p.shape[1:], p.dtype) for p in parts]
    if into is None:
        return _Job(parts, shapes, {}, 3 * n, start, finish)
    return _Job(list(parts) + list(into), shapes, {n + a: a for a in range(n)}, 3 * n, start, finish)


def _job_swap(fulls):
    def descs(outs, ss, rs, incoming):
        x, y, c, _ = _place()
        res = []
        for a, out in enumerate(outs):
            h = out.shape[0] // 2
            rows = out.at[pl.ds(((1 - c) if incoming else c) * h, h), :]
            res.append(_rdma(rows, rows, ss, rs, a, (x, y, 1 - c)))
        return res

    def start(ins, outs, ss, rs):
        for d in descs(outs, ss, rs, False):
            d.start()

    def finish(ins, outs, ss, rs):
        for d in descs(outs, ss, rs, True):
            d.wait_recv()
        for d in descs(outs, ss, rs, False):
            d.wait_send()

    return _Job(fulls, [_sds(f.shape, f.dtype) for f in fulls], {a: a for a in range(len(fulls))}, len(fulls),
                start, finish)


def _pcall(body, *, name, grid, in_specs, out_specs, out_shape, scratch_shapes=(), sem, jobs=()):
    in_specs, out_specs, out_shape = list(in_specs), list(out_specs), list(out_shape)
    scratch = list(scratch_shapes)
    n_in, n_out, n_scr = len(in_specs), len(out_shape), len(scratch)
    if not jobs:
        call = pl.pallas_call(body, name=name, grid=grid, in_specs=in_specs, out_specs=out_specs, out_shape=out_shape,
                              scratch_shapes=scratch, compiler_params=_cp(sem))
        return lambda *args: (call(*args), [])
    jin = sum(len(j.ins) for j in jobs)
    jout = sum(len(j.out_shapes) for j in jobs)
    aliases, pi, po = {}, n_in, n_out
    for j in jobs:
        for ia, oa in j.aliases.items():
            aliases[pi + ia] = po + oa
        pi, po = pi + len(j.ins), po + len(j.out_shapes)

    def wrapped(*refs):
        ins = refs[:n_in]
        jins = refs[n_in:n_in + jin]
        outs = refs[n_in + jin:n_in + jin + n_out]
        jouts = refs[n_in + jin + n_out:n_in + jin + n_out + jout]
        scr = refs[n_in + jin + n_out + jout:n_in + jin + n_out + jout + n_scr]
        sems = refs[n_in + jin + n_out + jout + n_scr:]
        first, last = None, None
        for d, g in enumerate(grid):
            f, l = pl.program_id(d) == 0, pl.program_id(d) == g - 1
            first = f if first is None else jnp.logical_and(first, f)
            last = l if last is None else jnp.logical_and(last, l)

        def each(what):
            pi, po = 0, 0
            for q, j in enumerate(jobs):
                getattr(j, what)(jins[pi:pi + len(j.ins)], jouts[po:po + len(j.out_shapes)], sems[2 * q], sems[2 * q + 1])
                pi, po = pi + len(j.ins), po + len(j.out_shapes)

        @pl.when(first)
        def _():
            each("start")

        body(*ins, *outs, *scr)

        @pl.when(last)
        def _():
            each("finish")

    call = pl.pallas_call(
        wrapped, name=name, grid=grid,
        in_specs=in_specs + [ANY] * jin, out_specs=out_specs + [ANY] * jout,
        out_shape=out_shape + [s for j in jobs for s in j.out_shapes],
        input_output_aliases=aliases,
        scratch_shapes=scratch + [pltpu.SemaphoreType.DMA((j.n_sems,)) for j in jobs for _ in range(2)],
        compiler_params=_cp(("arbitrary",) * len(grid)))

    def run(*args):
        res = call(*args, *[a for j in jobs for a in j.ins])
        main, rest, per_job = list(res[:n_out]), list(res[n_out:]), []
        for j in jobs:
            per_job.append(rest[:len(j.out_shapes)])
            rest = rest[len(j.out_shapes):]
        return main, per_job
    return run


def _comm_only(name, jobs):
    jin = sum(len(j.ins) for j in jobs)
    jout = sum(len(j.out_shapes) for j in jobs)
    aliases, pi, po = {}, 0, 0
    for j in jobs:
        for ia, oa in j.aliases.items():
            aliases[pi + ia] = po + oa
        pi, po = pi + len(j.ins), po + len(j.out_shapes)

    def body(*refs):
        jins, jouts, sems = refs[:jin], refs[jin:jin + jout], refs[jin + jout:]
        for what in ("start", "finish"):
            pi, po = 0, 0
            for q, j in enumerate(jobs):
                getattr(j, what)(jins[pi:pi + len(j.ins)], jouts[po:po + len(j.out_shapes)], sems[2 * q], sems[2 * q + 1])
                pi, po = pi + len(j.ins), po + len(j.out_shapes)

    res = pl.pallas_call(
        body, name=name, in_specs=[ANY] * jin, out_specs=[ANY] * jout,
        out_shape=[s for j in jobs for s in j.out_shapes], input_output_aliases=aliases,
        scratch_shapes=[pltpu.SemaphoreType.DMA((j.n_sems,)) for j in jobs for _ in range(2)],
    )(*[a for j in jobs for a in j.ins])
    rest, per_job = list(res), []
    for j in jobs:
        per_job.append(rest[:len(j.out_shapes)])
        rest = rest[len(j.out_shapes):]
    return per_job


def _small_allreduce(name, v):
    m_per, n = v.shape

    def body(x_ref, sum_ref, all_ref, send_sems, recv_sems, local_sem):
        x, y, c, chips = _place()
        me, sibling = (x, y, c), (x, y, 1 - c)

        def rows(px, py, pc):
            return all_ref.at[pl.ds((4 * px + 2 * py + pc) * m_per, m_per), :]

        def copy(kk, block, to, src=None):
            return pltpu.make_async_remote_copy(
                src_ref=rows(*block) if src is None else src, dst_ref=rows(*block),
                send_sem=send_sems.at[kk], recv_sem=recv_sems.at[kk], device_id=to, device_id_type=MESH)

        mine = pltpu.make_async_copy(x_ref, rows(*me), local_sem)
        mine.start()
        first = [copy(0, me, sibling, src=x_ref)]
        first += [copy(1 + j, me, (*chip, c), src=x_ref) for j, chip in enumerate(chips)]
        for cp in first:
            cp.start()
        passed = [copy(4 + j, (*chip, c), sibling) for j, chip in enumerate(chips)]
        for j, chip in enumerate(chips):
            copy(1 + j, (*chip, c), me).wait_recv()
            passed[j].start()
        copy(0, sibling, me).wait_recv()
        for j, chip in enumerate(chips):
            copy(4 + j, (*chip, 1 - c), me).wait_recv()
        for cp in first + passed:
            cp.wait_send()
        mine.wait()
        acc = all_ref[pl.ds(0, m_per), :]
        for d in range(1, 8):
            acc = acc + all_ref[pl.ds(d * m_per, m_per), :]
        sum_ref[...] = acc

    vm = pl.BlockSpec(memory_space=pltpu.VMEM)
    return pl.pallas_call(
        body, name=name, in_specs=[vm], out_specs=[vm, vm],
        out_shape=[_sds((m_per, n), F32), _sds((8 * m_per, n), F32)],
        scratch_shapes=[pltpu.SemaphoreType.DMA((7,)), pltpu.SemaphoreType.DMA((7,)), pltpu.SemaphoreType.DMA],
    )(v)[0]


def _in_layout(D, W6, nhb, Ls):
    nmain = W6 + 2 * D
    lay = []
    for k in range(N_CHIPS):
        g0, g1 = k * Ls, (k + 1) * Ls
        pieces = []
        a, b = max(g0, 0), min(g1, W6)
        if a < b:
            pieces.append((a - g0, b - g0, a))
        a, b = max(g0, W6 + nhb), min(g1, W6 + nhb + 2 * D)
        if a < b:
            pieces.append((a - g0, b - g0, a - nhb))
        a, b = max(g0, W6), min(g1, W6 + nhb)
        fpiece = (a - g0, b - g0, a - W6) if a < b else None
        assert fpiece is None or (b - a) == nhb
        main0 = min(p[2] for p in pieces)
        main1 = max(p[2] + p[1] - p[0] for p in pieces)
        lay.append(dict(pieces=pieces, f=fpiece, s=main0 // LANE, e=-(-main1 // LANE), main1=main1))
    assert sum(1 for l in lay if l["f"] is not None) == 1
    nbw = max(l["e"] - l["s"] + (1 if l["f"] else 0) for l in lay)
    for k in range(1, N_CHIPS):
        assert lay[k]["s"] >= lay[k - 1]["e"] - 1 and lay[k]["s"] > lay[k - 1]["s"]
    return lay, nbw, nmain


def _to_window(w, lay_k, nbw):
    D = w.shape[0]
    items = [(c0 - lay_k["s"] * LANE, l0, l1) for (l0, l1, c0) in lay_k["pieces"]]
    if lay_k["f"]:
        l0, l1, off = lay_k["f"]
        items.append(((lay_k["e"] - lay_k["s"]) * LANE + off, l0, l1))
    items.sort()
    cols, pos = [], 0
    for w0, l0, l1 in items:
        if w0 > pos:
            cols.append(jnp.zeros((D, w0 - pos), w.dtype))
        cols.append(w[:, l0:l1])
        pos = w0 + (l1 - l0)
    if pos < nbw * LANE:
        cols.append(jnp.zeros((D, nbw * LANE - pos), w.dtype))
    return jnp.concatenate(cols, axis=1)


def _from_window(win, lay_k):
    items = [(l0, c0 - lay_k["s"] * LANE, l1 - l0) for (l0, l1, c0) in lay_k["pieces"]]
    if lay_k["f"]:
        l0, l1, off = lay_k["f"]
        items.append((l0, (lay_k["e"] - lay_k["s"]) * LANE + off, l1 - l0))
    items.sort()
    return jnp.concatenate([win[:, w0:w0 + n] for (_, w0, n) in items], axis=1)


def _assemble_in(name, wins, lay, nbw, nmain):
    _, D, _ = wins.shape
    ncb = nmain // LANE + 1
    k1 = np.zeros(ncb, np.int32)
    i1 = np.zeros(ncb, np.int32)
    k2 = np.zeros(ncb, np.int32)
    i2 = np.zeros(ncb, np.int32)
    fl = np.zeros(ncb, np.int32)
    for b in range(ncb - 1):
        k = max(kk for kk in range(N_CHIPS) if lay[kk]["s"] <= b)
        k1[b], i1[b] = k, b - lay[k]["s"]
        if k >= 1 and b == lay[k]["s"] and lay[k - 1]["main1"] > b * LANE:
            k2[b], i2[b], fl[b] = k - 1, b - lay[k - 1]["s"], 1
    kf = [kk for kk in range(N_CHIPS) if lay[kk]["f"]][0]
    k1[ncb - 1], i1[ncb - 1] = kf, lay[kf]["e"] - lay[kf]["s"]

    def body(k1_ref, i1_ref, k2_ref, i2_ref, fl_ref, a_ref, b_ref, o_ref):
        b = pl.program_id(0)
        add = jnp.where(fl_ref[b] == 1, b_ref[...], jnp.zeros_like(b_ref))
        o_ref[...] = a_ref[...] + add

    gs = pltpu.PrefetchScalarGridSpec(
        num_scalar_prefetch=5, grid=(ncb,),
        in_specs=[pl.BlockSpec((None, D, LANE), lambda b, k1r, i1r, k2r, i2r, flr: (k1r[b], 0, i1r[b])),
                  pl.BlockSpec((None, D, LANE), lambda b, k1r, i1r, k2r, i2r, flr: (k2r[b], 0, i2r[b]))],
        out_specs=pl.BlockSpec((D, LANE), lambda b, k1r, i1r, k2r, i2r, flr: (0, b)))
    return pl.pallas_call(body, name=name, grid_spec=gs, out_shape=_sds((D, ncb * LANE), BF16),
                          compiler_params=_cp(("parallel",)))(
        jnp.asarray(k1), jnp.asarray(i1), jnp.asarray(k2), jnp.asarray(i2), jnp.asarray(fl), wins, wins)


def _hgroup(nh):
    return _pick(nh, (4, 2, 1))


def _a_specs_q(nh, G):
    ngrp = nh // G
    blk = (GROUP, G * HEAD_DIM)
    q = pl.BlockSpec(blk, lambda i, hg: (i, hg))
    ks = [pl.BlockSpec(blk, functools.partial(
        lambda i, hg, j: (jnp.maximum(i - (WIN_BLOCKS - 1) + j, 0), ngrp + hg), j=j)) for j in range(WIN_BLOCKS)]
    vs = [pl.BlockSpec(blk, functools.partial(
        lambda i, hg, j: (jnp.maximum(i - (WIN_BLOCKS - 1) + j, 0), 2 * ngrp + hg), j=j)) for j in range(WIN_BLOCKS)]
    return q, ks, vs


def _a_logits(q, ks, bias, i, scale):
    parts = [lax.dot_general(q, k, NT, preferred_element_type=F32) for k in ks]
    s = jnp.concatenate(parts, axis=1) * scale + bias
    col = lax.broadcasted_iota(jnp.int32, s.shape, 1)
    return jnp.where(col >= (WIN_BLOCKS - 1 - i) * GROUP, s, NEG_INF)


def _attn_a_fwd(name, qkv, bias2, nh, jobs=()):
    T = qkv.shape[0]
    ng = T // GROUP
    G = _hgroup(nh)
    scale = HEAD_DIM ** -0.5

    def body(q_ref, *refs):
        k_refs = refs[:WIN_BLOCKS]
        v_refs = refs[WIN_BLOCKS:2 * WIN_BLOCKS]
        bias_ref, o_ref, lse_ref = refs[2 * WIN_BLOCKS:]
        i, hg = pl.program_id(0), pl.program_id(1)

        @pl.when(hg == 0)
        def _():
            lse_ref[...] = jnp.zeros_like(lse_ref)

        for g in range(G):
            h = hg * G + g
            sl = slice(g * HEAD_DIM, (g + 1) * HEAD_DIM)
            s = _a_logits(q_ref[:, sl], [kr[:, sl] for kr in k_refs], bias_ref[h], i, scale)
            m = jnp.max(s, axis=1, keepdims=True)
            p = jnp.exp(s - m)
            l = jnp.sum(p, axis=1, keepdims=True)
            pb = (p / l).astype(BF16)
            o = jnp.zeros((GROUP, HEAD_DIM), F32)
            for j in range(WIN_BLOCKS):
                o = o + jnp.dot(pb[:, j * GROUP:(j + 1) * GROUP], v_refs[j][:, sl], preferred_element_type=F32)
            o_ref[:, sl] = o.astype(BF16)
            _put_col(lse_ref, h, m + jnp.log(l))

    q_spec, k_specs, v_specs = _a_specs_q(nh, G)
    stat = pl.BlockSpec((GROUP, LANE), lambda i, hg: (i, 0))
    return _pcall(
        body, name=name, grid=(ng, nh // G),
        in_specs=[q_spec] + k_specs + v_specs + [pl.BlockSpec((nh, GROUP, WIN), lambda i, hg: (0, 0, 0))],
        out_specs=[pl.BlockSpec((GROUP, G * HEAD_DIM), lambda i, hg: (i, hg)), stat],
        out_shape=[_sds((T, nh * HEAD_DIM), BF16), _sds((T, LANE), F32)],
        sem=("parallel", "arbitrary"), jobs=jobs,
    )(qkv, *([qkv] * (2 * WIN_BLOCKS)), bias2)


def _attn_a_dq(name, qkv, do, lse, bias2, nh, jobs=()):
    T = qkv.shape[0]
    ng = T // GROUP
    G = _hgroup(nh)
    scale = HEAD_DIM ** -0.5

    def body(q_ref, *refs):
        k_refs = refs[:WIN_BLOCKS]
        v_refs = refs[WIN_BLOCKS:2 * WIN_BLOCKS]
        do_ref, lse_ref, bias_ref, dq_ref, delta_ref, db_ref = refs[2 * WIN_BLOCKS:]
        i, hg = pl.program_id(0), pl.program_id(1)

        @pl.when(hg == 0)
        def _():
            delta_ref[...] = jnp.zeros_like(delta_ref)

        @pl.when(i == 0)
        def _():
            for g in range(G):
                db_ref[hg * G + g] = jnp.zeros((GROUP, WIN), F32)

        for g in range(G):
            h = hg * G + g
            sl = slice(g * HEAD_DIM, (g + 1) * HEAD_DIM)
            ks = [kr[:, sl] for kr in k_refs]
            s = _a_logits(q_ref[:, sl], ks, bias_ref[h], i, scale)
            p = jnp.exp(s - _col_of(lse_ref[...], h))
            dov = do_ref[:, sl]
            dp = jnp.concatenate([lax.dot_general(dov, vr[:, sl], NT, preferred_element_type=F32) for vr in v_refs],
                                 axis=1)
            delta = jnp.sum(p * dp, axis=1, keepdims=True)
            ds = p * (dp - delta)
            db_ref[h] += ds
            dsb = ds.astype(BF16)
            dq = jnp.zeros((GROUP, HEAD_DIM), F32)
            for j in range(WIN_BLOCKS):
                dq = dq + jnp.dot(dsb[:, j * GROUP:(j + 1) * GROUP], ks[j], preferred_element_type=F32)
            dq_ref[:, sl] = (dq * scale).astype(BF16)
            _put_col(delta_ref, h, delta)

    q_spec, k_specs, v_specs = _a_specs_q(nh, G)
    blk = pl.BlockSpec((GROUP, G * HEAD_DIM), lambda i, hg: (i, hg))
    stat = pl.BlockSpec((GROUP, LANE), lambda i, hg: (i, 0))
    full_b = pl.BlockSpec((nh, GROUP, WIN), lambda i, hg: (0, 0, 0))
    return _pcall(
        body, name=name, grid=(ng, nh // G),
        in_specs=[q_spec] + k_specs + v_specs + [blk, stat, full_b],
        out_specs=[blk, stat, full_b],
        out_shape=[_sds((T, nh * HEAD_DIM), BF16), _sds((T, LANE), F32), _sds((nh, GROUP, WIN), F32)],
        sem=("arbitrary", "arbitrary"), jobs=jobs,
    )(qkv, *([qkv] * (2 * WIN_BLOCKS)), do, lse, bias2)


def _attn_a_dkv(name, qkv, do, lse, delta, bias2, nh, jobs=()):
    T = qkv.shape[0]
    ng = T // GROUP
    G = _hgroup(nh)
    ngrp = nh // G
    scale = HEAD_DIM ** -0.5
    nj = WIN_BLOCKS

    def body(k_ref, v_ref, *refs):
        q_refs = refs[:nj]
        do_refs = refs[nj:2 * nj]
        lse_refs = refs[2 * nj:3 * nj]
        dl_refs = refs[3 * nj:4 * nj]
        bias_ref, dk_ref, dv_ref = refs[4 * nj:]
        r, hg = pl.program_id(0), pl.program_id(1)
        for g in range(G):
            h = hg * G + g
            sl = slice(g * HEAD_DIM, (g + 1) * HEAD_DIM)
            kv, vv = k_ref[:, sl], v_ref[:, sl]
            bias = bias_ref[h]
            dk = jnp.zeros((GROUP, HEAD_DIM), F32)
            dv = jnp.zeros((GROUP, HEAD_DIM), F32)
            for j in range(nj):
                qv, dov = q_refs[j][:, sl], do_refs[j][:, sl]
                c0 = (nj - 1 - j) * GROUP
                s = lax.dot_general(qv, kv, NT, preferred_element_type=F32) * scale + bias[:, c0:c0 + GROUP]
                p = jnp.exp(s - _col_of(lse_refs[j][...], h))
                p = jnp.where(r + j <= ng - 1, p, 0.0)
                dp = lax.dot_general(dov, vv, NT, preferred_element_type=F32)
                ds = p * (dp - _col_of(dl_refs[j][...], h))
                dv = dv + lax.dot_general(p.astype(BF16), dov, TN, preferred_element_type=F32)
                dk = dk + lax.dot_general(ds.astype(BF16), qv, TN, preferred_element_type=F32)
            dk_ref[:, sl] = (dk * scale).astype(BF16)
            dv_ref[:, sl] = dv.astype(BF16)

    def qmap(j):
        return functools.partial(lambda r, hg, j: (jnp.minimum(r + j, ng - 1), hg), j=j)

    def smap(j):
        return functools.partial(lambda r, hg, j: (jnp.minimum(r + j, ng - 1), 0), j=j)

    blk = (GROUP, G * HEAD_DIM)
    in_specs = ([pl.BlockSpec(blk, lambda r, hg: (r, ngrp + hg)), pl.BlockSpec(blk, lambda r, hg: (r, 2 * ngrp + hg))]
                + [pl.BlockSpec(blk, qmap(j)) for j in range(nj)]
                + [pl.BlockSpec(blk, qmap(j)) for j in range(nj)]
                + [pl.BlockSpec((GROUP, LANE), smap(j)) for j in range(nj)]
                + [pl.BlockSpec((GROUP, LANE), smap(j)) for j in range(nj)]
                + [pl.BlockSpec((nh, GROUP, WIN), lambda r, hg: (0, 0, 0))])
    out = pl.BlockSpec(blk, lambda r, hg: (r, hg))
    return _pcall(
        body, name=name, grid=(ng, ngrp), in_specs=in_specs, out_specs=[out, out],
        out_shape=[_sds((T, nh * HEAD_DIM), BF16)] * 2,
        sem=("parallel", "parallel"), jobs=jobs,
    )(qkv, qkv, *([qkv] * nj), *([do] * nj), *([lse] * nj), *([delta] * nj), bias2)


def _fox_prep(name, f, b_f):
    T = f.shape[0]
    tb = _pick(T, (256, 128))

    def body(f_ref, b_ref, cum_ref, cumt_ref, carry_ref):
        @pl.when(pl.program_id(0) == 0)
        def _():
            carry_ref[...] = jnp.zeros_like(carry_ref)

        z = f_ref[...] + b_ref[...]
        logf = jnp.minimum(z, 0.0) - jnp.log(1.0 + jnp.exp(-jnp.abs(z)))
        row = lax.broadcasted_iota(jnp.int32, (tb, tb), 0)
        col = lax.broadcasted_iota(jnp.int32, (tb, tb), 1)
        tri = (row >= col).astype(BF16)
        acc = jnp.zeros((tb, LANE), F32)
        for piece in _split3(logf):
            acc = acc + jnp.dot(tri, piece, preferred_element_type=F32)
        cum = acc + carry_ref[...]
        cum_ref[...] = cum
        cumt_ref[...] = cum.T
        carry_ref[...] = cum_ref[pl.ds(tb - 1, 1), :]

    return pl.pallas_call(
        body, name=name, grid=(T // tb,),
        in_specs=[pl.BlockSpec((tb, LANE), lambda i: (i, 0)), pl.BlockSpec((1, LANE), lambda i: (0, 0))],
        out_specs=[pl.BlockSpec((tb, LANE), lambda i: (i, 0)), pl.BlockSpec((LANE, tb), lambda i: (0, i))],
        out_shape=[_sds((T, LANE), F32), _sds((LANE, T), F32)],
        scratch_shapes=[pltpu.VMEM((1, LANE), F32)],
        compiler_params=_cp(("arbitrary",)),
    )(f, b_f)


def _fox_blk(T):
    return _pick(T, (256, 128))


def _fox_group(nh):
    return _hgroup(nh)


def _fox_allowed(i, j, tq, tk):
    diff = lax.broadcasted_iota(jnp.int32, (tq, tk), 1) - lax.broadcasted_iota(jnp.int32, (tq, tk), 0)
    return diff <= (i - j) * tq


def _fox_fwd(name, qkv, cum, cumt, nh, jobs=()):
    T = qkv.shape[0]
    tq = tk = _fox_blk(T)
    G = _fox_group(nh)
    ngrp = nh // G
    scale = HEAD_DIM ** -0.5

    def body(q_ref, k_ref, v_ref, cum_ref, cumt_ref, o_ref, lse_ref):
        i, hg = pl.program_id(0), pl.program_id(1)

        @pl.when(hg == 0)
        def _():
            lse_ref[...] = jnp.zeros_like(lse_ref)

        sls = [slice(g * HEAD_DIM, (g + 1) * HEAD_DIM) for g in range(G)]
        qs = [q_ref[:, sl] for sl in sls]
        cqs = [_col_of(cum_ref[...], hg * G + g) for g in range(G)]

        def step(j, carry, diagonal=False):
            k0 = pl.multiple_of(j * tk, tk)
            out = []
            for g in range(G):
                m, l, acc = carry[g]
                kj = k_ref[pl.ds(k0, tk), sls[g]]
                vj = v_ref[pl.ds(k0, tk), sls[g]]
                ck = cumt_ref[pl.ds(hg * G + g, 1), pl.ds(k0, tk)]
                s = lax.dot_general(qs[g], kj, NT, preferred_element_type=F32) * scale + (cqs[g] - ck)
                if diagonal:
                    s = jnp.where(_fox_allowed(i, j, tq, tk), s, NEG_INF)
                m_new = jnp.maximum(m, jnp.max(s, axis=1, keepdims=True))
                alpha = jnp.exp(m - m_new)
                p = jnp.exp(s - m_new)
                l = alpha * l + jnp.sum(p, axis=1, keepdims=True)
                acc = alpha * acc + jnp.dot(p.astype(BF16), vj, preferred_element_type=F32)
                out.append((m_new, l, acc))
            return tuple(out)

        one = (jnp.full((tq, 1), NEG_INF, F32), jnp.zeros((tq, 1), F32), jnp.zeros((tq, HEAD_DIM), F32))
        res = step(i, lax.fori_loop(0, i, step, tuple(one for _ in range(G))), diagonal=True)
        for g in range(G):
            m, l, acc = res[g]
            o_ref[:, sls[g]] = (acc / l).astype(BF16)
            _put_col(lse_ref, hg * G + g, m + jnp.log(l))

    GW = G * HEAD_DIM
    return _pcall(
        body, name=name, grid=(T // tq, ngrp),
        in_specs=[pl.BlockSpec((tq, GW), lambda i, hg: (i, 3 * ngrp + hg)),
                  pl.BlockSpec((T, GW), lambda i, hg: (0, 4 * ngrp + hg)),
                  pl.BlockSpec((T, GW), lambda i, hg: (0, 5 * ngrp + hg)),
                  pl.BlockSpec((tq, LANE), lambda i, hg: (i, 0)),
                  pl.BlockSpec((LANE, T), lambda i, hg: (0, 0))],
        out_specs=[pl.BlockSpec((tq, GW), lambda i, hg: (i, hg)), pl.BlockSpec((tq, LANE), lambda i, hg: (i, 0))],
        out_shape=[_sds((T, nh * HEAD_DIM), BF16), _sds((T, LANE), F32)],
        sem=("parallel", "arbitrary"), jobs=jobs,
    )(qkv, qkv, qkv, cum, cumt)


def _fox_dq(name, qkv, do, lse, cum, cumt, nh, jobs=()):
    T = qkv.shape[0]
    tq = tk = _fox_blk(T)
    G = _fox_group(nh)
    ngrp = nh // G
    GW = G * HEAD_DIM
    scale = HEAD_DIM ** -0.5

    def body(q_ref, k_ref, v_ref, do_ref, lse_ref, cum_ref, cumt_ref, dq_ref, delta_ref):
        i, hg = pl.program_id(0), pl.program_id(1)

        @pl.when(hg == 0)
        def _():
            delta_ref[...] = jnp.zeros_like(delta_ref)

        sls = [slice(g * HEAD_DIM, (g + 1) * HEAD_DIM) for g in range(G)]
        qs = [q_ref[:, sl] for sl in sls]
        dos = [do_ref[:, sl] for sl in sls]
        cqs = [_col_of(cum_ref[...], hg * G + g) for g in range(G)]
        lses = [_col_of(lse_ref[...], hg * G + g) for g in range(G)]

        def p_dp(j, g, ok):
            k0 = pl.multiple_of(j * tk, tk)
            kj = k_ref[pl.ds(k0, tk), sls[g]]
            vj = v_ref[pl.ds(k0, tk), sls[g]]
            ck = cumt_ref[pl.ds(hg * G + g, 1), pl.ds(k0, tk)]
            s = lax.dot_general(qs[g], kj, NT, preferred_element_type=F32) * scale + (cqs[g] - ck)
            if ok is not None:
                s = jnp.where(ok, s, NEG_INF)
            p = jnp.exp(s - lses[g])
            return p, lax.dot_general(dos[g], vj, NT, preferred_element_type=F32), kj

        def sweep_delta(j, deltas, diagonal=False):
            ok = _fox_allowed(i, j, tq, tk) if diagonal else None
            out = []
            for g in range(G):
                p, dp, _ = p_dp(j, g, ok)
                out.append(deltas[g] + jnp.sum(p * dp, axis=1, keepdims=True))
            return tuple(out)

        deltas = lax.fori_loop(0, i, sweep_delta, tuple(jnp.zeros((tq, 1), F32) for _ in range(G)))
        deltas = sweep_delta(i, deltas, diagonal=True)

        def sweep_dq(j, dqs, diagonal=False):
            ok = _fox_allowed(i, j, tq, tk) if diagonal else None
            out = []
            for g in range(G):
                p, dp, kj = p_dp(j, g, ok)
                ds = p * (dp - deltas[g])
                out.append(dqs[g] + jnp.dot(ds.astype(BF16), kj, preferred_element_type=F32))
            return tuple(out)

        dqs = lax.fori_loop(0, i, sweep_dq, tuple(jnp.zeros((tq, HEAD_DIM), F32) for _ in range(G)))
        dqs = sweep_dq(i, dqs, diagonal=True)
        for g in range(G):
            dq_ref[:, sls[g]] = (dqs[g] * scale).astype(BF16)
            _put_col(delta_ref, hg * G + g, deltas[g])

    blk = pl.BlockSpec((tq, GW), lambda i, hg: (i, hg))
    stat = pl.BlockSpec((tq, LANE), lambda i, hg: (i, 0))
    return _pcall(
        body, name=name, grid=(T // tq, ngrp),
        in_specs=[pl.BlockSpec((tq, GW), lambda i, hg: (i, 3 * ngrp + hg)),
                  pl.BlockSpec((T, GW), lambda i, hg: (0, 4 * ngrp + hg)),
                  pl.BlockSpec((T, GW), lambda i, hg: (0, 5 * ngrp + hg)),
                  blk, stat, stat, pl.BlockSpec((LANE, T), lambda i, hg: (0, 0))],
        out_specs=[blk, stat],
        out_shape=[_sds((T, nh * HEAD_DIM), BF16), _sds((T, LANE), F32)],
        sem=("parallel", "arbitrary"), jobs=jobs,
    )(qkv, qkv, qkv, do, lse, cum, cumt)


def _fox_dkv(name, qkv, do, lse, delta, cum, cumt, nh, jobs=()):
    T = qkv.shape[0]
    tq = tk = _fox_blk(T)
    nq = T // tq
    G = _fox_group(nh)
    ngrp = nh // G
    GW = G * HEAD_DIM
    scale = HEAD_DIM ** -0.5

    def body(k_ref, v_ref, q_ref, do_ref, lse_ref, dl_ref, cum_ref, cumt_ref, dk_ref, dv_ref, dc_ref):
        j, hg = pl.program_id(0), pl.program_id(1)

        @pl.when(hg == 0)
        def _():
            dc_ref[...] = jnp.zeros_like(dc_ref)

        sls = [slice(g * HEAD_DIM, (g + 1) * HEAD_DIM) for g in range(G)]
        kjs = [k_ref[:, sl] for sl in sls]
        vjs = [v_ref[:, sl] for sl in sls]
        k0 = pl.multiple_of(j * tk, tk)
        cks = [cumt_ref[pl.ds(hg * G + g, 1), pl.ds(k0, tk)] for g in range(G)]

        def step(i, carry, diagonal=False):
            q0 = pl.multiple_of(i * tq, tq)
            cum_i, lse_i, dl_i = cum_ref[pl.ds(q0, tq), :], lse_ref[pl.ds(q0, tq), :], dl_ref[pl.ds(q0, tq), :]
            out = []
            for g in range(G):
                dk, dv, dc = carry[g]
                h = hg * G + g
                qi = q_ref[pl.ds(q0, tq), sls[g]]
                doi = do_ref[pl.ds(q0, tq), sls[g]]
                s = lax.dot_general(qi, kjs[g], NT, preferred_element_type=F32) * scale + (_col_of(cum_i, h) - cks[g])
                if diagonal:
                    s = jnp.where(_fox_allowed(i, j, tq, tk), s, NEG_INF)
                p = jnp.exp(s - _col_of(lse_i, h))
                dp = lax.dot_general(doi, vjs[g], NT, preferred_element_type=F32)
                ds = p * (dp - _col_of(dl_i, h))
                dv = dv + lax.dot_general(p.astype(BF16), doi, TN, preferred_element_type=F32)
                dk = dk + lax.dot_general(ds.astype(BF16), qi, TN, preferred_element_type=F32)
                dc = dc - jnp.sum(ds, axis=0, keepdims=True)
                out.append((dk, dv, dc))
            return tuple(out)

        one = (jnp.zeros((tk, HEAD_DIM), F32), jnp.zeros((tk, HEAD_DIM), F32), jnp.zeros((1, tk), F32))
        res = lax.fori_loop(j + 1, nq, step, step(j, tuple(one for _ in range(G)), diagonal=True))
        sub = lax.broadcasted_iota(jnp.int32, (LANE, tk), 0)
        dc_all = dc_ref[...]
        for g in range(G):
            dk, dv, dc = res[g]
            dk_ref[:, sls[g]] = (dk * scale).astype(BF16)
            dv_ref[:, sls[g]] = dv.astype(BF16)
            dc_all = jnp.where(sub == hg * G + g, dc, dc_all)
        dc_ref[...] = dc_all

    whole = lambda c: pl.BlockSpec((T, GW), c)
    stat = pl.BlockSpec((T, LANE), lambda j, hg: (0, 0))
    out = pl.BlockSpec((tk, GW), lambda j, hg: (j, hg))
    return _pcall(
        body, name=name, grid=(T // tk, ngrp),
        in_specs=[pl.BlockSpec((tk, GW), lambda j, hg: (j, 4 * ngrp + hg)),
                  pl.BlockSpec((tk, GW), lambda j, hg: (j, 5 * ngrp + hg)),
                  whole(lambda j, hg: (0, 3 * ngrp + hg)), whole(lambda j, hg: (0, hg)),
                  stat, stat, stat, pl.BlockSpec((LANE, T), lambda j, hg: (0, 0))],
        out_specs=[out, out, pl.BlockSpec((LANE, tk), lambda j, hg: (0, j))],
        out_shape=[_sds((T, nh * HEAD_DIM), BF16)] * 2 + [_sds((LANE, T), F32)],
        sem=("parallel", "arbitrary"), jobs=jobs,
    )(qkv, qkv, qkv, do, lse, delta, cum, cumt)


def _fox_post(name, dcumt, f, b_f):
    T = f.shape[0]
    tb = _pick(T, (256, 128))
    nb = T // tb

    def body(dc_ref, f_ref, b_ref, df_ref, gb_ref, carry_ref):
        @pl.when(pl.program_id(0) == 0)
        def _():
            carry_ref[...] = jnp.zeros_like(carry_ref)
            gb_ref[...] = jnp.zeros_like(gb_ref)

        dc = dc_ref[...]
        row = lax.broadcasted_iota(jnp.int32, (tb, tb), 0)
        col = lax.broadcasted_iota(jnp.int32, (tb, tb), 1)
        tri = (row >= col).astype(BF16)
        acc = jnp.zeros((LANE, tb), F32)
        for piece in _split3(dc):
            acc = acc + jnp.dot(piece, tri, preferred_element_type=F32)
        dlogf = (acc + carry_ref[...]).T
        carry_ref[...] += jnp.sum(dc, axis=1, keepdims=True)
        z = f_ref[...] + b_ref[...]
        df = dlogf * _sigmoid(-z)
        df_ref[...] = df.astype(BF16)
        gb_ref[...] += jnp.sum(df, axis=0, keepdims=True)

    return pl.pallas_call(
        body, name=name, grid=(nb,),
        in_specs=[pl.BlockSpec((LANE, tb), lambda g: (0, nb - 1 - g)),
                  pl.BlockSpec((tb, LANE), lambda g: (nb - 1 - g, 0)),
                  pl.BlockSpec((1, LANE), lambda g: (0, 0))],
        out_specs=[pl.BlockSpec((tb, LANE), lambda g: (nb - 1 - g, 0)), pl.BlockSpec((1, LANE), lambda g: (0, 0))],
        out_shape=[_sds((T, LANE), BF16), _sds((1, LANE), F32)],
        scratch_shapes=[pltpu.VMEM((LANE, 1), F32)],
        compiler_params=_cp(("arbitrary",)),
    )(dcumt, f, b_f)


def _rel_tables(n_rel):
    max_rel = (n_rel - 1) // 2
    nj = GROUP + WIN - 1
    onehot = np.zeros((n_rel, nj), np.float32)
    for j in range(nj):
        dist = (WIN - 1) - j
        onehot[int(np.clip(dist, -max_rel, max_rel)) + max_rel, j] = 1.0
    a = np.arange(GROUP)[:, None]
    kb = np.arange(WIN)[None, :]
    lo = CHUNK * (a // CHUNK)
    inband = (kb >= lo) & (kb < lo + BAND)
    return onehot, inband


def _bias2_of(rel_bias, onehot, inband):
    bv = jnp.dot(rel_bias, jnp.asarray(onehot), precision=lax.Precision.HIGHEST)
    rows = [bv[:, GROUP - 1 - a:GROUP - 1 - a + WIN] for a in range(GROUP)]
    toe = jnp.stack(rows, axis=1)
    return jnp.where(jnp.asarray(inband)[None], toe, NEG_INF)


def _rel_grad_of(dbias2, onehot):
    nj = GROUP + WIN - 1
    dbv = sum(jnp.pad(dbias2[:, a, :], ((0, 0), (GROUP - 1 - a, nj - WIN - (GROUP - 1 - a)))) for a in range(GROUP))
    return jnp.dot(dbv, jnp.asarray(onehot).T, precision=lax.Precision.HIGHEST)


def kernel(x, g_mix, w_in, b_f, b_gate, rel_bias, w_branch_a, w_branch_b, w_out, g_ffn, w_gate_ffn, w_up_ffn, w_down_ffn, g_final, loss_target, m_g_mix, m_w_in, m_b_f, m_b_gate, m_rel_bias, m_w_branch_a, m_w_branch_b, m_w_out, m_g_ffn, m_w_gate_ffn, m_w_up_ffn, m_w_down_ffn, m_g_final, v_g_mix, v_w_in, v_b_f, v_b_gate, v_rel_bias, v_w_branch_a, v_w_branch_b, v_w_out, v_g_ffn, v_w_gate_ffn, v_w_up_ffn, v_w_down_ffn, v_g_final):
    T, D = x.shape[1], x.shape[2]
    Ls = w_in.shape[2]
    W = w_branch_a.shape[1]
    nh = W // HEAD_DIM
    nhb = b_f.shape[1]
    assert w_branch_b.shape[1] == W and nhb == nh and rel_bias.shape[1] == nh
    W6 = 6 * W
    Fl = w_gate_ffn.shape[2]
    Fp = -(-Fl // LANE) * LANE
    n_rel = rel_bias.shape[2]
    chip = 2 * lax.axis_index("x") + lax.axis_index("y")
    lay, nbw, nmain = _in_layout(D, W6, nhb, Ls)
    onehot, inband = _rel_tables(n_rel)

    xs, tgt = x[0], loss_target[0]

    win_f32 = lax.switch(chip, [functools.partial(_to_window, lay_k=lay[k], nbw=nbw) for k in range(N_CHIPS)], w_in[0])
    pad_c = lambda w: jnp.pad(w, ((0, 0), (0, Fp - Fl)))
    pad_r = lambda w: jnp.pad(w, ((0, Fp - Fl), (0, 0)))
    sh_in = _cast_bf16("cast_w_in", win_f32, chip)
    sh_a = _cast_bf16("cast_w_a", w_branch_a[0], chip)
    sh_b = _cast_bf16("cast_w_b", w_branch_b[0], chip)
    sh_o = _cast_bf16("cast_w_out", w_out[0], chip)
    sh_g = _cast_bf16("cast_w_gate", pad_c(w_gate_ffn[0]), chip)
    sh_u = _cast_bf16("cast_w_up", pad_c(w_up_ffn[0]), chip)
    sh_d = _cast_bf16("cast_w_down", pad_r(w_down_ffn[0]), chip)
    (wins,) = _allgather("ag_w_in", [sh_in])
    wc = _assemble_in("assemble_w_in", wins, lay, nbw, nmain)

    h1, r1 = _rms_fwd("rms1", xs, g_mix)
    qkv, ((wa_g, wb_g, wo_g),) = _mm_nn("proj_qkv", h1, wc, BF16, b_col0=0, n=W6, tm=1024,
                                        jobs=[_job_gather_ici([sh_a, sh_b, sh_o])])
    gates, ((wa_g, wb_g, wo_g), (wg_g,)) = _mm_nn(
        "proj_gates", h1, wc, BF16, b_col0=W6, n=2 * D, tm=1024,
        jobs=[_job_gather_d2d([wa_g, wb_g, wo_g]), _job_gather_ici([sh_g], part=(0, 2))])
    fl = _mm_nn("proj_f", h1, wc, F32, b_col0=nmain, n=LANE, tn=LANE)
    bias2 = _bias2_of(rel_bias[0], onehot, inband)
    bf_pad = jnp.pad(b_f, ((0, 0), (0, LANE - nhb)))
    (o_a, lse_a), ((wg_g,),) = _attn_a_fwd("attn_a_fwd", qkv, bias2, nh, jobs=[_job_gather_ici([wg_g], part=(1, 2))])
    cum, cumt = _fox_prep("fox_prep", fl, bf_pad)
    (o_b, lse_b), ((wu_g,), (wg_g,)) = _fox_fwd("fox_fwd", qkv, cum, cumt, nh,
                                                jobs=[_job_gather_ici([sh_u]), _job_gather_d2d([wg_g])])
    u_a = _mm_nn("branch_a", o_a, wa_g, BF16, tm=1024)
    u_b = _mm_nn("branch_b", o_b, wb_g, BF16, tm=1024)
    merged = _merge_fwd("merge", gates, u_a, u_b, b_gate)
    wo_full = wo_g.reshape(D, D)
    x1, ((wu_g,),) = _mm_nn("out_proj", merged, wo_full, F32, residual=xs, tm=1024, tn=_pick(D, (1024, 512, 256, 128)),
                            jobs=[_job_gather_d2d([wu_g])])
    h2, r2 = _rms_fwd("rms2", x1, g_ffn)

    tm_f = _pick(T, (1024, 512, 256, 128))
    tn_f = _pick(Fp, (1408, 1024, 512, 256, 128))
    tk_f = _pick(D, (1024, 512, 256, 128))
    nps_f = Fp // tn_f

    def swiglu_ep(accs, e_refs, o_refs):
        g, u = accs
        o_refs[0][...] = g.astype(BF16)
        o_refs[1][...] = u.astype(BF16)
        o_refs[2][...] = (g * _sigmoid(g) * u).astype(BF16)

    hid_spec = pl.BlockSpec((tm_f, tn_f), lambda i, j, k: (i, j))
    wcol_spec = pl.BlockSpec((None, tk_f, tn_f), lambda i, j, k: (j // nps_f, k, j % nps_f))
    (gate, up, hidden), ((wd_g,),) = _mm(
        "ffn_up", "nn", [h2], [pl.BlockSpec((tm_f, tk_f), lambda i, j, k: (i, k))], [wg_g, wu_g], [wcol_spec, wcol_spec],
        [(0, 0, 0), (0, 1, 1)], 2, (T // tm_f, N_CHIPS * Fp // tn_f, D // tk_f), tm_f, tn_f,
        [_sds((T, N_CHIPS * Fp), BF16)] * 3, [hid_spec] * 3, swiglu_ep, jobs=[_job_gather_ici([sh_d])])
    ((wd_g,),) = _comm_only("ag_w_down_d2d", [_job_gather_d2d([wd_g])])
    wd_full = wd_g.reshape(N_CHIPS * Fp, D)
    x2 = _mm_nn("ffn_down", hidden, wd_full, F32, residual=x1, tm=1024, tn=_pick(D, (1024, 512, 256, 128)),
                tk=_pick(N_CHIPS * Fp, (1408, 1024, 512, 256, 128)))

    dx2, dx2b, loss_part, gg_final = _final_loss_bwd("final_loss", x2, tgt, g_final.reshape(1, D))

    def swiglu_bwd_ep(accs, e_refs, o_refs):
        dh = accs[0]
        g = e_refs[0][...].astype(F32)
        u = e_refs[1][...].astype(F32)
        sg = _sigmoid(g)
        o_refs[0][...] = (dh * u * (sg * (1.0 + g * (1.0 - sg)))).astype(BF16)
        o_refs[1][...] = (dh * (g * sg)).astype(BF16)

    tk_b = _pick(D, (1024, 512, 256, 128))
    core = lax.axis_index("c")
    (dgate, dup), _ = _mm(
        "ffn_down_bwd", "nt", [dx2b], [pl.BlockSpec((tm_f, tk_b), lambda i, j, k: (i, k))],
        [wd_full], [pl.BlockSpec((tn_f, tk_b), lambda i, j, k: (j, k))], [(0, 0, 0)], 1,
        (T // tm_f, N_CHIPS * Fp // tn_f, D // tk_b), tm_f, tn_f,
        [_sds((T, N_CHIPS * Fp), BF16)] * 2, [hid_spec] * 2, swiglu_bwd_ep,
        extra=[gate, up], extra_specs=[hid_spec, hid_spec])
    dwd = _mm_tn("dw_down", hidden, dx2b, BF16, tm=_pick(N_CHIPS * Fp, (1408, 1024, 512, 256, 128)))
    dwd = dwd.reshape(N_CHIPS, Fp, D)
    dh2, ((sib_d,),) = _mm_nt("ffn_up_bwd", [dgate, dup], [wg_g, wu_g], F32, tm=1024, jobs=[_job_sibling([dwd])])
    tm_w = _pick(D, (2048, 1024, 512, 256, 128))
    dwg = _mm_tn("dw_gate", h2, dgate, BF16, slots=N_CHIPS, tm=tm_w)
    dwu = _mm_tn("dw_up", h2, dup, BF16, slots=N_CHIPS, tm=tm_w)
    dx1, dx1b, gg_ffn = _rms_bwd("rms2_bwd", [dh2], x1, r2, g_ffn, dx2, True)
    part_d = _add_bf16("rs_add_down", dwd, core, sib_d)

    dmerged, ((sib_g, sib_u),) = _mm_nt("out_proj_bwd", [dx1b], [wo_full], BF16, tm=1024,
                                        jobs=[_job_sibling([dwg, dwu])])
    dwo = _mm_tn("dw_out", merged, dx1b, BF16).reshape(N_CHIPS, D // N_CHIPS, D)
    du_a, du_b, dga, dgb, gbg_a, gbg_b = _merge_bwd("merge_bwd", dmerged, gates, u_a, u_b, b_gate)
    part_g = _add_bf16("rs_add_gate", dwg, core, sib_g)
    part_u = _add_bf16("rs_add_up", dwu, core, sib_u)
    do_a = _mm_nt("branch_a_bwd", [du_a], [wa_g], BF16, tm=1024)
    do_b = _mm_nt("branch_b_bwd", [du_b], [wb_g], BF16, tm=1024)
    dwa = _mm_tn("dw_a", o_a, du_a, BF16, slots=N_CHIPS)
    dwb = _mm_tn("dw_b", o_b, du_b, BF16, slots=N_CHIPS)

    (dq_a, delta_a, dbias2), ((got_d,), (sib_a, sib_b, sib_o)) = _attn_a_dq(
        "attn_a_dq", qkv, do_a, lse_a, bias2, nh,
        jobs=[_job_scatter([part_d], part=(0, 2)), _job_sibling([dwa, dwb, dwo])])
    part_a = _add_bf16("rs_add_a", dwa, core, sib_a)
    part_b = _add_bf16("rs_add_b", dwb, core, sib_b)
    part_o = _add_bf16("rs_add_out", dwo, core, sib_o)
    (dk_a, dv_a), ((got_d,), (got_g,)) = _attn_a_dkv(
        "attn_a_dkv", qkv, do_a, lse_a, delta_a, bias2, nh,
        jobs=[_job_scatter([part_d], part=(1, 2), into=[got_d]), _job_scatter([part_g], part=(0, 2))])
    full_d = _sum4("rs_sum_down", got_d, part_d, chip, core)
    (dq_b, delta_b), ((got_g,), (got_u,)) = _fox_dq(
        "fox_dq", qkv, do_b, lse_b, cum, cumt, nh,
        jobs=[_job_scatter([part_g], part=(1, 2), into=[got_g]), _job_scatter([part_u])])
    full_g = _sum4("rs_sum_gate", got_g, part_g, chip, core)
    full_u = _sum4("rs_sum_up", got_u, part_u, chip, core)
    (dk_b, dv_b, dcumt), ((got_a, got_b, got_o),) = _fox_dkv(
        "fox_dkv", qkv, do_b, lse_b, delta_b, cum, cumt, nh, jobs=[_job_scatter([part_a, part_b, part_o])])
    full_a = _sum4("rs_sum_a", got_a, part_a, chip, core)
    full_b = _sum4("rs_sum_b", got_b, part_b, chip, core)
    full_o = _sum4("rs_sum_out", got_o, part_o, chip, core)
    df, gbf = _fox_post("fox_post", dcumt, fl, bf_pad)

    dqkv = jnp.concatenate([dq_a, dk_a, dv_a, dq_b, dk_b, dv_b], axis=1)
    dgates = jnp.concatenate([dga, dgb], axis=1)
    dwc_q, ((g_d, g_g, g_u, g_a, g_b, g_o),) = _mm_tn(
        "dw_in_qkv", h1, dqkv, BF16, tn=_pick(W6, (2048, 1024, 512, 256, 128)),
        jobs=[_job_swap([full_d, full_g, full_u, full_a, full_b, full_o])])
    dwc_g = _mm_tn("dw_in_gates", h1, dgates, BF16, tn=_pick(2 * D, (2048, 1024, 512, 256, 128)))
    dwc_f = _mm_tn("dw_in_f", h1, df, BF16, tn=LANE)
    dwc = jnp.concatenate([dwc_q, dwc_g, dwc_f], axis=1)
    zeros_blk = jnp.zeros((D, LANE), BF16)
    win_parts = []
    for k in range(N_CHIPS):
        cols = [dwc[:, lay[k]["s"] * LANE:lay[k]["e"] * LANE]]
        nb = lay[k]["e"] - lay[k]["s"]
        if lay[k]["f"]:
            cols.append(dwc[:, nmain:nmain + LANE])
            nb += 1
        cols += [zeros_blk] * (nbw - nb)
        win_parts.append(jnp.concatenate(cols, axis=1) if len(cols) > 1 else cols[0])
    dwin = jnp.stack(win_parts, axis=0)
    big = {}

    def adamw(nm, w, g, m, v, jobs=()):
        (d, mn, vn, go), jouts = _adamw(f"adamw_{nm}", w[0], g, m[0], v[0], jobs=jobs)
        big[nm] = (go[None], d[None], mn[None], vn[None])
        return jouts

    ((sib_in,),) = adamw("w_gate_ffn", w_gate_ffn, g_g, m_w_gate_ffn, v_w_gate_ffn, jobs=[_job_sibling([dwin])])
    part_in = _add_bf16("rs_add_in", dwin, core, sib_in)
    dh, ((got_in,),) = _mm_nt("proj_qkv_bwd", [dqkv], [wc], F32, k0_list=[0], tk=_pick(W6, (1024, 512, 256, 128)), tm=1024,
                              jobs=[_job_scatter([part_in], part=(0, 5, 8))])
    dh, ((got_in,),) = _mm_nt("proj_gates_bwd", [dgates], [wc], F32, k0_list=[W6], tm=1024,
                              tk=_pick(math_gcd(W6, 2 * D), (1024, 512, 256, 128)), residual=dh,
                              jobs=[_job_scatter([part_in], part=(5, 8, 8), into=[got_in])])
    full_in = _sum4("rs_sum_in", got_in, part_in, chip, core)
    dh, ((g_win,),) = _mm_nt("proj_f_bwd", [df], [wc], F32, k0_list=[nmain], tk=LANE, residual=dh,
                             jobs=[_job_swap([full_in])])
    grad_x, gg_mix = _rms_bwd("rms1_bwd", [dh], xs, r1, g_mix, dx1, False)
    g_in = lax.switch(chip, [functools.partial(_from_window, lay_k=lay[k]) for k in range(N_CHIPS)], g_win)

    for nm, w, g, m, v in (("w_in", w_in, g_in, m_w_in, v_w_in), ("w_branch_a", w_branch_a, g_a, m_w_branch_a, v_w_branch_a),
                           ("w_branch_b", w_branch_b, g_b, m_w_branch_b, v_w_branch_b), ("w_out", w_out, g_o, m_w_out, v_w_out),
                           ("w_up_ffn", w_up_ffn, g_u, m_w_up_ffn, v_w_up_ffn),
                           ("w_down_ffn", w_down_ffn, g_d, m_w_down_ffn, v_w_down_ffn)):
        adamw(nm, w, g, m, v)

    g_rel = _rel_grad_of(dbias2, onehot)
    small = [("loss", loss_part[:, :1], None, None, None),
             ("g_mix", gg_mix, g_mix, m_g_mix, v_g_mix), ("b_f", gbf[:, :nhb], b_f, m_b_f, v_b_f),
             ("b_gate", jnp.concatenate([gbg_a, gbg_b], axis=1), b_gate, m_b_gate, v_b_gate),
             ("rel_bias", g_rel, rel_bias, m_rel_bias, v_rel_bias), ("g_ffn", gg_ffn, g_ffn, m_g_ffn, v_g_ffn),
             ("g_final", gg_final, g_final, m_g_final, v_g_final)]
    sizes = [int(np.prod(s[1].shape)) for s in small]
    total = sum(sizes)
    npad = -(-total // 1024) * 1024

    def pack(arrs):
        flat = jnp.concatenate([a.reshape(-1).astype(F32) for a in arrs])
        return jnp.pad(flat, (0, npad - total)).reshape(8, npad // 8)

    zero1 = jnp.zeros((1,), F32)
    g_all = _small_allreduce("small_allreduce", pack([s[1] for s in small]))
    w_s = pack([zero1 if s[2] is None else s[2] for s in small])
    m_s = pack([zero1 if s[3] is None else s[3] for s in small])
    v_s = pack([zero1 + 1.0 if s[4] is None else s[4] for s in small])
    (d_s, mn_s, vn_s, _), _ = _adamw("adamw_small", w_s, g_all, m_s, v_s)

    def unpack(packed):
        flat = packed.reshape(-1)
        out, pos = {}, 0
        for s, n in zip(small, sizes):
            if s[2] is not None:
                out[s[0]] = flat[pos:pos + n].reshape(s[2].shape)
            else:
                out[s[0]] = flat[pos:pos + n].reshape(())
            pos += n
        return out

    gs, ds, ms, vs = unpack(g_all), unpack(d_s), unpack(mn_s), unpack(vn_s)
    order = ["g_mix", "w_in", "b_f", "b_gate", "rel_bias", "w_branch_a", "w_branch_b", "w_out", "g_ffn",
             "w_gate_ffn", "w_up_ffn", "w_down_ffn", "g_final"]
    res = [[], [], [], []]
    for nm in order:
        four = big[nm] if nm in big else (gs[nm], ds[nm], ms[nm], vs[nm])
        for q in range(4):
            res[q].append(four[q])
    return (gs["loss"], grad_x[None], *res[0], *res[1], *res[2], *res[3])


def math_gcd(a, b):
    while b:
        a, b = b, a % b
    return a
```

```python
import functools

import numpy as np
import jax
import jax.numpy as jnp
from jax import lax
from jax.experimental import pallas as pl
from jax.experimental.pallas import tpu as pltpu

F32 = jnp.float32
BF16 = jnp.bfloat16
LANE = 128
HEAD_DIM = 128
CHUNK = 64
LEFT_CHUNKS = 8
GROUP = 128
WIN_BLOCKS = 5
WIN = WIN_BLOCKS * GROUP
BAND = (LEFT_CHUNKS + 1) * CHUNK
RMS_EPS = 1e-6
NEG_INF = -1e30
ADAM_LR = 0.001
ADAM_B1 = 0.9
ADAM_B2 = 0.999
ADAM_EPS = 1e-08
ADAM_WD = 0.01
ADAM_STEP = 10
N_CHIPS = 4
MESH = pl.DeviceIdType.MESH
VMEM_LIMIT = 52 * 1024 * 1024
ANY = pl.BlockSpec(memory_space=pl.ANY)

NN = (((1,), (0,)), ((), ()))
NT = (((1,), (1,)), ((), ()))
TN = (((0,), (0,)), ((), ()))


def _cp(sem):
    return pltpu.CompilerParams(dimension_semantics=sem, vmem_limit_bytes=VMEM_LIMIT)


def _sds(shape, dtype):
    return jax.ShapeDtypeStruct(shape, dtype)


def _pick(n, prefs):
    for p in prefs:
        if n % p == 0:
            return p
    return n


def _sigmoid(v):
    return 1.0 / (1.0 + jnp.exp(-v))


def _split3(v):
    hi = v.astype(BF16)
    r1 = v - hi.astype(F32)
    mid = r1.astype(BF16)
    lo = (r1 - mid.astype(F32)).astype(BF16)
    return hi, mid, lo


def _col_of(blk, h):
    lane = lax.broadcasted_iota(jnp.int32, blk.shape, 1)
    return jnp.sum(jnp.where(lane == h, blk, 0.0), axis=1, keepdims=True)


def _put_col(ref, h, col):
    lane = lax.broadcasted_iota(jnp.int32, ref.shape, 1)
    ref[...] = jnp.where(lane == h, col, ref[...])


def _mm(name, mode, a_list, a_specs, b_list, b_specs, pairs, n_acc, grid, tm, tn,
        out_shapes, out_specs, epilogue, extra=(), extra_specs=(), jobs=()):
    n_a, n_b, n_e, n_o = len(a_list), len(b_list), len(extra), len(out_shapes)
    nk = grid[2]
    dn = {"nn": NN, "nt": NT, "tn": TN}[mode]

    def body(*refs):
        a_refs = refs[:n_a]
        b_refs = refs[n_a:n_a + n_b]
        e_refs = refs[n_a + n_b:n_a + n_b + n_e]
        o_refs = refs[n_a + n_b + n_e:n_a + n_b + n_e + n_o]
        acc_refs = refs[n_a + n_b + n_e + n_o:]
        k = pl.program_id(2)

        @pl.when(k == 0)
        def _():
            for acc in acc_refs:
                acc[...] = jnp.zeros_like(acc)

        for ai, bi, ci in pairs:
            acc_refs[ci][...] += lax.dot_general(a_refs[ai][...], b_refs[bi][...], dn,
                                                 preferred_element_type=F32)

        @pl.when(k == nk - 1)
        def _():
            epilogue([acc[...] for acc in acc_refs], e_refs, o_refs)

    return _pcall(
        body, name=name, grid=grid,
        in_specs=list(a_specs) + list(b_specs) + list(extra_specs),
        out_specs=list(out_specs), out_shape=list(out_shapes),
        scratch_shapes=[pltpu.VMEM((tm, tn), F32) for _ in range(n_acc)],
        sem=("parallel", "parallel", "arbitrary"), jobs=jobs,
    )(*a_list, *b_list, *extra)


def _one(res, jobs):
    outs, jouts = res
    return (outs[0], jouts) if jobs else outs[0]


def _store(dtype):
    def ep(accs, e_refs, o_refs):
        o_refs[0][...] = accs[0].astype(dtype)
    return ep


def _mm_nn(name, a, b, out_dtype, *, b_col0=0, n=None, tm=512, tn=None, tk=None, residual=None, jobs=()):
    M, K = a.shape
    if b.ndim == 3:
        Ns = b.shape[2]
        n = b.shape[0] * Ns
        tn = tn or _pick(Ns, (1408, 1024, 512, 256, 128))
        nps = Ns // tn
        b_spec = pl.BlockSpec((None, tk or _pick(K, (1024, 512, 256, 128)), tn),
                              lambda i, j, k: (j // nps, k, j % nps))
    else:
        n = n or b.shape[1]
        tn = tn or _pick(math_gcd(n, b_col0) if b_col0 else n, (2048, 1024, 512, 256, 128))
        assert b_col0 % tn == 0 and n % tn == 0
        c0 = b_col0 // tn
        b_spec = pl.BlockSpec((tk or _pick(K, (1024, 512, 256, 128)), tn), lambda i, j, k: (k, c0 + j))
    tk = tk or _pick(K, (1024, 512, 256, 128))
    tm = _pick(M, (tm, 256, 128))
    grid = (M // tm, n // tn, K // tk)
    a_spec = pl.BlockSpec((tm, tk), lambda i, j, k: (i, k))
    o_spec = pl.BlockSpec((tm, tn), lambda i, j, k: (i, j))
    if residual is None:
        return _one(_mm(name, "nn", [a], [a_spec], [b], [b_spec], [(0, 0, 0)], 1, grid, tm, tn,
                        [_sds((M, n), out_dtype)], [o_spec], _store(out_dtype), jobs=jobs), jobs)

    def ep(accs, e_refs, o_refs):
        o_refs[0][...] = (e_refs[0][...] + accs[0]).astype(out_dtype)
    return _one(_mm(name, "nn", [a], [a_spec], [b], [b_spec], [(0, 0, 0)], 1, grid, tm, tn,
                    [_sds((M, n), out_dtype)], [o_spec], ep, extra=[residual], extra_specs=[o_spec], jobs=jobs), jobs)


def _mm_nt(name, a_list, b_list, out_dtype, *, k0_list=None, tm=512, tn=None, tk=None, residual=None, jobs=()):
    M, K = a_list[0].shape
    b0 = b_list[0]
    N = b0.shape[1] if b0.ndim == 3 else b0.shape[0]
    tm = _pick(M, (tm, 256, 128))
    tn = tn or _pick(N, (1024, 512, 256, 128))
    if b0.ndim == 3:
        Ks = b0.shape[2]
        tk = tk or _pick(Ks, (1408, 1024, 512, 256, 128))
        kps = Ks // tk
        b_specs = [pl.BlockSpec((None, tn, tk), lambda i, j, k: (k // kps, j, k % kps)) for _ in b_list]
    else:
        tk = tk or _pick(K, (1024, 896, 512, 256, 128))
        k0_list = k0_list or [0] * len(b_list)
        b_specs = []
        for k0 in k0_list:
            assert k0 % tk == 0
            b_specs.append(pl.BlockSpec((tn, tk), functools.partial(lambda i, j, k, c: (j, c + k), c=k0 // tk)))
    grid = (M // tm, N // tn, K // tk)
    a_specs = [pl.BlockSpec((tm, tk), lambda i, j, k: (i, k)) for _ in a_list]
    o_spec = pl.BlockSpec((tm, tn), lambda i, j, k: (i, j))
    pairs = [(p, p, 0) for p in range(len(a_list))]
    if residual is None:
        return _one(_mm(name, "nt", a_list, a_specs, b_list, b_specs, pairs, 1, grid, tm, tn,
                        [_sds((M, N), out_dtype)], [o_spec], _store(out_dtype), jobs=jobs), jobs)

    def ep(accs, e_refs, o_refs):
        o_refs[0][...] = (e_refs[0][...] + accs[0]).astype(out_dtype)
    return _one(_mm(name, "nt", a_list, a_specs, b_list, b_specs, pairs, 1, grid, tm, tn,
                    [_sds((M, N), out_dtype)], [o_spec], ep, extra=[residual], extra_specs=[o_spec], jobs=jobs), jobs)


def _mm_tn(name, a, b, out_dtype, *, slots=None, tm=None, tn=None, tk=1024, jobs=()):
    Kc, Mo = a.shape
    No = b.shape[1]
    tm = tm or _pick(Mo, (1024, 704, 512, 256, 128))
    tk = _pick(Kc, (tk, 256, 128))
    if slots:
        Ns = No // slots
        tn = tn or _pick(Ns, (1408, 1024, 512, 256, 128))
        nps = Ns // tn
        o_spec = pl.BlockSpec((None, tm, tn), lambda i, j, k: (j // nps, i, j % nps))
        o_shape = _sds((slots, Mo, Ns), out_dtype)
    else:
        tn = tn or _pick(No, (1024, 512, 256, 128))
        o_spec = pl.BlockSpec((tm, tn), lambda i, j, k: (i, j))
        o_shape = _sds((Mo, No), out_dtype)
    grid = (Mo // tm, No // tn, Kc // tk)
    a_spec = pl.BlockSpec((tk, tm), lambda i, j, k: (k, i))
    b_spec = pl.BlockSpec((tk, tn), lambda i, j, k: (k, j))
    return _one(_mm(name, "tn", [a], [a_spec], [b], [b_spec], [(0, 0, 0)], 1, grid, tm, tn,
                    [o_shape], [o_spec], _store(out_dtype), jobs=jobs), jobs)


def _cast_bf16(name, w, chip):
    R, C = w.shape
    tr = _pick(R, (256, 128, 64, 32, 16))

    def body(k_ref, w_ref, o_ref):
        o_ref[...] = w_ref[...].astype(BF16)

    gs = pltpu.PrefetchScalarGridSpec(
        num_scalar_prefetch=1, grid=(R // tr,),
        in_specs=[pl.BlockSpec((tr, C), lambda i, k: (i, 0))],
        out_specs=pl.BlockSpec((None, tr, C), lambda i, k: (k[0], i, 0)))
    return pl.pallas_call(body, name=name, grid_spec=gs, out_shape=_sds((N_CHIPS, R, C), BF16),
                          compiler_params=_cp(("parallel",)))(jnp.reshape(chip, (1,)).astype(jnp.int32), w)


def _rms_fwd(name, x, g):
    T, D = x.shape
    tr = _pick(T, (256, 128))

    def body(x_ref, g_ref, h_ref, r_ref):
        xv = x_ref[...]
        r = lax.rsqrt(jnp.mean(xv * xv, axis=1, keepdims=True) + RMS_EPS)
        h_ref[...] = (xv * r * g_ref[...]).astype(BF16)
        r_ref[...] = r

    row = pl.BlockSpec((tr, D), lambda i: (i, 0))
    return pl.pallas_call(
        body, name=name, grid=(T // tr,),
        in_specs=[row, pl.BlockSpec((1, D), lambda i: (0, 0))],
        out_specs=[row, pl.BlockSpec((tr, 1), lambda i: (i, 0))],
        out_shape=[_sds((T, D), BF16), _sds((T, 1), F32)], compiler_params=_cp(("parallel",)),
    )(x, g)


def _final_loss_bwd(name, x2, tgt, g):
    T, D = x2.shape
    tr = _pick(T, (256, 128))

    def body(x_ref, t_ref, g_ref, dx_ref, dxb_ref, loss_ref, gg_ref):
        @pl.when(pl.program_id(0) == 0)
        def _():
            loss_ref[...] = jnp.zeros_like(loss_ref)
            gg_ref[...] = jnp.zeros_like(gg_ref)

        xv = x_ref[...]
        gv = g_ref[...]
        r = lax.rsqrt(jnp.mean(xv * xv, axis=1, keepdims=True) + RMS_EPS)
        n = xv * r
        e = n * gv - t_ref[...]
        loss_ref[...] += 0.5 * jnp.sum(jnp.mean(e * e, axis=1, keepdims=True), axis=0, keepdims=True)
        dy = e * (1.0 / D)
        gg_ref[...] += jnp.sum(dy * n, axis=0, keepdims=True)
        gy = dy * gv
        dx = r * (gy - n * jnp.mean(gy * n, axis=1, keepdims=True))
        dx_ref[...] = dx
        dxb_ref[...] = dx.astype(BF16)

    row = pl.BlockSpec((tr, D), lambda i: (i, 0))
    vec = pl.BlockSpec((1, D), lambda i: (0, 0))
    return pl.pallas_call(
        body, name=name, grid=(T // tr,),
        in_specs=[row, row, vec],
        out_specs=[row, row, pl.BlockSpec((1, LANE), lambda i: (0, 0)), vec],
        out_shape=[_sds((T, D), F32), _sds((T, D), BF16), _sds((1, LANE), F32), _sds((1, D), F32)],
        compiler_params=_cp(("arbitrary",)),
    )(x2, tgt, g)


def _rms_bwd(name, dh_list, x, r, g, dres, want_bf16):
    T, D = x.shape
    tr = _pick(T, (128,))
    n_dh = len(dh_list)

    def body(*refs):
        dh_refs = refs[:n_dh]
        x_ref, r_ref, g_ref, dres_ref = refs[n_dh:n_dh + 4]
        outs = refs[n_dh + 4:]
        gg_ref = outs[-1]

        @pl.when(pl.program_id(0) == 0)
        def _():
            gg_ref[...] = jnp.zeros_like(gg_ref)

        dh = dh_refs[0][...]
        for ref in dh_refs[1:]:
            dh = dh + ref[...]
        rv = r_ref[...]
        n = x_ref[...] * rv
        gg_ref[...] += jnp.sum(dh * n, axis=0, keepdims=True)
        gy = dh * g_ref[...]
        dx = dres_ref[...] + rv * (gy - n * jnp.mean(gy * n, axis=1, keepdims=True))
        outs[0][...] = dx
        if want_bf16:
            outs[1][...] = dx.astype(BF16)

    row = pl.BlockSpec((tr, D), lambda i: (i, 0))
    vec = pl.BlockSpec((1, D), lambda i: (0, 0))
    out_specs = [row] + ([row] if want_bf16 else []) + [vec]
    out_shape = [_sds((T, D), F32)] + ([_sds((T, D), BF16)] if want_bf16 else []) + [_sds((1, D), F32)]
    return pl.pallas_call(
        body, name=name, grid=(T // tr,),
        in_specs=[row] * n_dh + [row, pl.BlockSpec((tr, 1), lambda i: (i, 0)), vec, row],
        out_specs=out_specs, out_shape=out_shape, compiler_params=_cp(("arbitrary",)),
    )(*dh_list, x, r, g, dres)


def _merge_fwd(name, gates, u_a, u_b, b_gate):
    T, D = u_a.shape
    tr = _pick(T, (256, 128))

    def body(ga_ref, gb_ref, ua_ref, ub_ref, ba_ref, bb_ref, o_ref):
        sa = _sigmoid(ga_ref[...].astype(F32) + ba_ref[...])
        sb = _sigmoid(gb_ref[...].astype(F32) + bb_ref[...])
        o_ref[...] = (sa * ua_ref[...].astype(F32) + sb * ub_ref[...].astype(F32)).astype(BF16)

    row = pl.BlockSpec((tr, D), lambda i: (i, 0))
    row1 = pl.BlockSpec((tr, D), lambda i: (i, 1))
    v0 = pl.BlockSpec((1, D), lambda i: (0, 0))
    v1 = pl.BlockSpec((1, D), lambda i: (0, 1))
    return pl.pallas_call(
        body, name=name, grid=(T // tr,),
        in_specs=[row, row1, row, row, v0, v1], out_specs=row,
        out_shape=_sds((T, D), BF16), compiler_params=_cp(("parallel",)),
    )(gates, gates, u_a, u_b, b_gate, b_gate)


def _merge_bwd(name, dm, gates, u_a, u_b, b_gate):
    T, D = u_a.shape
    tr = _pick(T, (128,))

    def body(dm_ref, ga_ref, gb_ref, ua_ref, ub_ref, ba_ref, bb_ref, dua_ref, dub_ref, dga_ref, dgb_ref,
             gba_ref, gbb_ref):
        @pl.when(pl.program_id(0) == 0)
        def _():
            gba_ref[...] = jnp.zeros_like(gba_ref)
            gbb_ref[...] = jnp.zeros_like(gbb_ref)

        d = dm_ref[...].astype(F32)
        sa = _sigmoid(ga_ref[...].astype(F32) + ba_ref[...])
        sb = _sigmoid(gb_ref[...].astype(F32) + bb_ref[...])
        dua_ref[...] = (d * sa).astype(BF16)
        dub_ref[...] = (d * sb).astype(BF16)
        dga = d * ua_ref[...].astype(F32) * sa * (1.0 - sa)
        dgb = d * ub_ref[...].astype(F32) * sb * (1.0 - sb)
        dga_ref[...] = dga.astype(BF16)
        dgb_ref[...] = dgb.astype(BF16)
        gba_ref[...] += jnp.sum(dga, axis=0, keepdims=True)
        gbb_ref[...] += jnp.sum(dgb, axis=0, keepdims=True)

    row = pl.BlockSpec((tr, D), lambda i: (i, 0))
    row1 = pl.BlockSpec((tr, D), lambda i: (i, 1))
    v0 = pl.BlockSpec((1, D), lambda i: (0, 0))
    v1 = pl.BlockSpec((1, D), lambda i: (0, 1))
    outs = pl.pallas_call(
        body, name=name, grid=(T // tr,),
        in_specs=[row, row, row1, row, row, v0, v1],
        out_specs=[row, row, row, row, v0, v0],
        out_shape=[_sds((T, D), BF16), _sds((T, D), BF16), _sds((T, D), BF16), _sds((T, D), BF16),
                   _sds((1, D), F32), _sds((1, D), F32)],
        compiler_params=_cp(("arbitrary",)),
    )(dm, gates, gates, u_a, u_b, b_gate, b_gate)
    return outs


def _adamw(name, w, g, m, v, jobs=()):
    R, C = w.shape
    Cg = g.shape[1]
    tr = _pick(R, (64, 32, 16, 8))
    c1 = 1.0 - ADAM_B1 ** ADAM_STEP
    c2 = 1.0 - ADAM_B2 ** ADAM_STEP

    def body(w_ref, g_ref, m_ref, v_ref, d_ref, mo_ref, vo_ref, go_ref):
        gv = g_ref[...] if Cg == C else g_ref[:, :C]
        mn = ADAM_B1 * m_ref[...] + (1.0 - ADAM_B1) * gv
        vn = ADAM_B2 * v_ref[...] + (1.0 - ADAM_B2) * (gv * gv)
        d_ref[...] = -ADAM_LR * ((mn / c1) / (jnp.sqrt(vn / c2) + ADAM_EPS) + ADAM_WD * w_ref[...])
        mo_ref[...] = mn
        vo_ref[...] = vn
        go_ref[...] = gv

    blk = pl.BlockSpec((tr, C), lambda i: (i, 0))
    gblk = pl.BlockSpec((tr, Cg), lambda i: (i, 0))
    return _pcall(
        body, name=name, grid=(R // tr,),
        in_specs=[blk, gblk, blk, blk], out_specs=[blk] * 4,
        out_shape=[_sds((R, C), F32)] * 4, sem=("parallel",), jobs=jobs,
    )(w, g, m, v)


def _add_bf16(name, a, a_row0, b):
    S, h, C = b.shape
    tr = _pick(h, (256, 128, 64, 32, 16))
    nb = h // tr

    def body(off_ref, a_ref, b_ref, o_ref):
        o_ref[...] = (a_ref[...].astype(F32) + b_ref[...].astype(F32)).astype(BF16)

    gs = pltpu.PrefetchScalarGridSpec(
        num_scalar_prefetch=1, grid=(S, nb),
        in_specs=[pl.BlockSpec((None, tr, C), lambda s, i, off: (s, off[0] * nb + i, 0)),
                  pl.BlockSpec((None, tr, C), lambda s, i, off: (s, i, 0))],
        out_specs=pl.BlockSpec((None, tr, C), lambda s, i, off: (s, i, 0)))
    return pl.pallas_call(body, name=name, grid_spec=gs, out_shape=_sds((S, h, C), BF16),
                          compiler_params=_cp(("parallel", "parallel")))(
        jnp.reshape(a_row0, (1,)).astype(jnp.int32), a, b)


def _sum4(name, got, mine, chip, core):
    S, h, C = got.shape
    tr = _pick(h, (256, 128, 64, 32, 16))
    nb = h // tr

    def body(chip_ref, core_ref, m_ref, g_ref, o_ref):
        acc = m_ref[...].astype(F32)
        for s in range(S):
            acc = acc + g_ref[s].astype(F32)
        o_ref[...] = acc

    gs = pltpu.PrefetchScalarGridSpec(
        num_scalar_prefetch=2, grid=(nb,),
        in_specs=[pl.BlockSpec((None, tr, C), lambda i, kc, cc: (kc[0], i, 0)),
                  pl.BlockSpec((S, tr, C), lambda i, kc, cc: (0, i, 0))],
        out_specs=pl.BlockSpec((tr, C), lambda i, kc, cc: (cc[0] * nb + i, 0)))
    return pl.pallas_call(body, name=name, grid_spec=gs, out_shape=_sds((2 * h, C), F32),
                          compiler_params=_cp(("parallel",)))(
        jnp.reshape(chip, (1,)).astype(jnp.int32), jnp.reshape(core, (1,)).astype(jnp.int32), mine, got)


def _place():
    x, y, c = lax.axis_index("x"), lax.axis_index("y"), lax.axis_index("c")
    chips = [(1 - x, y), (x, 1 - y), (1 - x, 1 - y)]
    return x, y, c, chips


def _allgather(name, shards):
    n = len(shards)
    NS = 7

    def body(*refs):
        out_refs = refs[n:2 * n]
        ss, rs = refs[2 * n:]
        x, y, c, _ = _place()
        k, kx, ky, kd = 2 * x + y, 2 * (1 - x) + y, 2 * x + (1 - y), 2 * (1 - x) + (1 - y)
        across_x, across_y, sibling = (1 - x, y, c), (x, 1 - y, c), (x, y, 1 - c)
        sends = []

        def go(cp):
            cp.start()
            sends.append(cp)

        for a, out in enumerate(out_refs):
            h = out.shape[1] // 2
            q = h // 2
            half = lambda slot, cc=c: out.at[slot, pl.ds(cc * h, h), :]
            part0 = lambda slot: out.at[slot, pl.ds(c * h, q), :]
            part1 = lambda slot: out.at[slot, pl.ds(c * h + q, q), :]
            b = NS * a
            go(_rdma(half(k), half(k), ss, rs, b + 0, across_x))
            go(_rdma(half(k), half(k), ss, rs, b + 1, across_y))
            _rdma(half(kx), half(kx), ss, rs, b + 0, across_x).wait_recv()
            go(_rdma(part0(kx), part0(kx), ss, rs, b + 2, across_y))
            go(_rdma(half(kx), half(kx), ss, rs, b + 4, sibling))
            _rdma(half(ky), half(ky), ss, rs, b + 1, across_y).wait_recv()
            go(_rdma(part1(ky), part1(ky), ss, rs, b + 3, across_x))
            go(_rdma(half(ky), half(ky), ss, rs, b + 5, sibling))
            _rdma(part0(kd), part0(kd), ss, rs, b + 2, across_y).wait_recv()
            _rdma(part1(kd), part1(kd), ss, rs, b + 3, across_x).wait_recv()
            go(_rdma(half(kd), half(kd), ss, rs, b + 6, sibling))
        for a, out in enumerate(out_refs):
            h = out.shape[1] // 2
            for j, slot in enumerate((kx, ky, kd)):
                rows = out.at[slot, pl.ds((1 - c) * h, h), :]
                _rdma(rows, rows, ss, rs, NS * a + 4 + j, sibling).wait_recv()
        for cp in sends:
            cp.wait_send()

    return pl.pallas_call(
        body, name=name,
        in_specs=[ANY] * n, out_specs=[ANY] * n,
        out_shape=[_sds(s.shape, s.dtype) for s in shards],
        input_output_aliases={a: a for a in range(n)},
        scratch_shapes=[pltpu.SemaphoreType.DMA((NS * n,)), pltpu.SemaphoreType.DMA((NS * n,))],
    )(*shards)


class _Job:
    def __init__(self, ins, out_shapes, aliases, n_sems, start, finish):
        self.ins, self.out_shapes, self.aliases, self.n_sems = list(ins), list(out_shapes), dict(aliases), n_sems
        self.start, self.finish = start, finish


def _rdma(src, dst, ss, rs, idx, to):
    return pltpu.make_async_remote_copy(src_ref=src, dst_ref=dst, send_sem=ss.at[idx], recv_sem=rs.at[idx],
                                        device_id=to, device_id_type=MESH)


def _job_gather_ici(bufs, part=(0, 1)):
    pi, pn = part

    def descs(outs, ss, rs, incoming):
        x, y, c, chips = _place()
        res = []
        for a, out in enumerate(outs):
            h = out.shape[1] // 2
            hp = h // pn
            for j, (cx, cy) in enumerate(chips):
                rows = out.at[(2 * cx + cy) if incoming else (2 * x + y), pl.ds(c * h + pi * hp, hp), :]
                res.append(_rdma(rows, rows, ss, rs, 3 * a + j, (cx, cy, c)))
        return res

    def start(ins, outs, ss, rs):
        for d in descs(outs, ss, rs, False):
            d.start()

    def finish(ins, outs, ss, rs):
        for d in descs(outs, ss, rs, True):
            d.wait_recv()
        for d in descs(outs, ss, rs, False):
            d.wait_send()

    return _Job(bufs, [_sds(b.shape, b.dtype) for b in bufs], {a: a for a in range(len(bufs))}, 3 * len(bufs),
                start, finish)


def _job_gather_d2d(bufs):
    def descs(outs, ss, rs, incoming):
        x, y, c, chips = _place()
        res = []
        for a, out in enumerate(outs):
            h = out.shape[1] // 2
            for j, (cx, cy) in enumerate(chips):
                rows = out.at[2 * cx + cy, pl.ds(((1 - c) if incoming else c) * h, h), :]
                res.append(_rdma(rows, rows, ss, rs, 3 * a + j, (x, y, 1 - c)))
        return res

    def start(ins, outs, ss, rs):
        for d in descs(outs, ss, rs, False):
            d.start()

    def finish(ins, outs, ss, rs):
        for d in descs(outs, ss, rs, True):
            d.wait_recv()
        for d in descs(outs, ss, rs, False):
            d.wait_send()

    return _Job(bufs, [_sds(b.shape, b.dtype) for b in bufs], {a: a for a in range(len(bufs))}, 3 * len(bufs),
                start, finish)


def _job_sibling(grads):
    def descs(ins, outs, ss, rs):
        x, y, c, _ = _place()
        res = []
        for a, (g, out) in enumerate(zip(ins, outs)):
            h = g.shape[1] // 2
            res.append(_rdma(g.at[:, pl.ds((1 - c) * h, h), :], out, ss, rs, a, (x, y, 1 - c)))
        return res

    def start(ins, outs, ss, rs):
        for d in descs(ins, outs, ss, rs):
            d.start()

    def finish(ins, outs, ss, rs):
        for d in descs(ins, outs, ss, rs):
            d.wait()

    return _Job(grads, [_sds((g.shape[0], g.shape[1] // 2, g.shape[2]), g.dtype) for g in grads], {}, len(grads),
                start, finish)


def _job_scatter(parts, part=(0, 1), into=None):
    p0, p1, pn = part if len(part) == 3 else (part[0], part[0] + 1, part[1])
    n = len(parts)

    def descs(ins, outs, ss, rs):
        x, y, c, chips = _place()
        res = []
        for a, (p, out) in enumerate(zip(ins[:n], outs)):
            hp = p.shape[1] // pn
            rows = pl.ds(p0 * hp, (p1 - p0) * hp)
            for j, (cx, cy) in enumerate(chips):
                res.append(_rdma(p.at[2 * cx + cy, rows, :], out.at[j, rows, :], ss, rs, 3 * a + j, (cx, cy, c)))
        return res

    def start(ins, outs, ss, rs):
        for d in descs(ins, outs, ss, rs):
            d.start()

    def finish(ins, outs, ss, rs):
        for d in descs(ins, outs, ss, rs):
            d.wait()

    shapes = [_sds((3,) + p.shape[1:], p.dtype) for p in parts]
    if into is None:
        return _Job(parts, shapes, {}, 3 * n, start, finish)
    return _Job(list(parts) + list(into), shapes, {n + a: a for a in range(n)}, 3 * n, start, finish)


def _job_swap(fulls):
    def descs(outs, ss, rs, incoming):
        x, y, c, _ = _place()
        res = []
        for a, out in enumerate(outs):
            h = out.shape[0] // 2
            rows = out.at[pl.ds(((1 - c) if incoming else c) * h, h), :]
            res.append(_rdma(rows, rows, ss, rs, a, (x, y, 1 - c)))
        return res

    def start(ins, outs, ss, rs):
        for d in descs(outs, ss, rs, False):
            d.start()

    def finish(ins, outs, ss, rs):
        for d in descs(outs, ss, rs, True):
            d.wait_recv()
        for d in descs(outs, ss, rs, False):
            d.wait_send()

    return _Job(fulls, [_sds(f.shape, f.dtype) for f in fulls], {a: a for a in range(len(fulls))}, len(fulls),
                start, finish)


def _pcall(body, *, name, grid, in_specs, out_specs, out_shape, scratch_shapes=(), sem, jobs=()):
    in_specs, out_specs, out_shape = list(in_specs), list(out_specs), list(out_shape)
    scratch = list(scratch_shapes)
    n_in, n_out, n_scr = len(in_specs), len(out_shape), len(scratch)
    if not jobs:
        call = pl.pallas_call(body, name=name, grid=grid, in_specs=in_specs, out_specs=out_specs, out_shape=out_shape,
                              scratch_shapes=scratch, compiler_params=_cp(sem))
        return lambda *args: (call(*args), [])
    jin = sum(len(j.ins) for j in jobs)
    jout = sum(len(j.out_shapes) for j in jobs)
    aliases, pi, po = {}, n_in, n_out
    for j in jobs:
        for ia, oa in j.aliases.items():
            aliases[pi + ia] = po + oa
        pi, po = pi + len(j.ins), po + len(j.out_shapes)

    def wrapped(*refs):
        ins = refs[:n_in]
        jins = refs[n_in:n_in + jin]
        outs = refs[n_in + jin:n_in + jin + n_out]
        jouts = refs[n_in + jin + n_out:n_in + jin + n_out + jout]
        scr = refs[n_in + jin + n_out + jout:n_in + jin + n_out + jout + n_scr]
        sems = refs[n_in + jin + n_out + jout + n_scr:]
        first, last = None, None
        for d, g in enumerate(grid):
            f, l = pl.program_id(d) == 0, pl.program_id(d) == g - 1
            first = f if first is None else jnp.logical_and(first, f)
            last = l if last is None else jnp.logical_and(last, l)

        def each(what):
            pi, po = 0, 0
            for q, j in enumerate(jobs):
                getattr(j, what)(jins[pi:pi + len(j.ins)], jouts[po:po + len(j.out_shapes)], sems[2 * q], sems[2 * q + 1])
                pi, po = pi + len(j.ins), po + len(j.out_shapes)

        @pl.when(first)
        def _():
            each("start")

        body(*ins, *outs, *scr)

        @pl.when(last)
        def _():
            each("finish")

    call = pl.pallas_call(
        wrapped, name=name, grid=grid,
        in_specs=in_specs + [ANY] * jin, out_specs=out_specs + [ANY] * jout,
        out_shape=out_shape + [s for j in jobs for s in j.out_shapes],
        input_output_aliases=aliases,
        scratch_shapes=scratch + [pltpu.SemaphoreType.DMA((j.n_sems,)) for j in jobs for _ in range(2)],
        compiler_params=_cp(("arbitrary",) * len(grid)))

    def run(*args):
        res = call(*args, *[a for j in jobs for a in j.ins])
        main, rest, per_job = list(res[:n_out]), list(res[n_out:]), []
        for j in jobs:
            per_job.append(rest[:len(j.out_shapes)])
            rest = rest[len(j.out_shapes):]
        return main, per_job
    return run


def _comm_only(name, jobs):
    jin = sum(len(j.ins) for j in jobs)
    jout = sum(len(j.out_shapes) for j in jobs)
    aliases, pi, po = {}, 0, 0
    for j in jobs:
        for ia, oa in j.aliases.items():
            aliases[pi + ia] = po + oa
        pi, po = pi + len(j.ins), po + len(j.out_shapes)

    def body(*refs):
        jins, jouts, sems = refs[:jin], refs[jin:jin + jout], refs[jin + jout:]
        for what in ("start", "finish"):
            pi, po = 0, 0
            for q, j in enumerate(jobs):
                getattr(j, what)(jins[pi:pi + len(j.ins)], jouts[po:po + len(j.out_shapes)], sems[2 * q], sems[2 * q + 1])
                pi, po = pi + len(j.ins), po + len(j.out_shapes)

    res = pl.pallas_call(
        body, name=name, in_specs=[ANY] * jin, out_specs=[ANY] * jout,
        out_shape=[s for j in jobs for s in j.out_shapes], input_output_aliases=aliases,
        scratch_shapes=[pltpu.SemaphoreType.DMA((j.n_sems,)) for j in jobs for _ in range(2)],
    )(*[a for j in jobs for a in j.ins])
    rest, per_job = list(res), []
    for j in jobs:
        per_job.append(rest[:len(j.out_shapes)])
        rest = rest[len(j.out_shapes):]
    return per_job


def _small_allreduce(name, v):
    m_per, n = v.shape

    def body(x_ref, sum_ref, all_ref, send_sems, recv_sems, local_sem):
        x, y, c, chips = _place()
        me, sibling = (x, y, c), (x, y, 1 - c)

        def rows(px, py, pc):
            return all_ref.at[pl.ds((4 * px + 2 * py + pc) * m_per, m_per), :]

        def copy(kk, block, to, src=None):
            return pltpu.make_async_remote_copy(
                src_ref=rows(*block) if src is None else src, dst_ref=rows(*block),
                send_sem=send_sems.at[kk], recv_sem=recv_sems.at[kk], device_id=to, device_id_type=MESH)

        mine = pltpu.make_async_copy(x_ref, rows(*me), local_sem)
        mine.start()
        first = [copy(0, me, sibling, src=x_ref)]
        first += [copy(1 + j, me, (*chip, c), src=x_ref) for j, chip in enumerate(chips)]
        for cp in first:
            cp.start()
        passed = [copy(4 + j, (*chip, c), sibling) for j, chip in enumerate(chips)]
        for j, chip in enumerate(chips):
            copy(1 + j, (*chip, c), me).wait_recv()
            passed[j].start()
        copy(0, sibling, me).wait_recv()
        for j, chip in enumerate(chips):
            copy(4 + j, (*chip, 1 - c), me).wait_recv()
        for cp in first + passed:
            cp.wait_send()
        mine.wait()
        acc = all_ref[pl.ds(0, m_per), :]
        for d in range(1, 8):
            acc = acc + all_ref[pl.ds(d * m_per, m_per), :]
        sum_ref[...] = acc

    vm = pl.BlockSpec(memory_space=pltpu.VMEM)
    return pl.pallas_call(
        body, name=name, in_specs=[vm], out_specs=[vm, vm],
        out_shape=[_sds((m_per, n), F32), _sds((8 * m_per, n), F32)],
        scratch_shapes=[pltpu.SemaphoreType.DMA((7,)), pltpu.SemaphoreType.DMA((7,)), pltpu.SemaphoreType.DMA],
    )(v)[0]


def _in_layout(D, W6, nhb, Ls):
    nmain = W6 + 2 * D
    lay = []
    for k in range(N_CHIPS):
        g0, g1 = k * Ls, (k + 1) * Ls
        pieces = []
        a, b = max(g0, 0), min(g1, W6)
        if a < b:
            pieces.append((a - g0, b - g0, a))
        a, b = max(g0, W6 + nhb), min(g1, W6 + nhb + 2 * D)
        if a < b:
            pieces.append((a - g0, b - g0, a - nhb))
        a, b = max(g0, W6), min(g1, W6 + nhb)
        fpiece = (a - g0, b - g0, a - W6) if a < b else None
        assert fpiece is None or (b - a) == nhb
        main0 = min(p[2] for p in pieces)
        main1 = max(p[2] + p[1] - p[0] for p in pieces)
        lay.append(dict(pieces=pieces, f=fpiece, s=main0 // LANE, e=-(-main1 // LANE), main1=main1))
    assert sum(1 for l in lay if l["f"] is not None) == 1
    nbw = max(l["e"] - l["s"] + (1 if l["f"] else 0) for l in lay)
    for k in range(1, N_CHIPS):
        assert lay[k]["s"] >= lay[k - 1]["e"] - 1 and lay[k]["s"] > lay[k - 1]["s"]
    return lay, nbw, nmain


def _to_window(w, lay_k, nbw):
    D = w.shape[0]
    items = [(c0 - lay_k["s"] * LANE, l0, l1) for (l0, l1, c0) in lay_k["pieces"]]
    if lay_k["f"]:
        l0, l1, off = lay_k["f"]
        items.append(((lay_k["e"] - lay_k["s"]) * LANE + off, l0, l1))
    items.sort()
    cols, pos = [], 0
    for w0, l0, l1 in items:
        if w0 > pos:
            cols.append(jnp.zeros((D, w0 - pos), w.dtype))
        cols.append(w[:, l0:l1])
        pos = w0 + (l1 - l0)
    if pos < nbw * LANE:
        cols.append(jnp.zeros((D, nbw * LANE - pos), w.dtype))
    return jnp.concatenate(cols, axis=1)


def _from_window(win, lay_k):
    items = [(l0, c0 - lay_k["s"] * LANE, l1 - l0) for (l0, l1, c0) in lay_k["pieces"]]
    if lay_k["f"]:
        l0, l1, off = lay_k["f"]
        items.append((l0, (lay_k["e"] - lay_k["s"]) * LANE + off, l1 - l0))
    items.sort()
    return jnp.concatenate([win[:, w0:w0 + n] for (_, w0, n) in items], axis=1)


def _assemble_in(name, wins, lay, nbw, nmain):
    _, D, _ = wins.shape
    ncb = nmain // LANE + 1
    k1 = np.zeros(ncb, np.int32)
    i1 = np.zeros(ncb, np.int32)
    k2 = np.zeros(ncb, np.int32)
    i2 = np.zeros(ncb, np.int32)
    fl = np.zeros(ncb, np.int32)
    for b in range(ncb - 1):
        k = max(kk for kk in range(N_CHIPS) if lay[kk]["s"] <= b)
        k1[b], i1[b] = k, b - lay[k]["s"]
        if k >= 1 and b == lay[k]["s"] and lay[k - 1]["main1"] > b * LANE:
            k2[b], i2[b], fl[b] = k - 1, b - lay[k - 1]["s"], 1
    kf = [kk for kk in range(N_CHIPS) if lay[kk]["f"]][0]
    k1[ncb - 1], i1[ncb - 1] = kf, lay[kf]["e"] - lay[kf]["s"]

    def body(k1_ref, i1_ref, k2_ref, i2_ref, fl_ref, a_ref, b_ref, o_ref):
        b = pl.program_id(0)
        add = jnp.where(fl_ref[b] == 1, b_ref[...], jnp.zeros_like(b_ref))
        o_ref[...] = a_ref[...] + add

    gs = pltpu.PrefetchScalarGridSpec(
        num_scalar_prefetch=5, grid=(ncb,),
        in_specs=[pl.BlockSpec((None, D, LANE), lambda b, k1r, i1r, k2r, i2r, flr: (k1r[b], 0, i1r[b])),
                  pl.BlockSpec((None, D, LANE), lambda b, k1r, i1r, k2r, i2r, flr: (k2r[b], 0, i2r[b]))],
        out_specs=pl.BlockSpec((D, LANE), lambda b, k1r, i1r, k2r, i2r, flr: (0, b)))
    return pl.pallas_call(body, name=name, grid_spec=gs, out_shape=_sds((D, ncb * LANE), BF16),
                          compiler_params=_cp(("parallel",)))(
        jnp.asarray(k1), jnp.asarray(i1), jnp.asarray(k2), jnp.asarray(i2), jnp.asarray(fl), wins, wins)


def _hgroup(nh):
    return _pick(nh, (4, 2, 1))


def _a_specs_q(nh, G):
    ngrp = nh // G
    blk = (GROUP, G * HEAD_DIM)
    q = pl.BlockSpec(blk, lambda i, hg: (i, hg))
    ks = [pl.BlockSpec(blk, functools.partial(
        lambda i, hg, j: (jnp.maximum(i - (WIN_BLOCKS - 1) + j, 0), ngrp + hg), j=j)) for j in range(WIN_BLOCKS)]
    vs = [pl.BlockSpec(blk, functools.partial(
        lambda i, hg, j: (jnp.maximum(i - (WIN_BLOCKS - 1) + j, 0), 2 * ngrp + hg), j=j)) for j in range(WIN_BLOCKS)]
    return q, ks, vs


def _a_logits(q, ks, bias, i, scale):
    parts = [lax.dot_general(q, k, NT, preferred_element_type=F32) for k in ks]
    s = jnp.concatenate(parts, axis=1) * scale + bias
    col = lax.broadcasted_iota(jnp.int32, s.shape, 1)
    return jnp.where(col >= (WIN_BLOCKS - 1 - i) * GROUP, s, NEG_INF)


def _attn_a_fwd(name, qkv, bias2, nh, jobs=()):
    T = qkv.shape[0]
    ng = T // GROUP
    G = _pick(nh, (8, 4, 2, 1))
    scale = HEAD_DIM ** -0.5

    def body(q_ref, *refs):
        k_refs = refs[:WIN_BLOCKS]
        v_refs = refs[WIN_BLOCKS:2 * WIN_BLOCKS]
        bias_ref, o_ref, lse_ref = refs[2 * WIN_BLOCKS:]
        i, hg = pl.program_id(0), pl.program_id(1)

        @pl.when(hg == 0)
        def _():
            lse_ref[...] = jnp.zeros_like(lse_ref)

        for g in range(G):
            h = hg * G + g
            sl = slice(g * HEAD_DIM, (g + 1) * HEAD_DIM)
            s = _a_logits(q_ref[:, sl], [kr[:, sl] for kr in k_refs], bias_ref[h], i, scale)
            m = jnp.max(s, axis=1, keepdims=True)
            p = jnp.exp(s - m)
            l = jnp.sum(p, axis=1, keepdims=True)
            pb = (p / l).astype(BF16)
            o = jnp.zeros((GROUP, HEAD_DIM), F32)
            for j in range(WIN_BLOCKS):
                o = o + jnp.dot(pb[:, j * GROUP:(j + 1) * GROUP], v_refs[j][:, sl], preferred_element_type=F32)
            o_ref[:, sl] = o.astype(BF16)
            _put_col(lse_ref, h, m + jnp.log(l))

    q_spec, k_specs, v_specs = _a_specs_q(nh, G)
    stat = pl.BlockSpec((GROUP, LANE), lambda i, hg: (i, 0))
    return _pcall(
        body, name=name, grid=(ng, nh // G),
        in_specs=[q_spec] + k_specs + v_specs + [pl.BlockSpec((nh, GROUP, WIN), lambda i, hg: (0, 0, 0))],
        out_specs=[pl.BlockSpec((GROUP, G * HEAD_DIM), lambda i, hg: (i, hg)), stat],
        out_shape=[_sds((T, nh * HEAD_DIM), BF16), _sds((T, LANE), F32)],
        sem=("parallel", "arbitrary"), jobs=jobs,
    )(qkv, *([qkv] * (2 * WIN_BLOCKS)), bias2)


def _attn_a_dq(name, qkv, do, lse, bias2, nh, jobs=()):
    T = qkv.shape[0]
    ng = T // GROUP
    G = _pick(nh, (8, 4, 2, 1))
    scale = HEAD_DIM ** -0.5

    def body(q_ref, *refs):
        k_refs = refs[:WIN_BLOCKS]
        v_refs = refs[WIN_BLOCKS:2 * WIN_BLOCKS]
        do_ref, lse_ref, bias_ref, dq_ref, delta_ref, db_ref = refs[2 * WIN_BLOCKS:]
        i, hg = pl.program_id(0), pl.program_id(1)

        @pl.when(hg == 0)
        def _():
            delta_ref[...] = jnp.zeros_like(delta_ref)

        @pl.when(i == 0)
        def _():
            for g in range(G):
                db_ref[hg * G + g] = jnp.zeros((GROUP, WIN), F32)

        for g in range(G):
            h = hg * G + g
            sl = slice(g * HEAD_DIM, (g + 1) * HEAD_DIM)
            ks = [kr[:, sl] for kr in k_refs]
            s = _a_logits(q_ref[:, sl], ks, bias_ref[h], i, scale)
            p = jnp.exp(s - _col_of(lse_ref[...], h))
            dov = do_ref[:, sl]
            dp = jnp.concatenate([lax.dot_general(dov, vr[:, sl], NT, preferred_element_type=F32) for vr in v_refs],
                                 axis=1)
            delta = jnp.sum(p * dp, axis=1, keepdims=True)
            ds = p * (dp - delta)
            db_ref[h] += ds
            dsb = ds.astype(BF16)
            dq = jnp.zeros((GROUP, HEAD_DIM), F32)
            for j in range(WIN_BLOCKS):
                dq = dq + jnp.dot(dsb[:, j * GROUP:(j + 1) * GROUP], ks[j], preferred_element_type=F32)
            dq_ref[:, sl] = (dq * scale).astype(BF16)
            _put_col(delta_ref, h, delta)

    q_spec, k_specs, v_specs = _a_specs_q(nh, G)
    blk = pl.BlockSpec((GROUP, G * HEAD_DIM), lambda i, hg: (i, hg))
    stat = pl.BlockSpec((GROUP, LANE), lambda i, hg: (i, 0))
    full_b = pl.BlockSpec((nh, GROUP, WIN), lambda i, hg: (0, 0, 0))
    return _pcall(
        body, name=name, grid=(ng, nh // G),
        in_specs=[q_spec] + k_specs + v_specs + [blk, stat, full_b],
        out_specs=[blk, stat, full_b],
        out_shape=[_sds((T, nh * HEAD_DIM), BF16), _sds((T, LANE), F32), _sds((nh, GROUP, WIN), F32)],
        sem=("arbitrary", "arbitrary"), jobs=jobs,
    )(qkv, *([qkv] * (2 * WIN_BLOCKS)), do, lse, bias2)


def _attn_a_dkv(name, qkv, do, lse, delta, bias2, nh, jobs=()):
    T = qkv.shape[0]
    ng = T // GROUP
    G = _pick(nh, (8, 4, 2, 1))
    ngrp = nh // G
    scale = HEAD_DIM ** -0.5
    nj = WIN_BLOCKS

    def body(k_ref, v_ref, *refs):
        q_refs = refs[:nj]
        do_refs = refs[nj:2 * nj]
        lse_refs = refs[2 * nj:3 * nj]
        dl_refs = refs[3 * nj:4 * nj]
        bias_ref, dk_ref, dv_ref = refs[4 * nj:]
        r, hg = pl.program_id(0), pl.program_id(1)
        for g in range(G):
            h = hg * G + g
            sl = slice(g * HEAD_DIM, (g + 1) * HEAD_DIM)
            kv, vv = k_ref[:, sl], v_ref[:, sl]
            bias = bias_ref[h]
            dk = jnp.zeros((GROUP, HEAD_DIM), F32)
            dv = jnp.zeros((GROUP, HEAD_DIM), F32)
            for j in range(nj):
                qv, dov = q_refs[j][:, sl], do_refs[j][:, sl]
                c0 = (nj - 1 - j) * GROUP
                s = lax.dot_general(qv, kv, NT, preferred_element_type=F32) * scale + bias[:, c0:c0 + GROUP]
                p = jnp.exp(s - _col_of(lse_refs[j][...], h))
                p = jnp.where(r + j <= ng - 1, p, 0.0)
                dp = lax.dot_general(dov, vv, NT, preferred_element_type=F32)
                ds = p * (dp - _col_of(dl_refs[j][...], h))
                dv = dv + lax.dot_general(p.astype(BF16), dov, TN, preferred_element_type=F32)
                dk = dk + lax.dot_general(ds.astype(BF16), qv, TN, preferred_element_type=F32)
            dk_ref[:, sl] = (dk * scale).astype(BF16)
            dv_ref[:, sl] = dv.astype(BF16)

    def qmap(j):
        return functools.partial(lambda r, hg, j: (jnp.minimum(r + j, ng - 1), hg), j=j)

    def smap(j):
        return functools.partial(lambda r, hg, j: (jnp.minimum(r + j, ng - 1), 0), j=j)

    blk = (GROUP, G * HEAD_DIM)
    in_specs = ([pl.BlockSpec(blk, lambda r, hg: (r, ngrp + hg)), pl.BlockSpec(blk, lambda r, hg: (r, 2 * ngrp + hg))]
                + [pl.BlockSpec(blk, qmap(j)) for j in range(nj)]
                + [pl.BlockSpec(blk, qmap(j)) for j in range(nj)]
                + [pl.BlockSpec((GROUP, LANE), smap(j)) for j in range(nj)]
                + [pl.BlockSpec((GROUP, LANE), smap(j)) for j in range(nj)]
                + [pl.BlockSpec((nh, GROUP, WIN), lambda r, hg: (0, 0, 0))])
    out = pl.BlockSpec(blk, lambda r, hg: (r, hg))
    return _pcall(
        body, name=name, grid=(ng, ngrp), in_specs=in_specs, out_specs=[out, out],
        out_shape=[_sds((T, nh * HEAD_DIM), BF16)] * 2,
        sem=("parallel", "parallel"), jobs=jobs,
    )(qkv, qkv, *([qkv] * nj), *([do] * nj), *([lse] * nj), *([delta] * nj), bias2)


def _fox_prep(name, f, b_f):
    T = f.shape[0]
    tb = _pick(T, (256, 128))

    def body(f_ref, b_ref, cum_ref, cumt_ref, carry_ref):
        @pl.when(pl.program_id(0) == 0)
        def _():
            carry_ref[...] = jnp.zeros_like(carry_ref)

        z = f_ref[...] + b_ref[...]
        logf = jnp.minimum(z, 0.0) - jnp.log(1.0 + jnp.exp(-jnp.abs(z)))
        row = lax.broadcasted_iota(jnp.int32, (tb, tb), 0)
        col = lax.broadcasted_iota(jnp.int32, (tb, tb), 1)
        tri = (row >= col).astype(BF16)
        acc = jnp.zeros((tb, LANE), F32)
        for piece in _split3(logf):
            acc = acc + jnp.dot(tri, piece, preferred_element_type=F32)
        cum = acc + carry_ref[...]
        cum_ref[...] = cum
        cumt_ref[...] = cum.T
        carry_ref[...] = cum_ref[pl.ds(tb - 1, 1), :]

    return pl.pallas_call(
        body, name=name, grid=(T // tb,),
        in_specs=[pl.BlockSpec((tb, LANE), lambda i: (i, 0)), pl.BlockSpec((1, LANE), lambda i: (0, 0))],
        out_specs=[pl.BlockSpec((tb, LANE), lambda i: (i, 0)), pl.BlockSpec((LANE, tb), lambda i: (0, i))],
        out_shape=[_sds((T, LANE), F32), _sds((LANE, T), F32)],
        scratch_shapes=[pltpu.VMEM((1, LANE), F32)],
        compiler_params=_cp(("arbitrary",)),
    )(f, b_f)


def _fox_blk(T):
    return _pick(T, (256, 128))


def _fox_group(nh):
    return _hgroup(nh)


def _fox_allowed(i, j, tq, tk):
    diff = lax.broadcasted_iota(jnp.int32, (tq, tk), 1) - lax.broadcasted_iota(jnp.int32, (tq, tk), 0)
    return diff <= (i - j) * tq


def _fox_fwd(name, qkv, cum, cumt, nh, jobs=()):
    T = qkv.shape[0]
    tq = tk = _fox_blk(T)
    G = _fox_group(nh)
    ngrp = nh // G
    scale = HEAD_DIM ** -0.5

    def body(q_ref, k_ref, v_ref, cum_ref, cumt_ref, o_ref, lse_ref):
        i, hg = pl.program_id(0), pl.program_id(1)

        @pl.when(hg == 0)
        def _():
            lse_ref[...] = jnp.zeros_like(lse_ref)

        sls = [slice(g * HEAD_DIM, (g + 1) * HEAD_DIM) for g in range(G)]
        qs = [q_ref[:, sl] for sl in sls]
        cqs = [_col_of(cum_ref[...], hg * G + g) for g in range(G)]

        def step(j, carry, diagonal=False):
            k0 = pl.multiple_of(j * tk, tk)
            out = []
            for g in range(G):
                m, l, acc = carry[g]
                kj = k_ref[pl.ds(k0, tk), sls[g]]
                vj = v_ref[pl.ds(k0, tk), sls[g]]
                ck = cumt_ref[pl.ds(hg * G + g, 1), pl.ds(k0, tk)]
                s = lax.dot_general(qs[g], kj, NT, preferred_element_type=F32) * scale + (cqs[g] - ck)
                if diagonal:
                    s = jnp.where(_fox_allowed(i, j, tq, tk), s, NEG_INF)
                m_new = jnp.maximum(m, jnp.max(s, axis=1, keepdims=True))
                alpha = jnp.exp(m - m_new)
                p = jnp.exp(s - m_new)
                l = alpha * l + jnp.sum(p, axis=1, keepdims=True)
                acc = alpha * acc + jnp.dot(p.astype(BF16), vj, preferred_element_type=F32)
                out.append((m_new, l, acc))
            return tuple(out)

        one = (jnp.full((tq, 1), NEG_INF, F32), jnp.zeros((tq, 1), F32), jnp.zeros((tq, HEAD_DIM), F32))
        res = step(i, lax.fori_loop(0, i, step, tuple(one for _ in range(G))), diagonal=True)
        for g in range(G):
            m, l, acc = res[g]
            o_ref[:, sls[g]] = (acc / l).astype(BF16)
            _put_col(lse_ref, hg * G + g, m + jnp.log(l))

    GW = G * HEAD_DIM
    return _pcall(
        body, name=name, grid=(T // tq, ngrp),
        in_specs=[pl.BlockSpec((tq, GW), lambda i, hg: (i, 3 * ngrp + hg)),
                  pl.BlockSpec((T, GW), lambda i, hg: (0, 4 * ngrp + hg)),
                  pl.BlockSpec((T, GW), lambda i, hg: (0, 5 * ngrp + hg)),
                  pl.BlockSpec((tq, LANE), lambda i, hg: (i, 0)),
                  pl.BlockSpec((LANE, T), lambda i, hg: (0, 0))],
        out_specs=[pl.BlockSpec((tq, GW), lambda i, hg: (i, hg)), pl.BlockSpec((tq, LANE), lambda i, hg: (i, 0))],
        out_shape=[_sds((T, nh * HEAD_DIM), BF16), _sds((T, LANE), F32)],
        sem=("parallel", "arbitrary"), jobs=jobs,
    )(qkv, qkv, qkv, cum, cumt)


def _fox_dq(name, qkv, do, lse, cum, cumt, nh, jobs=()):
    T = qkv.shape[0]
    tq = tk = _fox_blk(T)
    G = _fox_group(nh)
    ngrp = nh // G
    GW = G * HEAD_DIM
    scale = HEAD_DIM ** -0.5

    def body(q_ref, k_ref, v_ref, do_ref, lse_ref, cum_ref, cumt_ref, dq_ref, delta_ref):
        i, hg = pl.program_id(0), pl.program_id(1)

        @pl.when(hg == 0)
        def _():
            delta_ref[...] = jnp.zeros_like(delta_ref)

        sls = [slice(g * HEAD_DIM, (g + 1) * HEAD_DIM) for g in range(G)]
        qs = [q_ref[:, sl] for sl in sls]
        dos = [do_ref[:, sl] for sl in sls]
        cqs = [_col_of(cum_ref[...], hg * G + g) for g in range(G)]
        lses = [_col_of(lse_ref[...], hg * G + g) for g in range(G)]

        def p_dp(j, g, ok):
            k0 = pl.multiple_of(j * tk, tk)
            kj = k_ref[pl.ds(k0, tk), sls[g]]
            vj = v_ref[pl.ds(k0, tk), sls[g]]
            ck = cumt_ref[pl.ds(hg * G + g, 1), pl.ds(k0, tk)]
            s = lax.dot_general(qs[g], kj, NT, preferred_element_type=F32) * scale + (cqs[g] - ck)
            if ok is not None:
                s = jnp.where(ok, s, NEG_INF)
            p = jnp.exp(s - lses[g])
            return p, lax.dot_general(dos[g], vj, NT, preferred_element_type=F32), kj

        def sweep_delta(j, deltas, diagonal=False):
            ok = _fox_allowed(i, j, tq, tk) if diagonal else None
            out = []
            for g in range(G):
                p, dp, _ = p_dp(j, g, ok)
                out.append(deltas[g] + jnp.sum(p * dp, axis=1, keepdims=True))
            return tuple(out)

        deltas = lax.fori_loop(0, i, sweep_delta, tuple(jnp.zeros((tq, 1), F32) for _ in range(G)))
        deltas = sweep_delta(i, deltas, diagonal=True)

        def sweep_dq(j, dqs, diagonal=False):
            ok = _fox_allowed(i, j, tq, tk) if diagonal else None
            out = []
            for g in range(G):
                p, dp, kj = p_dp(j, g, ok)
                ds = p * (dp - deltas[g])
                out.append(dqs[g] + jnp.dot(ds.astype(BF16), kj, preferred_element_type=F32))
            return tuple(out)

        dqs = lax.fori_loop(0, i, sweep_dq, tuple(jnp.zeros((tq, HEAD_DIM), F32) for _ in range(G)))
        dqs = sweep_dq(i, dqs, diagonal=True)
        for g in range(G):
            dq_ref[:, sls[g]] = (dqs[g] * scale).astype(BF16)
            _put_col(delta_ref, hg * G + g, deltas[g])

    blk = pl.BlockSpec((tq, GW), lambda i, hg: (i, hg))
    stat = pl.BlockSpec((tq, LANE), lambda i, hg: (i, 0))
    return _pcall(
        body, name=name, grid=(T // tq, ngrp),
        in_specs=[pl.BlockSpec((tq, GW), lambda i, hg: (i, 3 * ngrp + hg)),
                  pl.BlockSpec((T, GW), lambda i, hg: (0, 4 * ngrp + hg)),
                  pl.BlockSpec((T, GW), lambda i, hg: (0, 5 * ngrp + hg)),
                  blk, stat, stat, pl.BlockSpec((LANE, T), lambda i, hg: (0, 0))],
        out_specs=[blk, stat],
        out_shape=[_sds((T, nh * HEAD_DIM), BF16), _sds((T, LANE), F32)],
        sem=("parallel", "arbitrary"), jobs=jobs,
    )(qkv, qkv, qkv, do, lse, cum, cumt)


def _fox_dkv(name, qkv, do, lse, delta, cum, cumt, nh, jobs=()):
    T = qkv.shape[0]
    tq = tk = _fox_blk(T)
    nq = T // tq
    G = _fox_group(nh)
    ngrp = nh // G
    GW = G * HEAD_DIM
    scale = HEAD_DIM ** -0.5

    def body(k_ref, v_ref, q_ref, do_ref, lse_ref, dl_ref, cum_ref, cumt_ref, dk_ref, dv_ref, dc_ref):
        j, hg = pl.program_id(0), pl.program_id(1)

        @pl.when(hg == 0)
        def _():
            dc_ref[...] = jnp.zeros_like(dc_ref)

        sls = [slice(g * HEAD_DIM, (g + 1) * HEAD_DIM) for g in range(G)]
        kjs = [k_ref[:, sl] for sl in sls]
        vjs = [v_ref[:, sl] for sl in sls]
        k0 = pl.multiple_of(j * tk, tk)
        cks = [cumt_ref[pl.ds(hg * G + g, 1), pl.ds(k0, tk)] for g in range(G)]

        def step(i, carry, diagonal=False):
            q0 = pl.multiple_of(i * tq, tq)
            cum_i, lse_i, dl_i = cum_ref[pl.ds(q0, tq), :], lse_ref[pl.ds(q0, tq), :], dl_ref[pl.ds(q0, tq), :]
            out = []
            for g in range(G):
                dk, dv, dc = carry[g]
                h = hg * G + g
                qi = q_ref[pl.ds(q0, tq), sls[g]]
                doi = do_ref[pl.ds(q0, tq), sls[g]]
                s = lax.dot_general(qi, kjs[g], NT, preferred_element_type=F32) * scale + (_col_of(cum_i, h) - cks[g])
                if diagonal:
                    s = jnp.where(_fox_allowed(i, j, tq, tk), s, NEG_INF)
                p = jnp.exp(s - _col_of(lse_i, h))
                dp = lax.dot_general(doi, vjs[g], NT, preferred_element_type=F32)
                ds = p * (dp - _col_of(dl_i, h))
                dv = dv + lax.dot_general(p.astype(BF16), doi, TN, preferred_element_type=F32)
                dk = dk + lax.dot_general(ds.astype(BF16), qi, TN, preferred_element_type=F32)
                dc = dc - jnp.sum(ds, axis=0, keepdims=True)
                out.append((dk, dv, dc))
            return tuple(out)

        one = (jnp.zeros((tk, HEAD_DIM), F32), jnp.zeros((tk, HEAD_DIM), F32), jnp.zeros((1, tk), F32))
        res = lax.fori_loop(j + 1, nq, step, step(j, tuple(one for _ in range(G)), diagonal=True))
        sub = lax.broadcasted_iota(jnp.int32, (LANE, tk), 0)
        dc_all = dc_ref[...]
        for g in range(G):
            dk, dv, dc = res[g]
            dk_ref[:, sls[g]] = (dk * scale).astype(BF16)
            dv_ref[:, sls[g]] = dv.astype(BF16)
            dc_all = jnp.where(sub == hg * G + g, dc, dc_all)
        dc_ref[...] = dc_all

    whole = lambda c: pl.BlockSpec((T, GW), c)
    stat = pl.BlockSpec((T, LANE), lambda j, hg: (0, 0))
    out = pl.BlockSpec((tk, GW), lambda j, hg: (j, hg))
    return _pcall(
        body, name=name, grid=(T // tk, ngrp),
        in_specs=[pl.BlockSpec((tk, GW), lambda j, hg: (j, 4 * ngrp + hg)),
                  pl.BlockSpec((tk, GW), lambda j, hg: (j, 5 * ngrp + hg)),
                  whole(lambda j, hg: (0, 3 * ngrp + hg)), whole(lambda j, hg: (0, hg)),
                  stat, stat, stat, pl.BlockSpec((LANE, T), lambda j, hg: (0, 0))],
        out_specs=[out, out, pl.BlockSpec((LANE, tk), lambda j, hg: (0, j))],
        out_shape=[_sds((T, nh * HEAD_DIM), BF16)] * 2 + [_sds((LANE, T), F32)],
        sem=("parallel", "arbitrary"), jobs=jobs,
    )(qkv, qkv, qkv, do, lse, delta, cum, cumt)


def _fox_post(name, dcumt, f, b_f):
    T = f.shape[0]
    tb = _pick(T, (256, 128))
    nb = T // tb

    def body(dc_ref, f_ref, b_ref, df_ref, gb_ref, carry_ref):
        @pl.when(pl.program_id(0) == 0)
        def _():
            carry_ref[...] = jnp.zeros_like(carry_ref)
            gb_ref[...] = jnp.zeros_like(gb_ref)

        dc = dc_ref[...]
        row = lax.broadcasted_iota(jnp.int32, (tb, tb), 0)
        col = lax.broadcasted_iota(jnp.int32, (tb, tb), 1)
        tri = (row >= col).astype(BF16)
        acc = jnp.zeros((LANE, tb), F32)
        for piece in _split3(dc):
            acc = acc + jnp.dot(piece, tri, preferred_element_type=F32)
        dlogf = (acc + carry_ref[...]).T
        carry_ref[...] += jnp.sum(dc, axis=1, keepdims=True)
        z = f_ref[...] + b_ref[...]
        df = dlogf * _sigmoid(-z)
        df_ref[...] = df.astype(BF16)
        gb_ref[...] += jnp.sum(df, axis=0, keepdims=True)

    return pl.pallas_call(
        body, name=name, grid=(nb,),
        in_specs=[pl.BlockSpec((LANE, tb), lambda g: (0, nb - 1 - g)),
                  pl.BlockSpec((tb, LANE), lambda g: (nb - 1 - g, 0)),
                  pl.BlockSpec((1, LANE), lambda g: (0, 0))],
        out_specs=[pl.BlockSpec((tb, LANE), lambda g: (nb - 1 - g, 0)), pl.BlockSpec((1, LANE), lambda g: (0, 0))],
        out_shape=[_sds((T, LANE), BF16), _sds((1, LANE), F32)],
        scratch_shapes=[pltpu.VMEM((LANE, 1), F32)],
        compiler_params=_cp(("arbitrary",)),
    )(dcumt, f, b_f)


def _rel_tables(n_rel):
    max_rel = (n_rel - 1) // 2
    nj = GROUP + WIN - 1
    onehot = np.zeros((n_rel, nj), np.float32)
    for j in range(nj):
        dist = (WIN - 1) - j
        onehot[int(np.clip(dist, -max_rel, max_rel)) + max_rel, j] = 1.0
    a = np.arange(GROUP)[:, None]
    kb = np.arange(WIN)[None, :]
    lo = CHUNK * (a // CHUNK)
    inband = (kb >= lo) & (kb < lo + BAND)
    return onehot, inband


def _bias2_of(rel_bias, onehot, inband):
    bv = jnp.dot(rel_bias, jnp.asarray(onehot), precision=lax.Precision.HIGHEST)
    rows = [bv[:, GROUP - 1 - a:GROUP - 1 - a + WIN] for a in range(GROUP)]
    toe = jnp.stack(rows, axis=1)
    return jnp.where(jnp.asarray(inband)[None], toe, NEG_INF)


def _rel_grad_of(dbias2, onehot):
    nj = GROUP + WIN - 1
    dbv = sum(jnp.pad(dbias2[:, a, :], ((0, 0), (GROUP - 1 - a, nj - WIN - (GROUP - 1 - a)))) for a in range(GROUP))
    return jnp.dot(dbv, jnp.asarray(onehot).T, precision=lax.Precision.HIGHEST)


def kernel(x, g_mix, w_in, b_f, b_gate, rel_bias, w_branch_a, w_branch_b, w_out, g_ffn, w_gate_ffn, w_up_ffn, w_down_ffn, g_final, loss_target, m_g_mix, m_w_in, m_b_f, m_b_gate, m_rel_bias, m_w_branch_a, m_w_branch_b, m_w_out, m_g_ffn, m_w_gate_ffn, m_w_up_ffn, m_w_down_ffn, m_g_final, v_g_mix, v_w_in, v_b_f, v_b_gate, v_rel_bias, v_w_branch_a, v_w_branch_b, v_w_out, v_g_ffn, v_w_gate_ffn, v_w_up_ffn, v_w_down_ffn, v_g_final):
    T, D = x.shape[1], x.shape[2]
    Ls = w_in.shape[2]
    W = w_branch_a.shape[1]
    nh = W // HEAD_DIM
    nhb = b_f.shape[1]
    assert w_branch_b.shape[1] == W and nhb == nh and rel_bias.shape[1] == nh
    W6 = 6 * W
    Fl = w_gate_ffn.shape[2]
    Fp = -(-Fl // LANE) * LANE
    n_rel = rel_bias.shape[2]
    chip = 2 * lax.axis_index("x") + lax.axis_index("y")
    lay, nbw, nmain = _in_layout(D, W6, nhb, Ls)
    onehot, inband = _rel_tables(n_rel)

    xs, tgt = x[0], loss_target[0]

    win_f32 = lax.switch(chip, [functools.partial(_to_window, lay_k=lay[k], nbw=nbw) for k in range(N_CHIPS)], w_in[0])
    pad_c = lambda w: jnp.pad(w, ((0, 0), (0, Fp - Fl)))
    pad_r = lambda w: jnp.pad(w, ((0, Fp - Fl), (0, 0)))
    sh_in = _cast_bf16("cast_w_in", win_f32, chip)
    sh_a = _cast_bf16("cast_w_a", w_branch_a[0], chip)
    sh_b = _cast_bf16("cast_w_b", w_branch_b[0], chip)
    sh_o = _cast_bf16("cast_w_out", w_out[0], chip)
    sh_g = _cast_bf16("cast_w_gate", pad_c(w_gate_ffn[0]), chip)
    sh_u = _cast_bf16("cast_w_up", pad_c(w_up_ffn[0]), chip)
    sh_d = _cast_bf16("cast_w_down", pad_r(w_down_ffn[0]), chip)
    (wins,) = _allgather("ag_w_in", [sh_in])
    wc = _assemble_in("assemble_w_in", wins, lay, nbw, nmain)

    h1, r1 = _rms_fwd("rms1", xs, g_mix)
    qkv, ((wa_g, wb_g, wo_g),) = _mm_nn("proj_qkv", h1, wc, BF16, b_col0=0, n=W6, tm=1024,
                                        jobs=[_job_gather_ici([sh_a, sh_b, sh_o])])
    gates, ((wa_g, wb_g, wo_g), (wg_g,)) = _mm_nn(
        "proj_gates", h1, wc, BF16, b_col0=W6, n=2 * D, tm=1024,
        jobs=[_job_gather_d2d([wa_g, wb_g, wo_g]), _job_gather_ici([sh_g], part=(0, 2))])
    fl = _mm_nn("proj_f", h1, wc, F32, b_col0=nmain, n=LANE, tn=LANE)
    bias2 = _bias2_of(rel_bias[0], onehot, inband)
    bf_pad = jnp.pad(b_f, ((0, 0), (0, LANE - nhb)))
    (o_a, lse_a), ((wg_g,),) = _attn_a_fwd("attn_a_fwd", qkv, bias2, nh, jobs=[_job_gather_ici([wg_g], part=(1, 2))])
    cum, cumt = _fox_prep("fox_prep", fl, bf_pad)
    (o_b, lse_b), ((wu_g,), (wg_g,)) = _fox_fwd("fox_fwd", qkv, cum, cumt, nh,
                                                jobs=[_job_gather_ici([sh_u]), _job_gather_d2d([wg_g])])
    u_a = _mm_nn("branch_a", o_a, wa_g, BF16, tm=1024)
    u_b = _mm_nn("branch_b", o_b, wb_g, BF16, tm=1024)
    merged = _merge_fwd("merge", gates, u_a, u_b, b_gate)
    wo_full = wo_g.reshape(D, D)
    x1, ((wu_g,),) = _mm_nn("out_proj", merged, wo_full, F32, residual=xs, tm=1024, tn=_pick(D, (1024, 512, 256, 128)),
                            jobs=[_job_gather_d2d([wu_g])])
    h2, r2 = _rms_fwd("rms2", x1, g_ffn)

    tm_f = _pick(T, (1024, 512, 256, 128))
    tn_f = _pick(Fp, (1408, 1024, 512, 256, 128))
    tk_f = _pick(D, (1024, 512, 256, 128))
    nps_f = Fp // tn_f

    def swiglu_ep(accs, e_refs, o_refs):
        g, u = accs
        o_refs[0][...] = g.astype(BF16)
        o_refs[1][...] = u.astype(BF16)
        o_refs[2][...] = (g * _sigmoid(g) * u).astype(BF16)

    hid_spec = pl.BlockSpec((tm_f, tn_f), lambda i, j, k: (i, j))
    wcol_spec = pl.BlockSpec((None, tk_f, tn_f), lambda i, j, k: (j // nps_f, k, j % nps_f))
    (gate, up, hidden), ((wd_g,),) = _mm(
        "ffn_up", "nn", [h2], [pl.BlockSpec((tm_f, tk_f), lambda i, j, k: (i, k))], [wg_g, wu_g], [wcol_spec, wcol_spec],
        [(0, 0, 0), (0, 1, 1)], 2, (T // tm_f, N_CHIPS * Fp // tn_f, D // tk_f), tm_f, tn_f,
        [_sds((T, N_CHIPS * Fp), BF16)] * 3, [hid_spec] * 3, swiglu_ep, jobs=[_job_gather_ici([sh_d])])
    ((wd_g,),) = _comm_only("ag_w_down_d2d", [_job_gather_d2d([wd_g])])
    wd_full = wd_g.reshape(N_CHIPS * Fp, D)
    x2 = _mm_nn("ffn_down", hidden, wd_full, F32, residual=x1, tm=1024, tn=_pick(D, (1024, 512, 256, 128)),
                tk=_pick(N_CHIPS * Fp, (1408, 1024, 512, 256, 128)))

    dx2, dx2b, loss_part, gg_final = _final_loss_bwd("final_loss", x2, tgt, g_final.reshape(1, D))

    def swiglu_bwd_ep(accs, e_refs, o_refs):
        dh = accs[0]
        g = e_refs[0][...].astype(F32)
        u = e_refs[1][...].astype(F32)
        sg = _sigmoid(g)
        o_refs[0][...] = (dh * u * (sg * (1.0 + g * (1.0 - sg)))).astype(BF16)
        o_refs[1][...] = (dh * (g * sg)).astype(BF16)

    tk_b = _pick(D, (1024, 512, 256, 128))
    core = lax.axis_index("c")
    (dgate, dup), _ = _mm(
        "ffn_down_bwd", "nt", [dx2b], [pl.BlockSpec((tm_f, tk_b), lambda i, j, k: (i, k))],
        [wd_full], [pl.BlockSpec((tn_f, tk_b), lambda i, j, k: (j, k))], [(0, 0, 0)], 1,
        (T // tm_f, N_CHIPS * Fp // tn_f, D // tk_b), tm_f, tn_f,
        [_sds((T, N_CHIPS * Fp), BF16)] * 2, [hid_spec] * 2, swiglu_bwd_ep,
        extra=[gate, up], extra_specs=[hid_spec, hid_spec])
    dwd = _mm_tn("dw_down", hidden, dx2b, BF16, tm=_pick(N_CHIPS * Fp, (1408, 1024, 512, 256, 128)))
    dwd = dwd.reshape(N_CHIPS, Fp, D)
    dh2, ((sib_d,),) = _mm_nt("ffn_up_bwd", [dgate, dup], [wg_g, wu_g], F32, tm=1024, jobs=[_job_sibling([dwd])])
    tm_w = _pick(D, (2048, 1024, 512, 256, 128))
    dwg = _mm_tn("dw_gate", h2, dgate, BF16, slots=N_CHIPS, tm=tm_w)
    dwu = _mm_tn("dw_up", h2, dup, BF16, slots=N_CHIPS, tm=tm_w)
    dx1, dx1b, gg_ffn = _rms_bwd("rms2_bwd", [dh2], x1, r2, g_ffn, dx2, True)
    part_d = _add_bf16("rs_add_down", dwd, core, sib_d)

    dmerged, ((sib_g, sib_u),) = _mm_nt("out_proj_bwd", [dx1b], [wo_full], BF16, tm=1024,
                                        jobs=[_job_sibling([dwg, dwu])])
    dwo = _mm_tn("dw_out", merged, dx1b, BF16).reshape(N_CHIPS, D // N_CHIPS, D)
    du_a, du_b, dga, dgb, gbg_a, gbg_b = _merge_bwd("merge_bwd", dmerged, gates, u_a, u_b, b_gate)
    part_g = _add_bf16("rs_add_gate", dwg, core, sib_g)
    part_u = _add_bf16("rs_add_up", dwu, core, sib_u)
    do_a = _mm_nt("branch_a_bwd", [du_a], [wa_g], BF16, tm=1024)
    do_b = _mm_nt("branch_b_bwd", [du_b], [wb_g], BF16, tm=1024)
    dwa = _mm_tn("dw_a", o_a, du_a, BF16, slots=N_CHIPS)
    dwb = _mm_tn("dw_b", o_b, du_b, BF16, slots=N_CHIPS)

    (dq_a, delta_a, dbias2), ((got_d,), (sib_a, sib_b, sib_o)) = _attn_a_dq(
        "attn_a_dq", qkv, do_a, lse_a, bias2, nh,
        jobs=[_job_scatter([part_d], part=(0, 2)), _job_sibling([dwa, dwb, dwo])])
    part_a = _add_bf16("rs_add_a", dwa, core, sib_a)
    part_b = _add_bf16("rs_add_b", dwb, core, sib_b)
    part_o = _add_bf16("rs_add_out", dwo, core, sib_o)
    (dk_a, dv_a), ((got_d,), (got_g,)) = _attn_a_dkv(
        "attn_a_dkv", qkv, do_a, lse_a, delta_a, bias2, nh,
        jobs=[_job_scatter([part_d], part=(1, 2), into=[got_d]), _job_scatter([part_g], part=(0, 2))])
    full_d = _sum4("rs_sum_down", got_d, part_d, chip, core)
    (dq_b, delta_b), ((got_g,), (got_u,)) = _fox_dq(
        "fox_dq", qkv, do_b, lse_b, cum, cumt, nh,
        jobs=[_job_scatter([part_g], part=(1, 2), into=[got_g]), _job_scatter([part_u])])
    full_g = _sum4("rs_sum_gate", got_g, part_g, chip, core)
    full_u = _sum4("rs_sum_up", got_u, part_u, chip, core)
    (dk_b, dv_b, dcumt), ((got_a, got_b, got_o),) = _fox_dkv(
        "fox_dkv", qkv, do_b, lse_b, delta_b, cum, cumt, nh, jobs=[_job_scatter([part_a, part_b, part_o])])
    full_a = _sum4("rs_sum_a", got_a, part_a, chip, core)
    full_b = _sum4("rs_sum_b", got_b, part_b, chip, core)
    full_o = _sum4("rs_sum_out", got_o, part_o, chip, core)
    df, gbf = _fox_post("fox_post", dcumt, fl, bf_pad)

    dqkv = jnp.concatenate([dq_a, dk_a, dv_a, dq_b, dk_b, dv_b], axis=1)
    dgates = jnp.concatenate([dga, dgb], axis=1)
    dwc_q, ((g_d, g_g, g_u, g_a, g_b, g_o),) = _mm_tn(
        "dw_in_qkv", h1, dqkv, BF16, tn=_pick(W6, (2048, 1024, 512, 256, 128)),
        jobs=[_job_swap([full_d, full_g, full_u, full_a, full_b, full_o])])
    dwc_g = _mm_tn("dw_in_gates", h1, dgates, BF16, tn=_pick(2 * D, (2048, 1024, 512, 256, 128)))
    dwc_f = _mm_tn("dw_in_f", h1, df, BF16, tn=LANE)
    dwc = jnp.concatenate([dwc_q, dwc_g, dwc_f], axis=1)
    zeros_blk = jnp.zeros((D, LANE), BF16)
    win_parts = []
    for k in range(N_CHIPS):
        cols = [dwc[:, lay[k]["s"] * LANE:lay[k]["e"] * LANE]]
        nb = lay[k]["e"] - lay[k]["s"]
        if lay[k]["f"]:
            cols.append(dwc[:, nmain:nmain + LANE])
            nb += 1
        cols += [zeros_blk] * (nbw - nb)
        win_parts.append(jnp.concatenate(cols, axis=1) if len(cols) > 1 else cols[0])
    dwin = jnp.stack(win_parts, axis=0)
    big = {}

    def adamw(nm, w, g, m, v, jobs=()):
        (d, mn, vn, go), jouts = _adamw(f"adamw_{nm}", w[0], g, m[0], v[0], jobs=jobs)
        big[nm] = (go[None], d[None], mn[None], vn[None])
        return jouts

    ((sib_in,),) = adamw("w_gate_ffn", w_gate_ffn, g_g, m_w_gate_ffn, v_w_gate_ffn, jobs=[_job_sibling([dwin])])
    part_in = _add_bf16("rs_add_in", dwin, core, sib_in)
    dh, ((got_in,),) = _mm_nt("proj_qkv_bwd", [dqkv], [wc], F32, k0_list=[0], tk=_pick(W6, (1024, 512, 256, 128)), tm=1024,
                              jobs=[_job_scatter([part_in], part=(0, 5, 8))])
    dh, ((got_in,),) = _mm_nt("proj_gates_bwd", [dgates], [wc], F32, k0_list=[W6], tm=1024,
                              tk=_pick(math_gcd(W6, 2 * D), (1024, 512, 256, 128)), residual=dh,
                              jobs=[_job_scatter([part_in], part=(5, 8, 8), into=[got_in])])
    full_in = _sum4("rs_sum_in", got_in, part_in, chip, core)
    dh, ((g_win,),) = _mm_nt("proj_f_bwd", [df], [wc], F32, k0_list=[nmain], tk=LANE, residual=dh,
                             jobs=[_job_swap([full_in])])
    grad_x, gg_mix = _rms_bwd("rms1_bwd", [dh], xs, r1, g_mix, dx1, False)
    g_in = lax.switch(chip, [functools.partial(_from_window, lay_k=lay[k]) for k in range(N_CHIPS)], g_win)

    for nm, w, g, m, v in (("w_in", w_in, g_in, m_w_in, v_w_in), ("w_branch_a", w_branch_a, g_a, m_w_branch_a, v_w_branch_a),
                           ("w_branch_b", w_branch_b, g_b, m_w_branch_b, v_w_branch_b), ("w_out", w_out, g_o, m_w_out, v_w_out),
                           ("w_up_ffn", w_up_ffn, g_u, m_w_up_ffn, v_w_up_ffn),
                           ("w_down_ffn", w_down_ffn, g_d, m_w_down_ffn, v_w_down_ffn)):
        adamw(nm, w, g, m, v)

    g_rel = _rel_grad_of(dbias2, onehot)
    small = [("loss", loss_part[:, :1], None, None, None),
             ("g_mix", gg_mix, g_mix, m_g_mix, v_g_mix), ("b_f", gbf[:, :nhb], b_f, m_b_f, v_b_f),
             ("b_gate", jnp.concatenate([gbg_a, gbg_b], axis=1), b_gate, m_b_gate, v_b_gate),
             ("rel_bias", g_rel, rel_bias, m_rel_bias, v_rel_bias), ("g_ffn", gg_ffn, g_ffn, m_g_ffn, v_g_ffn),
             ("g_final", gg_final, g_final, m_g_final, v_g_final)]
    sizes = [int(np.prod(s[1].shape)) for s in small]
    total = sum(sizes)
    npad = -(-total // 1024) * 1024

    def pack(arrs):
        flat = jnp.concatenate([a.reshape(-1).astype(F32) for a in arrs])
        return jnp.pad(flat, (0, npad - total)).reshape(8, npad // 8)

    zero1 = jnp.zeros((1,), F32)
    g_all = _small_allreduce("small_allreduce", pack([s[1] for s in small]))
    w_s = pack([zero1 if s[2] is None else s[2] for s in small])
    m_s = pack([zero1 if s[3] is None else s[3] for s in small])
    v_s = pack([zero1 + 1.0 if s[4] is None else s[4] for s in small])
    (d_s, mn_s, vn_s, _), _ = _adamw("adamw_small", w_s, g_all, m_s, v_s)

    def unpack(packed):
        flat = packed.reshape(-1)
        out, pos = {}, 0
        for s, n in zip(small, sizes):
            if s[2] is not None:
                out[s[0]] = flat[pos:pos + n].reshape(s[2].shape)
            else:
                out[s[0]] = flat[pos:pos + n].reshape(())
            pos += n
        return out

    gs, ds, ms, vs = unpack(g_all), unpack(d_s), unpack(mn_s), unpack(vn_s)
    order = ["g_mix", "w_in", "b_f", "b_gate", "rel_bias", "w_branch_a", "w_branch_b", "w_out", "g_ffn",
             "w_gate_ffn", "w_up_ffn", "w_down_ffn", "g_final"]
    res = [[], [], [], []]
    for nm in order:
        four = big[nm] if nm in big else (gs[nm], ds[nm], ms[nm], vs[nm])
        for q in range(4):
            res[q].append(four[q])
    return (gs["loss"], grad_x[None], *res[0], *res[1], *res[2], *res[3])


def math_gcd(a, b):
    while b:
        a, b = b, a % b
    return a
```

```python
import functools

import numpy as np
import jax
import jax.numpy as jnp
from jax import lax
from jax.experimental import pallas as pl
from jax.experimental.pallas import tpu as pltpu

F32 = jnp.float32
BF16 = jnp.bfloat16
LANE = 128
HEAD_DIM = 128
CHUNK = 64
LEFT_CHUNKS = 8
GROUP = 128
WIN_BLOCKS = 5
WIN = WIN_BLOCKS * GROUP
BAND = (LEFT_CHUNKS + 1) * CHUNK
RMS_EPS = 1e-6
NEG_INF = -1e30
ADAM_LR = 0.001
ADAM_B1 = 0.9
ADAM_B2 = 0.999
ADAM_EPS = 1e-08
ADAM_WD = 0.01
ADAM_STEP = 10
N_CHIPS = 4
MESH = pl.DeviceIdType.MESH
VMEM_LIMIT = 52 * 1024 * 1024
ANY = pl.BlockSpec(memory_space=pl.ANY)

NN = (((1,), (0,)), ((), ()))
NT = (((1,), (1,)), ((), ()))
TN = (((0,), (0,)), ((), ()))


def _cp(sem):
    return pltpu.CompilerParams(dimension_semantics=sem, vmem_limit_bytes=VMEM_LIMIT)


def _sds(shape, dtype):
    return jax.ShapeDtypeStruct(shape, dtype)


def _pick(n, prefs):
    for p in prefs:
        if n % p == 0:
            return p
    return n


def _sigmoid(v):
    return 1.0 / (1.0 + jnp.exp(-v))


def _split3(v):
    hi = v.astype(BF16)
    r1 = v - hi.astype(F32)
    mid = r1.astype(BF16)
    lo = (r1 - mid.astype(F32)).astype(BF16)
    return hi, mid, lo


def _col_of(blk, h):
    lane = lax.broadcasted_iota(jnp.int32, blk.shape, 1)
    return jnp.sum(jnp.where(lane == h, blk, 0.0), axis=1, keepdims=True)


def _put_col(ref, h, col):
    lane = lax.broadcasted_iota(jnp.int32, ref.shape, 1)
    ref[...] = jnp.where(lane == h, col, ref[...])


def _mm(name, mode, a_list, a_specs, b_list, b_specs, pairs, n_acc, grid, tm, tn,
        out_shapes, out_specs, epilogue, extra=(), extra_specs=(), jobs=()):
    n_a, n_b, n_e, n_o = len(a_list), len(b_list), len(extra), len(out_shapes)
    nk = grid[2]
    dn = {"nn": NN, "nt": NT, "tn": TN}[mode]

    def body(*refs):
        a_refs = refs[:n_a]
        b_refs = refs[n_a:n_a + n_b]
        e_refs = refs[n_a + n_b:n_a + n_b + n_e]
        o_refs = refs[n_a + n_b + n_e:n_a + n_b + n_e + n_o]
        acc_refs = refs[n_a + n_b + n_e + n_o:]
        k = pl.program_id(2)

        @pl.when(k == 0)
        def _():
            for acc in acc_refs:
                acc[...] = jnp.zeros_like(acc)

        for ai, bi, ci in pairs:
            acc_refs[ci][...] += lax.dot_general(a_refs[ai][...], b_refs[bi][...], dn,
                                                 preferred_element_type=F32)

        @pl.when(k == nk - 1)
        def _():
            epilogue([acc[...] for acc in acc_refs], e_refs, o_refs)

    return _pcall(
        body, name=name, grid=grid,
        in_specs=list(a_specs) + list(b_specs) + list(extra_specs),
        out_specs=list(out_specs), out_shape=list(out_shapes),
        scratch_shapes=[pltpu.VMEM((tm, tn), F32) for _ in range(n_acc)],
        sem=("parallel", "parallel", "arbitrary"), jobs=jobs,
    )(*a_list, *b_list, *extra)


def _one(res, jobs):
    outs, jouts = res
    return (outs[0], jouts) if jobs else outs[0]


def _store(dtype):
    def ep(accs, e_refs, o_refs):
        o_refs[0][...] = accs[0].astype(dtype)
    return ep


def _mm_nn(name, a, b, out_dtype, *, b_col0=0, n=None, tm=512, tn=None, tk=None, residual=None, jobs=()):
    M, K = a.shape
    if b.ndim == 3:
        Ns = b.shape[2]
        n = b.shape[0] * Ns
        tn = tn or _pick(Ns, (1408, 1024, 512, 256, 128))
        nps = Ns // tn
        b_spec = pl.BlockSpec((None, tk or _pick(K, (1024, 512, 256, 128)), tn),
                              lambda i, j, k: (j // nps, k, j % nps))
    else:
        n = n or b.shape[1]
        tn = tn or _pick(math_gcd(n, b_col0) if b_col0 else n, (2048, 1024, 512, 256, 128))
        assert b_col0 % tn == 0 and n % tn == 0
        c0 = b_col0 // tn
        b_spec = pl.BlockSpec((tk or _pick(K, (1024, 512, 256, 128)), tn), lambda i, j, k: (k, c0 + j))
    tk = tk or _pick(K, (1024, 512, 256, 128))
    tm = _pick(M, (tm, 256, 128))
    grid = (M // tm, n // tn, K // tk)
    a_spec = pl.BlockSpec((tm, tk), lambda i, j, k: (i, k))
    o_spec = pl.BlockSpec((tm, tn), lambda i, j, k: (i, j))
    if residual is None:
        return _one(_mm(name, "nn", [a], [a_spec], [b], [b_spec], [(0, 0, 0)], 1, grid, tm, tn,
                        [_sds((M, n), out_dtype)], [o_spec], _store(out_dtype), jobs=jobs), jobs)

    def ep(accs, e_refs, o_refs):
        o_refs[0][...] = (e_refs[0][...] + accs[0]).astype(out_dtype)
    return _one(_mm(name, "nn", [a], [a_spec], [b], [b_spec], [(0, 0, 0)], 1, grid, tm, tn,
                    [_sds((M, n), out_dtype)], [o_spec], ep, extra=[residual], extra_specs=[o_spec], jobs=jobs), jobs)


def _mm_nt(name, a_list, b_list, out_dtype, *, k0_list=None, tm=512, tn=None, tk=None, residual=None, jobs=()):
    M, K = a_list[0].shape
    b0 = b_list[0]
    N = b0.shape[1] if b0.ndim == 3 else b0.shape[0]
    tm = _pick(M, (tm, 256, 128))
    tn = tn or _pick(N, (1024, 512, 256, 128))
    if b0.ndim == 3:
        Ks = b0.shape[2]
        tk = tk or _pick(Ks, (1408, 1024, 512, 256, 128))
        kps = Ks // tk
        b_specs = [pl.BlockSpec((None, tn, tk), lambda i, j, k: (k // kps, j, k % kps)) for _ in b_list]
    else:
        tk = tk or _pick(K, (1024, 896, 512, 256, 128))
        k0_list = k0_list or [0] * len(b_list)
        b_specs = []
        for k0 in k0_list:
            assert k0 % tk == 0
            b_specs.append(pl.BlockSpec((tn, tk), functools.partial(lambda i, j, k, c: (j, c + k), c=k0 // tk)))
    grid = (M // tm, N // tn, K // tk)
    a_specs = [pl.BlockSpec((tm, tk), lambda i, j, k: (i, k)) for _ in a_list]
    o_spec = pl.BlockSpec((tm, tn), lambda i, j, k: (i, j))
    pairs = [(p, p, 0) for p in range(len(a_list))]
    if residual is None:
        return _one(_mm(name, "nt", a_list, a_specs, b_list, b_specs, pairs, 1, grid, tm, tn,
                        [_sds((M, N), out_dtype)], [o_spec], _store(out_dtype), jobs=jobs), jobs)

    def ep(accs, e_refs, o_refs):
        o_refs[0][...] = (e_refs[0][...] + accs[0]).astype(out_dtype)
    return _one(_mm(name, "nt", a_list, a_specs, b_list, b_specs, pairs, 1, grid, tm, tn,
                    [_sds((M, N), out_dtype)], [o_spec], ep, extra=[residual], extra_specs=[o_spec], jobs=jobs), jobs)


def _mm_tn(name, a, b, out_dtype, *, slots=None, tm=None, tn=None, tk=1024, jobs=()):
    Kc, Mo = a.shape
    No = b.shape[1]
    tm = tm or _pick(Mo, (1024, 704, 512, 256, 128))
    tk = _pick(Kc, (tk, 256, 128))
    if slots:
        Ns = No // slots
        tn = tn or _pick(Ns, (1408, 1024, 512, 256, 128))
        nps = Ns // tn
        o_spec = pl.BlockSpec((None, tm, tn), lambda i, j, k: (j // nps, i, j % nps))
        o_shape = _sds((slots, Mo, Ns), out_dtype)
    else:
        tn = tn or _pick(No, (1024, 512, 256, 128))
        o_spec = pl.BlockSpec((tm, tn), lambda i, j, k: (i, j))
        o_shape = _sds((Mo, No), out_dtype)
    grid = (Mo // tm, No // tn, Kc // tk)
    a_spec = pl.BlockSpec((tk, tm), lambda i, j, k: (k, i))
    b_spec = pl.BlockSpec((tk, tn), lambda i, j, k: (k, j))
    return _one(_mm(name, "tn", [a], [a_spec], [b], [b_spec], [(0, 0, 0)], 1, grid, tm, tn,
                    [o_shape], [o_spec], _store(out_dtype), jobs=jobs), jobs)


def _cast_bf16(name, w, chip):
    R, C = w.shape
    tr = _pick(R, (256, 128, 64, 32, 16))

    def body(k_ref, w_ref, o_ref):
        o_ref[...] = w_ref[...].astype(BF16)

    gs = pltpu.PrefetchScalarGridSpec(
        num_scalar_prefetch=1, grid=(R // tr,),
        in_specs=[pl.BlockSpec((tr, C), lambda i, k: (i, 0))],
        out_specs=pl.BlockSpec((None, tr, C), lambda i, k: (k[0], i, 0)))
    return pl.pallas_call(body, name=name, grid_spec=gs, out_shape=_sds((N_CHIPS, R, C), BF16),
                          compiler_params=_cp(("parallel",)))(jnp.reshape(chip, (1,)).astype(jnp.int32), w)


def _rms_fwd(name, x, g):
    T, D = x.shape
    tr = _pick(T, (256, 128))

    def body(x_ref, g_ref, h_ref, r_ref):
        xv = x_ref[...]
        r = lax.rsqrt(jnp.mean(xv * xv, axis=1, keepdims=True) + RMS_EPS)
        h_ref[...] = (xv * r * g_ref[...]).astype(BF16)
        r_ref[...] = r

    row = pl.BlockSpec((tr, D), lambda i: (i, 0))
    return pl.pallas_call(
        body, name=name, grid=(T // tr,),
        in_specs=[row, pl.BlockSpec((1, D), lambda i: (0, 0))],
        out_specs=[row, pl.BlockSpec((tr, 1), lambda i: (i, 0))],
        out_shape=[_sds((T, D), BF16), _sds((T, 1), F32)], compiler_params=_cp(("parallel",)),
    )(x, g)


def _final_loss_bwd(name, x2, tgt, g):
    T, D = x2.shape
    tr = _pick(T, (256, 128))

    def body(x_ref, t_ref, g_ref, dx_ref, dxb_ref, loss_ref, gg_ref):
        @pl.when(pl.program_id(0) == 0)
        def _():
            loss_ref[...] = jnp.zeros_like(loss_ref)
            gg_ref[...] = jnp.zeros_like(gg_ref)

        xv = x_ref[...]
        gv = g_ref[...]
        r = lax.rsqrt(jnp.mean(xv * xv, axis=1, keepdims=True) + RMS_EPS)
        n = xv * r
        e = n * gv - t_ref[...]
        loss_ref[...] += 0.5 * jnp.sum(jnp.mean(e * e, axis=1, keepdims=True), axis=0, keepdims=True)
        dy = e * (1.0 / D)
        gg_ref[...] += jnp.sum(dy * n, axis=0, keepdims=True)
        gy = dy * gv
        dx = r * (gy - n * jnp.mean(gy * n, axis=1, keepdims=True))
        dx_ref[...] = dx
        dxb_ref[...] = dx.astype(BF16)

    row = pl.BlockSpec((tr, D), lambda i: (i, 0))
    vec = pl.BlockSpec((1, D), lambda i: (0, 0))
    return pl.pallas_call(
        body, name=name, grid=(T // tr,),
        in_specs=[row, row, vec],
        out_specs=[row, row, pl.BlockSpec((1, LANE), lambda i: (0, 0)), vec],
        out_shape=[_sds((T, D), F32), _sds((T, D), BF16), _sds((1, LANE), F32), _sds((1, D), F32)],
        compiler_params=_cp(("arbitrary",)),
    )(x2, tgt, g)


def _rms_bwd(name, dh_list, x, r, g, dres, want_bf16):
    T, D = x.shape
    tr = _pick(T, (128,))
    n_dh = len(dh_list)

    def body(*refs):
        dh_refs = refs[:n_dh]
        x_ref, r_ref, g_ref, dres_ref = refs[n_dh:n_dh + 4]
        outs = refs[n_dh + 4:]
        gg_ref = outs[-1]

        @pl.when(pl.program_id(0) == 0)
        def _():
            gg_ref[...] = jnp.zeros_like(gg_ref)

        dh = dh_refs[0][...]
        for ref in dh_refs[1:]:
            dh = dh + ref[...]
        rv = r_ref[...]
        n = x_ref[...] * rv
        gg_ref[...] += jnp.sum(dh * n, axis=0, keepdims=True)
        gy = dh * g_ref[...]
        dx = dres_ref[...] + rv * (gy - n * jnp.mean(gy * n, axis=1, keepdims=True))
        outs[0][...] = dx
        if want_bf16:
            outs[1][...] = dx.astype(BF16)

    row = pl.BlockSpec((tr, D), lambda i: (i, 0))
    vec = pl.BlockSpec((1, D), lambda i: (0, 0))
    out_specs = [row] + ([row] if want_bf16 else []) + [vec]
    out_shape = [_sds((T, D), F32)] + ([_sds((T, D), BF16)] if want_bf16 else []) + [_sds((1, D), F32)]
    return pl.pallas_call(
        body, name=name, grid=(T // tr,),
        in_specs=[row] * n_dh + [row, pl.BlockSpec((tr, 1), lambda i: (i, 0)), vec, row],
        out_specs=out_specs, out_shape=out_shape, compiler_params=_cp(("arbitrary",)),
    )(*dh_list, x, r, g, dres)


def _merge_fwd(name, gates, u_a, u_b, b_gate):
    T, D = u_a.shape
    tr = _pick(T, (256, 128))

    def body(ga_ref, gb_ref, ua_ref, ub_ref, ba_ref, bb_ref, o_ref):
        sa = _sigmoid(ga_ref[...].astype(F32) + ba_ref[...])
        sb = _sigmoid(gb_ref[...].astype(F32) + bb_ref[...])
        o_ref[...] = (sa * ua_ref[...].astype(F32) + sb * ub_ref[...].astype(F32)).astype(BF16)

    row = pl.BlockSpec((tr, D), lambda i: (i, 0))
    row1 = pl.BlockSpec((tr, D), lambda i: (i, 1))
    v0 = pl.BlockSpec((1, D), lambda i: (0, 0))
    v1 = pl.BlockSpec((1, D), lambda i: (0, 1))
    return pl.pallas_call(
        body, name=name, grid=(T // tr,),
        in_specs=[row, row1, row, row, v0, v1], out_specs=row,
        out_shape=_sds((T, D), BF16), compiler_params=_cp(("parallel",)),
    )(gates, gates, u_a, u_b, b_gate, b_gate)


def _merge_bwd(name, dm, gates, u_a, u_b, b_gate):
    T, D = u_a.shape
    tr = _pick(T, (128,))

    def body(dm_ref, ga_ref, gb_ref, ua_ref, ub_ref, ba_ref, bb_ref, dua_ref, dub_ref, dga_ref, dgb_ref,
             gba_ref, gbb_ref):
        @pl.when(pl.program_id(0) == 0)
        def _():
            gba_ref[...] = jnp.zeros_like(gba_ref)
            gbb_ref[...] = jnp.zeros_like(gbb_ref)

        d = dm_ref[...].astype(F32)
        sa = _sigmoid(ga_ref[...].astype(F32) + ba_ref[...])
        sb = _sigmoid(gb_ref[...].astype(F32) + bb_ref[...])
        dua_ref[...] = (d * sa).astype(BF16)
        dub_ref[...] = (d * sb).astype(BF16)
        dga = d * ua_ref[...].astype(F32) * sa * (1.0 - sa)
        dgb = d * ub_ref[...].astype(F32) * sb * (1.0 - sb)
        dga_ref[...] = dga.astype(BF16)
        dgb_ref[...] = dgb.astype(BF16)
        gba_ref[...] += jnp.sum(dga, axis=0, keepdims=True)
        gbb_ref[...] += jnp.sum(dgb, axis=0, keepdims=True)

    row = pl.BlockSpec((tr, D), lambda i: (i, 0))
    row1 = pl.BlockSpec((tr, D), lambda i: (i, 1))
    v0 = pl.BlockSpec((1, D), lambda i: (0, 0))
    v1 = pl.BlockSpec((1, D), lambda i: (0, 1))
    outs = pl.pallas_call(
        body, name=name, grid=(T // tr,),
        in_specs=[row, row, row1, row, row, v0, v1],
        out_specs=[row, row, row, row, v0, v0],
        out_shape=[_sds((T, D), BF16), _sds((T, D), BF16), _sds((T, D), BF16), _sds((T, D), BF16),
                   _sds((1, D), F32), _sds((1, D), F32)],
        compiler_params=_cp(("arbitrary",)),
    )(dm, gates, gates, u_a, u_b, b_gate, b_gate)
    return outs


def _adamw(name, w, g, m, v, jobs=()):
    R, C = w.shape
    Cg = g.shape[1]
    tr = _pick(R, (64, 32, 16, 8))
    c1 = 1.0 - ADAM_B1 ** ADAM_STEP
    c2 = 1.0 - ADAM_B2 ** ADAM_STEP

    def body(w_ref, g_ref, m_ref, v_ref, d_ref, mo_ref, vo_ref, go_ref):
        gv = g_ref[...] if Cg == C else g_ref[:, :C]
        mn = ADAM_B1 * m_ref[...] + (1.0 - ADAM_B1) * gv
        vn = ADAM_B2 * v_ref[...] + (1.0 - ADAM_B2) * (gv * gv)
        d_ref[...] = -ADAM_LR * ((mn / c1) / (jnp.sqrt(vn / c2) + ADAM_EPS) + ADAM_WD * w_ref[...])
        mo_ref[...] = mn
        vo_ref[...] = vn
        go_ref[...] = gv

    blk = pl.BlockSpec((tr, C), lambda i: (i, 0))
    gblk = pl.BlockSpec((tr, Cg), lambda i: (i, 0))
    return _pcall(
        body, name=name, grid=(R // tr,),
        in_specs=[blk, gblk, blk, blk], out_specs=[blk] * 4,
        out_shape=[_sds((R, C), F32)] * 4, sem=("parallel",), jobs=jobs,
    )(w, g, m, v)


def _add_bf16(name, a, a_row0, b):
    S, h, C = b.shape
    tr = _pick(h, (256, 128, 64, 32, 16))
    nb = h // tr

    def body(off_ref, a_ref, b_ref, o_ref):
        o_ref[...] = (a_ref[...].astype(F32) + b_ref[...].astype(F32)).astype(BF16)

    gs = pltpu.PrefetchScalarGridSpec(
        num_scalar_prefetch=1, grid=(S, nb),
        in_specs=[pl.BlockSpec((None, tr, C), lambda s, i, off: (s, off[0] * nb + i, 0)),
                  pl.BlockSpec((None, tr, C), lambda s, i, off: (s, i, 0))],
        out_specs=pl.BlockSpec((None, tr, C), lambda s, i, off: (s, i, 0)))
    return pl.pallas_call(body, name=name, grid_spec=gs, out_shape=_sds((S, h, C), BF16),
                          compiler_params=_cp(("parallel", "parallel")))(
        jnp.reshape(a_row0, (1,)).astype(jnp.int32), a, b)


def _sum4(name, got, mine, chip, core):
    S, h, C = got.shape
    tr = _pick(h, (256, 128, 64, 32, 16))
    nb = h // tr

    def body(chip_ref, core_ref, m_ref, g_ref, o_ref):
        acc = m_ref[...].astype(F32)
        for s in range(S):
            acc = acc + g_ref[s].astype(F32)
        o_ref[...] = acc

    gs = pltpu.PrefetchScalarGridSpec(
        num_scalar_prefetch=2, grid=(nb,),
        in_specs=[pl.BlockSpec((None, tr, C), lambda i, kc, cc: (kc[0], i, 0)),
                  pl.BlockSpec((S, tr, C), lambda i, kc, cc: (0, i, 0))],
        out_specs=pl.BlockSpec((tr, C), lambda i, kc, cc: (cc[0] * nb + i, 0)))
    return pl.pallas_call(body, name=name, grid_spec=gs, out_shape=_sds((2 * h, C), F32),
                          compiler_params=_cp(("parallel",)))(
        jnp.reshape(chip, (1,)).astype(jnp.int32), jnp.reshape(core, (1,)).astype(jnp.int32), mine, got)


def _place():
    x, y, c = lax.axis_index("x"), lax.axis_index("y"), lax.axis_index("c")
    chips = [(1 - x, y), (x, 1 - y), (1 - x, 1 - y)]
    return x, y, c, chips


def _allgather(name, shards):
    n = len(shards)
    NS = 7

    def body(*refs):
        out_refs = refs[n:2 * n]
        ss, rs = refs[2 * n:]
        x, y, c, _ = _place()
        k, kx, ky, kd = 2 * x + y, 2 * (1 - x) + y, 2 * x + (1 - y), 2 * (1 - x) + (1 - y)
        across_x, across_y, sibling = (1 - x, y, c), (x, 1 - y, c), (x, y, 1 - c)
        sends = []

        def go(cp):
            cp.start()
            sends.append(cp)

        for a, out in enumerate(out_refs):
            h = out.shape[1] // 2
            q = h // 2
            half = lambda slot, cc=c: out.at[slot, pl.ds(cc * h, h), :]
            part0 = lambda slot: out.at[slot, pl.ds(c * h, q), :]
            part1 = lambda slot: out.at[slot, pl.ds(c * h + q, q), :]
            b = NS * a
            go(_rdma(half(k), half(k), ss, rs, b + 0, across_x))
            go(_rdma(half(k), half(k), ss, rs, b + 1, across_y))
            _rdma(half(kx), half(kx), ss, rs, b + 0, across_x).wait_recv()
            go(_rdma(part0(kx), part0(kx), ss, rs, b + 2, across_y))
            go(_rdma(half(kx), half(kx), ss, rs, b + 4, sibling))
            _rdma(half(ky), half(ky), ss, rs, b + 1, across_y).wait_recv()
            go(_rdma(part1(ky), part1(ky), ss, rs, b + 3, across_x))
            go(_rdma(half(ky), half(ky), ss, rs, b + 5, sibling))
            _rdma(part0(kd), part0(kd), ss, rs, b + 2, across_y).wait_recv()
            _rdma(part1(kd), part1(kd), ss, rs, b + 3, across_x).wait_recv()
            go(_rdma(half(kd), half(kd), ss, rs, b + 6, sibling))
        for a, out in enumerate(out_refs):
            h = out.shape[1] // 2
            for j, slot in enumerate((kx, ky, kd)):
                rows = out.at[slot, pl.ds((1 - c) * h, h), :]
                _rdma(rows, rows, ss, rs, NS * a + 4 + j, sibling).wait_recv()
        for cp in sends:
            cp.wait_send()

    return pl.pallas_call(
        body, name=name,
        in_specs=[ANY] * n, out_specs=[ANY] * n,
        out_shape=[_sds(s.shape, s.dtype) for s in shards],
        input_output_aliases={a: a for a in range(n)},
        scratch_shapes=[pltpu.SemaphoreType.DMA((NS * n,)), pltpu.SemaphoreType.DMA((NS * n,))],
    )(*shards)


class _Job:
    def __init__(self, ins, out_shapes, aliases, n_sems, start, finish):
        self.ins, self.out_shapes, self.aliases, self.n_sems = list(ins), list(out_shapes), dict(aliases), n_sems
        self.start, self.finish = start, finish


def _rdma(src, dst, ss, rs, idx, to):
    return pltpu.make_async_remote_copy(src_ref=src, dst_ref=dst, send_sem=ss.at[idx], recv_sem=rs.at[idx],
                                        device_id=to, device_id_type=MESH)


def _job_gather_ici(bufs, part=(0, 1)):
    pi, pn = part

    def descs(outs, ss, rs, incoming):
        x, y, c, chips = _place()
        res = []
        for a, out in enumerate(outs):
            h = out.shape[1] // 2
            hp = h // pn
            for j, (cx, cy) in enumerate(chips):
                rows = out.at[(2 * cx + cy) if incoming else (2 * x + y), pl.ds(c * h + pi * hp, hp), :]
                res.append(_rdma(rows, rows, ss, rs, 3 * a + j, (cx, cy, c)))
        return res

    def start(ins, outs, ss, rs):
        for d in descs(outs, ss, rs, False):
            d.start()

    def finish(ins, outs, ss, rs):
        for d in descs(outs, ss, rs, True):
            d.wait_recv()
        for d in descs(outs, ss, rs, False):
            d.wait_send()

    return _Job(bufs, [_sds(b.shape, b.dtype) for b in bufs], {a: a for a in range(len(bufs))}, 3 * len(bufs),
                start, finish)


def _job_gather_d2d(bufs):
    def descs(outs, ss, rs, incoming):
        x, y, c, chips = _place()
        res = []
        for a, out in enumerate(outs):
            h = out.shape[1] // 2
            for j, (cx, cy) in enumerate(chips):
                rows = out.at[2 * cx + cy, pl.ds(((1 - c) if incoming else c) * h, h), :]
                res.append(_rdma(rows, rows, ss, rs, 3 * a + j, (x, y, 1 - c)))
        return res

    def start(ins, outs, ss, rs):
        for d in descs(outs, ss, rs, False):
            d.start()

    def finish(ins, outs, ss, rs):
        for d in descs(outs, ss, rs, True):
            d.wait_recv()
        for d in descs(outs, ss, rs, False):
            d.wait_send()

    return _Job(bufs, [_sds(b.shape, b.dtype) for b in bufs], {a: a for a in range(len(bufs))}, 3 * len(bufs),
                start, finish)


def _job_sibling(grads):
    def descs(ins, outs, ss, rs):
        x, y, c, _ = _place()
        res = []
        for a, (g, out) in enumerate(zip(ins, outs)):
            h = g.shape[1] // 2
            res.append(_rdma(g.at[:, pl.ds((1 - c) * h, h), :], out, ss, rs, a, (x, y, 1 - c)))
        return res

    def start(ins, outs, ss, rs):
        for d in descs(ins, outs, ss, rs):
            d.start()

    def finish(ins, outs, ss, rs):
        for d in descs(ins, outs, ss, rs):
            d.wait()

    return _Job(grads, [_sds((g.shape[0], g.shape[1] // 2, g.shape[2]), g.dtype) for g in grads], {}, len(grads),
                start, finish)


def _job_scatter(parts, part=(0, 1), into=None):
    p0, p1, pn = part if len(part) == 3 else (part[0], part[0] + 1, part[1])
    n = len(parts)

    def descs(ins, outs, ss, rs):
        x, y, c, chips = _place()
        res = []
        for a, (p, out) in enumerate(zip(ins[:n], outs)):
            hp = p.shape[1] // pn
            rows = pl.ds(p0 * hp, (p1 - p0) * hp)
            for j, (cx, cy) in enumerate(chips):
                res.append(_rdma(p.at[2 * cx + cy, rows, :], out.at[j, rows, :], ss, rs, 3 * a + j, (cx, cy, c)))
        return res

    def start(ins, outs, ss, rs):
        for d in descs(ins, outs, ss, rs):
            d.start()

    def finish(ins, outs, ss, rs):
        for d in descs(ins, outs, ss, rs):
            d.wait()

    shapes = [_sds((3,) + p.shape[1:], p.dtype) for p in parts]
    if into is None:
        return _Job(parts, shapes, {}, 3 * n, start, finish)
    return _Job(list(parts) + list(into), shapes, {n + a: a for a in range(n)}, 3 * n, start, finish)


def _job_swap(fulls):
    def descs(outs, ss, rs, incoming):
        x, y, c, _ = _place()
        res = []
        for a, out in enumerate(outs):
            h = out.shape[0] // 2
            rows = out.at[pl.ds(((1 - c) if incoming else c) * h, h), :]
            res.append(_rdma(rows, rows, ss, rs, a, (x, y, 1 - c)))
        return res

    def start(ins, outs, ss, rs):
        for d in descs(outs, ss, rs, False):
            d.start()

    def finish(ins, outs, ss, rs):
        for d in descs(outs, ss, rs, True):
            d.wait_recv()
        for d in descs(outs, ss, rs, False):
            d.wait_send()

    return _Job(fulls, [_sds(f.shape, f.dtype) for f in fulls], {a: a for a in range(len(fulls))}, len(fulls),
                start, finish)


def _pcall(body, *, name, grid, in_specs, out_specs, out_shape, scratch_shapes=(), sem, jobs=()):
    in_specs, out_specs, out_shape = list(in_specs), list(out_specs), list(out_shape)
    scratch = list(scratch_shapes)
    n_in, n_out, n_scr = len(in_specs), len(out_shape), len(scratch)
    if not jobs:
        call = pl.pallas_call(body, name=name, grid=grid, in_specs=in_specs, out_specs=out_specs, out_shape=out_shape,
                              scratch_shapes=scratch, compiler_params=_cp(sem))
        return lambda *args: (call(*args), [])
    jin = sum(len(j.ins) for j in jobs)
    jout = sum(len(j.out_shapes) for j in jobs)
    aliases, pi, po = {}, n_in, n_out
    for j in jobs:
        for ia, oa in j.aliases.items():
            aliases[pi + ia] = po + oa
        pi, po = pi + len(j.ins), po + len(j.out_shapes)

    def wrapped(*refs):
        ins = refs[:n_in]
        jins = refs[n_in:n_in + jin]
        outs = refs[n_in + jin:n_in + jin + n_out]
        jouts = refs[n_in + jin + n_out:n_in + jin + n_out + jout]
        scr = refs[n_in + jin + n_out + jout:n_in + jin + n_out + jout + n_scr]
        sems = refs[n_in + jin + n_out + jout + n_scr:]
        first, last = None, None
        for d, g in enumerate(grid):
            f, l = pl.program_id(d) == 0, pl.program_id(d) == g - 1
            first = f if first is None else jnp.logical_and(first, f)
            last = l if last is None else jnp.logical_and(last, l)

        def each(what):
            pi, po = 0, 0
            for q, j in enumerate(jobs):
                getattr(j, what)(jins[pi:pi + len(j.ins)], jouts[po:po + len(j.out_shapes)], sems[2 * q], sems[2 * q + 1])
                pi, po = pi + len(j.ins), po + len(j.out_shapes)

        @pl.when(first)
        def _():
            each("start")

        body(*ins, *outs, *scr)

        @pl.when(last)
        def _():
            each("finish")

    call = pl.pallas_call(
        wrapped, name=name, grid=grid,
        in_specs=in_specs + [ANY] * jin, out_specs=out_specs + [ANY] * jout,
        out_shape=out_shape + [s for j in jobs for s in j.out_shapes],
        input_output_aliases=aliases,
        scratch_shapes=scratch + [pltpu.SemaphoreType.DMA((j.n_sems,)) for j in jobs for _ in range(2)],
        compiler_params=_cp(("arbitrary",) * len(grid)))

    def run(*args):
        res = call(*args, *[a for j in jobs for a in j.ins])
        main, rest, per_job = list(res[:n_out]), list(res[n_out:]), []
        for j in jobs:
            per_job.append(rest[:len(j.out_shapes)])
            rest = rest[len(j.out_shapes):]
        return main, per_job
    return run


def _comm_only(name, jobs):
    jin = sum(len(j.ins) for j in jobs)
    jout = sum(len(j.out_shapes) for j in jobs)
    aliases, pi, po = {}, 0, 0
    for j in jobs:
        for ia, oa in j.aliases.items():
            aliases[pi + ia] = po + oa
        pi, po = pi + len(j.ins), po + len(j.out_shapes)

    def body(*refs):
        jins, jouts, sems = refs[:jin], refs[jin:jin + jout], refs[jin + jout:]
        for what in ("start", "finish"):
            pi, po = 0, 0
            for q, j in enumerate(jobs):
                getattr(j, what)(jins[pi:pi + len(j.ins)], jouts[po:po + len(j.out_shapes)], sems[2 * q], sems[2 * q + 1])
                pi, po = pi + len(j.ins), po + len(j.out_shapes)

    res = pl.pallas_call(
        body, name=name, in_specs=[ANY] * jin, out_specs=[ANY] * jout,
        out_shape=[s for j in jobs for s in j.out_shapes], input_output_aliases=aliases,
        scratch_shapes=[pltpu.SemaphoreType.DMA((j.n_sems,)) for j in jobs for _ in range(2)],
    )(*[a for j in jobs for a in j.ins])
    rest, per_job = list(res), []
    for j in jobs:
        per_job.append(rest[:len(j.out_shapes)])
        rest = rest[len(j.out_shapes):]
    return per_job


def _small_allreduce(name, v):
    m_per, n = v.shape

    def body(x_ref, sum_ref, all_ref, send_sems, recv_sems, local_sem):
        x, y, c, chips = _place()
        me, sibling = (x, y, c), (x, y, 1 - c)

        def rows(px, py, pc):
            return all_ref.at[pl.ds((4 * px + 2 * py + pc) * m_per, m_per), :]

        def copy(kk, block, to, src=None):
            return pltpu.make_async_remote_copy(
                src_ref=rows(*block) if src is None else src, dst_ref=rows(*block),
                send_sem=send_sems.at[kk], recv_sem=recv_sems.at[kk], device_id=to, device_id_type=MESH)

        mine = pltpu.make_async_copy(x_ref, rows(*me), local_sem)
        mine.start()
        first = [copy(0, me, sibling, src=x_ref)]
        first += [copy(1 + j, me, (*chip, c), src=x_ref) for j, chip in enumerate(chips)]
        for cp in first:
            cp.start()
        passed = [copy(4 + j, (*chip, c), sibling) for j, chip in enumerate(chips)]
        for j, chip in enumerate(chips):
            copy(1 + j, (*chip, c), me).wait_recv()
            passed[j].start()
        copy(0, sibling, me).wait_recv()
        for j, chip in enumerate(chips):
            copy(4 + j, (*chip, 1 - c), me).wait_recv()
        for cp in first + passed:
            cp.wait_send()
        mine.wait()
        acc = all_ref[pl.ds(0, m_per), :]
        for d in range(1, 8):
            acc = acc + all_ref[pl.ds(d * m_per, m_per), :]
        sum_ref[...] = acc

    vm = pl.BlockSpec(memory_space=pltpu.VMEM)
    return pl.pallas_call(
        body, name=name, in_specs=[vm], out_specs=[vm, vm],
        out_shape=[_sds((m_per, n), F32), _sds((8 * m_per, n), F32)],
        scratch_shapes=[pltpu.SemaphoreType.DMA((7,)), pltpu.SemaphoreType.DMA((7,)), pltpu.SemaphoreType.DMA],
    )(v)[0]


def _in_layout(D, W6, nhb, Ls):
    nmain = W6 + 2 * D
    lay = []
    for k in range(N_CHIPS):
        g0, g1 = k * Ls, (k + 1) * Ls
        pieces = []
        a, b = max(g0, 0), min(g1, W6)
        if a < b:
            pieces.append((a - g0, b - g0, a))
        a, b = max(g0, W6 + nhb), min(g1, W6 + nhb + 2 * D)
        if a < b:
            pieces.append((a - g0, b - g0, a - nhb))
        a, b = max(g0, W6), min(g1, W6 + nhb)
        fpiece = (a - g0, b - g0, a - W6) if a < b else None
        assert fpiece is None or (b - a) == nhb
        main0 = min(p[2] for p in pieces)
        main1 = max(p[2] + p[1] - p[0] for p in pieces)
        lay.append(dict(pieces=pieces, f=fpiece, s=main0 // LANE, e=-(-main1 // LANE), main1=main1))
    assert sum(1 for l in lay if l["f"] is not None) == 1
    nbw = max(l["e"] - l["s"] + (1 if l["f"] else 0) for l in lay)
    for k in range(1, N_CHIPS):
        assert lay[k]["s"] >= lay[k - 1]["e"] - 1 and lay[k]["s"] > lay[k - 1]["s"]
    return lay, nbw, nmain


def _to_window(w, lay_k, nbw):
    D = w.shape[0]
    items = [(c0 - lay_k["s"] * LANE, l0, l1) for (l0, l1, c0) in lay_k["pieces"]]
    if lay_k["f"]:
        l0, l1, off = lay_k["f"]
        items.append(((lay_k["e"] - lay_k["s"]) * LANE + off, l0, l1))
    items.sort()
    cols, pos = [], 0
    for w0, l0, l1 in items:
        if w0 > pos:
            cols.append(jnp.zeros((D, w0 - pos), w.dtype))
        cols.append(w[:, l0:l1])
        pos = w0 + (l1 - l0)
    if pos < nbw * LANE:
        cols.append(jnp.zeros((D, nbw * LANE - pos), w.dtype))
    return jnp.concatenate(cols, axis=1)


def _from_window(win, lay_k):
    items = [(l0, c0 - lay_k["s"] * LANE, l1 - l0) for (l0, l1, c0) in lay_k["pieces"]]
    if lay_k["f"]:
        l0, l1, off = lay_k["f"]
        items.append((l0, (lay_k["e"] - lay_k["s"]) * LANE + off, l1 - l0))
    items.sort()
    return jnp.concatenate([win[:, w0:w0 + n] for (_, w0, n) in items], axis=1)


def _assemble_in(name, wins, lay, nbw, nmain):
    _, D, _ = wins.shape
    ncb = nmain // LANE + 1
    k1 = np.zeros(ncb, np.int32)
    i1 = np.zeros(ncb, np.int32)
    k2 = np.zeros(ncb, np.int32)
    i2 = np.zeros(ncb, np.int32)
    fl = np.zeros(ncb, np.int32)
    for b in range(ncb - 1):
        k = max(kk for kk in range(N_CHIPS) if lay[kk]["s"] <= b)
        k1[b], i1[b] = k, b - lay[k]["s"]
        if k >= 1 and b == lay[k]["s"] and lay[k - 1]["main1"] > b * LANE:
            k2[b], i2[b], fl[b] = k - 1, b - lay[k - 1]["s"], 1
    kf = [kk for kk in range(N_CHIPS) if lay[kk]["f"]][0]
    k1[ncb - 1], i1[ncb - 1] = kf, lay[kf]["e"] - lay[kf]["s"]

    def body(k1_ref, i1_ref, k2_ref, i2_ref, fl_ref, a_ref, b_ref, o_ref):
        b = pl.program_id(0)
        add = jnp.where(fl_ref[b] == 1, b_ref[...], jnp.zeros_like(b_ref))
        o_ref[...] = a_ref[...] + add

    gs = pltpu.PrefetchScalarGridSpec(
        num_scalar_prefetch=5, grid=(ncb,),
        in_specs=[pl.BlockSpec((None, D, LANE), lambda b, k1r, i1r, k2r, i2r, flr: (k1r[b], 0, i1r[b])),
                  pl.BlockSpec((None, D, LANE), lambda b, k1r, i1r, k2r, i2r, flr: (k2r[b], 0, i2r[b]))],
        out_specs=pl.BlockSpec((D, LANE), lambda b, k1r, i1r, k2r, i2r, flr: (0, b)))
    return pl.pallas_call(body, name=name, grid_spec=gs, out_shape=_sds((D, ncb * LANE), BF16),
                          compiler_params=_cp(("parallel",)))(
        jnp.asarray(k1), jnp.asarray(i1), jnp.asarray(k2), jnp.asarray(i2), jnp.asarray(fl), wins, wins)


def _hgroup(nh):
    return _pick(nh, (4, 2, 1))


def _a_specs_q(nh, G):
    ngrp = nh // G
    blk = (GROUP, G * HEAD_DIM)
    q = pl.BlockSpec(blk, lambda i, hg: (i, hg))
    ks = [pl.BlockSpec(blk, functools.partial(
        lambda i, hg, j: (jnp.maximum(i - (WIN_BLOCKS - 1) + j, 0), ngrp + hg), j=j)) for j in range(WIN_BLOCKS)]
    vs = [pl.BlockSpec(blk, functools.partial(
        lambda i, hg, j: (jnp.maximum(i - (WIN_BLOCKS - 1) + j, 0), 2 * ngrp + hg), j=j)) for j in range(WIN_BLOCKS)]
    return q, ks, vs


def _a_logits(q, ks, bias, i, scale):
    parts = [lax.dot_general(q, k, NT, preferred_element_type=F32) for k in ks]
    s = jnp.concatenate(parts, axis=1) * scale + bias
    col = lax.broadcasted_iota(jnp.int32, s.shape, 1)
    return jnp.where(col >= (WIN_BLOCKS - 1 - i) * GROUP, s, NEG_INF)


def _attn_a_fwd(name, qkv, bias2, nh, jobs=()):
    T = qkv.shape[0]
    ng = T // GROUP
    G = _pick(nh, (8, 4, 2, 1))
    scale = HEAD_DIM ** -0.5

    def body(q_ref, *refs):
        k_refs = refs[:WIN_BLOCKS]
        v_refs = refs[WIN_BLOCKS:2 * WIN_BLOCKS]
        bias_ref, o_ref, lse_ref = refs[2 * WIN_BLOCKS:]
        i, hg = pl.program_id(0), pl.program_id(1)

        @pl.when(hg == 0)
        def _():
            lse_ref[...] = jnp.zeros_like(lse_ref)

        for g in range(G):
            h = hg * G + g
            sl = slice(g * HEAD_DIM, (g + 1) * HEAD_DIM)
            s = _a_logits(q_ref[:, sl], [kr[:, sl] for kr in k_refs], bias_ref[h], i, scale)
            m = jnp.max(s, axis=1, keepdims=True)
            p = jnp.exp(s - m)
            l = jnp.sum(p, axis=1, keepdims=True)
            pb = (p / l).astype(BF16)
            o = jnp.zeros((GROUP, HEAD_DIM), F32)
            for j in range(WIN_BLOCKS):
                o = o + jnp.dot(pb[:, j * GROUP:(j + 1) * GROUP], v_refs[j][:, sl], preferred_element_type=F32)
            o_ref[:, sl] = o.astype(BF16)
            _put_col(lse_ref, h, m + jnp.log(l))

    q_spec, k_specs, v_specs = _a_specs_q(nh, G)
    stat = pl.BlockSpec((GROUP, LANE), lambda i, hg: (i, 0))
    return _pcall(
        body, name=name, grid=(ng, nh // G),
        in_specs=[q_spec] + k_specs + v_specs + [pl.BlockSpec((nh, GROUP, WIN), lambda i, hg: (0, 0, 0))],
        out_specs=[pl.BlockSpec((GROUP, G * HEAD_DIM), lambda i, hg: (i, hg)), stat],
        out_shape=[_sds((T, nh * HEAD_DIM), BF16), _sds((T, LANE), F32)],
        sem=("parallel", "arbitrary"), jobs=jobs,
    )(qkv, *([qkv] * (2 * WIN_BLOCKS)), bias2)


def _attn_a_dq(name, qkv, do, lse, bias2, nh, jobs=()):
    T = qkv.shape[0]
    ng = T // GROUP
    G = _pick(nh, (8, 4, 2, 1))
    scale = HEAD_DIM ** -0.5

    def body(q_ref, *refs):
        k_refs = refs[:WIN_BLOCKS]
        v_refs = refs[WIN_BLOCKS:2 * WIN_BLOCKS]
        do_ref, lse_ref, bias_ref, dq_ref, delta_ref, db_ref = refs[2 * WIN_BLOCKS:]
        i, hg = pl.program_id(0), pl.program_id(1)

        @pl.when(hg == 0)
        def _():
            delta_ref[...] = jnp.zeros_like(delta_ref)

        @pl.when(i == 0)
        def _():
            for g in range(G):
                db_ref[hg * G + g] = jnp.zeros((GROUP, WIN), F32)

        for g in range(G):
            h = hg * G + g
            sl = slice(g * HEAD_DIM, (g + 1) * HEAD_DIM)
            ks = [kr[:, sl] for kr in k_refs]
            s = _a_logits(q_ref[:, sl], ks, bias_ref[h], i, scale)
            p = jnp.exp(s - _col_of(lse_ref[...], h))
            dov = do_ref[:, sl]
            dp = jnp.concatenate([lax.dot_general(dov, vr[:, sl], NT, preferred_element_type=F32) for vr in v_refs],
                                 axis=1)
            delta = jnp.sum(p * dp, axis=1, keepdims=True)
            ds = p * (dp - delta)
            db_ref[h] += ds
            dsb = ds.astype(BF16)
            dq = jnp.zeros((GROUP, HEAD_DIM), F32)
            for j in range(WIN_BLOCKS):
                dq = dq + jnp.dot(dsb[:, j * GROUP:(j + 1) * GROUP], ks[j], preferred_element_type=F32)
            dq_ref[:, sl] = (dq * scale).astype(BF16)
            _put_col(delta_ref, h, delta)

    q_spec, k_specs, v_specs = _a_specs_q(nh, G)
    blk = pl.BlockSpec((GROUP, G * HEAD_DIM), lambda i, hg: (i, hg))
    stat = pl.BlockSpec((GROUP, LANE), lambda i, hg: (i, 0))
    full_b = pl.BlockSpec((nh, GROUP, WIN), lambda i, hg: (0, 0, 0))
    return _pcall(
        body, name=name, grid=(ng, nh // G),
        in_specs=[q_spec] + k_specs + v_specs + [blk, stat, full_b],
        out_specs=[blk, stat, full_b],
        out_shape=[_sds((T, nh * HEAD_DIM), BF16), _sds((T, LANE), F32), _sds((nh, GROUP, WIN), F32)],
        sem=("arbitrary", "arbitrary"), jobs=jobs,
    )(qkv, *([qkv] * (2 * WIN_BLOCKS)), do, lse, bias2)


def _attn_a_dkv(name, qkv, do, lse, delta, bias2, nh, jobs=()):
    T = qkv.shape[0]
    ng = T // GROUP
    G = _pick(nh, (8, 4, 2, 1))
    ngrp = nh // G
    scale = HEAD_DIM ** -0.5
    nj = WIN_BLOCKS

    def body(k_ref, v_ref, *refs):
        q_refs = refs[:nj]
        do_refs = refs[nj:2 * nj]
        lse_refs = refs[2 * nj:3 * nj]
        dl_refs = refs[3 * nj:4 * nj]
        bias_ref, dk_ref, dv_ref = refs[4 * nj:]
        r, hg = pl.program_id(0), pl.program_id(1)
        for g in range(G):
            h = hg * G + g
            sl = slice(g * HEAD_DIM, (g + 1) * HEAD_DIM)
            kv, vv = k_ref[:, sl], v_ref[:, sl]
            bias = bias_ref[h]
            dk = jnp.zeros((GROUP, HEAD_DIM), F32)
            dv = jnp.zeros((GROUP, HEAD_DIM), F32)
            for j in range(nj):
                qv, dov = q_refs[j][:, sl], do_refs[j][:, sl]
                c0 = (nj - 1 - j) * GROUP
                s = lax.dot_general(qv, kv, NT, preferred_element_type=F32) * scale + bias[:, c0:c0 + GROUP]
                p = jnp.exp(s - _col_of(lse_refs[j][...], h))
                p = jnp.where(r + j <= ng - 1, p, 0.0)
                dp = lax.dot_general(dov, vv, NT, preferred_element_type=F32)
                ds = p * (dp - _col_of(dl_refs[j][...], h))
                dv = dv + lax.dot_general(p.astype(BF16), dov, TN, preferred_element_type=F32)
                dk = dk + lax.dot_general(ds.astype(BF16), qv, TN, preferred_element_type=F32)
            dk_ref[:, sl] = (dk * scale).astype(BF16)
            dv_ref[:, sl] = dv.astype(BF16)

    def qmap(j):
        return functools.partial(lambda r, hg, j: (jnp.minimum(r + j, ng - 1), hg), j=j)

    def smap(j):
        return functools.partial(lambda r, hg, j: (jnp.minimum(r + j, ng - 1), 0), j=j)

    blk = (GROUP, G * HEAD_DIM)
    in_specs = ([pl.BlockSpec(blk, lambda r, hg: (r, ngrp + hg)), pl.BlockSpec(blk, lambda r, hg: (r, 2 * ngrp + hg))]
                + [pl.BlockSpec(blk, qmap(j)) for j in range(nj)]
                + [pl.BlockSpec(blk, qmap(j)) for j in range(nj)]
                + [pl.BlockSpec((GROUP, LANE), smap(j)) for j in range(nj)]
                + [pl.BlockSpec((GROUP, LANE), smap(j)) for j in range(nj)]
                + [pl.BlockSpec((nh, GROUP, WIN), lambda r, hg: (0, 0, 0))])
    out = pl.BlockSpec(blk, lambda r, hg: (r, hg))
    return _pcall(
        body, name=name, grid=(ng, ngrp), in_specs=in_specs, out_specs=[out, out],
        out_shape=[_sds((T, nh * HEAD_DIM), BF16)] * 2,
        sem=("parallel", "parallel"), jobs=jobs,
    )(qkv, qkv, *([qkv] * nj), *([do] * nj), *([lse] * nj), *([delta] * nj), bias2)


def _fox_prep(name, f, b_f):
    T = f.shape[0]
    tb = _pick(T, (256, 128))

    def body(f_ref, b_ref, cum_ref, cumt_ref, carry_ref):
        @pl.when(pl.program_id(0) == 0)
        def _():
            carry_ref[...] = jnp.zeros_like(carry_ref)

        z = f_ref[...] + b_ref[...]
        logf = jnp.minimum(z, 0.0) - jnp.log(1.0 + jnp.exp(-jnp.abs(z)))
        row = lax.broadcasted_iota(jnp.int32, (tb, tb), 0)
        col = lax.broadcasted_iota(jnp.int32, (tb, tb), 1)
        tri = (row >= col).astype(BF16)
        acc = jnp.zeros((tb, LANE), F32)
        for piece in _split3(logf):
            acc = acc + jnp.dot(tri, piece, preferred_element_type=F32)
        cum = acc + carry_ref[...]
        cum_ref[...] = cum
        cumt_ref[...] = cum.T
        carry_ref[...] = cum_ref[pl.ds(tb - 1, 1), :]

    return pl.pallas_call(
        body, name=name, grid=(T // tb,),
        in_specs=[pl.BlockSpec((tb, LANE), lambda i: (i, 0)), pl.BlockSpec((1, LANE), lambda i: (0, 0))],
        out_specs=[pl.BlockSpec((tb, LANE), lambda i: (i, 0)), pl.BlockSpec((LANE, tb), lambda i: (0, i))],
        out_shape=[_sds((T, LANE), F32), _sds((LANE, T), F32)],
        scratch_shapes=[pltpu.VMEM((1, LANE), F32)],
        compiler_params=_cp(("arbitrary",)),
    )(f, b_f)


def _fox_blk(T):
    return _pick(T, (256, 128))


def _fox_group(nh):
    return _hgroup(nh)


def _fox_allowed(i, j, tq, tk):
    diff = lax.broadcasted_iota(jnp.int32, (tq, tk), 1) - lax.broadcasted_iota(jnp.int32, (tq, tk), 0)
    return diff <= (i - j) * tq


def _fox_fwd(name, qkv, cum, cumt, nh, jobs=()):
    T = qkv.shape[0]
    tq = tk = _fox_blk(T)
    G = _pick(nh, (8, 4, 2, 1))
    ngrp = nh // G
    scale = HEAD_DIM ** -0.5

    def body(q_ref, k_ref, v_ref, cum_ref, cumt_ref, o_ref, lse_ref):
        i, hg = pl.program_id(0), pl.program_id(1)

        @pl.when(hg == 0)
        def _():
            lse_ref[...] = jnp.zeros_like(lse_ref)

        sls = [slice(g * HEAD_DIM, (g + 1) * HEAD_DIM) for g in range(G)]
        qs = [q_ref[:, sl] for sl in sls]
        cqs = [_col_of(cum_ref[...], hg * G + g) for g in range(G)]

        def step(j, carry, diagonal=False):
            k0 = pl.multiple_of(j * tk, tk)
            out = []
            for g in range(G):
                m, l, acc = carry[g]
                kj = k_ref[pl.ds(k0, tk), sls[g]]
                vj = v_ref[pl.ds(k0, tk), sls[g]]
                ck = cumt_ref[pl.ds(hg * G + g, 1), pl.ds(k0, tk)]
                s = lax.dot_general(qs[g], kj, NT, preferred_element_type=F32) * scale + (cqs[g] - ck)
                if diagonal:
                    s = jnp.where(_fox_allowed(i, j, tq, tk), s, NEG_INF)
                m_new = jnp.maximum(m, jnp.max(s, axis=1, keepdims=True))
                alpha = jnp.exp(m - m_new)
                p = jnp.exp(s - m_new)
                l = alpha * l + jnp.sum(p, axis=1, keepdims=True)
                acc = alpha * acc + jnp.dot(p.astype(BF16), vj, preferred_element_type=F32)
                out.append((m_new, l, acc))
            return tuple(out)

        one = (jnp.full((tq, 1), NEG_INF, F32), jnp.zeros((tq, 1), F32), jnp.zeros((tq, HEAD_DIM), F32))
        res = step(i, lax.fori_loop(0, i, step, tuple(one for _ in range(G))), diagonal=True)
        for g in range(G):
            m, l, acc = res[g]
            o_ref[:, sls[g]] = (acc / l).astype(BF16)
            _put_col(lse_ref, hg * G + g, m + jnp.log(l))

    GW = G * HEAD_DIM
    return _pcall(
        body, name=name, grid=(T // tq, ngrp),
        in_specs=[pl.BlockSpec((tq, GW), lambda i, hg: (i, 3 * ngrp + hg)),
                  pl.BlockSpec((T, GW), lambda i, hg: (0, 4 * ngrp + hg)),
                  pl.BlockSpec((T, GW), lambda i, hg: (0, 5 * ngrp + hg)),
                  pl.BlockSpec((tq, LANE), lambda i, hg: (i, 0)),
                  pl.BlockSpec((LANE, T), lambda i, hg: (0, 0))],
        out_specs=[pl.BlockSpec((tq, GW), lambda i, hg: (i, hg)), pl.BlockSpec((tq, LANE), lambda i, hg: (i, 0))],
        out_shape=[_sds((T, nh * HEAD_DIM), BF16), _sds((T, LANE), F32)],
        sem=("parallel", "arbitrary"), jobs=jobs,
    )(qkv, qkv, qkv, cum, cumt)


def _fox_dq(name, qkv, do, lse, cum, cumt, nh, jobs=()):
    T = qkv.shape[0]
    tq = tk = _fox_blk(T)
    G = _pick(nh, (8, 4, 2, 1))
    ngrp = nh // G
    GW = G * HEAD_DIM
    scale = HEAD_DIM ** -0.5

    def body(q_ref, k_ref, v_ref, do_ref, lse_ref, cum_ref, cumt_ref, dq_ref, delta_ref):
        i, hg = pl.program_id(0), pl.program_id(1)

        @pl.when(hg == 0)
        def _():
            delta_ref[...] = jnp.zeros_like(delta_ref)

        sls = [slice(g * HEAD_DIM, (g + 1) * HEAD_DIM) for g in range(G)]
        qs = [q_ref[:, sl] for sl in sls]
        dos = [do_ref[:, sl] for sl in sls]
        cqs = [_col_of(cum_ref[...], hg * G + g) for g in range(G)]
        lses = [_col_of(lse_ref[...], hg * G + g) for g in range(G)]

        def p_dp(j, g, ok):
            k0 = pl.multiple_of(j * tk, tk)
            kj = k_ref[pl.ds(k0, tk), sls[g]]
            vj = v_ref[pl.ds(k0, tk), sls[g]]
            ck = cumt_ref[pl.ds(hg * G + g, 1), pl.ds(k0, tk)]
            s = lax.dot_general(qs[g], kj, NT, preferred_element_type=F32) * scale + (cqs[g] - ck)
            if ok is not None:
                s = jnp.where(ok, s, NEG_INF)
            p = jnp.exp(s - lses[g])
            return p, lax.dot_general(dos[g], vj, NT, preferred_element_type=F32), kj

        def sweep_delta(j, deltas, diagonal=False):
            ok = _fox_allowed(i, j, tq, tk) if diagonal else None
            out = []
            for g in range(G):
                p, dp, _ = p_dp(j, g, ok)
                out.append(deltas[g] + jnp.sum(p * dp, axis=1, keepdims=True))
            return tuple(out)

        deltas = lax.fori_loop(0, i, sweep_delta, tuple(jnp.zeros((tq, 1), F32) for _ in range(G)))
        deltas = sweep_delta(i, deltas, diagonal=True)

        def sweep_dq(j, dqs, diagonal=False):
            ok = _fox_allowed(i, j, tq, tk) if diagonal else None
            out = []
            for g in range(G):
                p, dp, kj = p_dp(j, g, ok)
                ds = p * (dp - deltas[g])
                out.append(dqs[g] + jnp.dot(ds.astype(BF16), kj, preferred_element_type=F32))
            return tuple(out)

        dqs = lax.fori_loop(0, i, sweep_dq, tuple(jnp.zeros((tq, HEAD_DIM), F32) for _ in range(G)))
        dqs = sweep_dq(i, dqs, diagonal=True)
        for g in range(G):
            dq_ref[:, sls[g]] = (dqs[g] * scale).astype(BF16)
            _put_col(delta_ref, hg * G + g, deltas[g])

    blk = pl.BlockSpec((tq, GW), lambda i, hg: (i, hg))
    stat = pl.BlockSpec((tq, LANE), lambda i, hg: (i, 0))
    return _pcall(
        body, name=name, grid=(T // tq, ngrp),
        in_specs=[pl.BlockSpec((tq, GW), lambda i, hg: (i, 3 * ngrp + hg)),
                  pl.BlockSpec((T, GW), lambda i, hg: (0, 4 * ngrp + hg)),
                  pl.BlockSpec((T, GW), lambda i, hg: (0, 5 * ngrp + hg)),
                  blk, stat, stat, pl.BlockSpec((LANE, T), lambda i, hg: (0, 0))],
        out_specs=[blk, stat],
        out_shape=[_sds((T, nh * HEAD_DIM), BF16), _sds((T, LANE), F32)],
        sem=("parallel", "arbitrary"), jobs=jobs,
    )(qkv, qkv, qkv, do, lse, cum, cumt)


def _fox_dkv(name, qkv, do, lse, delta, cum, cumt, nh, jobs=()):
    T = qkv.shape[0]
    tq = tk = _fox_blk(T)
    nq = T // tq
    G = _fox_group(nh)
    ngrp = nh // G
    GW = G * HEAD_DIM
    scale = HEAD_DIM ** -0.5

    def body(k_ref, v_ref, q_ref, do_ref, lse_ref, dl_ref, cum_ref, cumt_ref, dk_ref, dv_ref, dc_ref):
        j, hg = pl.program_id(0), pl.program_id(1)

        @pl.when(hg == 0)
        def _():
            dc_ref[...] = jnp.zeros_like(dc_ref)

        sls = [slice(g * HEAD_DIM, (g + 1) * HEAD_DIM) for g in range(G)]
        kjs = [k_ref[:, sl] for sl in sls]
        vjs = [v_ref[:, sl] for sl in sls]
        k0 = pl.multiple_of(j * tk, tk)
        cks = [cumt_ref[pl.ds(hg * G + g, 1), pl.ds(k0, tk)] for g in range(G)]

        def step(i, carry, diagonal=False):
            q0 = pl.multiple_of(i * tq, tq)
            cum_i, lse_i, dl_i = cum_ref[pl.ds(q0, tq), :], lse_ref[pl.ds(q0, tq), :], dl_ref[pl.ds(q0, tq), :]
            out = []
            for g in range(G):
                dk, dv, dc = carry[g]
                h = hg * G + g
                qi = q_ref[pl.ds(q0, tq), sls[g]]
                doi = do_ref[pl.ds(q0, tq), sls[g]]
                s = lax.dot_general(qi, kjs[g], NT, preferred_element_type=F32) * scale + (_col_of(cum_i, h) - cks[g])
                if diagonal:
                    s = jnp.where(_fox_allowed(i, j, tq, tk), s, NEG_INF)
                p = jnp.exp(s - _col_of(lse_i, h))
                dp = lax.dot_general(doi, vjs[g], NT, preferred_element_type=F32)
                ds = p * (dp - _col_of(dl_i, h))
                dv = dv + lax.dot_general(p.astype(BF16), doi, TN, preferred_element_type=F32)
                dk = dk + lax.dot_general(ds.astype(BF16), qi, TN, preferred_element_type=F32)
                dc = dc - jnp.sum(ds, axis=0, keepdims=True)
                out.append((dk, dv, dc))
            return tuple(out)

        one = (jnp.zeros((tk, HEAD_DIM), F32), jnp.zeros((tk, HEAD_DIM), F32), jnp.zeros((1, tk), F32))
        res = lax.fori_loop(j + 1, nq, step, step(j, tuple(one for _ in range(G)), diagonal=True))
        sub = lax.broadcasted_iota(jnp.int32, (LANE, tk), 0)
        dc_all = dc_ref[...]
        for g in range(G):
            dk, dv, dc = res[g]
            dk_ref[:, sls[g]] = (dk * scale).astype(BF16)
            dv_ref[:, sls[g]] = dv.astype(BF16)
            dc_all = jnp.where(sub == hg * G + g, dc, dc_all)
        dc_ref[...] = dc_all

    whole = lambda c: pl.BlockSpec((T, GW), c)
    stat = pl.BlockSpec((T, LANE), lambda j, hg: (0, 0))
    out = pl.BlockSpec((tk, GW), lambda j, hg: (j, hg))
    return _pcall(
        body, name=name, grid=(T // tk, ngrp),
        in_specs=[pl.BlockSpec((tk, GW), lambda j, hg: (j, 4 * ngrp + hg)),
                  pl.BlockSpec((tk, GW), lambda j, hg: (j, 5 * ngrp + hg)),
                  whole(lambda j, hg: (0, 3 * ngrp + hg)), whole(lambda j, hg: (0, hg)),
                  stat, stat, stat, pl.BlockSpec((LANE, T), lambda j, hg: (0, 0))],
        out_specs=[out, out, pl.BlockSpec((LANE, tk), lambda j, hg: (0, j))],
        out_shape=[_sds((T, nh * HEAD_DIM), BF16)] * 2 + [_sds((LANE, T), F32)],
        sem=("parallel", "arbitrary"), jobs=jobs,
    )(qkv, qkv, qkv, do, lse, delta, cum, cumt)


def _fox_post(name, dcumt, f, b_f):
    T = f.shape[0]
    tb = _pick(T, (256, 128))
    nb = T // tb

    def body(dc_ref, f_ref, b_ref, df_ref, gb_ref, carry_ref):
        @pl.when(pl.program_id(0) == 0)
        def _():
            carry_ref[...] = jnp.zeros_like(carry_ref)
            gb_ref[...] = jnp.zeros_like(gb_ref)

        dc = dc_ref[...]
        row = lax.broadcasted_iota(jnp.int32, (tb, tb), 0)
        col = lax.broadcasted_iota(jnp.int32, (tb, tb), 1)
        tri = (row >= col).astype(BF16)
        acc = jnp.zeros((LANE, tb), F32)
        for piece in _split3(dc):
            acc = acc + jnp.dot(piece, tri, preferred_element_type=F32)
        dlogf = (acc + carry_ref[...]).T
        carry_ref[...] += jnp.sum(dc, axis=1, keepdims=True)
        z = f_ref[...] + b_ref[...]
        df = dlogf * _sigmoid(-z)
        df_ref[...] = df.astype(BF16)
        gb_ref[...] += jnp.sum(df, axis=0, keepdims=True)

    return pl.pallas_call(
        body, name=name, grid=(nb,),
        in_specs=[pl.BlockSpec((LANE, tb), lambda g: (0, nb - 1 - g)),
                  pl.BlockSpec((tb, LANE), lambda g: (nb - 1 - g, 0)),
                  pl.BlockSpec((1, LANE), lambda g: (0, 0))],
        out_specs=[pl.BlockSpec((tb, LANE), lambda g: (nb - 1 - g, 0)), pl.BlockSpec((1, LANE), lambda g: (0, 0))],
        out_shape=[_sds((T, LANE), BF16), _sds((1, LANE), F32)],
        scratch_shapes=[pltpu.VMEM((LANE, 1), F32)],
        compiler_params=_cp(("arbitrary",)),
    )(dcumt, f, b_f)


def _rel_tables(n_rel):
    max_rel = (n_rel - 1) // 2
    nj = GROUP + WIN - 1
    onehot = np.zeros((n_rel, nj), np.float32)
    for j in range(nj):
        dist = (WIN - 1) - j
        onehot[int(np.clip(dist, -max_rel, max_rel)) + max_rel, j] = 1.0
    a = np.arange(GROUP)[:, None]
    kb = np.arange(WIN)[None, :]
    lo = CHUNK * (a // CHUNK)
    inband = (kb >= lo) & (kb < lo + BAND)
    return onehot, inband


def _bias2_of(rel_bias, onehot, inband):
    bv = jnp.dot(rel_bias, jnp.asarray(onehot), precision=lax.Precision.HIGHEST)
    rows = [bv[:, GROUP - 1 - a:GROUP - 1 - a + WIN] for a in range(GROUP)]
    toe = jnp.stack(rows, axis=1)
    return jnp.where(jnp.asarray(inband)[None], toe, NEG_INF)


def _rel_grad_of(dbias2, onehot):
    nj = GROUP + WIN - 1
    dbv = sum(jnp.pad(dbias2[:, a, :], ((0, 0), (GROUP - 1 - a, nj - WIN - (GROUP - 1 - a)))) for a in range(GROUP))
    return jnp.dot(dbv, jnp.asarray(onehot).T, precision=lax.Precision.HIGHEST)


def kernel(x, g_mix, w_in, b_f, b_gate, rel_bias, w_branch_a, w_branch_b, w_out, g_ffn, w_gate_ffn, w_up_ffn, w_down_ffn, g_final, loss_target, m_g_mix, m_w_in, m_b_f, m_b_gate, m_rel_bias, m_w_branch_a, m_w_branch_b, m_w_out, m_g_ffn, m_w_gate_ffn, m_w_up_ffn, m_w_down_ffn, m_g_final, v_g_mix, v_w_in, v_b_f, v_b_gate, v_rel_bias, v_w_branch_a, v_w_branch_b, v_w_out, v_g_ffn, v_w_gate_ffn, v_w_up_ffn, v_w_down_ffn, v_g_final):
    T, D = x.shape[1], x.shape[2]
    Ls = w_in.shape[2]
    W = w_branch_a.shape[1]
    nh = W // HEAD_DIM
    nhb = b_f.shape[1]
    assert w_branch_b.shape[1] == W and nhb == nh and rel_bias.shape[1] == nh
    W6 = 6 * W
    Fl = w_gate_ffn.shape[2]
    Fp = -(-Fl // LANE) * LANE
    n_rel = rel_bias.shape[2]
    chip = 2 * lax.axis_index("x") + lax.axis_index("y")
    lay, nbw, nmain = _in_layout(D, W6, nhb, Ls)
    onehot, inband = _rel_tables(n_rel)

    xs, tgt = x[0], loss_target[0]

    win_f32 = lax.switch(chip, [functools.partial(_to_window, lay_k=lay[k], nbw=nbw) for k in range(N_CHIPS)], w_in[0])
    pad_c = lambda w: jnp.pad(w, ((0, 0), (0, Fp - Fl)))
    pad_r = lambda w: jnp.pad(w, ((0, Fp - Fl), (0, 0)))
    sh_in = _cast_bf16("cast_w_in", win_f32, chip)
    sh_a = _cast_bf16("cast_w_a", w_branch_a[0], chip)
    sh_b = _cast_bf16("cast_w_b", w_branch_b[0], chip)
    sh_o = _cast_bf16("cast_w_out", w_out[0], chip)
    sh_g = _cast_bf16("cast_w_gate", pad_c(w_gate_ffn[0]), chip)
    sh_u = _cast_bf16("cast_w_up", pad_c(w_up_ffn[0]), chip)
    sh_d = _cast_bf16("cast_w_down", pad_r(w_down_ffn[0]), chip)
    (wins,) = _allgather("ag_w_in", [sh_in])
    wc = _assemble_in("assemble_w_in", wins, lay, nbw, nmain)

    h1, r1 = _rms_fwd("rms1", xs, g_mix)
    qkv, ((wa_g, wb_g, wo_g),) = _mm_nn("proj_qkv", h1, wc, BF16, b_col0=0, n=W6, tm=1024,
                                        jobs=[_job_gather_ici([sh_a, sh_b, sh_o])])
    gates, ((wa_g, wb_g, wo_g), (wg_g,)) = _mm_nn(
        "proj_gates", h1, wc, BF16, b_col0=W6, n=2 * D, tm=1024,
        jobs=[_job_gather_d2d([wa_g, wb_g, wo_g]), _job_gather_ici([sh_g], part=(0, 2))])
    fl = _mm_nn("proj_f", h1, wc, F32, b_col0=nmain, n=LANE, tn=LANE)
    bias2 = _bias2_of(rel_bias[0], onehot, inband)
    bf_pad = jnp.pad(b_f, ((0, 0), (0, LANE - nhb)))
    (o_a, lse_a), ((wg_g,),) = _attn_a_fwd("attn_a_fwd", qkv, bias2, nh, jobs=[_job_gather_ici([wg_g], part=(1, 2))])
    cum, cumt = _fox_prep("fox_prep", fl, bf_pad)
    (o_b, lse_b), ((wu_g,), (wg_g,)) = _fox_fwd("fox_fwd", qkv, cum, cumt, nh,
                                                jobs=[_job_gather_ici([sh_u]), _job_gather_d2d([wg_g])])
    u_a = _mm_nn("branch_a", o_a, wa_g, BF16, tm=1024)
    u_b = _mm_nn("branch_b", o_b, wb_g, BF16, tm=1024)
    merged = _merge_fwd("merge", gates, u_a, u_b, b_gate)
    wo_full = wo_g.reshape(D, D)
    x1, ((wu_g,),) = _mm_nn("out_proj", merged, wo_full, F32, residual=xs, tm=1024, tn=_pick(D, (1024, 512, 256, 128)),
                            jobs=[_job_gather_d2d([wu_g])])
    h2, r2 = _rms_fwd("rms2", x1, g_ffn)

    tm_f = _pick(T, (1024, 512, 256, 128))
    tn_f = _pick(Fp, (1408, 1024, 512, 256, 128))
    tk_f = _pick(D, (1024, 512, 256, 128))
    nps_f = Fp // tn_f

    def swiglu_ep(accs, e_refs, o_refs):
        g, u = accs
        o_refs[0][...] = g.astype(BF16)
        o_refs[1][...] = u.astype(BF16)
        o_refs[2][...] = (g * _sigmoid(g) * u).astype(BF16)

    hid_spec = pl.BlockSpec((tm_f, tn_f), lambda i, j, k: (i, j))
    wcol_spec = pl.BlockSpec((None, tk_f, tn_f), lambda i, j, k: (j // nps_f, k, j % nps_f))
    (gate, up, hidden), ((wd_g,),) = _mm(
        "ffn_up", "nn", [h2], [pl.BlockSpec((tm_f, tk_f), lambda i, j, k: (i, k))], [wg_g, wu_g], [wcol_spec, wcol_spec],
        [(0, 0, 0), (0, 1, 1)], 2, (T // tm_f, N_CHIPS * Fp // tn_f, D // tk_f), tm_f, tn_f,
        [_sds((T, N_CHIPS * Fp), BF16)] * 3, [hid_spec] * 3, swiglu_ep, jobs=[_job_gather_ici([sh_d])])
    ((wd_g,),) = _comm_only("ag_w_down_d2d", [_job_gather_d2d([wd_g])])
    wd_full = wd_g.reshape(N_CHIPS * Fp, D)
    x2 = _mm_nn("ffn_down", hidden, wd_full, F32, residual=x1, tm=1024, tn=_pick(D, (1024, 512, 256, 128)),
                tk=_pick(N_CHIPS * Fp, (1408, 1024, 512, 256, 128)))

    dx2, dx2b, loss_part, gg_final = _final_loss_bwd("final_loss", x2, tgt, g_final.reshape(1, D))

    def swiglu_bwd_ep(accs, e_refs, o_refs):
        dh = accs[0]
        g = e_refs[0][...].astype(F32)
        u = e_refs[1][...].astype(F32)
        sg = _sigmoid(g)
        o_refs[0][...] = (dh * u * (sg * (1.0 + g * (1.0 - sg)))).astype(BF16)
        o_refs[1][...] = (dh * (g * sg)).astype(BF16)

    tk_b = _pick(D, (1024, 512, 256, 128))
    core = lax.axis_index("c")
    (dgate, dup), _ = _mm(
        "ffn_down_bwd", "nt", [dx2b], [pl.BlockSpec((tm_f, tk_b), lambda i, j, k: (i, k))],
        [wd_full], [pl.BlockSpec((tn_f, tk_b), lambda i, j, k: (j, k))], [(0, 0, 0)], 1,
        (T // tm_f, N_CHIPS * Fp // tn_f, D // tk_b), tm_f, tn_f,
        [_sds((T, N_CHIPS * Fp), BF16)] * 2, [hid_spec] * 2, swiglu_bwd_ep,
        extra=[gate, up], extra_specs=[hid_spec, hid_spec])
    dwd = _mm_tn("dw_down", hidden, dx2b, BF16, tm=_pick(N_CHIPS * Fp, (1408, 1024, 512, 256, 128)))
    dwd = dwd.reshape(N_CHIPS, Fp, D)
    dh2, ((sib_d,),) = _mm_nt("ffn_up_bwd", [dgate, dup], [wg_g, wu_g], F32, tm=1024, jobs=[_job_sibling([dwd])])
    tm_w = _pick(D, (2048, 1024, 512, 256, 128))
    dwg = _mm_tn("dw_gate", h2, dgate, BF16, slots=N_CHIPS, tm=tm_w)
    dwu = _mm_tn("dw_up", h2, dup, BF16, slots=N_CHIPS, tm=tm_w)
    dx1, dx1b, gg_ffn = _rms_bwd("rms2_bwd", [dh2], x1, r2, g_ffn, dx2, True)
    part_d = _add_bf16("rs_add_down", dwd, core, sib_d)

    dmerged, ((sib_g, sib_u),) = _mm_nt("out_proj_bwd", [dx1b], [wo_full], BF16, tm=1024,
                                        jobs=[_job_sibling([dwg, dwu])])
    dwo = _mm_tn("dw_out", merged, dx1b, BF16).reshape(N_CHIPS, D // N_CHIPS, D)
    du_a, du_b, dga, dgb, gbg_a, gbg_b = _merge_bwd("merge_bwd", dmerged, gates, u_a, u_b, b_gate)
    part_g = _add_bf16("rs_add_gate", dwg, core, sib_g)
    part_u = _add_bf16("rs_add_up", dwu, core, sib_u)
    do_a = _mm_nt("branch_a_bwd", [du_a], [wa_g], BF16, tm=1024)
    do_b = _mm_nt("branch_b_bwd", [du_b], [wb_g], BF16, tm=1024)
    dwa = _mm_tn("dw_a", o_a, du_a, BF16, slots=N_CHIPS)
    dwb = _mm_tn("dw_b", o_b, du_b, BF16, slots=N_CHIPS)

    (dq_a, delta_a, dbias2), ((got_d,), (sib_a, sib_b, sib_o)) = _attn_a_dq(
        "attn_a_dq", qkv, do_a, lse_a, bias2, nh,
        jobs=[_job_scatter([part_d], part=(0, 2)), _job_sibling([dwa, dwb, dwo])])
    part_a = _add_bf16("rs_add_a", dwa, core, sib_a)
    part_b = _add_bf16("rs_add_b", dwb, core, sib_b)
    part_o = _add_bf16("rs_add_out", dwo, core, sib_o)
    (dk_a, dv_a), ((got_d,), (got_g,)) = _attn_a_dkv(
        "attn_a_dkv", qkv, do_a, lse_a, delta_a, bias2, nh,
        jobs=[_job_scatter([part_d], part=(1, 2), into=[got_d]), _job_scatter([part_g], part=(0, 2))])
    full_d = _sum4("rs_sum_down", got_d, part_d, chip, core)
    (dq_b, delta_b), ((got_g,), (got_u,)) = _fox_dq(
        "fox_dq", qkv, do_b, lse_b, cum, cumt, nh,
        jobs=[_job_scatter([part_g], part=(1, 2), into=[got_g]), _job_scatter([part_u])])
    full_g = _sum4("rs_sum_gate", got_g, part_g, chip, core)
    full_u = _sum4("rs_sum_up", got_u, part_u, chip, core)
    (dk_b, dv_b, dcumt), ((got_a, got_b, got_o),) = _fox_dkv(
        "fox_dkv", qkv, do_b, lse_b, delta_b, cum, cumt, nh, jobs=[_job_scatter([part_a, part_b, part_o])])
    full_a = _sum4("rs_sum_a", got_a, part_a, chip, core)
    full_b = _sum4("rs_sum_b", got_b, part_b, chip, core)
    full_o = _sum4("rs_sum_out", got_o, part_o, chip, core)
    df, gbf = _fox_post("fox_post", dcumt, fl, bf_pad)

    dqkv = jnp.concatenate([dq_a, dk_a, dv_a, dq_b, dk_b, dv_b], axis=1)
    dgates = jnp.concatenate([dga, dgb], axis=1)
    dwc_q, ((g_d, g_g, g_u, g_a, g_b, g_o),) = _mm_tn(
        "dw_in_qkv", h1, dqkv, BF16, tn=_pick(W6, (2048, 1024, 512, 256, 128)),
        jobs=[_job_swap([full_d, full_g, full_u, full_a, full_b, full_o])])
    dwc_g = _mm_tn("dw_in_gates", h1, dgates, BF16, tn=_pick(2 * D, (2048, 1024, 512, 256, 128)))
    dwc_f = _mm_tn("dw_in_f", h1, df, BF16, tn=LANE)
    dwc = jnp.concatenate([dwc_q, dwc_g, dwc_f], axis=1)
    zeros_blk = jnp.zeros((D, LANE), BF16)
    win_parts = []
    for k in range(N_CHIPS):
        cols = [dwc[:, lay[k]["s"] * LANE:lay[k]["e"] * LANE]]
        nb = lay[k]["e"] - lay[k]["s"]
        if lay[k]["f"]:
            cols.append(dwc[:, nmain:nmain + LANE])
            nb += 1
        cols += [zeros_blk] * (nbw - nb)
        win_parts.append(jnp.concatenate(cols, axis=1) if len(cols) > 1 else cols[0])
    dwin = jnp.stack(win_parts, axis=0)
    big = {}

    def adamw(nm, w, g, m, v, jobs=()):
        (d, mn, vn, go), jouts = _adamw(f"adamw_{nm}", w[0], g, m[0], v[0], jobs=jobs)
        big[nm] = (go[None], d[None], mn[None], vn[None])
        return jouts

    ((sib_in,),) = adamw("w_gate_ffn", w_gate_ffn, g_g, m_w_gate_ffn, v_w_gate_ffn, jobs=[_job_sibling([dwin])])
    part_in = _add_bf16("rs_add_in", dwin, core, sib_in)
    dh, ((got_in,),) = _mm_nt("proj_qkv_bwd", [dqkv], [wc], F32, k0_list=[0], tk=_pick(W6, (1024, 512, 256, 128)), tm=1024,
                              jobs=[_job_scatter([part_in], part=(0, 5, 8))])
    dh, ((got_in,),) = _mm_nt("proj_gates_bwd", [dgates], [wc], F32, k0_list=[W6], tm=1024,
                              tk=_pick(math_gcd(W6, 2 * D), (1024, 512, 256, 128)), residual=dh,
                              jobs=[_job_scatter([part_in], part=(5, 8, 8), into=[got_in])])
    full_in = _sum4("rs_sum_in", got_in, part_in, chip, core)
    dh, ((g_win,),) = _mm_nt("proj_f_bwd", [df], [wc], F32, k0_list=[nmain], tk=LANE, residual=dh,
                             jobs=[_job_swap([full_in])])
    grad_x, gg_mix = _rms_bwd("rms1_bwd", [dh], xs, r1, g_mix, dx1, False)
    g_in = lax.switch(chip, [functools.partial(_from_window, lay_k=lay[k]) for k in range(N_CHIPS)], g_win)

    for nm, w, g, m, v in (("w_in", w_in, g_in, m_w_in, v_w_in), ("w_branch_a", w_branch_a, g_a, m_w_branch_a, v_w_branch_a),
                           ("w_branch_b", w_branch_b, g_b, m_w_branch_b, v_w_branch_b), ("w_out", w_out, g_o, m_w_out, v_w_out),
                           ("w_up_ffn", w_up_ffn, g_u, m_w_up_ffn, v_w_up_ffn),
                           ("w_down_ffn", w_down_ffn, g_d, m_w_down_ffn, v_w_down_ffn)):
        adamw(nm, w, g, m, v)

    g_rel = _rel_grad_of(dbias2, onehot)
    small = [("loss", loss_part[:, :1], None, None, None),
             ("g_mix", gg_mix, g_mix, m_g_mix, v_g_mix), ("b_f", gbf[:, :nhb], b_f, m_b_f, v_b_f),
             ("b_gate", jnp.concatenate([gbg_a, gbg_b], axis=1), b_gate, m_b_gate, v_b_gate),
             ("rel_bias", g_rel, rel_bias, m_rel_bias, v_rel_bias), ("g_ffn", gg_ffn, g_ffn, m_g_ffn, v_g_ffn),
             ("g_final", gg_final, g_final, m_g_final, v_g_final)]
    sizes = [int(np.prod(s[1].shape)) for s in small]
    total = sum(sizes)
    npad = -(-total // 1024) * 1024

    def pack(arrs):
        flat = jnp.concatenate([a.reshape(-1).astype(F32) for a in arrs])
        return jnp.pad(flat, (0, npad - total)).reshape(8, npad // 8)

    zero1 = jnp.zeros((1,), F32)
    g_all = _small_allreduce("small_allreduce", pack([s[1] for s in small]))
    w_s = pack([zero1 if s[2] is None else s[2] for s in small])
    m_s = pack([zero1 if s[3] is None else s[3] for s in small])
    v_s = pack([zero1 + 1.0 if s[4] is None else s[4] for s in small])
    (d_s, mn_s, vn_s, _), _ = _adamw("adamw_small", w_s, g_all, m_s, v_s)

    def unpack(packed):
        flat = packed.reshape(-1)
        out, pos = {}, 0
        for s, n in zip(small, sizes):
            if s[2] is not None:
                out[s[0]] = flat[pos:pos + n].reshape(s[2].shape)
            else:
                out[s[0]] = flat[pos:pos + n].reshape(())
            pos += n
        return out

    gs, ds, ms, vs = unpack(g_all), unpack(d_s), unpack(mn_s), unpack(vn_s)
    order = ["g_mix", "w_in", "b_f", "b_gate", "rel_bias", "w_branch_a", "w_branch_b", "w_out", "g_ffn",
             "w_gate_ffn", "w_up_ffn", "w_down_ffn", "g_final"]
    res = [[], [], [], []]
    for nm in order:
        four = big[nm] if nm in big else (gs[nm], ds[nm], ms[nm], vs[nm])
        for q in range(4):
            res[q].append(four[q])
    return (gs["loss"], grad_x[None], *res[0], *res[1], *res[2], *res[3])


def math_gcd(a, b):
    while b:
        a, b = b, a % b
    return a
```
